```python
import jax, jax.numpy as jnp
from jax import lax
import numpy as np


D_MODEL = 1024
BATCH = 8
SEQ = 8192
DEPTH = 2

N_MIXERS = 2
N_LAYERS_A = (DEPTH + 1) // 2
N_LAYERS_B = DEPTH // 2
MLA_HEADS = 8
QK_NOPE = 128
QK_ROPE = 64
V_DIM = 128
Q_LORA = 384
KV_LORA = 256
ROPE_THETA = 10000.0
SWA_HEADS = 16
SWA_KV_HEADS = 4
SWA_HEAD_DIM = 64
WINDOW = 128
D_FF = 4 * D_MODEL
BLOCK_Q = 128
EPS = 1e-6

kernel_name = 'hybrid_mla_swa_sink_alibi_adaln'


def rmsnorm(x, g):
    xf = x.astype(jnp.float32)
    y = xf * lax.rsqrt(jnp.mean(xf * xf, axis=-1, keepdims=True) + EPS)
    return (y * g.astype(jnp.float32)).astype(x.dtype)


def modulate(x, g, shift, scale):
    return rmsnorm(x, g) * (1.0 + scale[:, None, :]) + shift[:, None, :]


def rope(x, positions):
    half = QK_ROPE // 2
    inv_freq = ROPE_THETA ** (-jnp.arange(half, dtype=jnp.float32) / half)
    ang = positions.astype(jnp.float32)[..., None] * inv_freq
    shape = ang.shape[:2] + (1,) * (x.ndim - 3) + (half,)
    cos = jnp.cos(ang).reshape(shape)
    sin = jnp.sin(ang).reshape(shape)
    xf = x.astype(jnp.float32)
    x1, x2 = xf[..., :half], xf[..., half:]
    out = jnp.concatenate([x1 * cos - x2 * sin, x1 * sin + x2 * cos], axis=-1)
    return out.astype(x.dtype)


def alibi_slopes(n_heads):
    return jnp.asarray(2.0 ** (-8.0 * np.arange(1, n_heads + 1) / n_heads), dtype=jnp.float32)


def mla(h, positions, w_dq, g_q, w_uq, w_dkv, g_kv, w_ukv, w_o):
    B, S, _ = h.shape
    H = MLA_HEADS
    cq = rmsnorm(h @ w_dq, g_q)
    q = (cq @ w_uq).reshape(B, S, H, QK_NOPE + QK_ROPE)
    q_nope = q[..., :QK_NOPE]
    q_rope = rope(q[..., QK_NOPE:], positions)
    ckv_kr = h @ w_dkv
    ckv = rmsnorm(ckv_kr[..., :KV_LORA], g_kv)
    k_rope = rope(ckv_kr[..., KV_LORA:], positions)
    kv = (ckv @ w_ukv).reshape(B, S, H, QK_NOPE + V_DIM)
    k_nope, v = kv[..., :QK_NOPE], kv[..., QK_NOPE:]
    scale = (QK_NOPE + QK_ROPE) ** -0.5
    n_blk = S // BLOCK_Q
    qn_blk = q_nope.reshape(B, n_blk, BLOCK_Q, H, QK_NOPE).transpose(1, 0, 2, 3, 4)
    qr_blk = q_rope.reshape(B, n_blk, BLOCK_Q, H, QK_ROPE).transpose(1, 0, 2, 3, 4)
    key_idx = jnp.arange(S)

    def one_block(args):
        i, qn, qr = args
        s = (jnp.einsum('bqhd,bkhd->bhqk', qn, k_nope)
             + jnp.einsum('bqhr,bkr->bhqk', qr, k_rope)).astype(jnp.float32) * scale
        q_idx = i * BLOCK_Q + jnp.arange(BLOCK_Q)
        causal = key_idx[None, :] <= q_idx[:, None]
        s = jnp.where(causal[None, None], s, -jnp.inf)
        p = jax.nn.softmax(s, axis=-1).astype(v.dtype)
        return jnp.einsum('bhqk,bkhd->bqhd', p, v)

    o = lax.map(one_block, (jnp.arange(n_blk), qn_blk, qr_blk))
    o = o.transpose(1, 0, 2, 3, 4).reshape(B, S, H * V_DIM)
    return o @ w_o


def swa(h, w_qkv, b_qkv, sinks, w_o, b_o):
    B, S, _ = h.shape
    Hq, Hk, Dh, W = SWA_HEADS, SWA_KV_HEADS, SWA_HEAD_DIM, WINDOW
    G = Hq // Hk
    qkv = h @ w_qkv + b_qkv
    q = qkv[..., :Hq * Dh]
    k = qkv[..., Hq * Dh:(Hq + Hk) * Dh].reshape(B, S, Hk, Dh)
    v = qkv[..., (Hq + Hk) * Dh:].reshape(B, S, Hk, Dh)
    n_blk = S // W
    qb = q.reshape(B, n_blk, W, Hk, G, Dh)

    def band(t):
        tb = t.reshape(B, n_blk, W, Hk, Dh)
        prev = jnp.pad(tb[:, :-1], ((0, 0), (1, 0), (0, 0), (0, 0), (0, 0)))
        return jnp.concatenate([prev, tb], axis=2)

    kb, vb = band(k), band(v)
    s = jnp.einsum('bnqkgd,bnjkd->bnkgqj', qb, kb).astype(jnp.float32) * (Dh ** -0.5)
    dist = W + jnp.arange(W)[:, None] - jnp.arange(2 * W)[None, :]
    in_window = (dist >= 0) & (dist < W)
    real_key = (jnp.arange(n_blk)[:, None] > 0) | (jnp.arange(2 * W)[None, :] >= W)
    mask = in_window[None] & real_key[:, None, :]
    slopes = alibi_slopes(Hq).reshape(Hk, G)
    s = s - slopes[:, :, None, None] * dist.astype(jnp.float32)
    s = jnp.where(mask[None, :, None, None], s, -jnp.inf)
    sink = sinks.astype(jnp.float32).reshape(Hk, G)[:, :, None]
    m = jnp.maximum(s.max(axis=-1), sink)
    p = jnp.exp(s - m[..., None])
    denom = p.sum(axis=-1) + jnp.exp(sink - m)
    p = (p / denom[..., None]).astype(vb.dtype)
    o = jnp.einsum('bnkgqj,bnjkd->bnqkgd', p, vb).reshape(B, S, Hq * Dh)
    return o @ w_o + b_o


def _fwd_setup_inputs(seed: int = 0) -> dict:
    key = jax.random.key(seed)
    ks = jax.random.split(key, 24)
    f32 = jnp.float32

    def w(k, shape, fan_in, gain=1.0):
        return jax.random.normal(k, shape, f32) * (gain * fan_in ** -0.5)

    def g(k, shape):
        return 1.0 + 0.05 * jax.random.normal(k, shape, f32)

    A, Bn = N_LAYERS_A, N_LAYERS_B
    x = jax.random.normal(ks[0], (BATCH, SEQ, D_MODEL), f32)
    c = jax.random.normal(ks[1], (BATCH, D_MODEL), f32)
    positions = (jnp.arange(SEQ, dtype=jnp.int32)[None, :]
                 + jax.random.randint(ks[2], (BATCH, 1), 0, 1024, dtype=jnp.int32))
    return {
        'x': x,
        'c': c,
        'positions': positions,
        'w_ada': w(ks[3], (DEPTH, D_MODEL, 6 * D_MODEL), D_MODEL, 0.5),
        'b_ada': 0.02 * jax.random.normal(ks[4], (DEPTH, 6 * D_MODEL), f32),
        'g_mix': g(ks[5], (DEPTH, D_MODEL)),
        'g_mlp': g(ks[6], (DEPTH, D_MODEL)),
        'mla_w_dq': w(ks[7], (A, D_MODEL, Q_LORA), D_MODEL),
        'mla_g_q': g(ks[8], (A, Q_LORA)),
        'mla_w_uq': w(ks[9], (A, Q_LORA, MLA_HEADS * (QK_NOPE + QK_ROPE)), Q_LORA),
        'mla_w_dkv': w(ks[10], (A, D_MODEL, KV_LORA + QK_ROPE), D_MODEL),
        'mla_g_kv': g(ks[11], (A, KV_LORA)),
        'mla_w_ukv': w(ks[12], (A, KV_LORA, MLA_HEADS * (QK_NOPE + V_DIM)), KV_LORA),
        'mla_w_o': w(ks[13], (A, MLA_HEADS * V_DIM, D_MODEL), MLA_HEADS * V_DIM),
        'swa_w_qkv': w(ks[14], (Bn, D_MODEL, (SWA_HEADS + 2 * SWA_KV_HEADS) * SWA_HEAD_DIM), D_MODEL),
        'swa_b_qkv': 0.02 * jax.random.normal(ks[15], (Bn, (SWA_HEADS + 2 * SWA_KV_HEADS) * SWA_HEAD_DIM), f32),
        'swa_sinks': 0.5 * jax.random.normal(ks[16], (Bn, SWA_HEADS), f32),
        'swa_w_o': w(ks[17], (Bn, SWA_HEADS * SWA_HEAD_DIM, D_MODEL), SWA_HEADS * SWA_HEAD_DIM),
        'swa_b_o': 0.02 * jax.random.normal(ks[18], (Bn, D_MODEL), f32),
        'w_ff1': w(ks[19], (DEPTH, D_MODEL, D_FF), D_MODEL),
        'w_ff2': w(ks[20], (DEPTH, D_FF, D_MODEL), D_FF),
        'g_final': g(ks[21], (D_MODEL,)),
    }


def _fwd_reference(x, c, positions, w_ada, b_ada, g_mix, g_mlp,
              mla_w_dq, mla_g_q, mla_w_uq, mla_w_dkv, mla_g_kv, mla_w_ukv, mla_w_o,
              swa_w_qkv, swa_b_qkv, swa_sinks, swa_w_o, swa_b_o,
              w_ff1, w_ff2, g_final):
    cond = jax.nn.silu(c)
    for i in range(DEPTH):
        mod = cond @ w_ada[i] + b_ada[i]
        sh1, sc1, gt1, sh2, sc2, gt2 = jnp.split(mod, 6, axis=-1)
        h = modulate(x, g_mix[i], sh1, sc1)
        j = i // N_MIXERS
        if i % N_MIXERS == 0:
            y = mla(h, positions, mla_w_dq[j], mla_g_q[j], mla_w_uq[j], mla_w_dkv[j],
                    mla_g_kv[j], mla_w_ukv[j], mla_w_o[j])
        else:
            y = swa(h, swa_w_qkv[j], swa_b_qkv[j], swa_sinks[j], swa_w_o[j], swa_b_o[j])
        x = x + gt1[:, None, :] * y
        h = modulate(x, g_mlp[i], sh2, sc2)
        y = jnp.square(jax.nn.relu(h @ w_ff1[i])) @ w_ff2[i]
        x = x + gt2[:, None, :] * y
    return rmsnorm(x, g_final)


import jax as _jax
import jax.numpy as _jnp

TWIN_FORMAT = 'train_step'
FWD_PARAMS = ['x', 'c', 'positions', 'w_ada', 'b_ada', 'g_mix', 'g_mlp', 'mla_w_dq', 'mla_g_q', 'mla_w_uq', 'mla_w_dkv', 'mla_g_kv', 'mla_w_ukv', 'mla_w_o', 'swa_w_qkv', 'swa_b_qkv', 'swa_sinks', 'swa_w_o', 'swa_b_o', 'w_ff1', 'w_ff2', 'g_final']
TWIN_WEIGHTS = ['w_ada', 'b_ada', 'g_mix', 'g_mlp', 'mla_w_dq', 'mla_g_q', 'mla_w_uq', 'mla_w_dkv', 'mla_g_kv', 'mla_w_ukv', 'mla_w_o', 'swa_w_qkv', 'swa_b_qkv', 'swa_sinks', 'swa_w_o', 'swa_b_o', 'w_ff1', 'w_ff2', 'g_final']
TWIN_DIFF_INPUT = 'x'
TWIN_INPUTS = ['x', 'c', 'positions', 'w_ada', 'b_ada', 'g_mix', 'g_mlp', 'mla_w_dq', 'mla_g_q', 'mla_w_uq', 'mla_w_dkv', 'mla_g_kv', 'mla_w_ukv', 'mla_w_o', 'swa_w_qkv', 'swa_b_qkv', 'swa_sinks', 'swa_w_o', 'swa_b_o', 'w_ff1', 'w_ff2', 'g_final', 'loss_target', 'm_w_ada', 'm_b_ada', 'm_g_mix', 'm_g_mlp', 'm_mla_w_dq', 'm_mla_g_q', 'm_mla_w_uq', 'm_mla_w_dkv', 'm_mla_g_kv', 'm_mla_w_ukv', 'm_mla_w_o', 'm_swa_w_qkv', 'm_swa_b_qkv', 'm_swa_sinks', 'm_swa_w_o', 'm_swa_b_o', 'm_w_ff1', 'm_w_ff2', 'm_g_final', 'v_w_ada', 'v_b_ada', 'v_g_mix', 'v_g_mlp', 'v_mla_w_dq', 'v_mla_g_q', 'v_mla_w_uq', 'v_mla_w_dkv', 'v_mla_g_kv', 'v_mla_w_ukv', 'v_mla_w_o', 'v_swa_w_qkv', 'v_swa_b_qkv', 'v_swa_sinks', 'v_swa_w_o', 'v_swa_b_o', 'v_w_ff1', 'v_w_ff2', 'v_g_final']
TWIN_OUTPUTS = ['loss', 'grad_x', 'grad_w_ada', 'grad_b_ada', 'grad_g_mix', 'grad_g_mlp', 'grad_mla_w_dq', 'grad_mla_g_q', 'grad_mla_w_uq', 'grad_mla_w_dkv', 'grad_mla_g_kv', 'grad_mla_w_ukv', 'grad_mla_w_o', 'grad_swa_w_qkv', 'grad_swa_b_qkv', 'grad_swa_sinks', 'grad_swa_w_o', 'grad_swa_b_o', 'grad_w_ff1', 'grad_w_ff2', 'grad_g_final', 'delta_w_ada', 'delta_b_ada', 'delta_g_mix', 'delta_g_mlp', 'delta_mla_w_dq', 'delta_mla_g_q', 'delta_mla_w_uq', 'delta_mla_w_dkv', 'delta_mla_g_kv', 'delta_mla_w_ukv', 'delta_mla_w_o', 'delta_swa_w_qkv', 'delta_swa_b_qkv', 'delta_swa_sinks', 'delta_swa_w_o', 'delta_swa_b_o', 'delta_w_ff1', 'delta_w_ff2', 'delta_g_final', 'new_m_w_ada', 'new_m_b_ada', 'new_m_g_mix', 'new_m_g_mlp', 'new_m_mla_w_dq', 'new_m_mla_g_q', 'new_m_mla_w_uq', 'new_m_mla_w_dkv', 'new_m_mla_g_kv', 'new_m_mla_w_ukv', 'new_m_mla_w_o', 'new_m_swa_w_qkv', 'new_m_swa_b_qkv', 'new_m_swa_sinks', 'new_m_swa_w_o', 'new_m_swa_b_o', 'new_m_w_ff1', 'new_m_w_ff2', 'new_m_g_final', 'new_v_w_ada', 'new_v_b_ada', 'new_v_g_mix', 'new_v_g_mlp', 'new_v_mla_w_dq', 'new_v_mla_g_q', 'new_v_mla_w_uq', 'new_v_mla_w_dkv', 'new_v_mla_g_kv', 'new_v_mla_w_ukv', 'new_v_mla_w_o', 'new_v_swa_w_qkv', 'new_v_swa_b_qkv', 'new_v_swa_sinks', 'new_v_swa_w_o', 'new_v_swa_b_o', 'new_v_w_ff1', 'new_v_w_ff2', 'new_v_g_final']
TWIN_LEAF_KINDS = {'loss': 'loss', 'grad_x': 'grad_x', 'grad_w_ada': 'grad_w', 'grad_b_ada': 'grad_w', 'grad_g_mix': 'grad_w', 'grad_g_mlp': 'grad_w', 'grad_mla_w_dq': 'grad_w', 'grad_mla_g_q': 'grad_w', 'grad_mla_w_uq': 'grad_w', 'grad_mla_w_dkv': 'grad_w', 'grad_mla_g_kv': 'grad_w', 'grad_mla_w_ukv': 'grad_w', 'grad_mla_w_o': 'grad_w', 'grad_swa_w_qkv': 'grad_w', 'grad_swa_b_qkv': 'grad_w', 'grad_swa_sinks': 'grad_w', 'grad_swa_w_o': 'grad_w', 'grad_swa_b_o': 'grad_w', 'grad_w_ff1': 'grad_w', 'grad_w_ff2': 'grad_w', 'grad_g_final': 'grad_w', 'delta_w_ada': 'delta_w', 'delta_b_ada': 'delta_w', 'delta_g_mix': 'delta_w', 'delta_g_mlp': 'delta_w', 'delta_mla_w_dq': 'delta_w', 'delta_mla_g_q': 'delta_w', 'delta_mla_w_uq': 'delta_w', 'delta_mla_w_dkv': 'delta_w', 'delta_mla_g_kv': 'delta_w', 'delta_mla_w_ukv': 'delta_w', 'delta_mla_w_o': 'delta_w', 'delta_swa_w_qkv': 'delta_w', 'delta_swa_b_qkv': 'delta_w', 'delta_swa_sinks': 'delta_w', 'delta_swa_w_o': 'delta_w', 'delta_swa_b_o': 'delta_w', 'delta_w_ff1': 'delta_w', 'delta_w_ff2': 'delta_w', 'delta_g_final': 'delta_w', 'new_m_w_ada': 'new_m', 'new_m_b_ada': 'new_m', 'new_m_g_mix': 'new_m', 'new_m_g_mlp': 'new_m', 'new_m_mla_w_dq': 'new_m', 'new_m_mla_g_q': 'new_m', 'new_m_mla_w_uq': 'new_m', 'new_m_mla_w_dkv': 'new_m', 'new_m_mla_g_kv': 'new_m', 'new_m_mla_w_ukv': 'new_m', 'new_m_mla_w_o': 'new_m', 'new_m_swa_w_qkv': 'new_m', 'new_m_swa_b_qkv': 'new_m', 'new_m_swa_sinks': 'new_m', 'new_m_swa_w_o': 'new_m', 'new_m_swa_b_o': 'new_m', 'new_m_w_ff1': 'new_m', 'new_m_w_ff2': 'new_m', 'new_m_g_final': 'new_m', 'new_v_w_ada': 'new_v', 'new_v_b_ada': 'new_v', 'new_v_g_mix': 'new_v', 'new_v_g_mlp': 'new_v', 'new_v_mla_w_dq': 'new_v', 'new_v_mla_g_q': 'new_v', 'new_v_mla_w_uq': 'new_v', 'new_v_mla_w_dkv': 'new_v', 'new_v_mla_g_kv': 'new_v', 'new_v_mla_w_ukv': 'new_v', 'new_v_mla_w_o': 'new_v', 'new_v_swa_w_qkv': 'new_v', 'new_v_swa_b_qkv': 'new_v', 'new_v_swa_sinks': 'new_v', 'new_v_swa_w_o': 'new_v', 'new_v_swa_b_o': 'new_v', 'new_v_w_ff1': 'new_v', 'new_v_w_ff2': 'new_v', 'new_v_g_final': 'new_v'}


def _forward(args):
    return _fwd_reference(*[args[k] for k in FWD_PARAMS])


def _output_shape():
    def fwd():
        inp = _fwd_setup_inputs(0)
        return _fwd_reference(*[inp[k] for k in FWD_PARAMS])
    out = _jax.eval_shape(fwd)
    return out.shape, out.dtype

N_MICROBATCH = 1
ADAM_LR = 0.001
ADAM_B1 = 0.9
ADAM_B2 = 0.999
ADAM_EPS = 1e-08
ADAM_WD = 0.01
ADAM_STEP = 10
PER_EXAMPLE_BATCH_AXIS = {'x': 0, 'c': 0, 'positions': 0, 'loss_target': 0}
SHARED_INPUTS = []
_WEIGHT_DTYPES = {'w_ada': _jnp.float32, 'b_ada': _jnp.float32, 'g_mix': _jnp.float32, 'g_mlp': _jnp.float32, 'mla_w_dq': _jnp.float32, 'mla_g_q': _jnp.float32, 'mla_w_uq': _jnp.float32, 'mla_w_dkv': _jnp.float32, 'mla_g_kv': _jnp.float32, 'mla_w_ukv': _jnp.float32, 'mla_w_o': _jnp.float32, 'swa_w_qkv': _jnp.float32, 'swa_b_qkv': _jnp.float32, 'swa_sinks': _jnp.float32, 'swa_w_o': _jnp.float32, 'swa_b_o': _jnp.float32, 'w_ff1': _jnp.float32, 'w_ff2': _jnp.float32, 'g_final': _jnp.float32}
MOMENT_SCALE = {'w_ada': 2.212940e-01, 'b_ada': 4.846995e-01, 'g_mix': 3.452190e-02, 'g_mlp': 1.121826e-01, 'mla_w_dq': 2.314109e-02, 'mla_g_q': 2.235065e-02, 'mla_w_uq': 1.157019e-02, 'mla_w_dkv': 5.640254e-02, 'mla_g_kv': 6.724547e-02, 'mla_w_ukv': 2.326180e-02, 'mla_w_o': 3.469865e-02, 'swa_w_qkv': 4.481638e-02, 'swa_b_qkv': 1.304743e-01, 'swa_sinks': 3.829427e-02, 'swa_w_o': 5.051036e-02, 'swa_b_o': 1.516333e-01, 'w_ff1': 6.134633e-02, 'w_ff2': 1.562666e-01, 'g_final': 6.463272e+01}


def _to_microbatches(a, axis):
    t = _jnp.moveaxis(a, axis, 0)
    t = t.reshape((N_MICROBATCH, t.shape[0] // N_MICROBATCH) + t.shape[1:])
    return _jnp.moveaxis(t, 1, axis + 1)


def setup_inputs(seed: int = 0) -> dict:
    inp = _fwd_setup_inputs(seed)
    key = _jax.random.fold_in(_jax.random.key(seed), 7919)
    shape, _ = _output_shape()
    out = dict(inp)
    out["loss_target"] = _jax.random.normal(_jax.random.fold_in(key, 0), shape, _jnp.float32)
    for i, name in enumerate(TWIN_WEIGHTS):
        w = inp[name].astype(_jnp.float32)
        if MOMENT_SCALE is None:
            s = _jnp.sqrt(_jnp.mean(_jnp.square(w)) + 1e-30)
        else:
            s = MOMENT_SCALE[name]
        km, kv = _jax.random.split(_jax.random.fold_in(key, i + 1))
        out[name] = w
        out["m_" + name] = s * _jax.random.normal(km, w.shape, _jnp.float32)
        out["v_" + name] = (s * s) * _jax.random.uniform(kv, w.shape, _jnp.float32, 0.5, 1.5)
    if N_MICROBATCH > 1:
        for name, axis in PER_EXAMPLE_BATCH_AXIS.items():
            out[name] = _to_microbatches(out[name], axis)
    return {'x': out['x'], 'c': out['c'], 'positions': out['positions'], 'w_ada': out['w_ada'], 'b_ada': out['b_ada'], 'g_mix': out['g_mix'], 'g_mlp': out['g_mlp'], 'mla_w_dq': out['mla_w_dq'], 'mla_g_q': out['mla_g_q'], 'mla_w_uq': out['mla_w_uq'], 'mla_w_dkv': out['mla_w_dkv'], 'mla_g_kv': out['mla_g_kv'], 'mla_w_ukv': out['mla_w_ukv'], 'mla_w_o': out['mla_w_o'], 'swa_w_qkv': out['swa_w_qkv'], 'swa_b_qkv': out['swa_b_qkv'], 'swa_sinks': out['swa_sinks'], 'swa_w_o': out['swa_w_o'], 'swa_b_o': out['swa_b_o'], 'w_ff1': out['w_ff1'], 'w_ff2': out['w_ff2'], 'g_final': out['g_final'], 'loss_target': out['loss_target'], 'm_w_ada': out['m_w_ada'], 'm_b_ada': out['m_b_ada'], 'm_g_mix': out['m_g_mix'], 'm_g_mlp': out['m_g_mlp'], 'm_mla_w_dq': out['m_mla_w_dq'], 'm_mla_g_q': out['m_mla_g_q'], 'm_mla_w_uq': out['m_mla_w_uq'], 'm_mla_w_dkv': out['m_mla_w_dkv'], 'm_mla_g_kv': out['m_mla_g_kv'], 'm_mla_w_ukv': out['m_mla_w_ukv'], 'm_mla_w_o': out['m_mla_w_o'], 'm_swa_w_qkv': out['m_swa_w_qkv'], 'm_swa_b_qkv': out['m_swa_b_qkv'], 'm_swa_sinks': out['m_swa_sinks'], 'm_swa_w_o': out['m_swa_w_o'], 'm_swa_b_o': out['m_swa_b_o'], 'm_w_ff1': out['m_w_ff1'], 'm_w_ff2': out['m_w_ff2'], 'm_g_final': out['m_g_final'], 'v_w_ada': out['v_w_ada'], 'v_b_ada': out['v_b_ada'], 'v_g_mix': out['v_g_mix'], 'v_g_mlp': out['v_g_mlp'], 'v_mla_w_dq': out['v_mla_w_dq'], 'v_mla_g_q': out['v_mla_g_q'], 'v_mla_w_uq': out['v_mla_w_uq'], 'v_mla_w_dkv': out['v_mla_w_dkv'], 'v_mla_g_kv': out['v_mla_g_kv'], 'v_mla_w_ukv': out['v_mla_w_ukv'], 'v_mla_w_o': out['v_mla_w_o'], 'v_swa_w_qkv': out['v_swa_w_qkv'], 'v_swa_b_qkv': out['v_swa_b_qkv'], 'v_swa_sinks': out['v_swa_sinks'], 'v_swa_w_o': out['v_swa_w_o'], 'v_swa_b_o': out['v_swa_b_o'], 'v_w_ff1': out['v_w_ff1'], 'v_w_ff2': out['v_w_ff2'], 'v_g_final': out['v_g_final']}


def _loss(weights, diff, rest, loss_target):
    with _jax.named_scope("forward"):
        args = {**rest, TWIN_DIFF_INPUT: diff, **{k: w.astype(_WEIGHT_DTYPES[k]) for k, w in weights.items()}}
        y = _forward(args)
    with _jax.named_scope("loss_head"):
        err = _jnp.square(y.astype(_jnp.float32) - loss_target)
        return 0.5 * _jnp.sum(_jnp.mean(err, axis=-1)) if err.ndim else 0.5 * err


def _adamw(w, g, m, v):
    m = ADAM_B1 * m + (1.0 - ADAM_B1) * g
    v = ADAM_B2 * v + (1.0 - ADAM_B2) * _jnp.square(g)
    m_hat = m / (1.0 - ADAM_B1 ** ADAM_STEP)
    v_hat = v / (1.0 - ADAM_B2 ** ADAM_STEP)
    delta = -ADAM_LR * (m_hat / (_jnp.sqrt(v_hat) + ADAM_EPS) + ADAM_WD * w)
    return delta, m, v


def reference(x, c, positions, w_ada, b_ada, g_mix, g_mlp, mla_w_dq, mla_g_q, mla_w_uq, mla_w_dkv, mla_g_kv, mla_w_ukv, mla_w_o, swa_w_qkv, swa_b_qkv, swa_sinks, swa_w_o, swa_b_o, w_ff1, w_ff2, g_final, loss_target, m_w_ada, m_b_ada, m_g_mix, m_g_mlp, m_mla_w_dq, m_mla_g_q, m_mla_w_uq, m_mla_w_dkv, m_mla_g_kv, m_mla_w_ukv, m_mla_w_o, m_swa_w_qkv, m_swa_b_qkv, m_swa_sinks, m_swa_w_o, m_swa_b_o, m_w_ff1, m_w_ff2, m_g_final, v_w_ada, v_b_ada, v_g_mix, v_g_mlp, v_mla_w_dq, v_mla_g_q, v_mla_w_uq, v_mla_w_dkv, v_mla_g_kv, v_mla_w_ukv, v_mla_w_o, v_swa_w_qkv, v_swa_b_qkv, v_swa_sinks, v_swa_w_o, v_swa_b_o, v_w_ff1, v_w_ff2, v_g_final):
    given = dict(x=x, c=c, positions=positions, w_ada=w_ada, b_ada=b_ada, g_mix=g_mix, g_mlp=g_mlp, mla_w_dq=mla_w_dq, mla_g_q=mla_g_q, mla_w_uq=mla_w_uq, mla_w_dkv=mla_w_dkv, mla_g_kv=mla_g_kv, mla_w_ukv=mla_w_ukv, mla_w_o=mla_w_o, swa_w_qkv=swa_w_qkv, swa_b_qkv=swa_b_qkv, swa_sinks=swa_sinks, swa_w_o=swa_w_o, swa_b_o=swa_b_o, w_ff1=w_ff1, w_ff2=w_ff2, g_final=g_final, loss_target=loss_target, m_w_ada=m_w_ada, m_b_ada=m_b_ada, m_g_mix=m_g_mix, m_g_mlp=m_g_mlp, m_mla_w_dq=m_mla_w_dq, m_mla_g_q=m_mla_g_q, m_mla_w_uq=m_mla_w_uq, m_mla_w_dkv=m_mla_w_dkv, m_mla_g_kv=m_mla_g_kv, m_mla_w_ukv=m_mla_w_ukv, m_mla_w_o=m_mla_w_o, m_swa_w_qkv=m_swa_w_qkv, m_swa_b_qkv=m_swa_b_qkv, m_swa_sinks=m_swa_sinks, m_swa_w_o=m_swa_w_o, m_swa_b_o=m_swa_b_o, m_w_ff1=m_w_ff1, m_w_ff2=m_w_ff2, m_g_final=m_g_final, v_w_ada=v_w_ada, v_b_ada=v_b_ada, v_g_mix=v_g_mix, v_g_mlp=v_g_mlp, v_mla_w_dq=v_mla_w_dq, v_mla_g_q=v_mla_g_q, v_mla_w_uq=v_mla_w_uq, v_mla_w_dkv=v_mla_w_dkv, v_mla_g_kv=v_mla_g_kv, v_mla_w_ukv=v_mla_w_ukv, v_mla_w_o=v_mla_w_o, v_swa_w_qkv=v_swa_w_qkv, v_swa_b_qkv=v_swa_b_qkv, v_swa_sinks=v_swa_sinks, v_swa_w_o=v_swa_w_o, v_swa_b_o=v_swa_b_o, v_w_ff1=v_w_ff1, v_w_ff2=v_w_ff2, v_g_final=v_g_final)
    weights = {n: given[n] for n in TWIN_WEIGHTS}
    shared = {n: given[n] for n in SHARED_INPUTS}
    per_example = {n: given[n] for n in ['x', 'c', 'positions']}
    grad_fn = _jax.value_and_grad(_loss, argnums=(0, 1))

    def one_microbatch(ex, loss_target):
        ex = dict(ex)
        diff = ex.pop(TWIN_DIFF_INPUT)
        return grad_fn(weights, diff, {**shared, **ex}, loss_target)

    if N_MICROBATCH == 1:
        loss, (grad_w, grad_x) = one_microbatch(per_example, given["loss_target"])
    else:
        def body(carry, xs):
            loss_sum, grad_sum = carry
            l_k, (gw_k, gx_k) = one_microbatch(xs[0], xs[1])
            with _jax.named_scope("update"):
                return (loss_sum + l_k, _jax.tree.map(_jnp.add, grad_sum, gw_k)), gx_k

        init = (_jnp.zeros((), _jnp.float32), _jax.tree.map(_jnp.zeros_like, weights))
        (loss, grad_w), grad_x = _jax.lax.scan(body, init, (per_example, given["loss_target"]))
    with _jax.named_scope("update"):
        delta_w, new_m, new_v = {}, {}, {}
        for n in TWIN_WEIGHTS:
            delta_w[n], new_m[n], new_v[n] = _adamw(weights[n], grad_w[n], given["m_" + n], given["v_" + n])
    return (loss, grad_x, *[grad_w[n] for n in TWIN_WEIGHTS], *[delta_w[n] for n in TWIN_WEIGHTS],
            *[new_m[n] for n in TWIN_WEIGHTS], *[new_v[n] for n in TWIN_WEIGHTS])
```

```python
import functools

import jax
import jax.numpy as jnp
import numpy as np
from jax import lax
from jax.experimental import pallas as pl
from jax.experimental.pallas import tpu as pltpu

F32 = jnp.float32
BF16 = jnp.bfloat16
MESH_IDS = pl.DeviceIdType.MESH
N_DEV = 8

MLA_HEADS = 8
QK_NOPE = 128
QK_ROPE = 64
QK_DIM = QK_NOPE + QK_ROPE
V_DIM = 128
KV_LORA = 256
ROPE_THETA = 10000.0
SWA_HEADS = 16
SWA_KV_HEADS = 4
SWA_GROUP = SWA_HEADS // SWA_KV_HEADS
SWA_HEAD_DIM = 64
WINDOW = 128
EPS = 1e-6

ADAM_LR = 0.001
ADAM_B1 = 0.9
ADAM_B2 = 0.999
ADAM_EPS = 1e-08
ADAM_WD = 0.01
ADAM_STEP = 10

PACK_COLS = 1024
RS_CHUNKS = 4
VMEM_LIMIT = 56 << 20
ROW_TILE = 512
ROW_TILE_WIDE = 256
ATT_TILE = 512


def _dot(a, b):
    return jnp.dot(a, b, preferred_element_type=F32)


def _dot_nt(a, b):
    return lax.dot_general(a, b, (((1,), (1,)), ((), ())), preferred_element_type=F32)


def _dot_tn(a, b):
    return lax.dot_general(a, b, (((0,), (0,)), ((), ())), preferred_element_type=F32)


def _rstd(x):
    return lax.rsqrt(jnp.mean(x * x, axis=-1, keepdims=True) + EPS)


def _rms_bwd(dn, n, r):
    return r * (dn - n * jnp.mean(dn * n, axis=-1, keepdims=True))


def _modulate(x, g, sc, sh):
    r = _rstd(x)
    return ((x * r) * g) * (1.0 + sc) + sh


def _modulate_bwd(dh, x, g, sc):
    r = _rstd(x)
    n = x * r
    dsh = jnp.sum(dh, axis=0, keepdims=True)
    da = jnp.sum(dh * n, axis=0, keepdims=True)
    dx = _rms_bwd(dh * (g * (1.0 + sc)), n, r)
    return dx, dsh, da


def _first(i):
    return i == 0


def _acc(ref, val, i):
    @pl.when(i == 0)
    def _():
        ref[...] = val

    @pl.when(i != 0)
    def _():
        ref[...] += val


def _row_spec(shape, tm):
    nd = len(shape)
    return pl.BlockSpec(tuple(shape[:nd - 2]) + (tm, shape[-1]), lambda i: (0,) * (nd - 2) + (i, 0))


def _resident_spec(shape, single_buffer):
    nd = len(shape)
    if single_buffer:
        return pl.BlockSpec(tuple(shape), lambda i: (0,) * nd, pipeline_mode=pl.Buffered(1))
    return pl.BlockSpec(tuple(shape), lambda i: (0,) * nd)


def _rowcall(name, body, tokens, tm, row_in, full_in, row_out, acc_out=()):
    tm = min(tm, tokens)
    in_specs = [_row_spec(a.shape, tm) for a in row_in] + [_resident_spec(a.shape, True) for a in full_in]
    out_specs = [_row_spec(s.shape, tm) for s in row_out] + [_resident_spec(s.shape, False) for s in acc_out]
    return pl.pallas_call(
        body, name=name, grid=(tokens // tm,), in_specs=in_specs, out_specs=out_specs,
        out_shape=list(row_out) + list(acc_out),
        compiler_params=pltpu.CompilerParams(dimension_semantics=("arbitrary",), vmem_limit_bytes=VMEM_LIMIT),
    )(*row_in, *full_in)


def _sds(shape, dtype):
    return jax.ShapeDtypeStruct(tuple(shape), dtype)


def _mla_in_fwd(x, cos, sin, g, sc, sh, w_cat, g_q, w_uqx, g_kv, w_ukv):
    S, D = x.shape
    QL = g_q.shape[1]
    H = MLA_HEADS

    def body(x_ref, cos_ref, sin_ref, g_ref, sc_ref, sh_ref, wcat_ref, gq_ref, wuqx_ref, gkv_ref, wukv_ref,
             h_ref, cqp_ref, cq_ref, ckvp_ref, ckv_ref, q_ref, k_ref, v_ref):
        cs, sn = cos_ref[...], sin_ref[...]
        hb = _modulate(x_ref[...], g_ref[...], sc_ref[...], sh_ref[...]).astype(BF16)
        h_ref[...] = hb
        low = _dot(hb, wcat_ref[...])
        cqp = low[:, :QL]
        cqp_ref[...] = cqp
        cq = ((cqp * _rstd(cqp)) * gq_ref[...]).astype(BF16)
        cq_ref[...] = cq
        ckvp = low[:, QL:QL + KV_LORA]
        ckvp_ref[...] = ckvp
        ckv = ((ckvp * _rstd(ckvp)) * gkv_ref[...]).astype(BF16)
        ckv_ref[...] = ckv
        o = QL + KV_LORA
        kr = (low[:, o:o + QK_ROPE] * cs + low[:, o + QK_ROPE:o + 2 * QK_ROPE] * sn).astype(BF16)
        qx = _dot(cq, wuqx_ref[...])
        kv = _dot(ckv, wukv_ref[...])
        for hd in range(H):
            b = hd * 256
            q_ref[hd, :, 0:QK_NOPE] = qx[:, b:b + QK_NOPE].astype(BF16)
            q_ref[hd, :, QK_NOPE:QK_DIM] = (qx[:, b + 128:b + 192] * cs + qx[:, b + 192:b + 256] * sn).astype(BF16)
            k_ref[hd, :, 0:QK_NOPE] = kv[:, b:b + QK_NOPE].astype(BF16)
            k_ref[hd, :, QK_NOPE:QK_DIM] = kr
            v_ref[hd] = kv[:, b + 128:b + 256].astype(BF16)

    return _rowcall(
        "mla_in_fwd", body, S, ROW_TILE, [x, cos, sin], [g, sc, sh, w_cat, g_q, w_uqx, g_kv, w_ukv],
        [_sds((S, D), BF16), _sds((S, QL), F32), _sds((S, QL), BF16), _sds((S, KV_LORA), F32), _sds((S, KV_LORA), BF16),
         _sds((H, S, QK_DIM), BF16), _sds((H, S, QK_DIM), BF16), _sds((H, S, V_DIM), BF16)])


def _mla_attn_fwd(q, k, v, t):
    H, S, DQ = q.shape
    DV = v.shape[-1]
    t = min(t, S)
    nb = S // t
    scale = QK_DIM ** -0.5

    def body(q_ref, k_ref, v_ref, o_ref, lse_ref, m_s, l_s, acc_s):
        qi, ki = pl.program_id(1), pl.program_id(2)

        @pl.when(ki == 0)
        def _():
            m_s[...] = jnp.full_like(m_s, -jnp.inf)
            l_s[...] = jnp.zeros_like(l_s)
            acc_s[...] = jnp.zeros_like(acc_s)

        def step(diagonal):
            s = _dot_nt(q_ref[0], k_ref[0]) * scale
            if diagonal:
                row = lax.broadcasted_iota(jnp.int32, (t, t), 0)
                col = lax.broadcasted_iota(jnp.int32, (t, t), 1)
                s = jnp.where(col <= row, s, -jnp.inf)
            m_prev = m_s[...]
            m_new = jnp.maximum(m_prev, jnp.max(s, axis=-1, keepdims=True))
            alpha = jnp.exp(m_prev - m_new)
            p = jnp.exp(s - m_new)
            l_s[...] = alpha * l_s[...] + jnp.sum(p, axis=-1, keepdims=True)
            acc_s[...] = alpha * acc_s[...] + _dot(p.astype(BF16), v_ref[0])
            m_s[...] = m_new

        @pl.when(ki < qi)
        def _():
            step(False)

        @pl.when(ki == qi)
        def _():
            step(True)
            l = l_s[...]
            o_ref[...] = (acc_s[...] / l).astype(BF16)
            lse_ref[0] = m_s[...] + jnp.log(l)

    return pl.pallas_call(
        body, name="mla_attn_fwd", grid=(H, nb, nb),
        in_specs=[pl.BlockSpec((1, t, DQ), lambda h, i, j: (h, i, 0)),
                  pl.BlockSpec((1, t, DQ), lambda h, i, j: (h, jnp.minimum(i, j), 0)),
                  pl.BlockSpec((1, t, DV), lambda h, i, j: (h, jnp.minimum(i, j), 0))],
        out_specs=[pl.BlockSpec((t, DV), lambda h, i, j: (i, h)),
                   pl.BlockSpec((1, t, 1), lambda h, i, j: (h, i, 0))],
        out_shape=[_sds((S, H * DV), BF16), _sds((H, S, 1), F32)],
        scratch_shapes=[pltpu.VMEM((t, 1), F32), pltpu.VMEM((t, 1), F32), pltpu.VMEM((t, DV), F32)],
        compiler_params=pltpu.CompilerParams(dimension_semantics=("parallel", "parallel", "arbitrary"),
                                             vmem_limit_bytes=VMEM_LIMIT),
    )(q, k, v)


def _attn_out_fwd(o, x, w_o, b_o, gt, g, sc, sh):
    S, D = x.shape

    def body(o_ref, x_ref, wo_ref, bo_ref, gt_ref, g_ref, sc_ref, sh_ref, y_ref, x1_ref, h_ref):
        y = _dot(o_ref[...], wo_ref[...]) + bo_ref[...]
        y_ref[...] = y
        x1 = x_ref[...] + gt_ref[...] * y
        x1_ref[...] = x1
        h_ref[...] = _modulate(x1, g_ref[...], sc_ref[...], sh_ref[...]).astype(BF16)

    return _rowcall("attn_out_fwd", body, S, ROW_TILE, [o, x], [w_o, b_o, gt, g, sc, sh],
                    [_sds((S, D), F32), _sds((S, D), F32), _sds((S, D), BF16)])


def _mlp_fwd(h, x, w1, w2, gt):
    S, D = x.shape
    FF = w1.shape[1]

    def body(h_ref, x_ref, w1_ref, w2_ref, gt_ref, rl_ref, act_ref, y_ref, x2_ref):
        rl = jnp.maximum(_dot(h_ref[...], w1_ref[...]), 0.0)
        rl_ref[...] = rl.astype(BF16)
        act = (rl * rl).astype(BF16)
        act_ref[...] = act
        y = _dot(act, w2_ref[...])
        y_ref[...] = y
        x2_ref[...] = x_ref[...] + gt_ref[...] * y

    return _rowcall("mlp_fwd", body, S, ROW_TILE_WIDE, [h, x], [w1, w2, gt],
                    [_sds((S, FF), BF16), _sds((S, FF), BF16), _sds((S, D), F32), _sds((S, D), F32)])


def _swa_in_fwd(x, g, sc, sh, w_qkv, b_qkv):
    S, D = x.shape
    NQ = SWA_HEADS * SWA_HEAD_DIM
    NK = SWA_KV_HEADS * SWA_HEAD_DIM

    def body(x_ref, g_ref, sc_ref, sh_ref, w_ref, b_ref, h_ref, q_ref, k_ref, v_ref):
        hb = _modulate(x_ref[...], g_ref[...], sc_ref[...], sh_ref[...]).astype(BF16)
        h_ref[...] = hb
        qkv = _dot(hb, w_ref[...]) + b_ref[...]
        q_ref[...] = qkv[:, :NQ].astype(BF16)
        k_ref[...] = qkv[:, NQ:NQ + NK].astype(BF16)
        v_ref[...] = qkv[:, NQ + NK:].astype(BF16)

    return _rowcall("swa_in_fwd", body, S, ROW_TILE, [x], [g, sc, sh, w_qkv, b_qkv],
                    [_sds((S, D), BF16), _sds((S, NQ), BF16), _sds((S, NK), BF16), _sds((S, NK), BF16)])


def _alibi_slope(head):
    return float(np.float32(2.0 ** (-8.0 * (head + 1) / SWA_HEADS)))


def _swa_geometry(n):
    W = WINDOW
    row = lax.broadcasted_iota(jnp.int32, (W, 2 * W), 0)
    col = lax.broadcasted_iota(jnp.int32, (W, 2 * W), 1)
    dist = W + row - col
    valid = (dist >= 0) & (dist < W) & ((n > 0) | (col >= W))
    return dist.astype(F32), valid


def _swa_band_specs(W, nb, cols):
    prev = pl.BlockSpec((W, cols), lambda n: (jnp.maximum(jnp.minimum(n, nb - 1) - 1, 0), 0))
    cur = pl.BlockSpec((W, cols), lambda n: (jnp.minimum(n, nb - 1), 0))
    return prev, cur


def _swa_attn_fwd(q, k, v, sinks):
    S, NQ = q.shape
    NK = k.shape[1]
    W, Dh, G = WINDOW, SWA_HEAD_DIM, SWA_GROUP
    nb = S // W

    def body(q_ref, kp_ref, kc_ref, vp_ref, vc_ref, sink_ref, o_ref, lse_ref):
        distf, valid = _swa_geometry(pl.program_id(0))
        for kh in range(SWA_KV_HEADS):
            ck = slice(kh * Dh, (kh + 1) * Dh)
            kb = jnp.concatenate([kp_ref[:, ck], kc_ref[:, ck]], axis=0)
            vb = jnp.concatenate([vp_ref[:, ck], vc_ref[:, ck]], axis=0)
            for gi in range(G):
                hq = kh * G + gi
                cq = slice(hq * Dh, (hq + 1) * Dh)
                s = _dot_nt(q_ref[:, cq], kb) * (Dh ** -0.5) - _alibi_slope(hq) * distf
                s = jnp.where(valid, s, -jnp.inf)
                sink = sink_ref[:, hq:hq + 1]
                m = jnp.maximum(jnp.max(s, axis=-1, keepdims=True), sink)
                p = jnp.exp(s - m)
                denom = jnp.sum(p, axis=-1, keepdims=True) + jnp.exp(sink - m)
                o_ref[:, cq] = _dot((p / denom).astype(BF16), vb).astype(BF16)
                lse_ref[:, hq:hq + 1] = m + jnp.log(denom)

    kprev, kcur = _swa_band_specs(W, nb, NK)
    return pl.pallas_call(
        body, name="swa_attn_fwd", grid=(nb,),
        in_specs=[pl.BlockSpec((W, NQ), lambda n: (n, 0)), kprev, kcur, kprev, kcur,
                  pl.BlockSpec((1, SWA_HEADS), lambda n: (0, 0))],
        out_specs=[pl.BlockSpec((W, NQ), lambda n: (n, 0)), pl.BlockSpec((W, SWA_HEADS), lambda n: (n, 0))],
        out_shape=[_sds((S, NQ), BF16), _sds((S, SWA_HEADS), F32)],
        compiler_params=pltpu.CompilerParams(dimension_semantics=("arbitrary",), vmem_limit_bytes=VMEM_LIMIT),
    )(q, k, k, v, v, sinks)


def _final_loss(x, target, g):
    S, D = x.shape

    def body(x_ref, t_ref, g_ref, dx_ref, loss_ref, dg_ref):
        i = pl.program_id(0)
        xv = x_ref[...]
        r = _rstd(xv)
        n = xv * r
        err = n * g_ref[...] - t_ref[...]
        part = 0.5 * jnp.sum(jnp.mean(err * err, axis=-1, keepdims=True), axis=0, keepdims=True)
        _acc(loss_ref, jnp.broadcast_to(part, loss_ref.shape), i)
        dout = err / D
        _acc(dg_ref, jnp.sum(dout * n, axis=0, keepdims=True), i)
        dx_ref[...] = _rms_bwd(dout * g_ref[...], n, r)

    return _rowcall("final_loss", body, S, ROW_TILE, [x, target], [g], [_sds((S, D), F32)],
                    [_sds((1, 128), F32), _sds((1, D), F32)])


def _mlp_bwd_a(dx, y, rl, gt, w2):
    S, D = dx.shape
    FF = rl.shape[1]

    def body(dx_ref, y_ref, rl_ref, gt_ref, w2_ref, dy_ref, du_ref, dgt_ref):
        i = pl.program_id(0)
        dxv = dx_ref[...]
        _acc(dgt_ref, jnp.sum(dxv * y_ref[...], axis=0, keepdims=True), i)
        dy = (dxv * gt_ref[...]).astype(BF16)
        dy_ref[...] = dy
        dact = _dot_nt(dy, w2_ref[...])
        du_ref[...] = (dact * (2.0 * rl_ref[...].astype(F32))).astype(BF16)

    return _rowcall("mlp_bwd_a", body, S, ROW_TILE_WIDE, [dx, y, rl], [gt, w2],
                    [_sds((S, D), BF16), _sds((S, FF), BF16)], [_sds((1, D), F32)])


def _mlp_bwd_b(du, x, dx, w1, g, sc):
    S, D = x.shape

    def body(du_ref, x_ref, dx_ref, w1_ref, g_ref, sc_ref, dxo_ref, dsh_ref, da_ref):
        i = pl.program_id(0)
        dh = _dot_nt(du_ref[...], w1_ref[...])
        dxn, dsh, da = _modulate_bwd(dh, x_ref[...], g_ref[...], sc_ref[...])
        dxo_ref[...] = dx_ref[...] + dxn
        _acc(dsh_ref, dsh, i)
        _acc(da_ref, da, i)

    return _rowcall("mlp_bwd_b", body, S, ROW_TILE_WIDE, [du, x, dx], [w1, g, sc],
                    [_sds((S, D), F32)], [_sds((1, D), F32), _sds((1, D), F32)])


def _attn_out_bwd(dx, y, o, gt, w_o, n_heads):
    S, D = dx.shape
    NO = o.shape[1]
    dh = NO // n_heads

    def body(dx_ref, y_ref, o_ref, gt_ref, wo_ref, dy_ref, do_ref, dl_ref, dgt_ref, dbo_ref):
        i = pl.program_id(0)
        dxv = dx_ref[...]
        _acc(dgt_ref, jnp.sum(dxv * y_ref[...], axis=0, keepdims=True), i)
        dy = dxv * gt_ref[...]
        _acc(dbo_ref, jnp.sum(dy, axis=0, keepdims=True), i)
        dyb = dy.astype(BF16)
        dy_ref[...] = dyb
        do = _dot_nt(dyb, wo_ref[...])
        do_ref[...] = do.astype(BF16)
        prod = do * o_ref[...].astype(F32)
        for hd in range(n_heads):
            dl_ref[:, hd:hd + 1] = jnp.sum(prod[:, hd * dh:(hd + 1) * dh], axis=-1, keepdims=True)

    return _rowcall("attn_out_bwd", body, S, ROW_TILE, [dx, y, o], [gt, w_o],
                    [_sds((S, D), BF16), _sds((S, NO), BF16), _sds((S, n_heads), F32)],
                    [_sds((1, D), F32), _sds((1, D), F32)])


def _mla_attn_bwd(q, k, v, do, lse, delta, t):
    H, S, DQ = q.shape
    DV = v.shape[-1]
    t = min(t, S)
    nb = S // t
    scale = QK_DIM ** -0.5

    def body(q_ref, k_ref, v_ref, do_ref, lse_ref, dl_ref, dq_ref, dk_ref, dv_ref, dk_s, dv_s):
        ki, qi = pl.program_id(1), pl.program_id(2)

        @pl.when((ki == 0) & (qi == 0))
        def _():
            dq_ref[...] = jnp.zeros_like(dq_ref)

        @pl.when(qi == 0)
        def _():
            dk_s[...] = jnp.zeros_like(dk_s)
            dv_s[...] = jnp.zeros_like(dv_s)

        def step(diagonal):
            qv, kv_, vv, dov = q_ref[0], k_ref[0], v_ref[0], do_ref[...]
            s = _dot_nt(qv, kv_) * scale
            p = jnp.exp(s - lse_ref[0])
            if diagonal:
                row = lax.broadcasted_iota(jnp.int32, (t, t), 0)
                col = lax.broadcasted_iota(jnp.int32, (t, t), 1)
                p = jnp.where(col <= row, p, 0.0)
            dv_s[...] += _dot_tn(p.astype(BF16), dov)
            dp = _dot_nt(dov, vv)
            ds = (p * (dp - dl_ref[0]) * scale).astype(BF16)
            dk_s[...] += _dot_tn(ds, qv)
            rows = pl.ds(pl.multiple_of(qi * t, t), t)
            dq_ref[0, rows, :] += _dot(ds, kv_)

        @pl.when(qi > ki)
        def _():
            step(False)

        @pl.when(qi == ki)
        def _():
            step(True)

        @pl.when(qi == nb - 1)
        def _():
            dk_ref[0] = dk_s[...].astype(BF16)
            dv_ref[0] = dv_s[...].astype(BF16)

    qmap = lambda h, j, i: (h, jnp.maximum(i, j), 0)
    return pl.pallas_call(
        body, name="mla_attn_bwd", grid=(H, nb, nb),
        in_specs=[pl.BlockSpec((1, t, DQ), qmap),
                  pl.BlockSpec((1, t, DQ), lambda h, j, i: (h, j, 0)),
                  pl.BlockSpec((1, t, DV), lambda h, j, i: (h, j, 0)),
                  pl.BlockSpec((t, DV), lambda h, j, i: (jnp.maximum(i, j), h)),
                  pl.BlockSpec((1, t, 1), qmap), pl.BlockSpec((1, t, 1), qmap)],
        out_specs=[pl.BlockSpec((1, S, DQ), lambda h, j, i: (h, 0, 0)),
                   pl.BlockSpec((1, t, DQ), lambda h, j, i: (h, j, 0)),
                   pl.BlockSpec((1, t, DV), lambda h, j, i: (h, j, 0))],
        out_shape=[_sds((H, S, DQ), F32), _sds((H, S, DQ), BF16), _sds((H, S, DV), BF16)],
        scratch_shapes=[pltpu.VMEM((t, DQ), F32), pltpu.VMEM((t, DV), F32)],
        compiler_params=pltpu.CompilerParams(dimension_semantics=("parallel", "arbitrary", "arbitrary"),
                                             vmem_limit_bytes=VMEM_LIMIT),
    )(q, k, v, do, lse, delta)


def _swa_attn_bwd(q, k, v, do, lse, delta, sinks):
    S, NQ = q.shape
    NK = k.shape[1]
    W, Dh, G = WINDOW, SWA_HEAD_DIM, SWA_GROUP
    nb = S // W

    def body(q_ref, kp_ref, kc_ref, vp_ref, vc_ref, do_ref, lse_ref, dl_ref, sink_ref,
             dq_ref, dk_ref, dv_ref, dsink_ref, dkc_s, dvc_s):
        n = pl.program_id(0)

        @pl.when(n == 0)
        def _():
            dkc_s[...] = jnp.zeros_like(dkc_s)
            dvc_s[...] = jnp.zeros_like(dvc_s)
            dsink_ref[...] = jnp.zeros_like(dsink_ref)

        @pl.when(n < nb)
        def _():
            distf, valid = _swa_geometry(n)
            for kh in range(SWA_KV_HEADS):
                ck = slice(kh * Dh, (kh + 1) * Dh)
                kb = jnp.concatenate([kp_ref[:, ck], kc_ref[:, ck]], axis=0)
                vb = jnp.concatenate([vp_ref[:, ck], vc_ref[:, ck]], axis=0)
                dkb = jnp.zeros((2 * W, Dh), F32)
                dvb = jnp.zeros((2 * W, Dh), F32)
                for gi in range(G):
                    hq = kh * G + gi
                    cq = slice(hq * Dh, (hq + 1) * Dh)
                    qh, doh = q_ref[:, cq], do_ref[:, cq]
                    lse_h = lse_ref[:, hq:hq + 1]
                    dl_h = dl_ref[:, hq:hq + 1]
                    s = _dot_nt(qh, kb) * (Dh ** -0.5) - _alibi_slope(hq) * distf
                    p = jnp.where(valid, jnp.exp(s - lse_h), 0.0)
                    dvb = dvb + _dot_tn(p.astype(BF16), doh)
                    dp = _dot_nt(doh, vb)
                    dsb = ((p * (dp - dl_h)) * (Dh ** -0.5)).astype(BF16)
                    dq_ref[:, cq] = _dot(dsb, kb).astype(BF16)
                    dkb = dkb + _dot_tn(dsb, qh)
                    psink = jnp.exp(sink_ref[:, hq:hq + 1] - lse_h)
                    dsink_ref[:, hq:hq + 1] += -jnp.sum(psink * dl_h, axis=0, keepdims=True)
                dk_ref[:, ck] = (dkc_s[:, ck] + dkb[:W]).astype(BF16)
                dv_ref[:, ck] = (dvc_s[:, ck] + dvb[:W]).astype(BF16)
                dkc_s[:, ck] = dkb[W:]
                dvc_s[:, ck] = dvb[W:]

        @pl.when(n == nb)
        def _():
            dk_ref[...] = dkc_s[...].astype(BF16)
            dv_ref[...] = dvc_s[...].astype(BF16)

    kprev, kcur = _swa_band_specs(W, nb, NK)
    qspec = lambda cols: pl.BlockSpec((W, cols), lambda n: (jnp.minimum(n, nb - 1), 0))
    kvout = pl.BlockSpec((W, NK), lambda n: (jnp.maximum(n - 1, 0), 0))
    return pl.pallas_call(
        body, name="swa_attn_bwd", grid=(nb + 1,),
        in_specs=[qspec(NQ), kprev, kcur, kprev, kcur, qspec(NQ), qspec(SWA_HEADS), qspec(SWA_HEADS),
                  pl.BlockSpec((1, SWA_HEADS), lambda n: (0, 0))],
        out_specs=[qspec(NQ), kvout, kvout, pl.BlockSpec((1, 128), lambda n: (0, 0))],
        out_shape=[_sds((S, NQ), BF16), _sds((S, NK), BF16), _sds((S, NK), BF16), _sds((1, 128), F32)],
        scratch_shapes=[pltpu.VMEM((W, NK), F32), pltpu.VMEM((W, NK), F32)],
        compiler_params=pltpu.CompilerParams(dimension_semantics=("arbitrary",), vmem_limit_bytes=VMEM_LIMIT),
    )(q, k, k, v, v, do, lse, delta, sinks)


def _swa_in_bwd(dq, dk, dv, x, dx, w_qkv, g, sc):
    S, D = x.shape
    N = w_qkv.shape[1]

    def body(dq_ref, dk_ref, dv_ref, x_ref, dx_ref, w_ref, g_ref, sc_ref, dqkv_ref, dxo_ref, db_ref, dsh_ref, da_ref):
        i = pl.program_id(0)
        dqkv = jnp.concatenate([dq_ref[...], dk_ref[...], dv_ref[...]], axis=1)
        dqkv_ref[...] = dqkv
        _acc(db_ref, jnp.sum(dqkv.astype(F32), axis=0, keepdims=True), i)
        dh = _dot_nt(dqkv, w_ref[...])
        dxn, dsh, da = _modulate_bwd(dh, x_ref[...], g_ref[...], sc_ref[...])
        dxo_ref[...] = dx_ref[...] + dxn
        _acc(dsh_ref, dsh, i)
        _acc(da_ref, da, i)

    return _rowcall("swa_in_bwd", body, S, ROW_TILE, [dq, dk, dv, x, dx], [w_qkv, g, sc],
                    [_sds((S, N), BF16), _sds((S, D), F32)],
                    [_sds((1, N), F32), _sds((1, D), F32), _sds((1, D), F32)])


def _mla_in_bwd(dq, dk, dv, cos, sin, cqp, ckvp, x, dx, w_uqx, g_q, w_ukv, g_kv, w_cat, g, sc):
    S, D = x.shape
    H = MLA_HEADS
    QL = g_q.shape[1]
    NX = w_uqx.shape[1]
    NC = w_cat.shape[1]

    def body(dq_ref, dk_ref, dv_ref, cos_ref, sin_ref, cqp_ref, ckvp_ref, x_ref, dx_ref,
             wuqx_ref, gq_ref, wukv_ref, gkv_ref, wcat_ref, g_ref, sc_ref,
             dqx_ref, dkv_ref, dcat_ref, dxo_ref, dgq_ref, dgkv_ref, dsh_ref, da_ref):
        i = pl.program_id(0)
        cs, sn = cos_ref[...], sin_ref[...]
        dkr = jnp.zeros(cs.shape, F32)
        for hd in range(H):
            b = hd * 256
            dqh = dq_ref[hd]
            dqx_ref[:, b:b + QK_NOPE] = dqh[:, :QK_NOPE].astype(BF16)
            dqx_ref[:, b + 128:b + 192] = (dqh[:, QK_NOPE:] * cs).astype(BF16)
            dqx_ref[:, b + 192:b + 256] = (dqh[:, QK_NOPE:] * sn).astype(BF16)
            dkh = dk_ref[hd]
            dkv_ref[:, b:b + QK_NOPE] = dkh[:, :QK_NOPE]
            dkv_ref[:, b + 128:b + 256] = dv_ref[hd]
            dkr = dkr + dkh[:, QK_NOPE:].astype(F32)
        dcq = _dot_nt(dqx_ref[...], wuqx_ref[...])
        cqp = cqp_ref[...]
        rq = _rstd(cqp)
        nq = cqp * rq
        _acc(dgq_ref, jnp.sum(dcq * nq, axis=0, keepdims=True), i)
        dcqp = _rms_bwd(dcq * gq_ref[...], nq, rq)
        dckv = _dot_nt(dkv_ref[...], wukv_ref[...])
        ckvp = ckvp_ref[...]
        rk = _rstd(ckvp)
        nk = ckvp * rk
        _acc(dgkv_ref, jnp.sum(dckv * nk, axis=0, keepdims=True), i)
        dckvp = _rms_bwd(dckv * gkv_ref[...], nk, rk)
        dcat_ref[:, :QL] = dcqp.astype(BF16)
        dcat_ref[:, QL:QL + KV_LORA] = dckvp.astype(BF16)
        o = QL + KV_LORA
        dcat_ref[:, o:o + QK_ROPE] = (dkr * cs).astype(BF16)
        dcat_ref[:, o + QK_ROPE:o + 2 * QK_ROPE] = (dkr * sn).astype(BF16)
        dh = _dot_nt(dcat_ref[...], wcat_ref[...])
        dxn, dsh, da = _modulate_bwd(dh, x_ref[...], g_ref[...], sc_ref[...])
        dxo_ref[...] = dx_ref[...] + dxn
        _acc(dsh_ref, dsh, i)
        _acc(da_ref, da, i)

    return _rowcall("mla_in_bwd", body, S, ROW_TILE_WIDE, [dq, dk, dv, cos, sin, cqp, ckvp, x, dx],
                    [w_uqx, g_q, w_ukv, g_kv, w_cat, g, sc],
                    [_sds((S, NX), BF16), _sds((S, NX), BF16), _sds((S, NC), BF16), _sds((S, D), F32)],
                    [_sds((1, QL), F32), _sds((1, KV_LORA), F32), _sds((1, D), F32), _sds((1, D), F32)])


def _matmul_tn(name, a, b):
    S, K = a.shape
    N = b.shape[1]
    tk, tn, ts = min(K, 1024), min(N, 1024), min(S, ROW_TILE)
    if N % tn:
        tn = 512 if N % 512 == 0 else (384 if N % 384 == 0 else 128)
    if K % tk:
        tk = 512 if K % 512 == 0 else (384 if K % 384 == 0 else 128)
    ns = S // ts

    def body(a_ref, b_ref, o_ref):
        _acc(o_ref, _dot_tn(a_ref[...], b_ref[...]), pl.program_id(2))

    return pl.pallas_call(
        body, name=name, grid=(K // tk, N // tn, ns),
        in_specs=[pl.BlockSpec((ts, tk), lambda i, j, s: (s, i)), pl.BlockSpec((ts, tn), lambda i, j, s: (s, j))],
        out_specs=pl.BlockSpec((tk, tn), lambda i, j, s: (i, j)),
        out_shape=_sds((K, N), F32),
        compiler_params=pltpu.CompilerParams(dimension_semantics=("parallel", "parallel", "arbitrary"),
                                             vmem_limit_bytes=VMEM_LIMIT),
    )(a, b)


def _silu(c):
    return c * jax.nn.sigmoid(c)


def _ada_fwd(c_all, w_ada):
    L, D, NC = w_ada.shape

    def body(c_ref, w_ref, o_ref):
        cond = _silu(c_ref[...]).astype(BF16)
        o_ref[0] = _dot(cond, w_ref[0].astype(BF16))

    return pl.pallas_call(
        body, name="ada_fwd", grid=(L,),
        in_specs=[pl.BlockSpec(c_all.shape, lambda l: (0, 0)), pl.BlockSpec((1, D, NC), lambda l: (l, 0, 0))],
        out_specs=pl.BlockSpec((1, N_DEV, NC), lambda l: (l, 0, 0)),
        out_shape=_sds((L, N_DEV, NC), F32),
        compiler_params=pltpu.CompilerParams(dimension_semantics=("arbitrary",), vmem_limit_bytes=VMEM_LIMIT),
    )(c_all, w_ada)


def _adamw(w, g, m, v):
    m = ADAM_B1 * m + (1.0 - ADAM_B1) * g
    v = ADAM_B2 * v + (1.0 - ADAM_B2) * (g * g)
    m_hat = m / (1.0 - ADAM_B1 ** ADAM_STEP)
    v_hat = v / (1.0 - ADAM_B2 ** ADAM_STEP)
    delta = -ADAM_LR * (m_hat / (jnp.sqrt(v_hat) + ADAM_EPS) + ADAM_WD * w)
    return delta, m, v


def _ada_bwd_adamw(c_all_t, dmod_cols, w, m, v):
    L, D, NC = w.shape
    tr = min(D, 256)

    def body(ct_ref, dm_ref, w_ref, m_ref, v_ref, g_ref, d_ref, mo_ref, vo_ref):
        cond_t = _silu(ct_ref[...])
        dm = dm_ref[0]
        g = cond_t[:, 0:1] * dm[0:1, :]
        for b in range(1, N_DEV):
            g = g + cond_t[:, b:b + 1] * dm[b:b + 1, :]
        g_ref[0] = g
        d_ref[0], mo_ref[0], vo_ref[0] = _adamw(w_ref[0], g, m_ref[0], v_ref[0])

    wspec = pl.BlockSpec((1, tr, NC), lambda l, r: (l, r, 0))
    return pl.pallas_call(
        body, name="ada_bwd_adamw", grid=(L, D // tr),
        in_specs=[pl.BlockSpec((tr, N_DEV), lambda l, r: (r, 0)),
                  pl.BlockSpec((1, N_DEV, NC), lambda l, r: (l, 0, 0)), wspec, wspec, wspec],
        out_specs=[wspec] * 4, out_shape=[_sds(w.shape, F32)] * 4,
        compiler_params=pltpu.CompilerParams(dimension_semantics=("parallel", "parallel"), vmem_limit_bytes=VMEM_LIMIT),
    )(c_all_t, dmod_cols, w, m, v)


def _sum_devices(x):
    def body(x_ref, o_ref):
        s = x_ref[0]
        for j in range(1, N_DEV):
            s = s + x_ref[j]
        o_ref[...] = s

    return pl.pallas_call(body, name="sum_devices", out_shape=_sds(x.shape[1:], F32))(x)


def _adamw_small(w, g, m, v):
    def body(w_ref, g_ref, m_ref, v_ref, d_ref, mo_ref, vo_ref):
        d_ref[...], mo_ref[...], vo_ref[...] = _adamw(w_ref[...], g_ref[...], m_ref[...], v_ref[...])

    return pl.pallas_call(body, name="adamw_small", out_shape=[_sds(w.shape, F32)] * 3)(w, g, m, v)


def _me():
    return lax.axis_index("x") * 4 + lax.axis_index("y") * 2 + lax.axis_index("c")


def _peer(k):
    x, y, c = lax.axis_index("x"), lax.axis_index("y"), lax.axis_index("c")
    px = 1 - x if k & 4 else x
    py = 1 - y if k & 2 else y
    pc = 1 - c if k & 1 else c
    return (px, py, pc), px * 4 + py * 2 + pc


VMEM_SPEC = pl.BlockSpec(memory_space=pltpu.VMEM)
ANY_SPEC = pl.BlockSpec(memory_space=pl.ANY)


def _all_gather(name, x, out_dtype):
    R, C = x.shape
    cast = out_dtype != x.dtype

    def body(x_ref, out_ref, buf, send_sems, recv_sems, local_sem):
        me = _me()
        if cast:
            buf[...] = x_ref[...].astype(out_dtype)
            src = buf
        else:
            src = x_ref
        local = pltpu.make_async_copy(src, out_ref.at[me], local_sem)
        local.start()
        sends = []
        for k in range(1, N_DEV):
            dev, _ = _peer(k)
            cp = pltpu.make_async_remote_copy(src_ref=src, dst_ref=out_ref.at[me], send_sem=send_sems.at[k - 1],
                                              recv_sem=recv_sems.at[k - 1], device_id=dev, device_id_type=MESH_IDS)
            cp.start()
            sends.append(cp)
        for k in range(1, N_DEV):
            dev, pj = _peer(k)
            pltpu.make_async_remote_copy(src_ref=src, dst_ref=out_ref.at[pj], send_sem=send_sems.at[k - 1],
                                         recv_sem=recv_sems.at[k - 1], device_id=dev, device_id_type=MESH_IDS).wait_recv()
        for cp in sends:
            cp.wait_send()
        local.wait()

    return pl.pallas_call(
        body, name=name, in_specs=[VMEM_SPEC], out_specs=ANY_SPEC, out_shape=_sds((N_DEV, R, C), out_dtype),
        scratch_shapes=[pltpu.VMEM((R, C) if cast else (8, 128), out_dtype),
                        pltpu.SemaphoreType.DMA((N_DEV - 1,)), pltpu.SemaphoreType.DMA((N_DEV - 1,)),
                        pltpu.SemaphoreType.DMA(())],
        compiler_params=pltpu.CompilerParams(vmem_limit_bytes=VMEM_LIMIT),
    )(x)


def _all_to_all(name, x):
    _, R, C = x.shape

    def body(x_ref, out_ref, send_sems, recv_sems, local_sem):
        me = _me()
        local = pltpu.make_async_copy(x_ref.at[me], out_ref.at[me], local_sem)
        local.start()
        sends = []
        for k in range(1, N_DEV):
            dev, pj = _peer(k)
            cp = pltpu.make_async_remote_copy(src_ref=x_ref.at[pj], dst_ref=out_ref.at[me], send_sem=send_sems.at[k - 1],
                                              recv_sem=recv_sems.at[k - 1], device_id=dev, device_id_type=MESH_IDS)
            cp.start()
            sends.append(cp)
        for k in range(1, N_DEV):
            dev, pj = _peer(k)
            pltpu.make_async_remote_copy(src_ref=x_ref.at[pj], dst_ref=out_ref.at[pj], send_sem=send_sems.at[k - 1],
                                         recv_sem=recv_sems.at[k - 1], device_id=dev, device_id_type=MESH_IDS).wait_recv()
        for cp in sends:
            cp.wait_send()
        local.wait()

    return pl.pallas_call(
        body, name=name, in_specs=[VMEM_SPEC], out_specs=VMEM_SPEC, out_shape=_sds(x.shape, x.dtype),
        scratch_shapes=[pltpu.SemaphoreType.DMA((N_DEV - 1,)), pltpu.SemaphoreType.DMA((N_DEV - 1,)),
                        pltpu.SemaphoreType.DMA(())],
    )(x)


def _reduce_scatter_adamw(name, gblk, w, m, v):
    _, R, C = gblk.shape
    rows = 8
    for cand in (136, 128, 80, 64, 40, 32, 16, 8):
        if R % cand == 0:
            rows = cand
            break

    def body(g_ref, w_ref, m_ref, v_ref, go_ref, d_ref, mo_ref, vo_ref, recv, send_sems, recv_sems, local_sem):
        me = _me()
        local = pltpu.make_async_copy(g_ref.at[me], recv.at[me], local_sem)
        local.start()
        sends = []
        for k in range(1, N_DEV):
            dev, pj = _peer(k)
            cp = pltpu.make_async_remote_copy(src_ref=g_ref.at[pj], dst_ref=recv.at[me], send_sem=send_sems.at[k - 1],
                                              recv_sem=recv_sems.at[k - 1], device_id=dev, device_id_type=MESH_IDS)
            cp.start()
            sends.append(cp)
        for k in range(1, N_DEV):
            dev, pj = _peer(k)
            pltpu.make_async_remote_copy(src_ref=g_ref.at[pj], dst_ref=recv.at[pj], send_sem=send_sems.at[k - 1],
                                         recv_sem=recv_sems.at[k - 1], device_id=dev, device_id_type=MESH_IDS).wait_recv()
        local.wait()

        def chunk(i, carry):
            r = pl.ds(pl.multiple_of(i * rows, rows), rows)
            g = recv[0, r, :].astype(F32)
            for j in range(1, N_DEV):
                g = g + recv[j, r, :].astype(F32)
            go_ref[r, :] = g
            d_ref[r, :], mo_ref[r, :], vo_ref[r, :] = _adamw(w_ref[r, :], g, m_ref[r, :], v_ref[r, :])
            return carry

        lax.fori_loop(0, R // rows, chunk, 0)
        for cp in sends:
            cp.wait_send()

    return pl.pallas_call(
        body, name=name, in_specs=[ANY_SPEC, VMEM_SPEC, VMEM_SPEC, VMEM_SPEC], out_specs=[VMEM_SPEC] * 4,
        out_shape=[_sds((R, C), F32)] * 4,
        scratch_shapes=[pltpu.VMEM((N_DEV, R, C), BF16), pltpu.SemaphoreType.DMA((N_DEV - 1,)),
                        pltpu.SemaphoreType.DMA((N_DEV - 1,)), pltpu.SemaphoreType.DMA(())],
        compiler_params=pltpu.CompilerParams(vmem_limit_bytes=VMEM_LIMIT),
    )(gblk, w, m, v)


BIG = ["mla_w_dq", "mla_w_uq", "mla_w_dkv", "mla_w_ukv", "mla_w_o", "swa_w_qkv", "swa_w_o", "w_ff1", "w_ff2"]
ROW_SHARDED = {"mla_w_dq", "mla_w_dkv", "mla_w_o", "swa_w_o", "w_ff2"}


def _unblock(name, blocks):
    sh = blocks.shape[1:]
    if name in ROW_SHARDED:
        return jnp.moveaxis(blocks, 0, 1).reshape(sh[0], N_DEV * sh[1], sh[2])
    return jnp.moveaxis(blocks, 0, 2).reshape(sh[0], sh[1], N_DEV * sh[2])


def _block(name, full):
    L, K, N = full.shape
    if name in ROW_SHARDED:
        return jnp.moveaxis(full.reshape(L, N_DEV, K // N_DEV, N), 1, 0)
    return jnp.moveaxis(full.reshape(L, K, N_DEV, N // N_DEV), 2, 0)


def _rot_cols(w):
    half = QK_ROPE // 2
    return jnp.concatenate([-w[..., half:], w[..., :half]], axis=-1)


def _unrot_cols(gw):
    half = QK_ROPE // 2
    return jnp.concatenate([gw[..., half:], -gw[..., :half]], axis=-1)


def _row(v):
    return v.reshape(1, -1)


def _mlp_block_bwd(dx, sv, w1, w2, g, sc, gt):
    dy, du, dgt = _mlp_bwd_a(dx, sv["y2"], sv["rl"], gt, w2)
    dw2 = _matmul_tn("dw_ff2", sv["act"], dy)
    dw1 = _matmul_tn("dw_ff1", sv["h2"], du)
    dxo, dsh, da = _mlp_bwd_b(du, sv["x1"], dx, w1, g, sc)
    return dxo, dw1, dw2, dsh, da, dgt


def kernel(x, c, positions, w_ada, b_ada, g_mix, g_mlp, mla_w_dq, mla_g_q, mla_w_uq, mla_w_dkv, mla_g_kv, mla_w_ukv, mla_w_o, swa_w_qkv, swa_b_qkv, swa_sinks, swa_w_o, swa_b_o, w_ff1, w_ff2, g_final, loss_target, m_w_ada, m_b_ada, m_g_mix, m_g_mlp, m_mla_w_dq, m_mla_g_q, m_mla_w_uq, m_mla_w_dkv, m_mla_g_kv, m_mla_w_ukv, m_mla_w_o, m_swa_w_qkv, m_swa_b_qkv, m_swa_sinks, m_swa_w_o, m_swa_b_o, m_w_ff1, m_w_ff2, m_g_final, v_w_ada, v_b_ada, v_g_mix, v_g_mlp, v_mla_w_dq, v_mla_g_q, v_mla_w_uq, v_mla_w_dkv, v_mla_g_kv, v_mla_w_ukv, v_mla_w_o, v_swa_w_qkv, v_swa_b_qkv, v_swa_sinks, v_swa_w_o, v_swa_b_o, v_w_ff1, v_w_ff2, v_g_final):
    S, D = x.shape[1], x.shape[2]
    me = _me()
    x0 = x[0]
    target = loss_target[0]
    big_w = dict(mla_w_dq=mla_w_dq, mla_w_uq=mla_w_uq, mla_w_dkv=mla_w_dkv, mla_w_ukv=mla_w_ukv, mla_w_o=mla_w_o,
                 swa_w_qkv=swa_w_qkv, swa_w_o=swa_w_o, w_ff1=w_ff1, w_ff2=w_ff2)
    big_m = dict(mla_w_dq=m_mla_w_dq, mla_w_uq=m_mla_w_uq, mla_w_dkv=m_mla_w_dkv, mla_w_ukv=m_mla_w_ukv,
                 mla_w_o=m_mla_w_o, swa_w_qkv=m_swa_w_qkv, swa_w_o=m_swa_w_o, w_ff1=m_w_ff1, w_ff2=m_w_ff2)
    big_v = dict(mla_w_dq=v_mla_w_dq, mla_w_uq=v_mla_w_uq, mla_w_dkv=v_mla_w_dkv, mla_w_ukv=v_mla_w_ukv,
                 mla_w_o=v_mla_w_o, swa_w_qkv=v_swa_w_qkv, swa_w_o=v_swa_w_o, w_ff1=v_w_ff1, w_ff2=v_w_ff2)
    sizes = [big_w[n].size for n in BIG]
    offs = np.concatenate([[0], np.cumsum(sizes)])
    total = int(offs[-1])
    pack_rows = -(-total // (PACK_COLS * RS_CHUNKS * 16)) * RS_CHUNKS * 16

    def pack(parts):
        flat = jnp.concatenate([p.reshape(-1) for p in parts])
        return jnp.pad(flat, (0, pack_rows * PACK_COLS - total)).reshape(pack_rows, PACK_COLS)

    def unpack(flat2d, lead=()):
        flat = flat2d.reshape(lead + (-1,))
        return {n: flat[..., int(offs[i]):int(offs[i + 1])].reshape(lead + big_w[n].shape) for i, n in enumerate(BIG)}

    gathered = _all_gather("gather_weights", pack([big_w[n] for n in BIG]), BF16)
    wfull = {n: _unblock(n, b) for n, b in unpack(gathered, (N_DEV,)).items()}
    w_dq, w_dkv = wfull["mla_w_dq"][0], wfull["mla_w_dkv"][0]
    w_cat = jnp.concatenate([w_dq, w_dkv, _rot_cols(w_dkv[:, KV_LORA:])], axis=1)
    QL = w_dq.shape[1]
    w_uq = wfull["mla_w_uq"][0].reshape(QL, MLA_HEADS, QK_DIM)
    w_uqx = jnp.concatenate([w_uq, _rot_cols(w_uq[..., QK_NOPE:])], axis=-1).reshape(QL, MLA_HEADS * 256)
    w_ukv = wfull["mla_w_ukv"][0]
    w_o_mla, w_qkv, w_o_swa = wfull["mla_w_o"][0], wfull["swa_w_qkv"][0], wfull["swa_w_o"][0]
    ff1, ff2 = wfull["w_ff1"], wfull["w_ff2"]

    L = w_ada.shape[0]
    NC = w_ada.shape[2]
    nbq, nbo = swa_b_qkv.shape[1], swa_b_o.shape[1]
    cpad = -(-(D + nbq + nbo) // 1024) * 1024
    cpack = jnp.pad(jnp.concatenate([c[0], swa_b_qkv[0], swa_b_o[0]]), (0, cpad - (D + nbq + nbo))).reshape(8, cpad // 8)
    call = _all_gather("gather_c", cpack, F32).reshape(N_DEV, cpad)
    c_all = call[:, :D]
    b_qkv_full = call[:, D:D + nbq].reshape(1, N_DEV * nbq)
    b_o_full = call[:, D + nbq:D + nbq + nbo].reshape(1, N_DEV * nbo)
    mod_cols = _ada_fwd(c_all, w_ada)
    mpad = -(-(L * NC) // 1024) * 1024
    mod_send = jnp.pad(jnp.moveaxis(mod_cols, 1, 0).reshape(N_DEV, L * NC), ((0, 0), (0, mpad - L * NC)))
    mod_mine = _all_to_all("exchange_mod", mod_send.reshape(N_DEV, 8, mpad // 8)).reshape(N_DEV, mpad)[:, :L * NC]
    mod = jnp.moveaxis(mod_mine.reshape(N_DEV, L, NC), 0, 1).reshape(L, N_DEV * NC) + b_ada
    mods = mod.reshape(L, 6, 1, D)

    half = QK_ROPE // 2
    inv_freq = ROPE_THETA ** (-jnp.arange(half, dtype=F32) / half)
    ang = positions[0].astype(F32)[:, None] * inv_freq
    cos = jnp.concatenate([jnp.cos(ang), jnp.cos(ang)], axis=-1)
    sin = jnp.concatenate([jnp.sin(ang), jnp.sin(ang)], axis=-1)

    T_ATT = ATT_TILE
    zero_bias = jnp.zeros((1, D), F32)

    sh1, sc1, gt1, sh2, sc2, gt2 = [mods[0, i] for i in range(6)]
    gm0, gp0 = _row(g_mix[0]), _row(g_mlp[0])
    h1, cqp, cq, ckvp, ckv, q, k, v = _mla_in_fwd(x0, cos, sin, gm0, sc1, sh1, w_cat, mla_g_q, w_uqx, mla_g_kv, w_ukv)
    o0, lse0 = _mla_attn_fwd(q, k, v, T_ATT)
    y1, x1, h2 = _attn_out_fwd(o0, x0, w_o_mla, zero_bias, gt1, gp0, sc2, sh2)
    rl0, act0, y2, x2 = _mlp_fwd(h2, x1, ff1[0], ff2[0], gt2)
    sv0 = dict(y2=y2, rl=rl0, act=act0, h2=h2, x1=x1)

    th1, tc1, tg1, th2, tc2, tg2 = [mods[1, i] for i in range(6)]
    gm1, gp1 = _row(g_mix[1]), _row(g_mlp[1])
    h3, sq, sk, svv = _swa_in_fwd(x2, gm1, tc1, th1, w_qkv, b_qkv_full)
    o1, lse1 = _swa_attn_fwd(sq, sk, svv, swa_sinks)
    y3, x3, h4 = _attn_out_fwd(o1, x2, w_o_swa, b_o_full, tg1, gp1, tc2, th2)
    rl1, act1, y4, x4 = _mlp_fwd(h4, x3, ff1[1], ff2[1], tg2)
    sv1 = dict(y2=y4, rl=rl1, act=act1, h2=h4, x1=x3)
    dx4, loss_part, dg_final = _final_loss(x4, target, _row(g_final))

    dx3, dw1_1, dw2_1, dsh2_1, da2_1, dgt2_1 = _mlp_block_bwd(dx4, sv1, ff1[1], ff2[1], gp1, tc2, tg2)
    dy, do, dl, dgt1_1, db_o = _attn_out_bwd(dx3, y3, o1, tg1, w_o_swa, SWA_HEADS)
    dw_o_swa = _matmul_tn("dw_o", o1, dy)
    dsq, dsk, dsv, dsink = _swa_attn_bwd(sq, sk, svv, do, lse1, dl, swa_sinks)
    dqkv, dx2, db_qkv, dsh1_1, da1_1 = _swa_in_bwd(dsq, dsk, dsv, x2, dx3, w_qkv, gm1, tc1)
    dw_qkv = _matmul_tn("dw_qkv", h3, dqkv)

    dx1, dw1_0, dw2_0, dsh2_0, da2_0, dgt2_0 = _mlp_block_bwd(dx2, sv0, ff1[0], ff2[0], gp0, sc2, gt2)
    dy, do, dl, dgt1_0, _ = _attn_out_bwd(dx1, y1, o0, gt1, w_o_mla, MLA_HEADS)
    dw_o_mla = _matmul_tn("dw_o", o0, dy)
    delta = dl.T.reshape(MLA_HEADS, S, 1)
    dq, dk, dv = _mla_attn_bwd(q, k, v, do, lse0, delta, T_ATT)
    dqx, dkv, dcat, dx0, dg_q, dg_kv, dsh1_0, da1_0 = _mla_in_bwd(
        dq, dk, dv, cos, sin, cqp, ckvp, x0, dx1, w_uqx, mla_g_q, w_ukv, mla_g_kv, w_cat, gm0, sc1)
    dw_uqx = _matmul_tn("dw_uq", cq, dqx).reshape(QL, MLA_HEADS, 256)
    dw_ukv = _matmul_tn("dw_ukv", ckv, dkv)
    dw_cat = _matmul_tn("dw_down", h1, dcat)
    dw_uq = jnp.concatenate([dw_uqx[..., :QK_NOPE], dw_uqx[..., 128:192] + _unrot_cols(dw_uqx[..., 192:256])],
                            axis=-1).reshape(QL, MLA_HEADS * QK_DIM)
    o_kr = QL + KV_LORA
    dw_dkv = jnp.concatenate([dw_cat[:, QL:o_kr],
                              dw_cat[:, o_kr:o_kr + QK_ROPE] + _unrot_cols(dw_cat[:, o_kr + QK_ROPE:])], axis=1)

    gfull = dict(mla_w_dq=dw_cat[None, :, :QL], mla_w_uq=dw_uq[None], mla_w_dkv=dw_dkv[None], mla_w_ukv=dw_ukv[None],
                 mla_w_o=dw_o_mla[None], swa_w_qkv=dw_qkv[None], swa_w_o=dw_o_swa[None],
                 w_ff1=jnp.stack([dw1_0, dw1_1]), w_ff2=jnp.stack([dw2_0, dw2_1]))
    gflat = jnp.concatenate([_block(n, gfull[n]).astype(BF16).reshape(N_DEV, -1) for n in BIG], axis=1)
    gpack = jnp.pad(gflat, ((0, 0), (0, pack_rows * PACK_COLS - total))).reshape(N_DEV, pack_rows, PACK_COLS)
    wpack, mpack, vpack = (pack([d[n] for n in BIG]) for d in (big_w, big_m, big_v))
    rc = pack_rows // RS_CHUNKS
    outs = [_reduce_scatter_adamw("grad_exchange_adamw", gpack[:, i * rc:(i + 1) * rc], wpack[i * rc:(i + 1) * rc],
                                  mpack[i * rc:(i + 1) * rc], vpack[i * rc:(i + 1) * rc]) for i in range(RS_CHUNKS)]
    big_g, big_d, big_nm, big_nv = (unpack(jnp.concatenate([o[j] for o in outs], axis=0)) for j in range(4))

    dmod = jnp.stack([
        jnp.concatenate([dsh1_0, gm0 * da1_0, dgt1_0, dsh2_0, gp0 * da2_0, dgt2_0], axis=1),
        jnp.concatenate([dsh1_1, gm1 * da1_1, dgt1_1, dsh2_1, gp1 * da2_1, dgt2_1], axis=1)]).reshape(-1)
    dg_mix = jnp.concatenate([(1.0 + sc1) * da1_0, (1.0 + tc1) * da1_1], axis=1).reshape(-1)
    dg_mlp = jnp.concatenate([(1.0 + sc2) * da2_0, (1.0 + tc2) * da2_1], axis=1).reshape(-1)
    parts = [loss_part.reshape(-1), dmod, dg_mix, dg_mlp, dg_q.reshape(-1), dg_kv.reshape(-1), dsink.reshape(-1),
             dg_final.reshape(-1), db_qkv.reshape(-1), db_o.reshape(-1)]
    soffs = np.concatenate([[0], np.cumsum([p.size for p in parts])])
    spad = -(-int(soffs[-1]) // 1024) * 1024
    spack = jnp.pad(jnp.concatenate(parts), (0, spad - int(soffs[-1]))).reshape(8, spad // 8)
    sall = _all_gather("gather_small_grads", spack, F32)
    ssum = _sum_devices(sall).reshape(-1)
    tot = [ssum[int(soffs[i]):int(soffs[i + 1])] for i in range(len(parts))]
    loss = tot[0][0]
    nsink = swa_sinks.shape[1]
    small_g = dict(b_ada=tot[1].reshape(b_ada.shape), g_mix=tot[2].reshape(g_mix.shape), g_mlp=tot[3].reshape(g_mlp.shape),
                   mla_g_q=tot[4].reshape(mla_g_q.shape), mla_g_kv=tot[5].reshape(mla_g_kv.shape),
                   swa_sinks=tot[6][:nsink].reshape(swa_sinks.shape), g_final=tot[7].reshape(g_final.shape),
                   swa_b_qkv=lax.dynamic_slice(tot[8], (me * nbq,), (nbq,)).reshape(swa_b_qkv.shape),
                   swa_b_o=lax.dynamic_slice(tot[9], (me * nbo,), (nbo,)).reshape(swa_b_o.shape))
    small_w = dict(b_ada=b_ada, g_mix=g_mix, g_mlp=g_mlp, mla_g_q=mla_g_q, mla_g_kv=mla_g_kv, swa_sinks=swa_sinks,
                   g_final=g_final, swa_b_qkv=swa_b_qkv, swa_b_o=swa_b_o)
    small_m = dict(b_ada=m_b_ada, g_mix=m_g_mix, g_mlp=m_g_mlp, mla_g_q=m_mla_g_q, mla_g_kv=m_mla_g_kv,
                   swa_sinks=m_swa_sinks, g_final=m_g_final, swa_b_qkv=m_swa_b_qkv, swa_b_o=m_swa_b_o)
    small_v = dict(b_ada=v_b_ada, g_mix=v_g_mix, g_mlp=v_g_mlp, mla_g_q=v_mla_g_q, mla_g_kv=v_mla_g_kv,
                   swa_sinks=v_swa_sinks, g_final=v_g_final, swa_b_qkv=v_swa_b_qkv, swa_b_o=v_swa_b_o)
    SMALL = list(small_w)
    woffs = np.concatenate([[0], np.cumsum([small_w[n].size for n in SMALL])])
    wpad = -(-int(woffs[-1]) // 1024) * 1024

    def spack_of(d):
        flat = jnp.concatenate([d[n].reshape(-1) for n in SMALL])
        return jnp.pad(flat, (0, wpad - int(woffs[-1]))).reshape(8, wpad // 8)

    sm = _adamw_small(spack_of(small_w), spack_of(small_g), spack_of(small_m), spack_of(small_v))
    small_d, small_nm, small_nv = (
        {n: a.reshape(-1)[int(woffs[i]):int(woffs[i + 1])].reshape(small_w[n].shape) for i, n in enumerate(SMALL)}
        for a in sm)

    b_off = int(soffs[1])
    dmod_all = sall.reshape(N_DEV, -1)[:, b_off:b_off + L * N_DEV * NC].reshape(N_DEV, L, N_DEV * NC)
    dmod_cols = jnp.moveaxis(lax.dynamic_slice_in_dim(dmod_all, me * NC, NC, axis=2), 0, 1)
    ada_g, ada_d, ada_nm, ada_nv = _ada_bwd_adamw(c_all.T, dmod_cols, w_ada, m_w_ada, v_w_ada)

    order = ["w_ada", "b_ada", "g_mix", "g_mlp", "mla_w_dq", "mla_g_q", "mla_w_uq", "mla_w_dkv", "mla_g_kv",
             "mla_w_ukv", "mla_w_o", "swa_w_qkv", "swa_b_qkv", "swa_sinks", "swa_w_o", "swa_b_o", "w_ff1", "w_ff2", "g_final"]

    def collect(ada, big, small):
        return [ada if n == "w_ada" else (big[n] if n in big else small[n]) for n in order]

    return (loss, dx0.reshape(x.shape), *collect(ada_g, big_g, small_g), *collect(ada_d, big_d, small_d),
            *collect(ada_nm, big_nm, small_nm), *collect(ada_nv, big_nv, small_nv))
```

```python
import functools

import jax
import jax.numpy as jnp
import numpy as np
from jax import lax
from jax.experimental import pallas as pl
from jax.experimental.pallas import tpu as pltpu

F32 = jnp.float32
BF16 = jnp.bfloat16
MESH_IDS = pl.DeviceIdType.MESH
N_DEV = 8

MLA_HEADS = 8
QK_NOPE = 128
QK_ROPE = 64
QK_DIM = QK_NOPE + QK_ROPE
V_DIM = 128
KV_LORA = 256
ROPE_THETA = 10000.0
SWA_HEADS = 16
SWA_KV_HEADS = 4
SWA_GROUP = SWA_HEADS // SWA_KV_HEADS
SWA_HEAD_DIM = 64
WINDOW = 128
EPS = 1e-6
LOG2E = 1.4426950408889634

ADAM_LR = 0.001
ADAM_B1 = 0.9
ADAM_B2 = 0.999
ADAM_EPS = 1e-08
ADAM_WD = 0.01
ADAM_STEP = 10

PACK_COLS = 1024
RS_CHUNKS = 4
VMEM_LIMIT = 56 << 20
ROW_TILE = 512
ROW_TILE_WIDE = 256
ATT_TILE = 512


def _dot(a, b):
    return jnp.dot(a, b, preferred_element_type=F32)


def _dot_nt(a, b):
    return lax.dot_general(a, b, (((1,), (1,)), ((), ())), preferred_element_type=F32)


def _dot_tn(a, b):
    return lax.dot_general(a, b, (((0,), (0,)), ((), ())), preferred_element_type=F32)


def _rstd(x):
    return lax.rsqrt(jnp.mean(x * x, axis=-1, keepdims=True) + EPS)


def _rms_bwd(dn, n, r):
    return r * (dn - n * jnp.mean(dn * n, axis=-1, keepdims=True))


def _modulate(x, g, sc, sh):
    r = _rstd(x)
    return ((x * r) * g) * (1.0 + sc) + sh


def _modulate_bwd(dh, x, g, sc):
    r = _rstd(x)
    n = x * r
    dsh = jnp.sum(dh, axis=0, keepdims=True)
    da = jnp.sum(dh * n, axis=0, keepdims=True)
    dx = _rms_bwd(dh * (g * (1.0 + sc)), n, r)
    return dx, dsh, da


def _first(i):
    return i == 0


def _acc(ref, val, i):
    @pl.when(i == 0)
    def _():
        ref[...] = val

    @pl.when(i != 0)
    def _():
        ref[...] += val


def _row_spec(shape, tm):
    nd = len(shape)
    return pl.BlockSpec(tuple(shape[:nd - 2]) + (tm, shape[-1]), lambda i: (0,) * (nd - 2) + (i, 0))


def _resident_spec(shape, single_buffer):
    nd = len(shape)
    if single_buffer:
        return pl.BlockSpec(tuple(shape), lambda i: (0,) * nd, pipeline_mode=pl.Buffered(1))
    return pl.BlockSpec(tuple(shape), lambda i: (0,) * nd)


def _rowcall(name, body, tokens, tm, row_in, full_in, row_out, acc_out=()):
    tm = min(tm, tokens)
    in_specs = [_row_spec(a.shape, tm) for a in row_in] + [_resident_spec(a.shape, True) for a in full_in]
    out_specs = [_row_spec(s.shape, tm) for s in row_out] + [_resident_spec(s.shape, False) for s in acc_out]
    return pl.pallas_call(
        body, name=name, grid=(tokens // tm,), in_specs=in_specs, out_specs=out_specs,
        out_shape=list(row_out) + list(acc_out),
        compiler_params=pltpu.CompilerParams(dimension_semantics=("arbitrary",), vmem_limit_bytes=VMEM_LIMIT),
    )(*row_in, *full_in)


def _sds(shape, dtype):
    return jax.ShapeDtypeStruct(tuple(shape), dtype)


def _mla_in_fwd(x, cos, sin, g, sc, sh, w_cat, g_q, w_uqx, g_kv, w_ukv):
    S, D = x.shape
    QL = g_q.shape[1]
    H = MLA_HEADS

    def body(x_ref, cos_ref, sin_ref, g_ref, sc_ref, sh_ref, wcat_ref, gq_ref, wuqx_ref, gkv_ref, wukv_ref,
             h_ref, cqp_ref, cq_ref, ckvp_ref, ckv_ref, q_ref, k_ref, v_ref):
        cs, sn = cos_ref[...], sin_ref[...]
        hb = _modulate(x_ref[...], g_ref[...], sc_ref[...], sh_ref[...]).astype(BF16)
        h_ref[...] = hb
        low = _dot(hb, wcat_ref[...])
        cqp = low[:, :QL]
        cqp_ref[...] = cqp
        cq = ((cqp * _rstd(cqp)) * gq_ref[...]).astype(BF16)
        cq_ref[...] = cq
        ckvp = low[:, QL:QL + KV_LORA]
        ckvp_ref[...] = ckvp
        ckv = ((ckvp * _rstd(ckvp)) * gkv_ref[...]).astype(BF16)
        ckv_ref[...] = ckv
        o = QL + KV_LORA
        kr = (low[:, o:o + QK_ROPE] * cs + low[:, o + QK_ROPE:o + 2 * QK_ROPE] * sn).astype(BF16)
        qx = _dot(cq, wuqx_ref[...])
        kv = _dot(ckv, wukv_ref[...])
        for hd in range(H):
            b = hd * 256
            q_ref[hd, :, 0:QK_NOPE] = qx[:, b:b + QK_NOPE].astype(BF16)
            q_ref[hd, :, QK_NOPE:QK_DIM] = (qx[:, b + 128:b + 192] * cs + qx[:, b + 192:b + 256] * sn).astype(BF16)
            k_ref[hd, :, 0:QK_NOPE] = kv[:, b:b + QK_NOPE].astype(BF16)
            k_ref[hd, :, QK_NOPE:QK_DIM] = kr
            v_ref[hd, :, 0:V_DIM] = kv[:, b + 128:b + 256].astype(BF16)
            v_ref[hd, :, V_DIM:2 * V_DIM] = jnp.ones((x_ref.shape[0], V_DIM), BF16)

    return _rowcall(
        "mla_in_fwd", body, S, ROW_TILE, [x, cos, sin], [g, sc, sh, w_cat, g_q, w_uqx, g_kv, w_ukv],
        [_sds((S, D), BF16), _sds((S, QL), F32), _sds((S, QL), BF16), _sds((S, KV_LORA), F32), _sds((S, KV_LORA), BF16),
         _sds((H, S, QK_DIM), BF16), _sds((H, S, QK_DIM), BF16), _sds((H, S, 2 * V_DIM), BF16)])


def _mla_attn_fwd(q, k, v, t):
    H, S, DQ = q.shape
    DV = V_DIM
    t = min(t, S)
    nb = S // t
    scale = QK_DIM ** -0.5
    c2 = scale * LOG2E

    def body(q_ref, k_ref, v_ref, o_ref, lse_ref, m_s, acc_s):
        qi = pl.program_id(1)
        m_s[...] = jnp.full_like(m_s, -jnp.inf)
        acc_s[...] = jnp.zeros_like(acc_s)

        def tile(j, diagonal):
            rows = pl.ds(pl.multiple_of(j * t, t), t)
            s = _dot_nt(q_ref[0], k_ref[0, rows, :])
            if diagonal:
                row = lax.broadcasted_iota(jnp.int32, (t, t), 0)
                col = lax.broadcasted_iota(jnp.int32, (t, t), 1)
                s = jnp.where(col <= row, s, -jnp.inf)
            m_prev = m_s[...]
            m_new = jnp.maximum(m_prev, jnp.max(s, axis=-1, keepdims=True))
            alpha = jnp.exp2((m_prev - m_new) * c2)
            p = jnp.exp2((s - m_new) * c2)
            acc_s[...] = alpha * acc_s[...] + _dot(p.astype(BF16), v_ref[0, rows, :])
            m_s[...] = m_new

        def off_diagonal(j, carry):
            tile(j, False)
            return carry

        lax.fori_loop(0, qi, off_diagonal, 0)
        tile(qi, True)
        acc = acc_s[...]
        o_ref[...] = (acc[:, :DV] / acc[:, DV:]).astype(BF16)
        lse_ref[0] = m_s[...] * scale + jnp.log(acc[:, DV:DV + 1])

    return pl.pallas_call(
        body, name="mla_attn_fwd", grid=(H, nb),
        in_specs=[pl.BlockSpec((1, t, DQ), lambda h, i: (h, i, 0)),
                  pl.BlockSpec((1, S, DQ), lambda h, i: (h, 0, 0)),
                  pl.BlockSpec((1, S, 2 * DV), lambda h, i: (h, 0, 0))],
        out_specs=[pl.BlockSpec((t, DV), lambda h, i: (i, h)),
                   pl.BlockSpec((1, t, 1), lambda h, i: (h, i, 0))],
        out_shape=[_sds((S, H * DV), BF16), _sds((H, S, 1), F32)],
        scratch_shapes=[pltpu.VMEM((t, 1), F32), pltpu.VMEM((t, 2 * DV), F32)],
        compiler_params=pltpu.CompilerParams(dimension_semantics=("parallel", "arbitrary"),
                                             vmem_limit_bytes=VMEM_LIMIT),
    )(q, k, v)


def _attn_out_fwd(o, x, w_o, b_o, gt, g, sc, sh):
    S, D = x.shape

    def body(o_ref, x_ref, wo_ref, bo_ref, gt_ref, g_ref, sc_ref, sh_ref, y_ref, x1_ref, h_ref):
        y = _dot(o_ref[...], wo_ref[...]) + bo_ref[...]
        y_ref[...] = y
        x1 = x_ref[...] + gt_ref[...] * y
        x1_ref[...] = x1
        h_ref[...] = _modulate(x1, g_ref[...], sc_ref[...], sh_ref[...]).astype(BF16)

    return _rowcall("attn_out_fwd", body, S, ROW_TILE, [o, x], [w_o, b_o, gt, g, sc, sh],
                    [_sds((S, D), F32), _sds((S, D), F32), _sds((S, D), BF16)])


def _mlp_fwd(h, x, w1, w2, gt):
    S, D = x.shape
    FF = w1.shape[1]

    def body(h_ref, x_ref, w1_ref, w2_ref, gt_ref, rl_ref, act_ref, y_ref, x2_ref):
        rl = jnp.maximum(_dot(h_ref[...], w1_ref[...]), 0.0)
        rl_ref[...] = rl.astype(BF16)
        act = (rl * rl).astype(BF16)
        act_ref[...] = act
        y = _dot(act, w2_ref[...])
        y_ref[...] = y
        x2_ref[...] = x_ref[...] + gt_ref[...] * y

    return _rowcall("mlp_fwd", body, S, ROW_TILE_WIDE, [h, x], [w1, w2, gt],
                    [_sds((S, FF), BF16), _sds((S, FF), BF16), _sds((S, D), F32), _sds((S, D), F32)])


def _swa_in_fwd(x, g, sc, sh, w_qkv, b_qkv):
    S, D = x.shape
    NQ = SWA_HEADS * SWA_HEAD_DIM
    NK = SWA_KV_HEADS * SWA_HEAD_DIM

    def body(x_ref, g_ref, sc_ref, sh_ref, w_ref, b_ref, h_ref, q_ref, k_ref, v_ref):
        hb = _modulate(x_ref[...], g_ref[...], sc_ref[...], sh_ref[...]).astype(BF16)
        h_ref[...] = hb
        qkv = _dot(hb, w_ref[...]) + b_ref[...]
        q_ref[...] = qkv[:, :NQ].astype(BF16)
        k_ref[...] = qkv[:, NQ:NQ + NK].astype(BF16)
        v_ref[...] = qkv[:, NQ + NK:].astype(BF16)

    return _rowcall("swa_in_fwd", body, S, ROW_TILE, [x], [g, sc, sh, w_qkv, b_qkv],
                    [_sds((S, D), BF16), _sds((S, NQ), BF16), _sds((S, NK), BF16), _sds((S, NK), BF16)])


def _alibi_slope(head):
    return float(np.float32(2.0 ** (-8.0 * (head + 1) / SWA_HEADS)))


def _swa_geometry(n):
    W = WINDOW
    row = lax.broadcasted_iota(jnp.int32, (W, 2 * W), 0)
    col = lax.broadcasted_iota(jnp.int32, (W, 2 * W), 1)
    dist = W + row - col
    valid = (dist >= 0) & (dist < W) & ((n > 0) | (col >= W))
    return dist.astype(F32), valid


def _swa_band_specs(W, nb, cols):
    prev = pl.BlockSpec((W, cols), lambda n: (jnp.maximum(jnp.minimum(n, nb - 1) - 1, 0), 0))
    cur = pl.BlockSpec((W, cols), lambda n: (jnp.minimum(n, nb - 1), 0))
    return prev, cur


def _swa_attn_fwd(q, k, v, sinks):
    S, NQ = q.shape
    NK = k.shape[1]
    W, Dh, G = WINDOW, SWA_HEAD_DIM, SWA_GROUP
    nb = S // W

    def body(q_ref, kp_ref, kc_ref, vp_ref, vc_ref, sink_ref, o_ref, lse_ref):
        distf, valid = _swa_geometry(pl.program_id(0))
        for kh in range(SWA_KV_HEADS):
            ck = slice(kh * Dh, (kh + 1) * Dh)
            kb = jnp.concatenate([kp_ref[:, ck], kc_ref[:, ck]], axis=0)
            vb = jnp.concatenate([vp_ref[:, ck], vc_ref[:, ck]], axis=0)
            for gi in range(G):
                hq = kh * G + gi
                cq = slice(hq * Dh, (hq + 1) * Dh)
                s = _dot_nt(q_ref[:, cq], kb) * (Dh ** -0.5) - _alibi_slope(hq) * distf
                s = jnp.where(valid, s, -jnp.inf)
                sink = sink_ref[:, hq:hq + 1]
                m = jnp.maximum(jnp.max(s, axis=-1, keepdims=True), sink)
                p = jnp.exp(s - m)
                denom = jnp.sum(p, axis=-1, keepdims=True) + jnp.exp(sink - m)
                o_ref[:, cq] = _dot((p * (1.0 / denom)).astype(BF16), vb).astype(BF16)
                lse_ref[:, hq:hq + 1] = m + jnp.log(denom)

    kprev, kcur = _swa_band_specs(W, nb, NK)
    return pl.pallas_call(
        body, name="swa_attn_fwd", grid=(nb,),
        in_specs=[pl.BlockSpec((W, NQ), lambda n: (n, 0)), kprev, kcur, kprev, kcur,
                  pl.BlockSpec((1, SWA_HEADS), lambda n: (0, 0))],
        out_specs=[pl.BlockSpec((W, NQ), lambda n: (n, 0)), pl.BlockSpec((W, SWA_HEADS), lambda n: (n, 0))],
        out_shape=[_sds((S, NQ), BF16), _sds((S, SWA_HEADS), F32)],
        compiler_params=pltpu.CompilerParams(dimension_semantics=("arbitrary",), vmem_limit_bytes=VMEM_LIMIT),
    )(q, k, k, v, v, sinks)


def _final_loss(x, target, g):
    S, D = x.shape

    def body(x_ref, t_ref, g_ref, dx_ref, loss_ref, dg_ref):
        i = pl.program_id(0)
        xv = x_ref[...]
        r = _rstd(xv)
        n = xv * r
        err = n * g_ref[...] - t_ref[...]
        part = 0.5 * jnp.sum(jnp.mean(err * err, axis=-1, keepdims=True), axis=0, keepdims=True)
        _acc(loss_ref, jnp.broadcast_to(part, loss_ref.shape), i)
        dout = err / D
        _acc(dg_ref, jnp.sum(dout * n, axis=0, keepdims=True), i)
        dx_ref[...] = _rms_bwd(dout * g_ref[...], n, r)

    return _rowcall("final_loss", body, S, ROW_TILE, [x, target], [g], [_sds((S, D), F32)],
                    [_sds((1, 128), F32), _sds((1, D), F32)])


def _mlp_bwd_a(dx, y, rl, gt, w2):
    S, D = dx.shape
    FF = rl.shape[1]

    def body(dx_ref, y_ref, rl_ref, gt_ref, w2_ref, dy_ref, du_ref, dgt_ref):
        i = pl.program_id(0)
        dxv = dx_ref[...]
        _acc(dgt_ref, jnp.sum(dxv * y_ref[...], axis=0, keepdims=True), i)
        dy = (dxv * gt_ref[...]).astype(BF16)
        dy_ref[...] = dy
        dact = _dot_nt(dy, w2_ref[...])
        du_ref[...] = (dact * (2.0 * rl_ref[...].astype(F32))).astype(BF16)

    return _rowcall("mlp_bwd_a", body, S, ROW_TILE_WIDE, [dx, y, rl], [gt, w2],
                    [_sds((S, D), BF16), _sds((S, FF), BF16)], [_sds((1, D), F32)])


def _mlp_bwd_b(du, x, dx, w1, g, sc):
    S, D = x.shape

    def body(du_ref, x_ref, dx_ref, w1_ref, g_ref, sc_ref, dxo_ref, dsh_ref, da_ref):
        i = pl.program_id(0)
        dh = _dot_nt(du_ref[...], w1_ref[...])
        dxn, dsh, da = _modulate_bwd(dh, x_ref[...], g_ref[...], sc_ref[...])
        dxo_ref[...] = dx_ref[...] + dxn
        _acc(dsh_ref, dsh, i)
        _acc(da_ref, da, i)

    return _rowcall("mlp_bwd_b", body, S, ROW_TILE_WIDE, [du, x, dx], [w1, g, sc],
                    [_sds((S, D), F32)], [_sds((1, D), F32), _sds((1, D), F32)])


def _attn_out_bwd(dx, y, o, gt, w_o, n_heads):
    S, D = dx.shape
    NO = o.shape[1]
    dh = NO // n_heads

    def body(dx_ref, y_ref, o_ref, gt_ref, wo_ref, dy_ref, do_ref, dl_ref, dgt_ref, dbo_ref):
        i = pl.program_id(0)
        dxv = dx_ref[...]
        _acc(dgt_ref, jnp.sum(dxv * y_ref[...], axis=0, keepdims=True), i)
        dy = dxv * gt_ref[...]
        _acc(dbo_ref, jnp.sum(dy, axis=0, keepdims=True), i)
        dyb = dy.astype(BF16)
        dy_ref[...] = dyb
        do = _dot_nt(dyb, wo_ref[...])
        do_ref[...] = do.astype(BF16)
        prod = do * o_ref[...].astype(F32)
        for hd in range(n_heads):
            dl_ref[:, hd:hd + 1] = jnp.sum(prod[:, hd * dh:(hd + 1) * dh], axis=-1, keepdims=True)

    return _rowcall("attn_out_bwd", body, S, ROW_TILE, [dx, y, o], [gt, w_o],
                    [_sds((S, D), BF16), _sds((S, NO), BF16), _sds((S, n_heads), F32)],
                    [_sds((1, D), F32), _sds((1, D), F32)])


def _mla_attn_bwd(q, k, v, do, lse, delta, t):
    H, S, DQ = q.shape
    DV = V_DIM
    t = min(t, S)
    nb = S // t
    scale = QK_DIM ** -0.5
    c2 = scale * LOG2E

    def body(q_ref, k_ref, v_ref, do_ref, lse_ref, dl_ref, dq_ref, dk_ref, dv_ref, dk_s, dv_s):
        kj = pl.program_id(1)

        @pl.when(kj == 0)
        def _():
            dq_ref[...] = jnp.zeros_like(dq_ref)

        dk_s[...] = jnp.zeros_like(dk_s)
        dv_s[...] = jnp.zeros_like(dv_s)

        def tile(i, diagonal):
            rows = pl.ds(pl.multiple_of(i * t, t), t)
            qb, dob, kb = q_ref[0, rows, :], do_ref[rows, :], k_ref[0]
            p = jnp.exp2(_dot_nt(kb, qb) * c2 - lse_ref[0, i])
            if diagonal:
                key = lax.broadcasted_iota(jnp.int32, (t, t), 0)
                qry = lax.broadcasted_iota(jnp.int32, (t, t), 1)
                p = jnp.where(key <= qry, p, 0.0)
            dv_s[...] += _dot(p.astype(BF16), dob)
            dp = _dot_nt(v_ref[0], dob)
            ds = (p * (dp - dl_ref[0, i])).astype(BF16)
            dk_s[...] += _dot(ds, qb)
            dq_ref[0, rows, :] += _dot_tn(ds, kb)

        def off_diagonal(i, carry):
            tile(i, False)
            return carry

        tile(kj, True)
        lax.fori_loop(kj + 1, nb, off_diagonal, 0)
        dk_ref[0] = (dk_s[...] * scale).astype(BF16)
        dv_ref[0] = dv_s[...].astype(BF16)

    rowspec = pl.BlockSpec((1, nb, 1, t), lambda h, j: (h, 0, 0, 0))
    return pl.pallas_call(
        body, name="mla_attn_bwd", grid=(H, nb),
        in_specs=[pl.BlockSpec((1, S, DQ), lambda h, j: (h, 0, 0)),
                  pl.BlockSpec((1, t, DQ), lambda h, j: (h, j, 0)),
                  pl.BlockSpec((1, t, DV), lambda h, j: (h, j, 0)),
                  pl.BlockSpec((S, DV), lambda h, j: (0, h)), rowspec, rowspec],
        out_specs=[pl.BlockSpec((1, S, DQ), lambda h, j: (h, 0, 0)),
                   pl.BlockSpec((1, t, DQ), lambda h, j: (h, j, 0)),
                   pl.BlockSpec((1, t, DV), lambda h, j: (h, j, 0))],
        out_shape=[_sds((H, S, DQ), F32), _sds((H, S, DQ), BF16), _sds((H, S, DV), BF16)],
        scratch_shapes=[pltpu.VMEM((t, DQ), F32), pltpu.VMEM((t, DV), F32)],
        compiler_params=pltpu.CompilerParams(dimension_semantics=("parallel", "arbitrary"),
                                             vmem_limit_bytes=VMEM_LIMIT),
    )(q, k, v, do, lse, delta)


def _swa_attn_bwd(q, k, v, do, lse, delta, sinks):
    S, NQ = q.shape
    NK = k.shape[1]
    W, Dh, G = WINDOW, SWA_HEAD_DIM, SWA_GROUP
    nb = S // W

    def body(q_ref, kp_ref, kc_ref, vp_ref, vc_ref, do_ref, lse_ref, dl_ref, sink_ref,
             dq_ref, dk_ref, dv_ref, dsink_ref, dkc_s, dvc_s):
        n = pl.program_id(0)

        @pl.when(n == 0)
        def _():
            dkc_s[...] = jnp.zeros_like(dkc_s)
            dvc_s[...] = jnp.zeros_like(dvc_s)
            dsink_ref[...] = jnp.zeros_like(dsink_ref)

        @pl.when(n < nb)
        def _():
            distf, valid = _swa_geometry(n)
            for kh in range(SWA_KV_HEADS):
                ck = slice(kh * Dh, (kh + 1) * Dh)
                kb = jnp.concatenate([kp_ref[:, ck], kc_ref[:, ck]], axis=0)
                vb = jnp.concatenate([vp_ref[:, ck], vc_ref[:, ck]], axis=0)
                dkb = jnp.zeros((2 * W, Dh), F32)
                dvb = jnp.zeros((2 * W, Dh), F32)
                for gi in range(G):
                    hq = kh * G + gi
                    cq = slice(hq * Dh, (hq + 1) * Dh)
                    qh, doh = q_ref[:, cq], do_ref[:, cq]
                    lse_h = lse_ref[:, hq:hq + 1]
                    dl_h = dl_ref[:, hq:hq + 1]
                    s = _dot_nt(qh, kb) * (Dh ** -0.5) - _alibi_slope(hq) * distf
                    p = jnp.where(valid, jnp.exp(s - lse_h), 0.0)
                    dvb = dvb + _dot_tn(p.astype(BF16), doh)
                    dp = _dot_nt(doh, vb)
                    dsb = ((p * (dp - dl_h)) * (Dh ** -0.5)).astype(BF16)
                    dq_ref[:, cq] = _dot(dsb, kb).astype(BF16)
                    dkb = dkb + _dot_tn(dsb, qh)
                    psink = jnp.exp(sink_ref[:, hq:hq + 1] - lse_h)
                    dsink_ref[:, hq:hq + 1] += -jnp.sum(psink * dl_h, axis=0, keepdims=True)
                dk_ref[:, ck] = (dkc_s[:, ck] + dkb[:W]).astype(BF16)
                dv_ref[:, ck] = (dvc_s[:, ck] + dvb[:W]).astype(BF16)
                dkc_s[:, ck] = dkb[W:]
                dvc_s[:, ck] = dvb[W:]

        @pl.when(n == nb)
        def _():
            dk_ref[...] = dkc_s[...].astype(BF16)
            dv_ref[...] = dvc_s[...].astype(BF16)

    kprev, kcur = _swa_band_specs(W, nb, NK)
    qspec = lambda cols: pl.BlockSpec((W, cols), lambda n: (jnp.minimum(n, nb - 1), 0))
    kvout = pl.BlockSpec((W, NK), lambda n: (jnp.maximum(n - 1, 0), 0))
    return pl.pallas_call(
        body, name="swa_attn_bwd", grid=(nb + 1,),
        in_specs=[qspec(NQ), kprev, kcur, kprev, kcur, qspec(NQ), qspec(SWA_HEADS), qspec(SWA_HEADS),
                  pl.BlockSpec((1, SWA_HEADS), lambda n: (0, 0))],
        out_specs=[qspec(NQ), kvout, kvout, pl.BlockSpec((1, 128), lambda n: (0, 0))],
        out_shape=[_sds((S, NQ), BF16), _sds((S, NK), BF16), _sds((S, NK), BF16), _sds((1, 128), F32)],
        scratch_shapes=[pltpu.VMEM((W, NK), F32), pltpu.VMEM((W, NK), F32)],
        compiler_params=pltpu.CompilerParams(dimension_semantics=("arbitrary",), vmem_limit_bytes=VMEM_LIMIT),
    )(q, k, k, v, v, do, lse, delta, sinks)


def _swa_in_bwd(dq, dk, dv, x, dx, w_qkv, g, sc):
    S, D = x.shape
    N = w_qkv.shape[1]

    def body(dq_ref, dk_ref, dv_ref, x_ref, dx_ref, w_ref, g_ref, sc_ref, dqkv_ref, dxo_ref, db_ref, dsh_ref, da_ref):
        i = pl.program_id(0)
        dqkv = jnp.concatenate([dq_ref[...], dk_ref[...], dv_ref[...]], axis=1)
        dqkv_ref[...] = dqkv
        _acc(db_ref, jnp.sum(dqkv.astype(F32), axis=0, keepdims=True), i)
        dh = _dot_nt(dqkv, w_ref[...])
        dxn, dsh, da = _modulate_bwd(dh, x_ref[...], g_ref[...], sc_ref[...])
        dxo_ref[...] = dx_ref[...] + dxn
        _acc(dsh_ref, dsh, i)
        _acc(da_ref, da, i)

    return _rowcall("swa_in_bwd", body, S, ROW_TILE, [dq, dk, dv, x, dx], [w_qkv, g, sc],
                    [_sds((S, N), BF16), _sds((S, D), F32)],
                    [_sds((1, N), F32), _sds((1, D), F32), _sds((1, D), F32)])


def _mla_in_bwd(dq, dk, dv, cos, sin, cqp, ckvp, x, dx, w_uqx, g_q, w_ukv, g_kv, w_cat, g, sc):
    S, D = x.shape
    H = MLA_HEADS
    QL = g_q.shape[1]
    NX = w_uqx.shape[1]
    NC = w_cat.shape[1]

    def body(dq_ref, dk_ref, dv_ref, cos_ref, sin_ref, cqp_ref, ckvp_ref, x_ref, dx_ref,
             wuqx_ref, gq_ref, wukv_ref, gkv_ref, wcat_ref, g_ref, sc_ref,
             dqx_ref, dkv_ref, dcat_ref, dxo_ref, dgq_ref, dgkv_ref, dsh_ref, da_ref):
        i = pl.program_id(0)
        cs, sn = cos_ref[...], sin_ref[...]
        dkr = jnp.zeros(cs.shape, F32)
        for hd in range(H):
            b = hd * 256
            dqh = dq_ref[hd] * (QK_DIM ** -0.5)
            dqx_ref[:, b:b + QK_NOPE] = dqh[:, :QK_NOPE].astype(BF16)
            dqx_ref[:, b + 128:b + 192] = (dqh[:, QK_NOPE:] * cs).astype(BF16)
            dqx_ref[:, b + 192:b + 256] = (dqh[:, QK_NOPE:] * sn).astype(BF16)
            dkh = dk_ref[hd]
            dkv_ref[:, b:b + QK_NOPE] = dkh[:, :QK_NOPE]
            dkv_ref[:, b + 128:b + 256] = dv_ref[hd]
            dkr = dkr + dkh[:, QK_NOPE:].astype(F32)
        dcq = _dot_nt(dqx_ref[...], wuqx_ref[...])
        cqp = cqp_ref[...]
        rq = _rstd(cqp)
        nq = cqp * rq
        _acc(dgq_ref, jnp.sum(dcq * nq, axis=0, keepdims=True), i)
        dcqp = _rms_bwd(dcq * gq_ref[...], nq, rq)
        dckv = _dot_nt(dkv_ref[...], wukv_ref[...])
        ckvp = ckvp_ref[...]
        rk = _rstd(ckvp)
        nk = ckvp * rk
        _acc(dgkv_ref, jnp.sum(dckv * nk, axis=0, keepdims=True), i)
        dckvp = _rms_bwd(dckv * gkv_ref[...], nk, rk)
        dcat_ref[:, :QL] = dcqp.astype(BF16)
        dcat_ref[:, QL:QL + KV_LORA] = dckvp.astype(BF16)
        o = QL + KV_LORA
        dcat_ref[:, o:o + QK_ROPE] = (dkr * cs).astype(BF16)
        dcat_ref[:, o + QK_ROPE:o + 2 * QK_ROPE] = (dkr * sn).astype(BF16)
        dh = _dot_nt(dcat_ref[...], wcat_ref[...])
        dxn, dsh, da = _modulate_bwd(dh, x_ref[...], g_ref[...], sc_ref[...])
        dxo_ref[...] = dx_ref[...] + dxn
        _acc(dsh_ref, dsh, i)
        _acc(da_ref, da, i)

    return _rowcall("mla_in_bwd", body, S, ROW_TILE_WIDE, [dq, dk, dv, cos, sin, cqp, ckvp, x, dx],
                    [w_uqx, g_q, w_ukv, g_kv, w_cat, g, sc],
                    [_sds((S, NX), BF16), _sds((S, NX), BF16), _sds((S, NC), BF16), _sds((S, D), F32)],
                    [_sds((1, QL), F32), _sds((1, KV_LORA), F32), _sds((1, D), F32), _sds((1, D), F32)])


def _matmul_tn(name, a, b):
    S, K = a.shape
    N = b.shape[1]
    tk, tn, ts = min(K, 1024), min(N, 1024), min(S, ROW_TILE)
    if N % tn:
        tn = 512 if N % 512 == 0 else (384 if N % 384 == 0 else 128)
    if K % tk:
        tk = 512 if K % 512 == 0 else (384 if K % 384 == 0 else 128)
    ns = S // ts

    def body(a_ref, b_ref, o_ref):
        _acc(o_ref, _dot_tn(a_ref[...], b_ref[...]), pl.program_id(2))

    return pl.pallas_call(
        body, name=name, grid=(K // tk, N // tn, ns),
        in_specs=[pl.BlockSpec((ts, tk), lambda i, j, s: (s, i)), pl.BlockSpec((ts, tn), lambda i, j, s: (s, j))],
        out_specs=pl.BlockSpec((tk, tn), lambda i, j, s: (i, j)),
        out_shape=_sds((K, N), F32),
        compiler_params=pltpu.CompilerParams(dimension_semantics=("parallel", "parallel", "arbitrary"),
                                             vmem_limit_bytes=VMEM_LIMIT),
    )(a, b)


def _silu(c):
    return c * jax.nn.sigmoid(c)


def _ada_fwd(c_all, w_ada):
    L, D, NC = w_ada.shape

    def body(c_ref, w_ref, o_ref):
        cond = _silu(c_ref[...]).astype(BF16)
        o_ref[0] = _dot(cond, w_ref[0].astype(BF16))

    return pl.pallas_call(
        body, name="ada_fwd", grid=(L,),
        in_specs=[pl.BlockSpec(c_all.shape, lambda l: (0, 0)), pl.BlockSpec((1, D, NC), lambda l: (l, 0, 0))],
        out_specs=pl.BlockSpec((1, N_DEV, NC), lambda l: (l, 0, 0)),
        out_shape=_sds((L, N_DEV, NC), F32),
        compiler_params=pltpu.CompilerParams(dimension_semantics=("arbitrary",), vmem_limit_bytes=VMEM_LIMIT),
    )(c_all, w_ada)


def _adamw(w, g, m, v):
    m = ADAM_B1 * m + (1.0 - ADAM_B1) * g
    v = ADAM_B2 * v + (1.0 - ADAM_B2) * (g * g)
    m_hat = m / (1.0 - ADAM_B1 ** ADAM_STEP)
    v_hat = v / (1.0 - ADAM_B2 ** ADAM_STEP)
    delta = -ADAM_LR * (m_hat / (jnp.sqrt(v_hat) + ADAM_EPS) + ADAM_WD * w)
    return delta, m, v


def _ada_bwd_adamw(c_all_t, dmod_cols, w, m, v):
    L, D, NC = w.shape
    tr = min(D, 256)

    def body(ct_ref, dm_ref, w_ref, m_ref, v_ref, g_ref, d_ref, mo_ref, vo_ref):
        cond_t = _silu(ct_ref[...])
        dm = dm_ref[0]
        g = cond_t[:, 0:1] * dm[0:1, :]
        for b in range(1, N_DEV):
            g = g + cond_t[:, b:b + 1] * dm[b:b + 1, :]
        g_ref[0] = g
        d_ref[0], mo_ref[0], vo_ref[0] = _adamw(w_ref[0], g, m_ref[0], v_ref[0])

    wspec = pl.BlockSpec((1, tr, NC), lambda l, r: (l, r, 0))
    return pl.pallas_call(
        body, name="ada_bwd_adamw", grid=(L, D // tr),
        in_specs=[pl.BlockSpec((tr, N_DEV), lambda l, r: (r, 0)),
                  pl.BlockSpec((1, N_DEV, NC), lambda l, r: (l, 0, 0)), wspec, wspec, wspec],
        out_specs=[wspec] * 4, out_shape=[_sds(w.shape, F32)] * 4,
        compiler_params=pltpu.CompilerParams(dimension_semantics=("parallel", "parallel"), vmem_limit_bytes=VMEM_LIMIT),
    )(c_all_t, dmod_cols, w, m, v)


def _sum_devices(x):
    def body(x_ref, o_ref):
        s = x_ref[0]
        for j in range(1, N_DEV):
            s = s + x_ref[j]
        o_ref[...] = s

    return pl.pallas_call(body, name="sum_devices", out_shape=_sds(x.shape[1:], F32))(x)


def _adamw_small(w, g, m, v):
    def body(w_ref, g_ref, m_ref, v_ref, d_ref, mo_ref, vo_ref):
        d_ref[...], mo_ref[...], vo_ref[...] = _adamw(w_ref[...], g_ref[...], m_ref[...], v_ref[...])

    return pl.pallas_call(body, name="adamw_small", out_shape=[_sds(w.shape, F32)] * 3)(w, g, m, v)


def _me():
    return lax.axis_index("x") * 4 + lax.axis_index("y") * 2 + lax.axis_index("c")


def _peer(k):
    x, y, c = lax.axis_index("x"), lax.axis_index("y"), lax.axis_index("c")
    px = 1 - x if k & 4 else x
    py = 1 - y if k & 2 else y
    pc = 1 - c if k & 1 else c
    return (px, py, pc), px * 4 + py * 2 + pc


VMEM_SPEC = pl.BlockSpec(memory_space=pltpu.VMEM)
ANY_SPEC = pl.BlockSpec(memory_space=pl.ANY)


def _all_gather(name, x, out_dtype):
    R, C = x.shape
    cast = out_dtype != x.dtype

    def body(x_ref, out_ref, buf, send_sems, recv_sems, local_sem):
        me = _me()
        if cast:
            buf[...] = x_ref[...].astype(out_dtype)
            src = buf
        else:
            src = x_ref
        local = pltpu.make_async_copy(src, out_ref.at[me], local_sem)
        local.start()
        sends = []
        for k in range(1, N_DEV):
            dev, _ = _peer(k)
            cp = pltpu.make_async_remote_copy(src_ref=src, dst_ref=out_ref.at[me], send_sem=send_sems.at[k - 1],
                                              recv_sem=recv_sems.at[k - 1], device_id=dev, device_id_type=MESH_IDS)
            cp.start()
            sends.append(cp)
        for k in range(1, N_DEV):
            dev, pj = _peer(k)
            pltpu.make_async_remote_copy(src_ref=src, dst_ref=out_ref.at[pj], send_sem=send_sems.at[k - 1],
                                         recv_sem=recv_sems.at[k - 1], device_id=dev, device_id_type=MESH_IDS).wait_recv()
        for cp in sends:
            cp.wait_send()
        local.wait()

    return pl.pallas_call(
        body, name=name, in_specs=[VMEM_SPEC], out_specs=ANY_SPEC, out_shape=_sds((N_DEV, R, C), out_dtype),
        scratch_shapes=[pltpu.VMEM((R, C) if cast else (8, 128), out_dtype),
                        pltpu.SemaphoreType.DMA((N_DEV - 1,)), pltpu.SemaphoreType.DMA((N_DEV - 1,)),
                        pltpu.SemaphoreType.DMA(())],
        compiler_params=pltpu.CompilerParams(vmem_limit_bytes=VMEM_LIMIT),
    )(x)


def _all_to_all(name, x):
    _, R, C = x.shape

    def body(x_ref, out_ref, send_sems, recv_sems, local_sem):
        me = _me()
        local = pltpu.make_async_copy(x_ref.at[me], out_ref.at[me], local_sem)
        local.start()
        sends = []
        for k in range(1, N_DEV):
            dev, pj = _peer(k)
            cp = pltpu.make_async_remote_copy(src_ref=x_ref.at[pj], dst_ref=out_ref.at[me], send_sem=send_sems.at[k - 1],
                                              recv_sem=recv_sems.at[k - 1], device_id=dev, device_id_type=MESH_IDS)
            cp.start()
            sends.append(cp)
        for k in range(1, N_DEV):
            dev, pj = _peer(k)
            pltpu.make_async_remote_copy(src_ref=x_ref.at[pj], dst_ref=out_ref.at[pj], send_sem=send_sems.at[k - 1],
                                         recv_sem=recv_sems.at[k - 1], device_id=dev, device_id_type=MESH_IDS).wait_recv()
        for cp in sends:
            cp.wait_send()
        local.wait()

    return pl.pallas_call(
        body, name=name, in_specs=[VMEM_SPEC], out_specs=VMEM_SPEC, out_shape=_sds(x.shape, x.dtype),
        scratch_shapes=[pltpu.SemaphoreType.DMA((N_DEV - 1,)), pltpu.SemaphoreType.DMA((N_DEV - 1,)),
                        pltpu.SemaphoreType.DMA(())],
    )(x)


def _reduce_scatter_adamw(name, gblk, w, m, v):
    _, R, C = gblk.shape
    rows = 8
    for cand in (136, 128, 80, 64, 40, 32, 16, 8):
        if R % cand == 0:
            rows = cand
            break

    def body(g_ref, w_ref, m_ref, v_ref, go_ref, d_ref, mo_ref, vo_ref, recv, send_sems, recv_sems, local_sem):
        me = _me()
        local = pltpu.make_async_copy(g_ref.at[me], recv.at[me], local_sem)
        local.start()
        sends = []
        for k in range(1, N_DEV):
            dev, pj = _peer(k)
            cp = pltpu.make_async_remote_copy(src_ref=g_ref.at[pj], dst_ref=recv.at[me], send_sem=send_sems.at[k - 1],
                                              recv_sem=recv_sems.at[k - 1], device_id=dev, device_id_type=MESH_IDS)
            cp.start()
            sends.append(cp)
        for k in range(1, N_DEV):
            dev, pj = _peer(k)
            pltpu.make_async_remote_copy(src_ref=g_ref.at[pj], dst_ref=recv.at[pj], send_sem=send_sems.at[k - 1],
                                         recv_sem=recv_sems.at[k - 1], device_id=dev, device_id_type=MESH_IDS).wait_recv()
        local.wait()

        def chunk(i, carry):
            r = pl.ds(pl.multiple_of(i * rows, rows), rows)
            g = recv[0, r, :].astype(F32)
            for j in range(1, N_DEV):
                g = g + recv[j, r, :].astype(F32)
            go_ref[r, :] = g
            d_ref[r, :], mo_ref[r, :], vo_ref[r, :] = _adamw(w_ref[r, :], g, m_ref[r, :], v_ref[r, :])
            return carry

        lax.fori_loop(0, R // rows, chunk, 0)
        for cp in sends:
            cp.wait_send()

    return pl.pallas_call(
        body, name=name, in_specs=[ANY_SPEC, VMEM_SPEC, VMEM_SPEC, VMEM_SPEC], out_specs=[VMEM_SPEC] * 4,
        out_shape=[_sds((R, C), F32)] * 4,
        scratch_shapes=[pltpu.VMEM((N_DEV, R, C), BF16), pltpu.SemaphoreType.DMA((N_DEV - 1,)),
                        pltpu.SemaphoreType.DMA((N_DEV - 1,)), pltpu.SemaphoreType.DMA(())],
        compiler_params=pltpu.CompilerParams(vmem_limit_bytes=VMEM_LIMIT),
    )(gblk, w, m, v)


BIG = ["mla_w_dq", "mla_w_uq", "mla_w_dkv", "mla_w_ukv", "mla_w_o", "swa_w_qkv", "swa_w_o", "w_ff1", "w_ff2"]
ROW_SHARDED = {"mla_w_dq", "mla_w_dkv", "mla_w_o", "swa_w_o", "w_ff2"}


def _unblock(name, blocks):
    sh = blocks.shape[1:]
    if name in ROW_SHARDED:
        return jnp.moveaxis(blocks, 0, 1).reshape(sh[0], N_DEV * sh[1], sh[2])
    return jnp.moveaxis(blocks, 0, 2).reshape(sh[0], sh[1], N_DEV * sh[2])


def _block(name, full):
    L, K, N = full.shape
    if name in ROW_SHARDED:
        return jnp.moveaxis(full.reshape(L, N_DEV, K // N_DEV, N), 1, 0)
    return jnp.moveaxis(full.reshape(L, K, N_DEV, N // N_DEV), 2, 0)


def _rot_cols(w):
    half = QK_ROPE // 2
    return jnp.concatenate([-w[..., half:], w[..., :half]], axis=-1)


def _unrot_cols(gw):
    half = QK_ROPE // 2
    return jnp.concatenate([gw[..., half:], -gw[..., :half]], axis=-1)


def _row(v):
    return v.reshape(1, -1)


def _mlp_block_bwd(dx, sv, w1, w2, g, sc, gt):
    dy, du, dgt = _mlp_bwd_a(dx, sv["y2"], sv["rl"], gt, w2)
    dw2 = _matmul_tn("dw_ff2", sv["act"], dy)
    dw1 = _matmul_tn("dw_ff1", sv["h2"], du)
    dxo, dsh, da = _mlp_bwd_b(du, sv["x1"], dx, w1, g, sc)
    return dxo, dw1, dw2, dsh, da, dgt


def kernel(x, c, positions, w_ada, b_ada, g_mix, g_mlp, mla_w_dq, mla_g_q, mla_w_uq, mla_w_dkv, mla_g_kv, mla_w_ukv, mla_w_o, swa_w_qkv, swa_b_qkv, swa_sinks, swa_w_o, swa_b_o, w_ff1, w_ff2, g_final, loss_target, m_w_ada, m_b_ada, m_g_mix, m_g_mlp, m_mla_w_dq, m_mla_g_q, m_mla_w_uq, m_mla_w_dkv, m_mla_g_kv, m_mla_w_ukv, m_mla_w_o, m_swa_w_qkv, m_swa_b_qkv, m_swa_sinks, m_swa_w_o, m_swa_b_o, m_w_ff1, m_w_ff2, m_g_final, v_w_ada, v_b_ada, v_g_mix, v_g_mlp, v_mla_w_dq, v_mla_g_q, v_mla_w_uq, v_mla_w_dkv, v_mla_g_kv, v_mla_w_ukv, v_mla_w_o, v_swa_w_qkv, v_swa_b_qkv, v_swa_sinks, v_swa_w_o, v_swa_b_o, v_w_ff1, v_w_ff2, v_g_final):
    S, D = x.shape[1], x.shape[2]
    me = _me()
    x0 = x[0]
    target = loss_target[0]
    big_w = dict(mla_w_dq=mla_w_dq, mla_w_uq=mla_w_uq, mla_w_dkv=mla_w_dkv, mla_w_ukv=mla_w_ukv, mla_w_o=mla_w_o,
                 swa_w_qkv=swa_w_qkv, swa_w_o=swa_w_o, w_ff1=w_ff1, w_ff2=w_ff2)
    big_m = dict(mla_w_dq=m_mla_w_dq, mla_w_uq=m_mla_w_uq, mla_w_dkv=m_mla_w_dkv, mla_w_ukv=m_mla_w_ukv,
                 mla_w_o=m_mla_w_o, swa_w_qkv=m_swa_w_qkv, swa_w_o=m_swa_w_o, w_ff1=m_w_ff1, w_ff2=m_w_ff2)
    big_v = dict(mla_w_dq=v_mla_w_dq, mla_w_uq=v_mla_w_uq, mla_w_dkv=v_mla_w_dkv, mla_w_ukv=v_mla_w_ukv,
                 mla_w_o=v_mla_w_o, swa_w_qkv=v_swa_w_qkv, swa_w_o=v_swa_w_o, w_ff1=v_w_ff1, w_ff2=v_w_ff2)
    sizes = [big_w[n].size for n in BIG]
    offs = np.concatenate([[0], np.cumsum(sizes)])
    total = int(offs[-1])
    pack_rows = -(-total // (PACK_COLS * RS_CHUNKS * 16)) * RS_CHUNKS * 16

    def pack(parts):
        flat = jnp.concatenate([p.reshape(-1) for p in parts])
        return jnp.pad(flat, (0, pack_rows * PACK_COLS - total)).reshape(pack_rows, PACK_COLS)

    def unpack(flat2d, lead=()):
        flat = flat2d.reshape(lead + (-1,))
        return {n: flat[..., int(offs[i]):int(offs[i + 1])].reshape(lead + big_w[n].shape) for i, n in enumerate(BIG)}

    gathered = _all_gather("gather_weights", pack([big_w[n] for n in BIG]), BF16)
    wfull = {n: _unblock(n, b) for n, b in unpack(gathered, (N_DEV,)).items()}
    w_dq, w_dkv = wfull["mla_w_dq"][0], wfull["mla_w_dkv"][0]
    w_cat = jnp.concatenate([w_dq, w_dkv, _rot_cols(w_dkv[:, KV_LORA:])], axis=1)
    QL = w_dq.shape[1]
    w_uq = wfull["mla_w_uq"][0].reshape(QL, MLA_HEADS, QK_DIM)
    w_uqx = jnp.concatenate([w_uq, _rot_cols(w_uq[..., QK_NOPE:])], axis=-1).reshape(QL, MLA_HEADS * 256)
    w_ukv = wfull["mla_w_ukv"][0]
    w_o_mla, w_qkv, w_o_swa = wfull["mla_w_o"][0], wfull["swa_w_qkv"][0], wfull["swa_w_o"][0]
    ff1, ff2 = wfull["w_ff1"], wfull["w_ff2"]

    L = w_ada.shape[0]
    NC = w_ada.shape[2]
    nbq, nbo = swa_b_qkv.shape[1], swa_b_o.shape[1]
    cpad = -(-(D + nbq + nbo) // 1024) * 1024
    cpack = jnp.pad(jnp.concatenate([c[0], swa_b_qkv[0], swa_b_o[0]]), (0, cpad - (D + nbq + nbo))).reshape(8, cpad // 8)
    call = _all_gather("gather_c", cpack, F32).reshape(N_DEV, cpad)
    c_all = call[:, :D]
    b_qkv_full = call[:, D:D + nbq].reshape(1, N_DEV * nbq)
    b_o_full = call[:, D + nbq:D + nbq + nbo].reshape(1, N_DEV * nbo)
    mod_cols = _ada_fwd(c_all, w_ada)
    mpad = -(-(L * NC) // 1024) * 1024
    mod_send = jnp.pad(jnp.moveaxis(mod_cols, 1, 0).reshape(N_DEV, L * NC), ((0, 0), (0, mpad - L * NC)))
    mod_mine = _all_to_all("exchange_mod", mod_send.reshape(N_DEV, 8, mpad // 8)).reshape(N_DEV, mpad)[:, :L * NC]
    mod = jnp.moveaxis(mod_mine.reshape(N_DEV, L, NC), 0, 1).reshape(L, N_DEV * NC) + b_ada
    mods = mod.reshape(L, 6, 1, D)

    half = QK_ROPE // 2
    inv_freq = ROPE_THETA ** (-jnp.arange(half, dtype=F32) / half)
    ang = positions[0].astype(F32)[:, None] * inv_freq
    cos = jnp.concatenate([jnp.cos(ang), jnp.cos(ang)], axis=-1)
    sin = jnp.concatenate([jnp.sin(ang), jnp.sin(ang)], axis=-1)

    T_ATT = ATT_TILE
    zero_bias = jnp.zeros((1, D), F32)

    sh1, sc1, gt1, sh2, sc2, gt2 = [mods[0, i] for i in range(6)]
    gm0, gp0 = _row(g_mix[0]), _row(g_mlp[0])
    h1, cqp, cq, ckvp, ckv, q, k, v = _mla_in_fwd(x0, cos, sin, gm0, sc1, sh1, w_cat, mla_g_q, w_uqx, mla_g_kv, w_ukv)
    o0, lse0 = _mla_attn_fwd(q, k, v, T_ATT)
    y1, x1, h2 = _attn_out_fwd(o0, x0, w_o_mla, zero_bias, gt1, gp0, sc2, sh2)
    rl0, act0, y2, x2 = _mlp_fwd(h2, x1, ff1[0], ff2[0], gt2)
    sv0 = dict(y2=y2, rl=rl0, act=act0, h2=h2, x1=x1)

    th1, tc1, tg1, th2, tc2, tg2 = [mods[1, i] for i in range(6)]
    gm1, gp1 = _row(g_mix[1]), _row(g_mlp[1])
    h3, sq, sk, svv = _swa_in_fwd(x2, gm1, tc1, th1, w_qkv, b_qkv_full)
    o1, lse1 = _swa_attn_fwd(sq, sk, svv, swa_sinks)
    y3, x3, h4 = _attn_out_fwd(o1, x2, w_o_swa, b_o_full, tg1, gp1, tc2, th2)
    rl1, act1, y4, x4 = _mlp_fwd(h4, x3, ff1[1], ff2[1], tg2)
    sv1 = dict(y2=y4, rl=rl1, act=act1, h2=h4, x1=x3)
    dx4, loss_part, dg_final = _final_loss(x4, target, _row(g_final))

    dx3, dw1_1, dw2_1, dsh2_1, da2_1, dgt2_1 = _mlp_block_bwd(dx4, sv1, ff1[1], ff2[1], gp1, tc2, tg2)
    dy, do, dl, dgt1_1, db_o = _attn_out_bwd(dx3, y3, o1, tg1, w_o_swa, SWA_HEADS)
    dw_o_swa = _matmul_tn("dw_o", o1, dy)
    dsq, dsk, dsv, dsink = _swa_attn_bwd(sq, sk, svv, do, lse1, dl, swa_sinks)
    dqkv, dx2, db_qkv, dsh1_1, da1_1 = _swa_in_bwd(dsq, dsk, dsv, x2, dx3, w_qkv, gm1, tc1)
    dw_qkv = _matmul_tn("dw_qkv", h3, dqkv)

    dx1, dw1_0, dw2_0, dsh2_0, da2_0, dgt2_0 = _mlp_block_bwd(dx2, sv0, ff1[0], ff2[0], gp0, sc2, gt2)
    dy, do, dl, dgt1_0, _ = _attn_out_bwd(dx1, y1, o0, gt1, w_o_mla, MLA_HEADS)
    dw_o_mla = _matmul_tn("dw_o", o0, dy)
    tb = min(T_ATT, S)
    delta = dl.T.reshape(MLA_HEADS, S // tb, 1, tb)
    lse_rows = (lse0 * LOG2E).reshape(MLA_HEADS, S // tb, 1, tb)
    dq, dk, dv = _mla_attn_bwd(q, k, v, do, lse_rows, delta, T_ATT)
    dqx, dkv, dcat, dx0, dg_q, dg_kv, dsh1_0, da1_0 = _mla_in_bwd(
        dq, dk, dv, cos, sin, cqp, ckvp, x0, dx1, w_uqx, mla_g_q, w_ukv, mla_g_kv, w_cat, gm0, sc1)
    dw_uqx = _matmul_tn("dw_uq", cq, dqx).reshape(QL, MLA_HEADS, 256)
    dw_ukv = _matmul_tn("dw_ukv", ckv, dkv)
    dw_cat = _matmul_tn("dw_down", h1, dcat)
    dw_uq = jnp.concatenate([dw_uqx[..., :QK_NOPE], dw_uqx[..., 128:192] + _unrot_cols(dw_uqx[..., 192:256])],
                            axis=-1).reshape(QL, MLA_HEADS * QK_DIM)
    o_kr = QL + KV_LORA
    dw_dkv = jnp.concatenate([dw_cat[:, QL:o_kr],
                              dw_cat[:, o_kr:o_kr + QK_ROPE] + _unrot_cols(dw_cat[:, o_kr + QK_ROPE:])], axis=1)

    gfull = dict(mla_w_dq=dw_cat[None, :, :QL], mla_w_uq=dw_uq[None], mla_w_dkv=dw_dkv[None], mla_w_ukv=dw_ukv[None],
                 mla_w_o=dw_o_mla[None], swa_w_qkv=dw_qkv[None], swa_w_o=dw_o_swa[None],
                 w_ff1=jnp.stack([dw1_0, dw1_1]), w_ff2=jnp.stack([dw2_0, dw2_1]))
    gflat = jnp.concatenate([_block(n, gfull[n]).astype(BF16).reshape(N_DEV, -1) for n in BIG], axis=1)
    gpack = jnp.pad(gflat, ((0, 0), (0, pack_rows * PACK_COLS - total))).reshape(N_DEV, pack_rows, PACK_COLS)
    wpack, mpack, vpack = (pack([d[n] for n in BIG]) for d in (big_w, big_m, big_v))
    rc = pack_rows // RS_CHUNKS
    outs = [_reduce_scatter_adamw("grad_exchange_adamw", gpack[:, i * rc:(i + 1) * rc], wpack[i * rc:(i + 1) * rc],
                                  mpack[i * rc:(i + 1) * rc], vpack[i * rc:(i + 1) * rc]) for i in range(RS_CHUNKS)]
    big_g, big_d, big_nm, big_nv = (unpack(jnp.concatenate([o[j] for o in outs], axis=0)) for j in range(4))

    dmod = jnp.stack([
        jnp.concatenate([dsh1_0, gm0 * da1_0, dgt1_0, dsh2_0, gp0 * da2_0, dgt2_0], axis=1),
        jnp.concatenate([dsh1_1, gm1 * da1_1, dgt1_1, dsh2_1, gp1 * da2_1, dgt2_1], axis=1)]).reshape(-1)
    dg_mix = jnp.concatenate([(1.0 + sc1) * da1_0, (1.0 + tc1) * da1_1], axis=1).reshape(-1)
    dg_mlp = jnp.concatenate([(1.0 + sc2) * da2_0, (1.0 + tc2) * da2_1], axis=1).reshape(-1)
    parts = [loss_part.reshape(-1), dmod, dg_mix, dg_mlp, dg_q.reshape(-1), dg_kv.reshape(-1), dsink.reshape(-1),
             dg_final.reshape(-1), db_qkv.reshape(-1), db_o.reshape(-1)]
    soffs = np.concatenate([[0], np.cumsum([p.size for p in parts])])
    spad = -(-int(soffs[-1]) // 1024) * 1024
    spack = jnp.pad(jnp.concatenate(parts), (0, spad - int(soffs[-1]))).reshape(8, spad // 8)
    sall = _all_gather("gather_small_grads", spack, F32)
    ssum = _sum_devices(sall).reshape(-1)
    tot = [ssum[int(soffs[i]):int(soffs[i + 1])] for i in range(len(parts))]
    loss = tot[0][0]
    nsink = swa_sinks.shape[1]
    small_g = dict(b_ada=tot[1].reshape(b_ada.shape), g_mix=tot[2].reshape(g_mix.shape), g_mlp=tot[3].reshape(g_mlp.shape),
                   mla_g_q=tot[4].reshape(mla_g_q.shape), mla_g_kv=tot[5].reshape(mla_g_kv.shape),
                   swa_sinks=tot[6][:nsink].reshape(swa_sinks.shape), g_final=tot[7].reshape(g_final.shape),
                   swa_b_qkv=lax.dynamic_slice(tot[8], (me * nbq,), (nbq,)).reshape(swa_b_qkv.shape),
                   swa_b_o=lax.dynamic_slice(tot[9], (me * nbo,), (nbo,)).reshape(swa_b_o.shape))
    small_w = dict(b_ada=b_ada, g_mix=g_mix, g_mlp=g_mlp, mla_g_q=mla_g_q, mla_g_kv=mla_g_kv, swa_sinks=swa_sinks,
                   g_final=g_final, swa_b_qkv=swa_b_qkv, swa_b_o=swa_b_o)
    small_m = dict(b_ada=m_b_ada, g_mix=m_g_mix, g_mlp=m_g_mlp, mla_g_q=m_mla_g_q, mla_g_kv=m_mla_g_kv,
                   swa_sinks=m_swa_sinks, g_final=m_g_final, swa_b_qkv=m_swa_b_qkv, swa_b_o=m_swa_b_o)
    small_v = dict(b_ada=v_b_ada, g_mix=v_g_mix, g_mlp=v_g_mlp, mla_g_q=v_mla_g_q, mla_g_kv=v_mla_g_kv,
                   swa_sinks=v_swa_sinks, g_final=v_g_final, swa_b_qkv=v_swa_b_qkv, swa_b_o=v_swa_b_o)
    SMALL = list(small_w)
    woffs = np.concatenate([[0], np.cumsum([small_w[n].size for n in SMALL])])
    wpad = -(-int(woffs[-1]) // 1024) * 1024

    def spack_of(d):
        flat = jnp.concatenate([d[n].reshape(-1) for n in SMALL])
        return jnp.pad(flat, (0, wpad - int(woffs[-1]))).reshape(8, wpad // 8)

    sm = _adamw_small(spack_of(small_w), spack_of(small_g), spack_of(small_m), spack_of(small_v))
    small_d, small_nm, small_nv = (
        {n: a.reshape(-1)[int(woffs[i]):int(woffs[i + 1])].reshape(small_w[n].shape) for i, n in enumerate(SMALL)}
        for a in sm)

    b_off = int(soffs[1])
    dmod_all = sall.reshape(N_DEV, -1)[:, b_off:b_off + L * N_DEV * NC].reshape(N_DEV, L, N_DEV * NC)
    dmod_cols = jnp.moveaxis(lax.dynamic_slice_in_dim(dmod_all, me * NC, NC, axis=2), 0, 1)
    ada_g, ada_d, ada_nm, ada_nv = _ada_bwd_adamw(c_all.T, dmod_cols, w_ada, m_w_ada, v_w_ada)

    order = ["w_ada", "b_ada", "g_mix", "g_mlp", "mla_w_dq", "mla_g_q", "mla_w_uq", "mla_w_dkv", "mla_g_kv",
             "mla_w_ukv", "mla_w_o", "swa_w_qkv", "swa_b_qkv", "swa_sinks", "swa_w_o", "swa_b_o", "w_ff1", "w_ff2", "g_final"]

    def collect(ada, big, small):
        return [ada if n == "w_ada" else (big[n] if n in big else small[n]) for n in order]

    return (loss, dx0.reshape(x.shape), *collect(ada_g, big_g, small_g), *collect(ada_d, big_d, small_d),
            *collect(ada_nm, big_nm, small_nm), *collect(ada_nv, big_nv, small_nv))
```

```python
import functools

import jax
import jax.numpy as jnp
import numpy as np
from jax import lax
from jax.experimental import pallas as pl
from jax.experimental.pallas import tpu as pltpu

F32 = jnp.float32
BF16 = jnp.bfloat16
MESH_IDS = pl.DeviceIdType.MESH
N_DEV = 8

MLA_HEADS = 8
QK_NOPE = 128
QK_ROPE = 64
QK_DIM = QK_NOPE + QK_ROPE
V_DIM = 128
KV_LORA = 256
ROPE_THETA = 10000.0
SWA_HEADS = 16
SWA_KV_HEADS = 4
SWA_GROUP = SWA_HEADS // SWA_KV_HEADS
SWA_HEAD_DIM = 64
WINDOW = 128
EPS = 1e-6
LOG2E = 1.4426950408889634

ADAM_LR = 0.001
ADAM_B1 = 0.9
ADAM_B2 = 0.999
ADAM_EPS = 1e-08
ADAM_WD = 0.01
ADAM_STEP = 10

PACK_COLS = 1024
RS_CHUNKS = 4
VMEM_LIMIT = 56 << 20
ROW_TILE = 512
ROW_TILE_WIDE = 256
ATT_TILE = 512


def _dot(a, b):
    return jnp.dot(a, b, preferred_element_type=F32)


def _dot_nt(a, b):
    return lax.dot_general(a, b, (((1,), (1,)), ((), ())), preferred_element_type=F32)


def _dot_tn(a, b):
    return lax.dot_general(a, b, (((0,), (0,)), ((), ())), preferred_element_type=F32)


def _rstd(x):
    return lax.rsqrt(jnp.mean(x * x, axis=-1, keepdims=True) + EPS)


def _rms_bwd(dn, n, r):
    return r * (dn - n * jnp.mean(dn * n, axis=-1, keepdims=True))


def _modulate(x, g, sc, sh):
    r = _rstd(x)
    return ((x * r) * g) * (1.0 + sc) + sh


def _modulate_bwd(dh, x, g, sc):
    r = _rstd(x)
    n = x * r
    dsh = jnp.sum(dh, axis=0, keepdims=True)
    da = jnp.sum(dh * n, axis=0, keepdims=True)
    dx = _rms_bwd(dh * (g * (1.0 + sc)), n, r)
    return dx, dsh, da


def _first(i):
    return i == 0


def _acc(ref, val, i):
    @pl.when(i == 0)
    def _():
        ref[...] = val

    @pl.when(i != 0)
    def _():
        ref[...] += val


def _row_spec(shape, tm):
    nd = len(shape)
    return pl.BlockSpec(tuple(shape[:nd - 2]) + (tm, shape[-1]), lambda i: (0,) * (nd - 2) + (i, 0))


def _resident_spec(shape, single_buffer):
    nd = len(shape)
    if single_buffer:
        return pl.BlockSpec(tuple(shape), lambda i: (0,) * nd, pipeline_mode=pl.Buffered(1))
    return pl.BlockSpec(tuple(shape), lambda i: (0,) * nd)


def _rowcall(name, body, tokens, tm, row_in, full_in, row_out, acc_out=()):
    tm = min(tm, tokens)
    in_specs = [_row_spec(a.shape, tm) for a in row_in] + [_resident_spec(a.shape, True) for a in full_in]
    row_specs = [s[1] if isinstance(s, tuple) else _row_spec(s.shape, tm) for s in row_out]
    row_out = [s[0] if isinstance(s, tuple) else s for s in row_out]
    out_specs = row_specs + [_resident_spec(s.shape, False) for s in acc_out]
    return pl.pallas_call(
        body, name=name, grid=(tokens // tm,), in_specs=in_specs, out_specs=out_specs,
        out_shape=list(row_out) + list(acc_out),
        compiler_params=pltpu.CompilerParams(dimension_semantics=("arbitrary",), vmem_limit_bytes=VMEM_LIMIT),
    )(*row_in, *full_in)


def _sds(shape, dtype):
    return jax.ShapeDtypeStruct(tuple(shape), dtype)


def _mla_in_fwd(x, cos, sin, g, sc, sh, w_cat, g_q, w_uqx, g_kv, w_ukv, t):
    S, D = x.shape
    QL = g_q.shape[1]
    H = MLA_HEADS
    t = min(t, S)

    def body(x_ref, cos_ref, sin_ref, g_ref, sc_ref, sh_ref, wcat_ref, gq_ref, wuqx_ref, gkv_ref, wukv_ref,
             h_ref, cqp_ref, cq_ref, ckvp_ref, ckv_ref, q_ref, k_ref, v_ref, vt_ref):
        cs, sn = cos_ref[...], sin_ref[...]
        hb = _modulate(x_ref[...], g_ref[...], sc_ref[...], sh_ref[...]).astype(BF16)
        h_ref[...] = hb
        low = _dot(hb, wcat_ref[...])
        cqp = low[:, :QL]
        cqp_ref[...] = cqp
        cq = ((cqp * _rstd(cqp)) * gq_ref[...]).astype(BF16)
        cq_ref[...] = cq
        ckvp = low[:, QL:QL + KV_LORA]
        ckvp_ref[...] = ckvp
        ckv = ((ckvp * _rstd(ckvp)) * gkv_ref[...]).astype(BF16)
        ckv_ref[...] = ckv
        o = QL + KV_LORA
        kr = (low[:, o:o + QK_ROPE] * cs + low[:, o + QK_ROPE:o + 2 * QK_ROPE] * sn).astype(BF16)
        qx = _dot(cq, wuqx_ref[...])
        kv = _dot(ckv, wukv_ref[...])
        for hd in range(H):
            b = hd * 256
            q_ref[hd, :, 0:QK_NOPE] = qx[:, b:b + QK_NOPE].astype(BF16)
            q_ref[hd, :, QK_NOPE:QK_DIM] = (qx[:, b + 128:b + 192] * cs + qx[:, b + 192:b + 256] * sn).astype(BF16)
            k_ref[hd, :, 0:QK_NOPE] = kv[:, b:b + QK_NOPE].astype(BF16)
            k_ref[hd, :, QK_NOPE:QK_DIM] = kr
            vh = kv[:, b + 128:b + 256]
            v_ref[hd] = vh.astype(BF16)
            vt_ref[hd, 0, 0:V_DIM, :] = vh.T.astype(BF16)
            vt_ref[hd, 0, V_DIM:2 * V_DIM, :] = jnp.ones((V_DIM, x_ref.shape[0]), BF16)

    vt_spec = pl.BlockSpec((H, 1, 2 * V_DIM, t), lambda i: (0, i, 0, 0))
    return _rowcall(
        "mla_in_fwd", body, S, t, [x, cos, sin], [g, sc, sh, w_cat, g_q, w_uqx, g_kv, w_ukv],
        [_sds((S, D), BF16), _sds((S, QL), F32), _sds((S, QL), BF16), _sds((S, KV_LORA), F32), _sds((S, KV_LORA), BF16),
         _sds((H, S, QK_DIM), BF16), _sds((H, S, QK_DIM), BF16), _sds((H, S, V_DIM), BF16),
         (_sds((H, S // t, 2 * V_DIM, t), BF16), vt_spec)])


def _mla_attn_fwd(q, k, vt, t):
    H, S, DQ = q.shape
    DV = V_DIM
    t = min(t, S)
    nb = S // t
    scale = QK_DIM ** -0.5
    c2 = scale * LOG2E

    def body(q_ref, k_ref, vt_ref, o_ref, lse_ref, m_s, acc_s):
        qi = pl.program_id(1)
        m_s[...] = jnp.full_like(m_s, -jnp.inf)
        acc_s[...] = jnp.zeros_like(acc_s)

        def tile(j, diagonal):
            rows = pl.ds(pl.multiple_of(j * t, t), t)
            s = _dot_nt(k_ref[0, rows, :], q_ref[0])
            if diagonal:
                key = lax.broadcasted_iota(jnp.int32, (t, t), 0)
                qry = lax.broadcasted_iota(jnp.int32, (t, t), 1)
                s = jnp.where(key <= qry, s, -jnp.inf)
            m_prev = m_s[...]
            m_new = jnp.maximum(m_prev, jnp.max(s, axis=0, keepdims=True))
            alpha = jnp.exp2((m_prev - m_new) * c2)
            p = jnp.exp2((s - m_new) * c2)
            acc_s[...] = alpha * acc_s[...] + _dot(vt_ref[0, j], p.astype(BF16))
            m_s[...] = m_new

        def off_diagonal(j, carry):
            tile(j, False)
            return carry

        lax.fori_loop(0, qi, off_diagonal, 0)
        tile(qi, True)
        acc = acc_s[...]
        o_ref[...] = (acc[:DV] / acc[DV:]).T.astype(BF16)
        lse_ref[0, 0] = m_s[...] * scale + jnp.log(acc[DV:DV + 1])

    return pl.pallas_call(
        body, name="mla_attn_fwd", grid=(H, nb),
        in_specs=[pl.BlockSpec((1, t, DQ), lambda h, i: (h, i, 0)),
                  pl.BlockSpec((1, S, DQ), lambda h, i: (h, 0, 0)),
                  pl.BlockSpec((1, nb, 2 * DV, t), lambda h, i: (h, 0, 0, 0))],
        out_specs=[pl.BlockSpec((t, DV), lambda h, i: (i, h)),
                   pl.BlockSpec((1, 1, 1, t), lambda h, i: (h, i, 0, 0))],
        out_shape=[_sds((S, H * DV), BF16), _sds((H, nb, 1, t), F32)],
        scratch_shapes=[pltpu.VMEM((1, t), F32), pltpu.VMEM((2 * DV, t), F32)],
        compiler_params=pltpu.CompilerParams(dimension_semantics=("parallel", "arbitrary"),
                                             vmem_limit_bytes=VMEM_LIMIT),
    )(q, k, vt)


def _attn_out_fwd(o, x, w_o, b_o, gt, g, sc, sh):
    S, D = x.shape

    def body(o_ref, x_ref, wo_ref, bo_ref, gt_ref, g_ref, sc_ref, sh_ref, y_ref, x1_ref, h_ref):
        y = _dot(o_ref[...], wo_ref[...]) + bo_ref[...]
        y_ref[...] = y
        x1 = x_ref[...] + gt_ref[...] * y
        x1_ref[...] = x1
        h_ref[...] = _modulate(x1, g_ref[...], sc_ref[...], sh_ref[...]).astype(BF16)

    return _rowcall("attn_out_fwd", body, S, ROW_TILE, [o, x], [w_o, b_o, gt, g, sc, sh],
                    [_sds((S, D), F32), _sds((S, D), F32), _sds((S, D), BF16)])


def _mlp_fwd(h, x, w1, w2, gt):
    S, D = x.shape
    FF = w1.shape[1]

    def body(h_ref, x_ref, w1_ref, w2_ref, gt_ref, rl_ref, act_ref, y_ref, x2_ref):
        rl = jnp.maximum(_dot(h_ref[...], w1_ref[...]), 0.0)
        rl_ref[...] = rl.astype(BF16)
        act = (rl * rl).astype(BF16)
        act_ref[...] = act
        y = _dot(act, w2_ref[...])
        y_ref[...] = y
        x2_ref[...] = x_ref[...] + gt_ref[...] * y

    return _rowcall("mlp_fwd", body, S, ROW_TILE_WIDE, [h, x], [w1, w2, gt],
                    [_sds((S, FF), BF16), _sds((S, FF), BF16), _sds((S, D), F32), _sds((S, D), F32)])


def _swa_in_fwd(x, g, sc, sh, w_qkv, b_qkv):
    S, D = x.shape
    NQ = SWA_HEADS * SWA_HEAD_DIM
    NK = SWA_KV_HEADS * SWA_HEAD_DIM

    def body(x_ref, g_ref, sc_ref, sh_ref, w_ref, b_ref, h_ref, q_ref, k_ref, v_ref):
        hb = _modulate(x_ref[...], g_ref[...], sc_ref[...], sh_ref[...]).astype(BF16)
        h_ref[...] = hb
        qkv = _dot(hb, w_ref[...]) + b_ref[...]
        q_ref[...] = qkv[:, :NQ].astype(BF16)
        k_ref[...] = qkv[:, NQ:NQ + NK].astype(BF16)
        v_ref[...] = qkv[:, NQ + NK:].astype(BF16)

    return _rowcall("swa_in_fwd", body, S, ROW_TILE, [x], [g, sc, sh, w_qkv, b_qkv],
                    [_sds((S, D), BF16), _sds((S, NQ), BF16), _sds((S, NK), BF16), _sds((S, NK), BF16)])


def _alibi_slope(head):
    return float(np.float32(2.0 ** (-8.0 * (head + 1) / SWA_HEADS)))


def _swa_geometry(n):
    W = WINDOW
    row = lax.broadcasted_iota(jnp.int32, (W, 2 * W), 0)
    col = lax.broadcasted_iota(jnp.int32, (W, 2 * W), 1)
    dist = W + row - col
    valid = (dist >= 0) & (dist < W) & ((n > 0) | (col >= W))
    return dist.astype(F32), valid


def _swa_band_specs(W, nb, cols):
    prev = pl.BlockSpec((W, cols), lambda n: (jnp.maximum(jnp.minimum(n, nb - 1) - 1, 0), 0))
    cur = pl.BlockSpec((W, cols), lambda n: (jnp.minimum(n, nb - 1), 0))
    return prev, cur


def _swa_attn_fwd(q, k, v, sinks):
    S, NQ = q.shape
    NK = k.shape[1]
    W, Dh, G = WINDOW, SWA_HEAD_DIM, SWA_GROUP
    nb = S // W

    def body(q_ref, kp_ref, kc_ref, vp_ref, vc_ref, sink_ref, o_ref, lse_ref):
        distf, valid = _swa_geometry(pl.program_id(0))
        for kh in range(SWA_KV_HEADS):
            ck = slice(kh * Dh, (kh + 1) * Dh)
            kb = jnp.concatenate([kp_ref[:, ck], kc_ref[:, ck]], axis=0)
            vb = jnp.concatenate([vp_ref[:, ck], vc_ref[:, ck]], axis=0)
            for gi in range(G):
                hq = kh * G + gi
                cq = slice(hq * Dh, (hq + 1) * Dh)
                s = _dot_nt(q_ref[:, cq], kb) * (Dh ** -0.5) - _alibi_slope(hq) * distf
                s = jnp.where(valid, s, -jnp.inf)
                sink = sink_ref[:, hq:hq + 1]
                m = jnp.maximum(jnp.max(s, axis=-1, keepdims=True), sink)
                p = jnp.exp(s - m)
                denom = jnp.sum(p, axis=-1, keepdims=True) + jnp.exp(sink - m)
                o_ref[:, cq] = _dot((p * (1.0 / denom)).astype(BF16), vb).astype(BF16)
                lse_ref[:, hq:hq + 1] = m + jnp.log(denom)

    kprev, kcur = _swa_band_specs(W, nb, NK)
    return pl.pallas_call(
        body, name="swa_attn_fwd", grid=(nb,),
        in_specs=[pl.BlockSpec((W, NQ), lambda n: (n, 0)), kprev, kcur, kprev, kcur,
                  pl.BlockSpec((1, SWA_HEADS), lambda n: (0, 0))],
        out_specs=[pl.BlockSpec((W, NQ), lambda n: (n, 0)), pl.BlockSpec((W, SWA_HEADS), lambda n: (n, 0))],
        out_shape=[_sds((S, NQ), BF16), _sds((S, SWA_HEADS), F32)],
        compiler_params=pltpu.CompilerParams(dimension_semantics=("arbitrary",), vmem_limit_bytes=VMEM_LIMIT),
    )(q, k, k, v, v, sinks)


def _final_loss(x, target, g):
    S, D = x.shape

    def body(x_ref, t_ref, g_ref, dx_ref, loss_ref, dg_ref):
        i = pl.program_id(0)
        xv = x_ref[...]
        r = _rstd(xv)
        n = xv * r
        err = n * g_ref[...] - t_ref[...]
        part = 0.5 * jnp.sum(jnp.mean(err * err, axis=-1, keepdims=True), axis=0, keepdims=True)
        _acc(loss_ref, jnp.broadcast_to(part, loss_ref.shape), i)
        dout = err / D
        _acc(dg_ref, jnp.sum(dout * n, axis=0, keepdims=True), i)
        dx_ref[...] = _rms_bwd(dout * g_ref[...], n, r)

    return _rowcall("final_loss", body, S, ROW_TILE, [x, target], [g], [_sds((S, D), F32)],
                    [_sds((1, 128), F32), _sds((1, D), F32)])


def _mlp_bwd_a(dx, y, rl, gt, w2):
    S, D = dx.shape
    FF = rl.shape[1]

    def body(dx_ref, y_ref, rl_ref, gt_ref, w2_ref, dy_ref, du_ref, dgt_ref):
        i = pl.program_id(0)
        dxv = dx_ref[...]
        _acc(dgt_ref, jnp.sum(dxv * y_ref[...], axis=0, keepdims=True), i)
        dy = (dxv * gt_ref[...]).astype(BF16)
        dy_ref[...] = dy
        dact = _dot_nt(dy, w2_ref[...])
        du_ref[...] = (dact * (2.0 * rl_ref[...].astype(F32))).astype(BF16)

    return _rowcall("mlp_bwd_a", body, S, ROW_TILE_WIDE, [dx, y, rl], [gt, w2],
                    [_sds((S, D), BF16), _sds((S, FF), BF16)], [_sds((1, D), F32)])


def _mlp_bwd_b(du, x, dx, w1, g, sc):
    S, D = x.shape

    def body(du_ref, x_ref, dx_ref, w1_ref, g_ref, sc_ref, dxo_ref, dsh_ref, da_ref):
        i = pl.program_id(0)
        dh = _dot_nt(du_ref[...], w1_ref[...])
        dxn, dsh, da = _modulate_bwd(dh, x_ref[...], g_ref[...], sc_ref[...])
        dxo_ref[...] = dx_ref[...] + dxn
        _acc(dsh_ref, dsh, i)
        _acc(da_ref, da, i)

    return _rowcall("mlp_bwd_b", body, S, ROW_TILE_WIDE, [du, x, dx], [w1, g, sc],
                    [_sds((S, D), F32)], [_sds((1, D), F32), _sds((1, D), F32)])


def _attn_out_bwd(dx, y, o, gt, w_o, n_heads):
    S, D = dx.shape
    NO = o.shape[1]
    dh = NO // n_heads

    def body(dx_ref, y_ref, o_ref, gt_ref, wo_ref, dy_ref, do_ref, dl_ref, dgt_ref, dbo_ref):
        i = pl.program_id(0)
        dxv = dx_ref[...]
        _acc(dgt_ref, jnp.sum(dxv * y_ref[...], axis=0, keepdims=True), i)
        dy = dxv * gt_ref[...]
        _acc(dbo_ref, jnp.sum(dy, axis=0, keepdims=True), i)
        dyb = dy.astype(BF16)
        dy_ref[...] = dyb
        do = _dot_nt(dyb, wo_ref[...])
        do_ref[...] = do.astype(BF16)
        prod = do * o_ref[...].astype(F32)
        for hd in range(n_heads):
            dl_ref[:, hd:hd + 1] = jnp.sum(prod[:, hd * dh:(hd + 1) * dh], axis=-1, keepdims=True)

    return _rowcall("attn_out_bwd", body, S, ROW_TILE, [dx, y, o], [gt, w_o],
                    [_sds((S, D), BF16), _sds((S, NO), BF16), _sds((S, n_heads), F32)],
                    [_sds((1, D), F32), _sds((1, D), F32)])


def _mla_attn_bwd(q, k, v, do, lse, delta, t):
    H, S, DQ = q.shape
    DV = V_DIM
    t = min(t, S)
    nb = S // t
    scale = QK_DIM ** -0.5
    c2 = scale * LOG2E

    def body(q_ref, k_ref, v_ref, do_ref, lse_ref, dl_ref, dq_ref, dk_ref, dv_ref, dk_s, dv_s):
        kj = pl.program_id(1)

        @pl.when(kj == 0)
        def _():
            dq_ref[...] = jnp.zeros_like(dq_ref)

        dk_s[...] = jnp.zeros_like(dk_s)
        dv_s[...] = jnp.zeros_like(dv_s)

        def tile(i, diagonal):
            rows = pl.ds(pl.multiple_of(i * t, t), t)
            qb, dob, kb = q_ref[0, rows, :], do_ref[rows, :], k_ref[0]
            p = jnp.exp2(_dot_nt(kb, qb) * c2 - lse_ref[0, i])
            if diagonal:
                key = lax.broadcasted_iota(jnp.int32, (t, t), 0)
                qry = lax.broadcasted_iota(jnp.int32, (t, t), 1)
                p = jnp.where(key <= qry, p, 0.0)
            dv_s[...] += _dot(p.astype(BF16), dob)
            dp = _dot_nt(v_ref[0], dob)
            ds = (p * (dp - dl_ref[0, i])).astype(BF16)
            dk_s[...] += _dot(ds, qb)
            dq_ref[0, rows, :] += _dot_tn(ds, kb)

        def off_diagonal(i, carry):
            tile(i, False)
            return carry

        tile(kj, True)
        lax.fori_loop(kj + 1, nb, off_diagonal, 0)
        dk_ref[0] = (dk_s[...] * scale).astype(BF16)
        dv_ref[0] = dv_s[...].astype(BF16)

    rowspec = pl.BlockSpec((1, nb, 1, t), lambda h, j: (h, 0, 0, 0))
    return pl.pallas_call(
        body, name="mla_attn_bwd", grid=(H, nb),
        in_specs=[pl.BlockSpec((1, S, DQ), lambda h, j: (h, 0, 0)),
                  pl.BlockSpec((1, t, DQ), lambda h, j: (h, j, 0)),
                  pl.BlockSpec((1, t, DV), lambda h, j: (h, j, 0)),
                  pl.BlockSpec((S, DV), lambda h, j: (0, h)), rowspec, rowspec],
        out_specs=[pl.BlockSpec((1, S, DQ), lambda h, j: (h, 0, 0)),
                   pl.BlockSpec((1, t, DQ), lambda h, j: (h, j, 0)),
                   pl.BlockSpec((1, t, DV), lambda h, j: (h, j, 0))],
        out_shape=[_sds((H, S, DQ), F32), _sds((H, S, DQ), BF16), _sds((H, S, DV), BF16)],
        scratch_shapes=[pltpu.VMEM((t, DQ), F32), pltpu.VMEM((t, DV), F32)],
        compiler_params=pltpu.CompilerParams(dimension_semantics=("parallel", "arbitrary"),
                                             vmem_limit_bytes=VMEM_LIMIT),
    )(q, k, v, do, lse, delta)


def _swa_attn_bwd(q, k, v, do, lse, delta, sinks):
    S, NQ = q.shape
    NK = k.shape[1]
    W, Dh, G = WINDOW, SWA_HEAD_DIM, SWA_GROUP
    nb = S // W

    def body(q_ref, kp_ref, kc_ref, vp_ref, vc_ref, do_ref, lse_ref, dl_ref, sink_ref,
             dq_ref, dk_ref, dv_ref, dsink_ref, dkc_s, dvc_s):
        n = pl.program_id(0)

        @pl.when(n == 0)
        def _():
            dkc_s[...] = jnp.zeros_like(dkc_s)
            dvc_s[...] = jnp.zeros_like(dvc_s)
            dsink_ref[...] = jnp.zeros_like(dsink_ref)

        @pl.when(n < nb)
        def _():
            distf, valid = _swa_geometry(n)
            for kh in range(SWA_KV_HEADS):
                ck = slice(kh * Dh, (kh + 1) * Dh)
                kb = jnp.concatenate([kp_ref[:, ck], kc_ref[:, ck]], axis=0)
                vb = jnp.concatenate([vp_ref[:, ck], vc_ref[:, ck]], axis=0)
                dkb = jnp.zeros((2 * W, Dh), F32)
                dvb = jnp.zeros((2 * W, Dh), F32)
                for gi in range(G):
                    hq = kh * G + gi
                    cq = slice(hq * Dh, (hq + 1) * Dh)
                    qh, doh = q_ref[:, cq], do_ref[:, cq]
                    lse_h = lse_ref[:, hq:hq + 1]
                    dl_h = dl_ref[:, hq:hq + 1]
                    s = _dot_nt(qh, kb) * (Dh ** -0.5) - _alibi_slope(hq) * distf
                    p = jnp.where(valid, jnp.exp(s - lse_h), 0.0)
                    dvb = dvb + _dot_tn(p.astype(BF16), doh)
                    dp = _dot_nt(doh, vb)
                    dsb = ((p * (dp - dl_h)) * (Dh ** -0.5)).astype(BF16)
                    dq_ref[:, cq] = _dot(dsb, kb).astype(BF16)
                    dkb = dkb + _dot_tn(dsb, qh)
                    psink = jnp.exp(sink_ref[:, hq:hq + 1] - lse_h)
                    dsink_ref[:, hq:hq + 1] += -jnp.sum(psink * dl_h, axis=0, keepdims=True)
                dk_ref[:, ck] = (dkc_s[:, ck] + dkb[:W]).astype(BF16)
                dv_ref[:, ck] = (dvc_s[:, ck] + dvb[:W]).astype(BF16)
                dkc_s[:, ck] = dkb[W:]
                dvc_s[:, ck] = dvb[W:]

        @pl.when(n == nb)
        def _():
            dk_ref[...] = dkc_s[...].astype(BF16)
            dv_ref[...] = dvc_s[...].astype(BF16)

    kprev, kcur = _swa_band_specs(W, nb, NK)
    qspec = lambda cols: pl.BlockSpec((W, cols), lambda n: (jnp.minimum(n, nb - 1), 0))
    kvout = pl.BlockSpec((W, NK), lambda n: (jnp.maximum(n - 1, 0), 0))
    return pl.pallas_call(
        body, name="swa_attn_bwd", grid=(nb + 1,),
        in_specs=[qspec(NQ), kprev, kcur, kprev, kcur, qspec(NQ), qspec(SWA_HEADS), qspec(SWA_HEADS),
                  pl.BlockSpec((1, SWA_HEADS), lambda n: (0, 0))],
        out_specs=[qspec(NQ), kvout, kvout, pl.BlockSpec((1, 128), lambda n: (0, 0))],
        out_shape=[_sds((S, NQ), BF16), _sds((S, NK), BF16), _sds((S, NK), BF16), _sds((1, 128), F32)],
        scratch_shapes=[pltpu.VMEM((W, NK), F32), pltpu.VMEM((W, NK), F32)],
        compiler_params=pltpu.CompilerParams(dimension_semantics=("arbitrary",), vmem_limit_bytes=VMEM_LIMIT),
    )(q, k, k, v, v, do, lse, delta, sinks)


def _swa_in_bwd(dq, dk, dv, x, dx, w_qkv, g, sc):
    S, D = x.shape
    N = w_qkv.shape[1]

    def body(dq_ref, dk_ref, dv_ref, x_ref, dx_ref, w_ref, g_ref, sc_ref, dqkv_ref, dxo_ref, db_ref, dsh_ref, da_ref):
        i = pl.program_id(0)
        dqkv = jnp.concatenate([dq_ref[...], dk_ref[...], dv_ref[...]], axis=1)
        dqkv_ref[...] = dqkv
        _acc(db_ref, jnp.sum(dqkv.astype(F32), axis=0, keepdims=True), i)
        dh = _dot_nt(dqkv, w_ref[...])
        dxn, dsh, da = _modulate_bwd(dh, x_ref[...], g_ref[...], sc_ref[...])
        dxo_ref[...] = dx_ref[...] + dxn
        _acc(dsh_ref, dsh, i)
        _acc(da_ref, da, i)

    return _rowcall("swa_in_bwd", body, S, ROW_TILE, [dq, dk, dv, x, dx], [w_qkv, g, sc],
                    [_sds((S, N), BF16), _sds((S, D), F32)],
                    [_sds((1, N), F32), _sds((1, D), F32), _sds((1, D), F32)])


def _mla_in_bwd(dq, dk, dv, cos, sin, cqp, ckvp, x, dx, w_uqx, g_q, w_ukv, g_kv, w_cat, g, sc):
    S, D = x.shape
    H = MLA_HEADS
    QL = g_q.shape[1]
    NX = w_uqx.shape[1]
    NC = w_cat.shape[1]

    def body(dq_ref, dk_ref, dv_ref, cos_ref, sin_ref, cqp_ref, ckvp_ref, x_ref, dx_ref,
             wuqx_ref, gq_ref, wukv_ref, gkv_ref, wcat_ref, g_ref, sc_ref,
             dqx_ref, dkv_ref, dcat_ref, dxo_ref, dgq_ref, dgkv_ref, dsh_ref, da_ref):
        i = pl.program_id(0)
        cs, sn = cos_ref[...], sin_ref[...]
        dkr = jnp.zeros(cs.shape, F32)
        for hd in range(H):
            b = hd * 256
            dqh = dq_ref[hd] * (QK_DIM ** -0.5)
            dqx_ref[:, b:b + QK_NOPE] = dqh[:, :QK_NOPE].astype(BF16)
            dqx_ref[:, b + 128:b + 192] = (dqh[:, QK_NOPE:] * cs).astype(BF16)
            dqx_ref[:, b + 192:b + 256] = (dqh[:, QK_NOPE:] * sn).astype(BF16)
            dkh = dk_ref[hd]
            dkv_ref[:, b:b + QK_NOPE] = dkh[:, :QK_NOPE]
            dkv_ref[:, b + 128:b + 256] = dv_ref[hd]
            dkr = dkr + dkh[:, QK_NOPE:].astype(F32)
        dcq = _dot_nt(dqx_ref[...], wuqx_ref[...])
        cqp = cqp_ref[...]
        rq = _rstd(cqp)
        nq = cqp * rq
        _acc(dgq_ref, jnp.sum(dcq * nq, axis=0, keepdims=True), i)
        dcqp = _rms_bwd(dcq * gq_ref[...], nq, rq)
        dckv = _dot_nt(dkv_ref[...], wukv_ref[...])
        ckvp = ckvp_ref[...]
        rk = _rstd(ckvp)
        nk = ckvp * rk
        _acc(dgkv_ref, jnp.sum(dckv * nk, axis=0, keepdims=True), i)
        dckvp = _rms_bwd(dckv * gkv_ref[...], nk, rk)
        dcat_ref[:, :QL] = dcqp.astype(BF16)
        dcat_ref[:, QL:QL + KV_LORA] = dckvp.astype(BF16)
        o = QL + KV_LORA
        dcat_ref[:, o:o + QK_ROPE] = (dkr * cs).astype(BF16)
        dcat_ref[:, o + QK_ROPE:o + 2 * QK_ROPE] = (dkr * sn).astype(BF16)
        dh = _dot_nt(dcat_ref[...], wcat_ref[...])
        dxn, dsh, da = _modulate_bwd(dh, x_ref[...], g_ref[...], sc_ref[...])
        dxo_ref[...] = dx_ref[...] + dxn
        _acc(dsh_ref, dsh, i)
        _acc(da_ref, da, i)

    return _rowcall("mla_in_bwd", body, S, ROW_TILE_WIDE, [dq, dk, dv, cos, sin, cqp, ckvp, x, dx],
                    [w_uqx, g_q, w_ukv, g_kv, w_cat, g, sc],
                    [_sds((S, NX), BF16), _sds((S, NX), BF16), _sds((S, NC), BF16), _sds((S, D), F32)],
                    [_sds((1, QL), F32), _sds((1, KV_LORA), F32), _sds((1, D), F32), _sds((1, D), F32)])


def _matmul_tn(name, a, b):
    S, K = a.shape
    N = b.shape[1]
    tk, tn, ts = min(K, 1024), min(N, 1024), min(S, ROW_TILE)
    if N % tn:
        tn = 512 if N % 512 == 0 else (384 if N % 384 == 0 else 128)
    if K % tk:
        tk = 512 if K % 512 == 0 else (384 if K % 384 == 0 else 128)
    ns = S // ts

    def body(a_ref, b_ref, o_ref):
        _acc(o_ref, _dot_tn(a_ref[...], b_ref[...]), pl.program_id(2))

    return pl.pallas_call(
        body, name=name, grid=(K // tk, N // tn, ns),
        in_specs=[pl.BlockSpec((ts, tk), lambda i, j, s: (s, i)), pl.BlockSpec((ts, tn), lambda i, j, s: (s, j))],
        out_specs=pl.BlockSpec((tk, tn), lambda i, j, s: (i, j)),
        out_shape=_sds((K, N), F32),
        compiler_params=pltpu.CompilerParams(dimension_semantics=("parallel", "parallel", "arbitrary"),
                                             vmem_limit_bytes=VMEM_LIMIT),
    )(a, b)


def _silu(c):
    return c * jax.nn.sigmoid(c)


def _ada_fwd(c_all, w_ada):
    L, D, NC = w_ada.shape

    def body(c_ref, w_ref, o_ref):
        cond = _silu(c_ref[...]).astype(BF16)
        o_ref[0] = _dot(cond, w_ref[0].astype(BF16))

    return pl.pallas_call(
        body, name="ada_fwd", grid=(L,),
        in_specs=[pl.BlockSpec(c_all.shape, lambda l: (0, 0)), pl.BlockSpec((1, D, NC), lambda l: (l, 0, 0))],
        out_specs=pl.BlockSpec((1, N_DEV, NC), lambda l: (l, 0, 0)),
        out_shape=_sds((L, N_DEV, NC), F32),
        compiler_params=pltpu.CompilerParams(dimension_semantics=("arbitrary",), vmem_limit_bytes=VMEM_LIMIT),
    )(c_all, w_ada)


def _adamw(w, g, m, v):
    m = ADAM_B1 * m + (1.0 - ADAM_B1) * g
    v = ADAM_B2 * v + (1.0 - ADAM_B2) * (g * g)
    m_hat = m / (1.0 - ADAM_B1 ** ADAM_STEP)
    v_hat = v / (1.0 - ADAM_B2 ** ADAM_STEP)
    delta = -ADAM_LR * (m_hat / (jnp.sqrt(v_hat) + ADAM_EPS) + ADAM_WD * w)
    return delta, m, v


def _ada_bwd_adamw(c_all_t, dmod_cols, w, m, v):
    L, D, NC = w.shape
    tr = min(D, 256)

    def body(ct_ref, dm_ref, w_ref, m_ref, v_ref, g_ref, d_ref, mo_ref, vo_ref):
        cond_t = _silu(ct_ref[...])
        dm = dm_ref[0]
        g = cond_t[:, 0:1] * dm[0:1, :]
        for b in range(1, N_DEV):
            g = g + cond_t[:, b:b + 1] * dm[b:b + 1, :]
        g_ref[0] = g
        d_ref[0], mo_ref[0], vo_ref[0] = _adamw(w_ref[0], g, m_ref[0], v_ref[0])

    wspec = pl.BlockSpec((1, tr, NC), lambda l, r: (l, r, 0))
    return pl.pallas_call(
        body, name="ada_bwd_adamw", grid=(L, D // tr),
        in_specs=[pl.BlockSpec((tr, N_DEV), lambda l, r: (r, 0)),
                  pl.BlockSpec((1, N_DEV, NC), lambda l, r: (l, 0, 0)), wspec, wspec, wspec],
        out_specs=[wspec] * 4, out_shape=[_sds(w.shape, F32)] * 4,
        compiler_params=pltpu.CompilerParams(dimension_semantics=("parallel", "parallel"), vmem_limit_bytes=VMEM_LIMIT),
    )(c_all_t, dmod_cols, w, m, v)


def _sum_devices(x):
    def body(x_ref, o_ref):
        s = x_ref[0]
        for j in range(1, N_DEV):
            s = s + x_ref[j]
        o_ref[...] = s

    return pl.pallas_call(body, name="sum_devices", out_shape=_sds(x.shape[1:], F32))(x)


def _adamw_small(w, g, m, v):
    def body(w_ref, g_ref, m_ref, v_ref, d_ref, mo_ref, vo_ref):
        d_ref[...], mo_ref[...], vo_ref[...] = _adamw(w_ref[...], g_ref[...], m_ref[...], v_ref[...])

    return pl.pallas_call(body, name="adamw_small", out_shape=[_sds(w.shape, F32)] * 3)(w, g, m, v)


def _me():
    return lax.axis_index("x") * 4 + lax.axis_index("y") * 2 + lax.axis_index("c")


def _peer(k):
    x, y, c = lax.axis_index("x"), lax.axis_index("y"), lax.axis_index("c")
    px = 1 - x if k & 4 else x
    py = 1 - y if k & 2 else y
    pc = 1 - c if k & 1 else c
    return (px, py, pc), px * 4 + py * 2 + pc


VMEM_SPEC = pl.BlockSpec(memory_space=pltpu.VMEM)
ANY_SPEC = pl.BlockSpec(memory_space=pl.ANY)


def _all_gather(name, x, out_dtype):
    R, C = x.shape
    cast = out_dtype != x.dtype

    def body(x_ref, out_ref, buf, send_sems, recv_sems, local_sem):
        me = _me()
        if cast:
            buf[...] = x_ref[...].astype(out_dtype)
            src = buf
        else:
            src = x_ref
        local = pltpu.make_async_copy(src, out_ref.at[me], local_sem)
        local.start()
        sends = []
        for k in range(1, N_DEV):
            dev, _ = _peer(k)
            cp = pltpu.make_async_remote_copy(src_ref=src, dst_ref=out_ref.at[me], send_sem=send_sems.at[k - 1],
                                              recv_sem=recv_sems.at[k - 1], device_id=dev, device_id_type=MESH_IDS)
            cp.start()
            sends.append(cp)
        for k in range(1, N_DEV):
            dev, pj = _peer(k)
            pltpu.make_async_remote_copy(src_ref=src, dst_ref=out_ref.at[pj], send_sem=send_sems.at[k - 1],
                                         recv_sem=recv_sems.at[k - 1], device_id=dev, device_id_type=MESH_IDS).wait_recv()
        for cp in sends:
            cp.wait_send()
        local.wait()

    return pl.pallas_call(
        body, name=name, in_specs=[VMEM_SPEC], out_specs=ANY_SPEC, out_shape=_sds((N_DEV, R, C), out_dtype),
        scratch_shapes=[pltpu.VMEM((R, C) if cast else (8, 128), out_dtype),
                        pltpu.SemaphoreType.DMA((N_DEV - 1,)), pltpu.SemaphoreType.DMA((N_DEV - 1,)),
                        pltpu.SemaphoreType.DMA(())],
        compiler_params=pltpu.CompilerParams(vmem_limit_bytes=VMEM_LIMIT),
    )(x)


def _all_to_all(name, x):
    _, R, C = x.shape

    def body(x_ref, out_ref, send_sems, recv_sems, local_sem):
        me = _me()
        local = pltpu.make_async_copy(x_ref.at[me], out_ref.at[me], local_sem)
        local.start()
        sends = []
        for k in range(1, N_DEV):
            dev, pj = _peer(k)
            cp = pltpu.make_async_remote_copy(src_ref=x_ref.at[pj], dst_ref=out_ref.at[me], send_sem=send_sems.at[k - 1],
                                              recv_sem=recv_sems.at[k - 1], device_id=dev, device_id_type=MESH_IDS)
            cp.start()
            sends.append(cp)
        for k in range(1, N_DEV):
            dev, pj = _peer(k)
            pltpu.make_async_remote_copy(src_ref=x_ref.at[pj], dst_ref=out_ref.at[pj], send_sem=send_sems.at[k - 1],
                                         recv_sem=recv_sems.at[k - 1], device_id=dev, device_id_type=MESH_IDS).wait_recv()
        for cp in sends:
            cp.wait_send()
        local.wait()

    return pl.pallas_call(
        body, name=name, in_specs=[VMEM_SPEC], out_specs=VMEM_SPEC, out_shape=_sds(x.shape, x.dtype),
        scratch_shapes=[pltpu.SemaphoreType.DMA((N_DEV - 1,)), pltpu.SemaphoreType.DMA((N_DEV - 1,)),
                        pltpu.SemaphoreType.DMA(())],
    )(x)


def _reduce_scatter_adamw(name, gblk, w, m, v):
    _, R, C = gblk.shape
    rows = 8
    for cand in (136, 128, 80, 64, 40, 32, 16, 8):
        if R % cand == 0:
            rows = cand
            break

    def body(g_ref, w_ref, m_ref, v_ref, go_ref, d_ref, mo_ref, vo_ref, recv, send_sems, recv_sems, local_sem):
        me = _me()
        local = pltpu.make_async_copy(g_ref.at[me], recv.at[me], local_sem)
        local.start()
        sends = []
        for k in range(1, N_DEV):
            dev, pj = _peer(k)
            cp = pltpu.make_async_remote_copy(src_ref=g_ref.at[pj], dst_ref=recv.at[me], send_sem=send_sems.at[k - 1],
                                              recv_sem=recv_sems.at[k - 1], device_id=dev, device_id_type=MESH_IDS)
            cp.start()
            sends.append(cp)
        for k in range(1, N_DEV):
            dev, pj = _peer(k)
            pltpu.make_async_remote_copy(src_ref=g_ref.at[pj], dst_ref=recv.at[pj], send_sem=send_sems.at[k - 1],
                                         recv_sem=recv_sems.at[k - 1], device_id=dev, device_id_type=MESH_IDS).wait_recv()
        local.wait()

        def chunk(i, carry):
            r = pl.ds(pl.multiple_of(i * rows, rows), rows)
            g = recv[0, r, :].astype(F32)
            for j in range(1, N_DEV):
                g = g + recv[j, r, :].astype(F32)
            go_ref[r, :] = g
            d_ref[r, :], mo_ref[r, :], vo_ref[r, :] = _adamw(w_ref[r, :], g, m_ref[r, :], v_ref[r, :])
            return carry

        lax.fori_loop(0, R // rows, chunk, 0)
        for cp in sends:
            cp.wait_send()

    return pl.pallas_call(
        body, name=name, in_specs=[ANY_SPEC, VMEM_SPEC, VMEM_SPEC, VMEM_SPEC], out_specs=[VMEM_SPEC] * 4,
        out_shape=[_sds((R, C), F32)] * 4,
        scratch_shapes=[pltpu.VMEM((N_DEV, R, C), BF16), pltpu.SemaphoreType.DMA((N_DEV - 1,)),
                        pltpu.SemaphoreType.DMA((N_DEV - 1,)), pltpu.SemaphoreType.DMA(())],
        compiler_params=pltpu.CompilerParams(vmem_limit_bytes=VMEM_LIMIT),
    )(gblk, w, m, v)


BIG = ["mla_w_dq", "mla_w_uq", "mla_w_dkv", "mla_w_ukv", "mla_w_o", "swa_w_qkv", "swa_w_o", "w_ff1", "w_ff2"]
ROW_SHARDED = {"mla_w_dq", "mla_w_dkv", "mla_w_o", "swa_w_o", "w_ff2"}


def _unblock(name, blocks):
    sh = blocks.shape[1:]
    if name in ROW_SHARDED:
        return jnp.moveaxis(blocks, 0, 1).reshape(sh[0], N_DEV * sh[1], sh[2])
    return jnp.moveaxis(blocks, 0, 2).reshape(sh[0], sh[1], N_DEV * sh[2])


def _block(name, full):
    L, K, N = full.shape
    if name in ROW_SHARDED:
        return jnp.moveaxis(full.reshape(L, N_DEV, K // N_DEV, N), 1, 0)
    return jnp.moveaxis(full.reshape(L, K, N_DEV, N // N_DEV), 2, 0)


def _rot_cols(w):
    half = QK_ROPE // 2
    return jnp.concatenate([-w[..., half:], w[..., :half]], axis=-1)


def _unrot_cols(gw):
    half = QK_ROPE // 2
    return jnp.concatenate([gw[..., half:], -gw[..., :half]], axis=-1)


def _row(v):
    return v.reshape(1, -1)


def _mlp_block_bwd(dx, sv, w1, w2, g, sc, gt):
    dy, du, dgt = _mlp_bwd_a(dx, sv["y2"], sv["rl"], gt, w2)
    dw2 = _matmul_tn("dw_ff2", sv["act"], dy)
    dw1 = _matmul_tn("dw_ff1", sv["h2"], du)
    dxo, dsh, da = _mlp_bwd_b(du, sv["x1"], dx, w1, g, sc)
    return dxo, dw1, dw2, dsh, da, dgt


def kernel(x, c, positions, w_ada, b_ada, g_mix, g_mlp, mla_w_dq, mla_g_q, mla_w_uq, mla_w_dkv, mla_g_kv, mla_w_ukv, mla_w_o, swa_w_qkv, swa_b_qkv, swa_sinks, swa_w_o, swa_b_o, w_ff1, w_ff2, g_final, loss_target, m_w_ada, m_b_ada, m_g_mix, m_g_mlp, m_mla_w_dq, m_mla_g_q, m_mla_w_uq, m_mla_w_dkv, m_mla_g_kv, m_mla_w_ukv, m_mla_w_o, m_swa_w_qkv, m_swa_b_qkv, m_swa_sinks, m_swa_w_o, m_swa_b_o, m_w_ff1, m_w_ff2, m_g_final, v_w_ada, v_b_ada, v_g_mix, v_g_mlp, v_mla_w_dq, v_mla_g_q, v_mla_w_uq, v_mla_w_dkv, v_mla_g_kv, v_mla_w_ukv, v_mla_w_o, v_swa_w_qkv, v_swa_b_qkv, v_swa_sinks, v_swa_w_o, v_swa_b_o, v_w_ff1, v_w_ff2, v_g_final):
    S, D = x.shape[1], x.shape[2]
    me = _me()
    x0 = x[0]
    target = loss_target[0]
    big_w = dict(mla_w_dq=mla_w_dq, mla_w_uq=mla_w_uq, mla_w_dkv=mla_w_dkv, mla_w_ukv=mla_w_ukv, mla_w_o=mla_w_o,
                 swa_w_qkv=swa_w_qkv, swa_w_o=swa_w_o, w_ff1=w_ff1, w_ff2=w_ff2)
    big_m = dict(mla_w_dq=m_mla_w_dq, mla_w_uq=m_mla_w_uq, mla_w_dkv=m_mla_w_dkv, mla_w_ukv=m_mla_w_ukv,
                 mla_w_o=m_mla_w_o, swa_w_qkv=m_swa_w_qkv, swa_w_o=m_swa_w_o, w_ff1=m_w_ff1, w_ff2=m_w_ff2)
    big_v = dict(mla_w_dq=v_mla_w_dq, mla_w_uq=v_mla_w_uq, mla_w_dkv=v_mla_w_dkv, mla_w_ukv=v_mla_w_ukv,
                 mla_w_o=v_mla_w_o, swa_w_qkv=v_swa_w_qkv, swa_w_o=v_swa_w_o, w_ff1=v_w_ff1, w_ff2=v_w_ff2)
    sizes = [big_w[n].size for n in BIG]
    offs = np.concatenate([[0], np.cumsum(sizes)])
    total = int(offs[-1])
    pack_rows = -(-total // (PACK_COLS * RS_CHUNKS * 16)) * RS_CHUNKS * 16

    def pack(parts):
        flat = jnp.concatenate([p.reshape(-1) for p in parts])
        return jnp.pad(flat, (0, pack_rows * PACK_COLS - total)).reshape(pack_rows, PACK_COLS)

    def unpack(flat2d, lead=()):
        flat = flat2d.reshape(lead + (-1,))
        return {n: flat[..., int(offs[i]):int(offs[i + 1])].reshape(lead + big_w[n].shape) for i, n in enumerate(BIG)}

    gathered = _all_gather("gather_weights", pack([big_w[n] for n in BIG]), BF16)
    wfull = {n: _unblock(n, b) for n, b in unpack(gathered, (N_DEV,)).items()}
    w_dq, w_dkv = wfull["mla_w_dq"][0], wfull["mla_w_dkv"][0]
    w_cat = jnp.concatenate([w_dq, w_dkv, _rot_cols(w_dkv[:, KV_LORA:])], axis=1)
    QL = w_dq.shape[1]
    w_uq = wfull["mla_w_uq"][0].reshape(QL, MLA_HEADS, QK_DIM)
    w_uqx = jnp.concatenate([w_uq, _rot_cols(w_uq[..., QK_NOPE:])], axis=-1).reshape(QL, MLA_HEADS * 256)
    w_ukv = wfull["mla_w_ukv"][0]
    w_o_mla, w_qkv, w_o_swa = wfull["mla_w_o"][0], wfull["swa_w_qkv"][0], wfull["swa_w_o"][0]
    ff1, ff2 = wfull["w_ff1"], wfull["w_ff2"]

    L = w_ada.shape[0]
    NC = w_ada.shape[2]
    nbq, nbo = swa_b_qkv.shape[1], swa_b_o.shape[1]
    cpad = -(-(D + nbq + nbo) // 1024) * 1024
    cpack = jnp.pad(jnp.concatenate([c[0], swa_b_qkv[0], swa_b_o[0]]), (0, cpad - (D + nbq + nbo))).reshape(8, cpad // 8)
    call = _all_gather("gather_c", cpack, F32).reshape(N_DEV, cpad)
    c_all = call[:, :D]
    b_qkv_full = call[:, D:D + nbq].reshape(1, N_DEV * nbq)
    b_o_full = call[:, D + nbq:D + nbq + nbo].reshape(1, N_DEV * nbo)
    mod_cols = _ada_fwd(c_all, w_ada)
    mpad = -(-(L * NC) // 1024) * 1024
    mod_send = jnp.pad(jnp.moveaxis(mod_cols, 1, 0).reshape(N_DEV, L * NC), ((0, 0), (0, mpad - L * NC)))
    mod_mine = _all_to_all("exchange_mod", mod_send.reshape(N_DEV, 8, mpad // 8)).reshape(N_DEV, mpad)[:, :L * NC]
    mod = jnp.moveaxis(mod_mine.reshape(N_DEV, L, NC), 0, 1).reshape(L, N_DEV * NC) + b_ada
    mods = mod.reshape(L, 6, 1, D)

    half = QK_ROPE // 2
    inv_freq = ROPE_THETA ** (-jnp.arange(half, dtype=F32) / half)
    ang = positions[0].astype(F32)[:, None] * inv_freq
    cos = jnp.concatenate([jnp.cos(ang), jnp.cos(ang)], axis=-1)
    sin = jnp.concatenate([jnp.sin(ang), jnp.sin(ang)], axis=-1)

    T_ATT = ATT_TILE
    zero_bias = jnp.zeros((1, D), F32)

    sh1, sc1, gt1, sh2, sc2, gt2 = [mods[0, i] for i in range(6)]
    gm0, gp0 = _row(g_mix[0]), _row(g_mlp[0])
    h1, cqp, cq, ckvp, ckv, q, k, v, vt = _mla_in_fwd(x0, cos, sin, gm0, sc1, sh1, w_cat, mla_g_q, w_uqx, mla_g_kv,
                                                      w_ukv, T_ATT)
    o0, lse0 = _mla_attn_fwd(q, k, vt, T_ATT)
    y1, x1, h2 = _attn_out_fwd(o0, x0, w_o_mla, zero_bias, gt1, gp0, sc2, sh2)
    rl0, act0, y2, x2 = _mlp_fwd(h2, x1, ff1[0], ff2[0], gt2)
    sv0 = dict(y2=y2, rl=rl0, act=act0, h2=h2, x1=x1)

    th1, tc1, tg1, th2, tc2, tg2 = [mods[1, i] for i in range(6)]
    gm1, gp1 = _row(g_mix[1]), _row(g_mlp[1])
    h3, sq, sk, svv = _swa_in_fwd(x2, gm1, tc1, th1, w_qkv, b_qkv_full)
    o1, lse1 = _swa_attn_fwd(sq, sk, svv, swa_sinks)
    y3, x3, h4 = _attn_out_fwd(o1, x2, w_o_swa, b_o_full, tg1, gp1, tc2, th2)
    rl1, act1, y4, x4 = _mlp_fwd(h4, x3, ff1[1], ff2[1], tg2)
    sv1 = dict(y2=y4, rl=rl1, act=act1, h2=h4, x1=x3)
    dx4, loss_part, dg_final = _final_loss(x4, target, _row(g_final))

    dx3, dw1_1, dw2_1, dsh2_1, da2_1, dgt2_1 = _mlp_block_bwd(dx4, sv1, ff1[1], ff2[1], gp1, tc2, tg2)
    dy, do, dl, dgt1_1, db_o = _attn_out_bwd(dx3, y3, o1, tg1, w_o_swa, SWA_HEADS)
    dw_o_swa = _matmul_tn("dw_o", o1, dy)
    dsq, dsk, dsv, dsink = _swa_attn_bwd(sq, sk, svv, do, lse1, dl, swa_sinks)
    dqkv, dx2, db_qkv, dsh1_1, da1_1 = _swa_in_bwd(dsq, dsk, dsv, x2, dx3, w_qkv, gm1, tc1)
    dw_qkv = _matmul_tn("dw_qkv", h3, dqkv)

    dx1, dw1_0, dw2_0, dsh2_0, da2_0, dgt2_0 = _mlp_block_bwd(dx2, sv0, ff1[0], ff2[0], gp0, sc2, gt2)
    dy, do, dl, dgt1_0, _ = _attn_out_bwd(dx1, y1, o0, gt1, w_o_mla, MLA_HEADS)
    dw_o_mla = _matmul_tn("dw_o", o0, dy)
    tb = min(T_ATT, S)
    delta = dl.T.reshape(MLA_HEADS, S // tb, 1, tb)
    dq, dk, dv = _mla_attn_bwd(q, k, v, do, lse0 * LOG2E, delta, T_ATT)
    dqx, dkv, dcat, dx0, dg_q, dg_kv, dsh1_0, da1_0 = _mla_in_bwd(
        dq, dk, dv, cos, sin, cqp, ckvp, x0, dx1, w_uqx, mla_g_q, w_ukv, mla_g_kv, w_cat, gm0, sc1)
    dw_uqx = _matmul_tn("dw_uq", cq, dqx).reshape(QL, MLA_HEADS, 256)
    dw_ukv = _matmul_tn("dw_ukv", ckv, dkv)
    dw_cat = _matmul_tn("dw_down", h1, dcat)
    dw_uq = jnp.concatenate([dw_uqx[..., :QK_NOPE], dw_uqx[..., 128:192] + _unrot_cols(dw_uqx[..., 192:256])],
                            axis=-1).reshape(QL, MLA_HEADS * QK_DIM)
    o_kr = QL + KV_LORA
    dw_dkv = jnp.concatenate([dw_cat[:, QL:o_kr],
                              dw_cat[:, o_kr:o_kr + QK_ROPE] + _unrot_cols(dw_cat[:, o_kr + QK_ROPE:])], axis=1)

    gfull = dict(mla_w_dq=dw_cat[None, :, :QL], mla_w_uq=dw_uq[None], mla_w_dkv=dw_dkv[None], mla_w_ukv=dw_ukv[None],
                 mla_w_o=dw_o_mla[None], swa_w_qkv=dw_qkv[None], swa_w_o=dw_o_swa[None],
                 w_ff1=jnp.stack([dw1_0, dw1_1]), w_ff2=jnp.stack([dw2_0, dw2_1]))
    gflat = jnp.concatenate([_block(n, gfull[n]).astype(BF16).reshape(N_DEV, -1) for n in BIG], axis=1)
    gpack = jnp.pad(gflat, ((0, 0), (0, pack_rows * PACK_COLS - total))).reshape(N_DEV, pack_rows, PACK_COLS)
    wpack, mpack, vpack = (pack([d[n] for n in BIG]) for d in (big_w, big_m, big_v))
    rc = pack_rows // RS_CHUNKS
    outs = [_reduce_scatter_adamw("grad_exchange_adamw", gpack[:, i * rc:(i + 1) * rc], wpack[i * rc:(i + 1) * rc],
                                  mpack[i * rc:(i + 1) * rc], vpack[i * rc:(i + 1) * rc]) for i in range(RS_CHUNKS)]
    big_g, big_d, big_nm, big_nv = (unpack(jnp.concatenate([o[j] for o in outs], axis=0)) for j in range(4))

    dmod = jnp.stack([
        jnp.concatenate([dsh1_0, gm0 * da1_0, dgt1_0, dsh2_0, gp0 * da2_0, dgt2_0], axis=1),
        jnp.concatenate([dsh1_1, gm1 * da1_1, dgt1_1, dsh2_1, gp1 * da2_1, dgt2_1], axis=1)]).reshape(-1)
    dg_mix = jnp.concatenate([(1.0 + sc1) * da1_0, (1.0 + tc1) * da1_1], axis=1).reshape(-1)
    dg_mlp = jnp.concatenate([(1.0 + sc2) * da2_0, (1.0 + tc2) * da2_1], axis=1).reshape(-1)
    parts = [loss_part.reshape(-1), dmod, dg_mix, dg_mlp, dg_q.reshape(-1), dg_kv.reshape(-1), dsink.reshape(-1),
             dg_final.reshape(-1), db_qkv.reshape(-1), db_o.reshape(-1)]
    soffs = np.concatenate([[0], np.cumsum([p.size for p in parts])])
    spad = -(-int(soffs[-1]) // 1024) * 1024
    spack = jnp.pad(jnp.concatenate(parts), (0, spad - int(soffs[-1]))).reshape(8, spad // 8)
    sall = _all_gather("gather_small_grads", spack, F32)
    ssum = _sum_devices(sall).reshape(-1)
    tot = [ssum[int(soffs[i]):int(soffs[i + 1])] for i in range(len(parts))]
    loss = tot[0][0]
    nsink = swa_sinks.shape[1]
    small_g = dict(b_ada=tot[1].reshape(b_ada.shape), g_mix=tot[2].reshape(g_mix.shape), g_mlp=tot[3].reshape(g_mlp.shape),
                   mla_g_q=tot[4].reshape(mla_g_q.shape), mla_g_kv=tot[5].reshape(mla_g_kv.shape),
                   swa_sinks=tot[6][:nsink].reshape(swa_sinks.shape), g_final=tot[7].reshape(g_final.shape),
                   swa_b_qkv=lax.dynamic_slice(tot[8], (me * nbq,), (nbq,)).reshape(swa_b_qkv.shape),
                   swa_b_o=lax.dynamic_slice(tot[9], (me * nbo,), (nbo,)).reshape(swa_b_o.shape))
    small_w = dict(b_ada=b_ada, g_mix=g_mix, g_mlp=g_mlp, mla_g_q=mla_g_q, mla_g_kv=mla_g_kv, swa_sinks=swa_sinks,
                   g_final=g_final, swa_b_qkv=swa_b_qkv, swa_b_o=swa_b_o)
    small_m = dict(b_ada=m_b_ada, g_mix=m_g_mix, g_mlp=m_g_mlp, mla_g_q=m_mla_g_q, mla_g_kv=m_mla_g_kv,
                   swa_sinks=m_swa_sinks, g_final=m_g_final, swa_b_qkv=m_swa_b_qkv, swa_b_o=m_swa_b_o)
    small_v = dict(b_ada=v_b_ada, g_mix=v_g_mix, g_mlp=v_g_mlp, mla_g_q=v_mla_g_q, mla_g_kv=v_mla_g_kv,
                   swa_sinks=v_swa_sinks, g_final=v_g_final, swa_b_qkv=v_swa_b_qkv, swa_b_o=v_swa_b_o)
    SMALL = list(small_w)
    woffs = np.concatenate([[0], np.cumsum([small_w[n].size for n in SMALL])])
    wpad = -(-int(woffs[-1]) // 1024) * 1024

    def spack_of(d):
        flat = jnp.concatenate([d[n].reshape(-1) for n in SMALL])
        return jnp.pad(flat, (0, wpad - int(woffs[-1]))).reshape(8, wpad // 8)

    sm = _adamw_small(spack_of(small_w), spack_of(small_g), spack_of(small_m), spack_of(small_v))
    small_d, small_nm, small_nv = (
        {n: a.reshape(-1)[int(woffs[i]):int(woffs[i + 1])].reshape(small_w[n].shape) for i, n in enumerate(SMALL)}
        for a in sm)

    b_off = int(soffs[1])
    dmod_all = sall.reshape(N_DEV, -1)[:, b_off:b_off + L * N_DEV * NC].reshape(N_DEV, L, N_DEV * NC)
    dmod_cols = jnp.moveaxis(lax.dynamic_slice_in_dim(dmod_all, me * NC, NC, axis=2), 0, 1)
    ada_g, ada_d, ada_nm, ada_nv = _ada_bwd_adamw(c_all.T, dmod_cols, w_ada, m_w_ada, v_w_ada)

    order = ["w_ada", "b_ada", "g_mix", "g_mlp", "mla_w_dq", "mla_g_q", "mla_w_uq", "mla_w_dkv", "mla_g_kv",
             "mla_w_ukv", "mla_w_o", "swa_w_qkv", "swa_b_qkv", "swa_sinks", "swa_w_o", "swa_b_o", "w_ff1", "w_ff2", "g_final"]

    def collect(ada, big, small):
        return [ada if n == "w_ada" else (big[n] if n in big else small[n]) for n in order]

    return (loss, dx0.reshape(x.shape), *collect(ada_g, big_g, small_g), *collect(ada_d, big_d, small_d),
            *collect(ada_nm, big_nm, small_nm), *collect(ada_nv, big_nv, small_nv))
```

```python
import functools

import jax
import jax.numpy as jnp
import numpy as np
from jax import lax
from jax.experimental import pallas as pl
from jax.experimental.pallas import tpu as pltpu

F32 = jnp.float32
BF16 = jnp.bfloat16
MESH_IDS = pl.DeviceIdType.MESH
N_DEV = 8

MLA_HEADS = 8
QK_NOPE = 128
QK_ROPE = 64
QK_DIM = QK_NOPE + QK_ROPE
V_DIM = 128
KV_LORA = 256
ROPE_THETA = 10000.0
SWA_HEADS = 16
SWA_KV_HEADS = 4
SWA_GROUP = SWA_HEADS // SWA_KV_HEADS
SWA_HEAD_DIM = 64
WINDOW = 128
EPS = 1e-6
LOG2E = 1.4426950408889634

ADAM_LR = 0.001
ADAM_B1 = 0.9
ADAM_B2 = 0.999
ADAM_EPS = 1e-08
ADAM_WD = 0.01
ADAM_STEP = 10

PACK_COLS = 1024
RS_CHUNKS = 4
VMEM_LIMIT = 56 << 20
ROW_TILE = 512
ROW_TILE_WIDE = 256
ATT_TILE = 512


def _dot(a, b):
    return jnp.dot(a, b, preferred_element_type=F32)


def _dot_nt(a, b):
    return lax.dot_general(a, b, (((1,), (1,)), ((), ())), preferred_element_type=F32)


def _dot_tn(a, b):
    return lax.dot_general(a, b, (((0,), (0,)), ((), ())), preferred_element_type=F32)


def _rstd(x):
    return lax.rsqrt(jnp.mean(x * x, axis=-1, keepdims=True) + EPS)


def _rms_bwd(dn, n, r):
    return r * (dn - n * jnp.mean(dn * n, axis=-1, keepdims=True))


def _modulate(x, g, sc, sh):
    r = _rstd(x)
    return ((x * r) * g) * (1.0 + sc) + sh


def _modulate_bwd(dh, x, g, sc):
    r = _rstd(x)
    n = x * r
    dsh = jnp.sum(dh, axis=0, keepdims=True)
    da = jnp.sum(dh * n, axis=0, keepdims=True)
    dx = _rms_bwd(dh * (g * (1.0 + sc)), n, r)
    return dx, dsh, da


def _first(i):
    return i == 0


def _acc(ref, val, i):
    @pl.when(i == 0)
    def _():
        ref[...] = val

    @pl.when(i != 0)
    def _():
        ref[...] += val


def _row_spec(shape, tm):
    nd = len(shape)
    return pl.BlockSpec(tuple(shape[:nd - 2]) + (tm, shape[-1]), lambda i: (0,) * (nd - 2) + (i, 0))


def _resident_spec(shape, single_buffer):
    nd = len(shape)
    if single_buffer:
        return pl.BlockSpec(tuple(shape), lambda i: (0,) * nd, pipeline_mode=pl.Buffered(1))
    return pl.BlockSpec(tuple(shape), lambda i: (0,) * nd)


def _rowcall(name, body, tokens, tm, row_in, full_in, row_out, acc_out=()):
    tm = min(tm, tokens)
    in_specs = [_row_spec(a.shape, tm) for a in row_in] + [_resident_spec(a.shape, True) for a in full_in]
    row_specs = [s[1] if isinstance(s, tuple) else _row_spec(s.shape, tm) for s in row_out]
    row_out = [s[0] if isinstance(s, tuple) else s for s in row_out]
    out_specs = row_specs + [_resident_spec(s.shape, False) for s in acc_out]
    return pl.pallas_call(
        body, name=name, grid=(tokens // tm,), in_specs=in_specs, out_specs=out_specs,
        out_shape=list(row_out) + list(acc_out),
        compiler_params=pltpu.CompilerParams(dimension_semantics=("arbitrary",), vmem_limit_bytes=VMEM_LIMIT),
    )(*row_in, *full_in)


def _sds(shape, dtype):
    return jax.ShapeDtypeStruct(tuple(shape), dtype)


def _mla_in_fwd(x, cos, sin, g, sc, sh, w_cat, g_q, w_uqx, g_kv, w_ukv, t):
    S, D = x.shape
    QL = g_q.shape[1]
    H = MLA_HEADS
    t = min(t, S)

    def body(x_ref, cos_ref, sin_ref, g_ref, sc_ref, sh_ref, wcat_ref, gq_ref, wuqx_ref, gkv_ref, wukv_ref,
             h_ref, cqp_ref, cq_ref, ckvp_ref, ckv_ref, q_ref, k_ref, v_ref, vt_ref):
        cs, sn = cos_ref[...], sin_ref[...]
        hb = _modulate(x_ref[...], g_ref[...], sc_ref[...], sh_ref[...]).astype(BF16)
        h_ref[...] = hb
        low = _dot(hb, wcat_ref[...])
        cqp = low[:, :QL]
        cqp_ref[...] = cqp
        cq = ((cqp * _rstd(cqp)) * gq_ref[...]).astype(BF16)
        cq_ref[...] = cq
        ckvp = low[:, QL:QL + KV_LORA]
        ckvp_ref[...] = ckvp
        ckv = ((ckvp * _rstd(ckvp)) * gkv_ref[...]).astype(BF16)
        ckv_ref[...] = ckv
        o = QL + KV_LORA
        kr = (low[:, o:o + QK_ROPE] * cs + low[:, o + QK_ROPE:o + 2 * QK_ROPE] * sn).astype(BF16)
        qx = _dot(cq, wuqx_ref[...])
        kv = _dot(ckv, wukv_ref[...])
        for hd in range(H):
            b = hd * 256
            q_ref[hd, :, 0:QK_NOPE] = qx[:, b:b + QK_NOPE].astype(BF16)
            q_ref[hd, :, QK_NOPE:QK_DIM] = (qx[:, b + 128:b + 192] * cs + qx[:, b + 192:b + 256] * sn).astype(BF16)
            k_ref[hd, :, 0:QK_NOPE] = kv[:, b:b + QK_NOPE].astype(BF16)
            k_ref[hd, :, QK_NOPE:QK_DIM] = kr
            vh = kv[:, b + 128:b + 256]
            v_ref[hd] = vh.astype(BF16)
            vt_ref[hd, 0, 0:V_DIM, :] = vh.T.astype(BF16)
            vt_ref[hd, 0, V_DIM:2 * V_DIM, :] = jnp.ones((V_DIM, x_ref.shape[0]), BF16)

    vt_spec = pl.BlockSpec((H, 1, 2 * V_DIM, t), lambda i: (0, i, 0, 0))
    return _rowcall(
        "mla_in_fwd", body, S, t, [x, cos, sin], [g, sc, sh, w_cat, g_q, w_uqx, g_kv, w_ukv],
        [_sds((S, D), BF16), _sds((S, QL), F32), _sds((S, QL), BF16), _sds((S, KV_LORA), F32), _sds((S, KV_LORA), BF16),
         _sds((H, S, QK_DIM), BF16), _sds((H, S, QK_DIM), BF16), _sds((H, S, V_DIM), BF16),
         (_sds((H, S // t, 2 * V_DIM, t), BF16), vt_spec)])


def _mla_attn_fwd(q, k, vt, t, send):
    H, S, DQ = q.shape
    DV = V_DIM
    t = min(t, S)
    nb = S // t
    scale = QK_DIM ** -0.5
    c2 = scale * LOG2E

    def body(q_ref, k_ref, vt_ref, send_ref, o_ref, lse_ref, gath_ref, m_s, acc_s, send_sems, recv_sems, local_sem):
        hd, qi = pl.program_id(0), pl.program_id(1)

        def gather():
            return _exchange_copies(lambda j: send_ref, gath_ref, send_sems, recv_sems, local_sem)

        @pl.when((hd == 0) & (qi == 0))
        def _():
            _start_exchange(gather())

        m_s[...] = jnp.full_like(m_s, -jnp.inf)
        acc_s[...] = jnp.zeros_like(acc_s)

        def tile(j, diagonal):
            rows = pl.ds(pl.multiple_of(j * t, t), t)
            s = _dot_nt(k_ref[0, rows, :], q_ref[0])
            if diagonal:
                key = lax.broadcasted_iota(jnp.int32, (t, t), 0)
                qry = lax.broadcasted_iota(jnp.int32, (t, t), 1)
                s = jnp.where(key <= qry, s, -jnp.inf)
            m_prev = m_s[...]
            m_new = jnp.maximum(m_prev, jnp.max(s, axis=0, keepdims=True))
            alpha = jnp.exp2((m_prev - m_new) * c2)
            p = jnp.exp2((s - m_new) * c2)
            acc_s[...] = alpha * acc_s[...] + _dot(vt_ref[0, j], p.astype(BF16))
            m_s[...] = m_new

        def off_diagonal(j, carry):
            tile(j, False)
            return carry

        lax.fori_loop(0, qi, off_diagonal, 0)
        tile(qi, True)
        acc = acc_s[...]
        o_ref[...] = (acc[:DV] / acc[DV:]).T.astype(BF16)
        lse_ref[0, 0] = m_s[...] * scale + jnp.log(acc[DV:DV + 1])

        @pl.when((hd == H - 1) & (qi == nb - 1))
        def _():
            _finish_exchange(gather())

    return pl.pallas_call(
        body, name="mla_attn_fwd", grid=(H, nb),
        in_specs=[pl.BlockSpec((1, t, DQ), lambda h, i: (h, i, 0)),
                  pl.BlockSpec((1, S, DQ), lambda h, i: (h, 0, 0)),
                  pl.BlockSpec((1, nb, 2 * DV, t), lambda h, i: (h, 0, 0, 0)), ANY_SPEC],
        out_specs=[pl.BlockSpec((t, DV), lambda h, i: (i, h)),
                   pl.BlockSpec((1, 1, 1, t), lambda h, i: (h, i, 0, 0)), ANY_SPEC],
        out_shape=[_sds((S, H * DV), BF16), _sds((H, nb, 1, t), F32), _sds((N_DEV,) + send.shape, send.dtype)],
        scratch_shapes=[pltpu.VMEM((1, t), F32), pltpu.VMEM((2 * DV, t), F32)] + COMM_SEMS,
        compiler_params=pltpu.CompilerParams(dimension_semantics=("arbitrary", "arbitrary"),
                                             vmem_limit_bytes=VMEM_LIMIT),
    )(q, k, vt, send)


def _attn_out_fwd(o, x, w_o, b_o, gt, g, sc, sh):
    S, D = x.shape

    def body(o_ref, x_ref, wo_ref, bo_ref, gt_ref, g_ref, sc_ref, sh_ref, y_ref, x1_ref, h_ref):
        y = _dot(o_ref[...], wo_ref[...]) + bo_ref[...]
        y_ref[...] = y
        x1 = x_ref[...] + gt_ref[...] * y
        x1_ref[...] = x1
        h_ref[...] = _modulate(x1, g_ref[...], sc_ref[...], sh_ref[...]).astype(BF16)

    return _rowcall("attn_out_fwd", body, S, ROW_TILE, [o, x], [w_o, b_o, gt, g, sc, sh],
                    [_sds((S, D), F32), _sds((S, D), F32), _sds((S, D), BF16)])


def _mlp_fwd(h, x, w1, w2, gt):
    S, D = x.shape
    FF = w1.shape[1]

    def body(h_ref, x_ref, w1_ref, w2_ref, gt_ref, rl_ref, act_ref, y_ref, x2_ref):
        rl = jnp.maximum(_dot(h_ref[...], w1_ref[...]), 0.0)
        rl_ref[...] = rl.astype(BF16)
        act = (rl * rl).astype(BF16)
        act_ref[...] = act
        y = _dot(act, w2_ref[...])
        y_ref[...] = y
        x2_ref[...] = x_ref[...] + gt_ref[...] * y

    return _rowcall("mlp_fwd", body, S, ROW_TILE_WIDE, [h, x], [w1, w2, gt],
                    [_sds((S, FF), BF16), _sds((S, FF), BF16), _sds((S, D), F32), _sds((S, D), F32)])


def _swa_in_fwd(x, g, sc, sh, w_qkv, b_qkv):
    S, D = x.shape
    NQ = SWA_HEADS * SWA_HEAD_DIM
    NK = SWA_KV_HEADS * SWA_HEAD_DIM

    def body(x_ref, g_ref, sc_ref, sh_ref, w_ref, b_ref, h_ref, q_ref, k_ref, v_ref):
        hb = _modulate(x_ref[...], g_ref[...], sc_ref[...], sh_ref[...]).astype(BF16)
        h_ref[...] = hb
        qkv = _dot(hb, w_ref[...]) + b_ref[...]
        q_ref[...] = qkv[:, :NQ].astype(BF16)
        k_ref[...] = qkv[:, NQ:NQ + NK].astype(BF16)
        v_ref[...] = qkv[:, NQ + NK:].astype(BF16)

    return _rowcall("swa_in_fwd", body, S, ROW_TILE, [x], [g, sc, sh, w_qkv, b_qkv],
                    [_sds((S, D), BF16), _sds((S, NQ), BF16), _sds((S, NK), BF16), _sds((S, NK), BF16)])


def _alibi_slope(head):
    return float(np.float32(2.0 ** (-8.0 * (head + 1) / SWA_HEADS)))


def _swa_geometry(n):
    W = WINDOW
    row = lax.broadcasted_iota(jnp.int32, (W, 2 * W), 0)
    col = lax.broadcasted_iota(jnp.int32, (W, 2 * W), 1)
    dist = W + row - col
    valid = (dist >= 0) & (dist < W) & ((n > 0) | (col >= W))
    return dist.astype(F32), valid


def _swa_band_specs(W, nb, cols):
    prev = pl.BlockSpec((W, cols), lambda n: (jnp.maximum(jnp.minimum(n, nb - 1) - 1, 0), 0))
    cur = pl.BlockSpec((W, cols), lambda n: (jnp.minimum(n, nb - 1), 0))
    return prev, cur


def _swa_attn_fwd(q, k, v, sinks):
    S, NQ = q.shape
    NK = k.shape[1]
    W, Dh, G = WINDOW, SWA_HEAD_DIM, SWA_GROUP
    nb = S // W

    def body(q_ref, kp_ref, kc_ref, vp_ref, vc_ref, sink_ref, o_ref, lse_ref):
        distf, valid = _swa_geometry(pl.program_id(0))
        for kh in range(SWA_KV_HEADS):
            ck = slice(kh * Dh, (kh + 1) * Dh)
            kb = jnp.concatenate([kp_ref[:, ck], kc_ref[:, ck]], axis=0)
            vb = jnp.concatenate([vp_ref[:, ck], vc_ref[:, ck]], axis=0)
            for gi in range(G):
                hq = kh * G + gi
                cq = slice(hq * Dh, (hq + 1) * Dh)
                s = _dot_nt(q_ref[:, cq], kb) * (Dh ** -0.5) - _alibi_slope(hq) * distf
                s = jnp.where(valid, s, -jnp.inf)
                sink = sink_ref[:, hq:hq + 1]
                m = jnp.maximum(jnp.max(s, axis=-1, keepdims=True), sink)
                p = jnp.exp(s - m)
                denom = jnp.sum(p, axis=-1, keepdims=True) + jnp.exp(sink - m)
                o_ref[:, cq] = _dot((p * (1.0 / denom)).astype(BF16), vb).astype(BF16)
                lse_ref[:, hq:hq + 1] = m + jnp.log(denom)

    kprev, kcur = _swa_band_specs(W, nb, NK)
    return pl.pallas_call(
        body, name="swa_attn_fwd", grid=(nb,),
        in_specs=[pl.BlockSpec((W, NQ), lambda n: (n, 0)), kprev, kcur, kprev, kcur,
                  pl.BlockSpec((1, SWA_HEADS), lambda n: (0, 0))],
        out_specs=[pl.BlockSpec((W, NQ), lambda n: (n, 0)), pl.BlockSpec((W, SWA_HEADS), lambda n: (n, 0))],
        out_shape=[_sds((S, NQ), BF16), _sds((S, SWA_HEADS), F32)],
        compiler_params=pltpu.CompilerParams(dimension_semantics=("arbitrary",), vmem_limit_bytes=VMEM_LIMIT),
    )(q, k, k, v, v, sinks)


def _final_loss(x, target, g):
    S, D = x.shape

    def body(x_ref, t_ref, g_ref, dx_ref, loss_ref, dg_ref):
        i = pl.program_id(0)
        xv = x_ref[...]
        r = _rstd(xv)
        n = xv * r
        err = n * g_ref[...] - t_ref[...]
        part = 0.5 * jnp.sum(jnp.mean(err * err, axis=-1, keepdims=True), axis=0, keepdims=True)
        _acc(loss_ref, jnp.broadcast_to(part, loss_ref.shape), i)
        dout = err / D
        _acc(dg_ref, jnp.sum(dout * n, axis=0, keepdims=True), i)
        dx_ref[...] = _rms_bwd(dout * g_ref[...], n, r)

    return _rowcall("final_loss", body, S, ROW_TILE, [x, target], [g], [_sds((S, D), F32)],
                    [_sds((1, 128), F32), _sds((1, D), F32)])


def _mlp_bwd_a(dx, y, rl, gt, w2):
    S, D = dx.shape
    FF = rl.shape[1]

    def body(dx_ref, y_ref, rl_ref, gt_ref, w2_ref, dy_ref, du_ref, dgt_ref):
        i = pl.program_id(0)
        dxv = dx_ref[...]
        _acc(dgt_ref, jnp.sum(dxv * y_ref[...], axis=0, keepdims=True), i)
        dy = (dxv * gt_ref[...]).astype(BF16)
        dy_ref[...] = dy
        dact = _dot_nt(dy, w2_ref[...])
        du_ref[...] = (dact * (2.0 * rl_ref[...].astype(F32))).astype(BF16)

    return _rowcall("mlp_bwd_a", body, S, ROW_TILE_WIDE, [dx, y, rl], [gt, w2],
                    [_sds((S, D), BF16), _sds((S, FF), BF16)], [_sds((1, D), F32)])


def _mlp_bwd_b(du, x, dx, w1, g, sc):
    S, D = x.shape

    def body(du_ref, x_ref, dx_ref, w1_ref, g_ref, sc_ref, dxo_ref, dsh_ref, da_ref):
        i = pl.program_id(0)
        dh = _dot_nt(du_ref[...], w1_ref[...])
        dxn, dsh, da = _modulate_bwd(dh, x_ref[...], g_ref[...], sc_ref[...])
        dxo_ref[...] = dx_ref[...] + dxn
        _acc(dsh_ref, dsh, i)
        _acc(da_ref, da, i)

    return _rowcall("mlp_bwd_b", body, S, ROW_TILE_WIDE, [du, x, dx], [w1, g, sc],
                    [_sds((S, D), F32)], [_sds((1, D), F32), _sds((1, D), F32)])


def _attn_out_bwd(dx, y, o, gt, w_o, n_heads):
    S, D = dx.shape
    NO = o.shape[1]
    dh = NO // n_heads

    def body(dx_ref, y_ref, o_ref, gt_ref, wo_ref, dy_ref, do_ref, dl_ref, dgt_ref, dbo_ref):
        i = pl.program_id(0)
        dxv = dx_ref[...]
        _acc(dgt_ref, jnp.sum(dxv * y_ref[...], axis=0, keepdims=True), i)
        dy = dxv * gt_ref[...]
        _acc(dbo_ref, jnp.sum(dy, axis=0, keepdims=True), i)
        dyb = dy.astype(BF16)
        dy_ref[...] = dyb
        do = _dot_nt(dyb, wo_ref[...])
        do_ref[...] = do.astype(BF16)
        prod = do * o_ref[...].astype(F32)
        for hd in range(n_heads):
            dl_ref[:, hd:hd + 1] = jnp.sum(prod[:, hd * dh:(hd + 1) * dh], axis=-1, keepdims=True)

    return _rowcall("attn_out_bwd", body, S, ROW_TILE, [dx, y, o], [gt, w_o],
                    [_sds((S, D), BF16), _sds((S, NO), BF16), _sds((S, n_heads), F32)],
                    [_sds((1, D), F32), _sds((1, D), F32)])


def _mla_attn_bwd(q, k, v, do, lse, delta, t, gblk):
    H, S, DQ = q.shape
    DV = V_DIM
    t = min(t, S)
    nb = S // t
    scale = QK_DIM ** -0.5
    c2 = scale * LOG2E

    def body(q_ref, k_ref, v_ref, do_ref, lse_ref, dl_ref, g_ref, dq_ref, dk_ref, dv_ref, recv_ref, dk_s, dv_s,
             send_sems, recv_sems, local_sem):
        hd, kj = pl.program_id(0), pl.program_id(1)

        def scatter():
            return _exchange_copies(lambda j: g_ref.at[j], recv_ref, send_sems, recv_sems, local_sem)

        @pl.when((hd == 0) & (kj == 0))
        def _():
            _start_exchange(scatter())

        @pl.when(kj == 0)
        def _():
            dq_ref[...] = jnp.zeros_like(dq_ref)

        dk_s[...] = jnp.zeros_like(dk_s)
        dv_s[...] = jnp.zeros_like(dv_s)

        def tile(i, diagonal):
            rows = pl.ds(pl.multiple_of(i * t, t), t)
            qb, dob, kb = q_ref[0, rows, :], do_ref[rows, :], k_ref[0]
            p = jnp.exp2(_dot_nt(kb, qb) * c2 - lse_ref[0, i])
            if diagonal:
                key = lax.broadcasted_iota(jnp.int32, (t, t), 0)
                qry = lax.broadcasted_iota(jnp.int32, (t, t), 1)
                p = jnp.where(key <= qry, p, 0.0)
            dv_s[...] += _dot(p.astype(BF16), dob)
            dp = _dot_nt(v_ref[0], dob)
            ds = (p * (dp - dl_ref[0, i])).astype(BF16)
            dk_s[...] += _dot(ds, qb)
            dq_ref[0, rows, :] += _dot_tn(ds, kb)

        def off_diagonal(i, carry):
            tile(i, False)
            return carry

        tile(kj, True)
        lax.fori_loop(kj + 1, nb, off_diagonal, 0)
        dk_ref[0] = (dk_s[...] * scale).astype(BF16)
        dv_ref[0] = dv_s[...].astype(BF16)

        @pl.when((hd == H - 1) & (kj == nb - 1))
        def _():
            _finish_exchange(scatter())

    rowspec = pl.BlockSpec((1, nb, 1, t), lambda h, j: (h, 0, 0, 0))
    return pl.pallas_call(
        body, name="mla_attn_bwd", grid=(H, nb),
        in_specs=[pl.BlockSpec((1, S, DQ), lambda h, j: (h, 0, 0)),
                  pl.BlockSpec((1, t, DQ), lambda h, j: (h, j, 0)),
                  pl.BlockSpec((1, t, DV), lambda h, j: (h, j, 0)),
                  pl.BlockSpec((S, DV), lambda h, j: (0, h)), rowspec, rowspec, ANY_SPEC],
        out_specs=[pl.BlockSpec((1, S, DQ), lambda h, j: (h, 0, 0)),
                   pl.BlockSpec((1, t, DQ), lambda h, j: (h, j, 0)),
                   pl.BlockSpec((1, t, DV), lambda h, j: (h, j, 0)), ANY_SPEC],
        out_shape=[_sds((H, S, DQ), F32), _sds((H, S, DQ), BF16), _sds((H, S, DV), BF16), _sds(gblk.shape, gblk.dtype)],
        scratch_shapes=[pltpu.VMEM((t, DQ), F32), pltpu.VMEM((t, DV), F32)] + COMM_SEMS,
        compiler_params=pltpu.CompilerParams(dimension_semantics=("arbitrary", "arbitrary"),
                                             vmem_limit_bytes=VMEM_LIMIT),
    )(q, k, v, do, lse, delta, gblk)


def _swa_attn_bwd(q, k, v, do, lse, delta, sinks):
    S, NQ = q.shape
    NK = k.shape[1]
    W, Dh, G = WINDOW, SWA_HEAD_DIM, SWA_GROUP
    nb = S // W

    def body(q_ref, kp_ref, kc_ref, vp_ref, vc_ref, do_ref, lse_ref, dl_ref, sink_ref,
             dq_ref, dk_ref, dv_ref, dsink_ref, dkc_s, dvc_s):
        n = pl.program_id(0)

        @pl.when(n == 0)
        def _():
            dkc_s[...] = jnp.zeros_like(dkc_s)
            dvc_s[...] = jnp.zeros_like(dvc_s)
            dsink_ref[...] = jnp.zeros_like(dsink_ref)

        @pl.when(n < nb)
        def _():
            distf, valid = _swa_geometry(n)
            for kh in range(SWA_KV_HEADS):
                ck = slice(kh * Dh, (kh + 1) * Dh)
                kb = jnp.concatenate([kp_ref[:, ck], kc_ref[:, ck]], axis=0)
                vb = jnp.concatenate([vp_ref[:, ck], vc_ref[:, ck]], axis=0)
                dkb = jnp.zeros((2 * W, Dh), F32)
                dvb = jnp.zeros((2 * W, Dh), F32)
                for gi in range(G):
                    hq = kh * G + gi
                    cq = slice(hq * Dh, (hq + 1) * Dh)
                    qh, doh = q_ref[:, cq], do_ref[:, cq]
                    lse_h = lse_ref[:, hq:hq + 1]
                    dl_h = dl_ref[:, hq:hq + 1]
                    s = _dot_nt(qh, kb) * (Dh ** -0.5) - _alibi_slope(hq) * distf
                    p = jnp.where(valid, jnp.exp(s - lse_h), 0.0)
                    dvb = dvb + _dot_tn(p.astype(BF16), doh)
                    dp = _dot_nt(doh, vb)
                    dsb = ((p * (dp - dl_h)) * (Dh ** -0.5)).astype(BF16)
                    dq_ref[:, cq] = _dot(dsb, kb).astype(BF16)
                    dkb = dkb + _dot_tn(dsb, qh)
                    psink = jnp.exp(sink_ref[:, hq:hq + 1] - lse_h)
                    dsink_ref[:, hq:hq + 1] += -jnp.sum(psink * dl_h, axis=0, keepdims=True)
                dk_ref[:, ck] = (dkc_s[:, ck] + dkb[:W]).astype(BF16)
                dv_ref[:, ck] = (dvc_s[:, ck] + dvb[:W]).astype(BF16)
                dkc_s[:, ck] = dkb[W:]
                dvc_s[:, ck] = dvb[W:]

        @pl.when(n == nb)
        def _():
            dk_ref[...] = dkc_s[...].astype(BF16)
            dv_ref[...] = dvc_s[...].astype(BF16)

    kprev, kcur = _swa_band_specs(W, nb, NK)
    qspec = lambda cols: pl.BlockSpec((W, cols), lambda n: (jnp.minimum(n, nb - 1), 0))
    kvout = pl.BlockSpec((W, NK), lambda n: (jnp.maximum(n - 1, 0), 0))
    return pl.pallas_call(
        body, name="swa_attn_bwd", grid=(nb + 1,),
        in_specs=[qspec(NQ), kprev, kcur, kprev, kcur, qspec(NQ), qspec(SWA_HEADS), qspec(SWA_HEADS),
                  pl.BlockSpec((1, SWA_HEADS), lambda n: (0, 0))],
        out_specs=[qspec(NQ), kvout, kvout, pl.BlockSpec((1, 128), lambda n: (0, 0))],
        out_shape=[_sds((S, NQ), BF16), _sds((S, NK), BF16), _sds((S, NK), BF16), _sds((1, 128), F32)],
        scratch_shapes=[pltpu.VMEM((W, NK), F32), pltpu.VMEM((W, NK), F32)],
        compiler_params=pltpu.CompilerParams(dimension_semantics=("arbitrary",), vmem_limit_bytes=VMEM_LIMIT),
    )(q, k, k, v, v, do, lse, delta, sinks)


def _swa_in_bwd(dq, dk, dv, x, dx, w_qkv, g, sc):
    S, D = x.shape
    N = w_qkv.shape[1]

    def body(dq_ref, dk_ref, dv_ref, x_ref, dx_ref, w_ref, g_ref, sc_ref, dqkv_ref, dxo_ref, db_ref, dsh_ref, da_ref):
        i = pl.program_id(0)
        dqkv = jnp.concatenate([dq_ref[...], dk_ref[...], dv_ref[...]], axis=1)
        dqkv_ref[...] = dqkv
        _acc(db_ref, jnp.sum(dqkv.astype(F32), axis=0, keepdims=True), i)
        dh = _dot_nt(dqkv, w_ref[...])
        dxn, dsh, da = _modulate_bwd(dh, x_ref[...], g_ref[...], sc_ref[...])
        dxo_ref[...] = dx_ref[...] + dxn
        _acc(dsh_ref, dsh, i)
        _acc(da_ref, da, i)

    return _rowcall("swa_in_bwd", body, S, ROW_TILE, [dq, dk, dv, x, dx], [w_qkv, g, sc],
                    [_sds((S, N), BF16), _sds((S, D), F32)],
                    [_sds((1, N), F32), _sds((1, D), F32), _sds((1, D), F32)])


def _mla_in_bwd(dq, dk, dv, cos, sin, cqp, ckvp, x, dx, w_uqx, g_q, w_ukv, g_kv, w_cat, g, sc):
    S, D = x.shape
    H = MLA_HEADS
    QL = g_q.shape[1]
    NX = w_uqx.shape[1]
    NC = w_cat.shape[1]

    def body(dq_ref, dk_ref, dv_ref, cos_ref, sin_ref, cqp_ref, ckvp_ref, x_ref, dx_ref,
             wuqx_ref, gq_ref, wukv_ref, gkv_ref, wcat_ref, g_ref, sc_ref,
             dqx_ref, dkv_ref, dcat_ref, dxo_ref, dgq_ref, dgkv_ref, dsh_ref, da_ref):
        i = pl.program_id(0)
        cs, sn = cos_ref[...], sin_ref[...]
        dkr = jnp.zeros(cs.shape, F32)
        for hd in range(H):
            b = hd * 256
            dqh = dq_ref[hd] * (QK_DIM ** -0.5)
            dqx_ref[:, b:b + QK_NOPE] = dqh[:, :QK_NOPE].astype(BF16)
            dqx_ref[:, b + 128:b + 192] = (dqh[:, QK_NOPE:] * cs).astype(BF16)
            dqx_ref[:, b + 192:b + 256] = (dqh[:, QK_NOPE:] * sn).astype(BF16)
            dkh = dk_ref[hd]
            dkv_ref[:, b:b + QK_NOPE] = dkh[:, :QK_NOPE]
            dkv_ref[:, b + 128:b + 256] = dv_ref[hd]
            dkr = dkr + dkh[:, QK_NOPE:].astype(F32)
        dcq = _dot_nt(dqx_ref[...], wuqx_ref[...])
        cqp = cqp_ref[...]
        rq = _rstd(cqp)
        nq = cqp * rq
        _acc(dgq_ref, jnp.sum(dcq * nq, axis=0, keepdims=True), i)
        dcqp = _rms_bwd(dcq * gq_ref[...], nq, rq)
        dckv = _dot_nt(dkv_ref[...], wukv_ref[...])
        ckvp = ckvp_ref[...]
        rk = _rstd(ckvp)
        nk = ckvp * rk
        _acc(dgkv_ref, jnp.sum(dckv * nk, axis=0, keepdims=True), i)
        dckvp = _rms_bwd(dckv * gkv_ref[...], nk, rk)
        dcat_ref[:, :QL] = dcqp.astype(BF16)
        dcat_ref[:, QL:QL + KV_LORA] = dckvp.astype(BF16)
        o = QL + KV_LORA
        dcat_ref[:, o:o + QK_ROPE] = (dkr * cs).astype(BF16)
        dcat_ref[:, o + QK_ROPE:o + 2 * QK_ROPE] = (dkr * sn).astype(BF16)
        dh = _dot_nt(dcat_ref[...], wcat_ref[...])
        dxn, dsh, da = _modulate_bwd(dh, x_ref[...], g_ref[...], sc_ref[...])
        dxo_ref[...] = dx_ref[...] + dxn
        _acc(dsh_ref, dsh, i)
        _acc(da_ref, da, i)

    return _rowcall("mla_in_bwd", body, S, ROW_TILE_WIDE, [dq, dk, dv, cos, sin, cqp, ckvp, x, dx],
                    [w_uqx, g_q, w_ukv, g_kv, w_cat, g, sc],
                    [_sds((S, NX), BF16), _sds((S, NX), BF16), _sds((S, NC), BF16), _sds((S, D), F32)],
                    [_sds((1, QL), F32), _sds((1, KV_LORA), F32), _sds((1, D), F32), _sds((1, D), F32)])


def _matmul_tn(name, a, b):
    S, K = a.shape
    N = b.shape[1]
    tk, tn, ts = min(K, 1024), min(N, 1024), min(S, ROW_TILE)
    if N % tn:
        tn = 512 if N % 512 == 0 else (384 if N % 384 == 0 else 128)
    if K % tk:
        tk = 512 if K % 512 == 0 else (384 if K % 384 == 0 else 128)
    ns = S // ts

    def body(a_ref, b_ref, o_ref):
        _acc(o_ref, _dot_tn(a_ref[...], b_ref[...]), pl.program_id(2))

    return pl.pallas_call(
        body, name=name, grid=(K // tk, N // tn, ns),
        in_specs=[pl.BlockSpec((ts, tk), lambda i, j, s: (s, i)), pl.BlockSpec((ts, tn), lambda i, j, s: (s, j))],
        out_specs=pl.BlockSpec((tk, tn), lambda i, j, s: (i, j)),
        out_shape=_sds((K, N), F32),
        compiler_params=pltpu.CompilerParams(dimension_semantics=("parallel", "parallel", "arbitrary"),
                                             vmem_limit_bytes=VMEM_LIMIT),
    )(a, b)


def _silu(c):
    return c * jax.nn.sigmoid(c)


def _ada_fwd(c_all, w_ada):
    L, D, NC = w_ada.shape

    def body(c_ref, w_ref, o_ref):
        cond = _silu(c_ref[...]).astype(BF16)
        o_ref[0] = _dot(cond, w_ref[0].astype(BF16))

    return pl.pallas_call(
        body, name="ada_fwd", grid=(L,),
        in_specs=[pl.BlockSpec(c_all.shape, lambda l: (0, 0)), pl.BlockSpec((1, D, NC), lambda l: (l, 0, 0))],
        out_specs=pl.BlockSpec((1, N_DEV, NC), lambda l: (l, 0, 0)),
        out_shape=_sds((L, N_DEV, NC), F32),
        compiler_params=pltpu.CompilerParams(dimension_semantics=("arbitrary",), vmem_limit_bytes=VMEM_LIMIT),
    )(c_all, w_ada)


def _adamw(w, g, m, v):
    m = ADAM_B1 * m + (1.0 - ADAM_B1) * g
    v = ADAM_B2 * v + (1.0 - ADAM_B2) * (g * g)
    m_hat = m / (1.0 - ADAM_B1 ** ADAM_STEP)
    v_hat = v / (1.0 - ADAM_B2 ** ADAM_STEP)
    delta = -ADAM_LR * (m_hat / (jnp.sqrt(v_hat) + ADAM_EPS) + ADAM_WD * w)
    return delta, m, v


def _ada_bwd_adamw(c_all_t, dmod_cols, w, m, v):
    L, D, NC = w.shape
    tr = min(D, 256)

    def body(ct_ref, dm_ref, w_ref, m_ref, v_ref, g_ref, d_ref, mo_ref, vo_ref):
        cond_t = _silu(ct_ref[...])
        dm = dm_ref[0]
        g = cond_t[:, 0:1] * dm[0:1, :]
        for b in range(1, N_DEV):
            g = g + cond_t[:, b:b + 1] * dm[b:b + 1, :]
        g_ref[0] = g
        d_ref[0], mo_ref[0], vo_ref[0] = _adamw(w_ref[0], g, m_ref[0], v_ref[0])

    wspec = pl.BlockSpec((1, tr, NC), lambda l, r: (l, r, 0))
    return pl.pallas_call(
        body, name="ada_bwd_adamw", grid=(L, D // tr),
        in_specs=[pl.BlockSpec((tr, N_DEV), lambda l, r: (r, 0)),
                  pl.BlockSpec((1, N_DEV, NC), lambda l, r: (l, 0, 0)), wspec, wspec, wspec],
        out_specs=[wspec] * 4, out_shape=[_sds(w.shape, F32)] * 4,
        compiler_params=pltpu.CompilerParams(dimension_semantics=("parallel", "parallel"), vmem_limit_bytes=VMEM_LIMIT),
    )(c_all_t, dmod_cols, w, m, v)


def _sum_devices(x):
    def body(x_ref, o_ref):
        s = x_ref[0]
        for j in range(1, N_DEV):
            s = s + x_ref[j]
        o_ref[...] = s

    return pl.pallas_call(body, name="sum_devices", out_shape=_sds(x.shape[1:], F32))(x)


def _adamw_small(w, g, m, v):
    def body(w_ref, g_ref, m_ref, v_ref, d_ref, mo_ref, vo_ref):
        d_ref[...], mo_ref[...], vo_ref[...] = _adamw(w_ref[...], g_ref[...], m_ref[...], v_ref[...])

    return pl.pallas_call(body, name="adamw_small", out_shape=[_sds(w.shape, F32)] * 3)(w, g, m, v)


def _me():
    return lax.axis_index("x") * 4 + lax.axis_index("y") * 2 + lax.axis_index("c")


def _peer(k):
    x, y, c = lax.axis_index("x"), lax.axis_index("y"), lax.axis_index("c")
    px = 1 - x if k & 4 else x
    py = 1 - y if k & 2 else y
    pc = 1 - c if k & 1 else c
    return (px, py, pc), px * 4 + py * 2 + pc


VMEM_SPEC = pl.BlockSpec(memory_space=pltpu.VMEM)
ANY_SPEC = pl.BlockSpec(memory_space=pl.ANY)
COMM_SEMS = [pltpu.SemaphoreType.DMA((N_DEV - 1,)), pltpu.SemaphoreType.DMA((N_DEV - 1,)), pltpu.SemaphoreType.DMA(())]


def _exchange_copies(src_of, dst_ref, send_sems, recv_sems, local_sem):
    me = _me()
    local = pltpu.make_async_copy(src_of(me), dst_ref.at[me], local_sem)
    sends, recvs = [], []
    for k in range(1, N_DEV):
        dev, pj = _peer(k)
        sems = dict(send_sem=send_sems.at[k - 1], recv_sem=recv_sems.at[k - 1], device_id=dev, device_id_type=MESH_IDS)
        sends.append(pltpu.make_async_remote_copy(src_ref=src_of(pj), dst_ref=dst_ref.at[me], **sems))
        recvs.append(pltpu.make_async_remote_copy(src_ref=src_of(pj), dst_ref=dst_ref.at[pj], **sems))
    return local, sends, recvs


def _start_exchange(copies):
    local, sends, _ = copies
    local.start()
    for cp in sends:
        cp.start()


def _finish_exchange(copies):
    local, sends, recvs = copies
    for cp in recvs:
        cp.wait_recv()
    for cp in sends:
        cp.wait_send()
    local.wait()


def _sum_adamw(recv, w, m, v):
    _, R, C = recv.shape
    rows = max(d for d in range(16, min(R, 256) + 1, 16) if R % d == 0)

    def body(r_ref, w_ref, m_ref, v_ref, go_ref, d_ref, mo_ref, vo_ref):
        g = r_ref[0].astype(F32)
        for j in range(1, N_DEV):
            g = g + r_ref[j].astype(F32)
        go_ref[...] = g
        d_ref[...], mo_ref[...], vo_ref[...] = _adamw(w_ref[...], g, m_ref[...], v_ref[...])

    spec = pl.BlockSpec((rows, C), lambda i: (i, 0))
    return pl.pallas_call(
        body, name="sum_adamw", grid=(R // rows,),
        in_specs=[pl.BlockSpec((N_DEV, rows, C), lambda i: (0, i, 0)), spec, spec, spec],
        out_specs=[spec] * 4, out_shape=[_sds((R, C), F32)] * 4,
        compiler_params=pltpu.CompilerParams(dimension_semantics=("parallel",), vmem_limit_bytes=VMEM_LIMIT),
    )(recv, w, m, v)


def _all_gather(name, x, out_dtype):
    R, C = x.shape
    cast = out_dtype != x.dtype

    def body(x_ref, out_ref, buf, send_sems, recv_sems, local_sem):
        me = _me()
        if cast:
            buf[...] = x_ref[...].astype(out_dtype)
            src = buf
        else:
            src = x_ref
        local = pltpu.make_async_copy(src, out_ref.at[me], local_sem)
        local.start()
        sends = []
        for k in range(1, N_DEV):
            dev, _ = _peer(k)
            cp = pltpu.make_async_remote_copy(src_ref=src, dst_ref=out_ref.at[me], send_sem=send_sems.at[k - 1],
                                              recv_sem=recv_sems.at[k - 1], device_id=dev, device_id_type=MESH_IDS)
            cp.start()
            sends.append(cp)
        for k in range(1, N_DEV):
            dev, pj = _peer(k)
            pltpu.make_async_remote_copy(src_ref=src, dst_ref=out_ref.at[pj], send_sem=send_sems.at[k - 1],
                                         recv_sem=recv_sems.at[k - 1], device_id=dev, device_id_type=MESH_IDS).wait_recv()
        for cp in sends:
            cp.wait_send()
        local.wait()

    return pl.pallas_call(
        body, name=name, in_specs=[VMEM_SPEC], out_specs=ANY_SPEC, out_shape=_sds((N_DEV, R, C), out_dtype),
        scratch_shapes=[pltpu.VMEM((R, C) if cast else (8, 128), out_dtype),
                        pltpu.SemaphoreType.DMA((N_DEV - 1,)), pltpu.SemaphoreType.DMA((N_DEV - 1,)),
                        pltpu.SemaphoreType.DMA(())],
        compiler_params=pltpu.CompilerParams(vmem_limit_bytes=VMEM_LIMIT),
    )(x)


def _all_to_all(name, x):
    _, R, C = x.shape

    def body(x_ref, out_ref, send_sems, recv_sems, local_sem):
        me = _me()
        local = pltpu.make_async_copy(x_ref.at[me], out_ref.at[me], local_sem)
        local.start()
        sends = []
        for k in range(1, N_DEV):
            dev, pj = _peer(k)
            cp = pltpu.make_async_remote_copy(src_ref=x_ref.at[pj], dst_ref=out_ref.at[me], send_sem=send_sems.at[k - 1],
                                              recv_sem=recv_sems.at[k - 1], device_id=dev, device_id_type=MESH_IDS)
            cp.start()
            sends.append(cp)
        for k in range(1, N_DEV):
            dev, pj = _peer(k)
            pltpu.make_async_remote_copy(src_ref=x_ref.at[pj], dst_ref=out_ref.at[pj], send_sem=send_sems.at[k - 1],
                                         recv_sem=recv_sems.at[k - 1], device_id=dev, device_id_type=MESH_IDS).wait_recv()
        for cp in sends:
            cp.wait_send()
        local.wait()

    return pl.pallas_call(
        body, name=name, in_specs=[VMEM_SPEC], out_specs=VMEM_SPEC, out_shape=_sds(x.shape, x.dtype),
        scratch_shapes=[pltpu.SemaphoreType.DMA((N_DEV - 1,)), pltpu.SemaphoreType.DMA((N_DEV - 1,)),
                        pltpu.SemaphoreType.DMA(())],
    )(x)


def _reduce_scatter_adamw(name, gblk, w, m, v):
    _, R, C = gblk.shape
    rows = 8
    for cand in (136, 128, 80, 64, 40, 32, 16, 8):
        if R % cand == 0:
            rows = cand
            break

    def body(g_ref, w_ref, m_ref, v_ref, go_ref, d_ref, mo_ref, vo_ref, recv, send_sems, recv_sems, local_sem):
        me = _me()
        local = pltpu.make_async_copy(g_ref.at[me], recv.at[me], local_sem)
        local.start()
        sends = []
        for k in range(1, N_DEV):
            dev, pj = _peer(k)
            cp = pltpu.make_async_remote_copy(src_ref=g_ref.at[pj], dst_ref=recv.at[me], send_sem=send_sems.at[k - 1],
                                              recv_sem=recv_sems.at[k - 1], device_id=dev, device_id_type=MESH_IDS)
            cp.start()
            sends.append(cp)
        for k in range(1, N_DEV):
            dev, pj = _peer(k)
            pltpu.make_async_remote_copy(src_ref=g_ref.at[pj], dst_ref=recv.at[pj], send_sem=send_sems.at[k - 1],
                                         recv_sem=recv_sems.at[k - 1], device_id=dev, device_id_type=MESH_IDS).wait_recv()
        local.wait()

        def chunk(i, carry):
            r = pl.ds(pl.multiple_of(i * rows, rows), rows)
            g = recv[0, r, :].astype(F32)
            for j in range(1, N_DEV):
                g = g + recv[j, r, :].astype(F32)
            go_ref[r, :] = g
            d_ref[r, :], mo_ref[r, :], vo_ref[r, :] = _adamw(w_ref[r, :], g, m_ref[r, :], v_ref[r, :])
            return carry

        lax.fori_loop(0, R // rows, chunk, 0)
        for cp in sends:
            cp.wait_send()

    return pl.pallas_call(
        body, name=name, in_specs=[ANY_SPEC, VMEM_SPEC, VMEM_SPEC, VMEM_SPEC], out_specs=[VMEM_SPEC] * 4,
        out_shape=[_sds((R, C), F32)] * 4,
        scratch_shapes=[pltpu.VMEM((N_DEV, R, C), BF16), pltpu.SemaphoreType.DMA((N_DEV - 1,)),
                        pltpu.SemaphoreType.DMA((N_DEV - 1,)), pltpu.SemaphoreType.DMA(())],
        compiler_params=pltpu.CompilerParams(vmem_limit_bytes=VMEM_LIMIT),
    )(gblk, w, m, v)


FIRST_WEIGHTS = ["mla_w_dq", "mla_w_uq", "mla_w_dkv", "mla_w_ukv", "mla_w_o"]
LATE_WEIGHTS = ["swa_w_qkv", "swa_w_o", "w_ff1", "w_ff2"]
ROW_SHARDED = {"mla_w_dq", "mla_w_dkv", "mla_w_o", "swa_w_o", "w_ff2"}


def _unblock(name, blocks):
    sh = blocks.shape[1:]
    if name in ROW_SHARDED:
        return jnp.moveaxis(blocks, 0, 1).reshape(sh[0], N_DEV * sh[1], sh[2])
    return jnp.moveaxis(blocks, 0, 2).reshape(sh[0], sh[1], N_DEV * sh[2])


def _block(name, full):
    L, K, N = full.shape
    if name in ROW_SHARDED:
        return jnp.moveaxis(full.reshape(L, N_DEV, K // N_DEV, N), 1, 0)
    return jnp.moveaxis(full.reshape(L, K, N_DEV, N // N_DEV), 2, 0)


def _rot_cols(w):
    half = QK_ROPE // 2
    return jnp.concatenate([-w[..., half:], w[..., :half]], axis=-1)


def _unrot_cols(gw):
    half = QK_ROPE // 2
    return jnp.concatenate([gw[..., half:], -gw[..., :half]], axis=-1)


def _row(v):
    return v.reshape(1, -1)


def _mlp_block_bwd(dx, sv, w1, w2, g, sc, gt):
    dy, du, dgt = _mlp_bwd_a(dx, sv["y2"], sv["rl"], gt, w2)
    dw2 = _matmul_tn("dw_ff2", sv["act"], dy)
    dw1 = _matmul_tn("dw_ff1", sv["h2"], du)
    dxo, dsh, da = _mlp_bwd_b(du, sv["x1"], dx, w1, g, sc)
    return dxo, dw1, dw2, dsh, da, dgt


def kernel(x, c, positions, w_ada, b_ada, g_mix, g_mlp, mla_w_dq, mla_g_q, mla_w_uq, mla_w_dkv, mla_g_kv, mla_w_ukv, mla_w_o, swa_w_qkv, swa_b_qkv, swa_sinks, swa_w_o, swa_b_o, w_ff1, w_ff2, g_final, loss_target, m_w_ada, m_b_ada, m_g_mix, m_g_mlp, m_mla_w_dq, m_mla_g_q, m_mla_w_uq, m_mla_w_dkv, m_mla_g_kv, m_mla_w_ukv, m_mla_w_o, m_swa_w_qkv, m_swa_b_qkv, m_swa_sinks, m_swa_w_o, m_swa_b_o, m_w_ff1, m_w_ff2, m_g_final, v_w_ada, v_b_ada, v_g_mix, v_g_mlp, v_mla_w_dq, v_mla_g_q, v_mla_w_uq, v_mla_w_dkv, v_mla_g_kv, v_mla_w_ukv, v_mla_w_o, v_swa_w_qkv, v_swa_b_qkv, v_swa_sinks, v_swa_w_o, v_swa_b_o, v_w_ff1, v_w_ff2, v_g_final):
    S, D = x.shape[1], x.shape[2]
    me = _me()
    x0 = x[0]
    target = loss_target[0]
    big_w = dict(mla_w_dq=mla_w_dq, mla_w_uq=mla_w_uq, mla_w_dkv=mla_w_dkv, mla_w_ukv=mla_w_ukv, mla_w_o=mla_w_o,
                 swa_w_qkv=swa_w_qkv, swa_w_o=swa_w_o, w_ff1=w_ff1, w_ff2=w_ff2)
    big_m = dict(mla_w_dq=m_mla_w_dq, mla_w_uq=m_mla_w_uq, mla_w_dkv=m_mla_w_dkv, mla_w_ukv=m_mla_w_ukv,
                 mla_w_o=m_mla_w_o, swa_w_qkv=m_swa_w_qkv, swa_w_o=m_swa_w_o, w_ff1=m_w_ff1, w_ff2=m_w_ff2)
    big_v = dict(mla_w_dq=v_mla_w_dq, mla_w_uq=v_mla_w_uq, mla_w_dkv=v_mla_w_dkv, mla_w_ukv=v_mla_w_ukv,
                 mla_w_o=v_mla_w_o, swa_w_qkv=v_swa_w_qkv, swa_w_o=v_swa_w_o, w_ff1=v_w_ff1, w_ff2=v_w_ff2)
    groups = {"first": FIRST_WEIGHTS, "late": LATE_WEIGHTS}
    offs = {g: np.concatenate([[0], np.cumsum([big_w[n].size for n in names])]).astype(int) for g, names in groups.items()}
    prow = {g: -(-int(offs[g][-1]) // (PACK_COLS * 16)) * 16 for g in groups}

    def pack(g, d):
        flat = jnp.concatenate([d[n].reshape(-1) for n in groups[g]])
        return jnp.pad(flat, (0, prow[g] * PACK_COLS - int(offs[g][-1]))).reshape(prow[g], PACK_COLS)

    def pack_blocks(g, gfull):
        flat = jnp.concatenate([_block(n, gfull[n]).astype(BF16).reshape(N_DEV, -1) for n in groups[g]], axis=1)
        return jnp.pad(flat, ((0, 0), (0, prow[g] * PACK_COLS - int(offs[g][-1])))).reshape(N_DEV, prow[g], PACK_COLS)

    def unpack(g, flat2d, lead=()):
        flat = flat2d.reshape(lead + (-1,))
        return {n: flat[..., int(offs[g][i]):int(offs[g][i + 1])].reshape(lead + big_w[n].shape)
                for i, n in enumerate(groups[g])}

    gathered = _all_gather("gather_weights", pack("first", big_w), BF16)
    wfull = {n: _unblock(n, b) for n, b in unpack("first", gathered, (N_DEV,)).items()}
    w_dq, w_dkv = wfull["mla_w_dq"][0], wfull["mla_w_dkv"][0]
    w_cat = jnp.concatenate([w_dq, w_dkv, _rot_cols(w_dkv[:, KV_LORA:])], axis=1)
    QL = w_dq.shape[1]
    w_uq = wfull["mla_w_uq"][0].reshape(QL, MLA_HEADS, QK_DIM)
    w_uqx = jnp.concatenate([w_uq, _rot_cols(w_uq[..., QK_NOPE:])], axis=-1).reshape(QL, MLA_HEADS * 256)
    w_ukv = wfull["mla_w_ukv"][0]
    w_o_mla = wfull["mla_w_o"][0]

    L = w_ada.shape[0]
    NC = w_ada.shape[2]
    nbq, nbo = swa_b_qkv.shape[1], swa_b_o.shape[1]
    cpad = -(-(D + nbq + nbo) // 1024) * 1024
    cpack = jnp.pad(jnp.concatenate([c[0], swa_b_qkv[0], swa_b_o[0]]), (0, cpad - (D + nbq + nbo))).reshape(8, cpad // 8)
    call = _all_gather("gather_c", cpack, F32).reshape(N_DEV, cpad)
    c_all = call[:, :D]
    b_qkv_full = call[:, D:D + nbq].reshape(1, N_DEV * nbq)
    b_o_full = call[:, D + nbq:D + nbq + nbo].reshape(1, N_DEV * nbo)
    mod_cols = _ada_fwd(c_all, w_ada)
    mpad = -(-(L * NC) // 1024) * 1024
    mod_send = jnp.pad(jnp.moveaxis(mod_cols, 1, 0).reshape(N_DEV, L * NC), ((0, 0), (0, mpad - L * NC)))
    mod_mine = _all_to_all("exchange_mod", mod_send.reshape(N_DEV, 8, mpad // 8)).reshape(N_DEV, mpad)[:, :L * NC]
    mod = jnp.moveaxis(mod_mine.reshape(N_DEV, L, NC), 0, 1).reshape(L, N_DEV * NC) + b_ada
    mods = mod.reshape(L, 6, 1, D)

    half = QK_ROPE // 2
    inv_freq = ROPE_THETA ** (-jnp.arange(half, dtype=F32) / half)
    ang = positions[0].astype(F32)[:, None] * inv_freq
    cos = jnp.concatenate([jnp.cos(ang), jnp.cos(ang)], axis=-1)
    sin = jnp.concatenate([jnp.sin(ang), jnp.sin(ang)], axis=-1)

    T_ATT = ATT_TILE
    zero_bias = jnp.zeros((1, D), F32)

    sh1, sc1, gt1, sh2, sc2, gt2 = [mods[0, i] for i in range(6)]
    gm0, gp0 = _row(g_mix[0]), _row(g_mlp[0])
    h1, cqp, cq, ckvp, ckv, q, k, v, vt = _mla_in_fwd(x0, cos, sin, gm0, sc1, sh1, w_cat, mla_g_q, w_uqx, mla_g_kv,
                                                      w_ukv, T_ATT)
    o0, lse0, gathered = _mla_attn_fwd(q, k, vt, T_ATT, pack("late", big_w).astype(BF16))
    wfull = {n: _unblock(n, b) for n, b in unpack("late", gathered, (N_DEV,)).items()}
    w_qkv, w_o_swa = wfull["swa_w_qkv"][0], wfull["swa_w_o"][0]
    ff1, ff2 = wfull["w_ff1"], wfull["w_ff2"]
    y1, x1, h2 = _attn_out_fwd(o0, x0, w_o_mla, zero_bias, gt1, gp0, sc2, sh2)
    rl0, act0, y2, x2 = _mlp_fwd(h2, x1, ff1[0], ff2[0], gt2)
    sv0 = dict(y2=y2, rl=rl0, act=act0, h2=h2, x1=x1)

    th1, tc1, tg1, th2, tc2, tg2 = [mods[1, i] for i in range(6)]
    gm1, gp1 = _row(g_mix[1]), _row(g_mlp[1])
    h3, sq, sk, svv = _swa_in_fwd(x2, gm1, tc1, th1, w_qkv, b_qkv_full)
    o1, lse1 = _swa_attn_fwd(sq, sk, svv, swa_sinks)
    y3, x3, h4 = _attn_out_fwd(o1, x2, w_o_swa, b_o_full, tg1, gp1, tc2, th2)
    rl1, act1, y4, x4 = _mlp_fwd(h4, x3, ff1[1], ff2[1], tg2)
    sv1 = dict(y2=y4, rl=rl1, act=act1, h2=h4, x1=x3)
    dx4, loss_part, dg_final = _final_loss(x4, target, _row(g_final))

    dx3, dw1_1, dw2_1, dsh2_1, da2_1, dgt2_1 = _mlp_block_bwd(dx4, sv1, ff1[1], ff2[1], gp1, tc2, tg2)
    dy, do, dl, dgt1_1, db_o = _attn_out_bwd(dx3, y3, o1, tg1, w_o_swa, SWA_HEADS)
    dw_o_swa = _matmul_tn("dw_o", o1, dy)
    dsq, dsk, dsv, dsink = _swa_attn_bwd(sq, sk, svv, do, lse1, dl, swa_sinks)
    dqkv, dx2, db_qkv, dsh1_1, da1_1 = _swa_in_bwd(dsq, dsk, dsv, x2, dx3, w_qkv, gm1, tc1)
    dw_qkv = _matmul_tn("dw_qkv", h3, dqkv)

    dx1, dw1_0, dw2_0, dsh2_0, da2_0, dgt2_0 = _mlp_block_bwd(dx2, sv0, ff1[0], ff2[0], gp0, sc2, gt2)
    dy, do, dl, dgt1_0, _ = _attn_out_bwd(dx1, y1, o0, gt1, w_o_mla, MLA_HEADS)
    dw_o_mla = _matmul_tn("dw_o", o0, dy)
    tb = min(T_ATT, S)
    delta = dl.T.reshape(MLA_HEADS, S // tb, 1, tb)
    glate = dict(swa_w_qkv=dw_qkv[None], swa_w_o=dw_o_swa[None], w_ff1=jnp.stack([dw1_0, dw1_1]),
                 w_ff2=jnp.stack([dw2_0, dw2_1]))
    dq, dk, dv, recv = _mla_attn_bwd(q, k, v, do, lse0 * LOG2E, delta, T_ATT, pack_blocks("late", glate))
    late = _sum_adamw(recv, pack("late", big_w), pack("late", big_m), pack("late", big_v))
    dqx, dkv, dcat, dx0, dg_q, dg_kv, dsh1_0, da1_0 = _mla_in_bwd(
        dq, dk, dv, cos, sin, cqp, ckvp, x0, dx1, w_uqx, mla_g_q, w_ukv, mla_g_kv, w_cat, gm0, sc1)
    dw_uqx = _matmul_tn("dw_uq", cq, dqx).reshape(QL, MLA_HEADS, 256)
    dw_ukv = _matmul_tn("dw_ukv", ckv, dkv)
    dw_cat = _matmul_tn("dw_down", h1, dcat)
    dw_uq = jnp.concatenate([dw_uqx[..., :QK_NOPE], dw_uqx[..., 128:192] + _unrot_cols(dw_uqx[..., 192:256])],
                            axis=-1).reshape(QL, MLA_HEADS * QK_DIM)
    o_kr = QL + KV_LORA
    dw_dkv = jnp.concatenate([dw_cat[:, QL:o_kr],
                              dw_cat[:, o_kr:o_kr + QK_ROPE] + _unrot_cols(dw_cat[:, o_kr + QK_ROPE:])], axis=1)

    gfirst = dict(mla_w_dq=dw_cat[None, :, :QL], mla_w_uq=dw_uq[None], mla_w_dkv=dw_dkv[None], mla_w_ukv=dw_ukv[None],
                  mla_w_o=dw_o_mla[None])
    first = _reduce_scatter_adamw("grad_exchange_adamw", pack_blocks("first", gfirst), pack("first", big_w),
                                  pack("first", big_m), pack("first", big_v))
    big_g, big_d, big_nm, big_nv = ({**unpack("first", first[j]), **unpack("late", late[j])} for j in range(4))

    dmod = jnp.stack([
        jnp.concatenate([dsh1_0, gm0 * da1_0, dgt1_0, dsh2_0, gp0 * da2_0, dgt2_0], axis=1),
        jnp.concatenate([dsh1_1, gm1 * da1_1, dgt1_1, dsh2_1, gp1 * da2_1, dgt2_1], axis=1)]).reshape(-1)
    dg_mix = jnp.concatenate([(1.0 + sc1) * da1_0, (1.0 + tc1) * da1_1], axis=1).reshape(-1)
    dg_mlp = jnp.concatenate([(1.0 + sc2) * da2_0, (1.0 + tc2) * da2_1], axis=1).reshape(-1)
    parts = [loss_part.reshape(-1), dmod, dg_mix, dg_mlp, dg_q.reshape(-1), dg_kv.reshape(-1), dsink.reshape(-1),
             dg_final.reshape(-1), db_qkv.reshape(-1), db_o.reshape(-1)]
    soffs = np.concatenate([[0], np.cumsum([p.size for p in parts])])
    spad = -(-int(soffs[-1]) // 1024) * 1024
    spack = jnp.pad(jnp.concatenate(parts), (0, spad - int(soffs[-1]))).reshape(8, spad // 8)
    sall = _all_gather("gather_small_grads", spack, F32)
    ssum = _sum_devices(sall).reshape(-1)
    tot = [ssum[int(soffs[i]):int(soffs[i + 1])] for i in range(len(parts))]
    loss = tot[0][0]
    nsink = swa_sinks.shape[1]
    small_g = dict(b_ada=tot[1].reshape(b_ada.shape), g_mix=tot[2].reshape(g_mix.shape), g_mlp=tot[3].reshape(g_mlp.shape),
                   mla_g_q=tot[4].reshape(mla_g_q.shape), mla_g_kv=tot[5].reshape(mla_g_kv.shape),
                   swa_sinks=tot[6][:nsink].reshape(swa_sinks.shape), g_final=tot[7].reshape(g_final.shape),
                   swa_b_qkv=lax.dynamic_slice(tot[8], (me * nbq,), (nbq,)).reshape(swa_b_qkv.shape),
                   swa_b_o=lax.dynamic_slice(tot[9], (me * nbo,), (nbo,)).reshape(swa_b_o.shape))
    small_w = dict(b_ada=b_ada, g_mix=g_mix, g_mlp=g_mlp, mla_g_q=mla_g_q, mla_g_kv=mla_g_kv, swa_sinks=swa_sinks,
                   g_final=g_final, swa_b_qkv=swa_b_qkv, swa_b_o=swa_b_o)
    small_m = dict(b_ada=m_b_ada, g_mix=m_g_mix, g_mlp=m_g_mlp, mla_g_q=m_mla_g_q, mla_g_kv=m_mla_g_kv,
                   swa_sinks=m_swa_sinks, g_final=m_g_final, swa_b_qkv=m_swa_b_qkv, swa_b_o=m_swa_b_o)
    small_v = dict(b_ada=v_b_ada, g_mix=v_g_mix, g_mlp=v_g_mlp, mla_g_q=v_mla_g_q, mla_g_kv=v_mla_g_kv,
                   swa_sinks=v_swa_sinks, g_final=v_g_final, swa_b_qkv=v_swa_b_qkv, swa_b_o=v_swa_b_o)
    SMALL = list(small_w)
    woffs = np.concatenate([[0], np.cumsum([small_w[n].size for n in SMALL])])
    wpad = -(-int(woffs[-1]) // 1024) * 1024

    def spack_of(d):
        flat = jnp.concatenate([d[n].reshape(-1) for n in SMALL])
        return jnp.pad(flat, (0, wpad - int(woffs[-1]))).reshape(8, wpad // 8)

    sm = _adamw_small(spack_of(small_w), spack_of(small_g), spack_of(small_m), spack_of(small_v))
    small_d, small_nm, small_nv = (
        {n: a.reshape(-1)[int(woffs[i]):int(woffs[i + 1])].reshape(small_w[n].shape) for i, n in enumerate(SMALL)}
        for a in sm)

    b_off = int(soffs[1])
    dmod_all = sall.reshape(N_DEV, -1)[:, b_off:b_off + L * N_DEV * NC].reshape(N_DEV, L, N_DEV * NC)
    dmod_cols = jnp.moveaxis(lax.dynamic_slice_in_dim(dmod_all, me * NC, NC, axis=2), 0, 1)
    ada_g, ada_d, ada_nm, ada_nv = _ada_bwd_adamw(c_all.T, dmod_cols, w_ada, m_w_ada, v_w_ada)

    order = ["w_ada", "b_ada", "g_mix", "g_mlp", "mla_w_dq", "mla_g_q", "mla_w_uq", "mla_w_dkv", "mla_g_kv",
             "mla_w_ukv", "mla_w_o", "swa_w_qkv", "swa_b_qkv", "swa_sinks", "swa_w_o", "swa_b_o", "w_ff1", "w_ff2", "g_final"]

    def collect(ada, big, small):
        return [ada if n == "w_ada" else (big[n] if n in big else small[n]) for n in order]

    return (loss, dx0.reshape(x.shape), *collect(ada_g, big_g, small_g), *collect(ada_d, big_d, small_d),
            *collect(ada_nm, big_nm, small_nm), *collect(ada_nv, big_nv, small_nv))
```

```python
import functools

import jax
import jax.numpy as jnp
import numpy as np
from jax import lax
from jax.experimental import pallas as pl
from jax.experimental.pallas import tpu as pltpu

F32 = jnp.float32
BF16 = jnp.bfloat16
MESH_IDS = pl.DeviceIdType.MESH
N_DEV = 8

MLA_HEADS = 8
QK_NOPE = 128
QK_ROPE = 64
QK_DIM = QK_NOPE + QK_ROPE
V_DIM = 128
KV_LORA = 256
ROPE_THETA = 10000.0
SWA_HEADS = 16
SWA_KV_HEADS = 4
SWA_GROUP = SWA_HEADS // SWA_KV_HEADS
SWA_HEAD_DIM = 64
WINDOW = 128
EPS = 1e-6
LOG2E = 1.4426950408889634

ADAM_LR = 0.001
ADAM_B1 = 0.9
ADAM_B2 = 0.999
ADAM_EPS = 1e-08
ADAM_WD = 0.01
ADAM_STEP = 10

PACK_COLS = 1024
RS_CHUNKS = 4
VMEM_LIMIT = 56 << 20
ROW_TILE = 512
ROW_TILE_WIDE = 256
ATT_TILE = 512


def _dot(a, b):
    return jnp.dot(a, b, preferred_element_type=F32)


def _dot_nt(a, b):
    return lax.dot_general(a, b, (((1,), (1,)), ((), ())), preferred_element_type=F32)


def _dot_tn(a, b):
    return lax.dot_general(a, b, (((0,), (0,)), ((), ())), preferred_element_type=F32)


def _rstd(x):
    return lax.rsqrt(jnp.mean(x * x, axis=-1, keepdims=True) + EPS)


def _rms_bwd(dn, n, r):
    return r * (dn - n * jnp.mean(dn * n, axis=-1, keepdims=True))


def _modulate(x, g, sc, sh):
    r = _rstd(x)
    return ((x * r) * g) * (1.0 + sc) + sh


def _modulate_bwd(dh, x, g, sc):
    r = _rstd(x)
    n = x * r
    dsh = jnp.sum(dh, axis=0, keepdims=True)
    da = jnp.sum(dh * n, axis=0, keepdims=True)
    dx = _rms_bwd(dh * (g * (1.0 + sc)), n, r)
    return dx, dsh, da


def _first(i):
    return i == 0


def _acc(ref, val, i):
    @pl.when(i == 0)
    def _():
        ref[...] = val

    @pl.when(i != 0)
    def _():
        ref[...] += val


def _row_spec(shape, tm):
    nd = len(shape)
    return pl.BlockSpec(tuple(shape[:nd - 2]) + (tm, shape[-1]), lambda i: (0,) * (nd - 2) + (i, 0))


def _resident_spec(shape, single_buffer):
    nd = len(shape)
    if single_buffer:
        return pl.BlockSpec(tuple(shape), lambda i: (0,) * nd, pipeline_mode=pl.Buffered(1))
    return pl.BlockSpec(tuple(shape), lambda i: (0,) * nd)


def _rowcall(name, body, tokens, tm, row_in, full_in, row_out, acc_out=()):
    tm = min(tm, tokens)
    in_specs = [_row_spec(a.shape, tm) for a in row_in] + [_resident_spec(a.shape, True) for a in full_in]
    row_specs = [s[1] if isinstance(s, tuple) else _row_spec(s.shape, tm) for s in row_out]
    row_out = [s[0] if isinstance(s, tuple) else s for s in row_out]
    out_specs = row_specs + [_resident_spec(s.shape, False) for s in acc_out]
    return pl.pallas_call(
        body, name=name, grid=(tokens // tm,), in_specs=in_specs, out_specs=out_specs,
        out_shape=list(row_out) + list(acc_out),
        compiler_params=pltpu.CompilerParams(dimension_semantics=("arbitrary",), vmem_limit_bytes=VMEM_LIMIT),
    )(*row_in, *full_in)


def _sds(shape, dtype):
    return jax.ShapeDtypeStruct(tuple(shape), dtype)


def _mla_in_fwd(x, cos, sin, g, sc, sh, w_cat, g_q, w_uqx, g_kv, w_ukv, t):
    S, D = x.shape
    QL = g_q.shape[1]
    H = MLA_HEADS
    t = min(t, S)

    def body(x_ref, cos_ref, sin_ref, g_ref, sc_ref, sh_ref, wcat_ref, gq_ref, wuqx_ref, gkv_ref, wukv_ref,
             h_ref, cqp_ref, cq_ref, ckvp_ref, ckv_ref, q_ref, k_ref, v_ref, vt_ref):
        cs, sn = cos_ref[...], sin_ref[...]
        hb = _modulate(x_ref[...], g_ref[...], sc_ref[...], sh_ref[...]).astype(BF16)
        h_ref[...] = hb
        low = _dot(hb, wcat_ref[...])
        cqp = low[:, :QL]
        cqp_ref[...] = cqp
        cq = ((cqp * _rstd(cqp)) * gq_ref[...]).astype(BF16)
        cq_ref[...] = cq
        ckvp = low[:, QL:QL + KV_LORA]
        ckvp_ref[...] = ckvp
        ckv = ((ckvp * _rstd(ckvp)) * gkv_ref[...]).astype(BF16)
        ckv_ref[...] = ckv
        o = QL + KV_LORA
        kr = (low[:, o:o + QK_ROPE] * cs + low[:, o + QK_ROPE:o + 2 * QK_ROPE] * sn).astype(BF16)
        qx = _dot(cq, wuqx_ref[...])
        kv = _dot(ckv, wukv_ref[...])
        for hd in range(H):
            b = hd * 256
            q_ref[hd, :, 0:QK_NOPE] = qx[:, b:b + QK_NOPE].astype(BF16)
            q_ref[hd, :, QK_NOPE:QK_DIM] = (qx[:, b + 128:b + 192] * cs + qx[:, b + 192:b + 256] * sn).astype(BF16)
            k_ref[hd, :, 0:QK_NOPE] = kv[:, b:b + QK_NOPE].astype(BF16)
            k_ref[hd, :, QK_NOPE:QK_DIM] = kr
            vh = kv[:, b + 128:b + 256]
            v_ref[hd] = vh.astype(BF16)
            vt_ref[hd, 0, 0:V_DIM, :] = vh.T.astype(BF16)
            vt_ref[hd, 0, V_DIM:2 * V_DIM, :] = jnp.ones((V_DIM, x_ref.shape[0]), BF16)

    vt_spec = pl.BlockSpec((H, 1, 2 * V_DIM, t), lambda i: (0, i, 0, 0))
    return _rowcall(
        "mla_in_fwd", body, S, t, [x, cos, sin], [g, sc, sh, w_cat, g_q, w_uqx, g_kv, w_ukv],
        [_sds((S, D), BF16), _sds((S, QL), F32), _sds((S, QL), BF16), _sds((S, KV_LORA), F32), _sds((S, KV_LORA), BF16),
         _sds((H, S, QK_DIM), BF16), _sds((H, S, QK_DIM), BF16), _sds((H, S, V_DIM), BF16),
         (_sds((H, S // t, 2 * V_DIM, t), BF16), vt_spec)])


def _mla_attn_fwd(q, k, vt, t, send):
    H, S, DQ = q.shape
    DV = V_DIM
    t = min(t, S)
    nb = S // t
    scale = QK_DIM ** -0.5
    c2 = scale * LOG2E

    def body(q_ref, k_ref, vt_ref, send_ref, o_ref, lse_ref, gath_ref, m_s, acc_s, send_sems, recv_sems, local_sem):
        hd, qi = pl.program_id(0), pl.program_id(1)

        def gather():
            return _exchange_copies(lambda j: send_ref, gath_ref, send_sems, recv_sems, local_sem)

        @pl.when((hd == 0) & (qi == 0))
        def _():
            _start_exchange(gather())

        m_s[...] = jnp.full_like(m_s, -jnp.inf)
        acc_s[...] = jnp.zeros_like(acc_s)

        def tile(j, diagonal):
            rows = pl.ds(pl.multiple_of(j * t, t), t)
            s = _dot_nt(k_ref[0, rows, :], q_ref[0])
            if diagonal:
                key = lax.broadcasted_iota(jnp.int32, (t, t), 0)
                qry = lax.broadcasted_iota(jnp.int32, (t, t), 1)
                s = jnp.where(key <= qry, s, -jnp.inf)
            m_prev = m_s[...]
            m_new = jnp.maximum(m_prev, jnp.max(s, axis=0, keepdims=True))
            alpha = jnp.exp2((m_prev - m_new) * c2)
            p = jnp.exp2((s - m_new) * c2)
            acc_s[...] = alpha * acc_s[...] + _dot(vt_ref[0, j], p.astype(BF16))
            m_s[...] = m_new

        def off_diagonal(j, carry):
            tile(j, False)
            return carry

        lax.fori_loop(0, qi, off_diagonal, 0)
        tile(qi, True)
        acc = acc_s[...]
        o_ref[...] = (acc[:DV] / acc[DV:]).T.astype(BF16)
        lse_ref[0, 0] = m_s[...] * scale + jnp.log(acc[DV:DV + 1])

        @pl.when((hd == H - 1) & (qi == nb - 1))
        def _():
            _finish_exchange(gather())

    return pl.pallas_call(
        body, name="mla_attn_fwd", grid=(H, nb),
        in_specs=[pl.BlockSpec((1, t, DQ), lambda h, i: (h, i, 0)),
                  pl.BlockSpec((1, S, DQ), lambda h, i: (h, 0, 0)),
                  pl.BlockSpec((1, nb, 2 * DV, t), lambda h, i: (h, 0, 0, 0)), ANY_SPEC],
        out_specs=[pl.BlockSpec((t, DV), lambda h, i: (i, h)),
                   pl.BlockSpec((1, 1, 1, t), lambda h, i: (h, i, 0, 0)), ANY_SPEC],
        out_shape=[_sds((S, H * DV), BF16), _sds((H, nb, 1, t), F32), _sds((N_DEV,) + send.shape, send.dtype)],
        scratch_shapes=[pltpu.VMEM((1, t), F32), pltpu.VMEM((2 * DV, t), F32)] + COMM_SEMS,
        compiler_params=pltpu.CompilerParams(dimension_semantics=("arbitrary", "arbitrary"),
                                             vmem_limit_bytes=VMEM_LIMIT),
    )(q, k, vt, send)


def _attn_out_fwd(o, x, w_o, b_o, gt, g, sc, sh):
    S, D = x.shape

    def body(o_ref, x_ref, wo_ref, bo_ref, gt_ref, g_ref, sc_ref, sh_ref, y_ref, x1_ref, h_ref):
        y = _dot(o_ref[...], wo_ref[...]) + bo_ref[...]
        y_ref[...] = y
        x1 = x_ref[...] + gt_ref[...] * y
        x1_ref[...] = x1
        h_ref[...] = _modulate(x1, g_ref[...], sc_ref[...], sh_ref[...]).astype(BF16)

    return _rowcall("attn_out_fwd", body, S, ROW_TILE, [o, x], [w_o, b_o, gt, g, sc, sh],
                    [_sds((S, D), F32), _sds((S, D), F32), _sds((S, D), BF16)])


def _mlp_fwd(h, x, w1, w2, gt):
    S, D = x.shape
    FF = w1.shape[1]

    def body(h_ref, x_ref, w1_ref, w2_ref, gt_ref, rl_ref, act_ref, y_ref, x2_ref):
        rl = jnp.maximum(_dot(h_ref[...], w1_ref[...]), 0.0)
        rl_ref[...] = rl.astype(BF16)
        act = (rl * rl).astype(BF16)
        act_ref[...] = act
        y = _dot(act, w2_ref[...])
        y_ref[...] = y
        x2_ref[...] = x_ref[...] + gt_ref[...] * y

    return _rowcall("mlp_fwd", body, S, ROW_TILE_WIDE, [h, x], [w1, w2, gt],
                    [_sds((S, FF), BF16), _sds((S, FF), BF16), _sds((S, D), F32), _sds((S, D), F32)])


def _swa_in_fwd(x, g, sc, sh, w_qkv, b_qkv):
    S, D = x.shape
    NQ = SWA_HEADS * SWA_HEAD_DIM
    NK = SWA_KV_HEADS * SWA_HEAD_DIM

    def body(x_ref, g_ref, sc_ref, sh_ref, w_ref, b_ref, h_ref, q_ref, k_ref, v_ref):
        hb = _modulate(x_ref[...], g_ref[...], sc_ref[...], sh_ref[...]).astype(BF16)
        h_ref[...] = hb
        qkv = _dot(hb, w_ref[...]) + b_ref[...]
        q_ref[...] = qkv[:, :NQ].astype(BF16)
        k_ref[...] = qkv[:, NQ:NQ + NK].astype(BF16)
        v_ref[...] = qkv[:, NQ + NK:].astype(BF16)

    return _rowcall("swa_in_fwd", body, S, ROW_TILE, [x], [g, sc, sh, w_qkv, b_qkv],
                    [_sds((S, D), BF16), _sds((S, NQ), BF16), _sds((S, NK), BF16), _sds((S, NK), BF16)])


def _alibi_slope(head):
    return float(np.float32(2.0 ** (-8.0 * (head + 1) / SWA_HEADS)))


def _swa_geometry(n):
    W = WINDOW
    key = lax.broadcasted_iota(jnp.int32, (2 * W, W), 0)
    qry = lax.broadcasted_iota(jnp.int32, (2 * W, W), 1)
    dist = W + qry - key
    valid = (dist >= 0) & (dist < W) & ((n > 0) | (key >= W))
    return dist.astype(F32), valid


def _swa_band_specs(W, nb, cols):
    prev = pl.BlockSpec((W, cols), lambda n: (jnp.maximum(jnp.minimum(n, nb - 1) - 1, 0), 0))
    cur = pl.BlockSpec((W, cols), lambda n: (jnp.minimum(n, nb - 1), 0))
    return prev, cur


def _swa_attn_fwd(q, k, v, sinks):
    S, NQ = q.shape
    NK = k.shape[1]
    W, Dh, G = WINDOW, SWA_HEAD_DIM, SWA_GROUP
    nb = S // W

    def body(q_ref, kp_ref, kc_ref, vp_ref, vc_ref, sink_ref, o_ref, lse_ref):
        distf, valid = _swa_geometry(pl.program_id(0))
        kband = jnp.concatenate([kp_ref[...], kc_ref[...]], axis=0)
        vband_t = jnp.concatenate([vp_ref[...], vc_ref[...]], axis=0).astype(F32).T.astype(BF16)
        outs = []
        for kh in range(SWA_KV_HEADS):
            kb = kband[:, kh * Dh:(kh + 1) * Dh]
            vbt = vband_t[kh * Dh:(kh + 1) * Dh, :]
            for gi in range(G):
                hq = kh * G + gi
                s = _dot_nt(kb, q_ref[:, hq * Dh:(hq + 1) * Dh]) * (Dh ** -0.5) - _alibi_slope(hq) * distf
                s = jnp.where(valid, s, -jnp.inf)
                sink = sink_ref[:, hq:hq + 1]
                m = jnp.maximum(jnp.max(s, axis=0, keepdims=True), sink)
                p = jnp.exp(s - m)
                denom = jnp.sum(p, axis=0, keepdims=True) + jnp.exp(sink - m)
                outs.append(_dot(vbt, (p * (1.0 / denom)).astype(BF16)))
                lse_ref[hq:hq + 1, :] = m + jnp.log(denom)
        o_ref[...] = jnp.concatenate(outs, axis=0).T.astype(BF16)

    kprev, kcur = _swa_band_specs(W, nb, NK)
    return pl.pallas_call(
        body, name="swa_attn_fwd", grid=(nb,),
        in_specs=[pl.BlockSpec((W, NQ), lambda n: (n, 0)), kprev, kcur, kprev, kcur,
                  pl.BlockSpec((1, SWA_HEADS), lambda n: (0, 0))],
        out_specs=[pl.BlockSpec((W, NQ), lambda n: (n, 0)), pl.BlockSpec((SWA_HEADS, W), lambda n: (0, n))],
        out_shape=[_sds((S, NQ), BF16), _sds((SWA_HEADS, S), F32)],
        compiler_params=pltpu.CompilerParams(dimension_semantics=("arbitrary",), vmem_limit_bytes=VMEM_LIMIT),
    )(q, k, k, v, v, sinks)


def _final_loss(x, target, g):
    S, D = x.shape

    def body(x_ref, t_ref, g_ref, dx_ref, loss_ref, dg_ref):
        i = pl.program_id(0)
        xv = x_ref[...]
        r = _rstd(xv)
        n = xv * r
        err = n * g_ref[...] - t_ref[...]
        part = 0.5 * jnp.sum(jnp.mean(err * err, axis=-1, keepdims=True), axis=0, keepdims=True)
        _acc(loss_ref, jnp.broadcast_to(part, loss_ref.shape), i)
        dout = err / D
        _acc(dg_ref, jnp.sum(dout * n, axis=0, keepdims=True), i)
        dx_ref[...] = _rms_bwd(dout * g_ref[...], n, r)

    return _rowcall("final_loss", body, S, ROW_TILE, [x, target], [g], [_sds((S, D), F32)],
                    [_sds((1, 128), F32), _sds((1, D), F32)])


def _mlp_bwd_a(dx, y, rl, gt, w2):
    S, D = dx.shape
    FF = rl.shape[1]

    def body(dx_ref, y_ref, rl_ref, gt_ref, w2_ref, dy_ref, du_ref, dgt_ref):
        i = pl.program_id(0)
        dxv = dx_ref[...]
        _acc(dgt_ref, jnp.sum(dxv * y_ref[...], axis=0, keepdims=True), i)
        dy = (dxv * gt_ref[...]).astype(BF16)
        dy_ref[...] = dy
        dact = _dot_nt(dy, w2_ref[...])
        du_ref[...] = (dact * (2.0 * rl_ref[...].astype(F32))).astype(BF16)

    return _rowcall("mlp_bwd_a", body, S, ROW_TILE_WIDE, [dx, y, rl], [gt, w2],
                    [_sds((S, D), BF16), _sds((S, FF), BF16)], [_sds((1, D), F32)])


def _mlp_bwd_b(du, x, dx, w1, g, sc):
    S, D = x.shape

    def body(du_ref, x_ref, dx_ref, w1_ref, g_ref, sc_ref, dxo_ref, dsh_ref, da_ref):
        i = pl.program_id(0)
        dh = _dot_nt(du_ref[...], w1_ref[...])
        dxn, dsh, da = _modulate_bwd(dh, x_ref[...], g_ref[...], sc_ref[...])
        dxo_ref[...] = dx_ref[...] + dxn
        _acc(dsh_ref, dsh, i)
        _acc(da_ref, da, i)

    return _rowcall("mlp_bwd_b", body, S, ROW_TILE_WIDE, [du, x, dx], [w1, g, sc],
                    [_sds((S, D), F32)], [_sds((1, D), F32), _sds((1, D), F32)])


def _attn_out_bwd(dx, y, o, gt, w_o, n_heads):
    S, D = dx.shape
    NO = o.shape[1]
    dh = NO // n_heads

    def body(dx_ref, y_ref, o_ref, gt_ref, wo_ref, dy_ref, do_ref, dl_ref, dgt_ref, dbo_ref):
        i = pl.program_id(0)
        dxv = dx_ref[...]
        _acc(dgt_ref, jnp.sum(dxv * y_ref[...], axis=0, keepdims=True), i)
        dy = dxv * gt_ref[...]
        _acc(dbo_ref, jnp.sum(dy, axis=0, keepdims=True), i)
        dyb = dy.astype(BF16)
        dy_ref[...] = dyb
        do = _dot_nt(dyb, wo_ref[...])
        do_ref[...] = do.astype(BF16)
        prod = do * o_ref[...].astype(F32)
        for hd in range(n_heads):
            dl_ref[:, hd:hd + 1] = jnp.sum(prod[:, hd * dh:(hd + 1) * dh], axis=-1, keepdims=True)

    return _rowcall("attn_out_bwd", body, S, ROW_TILE, [dx, y, o], [gt, w_o],
                    [_sds((S, D), BF16), _sds((S, NO), BF16), _sds((S, n_heads), F32)],
                    [_sds((1, D), F32), _sds((1, D), F32)])


def _mla_attn_bwd(q, k, v, do, lse, delta, t, gblk):
    H, S, DQ = q.shape
    DV = V_DIM
    t = min(t, S)
    nb = S // t
    scale = QK_DIM ** -0.5
    c2 = scale * LOG2E

    def body(q_ref, k_ref, v_ref, do_ref, lse_ref, dl_ref, g_ref, dq_ref, dk_ref, dv_ref, recv_ref, dk_s, dv_s,
             send_sems, recv_sems, local_sem):
        hd, kj = pl.program_id(0), pl.program_id(1)

        def scatter():
            return _exchange_copies(lambda j: g_ref.at[j], recv_ref, send_sems, recv_sems, local_sem)

        @pl.when((hd == 0) & (kj == 0))
        def _():
            _start_exchange(scatter())

        @pl.when(kj == 0)
        def _():
            dq_ref[...] = jnp.zeros_like(dq_ref)

        dk_s[...] = jnp.zeros_like(dk_s)
        dv_s[...] = jnp.zeros_like(dv_s)

        def tile(i, diagonal):
            rows = pl.ds(pl.multiple_of(i * t, t), t)
            qb, dob, kb = q_ref[0, rows, :], do_ref[rows, :], k_ref[0]
            p = jnp.exp2(_dot_nt(kb, qb) * c2 - lse_ref[0, i])
            if diagonal:
                key = lax.broadcasted_iota(jnp.int32, (t, t), 0)
                qry = lax.broadcasted_iota(jnp.int32, (t, t), 1)
                p = jnp.where(key <= qry, p, 0.0)
            dv_s[...] += _dot(p.astype(BF16), dob)
            dp = _dot_nt(v_ref[0], dob)
            ds = (p * (dp - dl_ref[0, i])).astype(BF16)
            dk_s[...] += _dot(ds, qb)
            dq_ref[0, rows, :] += _dot_tn(ds, kb)

        def off_diagonal(i, carry):
            tile(i, False)
            return carry

        tile(kj, True)
        lax.fori_loop(kj + 1, nb, off_diagonal, 0)
        dk_ref[0] = (dk_s[...] * scale).astype(BF16)
        dv_ref[0] = dv_s[...].astype(BF16)

        @pl.when((hd == H - 1) & (kj == nb - 1))
        def _():
            _finish_exchange(scatter())

    rowspec = pl.BlockSpec((1, nb, 1, t), lambda h, j: (h, 0, 0, 0))
    return pl.pallas_call(
        body, name="mla_attn_bwd", grid=(H, nb),
        in_specs=[pl.BlockSpec((1, S, DQ), lambda h, j: (h, 0, 0)),
                  pl.BlockSpec((1, t, DQ), lambda h, j: (h, j, 0)),
                  pl.BlockSpec((1, t, DV), lambda h, j: (h, j, 0)),
                  pl.BlockSpec((S, DV), lambda h, j: (0, h)), rowspec, rowspec, ANY_SPEC],
        out_specs=[pl.BlockSpec((1, S, DQ), lambda h, j: (h, 0, 0)),
                   pl.BlockSpec((1, t, DQ), lambda h, j: (h, j, 0)),
                   pl.BlockSpec((1, t, DV), lambda h, j: (h, j, 0)), ANY_SPEC],
        out_shape=[_sds((H, S, DQ), F32), _sds((H, S, DQ), BF16), _sds((H, S, DV), BF16), _sds(gblk.shape, gblk.dtype)],
        scratch_shapes=[pltpu.VMEM((t, DQ), F32), pltpu.VMEM((t, DV), F32)] + COMM_SEMS,
        compiler_params=pltpu.CompilerParams(dimension_semantics=("arbitrary", "arbitrary"),
                                             vmem_limit_bytes=VMEM_LIMIT),
    )(q, k, v, do, lse, delta, gblk)


def _swa_attn_bwd(q, k, v, do, lse, delta, sinks):
    S, NQ = q.shape
    NK = k.shape[1]
    W, Dh, G = WINDOW, SWA_HEAD_DIM, SWA_GROUP
    nb = S // W

    def body(q_ref, kp_ref, kc_ref, vp_ref, vc_ref, do_ref, lse_ref, dl_ref, sink_ref,
             dq_ref, dk_ref, dv_ref, dsink_ref, dkc_s, dvc_s):
        n = pl.program_id(0)

        @pl.when(n == 0)
        def _():
            dkc_s[...] = jnp.zeros_like(dkc_s)
            dvc_s[...] = jnp.zeros_like(dvc_s)
            dsink_ref[...] = jnp.zeros_like(dsink_ref)

        @pl.when(n < nb)
        def _():
            distf, valid = _swa_geometry(n)
            kband = jnp.concatenate([kp_ref[...], kc_ref[...]], axis=0)
            vband = jnp.concatenate([vp_ref[...], vc_ref[...]], axis=0)
            kband_t = kband.astype(F32).T.astype(BF16)
            dq_t = []
            for kh in range(SWA_KV_HEADS):
                ck = slice(kh * Dh, (kh + 1) * Dh)
                kb, vb, kbt = kband[:, ck], vband[:, ck], kband_t[ck, :]
                dkb = jnp.zeros((2 * W, Dh), F32)
                dvb = jnp.zeros((2 * W, Dh), F32)
                for gi in range(G):
                    hq = kh * G + gi
                    cq = slice(hq * Dh, (hq + 1) * Dh)
                    qh, doh = q_ref[:, cq], do_ref[:, cq]
                    lse_h = lse_ref[hq:hq + 1, :]
                    dl_h = dl_ref[hq:hq + 1, :]
                    s = _dot_nt(kb, qh) * (Dh ** -0.5) - _alibi_slope(hq) * distf
                    p = jnp.where(valid, jnp.exp(s - lse_h), 0.0)
                    dvb = dvb + _dot(p.astype(BF16), doh)
                    dp = _dot_nt(vb, doh)
                    dsb = ((p * (dp - dl_h)) * (Dh ** -0.5)).astype(BF16)
                    dq_t.append(_dot(kbt, dsb))
                    dkb = dkb + _dot(dsb, qh)
                    psink = jnp.exp(sink_ref[:, hq:hq + 1] - lse_h)
                    dsink_ref[:, hq:hq + 1] += -jnp.sum(psink * dl_h, axis=1, keepdims=True)
                dk_ref[:, ck] = (dkc_s[:, ck] + dkb[:W]).astype(BF16)
                dv_ref[:, ck] = (dvc_s[:, ck] + dvb[:W]).astype(BF16)
                dkc_s[:, ck] = dkb[W:]
                dvc_s[:, ck] = dvb[W:]
            dq_ref[...] = jnp.concatenate(dq_t, axis=0).T.astype(BF16)

        @pl.when(n == nb)
        def _():
            dk_ref[...] = dkc_s[...].astype(BF16)
            dv_ref[...] = dvc_s[...].astype(BF16)

    kprev, kcur = _swa_band_specs(W, nb, NK)
    qspec = lambda cols: pl.BlockSpec((W, cols), lambda n: (jnp.minimum(n, nb - 1), 0))
    kvout = pl.BlockSpec((W, NK), lambda n: (jnp.maximum(n - 1, 0), 0))
    rowspec = pl.BlockSpec((SWA_HEADS, W), lambda n: (0, jnp.minimum(n, nb - 1)))
    return pl.pallas_call(
        body, name="swa_attn_bwd", grid=(nb + 1,),
        in_specs=[qspec(NQ), kprev, kcur, kprev, kcur, qspec(NQ), rowspec, rowspec,
                  pl.BlockSpec((1, SWA_HEADS), lambda n: (0, 0))],
        out_specs=[qspec(NQ), kvout, kvout, pl.BlockSpec((1, 128), lambda n: (0, 0))],
        out_shape=[_sds((S, NQ), BF16), _sds((S, NK), BF16), _sds((S, NK), BF16), _sds((1, 128), F32)],
        scratch_shapes=[pltpu.VMEM((W, NK), F32), pltpu.VMEM((W, NK), F32)],
        compiler_params=pltpu.CompilerParams(dimension_semantics=("arbitrary",), vmem_limit_bytes=VMEM_LIMIT),
    )(q, k, k, v, v, do, lse, delta, sinks)


def _swa_in_bwd(dq, dk, dv, x, dx, w_qkv, g, sc):
    S, D = x.shape
    N = w_qkv.shape[1]

    def body(dq_ref, dk_ref, dv_ref, x_ref, dx_ref, w_ref, g_ref, sc_ref, dqkv_ref, dxo_ref, db_ref, dsh_ref, da_ref):
        i = pl.program_id(0)
        dqkv = jnp.concatenate([dq_ref[...], dk_ref[...], dv_ref[...]], axis=1)
        dqkv_ref[...] = dqkv
        _acc(db_ref, jnp.sum(dqkv.astype(F32), axis=0, keepdims=True), i)
        dh = _dot_nt(dqkv, w_ref[...])
        dxn, dsh, da = _modulate_bwd(dh, x_ref[...], g_ref[...], sc_ref[...])
        dxo_ref[...] = dx_ref[...] + dxn
        _acc(dsh_ref, dsh, i)
        _acc(da_ref, da, i)

    return _rowcall("swa_in_bwd", body, S, ROW_TILE, [dq, dk, dv, x, dx], [w_qkv, g, sc],
                    [_sds((S, N), BF16), _sds((S, D), F32)],
                    [_sds((1, N), F32), _sds((1, D), F32), _sds((1, D), F32)])


def _mla_in_bwd(dq, dk, dv, cos, sin, cqp, ckvp, x, dx, w_uqx, g_q, w_ukv, g_kv, w_cat, g, sc):
    S, D = x.shape
    H = MLA_HEADS
    QL = g_q.shape[1]
    NX = w_uqx.shape[1]
    NC = w_cat.shape[1]

    def body(dq_ref, dk_ref, dv_ref, cos_ref, sin_ref, cqp_ref, ckvp_ref, x_ref, dx_ref,
             wuqx_ref, gq_ref, wukv_ref, gkv_ref, wcat_ref, g_ref, sc_ref,
             dqx_ref, dkv_ref, dcat_ref, dxo_ref, dgq_ref, dgkv_ref, dsh_ref, da_ref):
        i = pl.program_id(0)
        cs, sn = cos_ref[...], sin_ref[...]
        dkr = jnp.zeros(cs.shape, F32)
        for hd in range(H):
            b = hd * 256
            dqh = dq_ref[hd] * (QK_DIM ** -0.5)
            dqx_ref[:, b:b + QK_NOPE] = dqh[:, :QK_NOPE].astype(BF16)
            dqx_ref[:, b + 128:b + 192] = (dqh[:, QK_NOPE:] * cs).astype(BF16)
            dqx_ref[:, b + 192:b + 256] = (dqh[:, QK_NOPE:] * sn).astype(BF16)
            dkh = dk_ref[hd]
            dkv_ref[:, b:b + QK_NOPE] = dkh[:, :QK_NOPE]
            dkv_ref[:, b + 128:b + 256] = dv_ref[hd]
            dkr = dkr + dkh[:, QK_NOPE:].astype(F32)
        dcq = _dot_nt(dqx_ref[...], wuqx_ref[...])
        cqp = cqp_ref[...]
        rq = _rstd(cqp)
        nq = cqp * rq
        _acc(dgq_ref, jnp.sum(dcq * nq, axis=0, keepdims=True), i)
        dcqp = _rms_bwd(dcq * gq_ref[...], nq, rq)
        dckv = _dot_nt(dkv_ref[...], wukv_ref[...])
        ckvp = ckvp_ref[...]
        rk = _rstd(ckvp)
        nk = ckvp * rk
        _acc(dgkv_ref, jnp.sum(dckv * nk, axis=0, keepdims=True), i)
        dckvp = _rms_bwd(dckv * gkv_ref[...], nk, rk)
        dcat_ref[:, :QL] = dcqp.astype(BF16)
        dcat_ref[:, QL:QL + KV_LORA] = dckvp.astype(BF16)
        o = QL + KV_LORA
        dcat_ref[:, o:o + QK_ROPE] = (dkr * cs).astype(BF16)
        dcat_ref[:, o + QK_ROPE:o + 2 * QK_ROPE] = (dkr * sn).astype(BF16)
        dh = _dot_nt(dcat_ref[...], wcat_ref[...])
        dxn, dsh, da = _modulate_bwd(dh, x_ref[...], g_ref[...], sc_ref[...])
        dxo_ref[...] = dx_ref[...] + dxn
        _acc(dsh_ref, dsh, i)
        _acc(da_ref, da, i)

    return _rowcall("mla_in_bwd", body, S, ROW_TILE_WIDE, [dq, dk, dv, cos, sin, cqp, ckvp, x, dx],
                    [w_uqx, g_q, w_ukv, g_kv, w_cat, g, sc],
                    [_sds((S, NX), BF16), _sds((S, NX), BF16), _sds((S, NC), BF16), _sds((S, D), F32)],
                    [_sds((1, QL), F32), _sds((1, KV_LORA), F32), _sds((1, D), F32), _sds((1, D), F32)])


def _matmul_tn(name, a, b):
    S, K = a.shape
    N = b.shape[1]
    tk, tn, ts = min(K, 1024), min(N, 1024), min(S, ROW_TILE)
    if N % tn:
        tn = 512 if N % 512 == 0 else (384 if N % 384 == 0 else 128)
    if K % tk:
        tk = 512 if K % 512 == 0 else (384 if K % 384 == 0 else 128)
    ns = S // ts

    def body(a_ref, b_ref, o_ref):
        _acc(o_ref, _dot_tn(a_ref[...], b_ref[...]), pl.program_id(2))

    return pl.pallas_call(
        body, name=name, grid=(K // tk, N // tn, ns),
        in_specs=[pl.BlockSpec((ts, tk), lambda i, j, s: (s, i)), pl.BlockSpec((ts, tn), lambda i, j, s: (s, j))],
        out_specs=pl.BlockSpec((tk, tn), lambda i, j, s: (i, j)),
        out_shape=_sds((K, N), F32),
        compiler_params=pltpu.CompilerParams(dimension_semantics=("parallel", "parallel", "arbitrary"),
                                             vmem_limit_bytes=VMEM_LIMIT),
    )(a, b)


def _silu(c):
    return c * jax.nn.sigmoid(c)


def _ada_fwd(c_all, w_ada):
    L, D, NC = w_ada.shape

    def body(c_ref, w_ref, o_ref):
        cond = _silu(c_ref[...]).astype(BF16)
        o_ref[0] = _dot(cond, w_ref[0].astype(BF16))

    return pl.pallas_call(
        body, name="ada_fwd", grid=(L,),
        in_specs=[pl.BlockSpec(c_all.shape, lambda l: (0, 0)), pl.BlockSpec((1, D, NC), lambda l: (l, 0, 0))],
        out_specs=pl.BlockSpec((1, N_DEV, NC), lambda l: (l, 0, 0)),
        out_shape=_sds((L, N_DEV, NC), F32),
        compiler_params=pltpu.CompilerParams(dimension_semantics=("arbitrary",), vmem_limit_bytes=VMEM_LIMIT),
    )(c_all, w_ada)


def _adamw(w, g, m, v):
    m = ADAM_B1 * m + (1.0 - ADAM_B1) * g
    v = ADAM_B2 * v + (1.0 - ADAM_B2) * (g * g)
    m_hat = m / (1.0 - ADAM_B1 ** ADAM_STEP)
    v_hat = v / (1.0 - ADAM_B2 ** ADAM_STEP)
    delta = -ADAM_LR * (m_hat / (jnp.sqrt(v_hat) + ADAM_EPS) + ADAM_WD * w)
    return delta, m, v


def _ada_bwd_adamw(c_all_t, dmod_cols, w, m, v):
    L, D, NC = w.shape
    tr = min(D, 256)

    def body(ct_ref, dm_ref, w_ref, m_ref, v_ref, g_ref, d_ref, mo_ref, vo_ref):
        cond_t = _silu(ct_ref[...])
        dm = dm_ref[0]
        g = cond_t[:, 0:1] * dm[0:1, :]
        for b in range(1, N_DEV):
            g = g + cond_t[:, b:b + 1] * dm[b:b + 1, :]
        g_ref[0] = g
        d_ref[0], mo_ref[0], vo_ref[0] = _adamw(w_ref[0], g, m_ref[0], v_ref[0])

    wspec = pl.BlockSpec((1, tr, NC), lambda l, r: (l, r, 0))
    return pl.pallas_call(
        body, name="ada_bwd_adamw", grid=(L, D // tr),
        in_specs=[pl.BlockSpec((tr, N_DEV), lambda l, r: (r, 0)),
                  pl.BlockSpec((1, N_DEV, NC), lambda l, r: (l, 0, 0)), wspec, wspec, wspec],
        out_specs=[wspec] * 4, out_shape=[_sds(w.shape, F32)] * 4,
        compiler_params=pltpu.CompilerParams(dimension_semantics=("parallel", "parallel"), vmem_limit_bytes=VMEM_LIMIT),
    )(c_all_t, dmod_cols, w, m, v)


def _sum_devices(x):
    def body(x_ref, o_ref):
        s = x_ref[0]
        for j in range(1, N_DEV):
            s = s + x_ref[j]
        o_ref[...] = s

    return pl.pallas_call(body, name="sum_devices", out_shape=_sds(x.shape[1:], F32))(x)


def _adamw_small(w, g, m, v):
    def body(w_ref, g_ref, m_ref, v_ref, d_ref, mo_ref, vo_ref):
        d_ref[...], mo_ref[...], vo_ref[...] = _adamw(w_ref[...], g_ref[...], m_ref[...], v_ref[...])

    return pl.pallas_call(body, name="adamw_small", out_shape=[_sds(w.shape, F32)] * 3)(w, g, m, v)


def _me():
    return lax.axis_index("x") * 4 + lax.axis_index("y") * 2 + lax.axis_index("c")


def _peer(k):
    x, y, c = lax.axis_index("x"), lax.axis_index("y"), lax.axis_index("c")
    px = 1 - x if k & 4 else x
    py = 1 - y if k & 2 else y
    pc = 1 - c if k & 1 else c
    return (px, py, pc), px * 4 + py * 2 + pc


VMEM_SPEC = pl.BlockSpec(memory_space=pltpu.VMEM)
ANY_SPEC = pl.BlockSpec(memory_space=pl.ANY)
COMM_SEMS = [pltpu.SemaphoreType.DMA((N_DEV - 1,)), pltpu.SemaphoreType.DMA((N_DEV - 1,)), pltpu.SemaphoreType.DMA(())]


def _exchange_copies(src_of, dst_ref, send_sems, recv_sems, local_sem):
    me = _me()
    local = pltpu.make_async_copy(src_of(me), dst_ref.at[me], local_sem)
    sends, recvs = [], []
    for k in range(1, N_DEV):
        dev, pj = _peer(k)
        sems = dict(send_sem=send_sems.at[k - 1], recv_sem=recv_sems.at[k - 1], device_id=dev, device_id_type=MESH_IDS)
        sends.append(pltpu.make_async_remote_copy(src_ref=src_of(pj), dst_ref=dst_ref.at[me], **sems))
        recvs.append(pltpu.make_async_remote_copy(src_ref=src_of(pj), dst_ref=dst_ref.at[pj], **sems))
    return local, sends, recvs


def _start_exchange(copies):
    local, sends, _ = copies
    local.start()
    for cp in sends:
        cp.start()


def _finish_exchange(copies):
    local, sends, recvs = copies
    for cp in recvs:
        cp.wait_recv()
    for cp in sends:
        cp.wait_send()
    local.wait()


def _sum_adamw(recv, w, m, v):
    _, R, C = recv.shape
    rows = max(d for d in range(16, min(R, 256) + 1, 16) if R % d == 0)

    def body(r_ref, w_ref, m_ref, v_ref, go_ref, d_ref, mo_ref, vo_ref):
        g = r_ref[0].astype(F32)
        for j in range(1, N_DEV):
            g = g + r_ref[j].astype(F32)
        go_ref[...] = g
        d_ref[...], mo_ref[...], vo_ref[...] = _adamw(w_ref[...], g, m_ref[...], v_ref[...])

    spec = pl.BlockSpec((rows, C), lambda i: (i, 0))
    return pl.pallas_call(
        body, name="sum_adamw", grid=(R // rows,),
        in_specs=[pl.BlockSpec((N_DEV, rows, C), lambda i: (0, i, 0)), spec, spec, spec],
        out_specs=[spec] * 4, out_shape=[_sds((R, C), F32)] * 4,
        compiler_params=pltpu.CompilerParams(dimension_semantics=("parallel",), vmem_limit_bytes=VMEM_LIMIT),
    )(recv, w, m, v)


def _all_gather(name, x, out_dtype):
    R, C = x.shape
    cast = out_dtype != x.dtype

    def body(x_ref, out_ref, buf, send_sems, recv_sems, local_sem):
        me = _me()
        if cast:
            buf[...] = x_ref[...].astype(out_dtype)
            src = buf
        else:
            src = x_ref
        local = pltpu.make_async_copy(src, out_ref.at[me], local_sem)
        local.start()
        sends = []
        for k in range(1, N_DEV):
            dev, _ = _peer(k)
            cp = pltpu.make_async_remote_copy(src_ref=src, dst_ref=out_ref.at[me], send_sem=send_sems.at[k - 1],
                                              recv_sem=recv_sems.at[k - 1], device_id=dev, device_id_type=MESH_IDS)
            cp.start()
            sends.append(cp)
        for k in range(1, N_DEV):
            dev, pj = _peer(k)
            pltpu.make_async_remote_copy(src_ref=src, dst_ref=out_ref.at[pj], send_sem=send_sems.at[k - 1],
                                         recv_sem=recv_sems.at[k - 1], device_id=dev, device_id_type=MESH_IDS).wait_recv()
        for cp in sends:
            cp.wait_send()
        local.wait()

    return pl.pallas_call(
        body, name=name, in_specs=[VMEM_SPEC], out_specs=ANY_SPEC, out_shape=_sds((N_DEV, R, C), out_dtype),
        scratch_shapes=[pltpu.VMEM((R, C) if cast else (8, 128), out_dtype),
                        pltpu.SemaphoreType.DMA((N_DEV - 1,)), pltpu.SemaphoreType.DMA((N_DEV - 1,)),
                        pltpu.SemaphoreType.DMA(())],
        compiler_params=pltpu.CompilerParams(vmem_limit_bytes=VMEM_LIMIT),
    )(x)


def _all_to_all(name, x):
    _, R, C = x.shape

    def body(x_ref, out_ref, send_sems, recv_sems, local_sem):
        me = _me()
        local = pltpu.make_async_copy(x_ref.at[me], out_ref.at[me], local_sem)
        local.start()
        sends = []
        for k in range(1, N_DEV):
            dev, pj = _peer(k)
            cp = pltpu.make_async_remote_copy(src_ref=x_ref.at[pj], dst_ref=out_ref.at[me], send_sem=send_sems.at[k - 1],
                                              recv_sem=recv_sems.at[k - 1], device_id=dev, device_id_type=MESH_IDS)
            cp.start()
            sends.append(cp)
        for k in range(1, N_DEV):
            dev, pj = _peer(k)
            pltpu.make_async_remote_copy(src_ref=x_ref.at[pj], dst_ref=out_ref.at[pj], send_sem=send_sems.at[k - 1],
                                         recv_sem=recv_sems.at[k - 1], device_id=dev, device_id_type=MESH_IDS).wait_recv()
        for cp in sends:
            cp.wait_send()
        local.wait()

    return pl.pallas_call(
        body, name=name, in_specs=[VMEM_SPEC], out_specs=VMEM_SPEC, out_shape=_sds(x.shape, x.dtype),
        scratch_shapes=[pltpu.SemaphoreType.DMA((N_DEV - 1,)), pltpu.SemaphoreType.DMA((N_DEV - 1,)),
                        pltpu.SemaphoreType.DMA(())],
    )(x)


def _reduce_scatter_adamw(name, gblk, w, m, v):
    _, R, C = gblk.shape
    rows = 8
    for cand in (136, 128, 80, 64, 40, 32, 16, 8):
        if R % cand == 0:
            rows = cand
            break

    def body(g_ref, w_ref, m_ref, v_ref, go_ref, d_ref, mo_ref, vo_ref, recv, send_sems, recv_sems, local_sem):
        me = _me()
        local = pltpu.make_async_copy(g_ref.at[me], recv.at[me], local_sem)
        local.start()
        sends = []
        for k in range(1, N_DEV):
            dev, pj = _peer(k)
            cp = pltpu.make_async_remote_copy(src_ref=g_ref.at[pj], dst_ref=recv.at[me], send_sem=send_sems.at[k - 1],
                                              recv_sem=recv_sems.at[k - 1], device_id=dev, device_id_type=MESH_IDS)
            cp.start()
            sends.append(cp)
        for k in range(1, N_DEV):
            dev, pj = _peer(k)
            pltpu.make_async_remote_copy(src_ref=g_ref.at[pj], dst_ref=recv.at[pj], send_sem=send_sems.at[k - 1],
                                         recv_sem=recv_sems.at[k - 1], device_id=dev, device_id_type=MESH_IDS).wait_recv()
        local.wait()

        def chunk(i, carry):
            r = pl.ds(pl.multiple_of(i * rows, rows), rows)
            g = recv[0, r, :].astype(F32)
            for j in range(1, N_DEV):
                g = g + recv[j, r, :].astype(F32)
            go_ref[r, :] = g
            d_ref[r, :], mo_ref[r, :], vo_ref[r, :] = _adamw(w_ref[r, :], g, m_ref[r, :], v_ref[r, :])
            return carry

        lax.fori_loop(0, R // rows, chunk, 0)
        for cp in sends:
            cp.wait_send()

    return pl.pallas_call(
        body, name=name, in_specs=[ANY_SPEC, VMEM_SPEC, VMEM_SPEC, VMEM_SPEC], out_specs=[VMEM_SPEC] * 4,
        out_shape=[_sds((R, C), F32)] * 4,
        scratch_shapes=[pltpu.VMEM((N_DEV, R, C), BF16), pltpu.SemaphoreType.DMA((N_DEV - 1,)),
                        pltpu.SemaphoreType.DMA((N_DEV - 1,)), pltpu.SemaphoreType.DMA(())],
        compiler_params=pltpu.CompilerParams(vmem_limit_bytes=VMEM_LIMIT),
    )(gblk, w, m, v)


FIRST_WEIGHTS = ["mla_w_dq", "mla_w_uq", "mla_w_dkv", "mla_w_ukv", "mla_w_o"]
LATE_WEIGHTS = ["swa_w_qkv", "swa_w_o", "w_ff1", "w_ff2"]
ROW_SHARDED = {"mla_w_dq", "mla_w_dkv", "mla_w_o", "swa_w_o", "w_ff2"}


def _unblock(name, blocks):
    sh = blocks.shape[1:]
    if name in ROW_SHARDED:
        return jnp.moveaxis(blocks, 0, 1).reshape(sh[0], N_DEV * sh[1], sh[2])
    return jnp.moveaxis(blocks, 0, 2).reshape(sh[0], sh[1], N_DEV * sh[2])


def _block(name, full):
    L, K, N = full.shape
    if name in ROW_SHARDED:
        return jnp.moveaxis(full.reshape(L, N_DEV, K // N_DEV, N), 1, 0)
    return jnp.moveaxis(full.reshape(L, K, N_DEV, N // N_DEV), 2, 0)


def _rot_cols(w):
    half = QK_ROPE // 2
    return jnp.concatenate([-w[..., half:], w[..., :half]], axis=-1)


def _unrot_cols(gw):
    half = QK_ROPE // 2
    return jnp.concatenate([gw[..., half:], -gw[..., :half]], axis=-1)


def _row(v):
    return v.reshape(1, -1)


def _mlp_block_bwd(dx, sv, w1, w2, g, sc, gt):
    dy, du, dgt = _mlp_bwd_a(dx, sv["y2"], sv["rl"], gt, w2)
    dw2 = _matmul_tn("dw_ff2", sv["act"], dy)
    dw1 = _matmul_tn("dw_ff1", sv["h2"], du)
    dxo, dsh, da = _mlp_bwd_b(du, sv["x1"], dx, w1, g, sc)
    return dxo, dw1, dw2, dsh, da, dgt


def kernel(x, c, positions, w_ada, b_ada, g_mix, g_mlp, mla_w_dq, mla_g_q, mla_w_uq, mla_w_dkv, mla_g_kv, mla_w_ukv, mla_w_o, swa_w_qkv, swa_b_qkv, swa_sinks, swa_w_o, swa_b_o, w_ff1, w_ff2, g_final, loss_target, m_w_ada, m_b_ada, m_g_mix, m_g_mlp, m_mla_w_dq, m_mla_g_q, m_mla_w_uq, m_mla_w_dkv, m_mla_g_kv, m_mla_w_ukv, m_mla_w_o, m_swa_w_qkv, m_swa_b_qkv, m_swa_sinks, m_swa_w_o, m_swa_b_o, m_w_ff1, m_w_ff2, m_g_final, v_w_ada, v_b_ada, v_g_mix, v_g_mlp, v_mla_w_dq, v_mla_g_q, v_mla_w_uq, v_mla_w_dkv, v_mla_g_kv, v_mla_w_ukv, v_mla_w_o, v_swa_w_qkv, v_swa_b_qkv, v_swa_sinks, v_swa_w_o, v_swa_b_o, v_w_ff1, v_w_ff2, v_g_final):
    S, D = x.shape[1], x.shape[2]
    me = _me()
    x0 = x[0]
    target = loss_target[0]
    big_w = dict(mla_w_dq=mla_w_dq, mla_w_uq=mla_w_uq, mla_w_dkv=mla_w_dkv, mla_w_ukv=mla_w_ukv, mla_w_o=mla_w_o,
                 swa_w_qkv=swa_w_qkv, swa_w_o=swa_w_o, w_ff1=w_ff1, w_ff2=w_ff2)
    big_m = dict(mla_w_dq=m_mla_w_dq, mla_w_uq=m_mla_w_uq, mla_w_dkv=m_mla_w_dkv, mla_w_ukv=m_mla_w_ukv,
                 mla_w_o=m_mla_w_o, swa_w_qkv=m_swa_w_qkv, swa_w_o=m_swa_w_o, w_ff1=m_w_ff1, w_ff2=m_w_ff2)
    big_v = dict(mla_w_dq=v_mla_w_dq, mla_w_uq=v_mla_w_uq, mla_w_dkv=v_mla_w_dkv, mla_w_ukv=v_mla_w_ukv,
                 mla_w_o=v_mla_w_o, swa_w_qkv=v_swa_w_qkv, swa_w_o=v_swa_w_o, w_ff1=v_w_ff1, w_ff2=v_w_ff2)
    groups = {"first": FIRST_WEIGHTS, "late": LATE_WEIGHTS}
    wrows = {n: -(-big_w[n].size // (PACK_COLS * 16)) * 16 for n in FIRST_WEIGHTS + LATE_WEIGHTS}
    offs = {g: np.concatenate([[0], np.cumsum([wrows[n] for n in names])]).astype(int) for g, names in groups.items()}

    def as_rows(n, a, lead=()):
        flat = a.reshape(lead + (-1,))
        pad = wrows[n] * PACK_COLS - flat.shape[-1]
        if pad:
            flat = jnp.pad(flat, ((0, 0),) * len(lead) + ((0, pad),))
        return flat.reshape(lead + (wrows[n], PACK_COLS))

    def pack(g, d):
        return jnp.concatenate([as_rows(n, d[n]) for n in groups[g]], axis=0)

    def pack_blocks(g, gfull):
        return jnp.concatenate([as_rows(n, _block(n, gfull[n]).astype(BF16), (N_DEV,)) for n in groups[g]], axis=1)

    def unpack(g, packed, lead=()):
        out = {}
        for i, n in enumerate(groups[g]):
            part = packed[..., int(offs[g][i]):int(offs[g][i + 1]), :].reshape(lead + (-1,))
            out[n] = part[..., :big_w[n].size].reshape(lead + big_w[n].shape)
        return out

    gathered = _all_gather("gather_weights", pack("first", big_w), BF16)
    wfull = {n: _unblock(n, b) for n, b in unpack("first", gathered, (N_DEV,)).items()}
    w_dq, w_dkv = wfull["mla_w_dq"][0], wfull["mla_w_dkv"][0]
    w_cat = jnp.concatenate([w_dq, w_dkv, _rot_cols(w_dkv[:, KV_LORA:])], axis=1)
    QL = w_dq.shape[1]
    w_uq = wfull["mla_w_uq"][0].reshape(QL, MLA_HEADS, QK_DIM)
    w_uqx = jnp.concatenate([w_uq, _rot_cols(w_uq[..., QK_NOPE:])], axis=-1).reshape(QL, MLA_HEADS * 256)
    w_ukv = wfull["mla_w_ukv"][0]
    w_o_mla = wfull["mla_w_o"][0]

    L = w_ada.shape[0]
    NC = w_ada.shape[2]
    nbq, nbo = swa_b_qkv.shape[1], swa_b_o.shape[1]
    cpad = -(-(D + nbq + nbo) // 1024) * 1024
    cpack = jnp.pad(jnp.concatenate([c[0], swa_b_qkv[0], swa_b_o[0]]), (0, cpad - (D + nbq + nbo))).reshape(8, cpad // 8)
    call = _all_gather("gather_c", cpack, F32).reshape(N_DEV, cpad)
    c_all = call[:, :D]
    b_qkv_full = call[:, D:D + nbq].reshape(1, N_DEV * nbq)
    b_o_full = call[:, D + nbq:D + nbq + nbo].reshape(1, N_DEV * nbo)
    mod_cols = _ada_fwd(c_all, w_ada)
    mpad = -(-(L * NC) // 1024) * 1024
    mod_send = jnp.pad(jnp.moveaxis(mod_cols, 1, 0).reshape(N_DEV, L * NC), ((0, 0), (0, mpad - L * NC)))
    mod_mine = _all_to_all("exchange_mod", mod_send.reshape(N_DEV, 8, mpad // 8)).reshape(N_DEV, mpad)[:, :L * NC]
    mod = jnp.moveaxis(mod_mine.reshape(N_DEV, L, NC), 0, 1).reshape(L, N_DEV * NC) + b_ada
    mods = mod.reshape(L, 6, 1, D)

    half = QK_ROPE // 2
    inv_freq = ROPE_THETA ** (-jnp.arange(half, dtype=F32) / half)
    ang = positions[0].astype(F32)[:, None] * inv_freq
    cos = jnp.concatenate([jnp.cos(ang), jnp.cos(ang)], axis=-1)
    sin = jnp.concatenate([jnp.sin(ang), jnp.sin(ang)], axis=-1)

    T_ATT = ATT_TILE
    zero_bias = jnp.zeros((1, D), F32)

    sh1, sc1, gt1, sh2, sc2, gt2 = [mods[0, i] for i in range(6)]
    gm0, gp0 = _row(g_mix[0]), _row(g_mlp[0])
    h1, cqp, cq, ckvp, ckv, q, k, v, vt = _mla_in_fwd(x0, cos, sin, gm0, sc1, sh1, w_cat, mla_g_q, w_uqx, mla_g_kv,
                                                      w_ukv, T_ATT)
    o0, lse0, gathered = _mla_attn_fwd(q, k, vt, T_ATT, pack("late", big_w).astype(BF16))
    wfull = {n: _unblock(n, b) for n, b in unpack("late", gathered, (N_DEV,)).items()}
    w_qkv, w_o_swa = wfull["swa_w_qkv"][0], wfull["swa_w_o"][0]
    ff1, ff2 = wfull["w_ff1"], wfull["w_ff2"]
    y1, x1, h2 = _attn_out_fwd(o0, x0, w_o_mla, zero_bias, gt1, gp0, sc2, sh2)
    rl0, act0, y2, x2 = _mlp_fwd(h2, x1, ff1[0], ff2[0], gt2)
    sv0 = dict(y2=y2, rl=rl0, act=act0, h2=h2, x1=x1)

    th1, tc1, tg1, th2, tc2, tg2 = [mods[1, i] for i in range(6)]
    gm1, gp1 = _row(g_mix[1]), _row(g_mlp[1])
    h3, sq, sk, svv = _swa_in_fwd(x2, gm1, tc1, th1, w_qkv, b_qkv_full)
    o1, lse1 = _swa_attn_fwd(sq, sk, svv, swa_sinks)
    y3, x3, h4 = _attn_out_fwd(o1, x2, w_o_swa, b_o_full, tg1, gp1, tc2, th2)
    rl1, act1, y4, x4 = _mlp_fwd(h4, x3, ff1[1], ff2[1], tg2)
    sv1 = dict(y2=y4, rl=rl1, act=act1, h2=h4, x1=x3)
    dx4, loss_part, dg_final = _final_loss(x4, target, _row(g_final))

    dx3, dw1_1, dw2_1, dsh2_1, da2_1, dgt2_1 = _mlp_block_bwd(dx4, sv1, ff1[1], ff2[1], gp1, tc2, tg2)
    dy, do, dl, dgt1_1, db_o = _attn_out_bwd(dx3, y3, o1, tg1, w_o_swa, SWA_HEADS)
    dw_o_swa = _matmul_tn("dw_o", o1, dy)
    dsq, dsk, dsv, dsink = _swa_attn_bwd(sq, sk, svv, do, lse1, dl.T, swa_sinks)
    dqkv, dx2, db_qkv, dsh1_1, da1_1 = _swa_in_bwd(dsq, dsk, dsv, x2, dx3, w_qkv, gm1, tc1)
    dw_qkv = _matmul_tn("dw_qkv", h3, dqkv)

    dx1, dw1_0, dw2_0, dsh2_0, da2_0, dgt2_0 = _mlp_block_bwd(dx2, sv0, ff1[0], ff2[0], gp0, sc2, gt2)
    dy, do, dl, dgt1_0, _ = _attn_out_bwd(dx1, y1, o0, gt1, w_o_mla, MLA_HEADS)
    dw_o_mla = _matmul_tn("dw_o", o0, dy)
    tb = min(T_ATT, S)
    delta = dl.T.reshape(MLA_HEADS, S // tb, 1, tb)
    glate = dict(swa_w_qkv=dw_qkv[None], swa_w_o=dw_o_swa[None], w_ff1=jnp.stack([dw1_0, dw1_1]),
                 w_ff2=jnp.stack([dw2_0, dw2_1]))
    dq, dk, dv, recv = _mla_attn_bwd(q, k, v, do, lse0 * LOG2E, delta, T_ATT, pack_blocks("late", glate))
    late = _sum_adamw(recv, pack("late", big_w), pack("late", big_m), pack("late", big_v))
    dqx, dkv, dcat, dx0, dg_q, dg_kv, dsh1_0, da1_0 = _mla_in_bwd(
        dq, dk, dv, cos, sin, cqp, ckvp, x0, dx1, w_uqx, mla_g_q, w_ukv, mla_g_kv, w_cat, gm0, sc1)
    dw_uqx = _matmul_tn("dw_uq", cq, dqx).reshape(QL, MLA_HEADS, 256)
    dw_ukv = _matmul_tn("dw_ukv", ckv, dkv)
    dw_cat = _matmul_tn("dw_down", h1, dcat)
    dw_uq = jnp.concatenate([dw_uqx[..., :QK_NOPE], dw_uqx[..., 128:192] + _unrot_cols(dw_uqx[..., 192:256])],
                            axis=-1).reshape(QL, MLA_HEADS * QK_DIM)
    o_kr = QL + KV_LORA
    dw_dkv = jnp.concatenate([dw_cat[:, QL:o_kr],
                              dw_cat[:, o_kr:o_kr + QK_ROPE] + _unrot_cols(dw_cat[:, o_kr + QK_ROPE:])], axis=1)

    gfirst = dict(mla_w_dq=dw_cat[None, :, :QL], mla_w_uq=dw_uq[None], mla_w_dkv=dw_dkv[None], mla_w_ukv=dw_ukv[None],
                  mla_w_o=dw_o_mla[None])
    first = _reduce_scatter_adamw("grad_exchange_adamw", pack_blocks("first", gfirst), pack("first", big_w),
                                  pack("first", big_m), pack("first", big_v))
    big_g, big_d, big_nm, big_nv = ({**unpack("first", first[j]), **unpack("late", late[j])} for j in range(4))

    dmod = jnp.stack([
        jnp.concatenate([dsh1_0, gm0 * da1_0, dgt1_0, dsh2_0, gp0 * da2_0, dgt2_0], axis=1),
        jnp.concatenate([dsh1_1, gm1 * da1_1, dgt1_1, dsh2_1, gp1 * da2_1, dgt2_1], axis=1)]).reshape(-1)
    dg_mix = jnp.concatenate([(1.0 + sc1) * da1_0, (1.0 + tc1) * da1_1], axis=1).reshape(-1)
    dg_mlp = jnp.concatenate([(1.0 + sc2) * da2_0, (1.0 + tc2) * da2_1], axis=1).reshape(-1)
    parts = [loss_part.reshape(-1), dmod, dg_mix, dg_mlp, dg_q.reshape(-1), dg_kv.reshape(-1), dsink.reshape(-1),
             dg_final.reshape(-1), db_qkv.reshape(-1), db_o.reshape(-1)]
    soffs = np.concatenate([[0], np.cumsum([p.size for p in parts])])
    spad = -(-int(soffs[-1]) // 1024) * 1024
    spack = jnp.pad(jnp.concatenate(parts), (0, spad - int(soffs[-1]))).reshape(8, spad // 8)
    sall = _all_gather("gather_small_grads", spack, F32)
    ssum = _sum_devices(sall).reshape(-1)
    tot = [ssum[int(soffs[i]):int(soffs[i + 1])] for i in range(len(parts))]
    loss = tot[0][0]
    nsink = swa_sinks.shape[1]
    small_g = dict(b_ada=tot[1].reshape(b_ada.shape), g_mix=tot[2].reshape(g_mix.shape), g_mlp=tot[3].reshape(g_mlp.shape),
                   mla_g_q=tot[4].reshape(mla_g_q.shape), mla_g_kv=tot[5].reshape(mla_g_kv.shape),
                   swa_sinks=tot[6][:nsink].reshape(swa_sinks.shape), g_final=tot[7].reshape(g_final.shape),
                   swa_b_qkv=lax.dynamic_slice(tot[8], (me * nbq,), (nbq,)).reshape(swa_b_qkv.shape),
                   swa_b_o=lax.dynamic_slice(tot[9], (me * nbo,), (nbo,)).reshape(swa_b_o.shape))
    small_w = dict(b_ada=b_ada, g_mix=g_mix, g_mlp=g_mlp, mla_g_q=mla_g_q, mla_g_kv=mla_g_kv, swa_sinks=swa_sinks,
                   g_final=g_final, swa_b_qkv=swa_b_qkv, swa_b_o=swa_b_o)
    small_m = dict(b_ada=m_b_ada, g_mix=m_g_mix, g_mlp=m_g_mlp, mla_g_q=m_mla_g_q, mla_g_kv=m_mla_g_kv,
                   swa_sinks=m_swa_sinks, g_final=m_g_final, swa_b_qkv=m_swa_b_qkv, swa_b_o=m_swa_b_o)
    small_v = dict(b_ada=v_b_ada, g_mix=v_g_mix, g_mlp=v_g_mlp, mla_g_q=v_mla_g_q, mla_g_kv=v_mla_g_kv,
                   swa_sinks=v_swa_sinks, g_final=v_g_final, swa_b_qkv=v_swa_b_qkv, swa_b_o=v_swa_b_o)
    SMALL = list(small_w)
    woffs = np.concatenate([[0], np.cumsum([small_w[n].size for n in SMALL])])
    wpad = -(-int(woffs[-1]) // 1024) * 1024

    def spack_of(d):
        flat = jnp.concatenate([d[n].reshape(-1) for n in SMALL])
        return jnp.pad(flat, (0, wpad - int(woffs[-1]))).reshape(8, wpad // 8)

    sm = _adamw_small(spack_of(small_w), spack_of(small_g), spack_of(small_m), spack_of(small_v))
    small_d, small_nm, small_nv = (
        {n: a.reshape(-1)[int(woffs[i]):int(woffs[i + 1])].reshape(small_w[n].shape) for i, n in enumerate(SMALL)}
        for a in sm)

    b_off = int(soffs[1])
    dmod_all = sall.reshape(N_DEV, -1)[:, b_off:b_off + L * N_DEV * NC].reshape(N_DEV, L, N_DEV * NC)
    dmod_cols = jnp.moveaxis(lax.dynamic_slice_in_dim(dmod_all, me * NC, NC, axis=2), 0, 1)
    ada_g, ada_d, ada_nm, ada_nv = _ada_bwd_adamw(c_all.T, dmod_cols, w_ada, m_w_ada, v_w_ada)

    order = ["w_ada", "b_ada", "g_mix", "g_mlp", "mla_w_dq", "mla_g_q", "mla_w_uq", "mla_w_dkv", "mla_g_kv",
             "mla_w_ukv", "mla_w_o", "swa_w_qkv", "swa_b_qkv", "swa_sinks", "swa_w_o", "swa_b_o", "w_ff1", "w_ff2", "g_final"]

    def collect(ada, big, small):
        return [ada if n == "w_ada" else (big[n] if n in big else small[n]) for n in order]

    return (loss, dx0.reshape(x.shape), *collect(ada_g, big_g, small_g), *collect(ada_d, big_d, small_d),
            *collect(ada_nm, big_nm, small_nm), *collect(ada_nv, big_nv, small_nv))
```

```python
import functools

import jax
import jax.numpy as jnp
import numpy as np
from jax import lax
from jax.experimental import pallas as pl
from jax.experimental.pallas import tpu as pltpu

F32 = jnp.float32
BF16 = jnp.bfloat16
MESH_IDS = pl.DeviceIdType.MESH
N_DEV = 8

MLA_HEADS = 8
QK_NOPE = 128
QK_ROPE = 64
QK_DIM = QK_NOPE + QK_ROPE
V_DIM = 128
KV_LORA = 256
ROPE_THETA = 10000.0
SWA_HEADS = 16
SWA_KV_HEADS = 4
SWA_GROUP = SWA_HEADS // SWA_KV_HEADS
SWA_HEAD_DIM = 64
WINDOW = 128
EPS = 1e-6
LOG2E = 1.4426950408889634

ADAM_LR = 0.001
ADAM_B1 = 0.9
ADAM_B2 = 0.999
ADAM_EPS = 1e-08
ADAM_WD = 0.01
ADAM_STEP = 10

PACK_COLS = 1024
RS_CHUNKS = 4
VMEM_LIMIT = 56 << 20
ROW_TILE = 512
ROW_TILE_WIDE = 256
ATT_TILE = 512
TN_TOKENS = 2048


def _dot(a, b):
    return jnp.dot(a, b, preferred_element_type=F32)


def _dot_nt(a, b):
    return lax.dot_general(a, b, (((1,), (1,)), ((), ())), preferred_element_type=F32)


def _dot_tn(a, b):
    return lax.dot_general(a, b, (((0,), (0,)), ((), ())), preferred_element_type=F32)


def _rstd(x):
    return lax.rsqrt(jnp.mean(x * x, axis=-1, keepdims=True) + EPS)


def _rms_bwd(dn, n, r):
    return r * (dn - n * jnp.mean(dn * n, axis=-1, keepdims=True))


def _modulate(x, g, sc, sh):
    r = _rstd(x)
    return ((x * r) * g) * (1.0 + sc) + sh


def _modulate_bwd(dh, x, g, sc):
    r = _rstd(x)
    n = x * r
    dsh = jnp.sum(dh, axis=0, keepdims=True)
    da = jnp.sum(dh * n, axis=0, keepdims=True)
    dx = _rms_bwd(dh * (g * (1.0 + sc)), n, r)
    return dx, dsh, da


def _first(i):
    return i == 0


def _acc(ref, val, i):
    @pl.when(i == 0)
    def _():
        ref[...] = val

    @pl.when(i != 0)
    def _():
        ref[...] += val


def _row_spec(shape, tm):
    nd = len(shape)
    return pl.BlockSpec(tuple(shape[:nd - 2]) + (tm, shape[-1]), lambda i: (0,) * (nd - 2) + (i, 0))


def _resident_spec(shape, single_buffer):
    nd = len(shape)
    if single_buffer:
        return pl.BlockSpec(tuple(shape), lambda i: (0,) * nd, pipeline_mode=pl.Buffered(1))
    return pl.BlockSpec(tuple(shape), lambda i: (0,) * nd)


def _rowcall(name, body, tokens, tm, row_in, full_in, row_out, acc_out=()):
    tm = min(tm, tokens)
    in_specs = [_row_spec(a.shape, tm) for a in row_in] + [_resident_spec(a.shape, True) for a in full_in]
    row_specs = [s[1] if isinstance(s, tuple) else _row_spec(s.shape, tm) for s in row_out]
    row_out = [s[0] if isinstance(s, tuple) else s for s in row_out]
    out_specs = row_specs + [_resident_spec(s.shape, False) for s in acc_out]
    return pl.pallas_call(
        body, name=name, grid=(tokens // tm,), in_specs=in_specs, out_specs=out_specs,
        out_shape=list(row_out) + list(acc_out),
        compiler_params=pltpu.CompilerParams(dimension_semantics=("arbitrary",), vmem_limit_bytes=VMEM_LIMIT),
    )(*row_in, *full_in)


def _sds(shape, dtype):
    return jax.ShapeDtypeStruct(tuple(shape), dtype)


def _mla_in_fwd(x, cos, sin, g, sc, sh, w_cat, g_q, w_uqx, g_kv, w_ukv, t):
    S, D = x.shape
    QL = g_q.shape[1]
    H = MLA_HEADS
    t = min(t, S)

    def body(x_ref, cos_ref, sin_ref, g_ref, sc_ref, sh_ref, wcat_ref, gq_ref, wuqx_ref, gkv_ref, wukv_ref,
             h_ref, cqp_ref, cq_ref, ckvp_ref, ckv_ref, q_ref, k_ref, v_ref, vt_ref):
        cs, sn = cos_ref[...], sin_ref[...]
        hb = _modulate(x_ref[...], g_ref[...], sc_ref[...], sh_ref[...]).astype(BF16)
        h_ref[...] = hb
        low = _dot(hb, wcat_ref[...])
        cqp = low[:, :QL]
        cqp_ref[...] = cqp
        cq = ((cqp * _rstd(cqp)) * gq_ref[...]).astype(BF16)
        cq_ref[...] = cq
        ckvp = low[:, QL:QL + KV_LORA]
        ckvp_ref[...] = ckvp
        ckv = ((ckvp * _rstd(ckvp)) * gkv_ref[...]).astype(BF16)
        ckv_ref[...] = ckv
        o = QL + KV_LORA
        kr = (low[:, o:o + QK_ROPE] * cs + low[:, o + QK_ROPE:o + 2 * QK_ROPE] * sn).astype(BF16)
        qx = _dot(cq, wuqx_ref[...])
        kv = _dot(ckv, wukv_ref[...])
        for hd in range(H):
            b = hd * 256
            q_ref[hd, :, 0:QK_NOPE] = qx[:, b:b + QK_NOPE].astype(BF16)
            q_ref[hd, :, QK_NOPE:QK_DIM] = (qx[:, b + 128:b + 192] * cs + qx[:, b + 192:b + 256] * sn).astype(BF16)
            k_ref[hd, :, 0:QK_NOPE] = kv[:, b:b + QK_NOPE].astype(BF16)
            k_ref[hd, :, QK_NOPE:QK_DIM] = kr
            vh = kv[:, b + 128:b + 256]
            v_ref[hd] = vh.astype(BF16)
            vt_ref[hd, 0, 0:V_DIM, :] = vh.T.astype(BF16)
            vt_ref[hd, 0, V_DIM:2 * V_DIM, :] = jnp.ones((V_DIM, x_ref.shape[0]), BF16)

    vt_spec = pl.BlockSpec((H, 1, 2 * V_DIM, t), lambda i: (0, i, 0, 0))
    return _rowcall(
        "mla_in_fwd", body, S, t, [x, cos, sin], [g, sc, sh, w_cat, g_q, w_uqx, g_kv, w_ukv],
        [_sds((S, D), BF16), _sds((S, QL), F32), _sds((S, QL), BF16), _sds((S, KV_LORA), F32), _sds((S, KV_LORA), BF16),
         _sds((H, S, QK_DIM), BF16), _sds((H, S, QK_DIM), BF16), _sds((H, S, V_DIM), BF16),
         (_sds((H, S // t, 2 * V_DIM, t), BF16), vt_spec)])


def _mla_attn_fwd(q, k, vt, t, send):
    H, S, DQ = q.shape
    DV = V_DIM
    t = min(t, S)
    nb = S // t
    scale = QK_DIM ** -0.5
    c2 = scale * LOG2E

    def body(q_ref, k_ref, vt_ref, send_ref, o_ref, lse_ref, gath_ref, m_s, acc_s, s_buf, send_sems, recv_sems,
             local_sem):
        hd, qi = pl.program_id(0), pl.program_id(1)

        def gather():
            return _exchange_copies(lambda j: send_ref, gath_ref, send_sems, recv_sems, local_sem)

        @pl.when((hd == 0) & (qi == 0))
        def _():
            _start_exchange(gather())

        m_s[...] = jnp.full_like(m_s, -jnp.inf)
        acc_s[...] = jnp.zeros_like(acc_s)

        def scores(j, slot):
            rows = pl.ds(pl.multiple_of(j * t, t), t)
            s_buf[slot] = _dot_nt(k_ref[0, rows, :], q_ref[0])

        def absorb(j, slot, diagonal):
            s = s_buf[slot]
            if diagonal:
                key = lax.broadcasted_iota(jnp.int32, (t, t), 0)
                qry = lax.broadcasted_iota(jnp.int32, (t, t), 1)
                s = jnp.where(key <= qry, s, -jnp.inf)
            m_prev = m_s[...]
            m_new = jnp.maximum(m_prev, jnp.max(s, axis=0, keepdims=True))
            alpha = jnp.exp2((m_prev - m_new) * c2)
            p = jnp.exp2((s - m_new) * c2)
            acc_s[...] = alpha * acc_s[...] + _dot(vt_ref[0, j], p.astype(BF16))
            m_s[...] = m_new

        def pair(i, carry):
            j = 2 * i
            scores(j + 1, 1)
            absorb(j, 0, False)
            scores(j + 2, 0)
            absorb(j + 1, 1, False)
            return carry

        scores(0, 0)
        lax.fori_loop(0, qi // 2, pair, 0)

        @pl.when(qi % 2 == 0)
        def _():
            absorb(qi, 0, True)

        @pl.when(qi % 2 == 1)
        def _():
            scores(qi, 1)
            absorb(qi - 1, 0, False)
            absorb(qi, 1, True)

        acc = acc_s[...]
        o_ref[...] = (acc[:DV] / acc[DV:]).T.astype(BF16)
        lse_ref[0, 0] = m_s[...] * scale + jnp.log(acc[DV:DV + 1])

        @pl.when((hd == H - 1) & (qi == nb - 1))
        def _():
            _finish_exchange(gather())

    return pl.pallas_call(
        body, name="mla_attn_fwd", grid=(H, nb),
        in_specs=[pl.BlockSpec((1, t, DQ), lambda h, i: (h, i, 0)),
                  pl.BlockSpec((1, S, DQ), lambda h, i: (h, 0, 0)),
                  pl.BlockSpec((1, nb, 2 * DV, t), lambda h, i: (h, 0, 0, 0)), ANY_SPEC],
        out_specs=[pl.BlockSpec((t, DV), lambda h, i: (i, h)),
                   pl.BlockSpec((1, 1, 1, t), lambda h, i: (h, i, 0, 0)), ANY_SPEC],
        out_shape=[_sds((S, H * DV), BF16), _sds((H, nb, 1, t), F32), _sds((N_DEV,) + send.shape, send.dtype)],
        scratch_shapes=[pltpu.VMEM((1, t), F32), pltpu.VMEM((2 * DV, t), F32), pltpu.VMEM((2, t, t), F32)] + COMM_SEMS,
        compiler_params=pltpu.CompilerParams(dimension_semantics=("arbitrary", "arbitrary"),
                                             vmem_limit_bytes=VMEM_LIMIT),
    )(q, k, vt, send)


def _attn_out_fwd(o, x, w_o, b_o, gt, g, sc, sh):
    S, D = x.shape

    def body(o_ref, x_ref, wo_ref, bo_ref, gt_ref, g_ref, sc_ref, sh_ref, y_ref, x1_ref, h_ref):
        y = _dot(o_ref[...], wo_ref[...]) + bo_ref[...]
        y_ref[...] = y
        x1 = x_ref[...] + gt_ref[...] * y
        x1_ref[...] = x1
        h_ref[...] = _modulate(x1, g_ref[...], sc_ref[...], sh_ref[...]).astype(BF16)

    return _rowcall("attn_out_fwd", body, S, ROW_TILE, [o, x], [w_o, b_o, gt, g, sc, sh],
                    [_sds((S, D), F32), _sds((S, D), F32), _sds((S, D), BF16)])


def _mlp_fwd(h, x, w1, w2, gt):
    S, D = x.shape
    FF = w1.shape[1]

    def body(h_ref, x_ref, w1_ref, w2_ref, gt_ref, rl_ref, act_ref, y_ref, x2_ref):
        rl = jnp.maximum(_dot(h_ref[...], w1_ref[...]), 0.0)
        rl_ref[...] = rl.astype(BF16)
        act = (rl * rl).astype(BF16)
        act_ref[...] = act
        y = _dot(act, w2_ref[...])
        y_ref[...] = y
        x2_ref[...] = x_ref[...] + gt_ref[...] * y

    return _rowcall("mlp_fwd", body, S, ROW_TILE_WIDE, [h, x], [w1, w2, gt],
                    [_sds((S, FF), BF16), _sds((S, FF), BF16), _sds((S, D), F32), _sds((S, D), F32)])


def _swa_in_fwd(x, g, sc, sh, w_qkv, b_qkv):
    S, D = x.shape
    NQ = SWA_HEADS * SWA_HEAD_DIM
    NK = SWA_KV_HEADS * SWA_HEAD_DIM

    def body(x_ref, g_ref, sc_ref, sh_ref, w_ref, b_ref, h_ref, q_ref, k_ref, v_ref):
        hb = _modulate(x_ref[...], g_ref[...], sc_ref[...], sh_ref[...]).astype(BF16)
        h_ref[...] = hb
        qkv = _dot(hb, w_ref[...]) + b_ref[...]
        q_ref[...] = qkv[:, :NQ].astype(BF16)
        k_ref[...] = qkv[:, NQ:NQ + NK].astype(BF16)
        v_ref[...] = qkv[:, NQ + NK:].astype(BF16)

    return _rowcall("swa_in_fwd", body, S, ROW_TILE, [x], [g, sc, sh, w_qkv, b_qkv],
                    [_sds((S, D), BF16), _sds((S, NQ), BF16), _sds((S, NK), BF16), _sds((S, NK), BF16)])


def _alibi_slope(head):
    return float(np.float32(2.0 ** (-8.0 * (head + 1) / SWA_HEADS)))


def _swa_geometry(n):
    W = WINDOW
    key = lax.broadcasted_iota(jnp.int32, (2 * W, W), 0)
    qry = lax.broadcasted_iota(jnp.int32, (2 * W, W), 1)
    dist = W + qry - key
    valid = (dist >= 0) & (dist < W) & ((n > 0) | (key >= W))
    return dist.astype(F32), valid


def _swa_band_specs(W, nb, cols):
    prev = pl.BlockSpec((W, cols), lambda n: (jnp.maximum(jnp.minimum(n, nb - 1) - 1, 0), 0))
    cur = pl.BlockSpec((W, cols), lambda n: (jnp.minimum(n, nb - 1), 0))
    return prev, cur


def _swa_attn_fwd(q, k, v, sinks):
    S, NQ = q.shape
    NK = k.shape[1]
    W, Dh, G = WINDOW, SWA_HEAD_DIM, SWA_GROUP
    nb = S // W

    def body(q_ref, kp_ref, kc_ref, vp_ref, vc_ref, sink_ref, o_ref, lse_ref):
        distf, valid = _swa_geometry(pl.program_id(0))
        kband = jnp.concatenate([kp_ref[...], kc_ref[...]], axis=0)
        vband_t = jnp.concatenate([vp_ref[...], vc_ref[...]], axis=0).astype(F32).T.astype(BF16)
        outs = []
        for kh in range(SWA_KV_HEADS):
            kb = kband[:, kh * Dh:(kh + 1) * Dh]
            vbt = vband_t[kh * Dh:(kh + 1) * Dh, :]
            for gi in range(G):
                hq = kh * G + gi
                s = _dot_nt(kb, q_ref[:, hq * Dh:(hq + 1) * Dh]) * (Dh ** -0.5) - _alibi_slope(hq) * distf
                s = jnp.where(valid, s, -jnp.inf)
                sink = sink_ref[:, hq:hq + 1]
                m = jnp.maximum(jnp.max(s, axis=0, keepdims=True), sink)
                p = jnp.exp(s - m)
                denom = jnp.sum(p, axis=0, keepdims=True) + jnp.exp(sink - m)
                outs.append(_dot(vbt, (p * (1.0 / denom)).astype(BF16)))
                lse_ref[hq:hq + 1, :] = m + jnp.log(denom)
        o_ref[...] = jnp.concatenate(outs, axis=0).T.astype(BF16)

    kprev, kcur = _swa_band_specs(W, nb, NK)
    return pl.pallas_call(
        body, name="swa_attn_fwd", grid=(nb,),
        in_specs=[pl.BlockSpec((W, NQ), lambda n: (n, 0)), kprev, kcur, kprev, kcur,
                  pl.BlockSpec((1, SWA_HEADS), lambda n: (0, 0))],
        out_specs=[pl.BlockSpec((W, NQ), lambda n: (n, 0)), pl.BlockSpec((SWA_HEADS, W), lambda n: (0, n))],
        out_shape=[_sds((S, NQ), BF16), _sds((SWA_HEADS, S), F32)],
        compiler_params=pltpu.CompilerParams(dimension_semantics=("arbitrary",), vmem_limit_bytes=VMEM_LIMIT),
    )(q, k, k, v, v, sinks)


def _final_loss(x, target, g):
    S, D = x.shape

    def body(x_ref, t_ref, g_ref, dx_ref, loss_ref, dg_ref):
        i = pl.program_id(0)
        xv = x_ref[...]
        r = _rstd(xv)
        n = xv * r
        err = n * g_ref[...] - t_ref[...]
        part = 0.5 * jnp.sum(jnp.mean(err * err, axis=-1, keepdims=True), axis=0, keepdims=True)
        _acc(loss_ref, jnp.broadcast_to(part, loss_ref.shape), i)
        dout = err / D
        _acc(dg_ref, jnp.sum(dout * n, axis=0, keepdims=True), i)
        dx_ref[...] = _rms_bwd(dout * g_ref[...], n, r)

    return _rowcall("final_loss", body, S, ROW_TILE, [x, target], [g], [_sds((S, D), F32)],
                    [_sds((1, 128), F32), _sds((1, D), F32)])


def _mlp_bwd_a(dx, y, rl, gt, w2):
    S, D = dx.shape
    FF = rl.shape[1]

    def body(dx_ref, y_ref, rl_ref, gt_ref, w2_ref, dy_ref, du_ref, dgt_ref):
        i = pl.program_id(0)
        dxv = dx_ref[...]
        _acc(dgt_ref, jnp.sum(dxv * y_ref[...], axis=0, keepdims=True), i)
        dy = (dxv * gt_ref[...]).astype(BF16)
        dy_ref[...] = dy
        dact = _dot_nt(dy, w2_ref[...])
        du_ref[...] = (dact * (2.0 * rl_ref[...].astype(F32))).astype(BF16)

    return _rowcall("mlp_bwd_a", body, S, ROW_TILE_WIDE, [dx, y, rl], [gt, w2],
                    [_sds((S, D), BF16), _sds((S, FF), BF16)], [_sds((1, D), F32)])


def _mlp_bwd_b(du, x, dx, w1, g, sc):
    S, D = x.shape

    def body(du_ref, x_ref, dx_ref, w1_ref, g_ref, sc_ref, dxo_ref, dsh_ref, da_ref):
        i = pl.program_id(0)
        dh = _dot_nt(du_ref[...], w1_ref[...])
        dxn, dsh, da = _modulate_bwd(dh, x_ref[...], g_ref[...], sc_ref[...])
        dxo_ref[...] = dx_ref[...] + dxn
        _acc(dsh_ref, dsh, i)
        _acc(da_ref, da, i)

    return _rowcall("mlp_bwd_b", body, S, ROW_TILE_WIDE, [du, x, dx], [w1, g, sc],
                    [_sds((S, D), F32)], [_sds((1, D), F32), _sds((1, D), F32)])


def _attn_out_bwd(dx, y, o, gt, w_o, n_heads):
    S, D = dx.shape
    NO = o.shape[1]
    dh = NO // n_heads

    def body(dx_ref, y_ref, o_ref, gt_ref, wo_ref, dy_ref, do_ref, dl_ref, dgt_ref, dbo_ref):
        i = pl.program_id(0)
        dxv = dx_ref[...]
        _acc(dgt_ref, jnp.sum(dxv * y_ref[...], axis=0, keepdims=True), i)
        dy = dxv * gt_ref[...]
        _acc(dbo_ref, jnp.sum(dy, axis=0, keepdims=True), i)
        dyb = dy.astype(BF16)
        dy_ref[...] = dyb
        do = _dot_nt(dyb, wo_ref[...])
        do_ref[...] = do.astype(BF16)
        prod = do * o_ref[...].astype(F32)
        for hd in range(n_heads):
            dl_ref[:, hd:hd + 1] = jnp.sum(prod[:, hd * dh:(hd + 1) * dh], axis=-1, keepdims=True)

    return _rowcall("attn_out_bwd", body, S, ROW_TILE, [dx, y, o], [gt, w_o],
                    [_sds((S, D), BF16), _sds((S, NO), BF16), _sds((S, n_heads), F32)],
                    [_sds((1, D), F32), _sds((1, D), F32)])


def _mla_attn_bwd(q, k, v, do, lse, delta, t, gblk):
    H, S, DQ = q.shape
    DV = V_DIM
    t = min(t, S)
    nb = S // t
    scale = QK_DIM ** -0.5
    c2 = scale * LOG2E

    def body(q_ref, k_ref, v_ref, do_ref, lse_ref, dl_ref, g_ref, dq_ref, dk_ref, dv_ref, recv_ref, dk_s, dv_s,
             s_buf, dp_buf, send_sems, recv_sems, local_sem):
        hd, kj = pl.program_id(0), pl.program_id(1)

        def scatter():
            return _exchange_copies(lambda j: g_ref.at[j], recv_ref, send_sems, recv_sems, local_sem)

        @pl.when((hd == 0) & (kj == 0))
        def _():
            _start_exchange(scatter())

        @pl.when(kj == 0)
        def _():
            dq_ref[...] = jnp.zeros_like(dq_ref)

        dk_s[...] = jnp.zeros_like(dk_s)
        dv_s[...] = jnp.zeros_like(dv_s)

        def products(i, slot):
            rows = pl.ds(pl.multiple_of(i * t, t), t)
            s_buf[slot] = _dot_nt(k_ref[0], q_ref[0, rows, :])
            dp_buf[slot] = _dot_nt(v_ref[0], do_ref[rows, :])

        def absorb(i, slot, diagonal):
            rows = pl.ds(pl.multiple_of(i * t, t), t)
            qb, dob = q_ref[0, rows, :], do_ref[rows, :]
            p = jnp.exp2(s_buf[slot] * c2 - lse_ref[0, i])
            if diagonal:
                key = lax.broadcasted_iota(jnp.int32, (t, t), 0)
                qry = lax.broadcasted_iota(jnp.int32, (t, t), 1)
                p = jnp.where(key <= qry, p, 0.0)
            dv_s[...] += _dot(p.astype(BF16), dob)
            ds = (p * (dp_buf[slot] - dl_ref[0, i])).astype(BF16)
            dk_s[...] += _dot(ds, qb)
            dq_ref[0, rows, :] += _dot_tn(ds, k_ref[0])

        n_off = nb - 1 - kj
        first = kj + 1

        def pair(i, carry):
            u = 2 * i
            products(first + u + 1, 1)
            absorb(first + u, 0, False)
            products(jnp.where(u + 2 < n_off, first + u + 2, kj), 0)
            absorb(first + u + 1, 1, False)
            return carry

        products(jnp.where(n_off > 0, first, kj), 0)
        lax.fori_loop(0, n_off // 2, pair, 0)

        @pl.when(n_off % 2 == 0)
        def _():
            absorb(kj, 0, True)

        @pl.when(n_off % 2 == 1)
        def _():
            products(kj, 1)
            absorb(nb - 1, 0, False)
            absorb(kj, 1, True)

        dk_ref[0] = (dk_s[...] * scale).astype(BF16)
        dv_ref[0] = dv_s[...].astype(BF16)

        @pl.when((hd == H - 1) & (kj == nb - 1))
        def _():
            _finish_exchange(scatter())

    rowspec = pl.BlockSpec((1, nb, 1, t), lambda h, j: (h, 0, 0, 0))
    return pl.pallas_call(
        body, name="mla_attn_bwd", grid=(H, nb),
        in_specs=[pl.BlockSpec((1, S, DQ), lambda h, j: (h, 0, 0)),
                  pl.BlockSpec((1, t, DQ), lambda h, j: (h, j, 0)),
                  pl.BlockSpec((1, t, DV), lambda h, j: (h, j, 0)),
                  pl.BlockSpec((S, DV), lambda h, j: (0, h)), rowspec, rowspec, ANY_SPEC],
        out_specs=[pl.BlockSpec((1, S, DQ), lambda h, j: (h, 0, 0)),
                   pl.BlockSpec((1, t, DQ), lambda h, j: (h, j, 0)),
                   pl.BlockSpec((1, t, DV), lambda h, j: (h, j, 0)), ANY_SPEC],
        out_shape=[_sds((H, S, DQ), F32), _sds((H, S, DQ), BF16), _sds((H, S, DV), BF16), _sds(gblk.shape, gblk.dtype)],
        scratch_shapes=[pltpu.VMEM((t, DQ), F32), pltpu.VMEM((t, DV), F32), pltpu.VMEM((2, t, t), F32),
                        pltpu.VMEM((2, t, t), F32)] + COMM_SEMS,
        compiler_params=pltpu.CompilerParams(dimension_semantics=("arbitrary", "arbitrary"),
                                             vmem_limit_bytes=VMEM_LIMIT),
    )(q, k, v, do, lse, delta, gblk)


def _swa_attn_bwd(q, k, v, do, lse, delta, sinks):
    S, NQ = q.shape
    NK = k.shape[1]
    W, Dh, G = WINDOW, SWA_HEAD_DIM, SWA_GROUP
    nb = S // W

    def body(q_ref, kp_ref, kc_ref, vp_ref, vc_ref, do_ref, lse_ref, dl_ref, sink_ref,
             dq_ref, dk_ref, dv_ref, dsink_ref, dkc_s, dvc_s):
        n = pl.program_id(0)

        @pl.when(n == 0)
        def _():
            dkc_s[...] = jnp.zeros_like(dkc_s)
            dvc_s[...] = jnp.zeros_like(dvc_s)
            dsink_ref[...] = jnp.zeros_like(dsink_ref)

        @pl.when(n < nb)
        def _():
            distf, valid = _swa_geometry(n)
            kband = jnp.concatenate([kp_ref[...], kc_ref[...]], axis=0)
            vband = jnp.concatenate([vp_ref[...], vc_ref[...]], axis=0)
            kband_t = kband.astype(F32).T.astype(BF16)
            dq_t = []
            for kh in range(SWA_KV_HEADS):
                ck = slice(kh * Dh, (kh + 1) * Dh)
                kb, vb, kbt = kband[:, ck], vband[:, ck], kband_t[ck, :]
                dkb = jnp.zeros((2 * W, Dh), F32)
                dvb = jnp.zeros((2 * W, Dh), F32)
                for gi in range(G):
                    hq = kh * G + gi
                    cq = slice(hq * Dh, (hq + 1) * Dh)
                    qh, doh = q_ref[:, cq], do_ref[:, cq]
                    lse_h = lse_ref[hq:hq + 1, :]
                    dl_h = dl_ref[hq:hq + 1, :]
                    s = _dot_nt(kb, qh) * (Dh ** -0.5) - _alibi_slope(hq) * distf
                    p = jnp.where(valid, jnp.exp(s - lse_h), 0.0)
                    dvb = dvb + _dot(p.astype(BF16), doh)
                    dp = _dot_nt(vb, doh)
                    dsb = ((p * (dp - dl_h)) * (Dh ** -0.5)).astype(BF16)
                    dq_t.append(_dot(kbt, dsb))
                    dkb = dkb + _dot(dsb, qh)
                    psink = jnp.exp(sink_ref[:, hq:hq + 1] - lse_h)
                    dsink_ref[:, hq:hq + 1] += -jnp.sum(psink * dl_h, axis=1, keepdims=True)
                dk_ref[:, ck] = (dkc_s[:, ck] + dkb[:W]).astype(BF16)
                dv_ref[:, ck] = (dvc_s[:, ck] + dvb[:W]).astype(BF16)
                dkc_s[:, ck] = dkb[W:]
                dvc_s[:, ck] = dvb[W:]
            dq_ref[...] = jnp.concatenate(dq_t, axis=0).T.astype(BF16)

        @pl.when(n == nb)
        def _():
            dk_ref[...] = dkc_s[...].astype(BF16)
            dv_ref[...] = dvc_s[...].astype(BF16)

    kprev, kcur = _swa_band_specs(W, nb, NK)
    qspec = lambda cols: pl.BlockSpec((W, cols), lambda n: (jnp.minimum(n, nb - 1), 0))
    kvout = pl.BlockSpec((W, NK), lambda n: (jnp.maximum(n - 1, 0), 0))
    rowspec = pl.BlockSpec((SWA_HEADS, W), lambda n: (0, jnp.minimum(n, nb - 1)))
    return pl.pallas_call(
        body, name="swa_attn_bwd", grid=(nb + 1,),
        in_specs=[qspec(NQ), kprev, kcur, kprev, kcur, qspec(NQ), rowspec, rowspec,
                  pl.BlockSpec((1, SWA_HEADS), lambda n: (0, 0))],
        out_specs=[qspec(NQ), kvout, kvout, pl.BlockSpec((1, 128), lambda n: (0, 0))],
        out_shape=[_sds((S, NQ), BF16), _sds((S, NK), BF16), _sds((S, NK), BF16), _sds((1, 128), F32)],
        scratch_shapes=[pltpu.VMEM((W, NK), F32), pltpu.VMEM((W, NK), F32)],
        compiler_params=pltpu.CompilerParams(dimension_semantics=("arbitrary",), vmem_limit_bytes=VMEM_LIMIT),
    )(q, k, k, v, v, do, lse, delta, sinks)


def _swa_in_bwd(dq, dk, dv, x, dx, w_qkv, g, sc):
    S, D = x.shape
    N = w_qkv.shape[1]

    def body(dq_ref, dk_ref, dv_ref, x_ref, dx_ref, w_ref, g_ref, sc_ref, dqkv_ref, dxo_ref, db_ref, dsh_ref, da_ref):
        i = pl.program_id(0)
        dqkv = jnp.concatenate([dq_ref[...], dk_ref[...], dv_ref[...]], axis=1)
        dqkv_ref[...] = dqkv
        _acc(db_ref, jnp.sum(dqkv.astype(F32), axis=0, keepdims=True), i)
        dh = _dot_nt(dqkv, w_ref[...])
        dxn, dsh, da = _modulate_bwd(dh, x_ref[...], g_ref[...], sc_ref[...])
        dxo_ref[...] = dx_ref[...] + dxn
        _acc(dsh_ref, dsh, i)
        _acc(da_ref, da, i)

    return _rowcall("swa_in_bwd", body, S, ROW_TILE, [dq, dk, dv, x, dx], [w_qkv, g, sc],
                    [_sds((S, N), BF16), _sds((S, D), F32)],
                    [_sds((1, N), F32), _sds((1, D), F32), _sds((1, D), F32)])


def _mla_in_bwd(dq, dk, dv, cos, sin, cqp, ckvp, x, dx, w_uqx, g_q, w_ukv, g_kv, w_cat, g, sc):
    S, D = x.shape
    H = MLA_HEADS
    QL = g_q.shape[1]
    NX = w_uqx.shape[1]
    NC = w_cat.shape[1]

    def body(dq_ref, dk_ref, dv_ref, cos_ref, sin_ref, cqp_ref, ckvp_ref, x_ref, dx_ref,
             wuqx_ref, gq_ref, wukv_ref, gkv_ref, wcat_ref, g_ref, sc_ref,
             dqx_ref, dkv_ref, dcat_ref, dxo_ref, dgq_ref, dgkv_ref, dsh_ref, da_ref):
        i = pl.program_id(0)
        cs, sn = cos_ref[...], sin_ref[...]
        dkr = jnp.zeros(cs.shape, F32)
        for hd in range(H):
            b = hd * 256
            dqh = dq_ref[hd] * (QK_DIM ** -0.5)
            dqx_ref[:, b:b + QK_NOPE] = dqh[:, :QK_NOPE].astype(BF16)
            dqx_ref[:, b + 128:b + 192] = (dqh[:, QK_NOPE:] * cs).astype(BF16)
            dqx_ref[:, b + 192:b + 256] = (dqh[:, QK_NOPE:] * sn).astype(BF16)
            dkh = dk_ref[hd]
            dkv_ref[:, b:b + QK_NOPE] = dkh[:, :QK_NOPE]
            dkv_ref[:, b + 128:b + 256] = dv_ref[hd]
            dkr = dkr + dkh[:, QK_NOPE:].astype(F32)
        dcq = _dot_nt(dqx_ref[...], wuqx_ref[...])
        cqp = cqp_ref[...]
        rq = _rstd(cqp)
        nq = cqp * rq
        _acc(dgq_ref, jnp.sum(dcq * nq, axis=0, keepdims=True), i)
        dcqp = _rms_bwd(dcq * gq_ref[...], nq, rq)
        dckv = _dot_nt(dkv_ref[...], wukv_ref[...])
        ckvp = ckvp_ref[...]
        rk = _rstd(ckvp)
        nk = ckvp * rk
        _acc(dgkv_ref, jnp.sum(dckv * nk, axis=0, keepdims=True), i)
        dckvp = _rms_bwd(dckv * gkv_ref[...], nk, rk)
        dcat_ref[:, :QL] = dcqp.astype(BF16)
        dcat_ref[:, QL:QL + KV_LORA] = dckvp.astype(BF16)
        o = QL + KV_LORA
        dcat_ref[:, o:o + QK_ROPE] = (dkr * cs).astype(BF16)
        dcat_ref[:, o + QK_ROPE:o + 2 * QK_ROPE] = (dkr * sn).astype(BF16)
        dh = _dot_nt(dcat_ref[...], wcat_ref[...])
        dxn, dsh, da = _modulate_bwd(dh, x_ref[...], g_ref[...], sc_ref[...])
        dxo_ref[...] = dx_ref[...] + dxn
        _acc(dsh_ref, dsh, i)
        _acc(da_ref, da, i)

    return _rowcall("mla_in_bwd", body, S, ROW_TILE_WIDE, [dq, dk, dv, cos, sin, cqp, ckvp, x, dx],
                    [w_uqx, g_q, w_ukv, g_kv, w_cat, g, sc],
                    [_sds((S, NX), BF16), _sds((S, NX), BF16), _sds((S, NC), BF16), _sds((S, D), F32)],
                    [_sds((1, QL), F32), _sds((1, KV_LORA), F32), _sds((1, D), F32), _sds((1, D), F32)])


def _matmul_tn(name, a, b):
    S, K = a.shape
    N = b.shape[1]
    tk, tn, ts = min(K, 1024), min(N, 1024), min(S, TN_TOKENS)
    if N % tn:
        tn = 512 if N % 512 == 0 else (384 if N % 384 == 0 else 128)
    if K % tk:
        tk = 512 if K % 512 == 0 else (384 if K % 384 == 0 else 128)
    ns = S // ts

    def body(a_ref, b_ref, o_ref):
        _acc(o_ref, _dot_tn(a_ref[...], b_ref[...]), pl.program_id(2))

    return pl.pallas_call(
        body, name=name, grid=(K // tk, N // tn, ns),
        in_specs=[pl.BlockSpec((ts, tk), lambda i, j, s: (s, i)), pl.BlockSpec((ts, tn), lambda i, j, s: (s, j))],
        out_specs=pl.BlockSpec((tk, tn), lambda i, j, s: (i, j)),
        out_shape=_sds((K, N), F32),
        compiler_params=pltpu.CompilerParams(dimension_semantics=("parallel", "parallel", "arbitrary"),
                                             vmem_limit_bytes=VMEM_LIMIT),
    )(a, b)


def _silu(c):
    return c * jax.nn.sigmoid(c)


def _ada_fwd(c_all, w_ada):
    L, D, NC = w_ada.shape

    def body(c_ref, w_ref, o_ref):
        cond = _silu(c_ref[...]).astype(BF16)
        o_ref[0] = _dot(cond, w_ref[0].astype(BF16))

    return pl.pallas_call(
        body, name="ada_fwd", grid=(L,),
        in_specs=[pl.BlockSpec(c_all.shape, lambda l: (0, 0)), pl.BlockSpec((1, D, NC), lambda l: (l, 0, 0))],
        out_specs=pl.BlockSpec((1, N_DEV, NC), lambda l: (l, 0, 0)),
        out_shape=_sds((L, N_DEV, NC), F32),
        compiler_params=pltpu.CompilerParams(dimension_semantics=("arbitrary",), vmem_limit_bytes=VMEM_LIMIT),
    )(c_all, w_ada)


def _adamw(w, g, m, v):
    m = ADAM_B1 * m + (1.0 - ADAM_B1) * g
    v = ADAM_B2 * v + (1.0 - ADAM_B2) * (g * g)
    m_hat = m / (1.0 - ADAM_B1 ** ADAM_STEP)
    v_hat = v / (1.0 - ADAM_B2 ** ADAM_STEP)
    delta = -ADAM_LR * (m_hat / (jnp.sqrt(v_hat) + ADAM_EPS) + ADAM_WD * w)
    return delta, m, v


def _ada_bwd_adamw(c_all_t, dmod_cols, w, m, v):
    L, D, NC = w.shape
    tr = min(D, 256)

    def body(ct_ref, dm_ref, w_ref, m_ref, v_ref, g_ref, d_ref, mo_ref, vo_ref):
        cond_t = _silu(ct_ref[...])
        dm = dm_ref[0]
        g = cond_t[:, 0:1] * dm[0:1, :]
        for b in range(1, N_DEV):
            g = g + cond_t[:, b:b + 1] * dm[b:b + 1, :]
        g_ref[0] = g
        d_ref[0], mo_ref[0], vo_ref[0] = _adamw(w_ref[0], g, m_ref[0], v_ref[0])

    wspec = pl.BlockSpec((1, tr, NC), lambda l, r: (l, r, 0))
    return pl.pallas_call(
        body, name="ada_bwd_adamw", grid=(L, D // tr),
        in_specs=[pl.BlockSpec((tr, N_DEV), lambda l, r: (r, 0)),
                  pl.BlockSpec((1, N_DEV, NC), lambda l, r: (l, 0, 0)), wspec, wspec, wspec],
        out_specs=[wspec] * 4, out_shape=[_sds(w.shape, F32)] * 4,
        compiler_params=pltpu.CompilerParams(dimension_semantics=("parallel", "parallel"), vmem_limit_bytes=VMEM_LIMIT),
    )(c_all_t, dmod_cols, w, m, v)


def _sum_devices(x):
    def body(x_ref, o_ref):
        s = x_ref[0]
        for j in range(1, N_DEV):
            s = s + x_ref[j]
        o_ref[...] = s

    return pl.pallas_call(body, name="sum_devices", out_shape=_sds(x.shape[1:], F32))(x)


def _adamw_small(w, g, m, v):
    def body(w_ref, g_ref, m_ref, v_ref, d_ref, mo_ref, vo_ref):
        d_ref[...], mo_ref[...], vo_ref[...] = _adamw(w_ref[...], g_ref[...], m_ref[...], v_ref[...])

    return pl.pallas_call(body, name="adamw_small", out_shape=[_sds(w.shape, F32)] * 3)(w, g, m, v)


def _me():
    return lax.axis_index("x") * 4 + lax.axis_index("y") * 2 + lax.axis_index("c")


def _peer(k):
    x, y, c = lax.axis_index("x"), lax.axis_index("y"), lax.axis_index("c")
    px = 1 - x if k & 4 else x
    py = 1 - y if k & 2 else y
    pc = 1 - c if k & 1 else c
    return (px, py, pc), px * 4 + py * 2 + pc


VMEM_SPEC = pl.BlockSpec(memory_space=pltpu.VMEM)
ANY_SPEC = pl.BlockSpec(memory_space=pl.ANY)
COMM_SEMS = [pltpu.SemaphoreType.DMA((N_DEV - 1,)), pltpu.SemaphoreType.DMA((N_DEV - 1,)), pltpu.SemaphoreType.DMA(())]


def _exchange_copies(src_of, dst_ref, send_sems, recv_sems, local_sem):
    me = _me()
    local = pltpu.make_async_copy(src_of(me), dst_ref.at[me], local_sem)
    sends, recvs = [], []
    for k in range(1, N_DEV):
        dev, pj = _peer(k)
        sems = dict(send_sem=send_sems.at[k - 1], recv_sem=recv_sems.at[k - 1], device_id=dev, device_id_type=MESH_IDS)
        sends.append(pltpu.make_async_remote_copy(src_ref=src_of(pj), dst_ref=dst_ref.at[me], **sems))
        recvs.append(pltpu.make_async_remote_copy(src_ref=src_of(pj), dst_ref=dst_ref.at[pj], **sems))
    return local, sends, recvs


def _start_exchange(copies):
    local, sends, _ = copies
    local.start()
    for cp in sends:
        cp.start()


def _finish_exchange(copies):
    local, sends, recvs = copies
    for cp in recvs:
        cp.wait_recv()
    for cp in sends:
        cp.wait_send()
    local.wait()


def _sum_adamw(recv, w, m, v):
    _, R, C = recv.shape
    rows = max(d for d in range(16, min(R, 256) + 1, 16) if R % d == 0)

    def body(r_ref, w_ref, m_ref, v_ref, go_ref, d_ref, mo_ref, vo_ref):
        g = r_ref[0].astype(F32)
        for j in range(1, N_DEV):
            g = g + r_ref[j].astype(F32)
        go_ref[...] = g
        d_ref[...], mo_ref[...], vo_ref[...] = _adamw(w_ref[...], g, m_ref[...], v_ref[...])

    spec = pl.BlockSpec((rows, C), lambda i: (i, 0))
    return pl.pallas_call(
        body, name="sum_adamw", grid=(R // rows,),
        in_specs=[pl.BlockSpec((N_DEV, rows, C), lambda i: (0, i, 0)), spec, spec, spec],
        out_specs=[spec] * 4, out_shape=[_sds((R, C), F32)] * 4,
        compiler_params=pltpu.CompilerParams(dimension_semantics=("parallel",), vmem_limit_bytes=VMEM_LIMIT),
    )(recv, w, m, v)


def _all_gather(name, x, out_dtype):
    R, C = x.shape
    cast = out_dtype != x.dtype

    def body(x_ref, out_ref, buf, send_sems, recv_sems, local_sem):
        me = _me()
        if cast:
            buf[...] = x_ref[...].astype(out_dtype)
            src = buf
        else:
            src = x_ref
        local = pltpu.make_async_copy(src, out_ref.at[me], local_sem)
        local.start()
        sends = []
        for k in range(1, N_DEV):
            dev, _ = _peer(k)
            cp = pltpu.make_async_remote_copy(src_ref=src, dst_ref=out_ref.at[me], send_sem=send_sems.at[k - 1],
                                              recv_sem=recv_sems.at[k - 1], device_id=dev, device_id_type=MESH_IDS)
            cp.start()
            sends.append(cp)
        for k in range(1, N_DEV):
            dev, pj = _peer(k)
            pltpu.make_async_remote_copy(src_ref=src, dst_ref=out_ref.at[pj], send_sem=send_sems.at[k - 1],
                                         recv_sem=recv_sems.at[k - 1], device_id=dev, device_id_type=MESH_IDS).wait_recv()
        for cp in sends:
            cp.wait_send()
        local.wait()

    return pl.pallas_call(
        body, name=name, in_specs=[VMEM_SPEC], out_specs=ANY_SPEC, out_shape=_sds((N_DEV, R, C), out_dtype),
        scratch_shapes=[pltpu.VMEM((R, C) if cast else (8, 128), out_dtype),
                        pltpu.SemaphoreType.DMA((N_DEV - 1,)), pltpu.SemaphoreType.DMA((N_DEV - 1,)),
                        pltpu.SemaphoreType.DMA(())],
        compiler_params=pltpu.CompilerParams(vmem_limit_bytes=VMEM_LIMIT),
    )(x)


def _all_to_all(name, x):
    _, R, C = x.shape

    def body(x_ref, out_ref, send_sems, recv_sems, local_sem):
        me = _me()
        local = pltpu.make_async_copy(x_ref.at[me], out_ref.at[me], local_sem)
        local.start()
        sends = []
        for k in range(1, N_DEV):
            dev, pj = _peer(k)
            cp = pltpu.make_async_remote_copy(src_ref=x_ref.at[pj], dst_ref=out_ref.at[me], send_sem=send_sems.at[k - 1],
                                              recv_sem=recv_sems.at[k - 1], device_id=dev, device_id_type=MESH_IDS)
            cp.start()
            sends.append(cp)
        for k in range(1, N_DEV):
            dev, pj = _peer(k)
            pltpu.make_async_remote_copy(src_ref=x_ref.at[pj], dst_ref=out_ref.at[pj], send_sem=send_sems.at[k - 1],
                                         recv_sem=recv_sems.at[k - 1], device_id=dev, device_id_type=MESH_IDS).wait_recv()
        for cp in sends:
            cp.wait_send()
        local.wait()

    return pl.pallas_call(
        body, name=name, in_specs=[VMEM_SPEC], out_specs=VMEM_SPEC, out_shape=_sds(x.shape, x.dtype),
        scratch_shapes=[pltpu.SemaphoreType.DMA((N_DEV - 1,)), pltpu.SemaphoreType.DMA((N_DEV - 1,)),
                        pltpu.SemaphoreType.DMA(())],
    )(x)


def _reduce_scatter_adamw(name, gblk, w, m, v):
    _, R, C = gblk.shape
    rows = 8
    for cand in (136, 128, 80, 64, 40, 32, 16, 8):
        if R % cand == 0:
            rows = cand
            break

    def body(g_ref, w_ref, m_ref, v_ref, go_ref, d_ref, mo_ref, vo_ref, recv, send_sems, recv_sems, local_sem):
        me = _me()
        local = pltpu.make_async_copy(g_ref.at[me], recv.at[me], local_sem)
        local.start()
        sends = []
        for k in range(1, N_DEV):
            dev, pj = _peer(k)
            cp = pltpu.make_async_remote_copy(src_ref=g_ref.at[pj], dst_ref=recv.at[me], send_sem=send_sems.at[k - 1],
                                              recv_sem=recv_sems.at[k - 1], device_id=dev, device_id_type=MESH_IDS)
            cp.start()
            sends.append(cp)
        for k in range(1, N_DEV):
            dev, pj = _peer(k)
            pltpu.make_async_remote_copy(src_ref=g_ref.at[pj], dst_ref=recv.at[pj], send_sem=send_sems.at[k - 1],
                                         recv_sem=recv_sems.at[k - 1], device_id=dev, device_id_type=MESH_IDS).wait_recv()
        local.wait()

        def chunk(i, carry):
            r = pl.ds(pl.multiple_of(i * rows, rows), rows)
            g = recv[0, r, :].astype(F32)
            for j in range(1, N_DEV):
                g = g + recv[j, r, :].astype(F32)
            go_ref[r, :] = g
            d_ref[r, :], mo_ref[r, :], vo_ref[r, :] = _adamw(w_ref[r, :], g, m_ref[r, :], v_ref[r, :])
            return carry

        lax.fori_loop(0, R // rows, chunk, 0)
        for cp in sends:
            cp.wait_send()

    return pl.pallas_call(
        body, name=name, in_specs=[ANY_SPEC, VMEM_SPEC, VMEM_SPEC, VMEM_SPEC], out_specs=[VMEM_SPEC] * 4,
        out_shape=[_sds((R, C), F32)] * 4,
        scratch_shapes=[pltpu.VMEM((N_DEV, R, C), BF16), pltpu.SemaphoreType.DMA((N_DEV - 1,)),
                        pltpu.SemaphoreType.DMA((N_DEV - 1,)), pltpu.SemaphoreType.DMA(())],
        compiler_params=pltpu.CompilerParams(vmem_limit_bytes=VMEM_LIMIT),
    )(gblk, w, m, v)


FIRST_WEIGHTS = ["mla_w_dq", "mla_w_uq", "mla_w_dkv", "mla_w_ukv", "mla_w_o"]
LATE_WEIGHTS = ["swa_w_qkv", "swa_w_o", "w_ff1", "w_ff2"]
ROW_SHARDED = {"mla_w_dq", "mla_w_dkv", "mla_w_o", "swa_w_o", "w_ff2"}


def _unblock(name, blocks):
    sh = blocks.shape[1:]
    if name in ROW_SHARDED:
        return jnp.moveaxis(blocks, 0, 1).reshape(sh[0], N_DEV * sh[1], sh[2])
    return jnp.moveaxis(blocks, 0, 2).reshape(sh[0], sh[1], N_DEV * sh[2])


def _block(name, full):
    L, K, N = full.shape
    if name in ROW_SHARDED:
        return jnp.moveaxis(full.reshape(L, N_DEV, K // N_DEV, N), 1, 0)
    return jnp.moveaxis(full.reshape(L, K, N_DEV, N // N_DEV), 2, 0)


def _rot_cols(w):
    half = QK_ROPE // 2
    return jnp.concatenate([-w[..., half:], w[..., :half]], axis=-1)


def _unrot_cols(gw):
    half = QK_ROPE // 2
    return jnp.concatenate([gw[..., half:], -gw[..., :half]], axis=-1)


def _row(v):
    return v.reshape(1, -1)


def _mlp_block_bwd(dx, sv, w1, w2, g, sc, gt):
    dy, du, dgt = _mlp_bwd_a(dx, sv["y2"], sv["rl"], gt, w2)
    dw2 = _matmul_tn("dw_ff2", sv["act"], dy)
    dw1 = _matmul_tn("dw_ff1", sv["h2"], du)
    dxo, dsh, da = _mlp_bwd_b(du, sv["x1"], dx, w1, g, sc)
    return dxo, dw1, dw2, dsh, da, dgt


def kernel(x, c, positions, w_ada, b_ada, g_mix, g_mlp, mla_w_dq, mla_g_q, mla_w_uq, mla_w_dkv, mla_g_kv, mla_w_ukv, mla_w_o, swa_w_qkv, swa_b_qkv, swa_sinks, swa_w_o, swa_b_o, w_ff1, w_ff2, g_final, loss_target, m_w_ada, m_b_ada, m_g_mix, m_g_mlp, m_mla_w_dq, m_mla_g_q, m_mla_w_uq, m_mla_w_dkv, m_mla_g_kv, m_mla_w_ukv, m_mla_w_o, m_swa_w_qkv, m_swa_b_qkv, m_swa_sinks, m_swa_w_o, m_swa_b_o, m_w_ff1, m_w_ff2, m_g_final, v_w_ada, v_b_ada, v_g_mix, v_g_mlp, v_mla_w_dq, v_mla_g_q, v_mla_w_uq, v_mla_w_dkv, v_mla_g_kv, v_mla_w_ukv, v_mla_w_o, v_swa_w_qkv, v_swa_b_qkv, v_swa_sinks, v_swa_w_o, v_swa_b_o, v_w_ff1, v_w_ff2, v_g_final):
    S, D = x.shape[1], x.shape[2]
    me = _me()
    x0 = x[0]
    target = loss_target[0]
    big_w = dict(mla_w_dq=mla_w_dq, mla_w_uq=mla_w_uq, mla_w_dkv=mla_w_dkv, mla_w_ukv=mla_w_ukv, mla_w_o=mla_w_o,
                 swa_w_qkv=swa_w_qkv, swa_w_o=swa_w_o, w_ff1=w_ff1, w_ff2=w_ff2)
    big_m = dict(mla_w_dq=m_mla_w_dq, mla_w_uq=m_mla_w_uq, mla_w_dkv=m_mla_w_dkv, mla_w_ukv=m_mla_w_ukv,
                 mla_w_o=m_mla_w_o, swa_w_qkv=m_swa_w_qkv, swa_w_o=m_swa_w_o, w_ff1=m_w_ff1, w_ff2=m_w_ff2)
    big_v = dict(mla_w_dq=v_mla_w_dq, mla_w_uq=v_mla_w_uq, mla_w_dkv=v_mla_w_dkv, mla_w_ukv=v_mla_w_ukv,
                 mla_w_o=v_mla_w_o, swa_w_qkv=v_swa_w_qkv, swa_w_o=v_swa_w_o, w_ff1=v_w_ff1, w_ff2=v_w_ff2)
    groups = {"first": FIRST_WEIGHTS, "late": LATE_WEIGHTS}
    wrows = {n: -(-big_w[n].size // (PACK_COLS * 16)) * 16 for n in FIRST_WEIGHTS + LATE_WEIGHTS}
    offs = {g: np.concatenate([[0], np.cumsum([wrows[n] for n in names])]).astype(int) for g, names in groups.items()}

    def as_rows(n, a, lead=()):
        flat = a.reshape(lead + (-1,))
        pad = wrows[n] * PACK_COLS - flat.shape[-1]
        if pad:
            flat = jnp.pad(flat, ((0, 0),) * len(lead) + ((0, pad),))
        return flat.reshape(lead + (wrows[n], PACK_COLS))

    def pack(g, d):
        return jnp.concatenate([as_rows(n, d[n]) for n in groups[g]], axis=0)

    def pack_blocks(g, gfull):
        return jnp.concatenate([as_rows(n, _block(n, gfull[n]).astype(BF16), (N_DEV,)) for n in groups[g]], axis=1)

    def unpack(g, packed, lead=()):
        out = {}
        for i, n in enumerate(groups[g]):
            part = packed[..., int(offs[g][i]):int(offs[g][i + 1]), :].reshape(lead + (-1,))
            out[n] = part[..., :big_w[n].size].reshape(lead + big_w[n].shape)
        return out

    gathered = _all_gather("gather_weights", pack("first", big_w), BF16)
    wfull = {n: _unblock(n, b) for n, b in unpack("first", gathered, (N_DEV,)).items()}
    w_dq, w_dkv = wfull["mla_w_dq"][0], wfull["mla_w_dkv"][0]
    w_cat = jnp.concatenate([w_dq, w_dkv, _rot_cols(w_dkv[:, KV_LORA:])], axis=1)
    QL = w_dq.shape[1]
    w_uq = wfull["mla_w_uq"][0].reshape(QL, MLA_HEADS, QK_DIM)
    w_uqx = jnp.concatenate([w_uq, _rot_cols(w_uq[..., QK_NOPE:])], axis=-1).reshape(QL, MLA_HEADS * 256)
    w_ukv = wfull["mla_w_ukv"][0]
    w_o_mla = wfull["mla_w_o"][0]

    L = w_ada.shape[0]
    NC = w_ada.shape[2]
    nbq, nbo = swa_b_qkv.shape[1], swa_b_o.shape[1]
    cpad = -(-(D + nbq + nbo) // 1024) * 1024
    cpack = jnp.pad(jnp.concatenate([c[0], swa_b_qkv[0], swa_b_o[0]]), (0, cpad - (D + nbq + nbo))).reshape(8, cpad // 8)
    call = _all_gather("gather_c", cpack, F32).reshape(N_DEV, cpad)
    c_all = call[:, :D]
    b_qkv_full = call[:, D:D + nbq].reshape(1, N_DEV * nbq)
    b_o_full = call[:, D + nbq:D + nbq + nbo].reshape(1, N_DEV * nbo)
    mod_cols = _ada_fwd(c_all, w_ada)
    mpad = -(-(L * NC) // 1024) * 1024
    mod_send = jnp.pad(jnp.moveaxis(mod_cols, 1, 0).reshape(N_DEV, L * NC), ((0, 0), (0, mpad - L * NC)))
    mod_mine = _all_to_all("exchange_mod", mod_send.reshape(N_DEV, 8, mpad // 8)).reshape(N_DEV, mpad)[:, :L * NC]
    mod = jnp.moveaxis(mod_mine.reshape(N_DEV, L, NC), 0, 1).reshape(L, N_DEV * NC) + b_ada
    mods = mod.reshape(L, 6, 1, D)

    half = QK_ROPE // 2
    inv_freq = ROPE_THETA ** (-jnp.arange(half, dtype=F32) / half)
    ang = positions[0].astype(F32)[:, None] * inv_freq
    cos = jnp.concatenate([jnp.cos(ang), jnp.cos(ang)], axis=-1)
    sin = jnp.concatenate([jnp.sin(ang), jnp.sin(ang)], axis=-1)

    T_ATT = ATT_TILE
    zero_bias = jnp.zeros((1, D), F32)

    sh1, sc1, gt1, sh2, sc2, gt2 = [mods[0, i] for i in range(6)]
    gm0, gp0 = _row(g_mix[0]), _row(g_mlp[0])
    h1, cqp, cq, ckvp, ckv, q, k, v, vt = _mla_in_fwd(x0, cos, sin, gm0, sc1, sh1, w_cat, mla_g_q, w_uqx, mla_g_kv,
                                                      w_ukv, T_ATT)
    o0, lse0, gathered = _mla_attn_fwd(q, k, vt, T_ATT, pack("late", big_w).astype(BF16))
    wfull = {n: _unblock(n, b) for n, b in unpack("late", gathered, (N_DEV,)).items()}
    w_qkv, w_o_swa = wfull["swa_w_qkv"][0], wfull["swa_w_o"][0]
    ff1, ff2 = wfull["w_ff1"], wfull["w_ff2"]
    y1, x1, h2 = _attn_out_fwd(o0, x0, w_o_mla, zero_bias, gt1, gp0, sc2, sh2)
    rl0, act0, y2, x2 = _mlp_fwd(h2, x1, ff1[0], ff2[0], gt2)
    sv0 = dict(y2=y2, rl=rl0, act=act0, h2=h2, x1=x1)

    th1, tc1, tg1, th2, tc2, tg2 = [mods[1, i] for i in range(6)]
    gm1, gp1 = _row(g_mix[1]), _row(g_mlp[1])
    h3, sq, sk, svv = _swa_in_fwd(x2, gm1, tc1, th1, w_qkv, b_qkv_full)
    o1, lse1 = _swa_attn_fwd(sq, sk, svv, swa_sinks)
    y3, x3, h4 = _attn_out_fwd(o1, x2, w_o_swa, b_o_full, tg1, gp1, tc2, th2)
    rl1, act1, y4, x4 = _mlp_fwd(h4, x3, ff1[1], ff2[1], tg2)
    sv1 = dict(y2=y4, rl=rl1, act=act1, h2=h4, x1=x3)
    dx4, loss_part, dg_final = _final_loss(x4, target, _row(g_final))

    dx3, dw1_1, dw2_1, dsh2_1, da2_1, dgt2_1 = _mlp_block_bwd(dx4, sv1, ff1[1], ff2[1], gp1, tc2, tg2)
    dy, do, dl, dgt1_1, db_o = _attn_out_bwd(dx3, y3, o1, tg1, w_o_swa, SWA_HEADS)
    dw_o_swa = _matmul_tn("dw_o", o1, dy)
    dsq, dsk, dsv, dsink = _swa_attn_bwd(sq, sk, svv, do, lse1, dl.T, swa_sinks)
    dqkv, dx2, db_qkv, dsh1_1, da1_1 = _swa_in_bwd(dsq, dsk, dsv, x2, dx3, w_qkv, gm1, tc1)
    dw_qkv = _matmul_tn("dw_qkv", h3, dqkv)

    dx1, dw1_0, dw2_0, dsh2_0, da2_0, dgt2_0 = _mlp_block_bwd(dx2, sv0, ff1[0], ff2[0], gp0, sc2, gt2)
    dy, do, dl, dgt1_0, _ = _attn_out_bwd(dx1, y1, o0, gt1, w_o_mla, MLA_HEADS)
    dw_o_mla = _matmul_tn("dw_o", o0, dy)
    tb = min(T_ATT, S)
    delta = dl.T.reshape(MLA_HEADS, S // tb, 1, tb)
    glate = dict(swa_w_qkv=dw_qkv[None], swa_w_o=dw_o_swa[None], w_ff1=jnp.stack([dw1_0, dw1_1]),
                 w_ff2=jnp.stack([dw2_0, dw2_1]))
    dq, dk, dv, recv = _mla_attn_bwd(q, k, v, do, lse0 * LOG2E, delta, T_ATT, pack_blocks("late", glate))
    late = _sum_adamw(recv, pack("late", big_w), pack("late", big_m), pack("late", big_v))
    dqx, dkv, dcat, dx0, dg_q, dg_kv, dsh1_0, da1_0 = _mla_in_bwd(
        dq, dk, dv, cos, sin, cqp, ckvp, x0, dx1, w_uqx, mla_g_q, w_ukv, mla_g_kv, w_cat, gm0, sc1)
    dw_uqx = _matmul_tn("dw_uq", cq, dqx).reshape(QL, MLA_HEADS, 256)
    dw_ukv = _matmul_tn("dw_ukv", ckv, dkv)
    dw_cat = _matmul_tn("dw_down", h1, dcat)
    dw_uq = jnp.concatenate([dw_uqx[..., :QK_NOPE], dw_uqx[..., 128:192] + _unrot_cols(dw_uqx[..., 192:256])],
                            axis=-1).reshape(QL, MLA_HEADS * QK_DIM)
    o_kr = QL + KV_LORA
    dw_dkv = jnp.concatenate([dw_cat[:, QL:o_kr],
                              dw_cat[:, o_kr:o_kr + QK_ROPE] + _unrot_cols(dw_cat[:, o_kr + QK_ROPE:])], axis=1)

    gfirst = dict(mla_w_dq=dw_cat[None, :, :QL], mla_w_uq=dw_uq[None], mla_w_dkv=dw_dkv[None], mla_w_ukv=dw_ukv[None],
                  mla_w_o=dw_o_mla[None])
    first = _reduce_scatter_adamw("grad_exchange_adamw", pack_blocks("first", gfirst), pack("first", big_w),
                                  pack("first", big_m), pack("first", big_v))
    big_g, big_d, big_nm, big_nv = ({**unpack("first", first[j]), **unpack("late", late[j])} for j in range(4))

    dmod = jnp.stack([
        jnp.concatenate([dsh1_0, gm0 * da1_0, dgt1_0, dsh2_0, gp0 * da2_0, dgt2_0], axis=1),
        jnp.concatenate([dsh1_1, gm1 * da1_1, dgt1_1, dsh2_1, gp1 * da2_1, dgt2_1], axis=1)]).reshape(-1)
    dg_mix = jnp.concatenate([(1.0 + sc1) * da1_0, (1.0 + tc1) * da1_1], axis=1).reshape(-1)
    dg_mlp = jnp.concatenate([(1.0 + sc2) * da2_0, (1.0 + tc2) * da2_1], axis=1).reshape(-1)
    parts = [loss_part.reshape(-1), dmod, dg_mix, dg_mlp, dg_q.reshape(-1), dg_kv.reshape(-1), dsink.reshape(-1),
             dg_final.reshape(-1), db_qkv.reshape(-1), db_o.reshape(-1)]
    soffs = np.concatenate([[0], np.cumsum([p.size for p in parts])])
    spad = -(-int(soffs[-1]) // 1024) * 1024
    spack = jnp.pad(jnp.concatenate(parts), (0, spad - int(soffs[-1]))).reshape(8, spad // 8)
    sall = _all_gather("gather_small_grads", spack, F32)
    ssum = _sum_devices(sall).reshape(-1)
    tot = [ssum[int(soffs[i]):int(soffs[i + 1])] for i in range(len(parts))]
    loss = tot[0][0]
    nsink = swa_sinks.shape[1]
    small_g = dict(b_ada=tot[1].reshape(b_ada.shape), g_mix=tot[2].reshape(g_mix.shape), g_mlp=tot[3].reshape(g_mlp.shape),
                   mla_g_q=tot[4].reshape(mla_g_q.shape), mla_g_kv=tot[5].reshape(mla_g_kv.shape),
                   swa_sinks=tot[6][:nsink].reshape(swa_sinks.shape), g_final=tot[7].reshape(g_final.shape),
                   swa_b_qkv=lax.dynamic_slice(tot[8], (me * nbq,), (nbq,)).reshape(swa_b_qkv.shape),
                   swa_b_o=lax.dynamic_slice(tot[9], (me * nbo,), (nbo,)).reshape(swa_b_o.shape))
    small_w = dict(b_ada=b_ada, g_mix=g_mix, g_mlp=g_mlp, mla_g_q=mla_g_q, mla_g_kv=mla_g_kv, swa_sinks=swa_sinks,
                   g_final=g_final, swa_b_qkv=swa_b_qkv, swa_b_o=swa_b_o)
    small_m = dict(b_ada=m_b_ada, g_mix=m_g_mix, g_mlp=m_g_mlp, mla_g_q=m_mla_g_q, mla_g_kv=m_mla_g_kv,
                   swa_sinks=m_swa_sinks, g_final=m_g_final, swa_b_qkv=m_swa_b_qkv, swa_b_o=m_swa_b_o)
    small_v = dict(b_ada=v_b_ada, g_mix=v_g_mix, g_mlp=v_g_mlp, mla_g_q=v_mla_g_q, mla_g_kv=v_mla_g_kv,
                   swa_sinks=v_swa_sinks, g_final=v_g_final, swa_b_qkv=v_swa_b_qkv, swa_b_o=v_swa_b_o)
    SMALL = list(small_w)
    woffs = np.concatenate([[0], np.cumsum([small_w[n].size for n in SMALL])])
    wpad = -(-int(woffs[-1]) // 1024) * 1024

    def spack_of(d):
        flat = jnp.concatenate([d[n].reshape(-1) for n in SMALL])
        return jnp.pad(flat, (0, wpad - int(woffs[-1]))).reshape(8, wpad // 8)

    sm = _adamw_small(spack_of(small_w), spack_of(small_g), spack_of(small_m), spack_of(small_v))
    small_d, small_nm, small_nv = (
        {n: a.reshape(-1)[int(woffs[i]):int(woffs[i + 1])].reshape(small_w[n].shape) for i, n in enumerate(SMALL)}
        for a in sm)

    b_off = int(soffs[1])
    dmod_all = sall.reshape(N_DEV, -1)[:, b_off:b_off + L * N_DEV * NC].reshape(N_DEV, L, N_DEV * NC)
    dmod_cols = jnp.moveaxis(lax.dynamic_slice_in_dim(dmod_all, me * NC, NC, axis=2), 0, 1)
    ada_g, ada_d, ada_nm, ada_nv = _ada_bwd_adamw(c_all.T, dmod_cols, w_ada, m_w_ada, v_w_ada)

    order = ["w_ada", "b_ada", "g_mix", "g_mlp", "mla_w_dq", "mla_g_q", "mla_w_uq", "mla_w_dkv", "mla_g_kv",
             "mla_w_ukv", "mla_w_o", "swa_w_qkv", "swa_b_qkv", "swa_sinks", "swa_w_o", "swa_b_o", "w_ff1", "w_ff2", "g_final"]

    def collect(ada, big, small):
        return [ada if n == "w_ada" else (big[n] if n in big else small[n]) for n in order]

    return (loss, dx0.reshape(x.shape), *collect(ada_g, big_g, small_g), *collect(ada_d, big_d, small_d),
            *collect(ada_nm, big_nm, small_nm), *collect(ada_nv, big_nv, small_nv))
```

```python
import functools

import jax
import jax.numpy as jnp
import numpy as np
from jax import lax
from jax.experimental import pallas as pl
from jax.experimental.pallas import tpu as pltpu

F32 = jnp.float32
BF16 = jnp.bfloat16
MESH_IDS = pl.DeviceIdType.MESH
N_DEV = 8

MLA_HEADS = 8
QK_NOPE = 128
QK_ROPE = 64
QK_DIM = QK_NOPE + QK_ROPE
V_DIM = 128
KV_LORA = 256
ROPE_THETA = 10000.0
SWA_HEADS = 16
SWA_KV_HEADS = 4
SWA_GROUP = SWA_HEADS // SWA_KV_HEADS
SWA_HEAD_DIM = 64
WINDOW = 128
EPS = 1e-6
LOG2E = 1.4426950408889634

ADAM_LR = 0.001
ADAM_B1 = 0.9
ADAM_B2 = 0.999
ADAM_EPS = 1e-08
ADAM_WD = 0.01
ADAM_STEP = 10

PACK_COLS = 1024
RS_CHUNKS = 4
VMEM_LIMIT = 56 << 20
ROW_TILE = 512
ROW_TILE_WIDE = 256
ATT_TILE = 512
TN_TOKENS = 2048


def _dot(a, b):
    return jnp.dot(a, b, preferred_element_type=F32)


def _dot_nt(a, b):
    return lax.dot_general(a, b, (((1,), (1,)), ((), ())), preferred_element_type=F32)


def _dot_tn(a, b):
    return lax.dot_general(a, b, (((0,), (0,)), ((), ())), preferred_element_type=F32)


def _rstd(x):
    return lax.rsqrt(jnp.mean(x * x, axis=-1, keepdims=True) + EPS)


def _rms_bwd(dn, n, r):
    return r * (dn - n * jnp.mean(dn * n, axis=-1, keepdims=True))


def _modulate(x, g, sc, sh):
    r = _rstd(x)
    return ((x * r) * g) * (1.0 + sc) + sh


def _modulate_bwd(dh, x, g, sc):
    r = _rstd(x)
    n = x * r
    dsh = jnp.sum(dh, axis=0, keepdims=True)
    da = jnp.sum(dh * n, axis=0, keepdims=True)
    dx = _rms_bwd(dh * (g * (1.0 + sc)), n, r)
    return dx, dsh, da


def _first(i):
    return i == 0


def _acc(ref, val, i):
    @pl.when(i == 0)
    def _():
        ref[...] = val

    @pl.when(i != 0)
    def _():
        ref[...] += val


def _row_spec(shape, tm):
    nd = len(shape)
    return pl.BlockSpec(tuple(shape[:nd - 2]) + (tm, shape[-1]), lambda i: (0,) * (nd - 2) + (i, 0))


def _resident_spec(shape, single_buffer):
    nd = len(shape)
    if single_buffer:
        return pl.BlockSpec(tuple(shape), lambda i: (0,) * nd, pipeline_mode=pl.Buffered(1))
    return pl.BlockSpec(tuple(shape), lambda i: (0,) * nd)


def _rowcall(name, body, tokens, tm, row_in, full_in, row_out, acc_out=()):
    tm = min(tm, tokens)
    in_specs = [_row_spec(a.shape, tm) for a in row_in] + [_resident_spec(a.shape, True) for a in full_in]
    row_specs = [s[1] if isinstance(s, tuple) else _row_spec(s.shape, tm) for s in row_out]
    row_out = [s[0] if isinstance(s, tuple) else s for s in row_out]
    out_specs = row_specs + [_resident_spec(s.shape, False) for s in acc_out]
    return pl.pallas_call(
        body, name=name, grid=(tokens // tm,), in_specs=in_specs, out_specs=out_specs,
        out_shape=list(row_out) + list(acc_out),
        compiler_params=pltpu.CompilerParams(dimension_semantics=("arbitrary",), vmem_limit_bytes=VMEM_LIMIT),
    )(*row_in, *full_in)


def _sds(shape, dtype):
    return jax.ShapeDtypeStruct(tuple(shape), dtype)


def _mla_in_fwd(x, cos, sin, g, sc, sh, w_cat, g_q, w_uqx, g_kv, w_ukv, t):
    S, D = x.shape
    QL = g_q.shape[1]
    H = MLA_HEADS
    t = min(t, S)

    def body(x_ref, cos_ref, sin_ref, g_ref, sc_ref, sh_ref, wcat_ref, gq_ref, wuqx_ref, gkv_ref, wukv_ref,
             h_ref, cqp_ref, cq_ref, ckvp_ref, ckv_ref, q_ref, k_ref, v_ref, vt_ref):
        cs, sn = cos_ref[...], sin_ref[...]
        hb = _modulate(x_ref[...], g_ref[...], sc_ref[...], sh_ref[...]).astype(BF16)
        h_ref[...] = hb
        low = _dot(hb, wcat_ref[...])
        cqp = low[:, :QL]
        cqp_ref[...] = cqp
        cq = ((cqp * _rstd(cqp)) * gq_ref[...]).astype(BF16)
        cq_ref[...] = cq
        ckvp = low[:, QL:QL + KV_LORA]
        ckvp_ref[...] = ckvp
        ckv = ((ckvp * _rstd(ckvp)) * gkv_ref[...]).astype(BF16)
        ckv_ref[...] = ckv
        o = QL + KV_LORA
        kr = (low[:, o:o + QK_ROPE] * cs + low[:, o + QK_ROPE:o + 2 * QK_ROPE] * sn).astype(BF16)
        qx = _dot(cq, wuqx_ref[...])
        kv = _dot(ckv, wukv_ref[...])
        for hd in range(H):
            b = hd * 256
            q_ref[hd, :, 0:QK_NOPE] = qx[:, b:b + QK_NOPE].astype(BF16)
            q_ref[hd, :, QK_NOPE:QK_DIM] = (qx[:, b + 128:b + 192] * cs + qx[:, b + 192:b + 256] * sn).astype(BF16)
            k_ref[hd, :, 0:QK_NOPE] = kv[:, b:b + QK_NOPE].astype(BF16)
            k_ref[hd, :, QK_NOPE:QK_DIM] = kr
            vh = kv[:, b + 128:b + 256]
            v_ref[hd] = vh.astype(BF16)
            vt_ref[hd, 0, 0:V_DIM, :] = vh.T.astype(BF16)
            vt_ref[hd, 0, V_DIM:2 * V_DIM, :] = jnp.ones((V_DIM, x_ref.shape[0]), BF16)

    vt_spec = pl.BlockSpec((H, 1, 2 * V_DIM, t), lambda i: (0, i, 0, 0))
    return _rowcall(
        "mla_in_fwd", body, S, t, [x, cos, sin], [g, sc, sh, w_cat, g_q, w_uqx, g_kv, w_ukv],
        [_sds((S, D), BF16), _sds((S, QL), F32), _sds((S, QL), BF16), _sds((S, KV_LORA), F32), _sds((S, KV_LORA), BF16),
         _sds((H, S, QK_DIM), BF16), _sds((H, S, QK_DIM), BF16), _sds((H, S, V_DIM), BF16),
         (_sds((H, S // t, 2 * V_DIM, t), BF16), vt_spec)])


def _mla_attn_fwd(q, k, vt, t, sends):
    H, S, DQ = q.shape
    DV = V_DIM
    t = min(t, S)
    nb = S // t
    scale = QK_DIM ** -0.5
    c2 = scale * LOG2E

    ns = len(sends)

    def body(q_ref, k_ref, vt_ref, *rest):
        send_refs, (o_ref, lse_ref), gath_refs = rest[:ns], rest[ns:ns + 2], rest[ns + 2:2 * ns + 2]
        m_s, acc_s, s_buf, send_sems, recv_sems, local_sems = rest[2 * ns + 2:]
        hd, qi = pl.program_id(0), pl.program_id(1)

        def gather():
            return _exchange_copies([lambda j, r=r: r for r in send_refs], gath_refs, send_sems, recv_sems, local_sems)

        @pl.when((hd == 0) & (qi == 0))
        def _():
            _start_exchange(gather())

        m_s[...] = jnp.full_like(m_s, -jnp.inf)
        acc_s[...] = jnp.zeros_like(acc_s)

        def scores(j, slot):
            rows = pl.ds(pl.multiple_of(j * t, t), t)
            s_buf[slot] = _dot_nt(k_ref[0, rows, :], q_ref[0])

        def absorb(j, slot, diagonal):
            s = s_buf[slot]
            if diagonal:
                key = lax.broadcasted_iota(jnp.int32, (t, t), 0)
                qry = lax.broadcasted_iota(jnp.int32, (t, t), 1)
                s = jnp.where(key <= qry, s, -jnp.inf)
            m_prev = m_s[...]
            m_new = jnp.maximum(m_prev, jnp.max(s, axis=0, keepdims=True))
            alpha = jnp.exp2((m_prev - m_new) * c2)
            p = jnp.exp2((s - m_new) * c2)
            acc_s[...] = alpha * acc_s[...] + _dot(vt_ref[0, j], p.astype(BF16))
            m_s[...] = m_new

        def pair(i, carry):
            j = 2 * i
            scores(j + 1, 1)
            absorb(j, 0, False)
            scores(j + 2, 0)
            absorb(j + 1, 1, False)
            return carry

        scores(0, 0)
        lax.fori_loop(0, qi // 2, pair, 0)

        @pl.when(qi % 2 == 0)
        def _():
            absorb(qi, 0, True)

        @pl.when(qi % 2 == 1)
        def _():
            scores(qi, 1)
            absorb(qi - 1, 0, False)
            absorb(qi, 1, True)

        acc = acc_s[...]
        o_ref[...] = (acc[:DV] / acc[DV:]).T.astype(BF16)
        lse_ref[0, 0] = m_s[...] * scale + jnp.log(acc[DV:DV + 1])

        @pl.when((hd == H - 1) & (qi == nb - 1))
        def _():
            _finish_exchange(gather())

    outs = pl.pallas_call(
        body, name="mla_attn_fwd", grid=(H, nb),
        in_specs=[pl.BlockSpec((1, t, DQ), lambda h, i: (h, i, 0)),
                  pl.BlockSpec((1, S, DQ), lambda h, i: (h, 0, 0)),
                  pl.BlockSpec((1, nb, 2 * DV, t), lambda h, i: (h, 0, 0, 0))] + [ANY_SPEC] * ns,
        out_specs=[pl.BlockSpec((t, DV), lambda h, i: (i, h)),
                   pl.BlockSpec((1, 1, 1, t), lambda h, i: (h, i, 0, 0))] + [ANY_SPEC] * ns,
        out_shape=[_sds((S, H * DV), BF16), _sds((H, nb, 1, t), F32)]
        + [_sds((N_DEV,) + a.shape, a.dtype) for a in sends],
        scratch_shapes=[pltpu.VMEM((1, t), F32), pltpu.VMEM((2 * DV, t), F32), pltpu.VMEM((2, t, t), F32)]
        + _comm_sems(ns),
        compiler_params=pltpu.CompilerParams(dimension_semantics=("arbitrary", "arbitrary"),
                                             vmem_limit_bytes=VMEM_LIMIT),
    )(q, k, vt, *sends)
    return outs[0], outs[1], outs[2:]


def _attn_out_fwd(o, x, w_o, b_o, gt, g, sc, sh):
    S, D = x.shape

    def body(o_ref, x_ref, wo_ref, bo_ref, gt_ref, g_ref, sc_ref, sh_ref, y_ref, x1_ref, h_ref):
        y = _dot(o_ref[...], wo_ref[...]) + bo_ref[...]
        y_ref[...] = y
        x1 = x_ref[...] + gt_ref[...] * y
        x1_ref[...] = x1
        h_ref[...] = _modulate(x1, g_ref[...], sc_ref[...], sh_ref[...]).astype(BF16)

    return _rowcall("attn_out_fwd", body, S, ROW_TILE, [o, x], [w_o, b_o, gt, g, sc, sh],
                    [_sds((S, D), F32), _sds((S, D), F32), _sds((S, D), BF16)])


def _mlp_fwd(h, x, w1, w2, gt):
    S, D = x.shape
    FF = w1.shape[1]

    def body(h_ref, x_ref, w1_ref, w2_ref, gt_ref, rl_ref, act_ref, y_ref, x2_ref):
        rl = jnp.maximum(_dot(h_ref[...], w1_ref[...]), 0.0)
        rl_ref[...] = rl.astype(BF16)
        act = (rl * rl).astype(BF16)
        act_ref[...] = act
        y = _dot(act, w2_ref[...])
        y_ref[...] = y
        x2_ref[...] = x_ref[...] + gt_ref[...] * y

    return _rowcall("mlp_fwd", body, S, ROW_TILE_WIDE, [h, x], [w1, w2, gt],
                    [_sds((S, FF), BF16), _sds((S, FF), BF16), _sds((S, D), F32), _sds((S, D), F32)])


def _swa_in_fwd(x, g, sc, sh, w_qkv, b_qkv):
    S, D = x.shape
    NQ = SWA_HEADS * SWA_HEAD_DIM
    NK = SWA_KV_HEADS * SWA_HEAD_DIM

    def body(x_ref, g_ref, sc_ref, sh_ref, w_ref, b_ref, h_ref, q_ref, k_ref, v_ref):
        hb = _modulate(x_ref[...], g_ref[...], sc_ref[...], sh_ref[...]).astype(BF16)
        h_ref[...] = hb
        qkv = _dot(hb, w_ref[...]) + b_ref[...]
        q_ref[...] = qkv[:, :NQ].astype(BF16)
        k_ref[...] = qkv[:, NQ:NQ + NK].astype(BF16)
        v_ref[...] = qkv[:, NQ + NK:].astype(BF16)

    return _rowcall("swa_in_fwd", body, S, ROW_TILE, [x], [g, sc, sh, w_qkv, b_qkv],
                    [_sds((S, D), BF16), _sds((S, NQ), BF16), _sds((S, NK), BF16), _sds((S, NK), BF16)])


def _alibi_slope(head):
    return float(np.float32(2.0 ** (-8.0 * (head + 1) / SWA_HEADS)))


def _swa_geometry(n):
    W = WINDOW
    key = lax.broadcasted_iota(jnp.int32, (2 * W, W), 0)
    qry = lax.broadcasted_iota(jnp.int32, (2 * W, W), 1)
    dist = W + qry - key
    valid = (dist >= 0) & (dist < W) & ((n > 0) | (key >= W))
    return dist.astype(F32), valid


def _swa_band_specs(W, nb, cols):
    prev = pl.BlockSpec((W, cols), lambda n: (jnp.maximum(jnp.minimum(n, nb - 1) - 1, 0), 0))
    cur = pl.BlockSpec((W, cols), lambda n: (jnp.minimum(n, nb - 1), 0))
    return prev, cur


def _swa_attn_fwd(q, k, v, sinks):
    S, NQ = q.shape
    NK = k.shape[1]
    W, Dh, G = WINDOW, SWA_HEAD_DIM, SWA_GROUP
    nb = S // W

    def body(q_ref, kp_ref, kc_ref, vp_ref, vc_ref, sink_ref, o_ref, lse_ref):
        distf, valid = _swa_geometry(pl.program_id(0))
        kband = jnp.concatenate([kp_ref[...], kc_ref[...]], axis=0)
        vband_t = jnp.concatenate([vp_ref[...], vc_ref[...]], axis=0).astype(F32).T.astype(BF16)
        outs = []
        for kh in range(SWA_KV_HEADS):
            kb = kband[:, kh * Dh:(kh + 1) * Dh]
            vbt = vband_t[kh * Dh:(kh + 1) * Dh, :]
            for gi in range(G):
                hq = kh * G + gi
                s = _dot_nt(kb, q_ref[:, hq * Dh:(hq + 1) * Dh]) * (Dh ** -0.5) - _alibi_slope(hq) * distf
                s = jnp.where(valid, s, -jnp.inf)
                sink = sink_ref[:, hq:hq + 1]
                m = jnp.maximum(jnp.max(s, axis=0, keepdims=True), sink)
                p = jnp.exp(s - m)
                denom = jnp.sum(p, axis=0, keepdims=True) + jnp.exp(sink - m)
                outs.append(_dot(vbt, (p * (1.0 / denom)).astype(BF16)))
                lse_ref[hq:hq + 1, :] = m + jnp.log(denom)
        o_ref[...] = jnp.concatenate(outs, axis=0).T.astype(BF16)

    kprev, kcur = _swa_band_specs(W, nb, NK)
    return pl.pallas_call(
        body, name="swa_attn_fwd", grid=(nb,),
        in_specs=[pl.BlockSpec((W, NQ), lambda n: (n, 0)), kprev, kcur, kprev, kcur,
                  pl.BlockSpec((1, SWA_HEADS), lambda n: (0, 0))],
        out_specs=[pl.BlockSpec((W, NQ), lambda n: (n, 0)), pl.BlockSpec((SWA_HEADS, W), lambda n: (0, n))],
        out_shape=[_sds((S, NQ), BF16), _sds((SWA_HEADS, S), F32)],
        compiler_params=pltpu.CompilerParams(dimension_semantics=("arbitrary",), vmem_limit_bytes=VMEM_LIMIT),
    )(q, k, k, v, v, sinks)


def _final_loss(x, target, g):
    S, D = x.shape

    def body(x_ref, t_ref, g_ref, dx_ref, loss_ref, dg_ref):
        i = pl.program_id(0)
        xv = x_ref[...]
        r = _rstd(xv)
        n = xv * r
        err = n * g_ref[...] - t_ref[...]
        part = 0.5 * jnp.sum(jnp.mean(err * err, axis=-1, keepdims=True), axis=0, keepdims=True)
        _acc(loss_ref, jnp.broadcast_to(part, loss_ref.shape), i)
        dout = err / D
        _acc(dg_ref, jnp.sum(dout * n, axis=0, keepdims=True), i)
        dx_ref[...] = _rms_bwd(dout * g_ref[...], n, r)

    return _rowcall("final_loss", body, S, ROW_TILE, [x, target], [g], [_sds((S, D), F32)],
                    [_sds((1, 128), F32), _sds((1, D), F32)])


def _mlp_bwd_a(dx, y, rl, gt, w2):
    S, D = dx.shape
    FF = rl.shape[1]

    def body(dx_ref, y_ref, rl_ref, gt_ref, w2_ref, dy_ref, du_ref, dgt_ref):
        i = pl.program_id(0)
        dxv = dx_ref[...]
        _acc(dgt_ref, jnp.sum(dxv * y_ref[...], axis=0, keepdims=True), i)
        dy = (dxv * gt_ref[...]).astype(BF16)
        dy_ref[...] = dy
        dact = _dot_nt(dy, w2_ref[...])
        du_ref[...] = (dact * (2.0 * rl_ref[...].astype(F32))).astype(BF16)

    return _rowcall("mlp_bwd_a", body, S, ROW_TILE_WIDE, [dx, y, rl], [gt, w2],
                    [_sds((S, D), BF16), _sds((S, FF), BF16)], [_sds((1, D), F32)])


def _mlp_bwd_b(du, x, dx, w1, g, sc):
    S, D = x.shape

    def body(du_ref, x_ref, dx_ref, w1_ref, g_ref, sc_ref, dxo_ref, dsh_ref, da_ref):
        i = pl.program_id(0)
        dh = _dot_nt(du_ref[...], w1_ref[...])
        dxn, dsh, da = _modulate_bwd(dh, x_ref[...], g_ref[...], sc_ref[...])
        dxo_ref[...] = dx_ref[...] + dxn
        _acc(dsh_ref, dsh, i)
        _acc(da_ref, da, i)

    return _rowcall("mlp_bwd_b", body, S, ROW_TILE_WIDE, [du, x, dx], [w1, g, sc],
                    [_sds((S, D), F32)], [_sds((1, D), F32), _sds((1, D), F32)])


def _attn_out_bwd(dx, y, o, gt, w_o, n_heads):
    S, D = dx.shape
    NO = o.shape[1]
    dh = NO // n_heads

    def body(dx_ref, y_ref, o_ref, gt_ref, wo_ref, dy_ref, do_ref, dl_ref, dgt_ref, dbo_ref):
        i = pl.program_id(0)
        dxv = dx_ref[...]
        _acc(dgt_ref, jnp.sum(dxv * y_ref[...], axis=0, keepdims=True), i)
        dy = dxv * gt_ref[...]
        _acc(dbo_ref, jnp.sum(dy, axis=0, keepdims=True), i)
        dyb = dy.astype(BF16)
        dy_ref[...] = dyb
        do = _dot_nt(dyb, wo_ref[...])
        do_ref[...] = do.astype(BF16)
        prod = do * o_ref[...].astype(F32)
        for hd in range(n_heads):
            dl_ref[:, hd:hd + 1] = jnp.sum(prod[:, hd * dh:(hd + 1) * dh], axis=-1, keepdims=True)

    return _rowcall("attn_out_bwd", body, S, ROW_TILE, [dx, y, o], [gt, w_o],
                    [_sds((S, D), BF16), _sds((S, NO), BF16), _sds((S, n_heads), F32)],
                    [_sds((1, D), F32), _sds((1, D), F32)])


def _mla_attn_bwd(q, k, v, do, lse, delta, t, gblks):
    H, S, DQ = q.shape
    DV = V_DIM
    t = min(t, S)
    nb = S // t
    scale = QK_DIM ** -0.5
    c2 = scale * LOG2E

    ng = len(gblks)

    def body(q_ref, k_ref, v_ref, do_ref, lse_ref, dl_ref, *rest):
        g_refs, (dq_ref, dk_ref, dv_ref), recv_refs = rest[:ng], rest[ng:ng + 3], rest[ng + 3:2 * ng + 3]
        dk_s, dv_s, s_buf, dp_buf, send_sems, recv_sems, local_sems = rest[2 * ng + 3:]
        hd, kj = pl.program_id(0), pl.program_id(1)

        def scatter():
            return _exchange_copies([lambda j, r=r: r.at[j] for r in g_refs], recv_refs, send_sems, recv_sems,
                                    local_sems)

        @pl.when((hd == 0) & (kj == 0))
        def _():
            _start_exchange(scatter())

        @pl.when(kj == 0)
        def _():
            dq_ref[...] = jnp.zeros_like(dq_ref)

        dk_s[...] = jnp.zeros_like(dk_s)
        dv_s[...] = jnp.zeros_like(dv_s)

        def products(i, slot):
            rows = pl.ds(pl.multiple_of(i * t, t), t)
            s_buf[slot] = _dot_nt(k_ref[0], q_ref[0, rows, :])
            dp_buf[slot] = _dot_nt(v_ref[0], do_ref[rows, :])

        def absorb(i, slot, diagonal):
            rows = pl.ds(pl.multiple_of(i * t, t), t)
            qb, dob = q_ref[0, rows, :], do_ref[rows, :]
            p = jnp.exp2(s_buf[slot] * c2 - lse_ref[0, i])
            if diagonal:
                key = lax.broadcasted_iota(jnp.int32, (t, t), 0)
                qry = lax.broadcasted_iota(jnp.int32, (t, t), 1)
                p = jnp.where(key <= qry, p, 0.0)
            dv_s[...] += _dot(p.astype(BF16), dob)
            ds = (p * (dp_buf[slot] - dl_ref[0, i])).astype(BF16)
            dk_s[...] += _dot(ds, qb)
            dq_ref[0, rows, :] += _dot_tn(ds, k_ref[0])

        n_off = nb - 1 - kj
        first = kj + 1

        def pair(i, carry):
            u = 2 * i
            products(first + u + 1, 1)
            absorb(first + u, 0, False)
            products(jnp.where(u + 2 < n_off, first + u + 2, kj), 0)
            absorb(first + u + 1, 1, False)
            return carry

        products(jnp.where(n_off > 0, first, kj), 0)
        lax.fori_loop(0, n_off // 2, pair, 0)

        @pl.when(n_off % 2 == 0)
        def _():
            absorb(kj, 0, True)

        @pl.when(n_off % 2 == 1)
        def _():
            products(kj, 1)
            absorb(nb - 1, 0, False)
            absorb(kj, 1, True)

        dk_ref[0] = (dk_s[...] * scale).astype(BF16)
        dv_ref[0] = dv_s[...].astype(BF16)

        @pl.when((hd == H - 1) & (kj == nb - 1))
        def _():
            _finish_exchange(scatter())

    rowspec = pl.BlockSpec((1, nb, 1, t), lambda h, j: (h, 0, 0, 0))
    outs = pl.pallas_call(
        body, name="mla_attn_bwd", grid=(H, nb),
        in_specs=[pl.BlockSpec((1, S, DQ), lambda h, j: (h, 0, 0)),
                  pl.BlockSpec((1, t, DQ), lambda h, j: (h, j, 0)),
                  pl.BlockSpec((1, t, DV), lambda h, j: (h, j, 0)),
                  pl.BlockSpec((S, DV), lambda h, j: (0, h)), rowspec, rowspec] + [ANY_SPEC] * ng,
        out_specs=[pl.BlockSpec((1, S, DQ), lambda h, j: (h, 0, 0)),
                   pl.BlockSpec((1, t, DQ), lambda h, j: (h, j, 0)),
                   pl.BlockSpec((1, t, DV), lambda h, j: (h, j, 0))] + [ANY_SPEC] * ng,
        out_shape=[_sds((H, S, DQ), F32), _sds((H, S, DQ), BF16), _sds((H, S, DV), BF16)]
        + [_sds(g.shape, g.dtype) for g in gblks],
        scratch_shapes=[pltpu.VMEM((t, DQ), F32), pltpu.VMEM((t, DV), F32), pltpu.VMEM((2, t, t), F32),
                        pltpu.VMEM((2, t, t), F32)] + _comm_sems(ng),
        compiler_params=pltpu.CompilerParams(dimension_semantics=("arbitrary", "arbitrary"),
                                             vmem_limit_bytes=VMEM_LIMIT),
    )(q, k, v, do, lse, delta, *gblks)
    return outs[0], outs[1], outs[2], outs[3:]


def _swa_attn_bwd(q, k, v, do, lse, delta, sinks):
    S, NQ = q.shape
    NK = k.shape[1]
    W, Dh, G = WINDOW, SWA_HEAD_DIM, SWA_GROUP
    nb = S // W

    def body(q_ref, kp_ref, kc_ref, vp_ref, vc_ref, do_ref, lse_ref, dl_ref, sink_ref,
             dq_ref, dk_ref, dv_ref, dsink_ref, dkc_s, dvc_s):
        n = pl.program_id(0)

        @pl.when(n == 0)
        def _():
            dkc_s[...] = jnp.zeros_like(dkc_s)
            dvc_s[...] = jnp.zeros_like(dvc_s)
            dsink_ref[...] = jnp.zeros_like(dsink_ref)

        @pl.when(n < nb)
        def _():
            distf, valid = _swa_geometry(n)
            kband = jnp.concatenate([kp_ref[...], kc_ref[...]], axis=0)
            vband = jnp.concatenate([vp_ref[...], vc_ref[...]], axis=0)
            kband_t = kband.astype(F32).T.astype(BF16)
            dq_t = []
            for kh in range(SWA_KV_HEADS):
                ck = slice(kh * Dh, (kh + 1) * Dh)
                kb, vb, kbt = kband[:, ck], vband[:, ck], kband_t[ck, :]
                dkb = jnp.zeros((2 * W, Dh), F32)
                dvb = jnp.zeros((2 * W, Dh), F32)
                for gi in range(G):
                    hq = kh * G + gi
                    cq = slice(hq * Dh, (hq + 1) * Dh)
                    qh, doh = q_ref[:, cq], do_ref[:, cq]
                    lse_h = lse_ref[hq:hq + 1, :]
                    dl_h = dl_ref[hq:hq + 1, :]
                    s = _dot_nt(kb, qh) * (Dh ** -0.5) - _alibi_slope(hq) * distf
                    p = jnp.where(valid, jnp.exp(s - lse_h), 0.0)
                    dvb = dvb + _dot(p.astype(BF16), doh)
                    dp = _dot_nt(vb, doh)
                    dsb = ((p * (dp - dl_h)) * (Dh ** -0.5)).astype(BF16)
                    dq_t.append(_dot(kbt, dsb))
                    dkb = dkb + _dot(dsb, qh)
                    psink = jnp.exp(sink_ref[:, hq:hq + 1] - lse_h)
                    dsink_ref[:, hq:hq + 1] += -jnp.sum(psink * dl_h, axis=1, keepdims=True)
                dk_ref[:, ck] = (dkc_s[:, ck] + dkb[:W]).astype(BF16)
                dv_ref[:, ck] = (dvc_s[:, ck] + dvb[:W]).astype(BF16)
                dkc_s[:, ck] = dkb[W:]
                dvc_s[:, ck] = dvb[W:]
            dq_ref[...] = jnp.concatenate(dq_t, axis=0).T.astype(BF16)

        @pl.when(n == nb)
        def _():
            dk_ref[...] = dkc_s[...].astype(BF16)
            dv_ref[...] = dvc_s[...].astype(BF16)

    kprev, kcur = _swa_band_specs(W, nb, NK)
    qspec = lambda cols: pl.BlockSpec((W, cols), lambda n: (jnp.minimum(n, nb - 1), 0))
    kvout = pl.BlockSpec((W, NK), lambda n: (jnp.maximum(n - 1, 0), 0))
    rowspec = pl.BlockSpec((SWA_HEADS, W), lambda n: (0, jnp.minimum(n, nb - 1)))
    return pl.pallas_call(
        body, name="swa_attn_bwd", grid=(nb + 1,),
        in_specs=[qspec(NQ), kprev, kcur, kprev, kcur, qspec(NQ), rowspec, rowspec,
                  pl.BlockSpec((1, SWA_HEADS), lambda n: (0, 0))],
        out_specs=[qspec(NQ), kvout, kvout, pl.BlockSpec((1, 128), lambda n: (0, 0))],
        out_shape=[_sds((S, NQ), BF16), _sds((S, NK), BF16), _sds((S, NK), BF16), _sds((1, 128), F32)],
        scratch_shapes=[pltpu.VMEM((W, NK), F32), pltpu.VMEM((W, NK), F32)],
        compiler_params=pltpu.CompilerParams(dimension_semantics=("arbitrary",), vmem_limit_bytes=VMEM_LIMIT),
    )(q, k, k, v, v, do, lse, delta, sinks)


def _swa_in_bwd(dq, dk, dv, x, dx, w_qkv, g, sc):
    S, D = x.shape
    N = w_qkv.shape[1]

    def body(dq_ref, dk_ref, dv_ref, x_ref, dx_ref, w_ref, g_ref, sc_ref, dqkv_ref, dxo_ref, db_ref, dsh_ref, da_ref):
        i = pl.program_id(0)
        dqkv = jnp.concatenate([dq_ref[...], dk_ref[...], dv_ref[...]], axis=1)
        dqkv_ref[...] = dqkv
        _acc(db_ref, jnp.sum(dqkv.astype(F32), axis=0, keepdims=True), i)
        dh = _dot_nt(dqkv, w_ref[...])
        dxn, dsh, da = _modulate_bwd(dh, x_ref[...], g_ref[...], sc_ref[...])
        dxo_ref[...] = dx_ref[...] + dxn
        _acc(dsh_ref, dsh, i)
        _acc(da_ref, da, i)

    return _rowcall("swa_in_bwd", body, S, ROW_TILE, [dq, dk, dv, x, dx], [w_qkv, g, sc],
                    [_sds((S, N), BF16), _sds((S, D), F32)],
                    [_sds((1, N), F32), _sds((1, D), F32), _sds((1, D), F32)])


def _mla_in_bwd(dq, dk, dv, cos, sin, cqp, ckvp, x, dx, w_uqx, g_q, w_ukv, g_kv, w_cat, g, sc):
    S, D = x.shape
    H = MLA_HEADS
    QL = g_q.shape[1]
    NX = w_uqx.shape[1]
    NC = w_cat.shape[1]

    def body(dq_ref, dk_ref, dv_ref, cos_ref, sin_ref, cqp_ref, ckvp_ref, x_ref, dx_ref,
             wuqx_ref, gq_ref, wukv_ref, gkv_ref, wcat_ref, g_ref, sc_ref,
             dqx_ref, dkv_ref, dcat_ref, dxo_ref, dgq_ref, dgkv_ref, dsh_ref, da_ref):
        i = pl.program_id(0)
        cs, sn = cos_ref[...], sin_ref[...]
        dkr = jnp.zeros(cs.shape, F32)
        for hd in range(H):
            b = hd * 256
            dqh = dq_ref[hd] * (QK_DIM ** -0.5)
            dqx_ref[:, b:b + QK_NOPE] = dqh[:, :QK_NOPE].astype(BF16)
            dqx_ref[:, b + 128:b + 192] = (dqh[:, QK_NOPE:] * cs).astype(BF16)
            dqx_ref[:, b + 192:b + 256] = (dqh[:, QK_NOPE:] * sn).astype(BF16)
            dkh = dk_ref[hd]
            dkv_ref[:, b:b + QK_NOPE] = dkh[:, :QK_NOPE]
            dkv_ref[:, b + 128:b + 256] = dv_ref[hd]
            dkr = dkr + dkh[:, QK_NOPE:].astype(F32)
        dcq = _dot_nt(dqx_ref[...], wuqx_ref[...])
        cqp = cqp_ref[...]
        rq = _rstd(cqp)
        nq = cqp * rq
        _acc(dgq_ref, jnp.sum(dcq * nq, axis=0, keepdims=True), i)
        dcqp = _rms_bwd(dcq * gq_ref[...], nq, rq)
        dckv = _dot_nt(dkv_ref[...], wukv_ref[...])
        ckvp = ckvp_ref[...]
        rk = _rstd(ckvp)
        nk = ckvp * rk
        _acc(dgkv_ref, jnp.sum(dckv * nk, axis=0, keepdims=True), i)
        dckvp = _rms_bwd(dckv * gkv_ref[...], nk, rk)
        dcat_ref[:, :QL] = dcqp.astype(BF16)
        dcat_ref[:, QL:QL + KV_LORA] = dckvp.astype(BF16)
        o = QL + KV_LORA
        dcat_ref[:, o:o + QK_ROPE] = (dkr * cs).astype(BF16)
        dcat_ref[:, o + QK_ROPE:o + 2 * QK_ROPE] = (dkr * sn).astype(BF16)
        dh = _dot_nt(dcat_ref[...], wcat_ref[...])
        dxn, dsh, da = _modulate_bwd(dh, x_ref[...], g_ref[...], sc_ref[...])
        dxo_ref[...] = dx_ref[...] + dxn
        _acc(dsh_ref, dsh, i)
        _acc(da_ref, da, i)

    return _rowcall("mla_in_bwd", body, S, ROW_TILE_WIDE, [dq, dk, dv, cos, sin, cqp, ckvp, x, dx],
                    [w_uqx, g_q, w_ukv, g_kv, w_cat, g, sc],
                    [_sds((S, NX), BF16), _sds((S, NX), BF16), _sds((S, NC), BF16), _sds((S, D), F32)],
                    [_sds((1, QL), F32), _sds((1, KV_LORA), F32), _sds((1, D), F32), _sds((1, D), F32)])


def _matmul_tn(name, a, b):
    S, K = a.shape
    N = b.shape[1]
    tk, tn, ts = min(K, 1024), min(N, 1024), min(S, TN_TOKENS)
    if N % tn:
        tn = 512 if N % 512 == 0 else (384 if N % 384 == 0 else 128)
    if K % tk:
        tk = 512 if K % 512 == 0 else (384 if K % 384 == 0 else 128)
    ns = S // ts

    def body(a_ref, b_ref, o_ref):
        _acc(o_ref, _dot_tn(a_ref[...], b_ref[...]), pl.program_id(2))

    return pl.pallas_call(
        body, name=name, grid=(K // tk, N // tn, ns),
        in_specs=[pl.BlockSpec((ts, tk), lambda i, j, s: (s, i)), pl.BlockSpec((ts, tn), lambda i, j, s: (s, j))],
        out_specs=pl.BlockSpec((tk, tn), lambda i, j, s: (i, j)),
        out_shape=_sds((K, N), F32),
        compiler_params=pltpu.CompilerParams(dimension_semantics=("parallel", "parallel", "arbitrary"),
                                             vmem_limit_bytes=VMEM_LIMIT),
    )(a, b)


def _silu(c):
    return c * jax.nn.sigmoid(c)


def _ada_fwd(c_all, w_ada):
    L, D, NC = w_ada.shape

    def body(c_ref, w_ref, o_ref):
        cond = _silu(c_ref[...]).astype(BF16)
        o_ref[0] = _dot(cond, w_ref[0].astype(BF16))

    return pl.pallas_call(
        body, name="ada_fwd", grid=(L,),
        in_specs=[pl.BlockSpec(c_all.shape, lambda l: (0, 0)), pl.BlockSpec((1, D, NC), lambda l: (l, 0, 0))],
        out_specs=pl.BlockSpec((1, N_DEV, NC), lambda l: (l, 0, 0)),
        out_shape=_sds((L, N_DEV, NC), F32),
        compiler_params=pltpu.CompilerParams(dimension_semantics=("arbitrary",), vmem_limit_bytes=VMEM_LIMIT),
    )(c_all, w_ada)


def _adamw(w, g, m, v):
    m = ADAM_B1 * m + (1.0 - ADAM_B1) * g
    v = ADAM_B2 * v + (1.0 - ADAM_B2) * (g * g)
    m_hat = m / (1.0 - ADAM_B1 ** ADAM_STEP)
    v_hat = v / (1.0 - ADAM_B2 ** ADAM_STEP)
    delta = -ADAM_LR * (m_hat / (jnp.sqrt(v_hat) + ADAM_EPS) + ADAM_WD * w)
    return delta, m, v


def _ada_bwd_adamw(c_all_t, dmod_cols, w, m, v):
    L, D, NC = w.shape
    tr = min(D, 256)

    def body(ct_ref, dm_ref, w_ref, m_ref, v_ref, g_ref, d_ref, mo_ref, vo_ref):
        cond_t = _silu(ct_ref[...])
        dm = dm_ref[0]
        g = cond_t[:, 0:1] * dm[0:1, :]
        for b in range(1, N_DEV):
            g = g + cond_t[:, b:b + 1] * dm[b:b + 1, :]
        g_ref[0] = g
        d_ref[0], mo_ref[0], vo_ref[0] = _adamw(w_ref[0], g, m_ref[0], v_ref[0])

    wspec = pl.BlockSpec((1, tr, NC), lambda l, r: (l, r, 0))
    return pl.pallas_call(
        body, name="ada_bwd_adamw", grid=(L, D // tr),
        in_specs=[pl.BlockSpec((tr, N_DEV), lambda l, r: (r, 0)),
                  pl.BlockSpec((1, N_DEV, NC), lambda l, r: (l, 0, 0)), wspec, wspec, wspec],
        out_specs=[wspec] * 4, out_shape=[_sds(w.shape, F32)] * 4,
        compiler_params=pltpu.CompilerParams(dimension_semantics=("parallel", "parallel"), vmem_limit_bytes=VMEM_LIMIT),
    )(c_all_t, dmod_cols, w, m, v)


def _sum_devices(x):
    def body(x_ref, o_ref):
        s = x_ref[0]
        for j in range(1, N_DEV):
            s = s + x_ref[j]
        o_ref[...] = s

    return pl.pallas_call(body, name="sum_devices", out_shape=_sds(x.shape[1:], F32))(x)


def _adamw_small(w, g, m, v):
    def body(w_ref, g_ref, m_ref, v_ref, d_ref, mo_ref, vo_ref):
        d_ref[...], mo_ref[...], vo_ref[...] = _adamw(w_ref[...], g_ref[...], m_ref[...], v_ref[...])

    return pl.pallas_call(body, name="adamw_small", out_shape=[_sds(w.shape, F32)] * 3)(w, g, m, v)


def _me():
    return lax.axis_index("x") * 4 + lax.axis_index("y") * 2 + lax.axis_index("c")


def _peer(k):
    x, y, c = lax.axis_index("x"), lax.axis_index("y"), lax.axis_index("c")
    px = 1 - x if k & 4 else x
    py = 1 - y if k & 2 else y
    pc = 1 - c if k & 1 else c
    return (px, py, pc), px * 4 + py * 2 + pc


VMEM_SPEC = pl.BlockSpec(memory_space=pltpu.VMEM)
ANY_SPEC = pl.BlockSpec(memory_space=pl.ANY)
def _comm_sems(n):
    return [pltpu.SemaphoreType.DMA((n * (N_DEV - 1),)), pltpu.SemaphoreType.DMA((n * (N_DEV - 1),)),
            pltpu.SemaphoreType.DMA((n,))]


def _exchange_copies(srcs_of, dst_refs, send_sems, recv_sems, local_sems):
    me = _me()
    local, sends, recvs = [], [], []
    for a, (src_of, dst_ref) in enumerate(zip(srcs_of, dst_refs)):
        local.append(pltpu.make_async_copy(src_of(me), dst_ref.at[me], local_sems.at[a]))
        for k in range(1, N_DEV):
            dev, pj = _peer(k)
            i = a * (N_DEV - 1) + k - 1
            sems = dict(send_sem=send_sems.at[i], recv_sem=recv_sems.at[i], device_id=dev, device_id_type=MESH_IDS)
            sends.append(pltpu.make_async_remote_copy(src_ref=src_of(pj), dst_ref=dst_ref.at[me], **sems))
            recvs.append(pltpu.make_async_remote_copy(src_ref=src_of(pj), dst_ref=dst_ref.at[pj], **sems))
    return local, sends, recvs


def _start_exchange(copies):
    local, sends, _ = copies
    for cp in local + sends:
        cp.start()


def _finish_exchange(copies):
    local, sends, recvs = copies
    for cp in recvs:
        cp.wait_recv()
    for cp in sends:
        cp.wait_send()
    for cp in local:
        cp.wait()


def _sum_adamw(recv, w, m, v):
    shape = w.shape
    C = shape[-1]
    R = w.size // C
    rows = max(d for d in range(16, min(R, 512) + 1, 16) if R % d == 0 and d * C <= 256 * 1024)

    def body(r_ref, w_ref, m_ref, v_ref, go_ref, d_ref, mo_ref, vo_ref):
        g = r_ref[0].astype(F32)
        for j in range(1, N_DEV):
            g = g + r_ref[j].astype(F32)
        go_ref[...] = g
        d_ref[...], mo_ref[...], vo_ref[...] = _adamw(w_ref[...], g, m_ref[...], v_ref[...])

    spec = pl.BlockSpec((rows, C), lambda i: (i, 0))
    outs = pl.pallas_call(
        body, name="sum_adamw", grid=(R // rows,),
        in_specs=[pl.BlockSpec((N_DEV, rows, C), lambda i: (0, i, 0)), spec, spec, spec],
        out_specs=[spec] * 4, out_shape=[_sds((R, C), F32)] * 4,
        compiler_params=pltpu.CompilerParams(dimension_semantics=("parallel",), vmem_limit_bytes=VMEM_LIMIT),
    )(recv.reshape(N_DEV, R, C), w.reshape(R, C), m.reshape(R, C), v.reshape(R, C))
    return [o.reshape(shape) for o in outs]


def _all_gather(name, x, out_dtype):
    R, C = x.shape
    cast = out_dtype != x.dtype

    def body(x_ref, out_ref, buf, send_sems, recv_sems, local_sem):
        me = _me()
        if cast:
            buf[...] = x_ref[...].astype(out_dtype)
            src = buf
        else:
            src = x_ref
        local = pltpu.make_async_copy(src, out_ref.at[me], local_sem)
        local.start()
        sends = []
        for k in range(1, N_DEV):
            dev, _ = _peer(k)
            cp = pltpu.make_async_remote_copy(src_ref=src, dst_ref=out_ref.at[me], send_sem=send_sems.at[k - 1],
                                              recv_sem=recv_sems.at[k - 1], device_id=dev, device_id_type=MESH_IDS)
            cp.start()
            sends.append(cp)
        for k in range(1, N_DEV):
            dev, pj = _peer(k)
            pltpu.make_async_remote_copy(src_ref=src, dst_ref=out_ref.at[pj], send_sem=send_sems.at[k - 1],
                                         recv_sem=recv_sems.at[k - 1], device_id=dev, device_id_type=MESH_IDS).wait_recv()
        for cp in sends:
            cp.wait_send()
        local.wait()

    return pl.pallas_call(
        body, name=name, in_specs=[VMEM_SPEC], out_specs=ANY_SPEC, out_shape=_sds((N_DEV, R, C), out_dtype),
        scratch_shapes=[pltpu.VMEM((R, C) if cast else (8, 128), out_dtype),
                        pltpu.SemaphoreType.DMA((N_DEV - 1,)), pltpu.SemaphoreType.DMA((N_DEV - 1,)),
                        pltpu.SemaphoreType.DMA(())],
        compiler_params=pltpu.CompilerParams(vmem_limit_bytes=VMEM_LIMIT),
    )(x)


def _all_to_all(name, x):
    _, R, C = x.shape

    def body(x_ref, out_ref, send_sems, recv_sems, local_sem):
        me = _me()
        local = pltpu.make_async_copy(x_ref.at[me], out_ref.at[me], local_sem)
        local.start()
        sends = []
        for k in range(1, N_DEV):
            dev, pj = _peer(k)
            cp = pltpu.make_async_remote_copy(src_ref=x_ref.at[pj], dst_ref=out_ref.at[me], send_sem=send_sems.at[k - 1],
                                              recv_sem=recv_sems.at[k - 1], device_id=dev, device_id_type=MESH_IDS)
            cp.start()
            sends.append(cp)
        for k in range(1, N_DEV):
            dev, pj = _peer(k)
            pltpu.make_async_remote_copy(src_ref=x_ref.at[pj], dst_ref=out_ref.at[pj], send_sem=send_sems.at[k - 1],
                                         recv_sem=recv_sems.at[k - 1], device_id=dev, device_id_type=MESH_IDS).wait_recv()
        for cp in sends:
            cp.wait_send()
        local.wait()

    return pl.pallas_call(
        body, name=name, in_specs=[VMEM_SPEC], out_specs=VMEM_SPEC, out_shape=_sds(x.shape, x.dtype),
        scratch_shapes=[pltpu.SemaphoreType.DMA((N_DEV - 1,)), pltpu.SemaphoreType.DMA((N_DEV - 1,)),
                        pltpu.SemaphoreType.DMA(())],
    )(x)


def _reduce_scatter_adamw(name, gblk, w, m, v):
    _, R, C = gblk.shape
    rows = 8
    for cand in (136, 128, 80, 64, 40, 32, 16, 8):
        if R % cand == 0:
            rows = cand
            break

    def body(g_ref, w_ref, m_ref, v_ref, go_ref, d_ref, mo_ref, vo_ref, recv, send_sems, recv_sems, local_sem):
        me = _me()
        local = pltpu.make_async_copy(g_ref.at[me], recv.at[me], local_sem)
        local.start()
        sends = []
        for k in range(1, N_DEV):
            dev, pj = _peer(k)
            cp = pltpu.make_async_remote_copy(src_ref=g_ref.at[pj], dst_ref=recv.at[me], send_sem=send_sems.at[k - 1],
                                              recv_sem=recv_sems.at[k - 1], device_id=dev, device_id_type=MESH_IDS)
            cp.start()
            sends.append(cp)
        for k in range(1, N_DEV):
            dev, pj = _peer(k)
            pltpu.make_async_remote_copy(src_ref=g_ref.at[pj], dst_ref=recv.at[pj], send_sem=send_sems.at[k - 1],
                                         recv_sem=recv_sems.at[k - 1], device_id=dev, device_id_type=MESH_IDS).wait_recv()
        local.wait()

        def chunk(i, carry):
            r = pl.ds(pl.multiple_of(i * rows, rows), rows)
            g = recv[0, r, :].astype(F32)
            for j in range(1, N_DEV):
                g = g + recv[j, r, :].astype(F32)
            go_ref[r, :] = g
            d_ref[r, :], mo_ref[r, :], vo_ref[r, :] = _adamw(w_ref[r, :], g, m_ref[r, :], v_ref[r, :])
            return carry

        lax.fori_loop(0, R // rows, chunk, 0)
        for cp in sends:
            cp.wait_send()

    return pl.pallas_call(
        body, name=name, in_specs=[ANY_SPEC, VMEM_SPEC, VMEM_SPEC, VMEM_SPEC], out_specs=[VMEM_SPEC] * 4,
        out_shape=[_sds((R, C), F32)] * 4,
        scratch_shapes=[pltpu.VMEM((N_DEV, R, C), BF16), pltpu.SemaphoreType.DMA((N_DEV - 1,)),
                        pltpu.SemaphoreType.DMA((N_DEV - 1,)), pltpu.SemaphoreType.DMA(())],
        compiler_params=pltpu.CompilerParams(vmem_limit_bytes=VMEM_LIMIT),
    )(gblk, w, m, v)


FIRST_WEIGHTS = ["mla_w_dq", "mla_w_uq", "mla_w_dkv", "mla_w_ukv", "mla_w_o"]
LATE_WEIGHTS = ["swa_w_qkv", "swa_w_o", "w_ff1", "w_ff2"]
ROW_SHARDED = {"mla_w_dq", "mla_w_dkv", "mla_w_o", "swa_w_o", "w_ff2"}


def _unblock(name, blocks):
    sh = blocks.shape[1:]
    if name in ROW_SHARDED:
        return jnp.moveaxis(blocks, 0, 1).reshape(sh[0], N_DEV * sh[1], sh[2])
    return jnp.moveaxis(blocks, 0, 2).reshape(sh[0], sh[1], N_DEV * sh[2])


def _block(name, full):
    L, K, N = full.shape
    if name in ROW_SHARDED:
        return jnp.moveaxis(full.reshape(L, N_DEV, K // N_DEV, N), 1, 0)
    return jnp.moveaxis(full.reshape(L, K, N_DEV, N // N_DEV), 2, 0)


def _rot_cols(w):
    half = QK_ROPE // 2
    return jnp.concatenate([-w[..., half:], w[..., :half]], axis=-1)


def _unrot_cols(gw):
    half = QK_ROPE // 2
    return jnp.concatenate([gw[..., half:], -gw[..., :half]], axis=-1)


def _row(v):
    return v.reshape(1, -1)


def _mlp_block_bwd(dx, sv, w1, w2, g, sc, gt):
    dy, du, dgt = _mlp_bwd_a(dx, sv["y2"], sv["rl"], gt, w2)
    dw2 = _matmul_tn("dw_ff2", sv["act"], dy)
    dw1 = _matmul_tn("dw_ff1", sv["h2"], du)
    dxo, dsh, da = _mlp_bwd_b(du, sv["x1"], dx, w1, g, sc)
    return dxo, dw1, dw2, dsh, da, dgt


def kernel(x, c, positions, w_ada, b_ada, g_mix, g_mlp, mla_w_dq, mla_g_q, mla_w_uq, mla_w_dkv, mla_g_kv, mla_w_ukv, mla_w_o, swa_w_qkv, swa_b_qkv, swa_sinks, swa_w_o, swa_b_o, w_ff1, w_ff2, g_final, loss_target, m_w_ada, m_b_ada, m_g_mix, m_g_mlp, m_mla_w_dq, m_mla_g_q, m_mla_w_uq, m_mla_w_dkv, m_mla_g_kv, m_mla_w_ukv, m_mla_w_o, m_swa_w_qkv, m_swa_b_qkv, m_swa_sinks, m_swa_w_o, m_swa_b_o, m_w_ff1, m_w_ff2, m_g_final, v_w_ada, v_b_ada, v_g_mix, v_g_mlp, v_mla_w_dq, v_mla_g_q, v_mla_w_uq, v_mla_w_dkv, v_mla_g_kv, v_mla_w_ukv, v_mla_w_o, v_swa_w_qkv, v_swa_b_qkv, v_swa_sinks, v_swa_w_o, v_swa_b_o, v_w_ff1, v_w_ff2, v_g_final):
    S, D = x.shape[1], x.shape[2]
    me = _me()
    x0 = x[0]
    target = loss_target[0]
    big_w = dict(mla_w_dq=mla_w_dq, mla_w_uq=mla_w_uq, mla_w_dkv=mla_w_dkv, mla_w_ukv=mla_w_ukv, mla_w_o=mla_w_o,
                 swa_w_qkv=swa_w_qkv, swa_w_o=swa_w_o, w_ff1=w_ff1, w_ff2=w_ff2)
    big_m = dict(mla_w_dq=m_mla_w_dq, mla_w_uq=m_mla_w_uq, mla_w_dkv=m_mla_w_dkv, mla_w_ukv=m_mla_w_ukv,
                 mla_w_o=m_mla_w_o, swa_w_qkv=m_swa_w_qkv, swa_w_o=m_swa_w_o, w_ff1=m_w_ff1, w_ff2=m_w_ff2)
    big_v = dict(mla_w_dq=v_mla_w_dq, mla_w_uq=v_mla_w_uq, mla_w_dkv=v_mla_w_dkv, mla_w_ukv=v_mla_w_ukv,
                 mla_w_o=v_mla_w_o, swa_w_qkv=v_swa_w_qkv, swa_w_o=v_swa_w_o, w_ff1=v_w_ff1, w_ff2=v_w_ff2)
    groups = {"first": FIRST_WEIGHTS, "late": LATE_WEIGHTS}
    wrows = {n: -(-big_w[n].size // (PACK_COLS * 16)) * 16 for n in FIRST_WEIGHTS + LATE_WEIGHTS}
    offs = {g: np.concatenate([[0], np.cumsum([wrows[n] for n in names])]).astype(int) for g, names in groups.items()}

    def as_rows(n, a, lead=()):
        flat = a.reshape(lead + (-1,))
        pad = wrows[n] * PACK_COLS - flat.shape[-1]
        if pad:
            flat = jnp.pad(flat, ((0, 0),) * len(lead) + ((0, pad),))
        return flat.reshape(lead + (wrows[n], PACK_COLS))

    def pack(g, d):
        return jnp.concatenate([as_rows(n, d[n]) for n in groups[g]], axis=0)

    def pack_blocks(g, gfull):
        return jnp.concatenate([as_rows(n, _block(n, gfull[n]).astype(BF16), (N_DEV,)) for n in groups[g]], axis=1)

    def unpack(g, packed, lead=()):
        out = {}
        for i, n in enumerate(groups[g]):
            part = packed[..., int(offs[g][i]):int(offs[g][i + 1]), :].reshape(lead + (-1,))
            out[n] = part[..., :big_w[n].size].reshape(lead + big_w[n].shape)
        return out

    gathered = _all_gather("gather_weights", pack("first", big_w), BF16)
    wfull = {n: _unblock(n, b) for n, b in unpack("first", gathered, (N_DEV,)).items()}
    w_dq, w_dkv = wfull["mla_w_dq"][0], wfull["mla_w_dkv"][0]
    w_cat = jnp.concatenate([w_dq, w_dkv, _rot_cols(w_dkv[:, KV_LORA:])], axis=1)
    QL = w_dq.shape[1]
    w_uq = wfull["mla_w_uq"][0].reshape(QL, MLA_HEADS, QK_DIM)
    w_uqx = jnp.concatenate([w_uq, _rot_cols(w_uq[..., QK_NOPE:])], axis=-1).reshape(QL, MLA_HEADS * 256)
    w_ukv = wfull["mla_w_ukv"][0]
    w_o_mla = wfull["mla_w_o"][0]

    L = w_ada.shape[0]
    NC = w_ada.shape[2]
    nbq, nbo = swa_b_qkv.shape[1], swa_b_o.shape[1]
    cpad = -(-(D + nbq + nbo) // 1024) * 1024
    cpack = jnp.pad(jnp.concatenate([c[0], swa_b_qkv[0], swa_b_o[0]]), (0, cpad - (D + nbq + nbo))).reshape(8, cpad // 8)
    call = _all_gather("gather_c", cpack, F32).reshape(N_DEV, cpad)
    c_all = call[:, :D]
    b_qkv_full = call[:, D:D + nbq].reshape(1, N_DEV * nbq)
    b_o_full = call[:, D + nbq:D + nbq + nbo].reshape(1, N_DEV * nbo)
    mod_cols = _ada_fwd(c_all, w_ada)
    mpad = -(-(L * NC) // 1024) * 1024
    mod_send = jnp.pad(jnp.moveaxis(mod_cols, 1, 0).reshape(N_DEV, L * NC), ((0, 0), (0, mpad - L * NC)))
    mod_mine = _all_to_all("exchange_mod", mod_send.reshape(N_DEV, 8, mpad // 8)).reshape(N_DEV, mpad)[:, :L * NC]
    mod = jnp.moveaxis(mod_mine.reshape(N_DEV, L, NC), 0, 1).reshape(L, N_DEV * NC) + b_ada
    mods = mod.reshape(L, 6, 1, D)

    half = QK_ROPE // 2
    inv_freq = ROPE_THETA ** (-jnp.arange(half, dtype=F32) / half)
    ang = positions[0].astype(F32)[:, None] * inv_freq
    cos = jnp.concatenate([jnp.cos(ang), jnp.cos(ang)], axis=-1)
    sin = jnp.concatenate([jnp.sin(ang), jnp.sin(ang)], axis=-1)

    T_ATT = ATT_TILE
    zero_bias = jnp.zeros((1, D), F32)

    sh1, sc1, gt1, sh2, sc2, gt2 = [mods[0, i] for i in range(6)]
    gm0, gp0 = _row(g_mix[0]), _row(g_mlp[0])
    h1, cqp, cq, ckvp, ckv, q, k, v, vt = _mla_in_fwd(x0, cos, sin, gm0, sc1, sh1, w_cat, mla_g_q, w_uqx, mla_g_kv,
                                                      w_ukv, T_ATT)
    o0, lse0, gathered = _mla_attn_fwd(q, k, vt, T_ATT, [big_w[n].astype(BF16) for n in LATE_WEIGHTS])
    wfull = {n: _unblock(n, b) for n, b in zip(LATE_WEIGHTS, gathered)}
    w_qkv, w_o_swa = wfull["swa_w_qkv"][0], wfull["swa_w_o"][0]
    ff1, ff2 = wfull["w_ff1"], wfull["w_ff2"]
    y1, x1, h2 = _attn_out_fwd(o0, x0, w_o_mla, zero_bias, gt1, gp0, sc2, sh2)
    rl0, act0, y2, x2 = _mlp_fwd(h2, x1, ff1[0], ff2[0], gt2)
    sv0 = dict(y2=y2, rl=rl0, act=act0, h2=h2, x1=x1)

    th1, tc1, tg1, th2, tc2, tg2 = [mods[1, i] for i in range(6)]
    gm1, gp1 = _row(g_mix[1]), _row(g_mlp[1])
    h3, sq, sk, svv = _swa_in_fwd(x2, gm1, tc1, th1, w_qkv, b_qkv_full)
    o1, lse1 = _swa_attn_fwd(sq, sk, svv, swa_sinks)
    y3, x3, h4 = _attn_out_fwd(o1, x2, w_o_swa, b_o_full, tg1, gp1, tc2, th2)
    rl1, act1, y4, x4 = _mlp_fwd(h4, x3, ff1[1], ff2[1], tg2)
    sv1 = dict(y2=y4, rl=rl1, act=act1, h2=h4, x1=x3)
    dx4, loss_part, dg_final = _final_loss(x4, target, _row(g_final))

    dx3, dw1_1, dw2_1, dsh2_1, da2_1, dgt2_1 = _mlp_block_bwd(dx4, sv1, ff1[1], ff2[1], gp1, tc2, tg2)
    dy, do, dl, dgt1_1, db_o = _attn_out_bwd(dx3, y3, o1, tg1, w_o_swa, SWA_HEADS)
    dw_o_swa = _matmul_tn("dw_o", o1, dy)
    dsq, dsk, dsv, dsink = _swa_attn_bwd(sq, sk, svv, do, lse1, dl.T, swa_sinks)
    dqkv, dx2, db_qkv, dsh1_1, da1_1 = _swa_in_bwd(dsq, dsk, dsv, x2, dx3, w_qkv, gm1, tc1)
    dw_qkv = _matmul_tn("dw_qkv", h3, dqkv)

    dx1, dw1_0, dw2_0, dsh2_0, da2_0, dgt2_0 = _mlp_block_bwd(dx2, sv0, ff1[0], ff2[0], gp0, sc2, gt2)
    dy, do, dl, dgt1_0, _ = _attn_out_bwd(dx1, y1, o0, gt1, w_o_mla, MLA_HEADS)
    dw_o_mla = _matmul_tn("dw_o", o0, dy)
    tb = min(T_ATT, S)
    delta = dl.T.reshape(MLA_HEADS, S // tb, 1, tb)
    glate = dict(swa_w_qkv=dw_qkv[None], swa_w_o=dw_o_swa[None], w_ff1=jnp.stack([dw1_0, dw1_1]),
                 w_ff2=jnp.stack([dw2_0, dw2_1]))
    dq, dk, dv, recv = _mla_attn_bwd(q, k, v, do, lse0 * LOG2E, delta, T_ATT,
                                     [_block(n, glate[n]).astype(BF16) for n in LATE_WEIGHTS])
    late = {n: _sum_adamw(r, big_w[n], big_m[n], big_v[n]) for n, r in zip(LATE_WEIGHTS, recv)}
    dqx, dkv, dcat, dx0, dg_q, dg_kv, dsh1_0, da1_0 = _mla_in_bwd(
        dq, dk, dv, cos, sin, cqp, ckvp, x0, dx1, w_uqx, mla_g_q, w_ukv, mla_g_kv, w_cat, gm0, sc1)
    dw_uqx = _matmul_tn("dw_uq", cq, dqx).reshape(QL, MLA_HEADS, 256)
    dw_ukv = _matmul_tn("dw_ukv", ckv, dkv)
    dw_cat = _matmul_tn("dw_down", h1, dcat)
    dw_uq = jnp.concatenate([dw_uqx[..., :QK_NOPE], dw_uqx[..., 128:192] + _unrot_cols(dw_uqx[..., 192:256])],
                            axis=-1).reshape(QL, MLA_HEADS * QK_DIM)
    o_kr = QL + KV_LORA
    dw_dkv = jnp.concatenate([dw_cat[:, QL:o_kr],
                              dw_cat[:, o_kr:o_kr + QK_ROPE] + _unrot_cols(dw_cat[:, o_kr + QK_ROPE:])], axis=1)

    gfirst = dict(mla_w_dq=dw_cat[None, :, :QL], mla_w_uq=dw_uq[None], mla_w_dkv=dw_dkv[None], mla_w_ukv=dw_ukv[None],
                  mla_w_o=dw_o_mla[None])
    first = _reduce_scatter_adamw("grad_exchange_adamw", pack_blocks("first", gfirst), pack("first", big_w),
                                  pack("first", big_m), pack("first", big_v))
    big_g, big_d, big_nm, big_nv = ({**unpack("first", first[j]), **{n: late[n][j] for n in LATE_WEIGHTS}}
                                    for j in range(4))

    dmod = jnp.stack([
        jnp.concatenate([dsh1_0, gm0 * da1_0, dgt1_0, dsh2_0, gp0 * da2_0, dgt2_0], axis=1),
        jnp.concatenate([dsh1_1, gm1 * da1_1, dgt1_1, dsh2_1, gp1 * da2_1, dgt2_1], axis=1)]).reshape(-1)
    dg_mix = jnp.concatenate([(1.0 + sc1) * da1_0, (1.0 + tc1) * da1_1], axis=1).reshape(-1)
    dg_mlp = jnp.concatenate([(1.0 + sc2) * da2_0, (1.0 + tc2) * da2_1], axis=1).reshape(-1)
    parts = [loss_part.reshape(-1), dmod, dg_mix, dg_mlp, dg_q.reshape(-1), dg_kv.reshape(-1), dsink.reshape(-1),
             dg_final.reshape(-1), db_qkv.reshape(-1), db_o.reshape(-1)]
    soffs = np.concatenate([[0], np.cumsum([p.size for p in parts])])
    spad = -(-int(soffs[-1]) // 1024) * 1024
    spack = jnp.pad(jnp.concatenate(parts), (0, spad - int(soffs[-1]))).reshape(8, spad // 8)
    sall = _all_gather("gather_small_grads", spack, F32)
    ssum = _sum_devices(sall).reshape(-1)
    tot = [ssum[int(soffs[i]):int(soffs[i + 1])] for i in range(len(parts))]
    loss = tot[0][0]
    nsink = swa_sinks.shape[1]
    small_g = dict(b_ada=tot[1].reshape(b_ada.shape), g_mix=tot[2].reshape(g_mix.shape), g_mlp=tot[3].reshape(g_mlp.shape),
                   mla_g_q=tot[4].reshape(mla_g_q.shape), mla_g_kv=tot[5].reshape(mla_g_kv.shape),
                   swa_sinks=tot[6][:nsink].reshape(swa_sinks.shape), g_final=tot[7].reshape(g_final.shape),
                   swa_b_qkv=lax.dynamic_slice(tot[8], (me * nbq,), (nbq,)).reshape(swa_b_qkv.shape),
                   swa_b_o=lax.dynamic_slice(tot[9], (me * nbo,), (nbo,)).reshape(swa_b_o.shape))
    small_w = dict(b_ada=b_ada, g_mix=g_mix, g_mlp=g_mlp, mla_g_q=mla_g_q, mla_g_kv=mla_g_kv, swa_sinks=swa_sinks,
                   g_final=g_final, swa_b_qkv=swa_b_qkv, swa_b_o=swa_b_o)
    small_m = dict(b_ada=m_b_ada, g_mix=m_g_mix, g_mlp=m_g_mlp, mla_g_q=m_mla_g_q, mla_g_kv=m_mla_g_kv,
                   swa_sinks=m_swa_sinks, g_final=m_g_final, swa_b_qkv=m_swa_b_qkv, swa_b_o=m_swa_b_o)
    small_v = dict(b_ada=v_b_ada, g_mix=v_g_mix, g_mlp=v_g_mlp, mla_g_q=v_mla_g_q, mla_g_kv=v_mla_g_kv,
                   swa_sinks=v_swa_sinks, g_final=v_g_final, swa_b_qkv=v_swa_b_qkv, swa_b_o=v_swa_b_o)
    SMALL = list(small_w)
    woffs = np.concatenate([[0], np.cumsum([small_w[n].size for n in SMALL])])
    wpad = -(-int(woffs[-1]) // 1024) * 1024

    def spack_of(d):
        flat = jnp.concatenate([d[n].reshape(-1) for n in SMALL])
        return jnp.pad(flat, (0, wpad - int(woffs[-1]))).reshape(8, wpad // 8)

    sm = _adamw_small(spack_of(small_w), spack_of(small_g), spack_of(small_m), spack_of(small_v))
    small_d, small_nm, small_nv = (
        {n: a.reshape(-1)[int(woffs[i]):int(woffs[i + 1])].reshape(small_w[n].shape) for i, n in enumerate(SMALL)}
        for a in sm)

    b_off = int(soffs[1])
    dmod_all = sall.reshape(N_DEV, -1)[:, b_off:b_off + L * N_DEV * NC].reshape(N_DEV, L, N_DEV * NC)
    dmod_cols = jnp.moveaxis(lax.dynamic_slice_in_dim(dmod_all, me * NC, NC, axis=2), 0, 1)
    ada_g, ada_d, ada_nm, ada_nv = _ada_bwd_adamw(c_all.T, dmod_cols, w_ada, m_w_ada, v_w_ada)

    order = ["w_ada", "b_ada", "g_mix", "g_mlp", "mla_w_dq", "mla_g_q", "mla_w_uq", "mla_w_dkv", "mla_g_kv",
             "mla_w_ukv", "mla_w_o", "swa_w_qkv", "swa_b_qkv", "swa_sinks", "swa_w_o", "swa_b_o", "w_ff1", "w_ff2", "g_final"]

    def collect(ada, big, small):
        return [ada if n == "w_ada" else (big[n] if n in big else small[n]) for n in order]

    return (loss, dx0.reshape(x.shape), *collect(ada_g, big_g, small_g), *collect(ada_d, big_d, small_d),
            *collect(ada_nm, big_nm, small_nm), *collect(ada_nv, big_nv, small_nv))
```

```python
import functools

import jax
import jax.numpy as jnp
import numpy as np
from jax import lax
from jax.experimental import pallas as pl
from jax.experimental.pallas import tpu as pltpu

F32 = jnp.float32
BF16 = jnp.bfloat16
MESH_IDS = pl.DeviceIdType.MESH
N_DEV = 8

MLA_HEADS = 8
QK_NOPE = 128
QK_ROPE = 64
QK_DIM = QK_NOPE + QK_ROPE
V_DIM = 128
KV_LORA = 256
ROPE_THETA = 10000.0
SWA_HEADS = 16
SWA_KV_HEADS = 4
SWA_GROUP = SWA_HEADS // SWA_KV_HEADS
SWA_HEAD_DIM = 64
WINDOW = 128
EPS = 1e-6
LOG2E = 1.4426950408889634

ADAM_LR = 0.001
ADAM_B1 = 0.9
ADAM_B2 = 0.999
ADAM_EPS = 1e-08
ADAM_WD = 0.01
ADAM_STEP = 10

PACK_COLS = 1024
RS_CHUNKS = 4
VMEM_LIMIT = 56 << 20
ROW_TILE = 512
ROW_TILE_WIDE = 256
ATT_TILE = 512
TN_TOKENS = 2048


def _dot(a, b):
    return jnp.dot(a, b, preferred_element_type=F32)


def _dot_nt(a, b):
    return lax.dot_general(a, b, (((1,), (1,)), ((), ())), preferred_element_type=F32)


def _dot_tn(a, b):
    return lax.dot_general(a, b, (((0,), (0,)), ((), ())), preferred_element_type=F32)


def _rstd(x):
    return lax.rsqrt(jnp.mean(x * x, axis=-1, keepdims=True) + EPS)


def _rms_bwd(dn, n, r):
    return r * (dn - n * jnp.mean(dn * n, axis=-1, keepdims=True))


def _modulate(x, g, sc, sh):
    r = _rstd(x)
    return ((x * r) * g) * (1.0 + sc) + sh


def _modulate_bwd(dh, x, g, sc):
    r = _rstd(x)
    n = x * r
    dsh = jnp.sum(dh, axis=0, keepdims=True)
    da = jnp.sum(dh * n, axis=0, keepdims=True)
    dx = _rms_bwd(dh * (g * (1.0 + sc)), n, r)
    return dx, dsh, da


def _first(i):
    return i == 0


def _acc(ref, val, i):
    @pl.when(i == 0)
    def _():
        ref[...] = val

    @pl.when(i != 0)
    def _():
        ref[...] += val


def _row_spec(shape, tm):
    nd = len(shape)
    return pl.BlockSpec(tuple(shape[:nd - 2]) + (tm, shape[-1]), lambda i: (0,) * (nd - 2) + (i, 0))


def _resident_spec(shape, single_buffer):
    nd = len(shape)
    if single_buffer:
        return pl.BlockSpec(tuple(shape), lambda i: (0,) * nd, pipeline_mode=pl.Buffered(1))
    return pl.BlockSpec(tuple(shape), lambda i: (0,) * nd)


def _rowcall(name, body, tokens, tm, row_in, full_in, row_out, acc_out=()):
    tm = min(tm, tokens)
    in_specs = [_row_spec(a.shape, tm) for a in row_in] + [_resident_spec(a.shape, True) for a in full_in]
    row_specs = [s[1] if isinstance(s, tuple) else _row_spec(s.shape, tm) for s in row_out]
    row_out = [s[0] if isinstance(s, tuple) else s for s in row_out]
    out_specs = row_specs + [_resident_spec(s.shape, False) for s in acc_out]
    return pl.pallas_call(
        body, name=name, grid=(tokens // tm,), in_specs=in_specs, out_specs=out_specs,
        out_shape=list(row_out) + list(acc_out),
        compiler_params=pltpu.CompilerParams(dimension_semantics=("arbitrary",), vmem_limit_bytes=VMEM_LIMIT),
    )(*row_in, *full_in)


def _sds(shape, dtype):
    return jax.ShapeDtypeStruct(tuple(shape), dtype)


def _mla_in_fwd(x, cos, sin, g, sc, sh, w_cat, g_q, w_uqx, g_kv, w_ukv, t):
    S, D = x.shape
    QL = g_q.shape[1]
    H = MLA_HEADS
    t = min(t, S)

    def body(x_ref, cos_ref, sin_ref, g_ref, sc_ref, sh_ref, wcat_ref, gq_ref, wuqx_ref, gkv_ref, wukv_ref,
             h_ref, cqp_ref, cq_ref, ckvp_ref, ckv_ref, q_ref, k_ref, v_ref, vt_ref):
        cs, sn = cos_ref[...], sin_ref[...]
        hb = _modulate(x_ref[...], g_ref[...], sc_ref[...], sh_ref[...]).astype(BF16)
        h_ref[...] = hb
        low = _dot(hb, wcat_ref[...])
        cqp = low[:, :QL]
        cqp_ref[...] = cqp
        cq = ((cqp * _rstd(cqp)) * gq_ref[...]).astype(BF16)
        cq_ref[...] = cq
        ckvp = low[:, QL:QL + KV_LORA]
        ckvp_ref[...] = ckvp
        ckv = ((ckvp * _rstd(ckvp)) * gkv_ref[...]).astype(BF16)
        ckv_ref[...] = ckv
        o = QL + KV_LORA
        kr = (low[:, o:o + QK_ROPE] * cs + low[:, o + QK_ROPE:o + 2 * QK_ROPE] * sn).astype(BF16)
        qx = _dot(cq, wuqx_ref[...])
        kv = _dot(ckv, wukv_ref[...])
        for hd in range(H):
            b = hd * 256
            q_ref[hd, :, 0:QK_NOPE] = qx[:, b:b + QK_NOPE].astype(BF16)
            q_ref[hd, :, QK_NOPE:QK_DIM] = (qx[:, b + 128:b + 192] * cs + qx[:, b + 192:b + 256] * sn).astype(BF16)
            k_ref[hd, :, 0:QK_NOPE] = kv[:, b:b + QK_NOPE].astype(BF16)
            k_ref[hd, :, QK_NOPE:QK_DIM] = kr
            vh = kv[:, b + 128:b + 256]
            v_ref[hd] = vh.astype(BF16)
            vt_ref[hd, 0, 0:V_DIM, :] = vh.T.astype(BF16)
            vt_ref[hd, 0, V_DIM:2 * V_DIM, :] = jnp.ones((V_DIM, x_ref.shape[0]), BF16)

    vt_spec = pl.BlockSpec((H, 1, 2 * V_DIM, t), lambda i: (0, i, 0, 0))
    return _rowcall(
        "mla_in_fwd", body, S, t, [x, cos, sin], [g, sc, sh, w_cat, g_q, w_uqx, g_kv, w_ukv],
        [_sds((S, D), BF16), _sds((S, QL), F32), _sds((S, QL), BF16), _sds((S, KV_LORA), F32), _sds((S, KV_LORA), BF16),
         _sds((H, S, QK_DIM), BF16), _sds((H, S, QK_DIM), BF16), _sds((H, S, V_DIM), BF16),
         (_sds((H, S // t, 2 * V_DIM, t), BF16), vt_spec)])


def _mla_attn_fwd(q, k, vt, t, sends):
    H, S, DQ = q.shape
    DV = V_DIM
    t = min(t, S)
    nb = S // t
    scale = QK_DIM ** -0.5
    c2 = scale * LOG2E

    ns = len(sends)

    def body(q_ref, k_ref, vt_ref, *rest):
        send_refs, (o_ref, lse_ref), gath_refs = rest[:ns], rest[ns:ns + 2], rest[ns + 2:2 * ns + 2]
        m_s, acc_s, s_buf, send_sems, recv_sems, local_sems = rest[2 * ns + 2:]
        hd, qi = pl.program_id(0), pl.program_id(1)

        def gather():
            return _exchange_copies([lambda j, r=r: r for r in send_refs], gath_refs, send_sems, recv_sems, local_sems)

        @pl.when((hd == 0) & (qi == 0))
        def _():
            _start_exchange(gather())

        m_s[...] = jnp.full_like(m_s, -jnp.inf)
        acc_s[...] = jnp.zeros_like(acc_s)

        def scores(j, slot):
            rows = pl.ds(pl.multiple_of(j * t, t), t)
            s_buf[slot] = _dot_nt(k_ref[0, rows, :], q_ref[0])

        def absorb(j, slot, diagonal):
            s = s_buf[slot]
            if diagonal:
                key = lax.broadcasted_iota(jnp.int32, (t, t), 0)
                qry = lax.broadcasted_iota(jnp.int32, (t, t), 1)
                s = jnp.where(key <= qry, s, -jnp.inf)
            m_prev = m_s[...]
            m_new = jnp.maximum(m_prev, jnp.max(s, axis=0, keepdims=True))
            alpha = jnp.exp2((m_prev - m_new) * c2)
            p = jnp.exp2((s - m_new) * c2)
            acc_s[...] = alpha * acc_s[...] + _dot(vt_ref[0, j], p.astype(BF16))
            m_s[...] = m_new

        def pair(i, carry):
            j = 2 * i
            scores(j + 1, 1)
            absorb(j, 0, False)
            scores(j + 2, 0)
            absorb(j + 1, 1, False)
            return carry

        scores(0, 0)
        lax.fori_loop(0, qi // 2, pair, 0)

        @pl.when(qi % 2 == 0)
        def _():
            absorb(qi, 0, True)

        @pl.when(qi % 2 == 1)
        def _():
            scores(qi, 1)
            absorb(qi - 1, 0, False)
            absorb(qi, 1, True)

        acc = acc_s[...]
        o_ref[...] = (acc[:DV] / acc[DV:]).T.astype(BF16)
        lse_ref[0, 0] = m_s[...] * scale + jnp.log(acc[DV:DV + 1])

        @pl.when((hd == H - 1) & (qi == nb - 1))
        def _():
            _finish_exchange(gather())

    outs = pl.pallas_call(
        body, name="mla_attn_fwd", grid=(H, nb),
        in_specs=[pl.BlockSpec((1, t, DQ), lambda h, i: (h, i, 0)),
                  pl.BlockSpec((1, S, DQ), lambda h, i: (h, 0, 0)),
                  pl.BlockSpec((1, nb, 2 * DV, t), lambda h, i: (h, 0, 0, 0))] + [ANY_SPEC] * ns,
        out_specs=[pl.BlockSpec((t, DV), lambda h, i: (i, h)),
                   pl.BlockSpec((1, 1, 1, t), lambda h, i: (h, i, 0, 0))] + [ANY_SPEC] * ns,
        out_shape=[_sds((S, H * DV), BF16), _sds((H, nb, 1, t), F32)]
        + [_sds((N_DEV,) + a.shape, a.dtype) for a in sends],
        scratch_shapes=[pltpu.VMEM((1, t), F32), pltpu.VMEM((2 * DV, t), F32), pltpu.VMEM((2, t, t), F32)]
        + _comm_sems(ns),
        compiler_params=pltpu.CompilerParams(dimension_semantics=("arbitrary", "arbitrary"),
                                             vmem_limit_bytes=VMEM_LIMIT),
    )(q, k, vt, *sends)
    return outs[0], outs[1], outs[2:]


def _attn_out_fwd(o, x, w_o, b_o, gt, g, sc, sh):
    S, D = x.shape

    def body(o_ref, x_ref, wo_ref, bo_ref, gt_ref, g_ref, sc_ref, sh_ref, y_ref, x1_ref, h_ref):
        y = _dot(o_ref[...], wo_ref[...]) + bo_ref[...]
        y_ref[...] = y
        x1 = x_ref[...] + gt_ref[...] * y
        x1_ref[...] = x1
        h_ref[...] = _modulate(x1, g_ref[...], sc_ref[...], sh_ref[...]).astype(BF16)

    return _rowcall("attn_out_fwd", body, S, ROW_TILE, [o, x], [w_o, b_o, gt, g, sc, sh],
                    [_sds((S, D), F32), _sds((S, D), F32), _sds((S, D), BF16)])


def _mlp_fwd(h, x, w1, w2, gt):
    S, D = x.shape
    FF = w1.shape[1]

    def body(h_ref, x_ref, w1_ref, w2_ref, gt_ref, rl_ref, act_ref, y_ref, x2_ref):
        rl = jnp.maximum(_dot(h_ref[...], w1_ref[...]), 0.0)
        rl_ref[...] = rl.astype(BF16)
        act = (rl * rl).astype(BF16)
        act_ref[...] = act
        y = _dot(act, w2_ref[...])
        y_ref[...] = y
        x2_ref[...] = x_ref[...] + gt_ref[...] * y

    return _rowcall("mlp_fwd", body, S, ROW_TILE_WIDE, [h, x], [w1, w2, gt],
                    [_sds((S, FF), BF16), _sds((S, FF), BF16), _sds((S, D), F32), _sds((S, D), F32)])


def _swa_in_fwd(x, g, sc, sh, w_qkv, b_qkv):
    S, D = x.shape
    NQ = SWA_HEADS * SWA_HEAD_DIM
    NK = SWA_KV_HEADS * SWA_HEAD_DIM

    def body(x_ref, g_ref, sc_ref, sh_ref, w_ref, b_ref, h_ref, q_ref, k_ref, v_ref):
        hb = _modulate(x_ref[...], g_ref[...], sc_ref[...], sh_ref[...]).astype(BF16)
        h_ref[...] = hb
        qkv = _dot(hb, w_ref[...]) + b_ref[...]
        q_ref[...] = qkv[:, :NQ].astype(BF16)
        k_ref[...] = qkv[:, NQ:NQ + NK].astype(BF16)
        v_ref[...] = qkv[:, NQ + NK:].astype(BF16)

    return _rowcall("swa_in_fwd", body, S, ROW_TILE, [x], [g, sc, sh, w_qkv, b_qkv],
                    [_sds((S, D), BF16), _sds((S, NQ), BF16), _sds((S, NK), BF16), _sds((S, NK), BF16)])


def _alibi_slope(head):
    return float(np.float32(2.0 ** (-8.0 * (head + 1) / SWA_HEADS)))


def _swa_geometry(n):
    W, G = WINDOW, SWA_GROUP
    key = lax.broadcasted_iota(jnp.int32, (2 * W, G * W), 0)
    qry = lax.broadcasted_iota(jnp.int32, (2 * W, G * W), 1) & (W - 1)
    dist = W + qry - key
    valid = (dist >= 0) & (dist < W) & ((n > 0) | (key >= W))
    return dist.astype(F32), valid


def _swa_group(kh, q_ref, sink_ref):
    W, G, Dh = WINDOW, SWA_GROUP, SWA_HEAD_DIM
    heads = [kh * G + g for g in range(G)]
    q4 = jnp.concatenate([q_ref[:, h * Dh:(h + 1) * Dh] for h in heads], axis=0)
    slopes = jnp.concatenate([jnp.full((1, W), _alibi_slope(h), F32) for h in heads], axis=1)
    sinks = jnp.concatenate([jnp.broadcast_to(sink_ref[:, h:h + 1], (1, W)) for h in heads], axis=1)
    return heads, q4, slopes, sinks


def _swa_band_specs(W, nb, cols):
    prev = pl.BlockSpec((W, cols), lambda n: (jnp.maximum(jnp.minimum(n, nb - 1) - 1, 0), 0))
    cur = pl.BlockSpec((W, cols), lambda n: (jnp.minimum(n, nb - 1), 0))
    return prev, cur


def _swa_attn_fwd(q, k, v, sinks):
    S, NQ = q.shape
    NK = k.shape[1]
    W, Dh, G = WINDOW, SWA_HEAD_DIM, SWA_GROUP
    nb = S // W

    def body(q_ref, kp_ref, kc_ref, vp_ref, vc_ref, sink_ref, o_ref, lse_ref):
        distf, valid = _swa_geometry(pl.program_id(0))
        kband = jnp.concatenate([kp_ref[...], kc_ref[...]], axis=0)
        vband_t = jnp.concatenate([vp_ref[...], vc_ref[...]], axis=0).astype(F32).T.astype(BF16)
        outs = []
        for kh in range(SWA_KV_HEADS):
            kb = kband[:, kh * Dh:(kh + 1) * Dh]
            vbt = vband_t[kh * Dh:(kh + 1) * Dh, :]
            heads, q4, slopes, sinks = _swa_group(kh, q_ref, sink_ref)
            s = _dot_nt(kb, q4) * (Dh ** -0.5) - slopes * distf
            s = jnp.where(valid, s, -jnp.inf)
            m = jnp.maximum(jnp.max(s, axis=0, keepdims=True), sinks)
            p = jnp.exp(s - m)
            denom = jnp.sum(p, axis=0, keepdims=True) + jnp.exp(sinks - m)
            out4 = _dot(vbt, (p * (1.0 / denom)).astype(BF16))
            lse4 = m + jnp.log(denom)
            for g, h in enumerate(heads):
                outs.append(out4[:, g * W:(g + 1) * W])
                lse_ref[h:h + 1, :] = lse4[:, g * W:(g + 1) * W]
        o_ref[...] = jnp.concatenate(outs, axis=0).T.astype(BF16)

    kprev, kcur = _swa_band_specs(W, nb, NK)
    return pl.pallas_call(
        body, name="swa_attn_fwd", grid=(nb,),
        in_specs=[pl.BlockSpec((W, NQ), lambda n: (n, 0)), kprev, kcur, kprev, kcur,
                  pl.BlockSpec((1, SWA_HEADS), lambda n: (0, 0))],
        out_specs=[pl.BlockSpec((W, NQ), lambda n: (n, 0)), pl.BlockSpec((SWA_HEADS, W), lambda n: (0, n))],
        out_shape=[_sds((S, NQ), BF16), _sds((SWA_HEADS, S), F32)],
        compiler_params=pltpu.CompilerParams(dimension_semantics=("arbitrary",), vmem_limit_bytes=VMEM_LIMIT),
    )(q, k, k, v, v, sinks)


def _final_loss(x, target, g):
    S, D = x.shape

    def body(x_ref, t_ref, g_ref, dx_ref, loss_ref, dg_ref):
        i = pl.program_id(0)
        xv = x_ref[...]
        r = _rstd(xv)
        n = xv * r
        err = n * g_ref[...] - t_ref[...]
        part = 0.5 * jnp.sum(jnp.mean(err * err, axis=-1, keepdims=True), axis=0, keepdims=True)
        _acc(loss_ref, jnp.broadcast_to(part, loss_ref.shape), i)
        dout = err / D
        _acc(dg_ref, jnp.sum(dout * n, axis=0, keepdims=True), i)
        dx_ref[...] = _rms_bwd(dout * g_ref[...], n, r)

    return _rowcall("final_loss", body, S, ROW_TILE, [x, target], [g], [_sds((S, D), F32)],
                    [_sds((1, 128), F32), _sds((1, D), F32)])


def _mlp_bwd_a(dx, y, rl, gt, w2):
    S, D = dx.shape
    FF = rl.shape[1]

    def body(dx_ref, y_ref, rl_ref, gt_ref, w2_ref, dy_ref, du_ref, dgt_ref):
        i = pl.program_id(0)
        dxv = dx_ref[...]
        _acc(dgt_ref, jnp.sum(dxv * y_ref[...], axis=0, keepdims=True), i)
        dy = (dxv * gt_ref[...]).astype(BF16)
        dy_ref[...] = dy
        dact = _dot_nt(dy, w2_ref[...])
        du_ref[...] = (dact * (2.0 * rl_ref[...].astype(F32))).astype(BF16)

    return _rowcall("mlp_bwd_a", body, S, ROW_TILE_WIDE, [dx, y, rl], [gt, w2],
                    [_sds((S, D), BF16), _sds((S, FF), BF16)], [_sds((1, D), F32)])


def _mlp_bwd_b(du, x, dx, w1, g, sc):
    S, D = x.shape

    def body(du_ref, x_ref, dx_ref, w1_ref, g_ref, sc_ref, dxo_ref, dsh_ref, da_ref):
        i = pl.program_id(0)
        dh = _dot_nt(du_ref[...], w1_ref[...])
        dxn, dsh, da = _modulate_bwd(dh, x_ref[...], g_ref[...], sc_ref[...])
        dxo_ref[...] = dx_ref[...] + dxn
        _acc(dsh_ref, dsh, i)
        _acc(da_ref, da, i)

    return _rowcall("mlp_bwd_b", body, S, ROW_TILE_WIDE, [du, x, dx], [w1, g, sc],
                    [_sds((S, D), F32)], [_sds((1, D), F32), _sds((1, D), F32)])


def _attn_out_bwd(dx, y, o, gt, w_o, n_heads):
    S, D = dx.shape
    NO = o.shape[1]
    dh = NO // n_heads
    member = (jnp.arange(NO)[None, :] // dh == jnp.arange(16)[:, None]).astype(BF16)

    def body(dx_ref, y_ref, o_ref, gt_ref, wo_ref, mem_ref, dy_ref, do_ref, dl_ref, dgt_ref, dbo_ref):
        i = pl.program_id(0)
        dxv = dx_ref[...]
        _acc(dgt_ref, jnp.sum(dxv * y_ref[...], axis=0, keepdims=True), i)
        dy = dxv * gt_ref[...]
        _acc(dbo_ref, jnp.sum(dy, axis=0, keepdims=True), i)
        dyb = dy.astype(BF16)
        dy_ref[...] = dyb
        do = _dot_nt(dyb, wo_ref[...])
        do_ref[...] = do.astype(BF16)
        prod = do * o_ref[...].astype(F32)
        hi = prod.astype(BF16)
        lo = (prod - hi.astype(F32)).astype(BF16)
        dl_ref[...] = _dot_nt(mem_ref[...], hi) + _dot_nt(mem_ref[...], lo)

    tm = min(ROW_TILE, S)
    return _rowcall("attn_out_bwd", body, S, ROW_TILE, [dx, y, o], [gt, w_o, member],
                    [_sds((S, D), BF16), _sds((S, NO), BF16),
                     (_sds((16, S), F32), pl.BlockSpec((16, tm), lambda i: (0, i)))],
                    [_sds((1, D), F32), _sds((1, D), F32)])


def _mla_attn_bwd(q, k, v, do, lse, delta, t, gblks):
    H, S, DQ = q.shape
    DV = V_DIM
    t = min(t, S)
    nb = S // t
    scale = QK_DIM ** -0.5
    c2 = scale * LOG2E

    ng = len(gblks)

    def body(q_ref, k_ref, v_ref, do_ref, lse_ref, dl_ref, *rest):
        g_refs, (dq_ref, dk_ref, dv_ref), recv_refs = rest[:ng], rest[ng:ng + 3], rest[ng + 3:2 * ng + 3]
        dk_s, dv_s, s_buf, dp_buf, send_sems, recv_sems, local_sems = rest[2 * ng + 3:]
        hd, kj = pl.program_id(0), pl.program_id(1)

        def scatter():
            return _exchange_copies([lambda j, r=r: r.at[j] for r in g_refs], recv_refs, send_sems, recv_sems,
                                    local_sems)

        @pl.when((hd == 0) & (kj == 0))
        def _():
            _start_exchange(scatter())

        @pl.when(kj == 0)
        def _():
            dq_ref[...] = jnp.zeros_like(dq_ref)

        dk_s[...] = jnp.zeros_like(dk_s)
        dv_s[...] = jnp.zeros_like(dv_s)

        def products(i, slot):
            rows = pl.ds(pl.multiple_of(i * t, t), t)
            s_buf[slot] = _dot_nt(k_ref[0], q_ref[0, rows, :])
            dp_buf[slot] = _dot_nt(v_ref[0], do_ref[rows, :])

        def absorb(i, slot, diagonal):
            rows = pl.ds(pl.multiple_of(i * t, t), t)
            qb, dob = q_ref[0, rows, :], do_ref[rows, :]
            p = jnp.exp2(s_buf[slot] * c2 - lse_ref[0, i])
            if diagonal:
                key = lax.broadcasted_iota(jnp.int32, (t, t), 0)
                qry = lax.broadcasted_iota(jnp.int32, (t, t), 1)
                p = jnp.where(key <= qry, p, 0.0)
            dv_s[...] += _dot(p.astype(BF16), dob)
            ds = (p * (dp_buf[slot] - dl_ref[0, i])).astype(BF16)
            dk_s[...] += _dot(ds, qb)
            dq_ref[0, rows, :] += _dot_tn(ds, k_ref[0])

        n_off = nb - 1 - kj
        first = kj + 1

        def pair(i, carry):
            u = 2 * i
            products(first + u + 1, 1)
            absorb(first + u, 0, False)
            products(jnp.where(u + 2 < n_off, first + u + 2, kj), 0)
            absorb(first + u + 1, 1, False)
            return carry

        products(jnp.where(n_off > 0, first, kj), 0)
        lax.fori_loop(0, n_off // 2, pair, 0)

        @pl.when(n_off % 2 == 0)
        def _():
            absorb(kj, 0, True)

        @pl.when(n_off % 2 == 1)
        def _():
            products(kj, 1)
            absorb(nb - 1, 0, False)
            absorb(kj, 1, True)

        dk_ref[0] = (dk_s[...] * scale).astype(BF16)
        dv_ref[0] = dv_s[...].astype(BF16)

        @pl.when((hd == H - 1) & (kj == nb - 1))
        def _():
            _finish_exchange(scatter())

    rowspec = pl.BlockSpec((1, nb, 1, t), lambda h, j: (h, 0, 0, 0))
    outs = pl.pallas_call(
        body, name="mla_attn_bwd", grid=(H, nb),
        in_specs=[pl.BlockSpec((1, S, DQ), lambda h, j: (h, 0, 0)),
                  pl.BlockSpec((1, t, DQ), lambda h, j: (h, j, 0)),
                  pl.BlockSpec((1, t, DV), lambda h, j: (h, j, 0)),
                  pl.BlockSpec((S, DV), lambda h, j: (0, h)), rowspec, rowspec] + [ANY_SPEC] * ng,
        out_specs=[pl.BlockSpec((1, S, DQ), lambda h, j: (h, 0, 0)),
                   pl.BlockSpec((1, t, DQ), lambda h, j: (h, j, 0)),
                   pl.BlockSpec((1, t, DV), lambda h, j: (h, j, 0))] + [ANY_SPEC] * ng,
        out_shape=[_sds((H, S, DQ), F32), _sds((H, S, DQ), BF16), _sds((H, S, DV), BF16)]
        + [_sds(g.shape, g.dtype) for g in gblks],
        scratch_shapes=[pltpu.VMEM((t, DQ), F32), pltpu.VMEM((t, DV), F32), pltpu.VMEM((2, t, t), F32),
                        pltpu.VMEM((2, t, t), F32)] + _comm_sems(ng),
        compiler_params=pltpu.CompilerParams(dimension_semantics=("arbitrary", "arbitrary"),
                                             vmem_limit_bytes=VMEM_LIMIT),
    )(q, k, v, do, lse, delta, *gblks)
    return outs[0], outs[1], outs[2], outs[3:]


def _swa_attn_bwd(q, k, v, do, lse, delta, sinks):
    S, NQ = q.shape
    NK = k.shape[1]
    W, Dh, G = WINDOW, SWA_HEAD_DIM, SWA_GROUP
    nb = S // W

    def body(q_ref, kp_ref, kc_ref, vp_ref, vc_ref, do_ref, lse_ref, dl_ref, sink_ref,
             dq_ref, dk_ref, dv_ref, dsink_ref, dkc_s, dvc_s):
        n = pl.program_id(0)

        @pl.when(n == 0)
        def _():
            dkc_s[...] = jnp.zeros_like(dkc_s)
            dvc_s[...] = jnp.zeros_like(dvc_s)
            dsink_ref[...] = jnp.zeros_like(dsink_ref)

        @pl.when(n < nb)
        def _():
            distf, valid = _swa_geometry(n)
            kband = jnp.concatenate([kp_ref[...], kc_ref[...]], axis=0)
            vband = jnp.concatenate([vp_ref[...], vc_ref[...]], axis=0)
            kband_t = kband.astype(F32).T.astype(BF16)
            dq_t = []
            for kh in range(SWA_KV_HEADS):
                ck = slice(kh * Dh, (kh + 1) * Dh)
                kb, vb, kbt = kband[:, ck], vband[:, ck], kband_t[ck, :]
                heads, q4, slopes, sinks = _swa_group(kh, q_ref, sink_ref)
                do4 = jnp.concatenate([do_ref[:, h * Dh:(h + 1) * Dh] for h in heads], axis=0)
                lse4 = jnp.concatenate([lse_ref[h:h + 1, :] for h in heads], axis=1)
                dl4 = jnp.concatenate([dl_ref[h:h + 1, :] for h in heads], axis=1)
                s = _dot_nt(kb, q4) * (Dh ** -0.5) - slopes * distf
                p = jnp.where(valid, jnp.exp(s - lse4), 0.0)
                dvb = _dot(p.astype(BF16), do4)
                dp = _dot_nt(vb, do4)
                dsb = ((p * (dp - dl4)) * (Dh ** -0.5)).astype(BF16)
                dq4 = _dot(kbt, dsb)
                dkb = _dot(dsb, q4)
                dsk4 = jnp.exp(sinks - lse4) * dl4
                for g, h in enumerate(heads):
                    dq_t.append(dq4[:, g * W:(g + 1) * W])
                    dsink_ref[:, h:h + 1] += -jnp.sum(dsk4[:, g * W:(g + 1) * W], axis=1, keepdims=True)
                dk_ref[:, ck] = (dkc_s[:, ck] + dkb[:W]).astype(BF16)
                dv_ref[:, ck] = (dvc_s[:, ck] + dvb[:W]).astype(BF16)
                dkc_s[:, ck] = dkb[W:]
                dvc_s[:, ck] = dvb[W:]
            dq_ref[...] = jnp.concatenate(dq_t, axis=0).T.astype(BF16)

        @pl.when(n == nb)
        def _():
            dk_ref[...] = dkc_s[...].astype(BF16)
            dv_ref[...] = dvc_s[...].astype(BF16)

    kprev, kcur = _swa_band_specs(W, nb, NK)
    qspec = lambda cols: pl.BlockSpec((W, cols), lambda n: (jnp.minimum(n, nb - 1), 0))
    kvout = pl.BlockSpec((W, NK), lambda n: (jnp.maximum(n - 1, 0), 0))
    rowspec = pl.BlockSpec((SWA_HEADS, W), lambda n: (0, jnp.minimum(n, nb - 1)))
    return pl.pallas_call(
        body, name="swa_attn_bwd", grid=(nb + 1,),
        in_specs=[qspec(NQ), kprev, kcur, kprev, kcur, qspec(NQ), rowspec, rowspec,
                  pl.BlockSpec((1, SWA_HEADS), lambda n: (0, 0))],
        out_specs=[qspec(NQ), kvout, kvout, pl.BlockSpec((1, 128), lambda n: (0, 0))],
        out_shape=[_sds((S, NQ), BF16), _sds((S, NK), BF16), _sds((S, NK), BF16), _sds((1, 128), F32)],
        scratch_shapes=[pltpu.VMEM((W, NK), F32), pltpu.VMEM((W, NK), F32)],
        compiler_params=pltpu.CompilerParams(dimension_semantics=("arbitrary",), vmem_limit_bytes=VMEM_LIMIT),
    )(q, k, k, v, v, do, lse, delta, sinks)


def _swa_in_bwd(dq, dk, dv, x, dx, w_qkv, g, sc):
    S, D = x.shape
    N = w_qkv.shape[1]

    def body(dq_ref, dk_ref, dv_ref, x_ref, dx_ref, w_ref, g_ref, sc_ref, dqkv_ref, dxo_ref, db_ref, dsh_ref, da_ref):
        i = pl.program_id(0)
        dqkv = jnp.concatenate([dq_ref[...], dk_ref[...], dv_ref[...]], axis=1)
        dqkv_ref[...] = dqkv
        _acc(db_ref, jnp.sum(dqkv.astype(F32), axis=0, keepdims=True), i)
        dh = _dot_nt(dqkv, w_ref[...])
        dxn, dsh, da = _modulate_bwd(dh, x_ref[...], g_ref[...], sc_ref[...])
        dxo_ref[...] = dx_ref[...] + dxn
        _acc(dsh_ref, dsh, i)
        _acc(da_ref, da, i)

    return _rowcall("swa_in_bwd", body, S, ROW_TILE, [dq, dk, dv, x, dx], [w_qkv, g, sc],
                    [_sds((S, N), BF16), _sds((S, D), F32)],
                    [_sds((1, N), F32), _sds((1, D), F32), _sds((1, D), F32)])


def _mla_in_bwd(dq, dk, dv, cos, sin, cqp, ckvp, x, dx, w_uqx, g_q, w_ukv, g_kv, w_cat, g, sc):
    S, D = x.shape
    H = MLA_HEADS
    QL = g_q.shape[1]
    NX = w_uqx.shape[1]
    NC = w_cat.shape[1]

    def body(dq_ref, dk_ref, dv_ref, cos_ref, sin_ref, cqp_ref, ckvp_ref, x_ref, dx_ref,
             wuqx_ref, gq_ref, wukv_ref, gkv_ref, wcat_ref, g_ref, sc_ref,
             dqx_ref, dkv_ref, dcat_ref, dxo_ref, dgq_ref, dgkv_ref, dsh_ref, da_ref):
        i = pl.program_id(0)
        cs, sn = cos_ref[...], sin_ref[...]
        dkr = jnp.zeros(cs.shape, F32)
        for hd in range(H):
            b = hd * 256
            dqh = dq_ref[hd] * (QK_DIM ** -0.5)
            dqx_ref[:, b:b + QK_NOPE] = dqh[:, :QK_NOPE].astype(BF16)
            dqx_ref[:, b + 128:b + 192] = (dqh[:, QK_NOPE:] * cs).astype(BF16)
            dqx_ref[:, b + 192:b + 256] = (dqh[:, QK_NOPE:] * sn).astype(BF16)
            dkh = dk_ref[hd]
            dkv_ref[:, b:b + QK_NOPE] = dkh[:, :QK_NOPE]
            dkv_ref[:, b + 128:b + 256] = dv_ref[hd]
            dkr = dkr + dkh[:, QK_NOPE:].astype(F32)
        dcq = _dot_nt(dqx_ref[...], wuqx_ref[...])
        cqp = cqp_ref[...]
        rq = _rstd(cqp)
        nq = cqp * rq
        _acc(dgq_ref, jnp.sum(dcq * nq, axis=0, keepdims=True), i)
        dcqp = _rms_bwd(dcq * gq_ref[...], nq, rq)
        dckv = _dot_nt(dkv_ref[...], wukv_ref[...])
        ckvp = ckvp_ref[...]
        rk = _rstd(ckvp)
        nk = ckvp * rk
        _acc(dgkv_ref, jnp.sum(dckv * nk, axis=0, keepdims=True), i)
        dckvp = _rms_bwd(dckv * gkv_ref[...], nk, rk)
        dcat_ref[:, :QL] = dcqp.astype(BF16)
        dcat_ref[:, QL:QL + KV_LORA] = dckvp.astype(BF16)
        o = QL + KV_LORA
        dcat_ref[:, o:o + QK_ROPE] = (dkr * cs).astype(BF16)
        dcat_ref[:, o + QK_ROPE:o + 2 * QK_ROPE] = (dkr * sn).astype(BF16)
        dh = _dot_nt(dcat_ref[...], wcat_ref[...])
        dxn, dsh, da = _modulate_bwd(dh, x_ref[...], g_ref[...], sc_ref[...])
        dxo_ref[...] = dx_ref[...] + dxn
        _acc(dsh_ref, dsh, i)
        _acc(da_ref, da, i)

    return _rowcall("mla_in_bwd", body, S, ROW_TILE_WIDE, [dq, dk, dv, cos, sin, cqp, ckvp, x, dx],
                    [w_uqx, g_q, w_ukv, g_kv, w_cat, g, sc],
                    [_sds((S, NX), BF16), _sds((S, NX), BF16), _sds((S, NC), BF16), _sds((S, D), F32)],
                    [_sds((1, QL), F32), _sds((1, KV_LORA), F32), _sds((1, D), F32), _sds((1, D), F32)])


def _matmul_tn(name, a, b):
    S, K = a.shape
    N = b.shape[1]
    tk, tn, ts = min(K, 1024), min(N, 1024), min(S, TN_TOKENS)
    if N % tn:
        tn = 512 if N % 512 == 0 else (384 if N % 384 == 0 else 128)
    if K % tk:
        tk = 512 if K % 512 == 0 else (384 if K % 384 == 0 else 128)
    ns = S // ts

    def body(a_ref, b_ref, o_ref):
        _acc(o_ref, _dot_tn(a_ref[...], b_ref[...]), pl.program_id(2))

    return pl.pallas_call(
        body, name=name, grid=(K // tk, N // tn, ns),
        in_specs=[pl.BlockSpec((ts, tk), lambda i, j, s: (s, i)), pl.BlockSpec((ts, tn), lambda i, j, s: (s, j))],
        out_specs=pl.BlockSpec((tk, tn), lambda i, j, s: (i, j)),
        out_shape=_sds((K, N), F32),
        compiler_params=pltpu.CompilerParams(dimension_semantics=("parallel", "parallel", "arbitrary"),
                                             vmem_limit_bytes=VMEM_LIMIT),
    )(a, b)


def _silu(c):
    return c * jax.nn.sigmoid(c)


def _ada_fwd(c_all, w_ada):
    L, D, NC = w_ada.shape

    def body(c_ref, w_ref, o_ref):
        cond = _silu(c_ref[...]).astype(BF16)
        o_ref[0] = _dot(cond, w_ref[0].astype(BF16))

    return pl.pallas_call(
        body, name="ada_fwd", grid=(L,),
        in_specs=[pl.BlockSpec(c_all.shape, lambda l: (0, 0)), pl.BlockSpec((1, D, NC), lambda l: (l, 0, 0))],
        out_specs=pl.BlockSpec((1, N_DEV, NC), lambda l: (l, 0, 0)),
        out_shape=_sds((L, N_DEV, NC), F32),
        compiler_params=pltpu.CompilerParams(dimension_semantics=("arbitrary",), vmem_limit_bytes=VMEM_LIMIT),
    )(c_all, w_ada)


def _adamw(w, g, m, v):
    m = ADAM_B1 * m + (1.0 - ADAM_B1) * g
    v = ADAM_B2 * v + (1.0 - ADAM_B2) * (g * g)
    m_hat = m / (1.0 - ADAM_B1 ** ADAM_STEP)
    v_hat = v / (1.0 - ADAM_B2 ** ADAM_STEP)
    delta = -ADAM_LR * (m_hat / (jnp.sqrt(v_hat) + ADAM_EPS) + ADAM_WD * w)
    return delta, m, v


def _ada_bwd_adamw(c_all_t, dmod_cols, w, m, v):
    L, D, NC = w.shape
    tr = min(D, 256)

    def body(ct_ref, dm_ref, w_ref, m_ref, v_ref, g_ref, d_ref, mo_ref, vo_ref):
        cond_t = _silu(ct_ref[...])
        dm = dm_ref[0]
        g = cond_t[:, 0:1] * dm[0:1, :]
        for b in range(1, N_DEV):
            g = g + cond_t[:, b:b + 1] * dm[b:b + 1, :]
        g_ref[0] = g
        d_ref[0], mo_ref[0], vo_ref[0] = _adamw(w_ref[0], g, m_ref[0], v_ref[0])

    wspec = pl.BlockSpec((1, tr, NC), lambda l, r: (l, r, 0))
    return pl.pallas_call(
        body, name="ada_bwd_adamw", grid=(L, D // tr),
        in_specs=[pl.BlockSpec((tr, N_DEV), lambda l, r: (r, 0)),
                  pl.BlockSpec((1, N_DEV, NC), lambda l, r: (l, 0, 0)), wspec, wspec, wspec],
        out_specs=[wspec] * 4, out_shape=[_sds(w.shape, F32)] * 4,
        compiler_params=pltpu.CompilerParams(dimension_semantics=("parallel", "parallel"), vmem_limit_bytes=VMEM_LIMIT),
    )(c_all_t, dmod_cols, w, m, v)


def _sum_devices(x):
    def body(x_ref, o_ref):
        s = x_ref[0]
        for j in range(1, N_DEV):
            s = s + x_ref[j]
        o_ref[...] = s

    return pl.pallas_call(body, name="sum_devices", out_shape=_sds(x.shape[1:], F32))(x)


def _adamw_small(w, g, m, v):
    def body(w_ref, g_ref, m_ref, v_ref, d_ref, mo_ref, vo_ref):
        d_ref[...], mo_ref[...], vo_ref[...] = _adamw(w_ref[...], g_ref[...], m_ref[...], v_ref[...])

    return pl.pallas_call(body, name="adamw_small", out_shape=[_sds(w.shape, F32)] * 3)(w, g, m, v)


def _me():
    return lax.axis_index("x") * 4 + lax.axis_index("y") * 2 + lax.axis_index("c")


def _peer(k):
    x, y, c = lax.axis_index("x"), lax.axis_index("y"), lax.axis_index("c")
    px = 1 - x if k & 4 else x
    py = 1 - y if k & 2 else y
    pc = 1 - c if k & 1 else c
    return (px, py, pc), px * 4 + py * 2 + pc


VMEM_SPEC = pl.BlockSpec(memory_space=pltpu.VMEM)
ANY_SPEC = pl.BlockSpec(memory_space=pl.ANY)
def _comm_sems(n):
    return [pltpu.SemaphoreType.DMA((n * (N_DEV - 1),)), pltpu.SemaphoreType.DMA((n * (N_DEV - 1),)),
            pltpu.SemaphoreType.DMA((n,))]


def _exchange_copies(srcs_of, dst_refs, send_sems, recv_sems, local_sems):
    me = _me()
    local, sends, recvs = [], [], []
    for a, (src_of, dst_ref) in enumerate(zip(srcs_of, dst_refs)):
        local.append(pltpu.make_async_copy(src_of(me), dst_ref.at[me], local_sems.at[a]))
        for k in range(1, N_DEV):
            dev, pj = _peer(k)
            i = a * (N_DEV - 1) + k - 1
            sems = dict(send_sem=send_sems.at[i], recv_sem=recv_sems.at[i], device_id=dev, device_id_type=MESH_IDS)
            sends.append(pltpu.make_async_remote_copy(src_ref=src_of(pj), dst_ref=dst_ref.at[me], **sems))
            recvs.append(pltpu.make_async_remote_copy(src_ref=src_of(pj), dst_ref=dst_ref.at[pj], **sems))
    return local, sends, recvs


def _start_exchange(copies):
    local, sends, _ = copies
    for cp in local + sends:
        cp.start()


def _finish_exchange(copies):
    local, sends, recvs = copies
    for cp in recvs:
        cp.wait_recv()
    for cp in sends:
        cp.wait_send()
    for cp in local:
        cp.wait()


def _sum_adamw(recv, w, m, v):
    shape = w.shape
    C = shape[-1]
    R = w.size // C
    rows = max(d for d in range(16, min(R, 512) + 1, 16) if R % d == 0 and d * C <= 256 * 1024)

    def body(r_ref, w_ref, m_ref, v_ref, go_ref, d_ref, mo_ref, vo_ref):
        g = r_ref[0].astype(F32)
        for j in range(1, N_DEV):
            g = g + r_ref[j].astype(F32)
        go_ref[...] = g
        d_ref[...], mo_ref[...], vo_ref[...] = _adamw(w_ref[...], g, m_ref[...], v_ref[...])

    spec = pl.BlockSpec((rows, C), lambda i: (i, 0))
    outs = pl.pallas_call(
        body, name="sum_adamw", grid=(R // rows,),
        in_specs=[pl.BlockSpec((N_DEV, rows, C), lambda i: (0, i, 0)), spec, spec, spec],
        out_specs=[spec] * 4, out_shape=[_sds((R, C), F32)] * 4,
        compiler_params=pltpu.CompilerParams(dimension_semantics=("parallel",), vmem_limit_bytes=VMEM_LIMIT),
    )(recv.reshape(N_DEV, R, C), w.reshape(R, C), m.reshape(R, C), v.reshape(R, C))
    return [o.reshape(shape) for o in outs]


def _all_gather(name, x, out_dtype):
    R, C = x.shape
    cast = out_dtype != x.dtype

    def body(x_ref, out_ref, buf, send_sems, recv_sems, local_sem):
        me = _me()
        if cast:
            buf[...] = x_ref[...].astype(out_dtype)
            src = buf
        else:
            src = x_ref
        local = pltpu.make_async_copy(src, out_ref.at[me], local_sem)
        local.start()
        sends = []
        for k in range(1, N_DEV):
            dev, _ = _peer(k)
            cp = pltpu.make_async_remote_copy(src_ref=src, dst_ref=out_ref.at[me], send_sem=send_sems.at[k - 1],
                                              recv_sem=recv_sems.at[k - 1], device_id=dev, device_id_type=MESH_IDS)
            cp.start()
            sends.append(cp)
        for k in range(1, N_DEV):
            dev, pj = _peer(k)
            pltpu.make_async_remote_copy(src_ref=src, dst_ref=out_ref.at[pj], send_sem=send_sems.at[k - 1],
                                         recv_sem=recv_sems.at[k - 1], device_id=dev, device_id_type=MESH_IDS).wait_recv()
        for cp in sends:
            cp.wait_send()
        local.wait()

    return pl.pallas_call(
        body, name=name, in_specs=[VMEM_SPEC], out_specs=ANY_SPEC, out_shape=_sds((N_DEV, R, C), out_dtype),
        scratch_shapes=[pltpu.VMEM((R, C) if cast else (8, 128), out_dtype),
                        pltpu.SemaphoreType.DMA((N_DEV - 1,)), pltpu.SemaphoreType.DMA((N_DEV - 1,)),
                        pltpu.SemaphoreType.DMA(())],
        compiler_params=pltpu.CompilerParams(vmem_limit_bytes=VMEM_LIMIT),
    )(x)


def _all_to_all(name, x):
    _, R, C = x.shape

    def body(x_ref, out_ref, send_sems, recv_sems, local_sem):
        me = _me()
        local = pltpu.make_async_copy(x_ref.at[me], out_ref.at[me], local_sem)
        local.start()
        sends = []
        for k in range(1, N_DEV):
            dev, pj = _peer(k)
            cp = pltpu.make_async_remote_copy(src_ref=x_ref.at[pj], dst_ref=out_ref.at[me], send_sem=send_sems.at[k - 1],
                                              recv_sem=recv_sems.at[k - 1], device_id=dev, device_id_type=MESH_IDS)
            cp.start()
            sends.append(cp)
        for k in range(1, N_DEV):
            dev, pj = _peer(k)
            pltpu.make_async_remote_copy(src_ref=x_ref.at[pj], dst_ref=out_ref.at[pj], send_sem=send_sems.at[k - 1],
                                         recv_sem=recv_sems.at[k - 1], device_id=dev, device_id_type=MESH_IDS).wait_recv()
        for cp in sends:
            cp.wait_send()
        local.wait()

    return pl.pallas_call(
        body, name=name, in_specs=[VMEM_SPEC], out_specs=VMEM_SPEC, out_shape=_sds(x.shape, x.dtype),
        scratch_shapes=[pltpu.SemaphoreType.DMA((N_DEV - 1,)), pltpu.SemaphoreType.DMA((N_DEV - 1,)),
                        pltpu.SemaphoreType.DMA(())],
    )(x)


def _reduce_scatter_adamw(name, gblk, w, m, v):
    _, R, C = gblk.shape
    rows = 8
    for cand in (136, 128, 80, 64, 40, 32, 16, 8):
        if R % cand == 0:
            rows = cand
            break

    def body(g_ref, w_ref, m_ref, v_ref, go_ref, d_ref, mo_ref, vo_ref, recv, send_sems, recv_sems, local_sem):
        me = _me()
        local = pltpu.make_async_copy(g_ref.at[me], recv.at[me], local_sem)
        local.start()
        sends = []
        for k in range(1, N_DEV):
            dev, pj = _peer(k)
            cp = pltpu.make_async_remote_copy(src_ref=g_ref.at[pj], dst_ref=recv.at[me], send_sem=send_sems.at[k - 1],
                                              recv_sem=recv_sems.at[k - 1], device_id=dev, device_id_type=MESH_IDS)
            cp.start()
            sends.append(cp)
        for k in range(1, N_DEV):
            dev, pj = _peer(k)
            pltpu.make_async_remote_copy(src_ref=g_ref.at[pj], dst_ref=recv.at[pj], send_sem=send_sems.at[k - 1],
                                         recv_sem=recv_sems.at[k - 1], device_id=dev, device_id_type=MESH_IDS).wait_recv()
        local.wait()

        def chunk(i, carry):
            r = pl.ds(pl.multiple_of(i * rows, rows), rows)
            g = recv[0, r, :].astype(F32)
            for j in range(1, N_DEV):
                g = g + recv[j, r, :].astype(F32)
            go_ref[r, :] = g
            d_ref[r, :], mo_ref[r, :], vo_ref[r, :] = _adamw(w_ref[r, :], g, m_ref[r, :], v_ref[r, :])
            return carry

        lax.fori_loop(0, R // rows, chunk, 0)
        for cp in sends:
            cp.wait_send()

    return pl.pallas_call(
        body, name=name, in_specs=[ANY_SPEC, VMEM_SPEC, VMEM_SPEC, VMEM_SPEC], out_specs=[VMEM_SPEC] * 4,
        out_shape=[_sds((R, C), F32)] * 4,
        scratch_shapes=[pltpu.VMEM((N_DEV, R, C), BF16), pltpu.SemaphoreType.DMA((N_DEV - 1,)),
                        pltpu.SemaphoreType.DMA((N_DEV - 1,)), pltpu.SemaphoreType.DMA(())],
        compiler_params=pltpu.CompilerParams(vmem_limit_bytes=VMEM_LIMIT),
    )(gblk, w, m, v)


FIRST_WEIGHTS = ["mla_w_dq", "mla_w_uq", "mla_w_dkv", "mla_w_ukv"]
LATE_WEIGHTS = ["mla_w_o", "swa_w_qkv", "swa_w_o", "w_ff1", "w_ff2"]
ROW_SHARDED = {"mla_w_dq", "mla_w_dkv", "mla_w_o", "swa_w_o", "w_ff2"}


def _unblock(name, blocks):
    sh = blocks.shape[1:]
    if name in ROW_SHARDED:
        return jnp.moveaxis(blocks, 0, 1).reshape(sh[0], N_DEV * sh[1], sh[2])
    return jnp.moveaxis(blocks, 0, 2).reshape(sh[0], sh[1], N_DEV * sh[2])


def _block(name, full):
    L, K, N = full.shape
    if name in ROW_SHARDED:
        return jnp.moveaxis(full.reshape(L, N_DEV, K // N_DEV, N), 1, 0)
    return jnp.moveaxis(full.reshape(L, K, N_DEV, N // N_DEV), 2, 0)


def _rot_cols(w):
    half = QK_ROPE // 2
    return jnp.concatenate([-w[..., half:], w[..., :half]], axis=-1)


def _unrot_cols(gw):
    half = QK_ROPE // 2
    return jnp.concatenate([gw[..., half:], -gw[..., :half]], axis=-1)


def _row(v):
    return v.reshape(1, -1)


def _mlp_block_bwd(dx, sv, w1, w2, g, sc, gt):
    dy, du, dgt = _mlp_bwd_a(dx, sv["y2"], sv["rl"], gt, w2)
    dw2 = _matmul_tn("dw_ff2", sv["act"], dy)
    dw1 = _matmul_tn("dw_ff1", sv["h2"], du)
    dxo, dsh, da = _mlp_bwd_b(du, sv["x1"], dx, w1, g, sc)
    return dxo, dw1, dw2, dsh, da, dgt


def kernel(x, c, positions, w_ada, b_ada, g_mix, g_mlp, mla_w_dq, mla_g_q, mla_w_uq, mla_w_dkv, mla_g_kv, mla_w_ukv, mla_w_o, swa_w_qkv, swa_b_qkv, swa_sinks, swa_w_o, swa_b_o, w_ff1, w_ff2, g_final, loss_target, m_w_ada, m_b_ada, m_g_mix, m_g_mlp, m_mla_w_dq, m_mla_g_q, m_mla_w_uq, m_mla_w_dkv, m_mla_g_kv, m_mla_w_ukv, m_mla_w_o, m_swa_w_qkv, m_swa_b_qkv, m_swa_sinks, m_swa_w_o, m_swa_b_o, m_w_ff1, m_w_ff2, m_g_final, v_w_ada, v_b_ada, v_g_mix, v_g_mlp, v_mla_w_dq, v_mla_g_q, v_mla_w_uq, v_mla_w_dkv, v_mla_g_kv, v_mla_w_ukv, v_mla_w_o, v_swa_w_qkv, v_swa_b_qkv, v_swa_sinks, v_swa_w_o, v_swa_b_o, v_w_ff1, v_w_ff2, v_g_final):
    S, D = x.shape[1], x.shape[2]
    me = _me()
    x0 = x[0]
    target = loss_target[0]
    big_w = dict(mla_w_dq=mla_w_dq, mla_w_uq=mla_w_uq, mla_w_dkv=mla_w_dkv, mla_w_ukv=mla_w_ukv, mla_w_o=mla_w_o,
                 swa_w_qkv=swa_w_qkv, swa_w_o=swa_w_o, w_ff1=w_ff1, w_ff2=w_ff2)
    big_m = dict(mla_w_dq=m_mla_w_dq, mla_w_uq=m_mla_w_uq, mla_w_dkv=m_mla_w_dkv, mla_w_ukv=m_mla_w_ukv,
                 mla_w_o=m_mla_w_o, swa_w_qkv=m_swa_w_qkv, swa_w_o=m_swa_w_o, w_ff1=m_w_ff1, w_ff2=m_w_ff2)
    big_v = dict(mla_w_dq=v_mla_w_dq, mla_w_uq=v_mla_w_uq, mla_w_dkv=v_mla_w_dkv, mla_w_ukv=v_mla_w_ukv,
                 mla_w_o=v_mla_w_o, swa_w_qkv=v_swa_w_qkv, swa_w_o=v_swa_w_o, w_ff1=v_w_ff1, w_ff2=v_w_ff2)
    groups = {"first": FIRST_WEIGHTS, "late": LATE_WEIGHTS}
    wrows = {n: -(-big_w[n].size // (PACK_COLS * 16)) * 16 for n in FIRST_WEIGHTS + LATE_WEIGHTS}
    offs = {g: np.concatenate([[0], np.cumsum([wrows[n] for n in names])]).astype(int) for g, names in groups.items()}

    def as_rows(n, a, lead=()):
        flat = a.reshape(lead + (-1,))
        pad = wrows[n] * PACK_COLS - flat.shape[-1]
        if pad:
            flat = jnp.pad(flat, ((0, 0),) * len(lead) + ((0, pad),))
        return flat.reshape(lead + (wrows[n], PACK_COLS))

    def pack(g, d):
        return jnp.concatenate([as_rows(n, d[n]) for n in groups[g]], axis=0)

    def pack_blocks(g, gfull):
        return jnp.concatenate([as_rows(n, _block(n, gfull[n]).astype(BF16), (N_DEV,)) for n in groups[g]], axis=1)

    def unpack(g, packed, lead=()):
        out = {}
        for i, n in enumerate(groups[g]):
            part = packed[..., int(offs[g][i]):int(offs[g][i + 1]), :].reshape(lead + (-1,))
            out[n] = part[..., :big_w[n].size].reshape(lead + big_w[n].shape)
        return out

    gathered = _all_gather("gather_weights", pack("first", big_w), BF16)
    wfull = {n: _unblock(n, b) for n, b in unpack("first", gathered, (N_DEV,)).items()}
    w_dq, w_dkv = wfull["mla_w_dq"][0], wfull["mla_w_dkv"][0]
    w_cat = jnp.concatenate([w_dq, w_dkv, _rot_cols(w_dkv[:, KV_LORA:])], axis=1)
    QL = w_dq.shape[1]
    w_uq = wfull["mla_w_uq"][0].reshape(QL, MLA_HEADS, QK_DIM)
    w_uqx = jnp.concatenate([w_uq, _rot_cols(w_uq[..., QK_NOPE:])], axis=-1).reshape(QL, MLA_HEADS * 256)
    w_ukv = wfull["mla_w_ukv"][0]

    L = w_ada.shape[0]
    NC = w_ada.shape[2]
    nbq, nbo = swa_b_qkv.shape[1], swa_b_o.shape[1]
    cpad = -(-(D + nbq + nbo) // 1024) * 1024
    cpack = jnp.pad(jnp.concatenate([c[0], swa_b_qkv[0], swa_b_o[0]]), (0, cpad - (D + nbq + nbo))).reshape(8, cpad // 8)
    call = _all_gather("gather_c", cpack, F32).reshape(N_DEV, cpad)
    c_all = call[:, :D]
    b_qkv_full = call[:, D:D + nbq].reshape(1, N_DEV * nbq)
    b_o_full = call[:, D + nbq:D + nbq + nbo].reshape(1, N_DEV * nbo)
    mod_cols = _ada_fwd(c_all, w_ada)
    mpad = -(-(L * NC) // 1024) * 1024
    mod_send = jnp.pad(jnp.moveaxis(mod_cols, 1, 0).reshape(N_DEV, L * NC), ((0, 0), (0, mpad - L * NC)))
    mod_mine = _all_to_all("exchange_mod", mod_send.reshape(N_DEV, 8, mpad // 8)).reshape(N_DEV, mpad)[:, :L * NC]
    mod = jnp.moveaxis(mod_mine.reshape(N_DEV, L, NC), 0, 1).reshape(L, N_DEV * NC) + b_ada
    mods = mod.reshape(L, 6, 1, D)

    half = QK_ROPE // 2
    inv_freq = ROPE_THETA ** (-jnp.arange(half, dtype=F32) / half)
    ang = positions[0].astype(F32)[:, None] * inv_freq
    cos = jnp.concatenate([jnp.cos(ang), jnp.cos(ang)], axis=-1)
    sin = jnp.concatenate([jnp.sin(ang), jnp.sin(ang)], axis=-1)

    T_ATT = ATT_TILE
    zero_bias = jnp.zeros((1, D), F32)

    sh1, sc1, gt1, sh2, sc2, gt2 = [mods[0, i] for i in range(6)]
    gm0, gp0 = _row(g_mix[0]), _row(g_mlp[0])
    h1, cqp, cq, ckvp, ckv, q, k, v, vt = _mla_in_fwd(x0, cos, sin, gm0, sc1, sh1, w_cat, mla_g_q, w_uqx, mla_g_kv,
                                                      w_ukv, T_ATT)
    o0, lse0, gathered = _mla_attn_fwd(q, k, vt, T_ATT, [big_w[n].astype(BF16) for n in LATE_WEIGHTS])
    wfull = {n: _unblock(n, b) for n, b in zip(LATE_WEIGHTS, gathered)}
    w_o_mla, w_qkv, w_o_swa = wfull["mla_w_o"][0], wfull["swa_w_qkv"][0], wfull["swa_w_o"][0]
    ff1, ff2 = wfull["w_ff1"], wfull["w_ff2"]
    y1, x1, h2 = _attn_out_fwd(o0, x0, w_o_mla, zero_bias, gt1, gp0, sc2, sh2)
    rl0, act0, y2, x2 = _mlp_fwd(h2, x1, ff1[0], ff2[0], gt2)
    sv0 = dict(y2=y2, rl=rl0, act=act0, h2=h2, x1=x1)

    th1, tc1, tg1, th2, tc2, tg2 = [mods[1, i] for i in range(6)]
    gm1, gp1 = _row(g_mix[1]), _row(g_mlp[1])
    h3, sq, sk, svv = _swa_in_fwd(x2, gm1, tc1, th1, w_qkv, b_qkv_full)
    o1, lse1 = _swa_attn_fwd(sq, sk, svv, swa_sinks)
    y3, x3, h4 = _attn_out_fwd(o1, x2, w_o_swa, b_o_full, tg1, gp1, tc2, th2)
    rl1, act1, y4, x4 = _mlp_fwd(h4, x3, ff1[1], ff2[1], tg2)
    sv1 = dict(y2=y4, rl=rl1, act=act1, h2=h4, x1=x3)
    dx4, loss_part, dg_final = _final_loss(x4, target, _row(g_final))

    dx3, dw1_1, dw2_1, dsh2_1, da2_1, dgt2_1 = _mlp_block_bwd(dx4, sv1, ff1[1], ff2[1], gp1, tc2, tg2)
    dy, do, dl, dgt1_1, db_o = _attn_out_bwd(dx3, y3, o1, tg1, w_o_swa, SWA_HEADS)
    dw_o_swa = _matmul_tn("dw_o", o1, dy)
    dsq, dsk, dsv, dsink = _swa_attn_bwd(sq, sk, svv, do, lse1, dl, swa_sinks)
    dqkv, dx2, db_qkv, dsh1_1, da1_1 = _swa_in_bwd(dsq, dsk, dsv, x2, dx3, w_qkv, gm1, tc1)
    dw_qkv = _matmul_tn("dw_qkv", h3, dqkv)

    dx1, dw1_0, dw2_0, dsh2_0, da2_0, dgt2_0 = _mlp_block_bwd(dx2, sv0, ff1[0], ff2[0], gp0, sc2, gt2)
    dy, do, dl, dgt1_0, _ = _attn_out_bwd(dx1, y1, o0, gt1, w_o_mla, MLA_HEADS)
    dw_o_mla = _matmul_tn("dw_o", o0, dy)
    tb = min(T_ATT, S)
    delta = dl[:MLA_HEADS].reshape(MLA_HEADS, S // tb, 1, tb)
    glate = dict(mla_w_o=dw_o_mla[None], swa_w_qkv=dw_qkv[None], swa_w_o=dw_o_swa[None],
                 w_ff1=jnp.stack([dw1_0, dw1_1]), w_ff2=jnp.stack([dw2_0, dw2_1]))
    dq, dk, dv, recv = _mla_attn_bwd(q, k, v, do, lse0 * LOG2E, delta, T_ATT,
                                     [_block(n, glate[n]).astype(BF16) for n in LATE_WEIGHTS])
    late = {n: _sum_adamw(r, big_w[n], big_m[n], big_v[n]) for n, r in zip(LATE_WEIGHTS, recv)}
    dqx, dkv, dcat, dx0, dg_q, dg_kv, dsh1_0, da1_0 = _mla_in_bwd(
        dq, dk, dv, cos, sin, cqp, ckvp, x0, dx1, w_uqx, mla_g_q, w_ukv, mla_g_kv, w_cat, gm0, sc1)
    dw_uqx = _matmul_tn("dw_uq", cq, dqx).reshape(QL, MLA_HEADS, 256)
    dw_ukv = _matmul_tn("dw_ukv", ckv, dkv)
    dw_cat = _matmul_tn("dw_down", h1, dcat)
    dw_uq = jnp.concatenate([dw_uqx[..., :QK_NOPE], dw_uqx[..., 128:192] + _unrot_cols(dw_uqx[..., 192:256])],
                            axis=-1).reshape(QL, MLA_HEADS * QK_DIM)
    o_kr = QL + KV_LORA
    dw_dkv = jnp.concatenate([dw_cat[:, QL:o_kr],
                              dw_cat[:, o_kr:o_kr + QK_ROPE] + _unrot_cols(dw_cat[:, o_kr + QK_ROPE:])], axis=1)

    gfirst = dict(mla_w_dq=dw_cat[None, :, :QL], mla_w_uq=dw_uq[None], mla_w_dkv=dw_dkv[None], mla_w_ukv=dw_ukv[None])
    first = _reduce_scatter_adamw("grad_exchange_adamw", pack_blocks("first", gfirst), pack("first", big_w),
                                  pack("first", big_m), pack("first", big_v))
    big_g, big_d, big_nm, big_nv = ({**unpack("first", first[j]), **{n: late[n][j] for n in LATE_WEIGHTS}}
                                    for j in range(4))

    dmod = jnp.stack([
        jnp.concatenate([dsh1_0, gm0 * da1_0, dgt1_0, dsh2_0, gp0 * da2_0, dgt2_0], axis=1),
        jnp.concatenate([dsh1_1, gm1 * da1_1, dgt1_1, dsh2_1, gp1 * da2_1, dgt2_1], axis=1)]).reshape(-1)
    dg_mix = jnp.concatenate([(1.0 + sc1) * da1_0, (1.0 + tc1) * da1_1], axis=1).reshape(-1)
    dg_mlp = jnp.concatenate([(1.0 + sc2) * da2_0, (1.0 + tc2) * da2_1], axis=1).reshape(-1)
    parts = [loss_part.reshape(-1), dmod, dg_mix, dg_mlp, dg_q.reshape(-1), dg_kv.reshape(-1), dsink.reshape(-1),
             dg_final.reshape(-1), db_qkv.reshape(-1), db_o.reshape(-1)]
    soffs = np.concatenate([[0], np.cumsum([p.size for p in parts])])
    spad = -(-int(soffs[-1]) // 1024) * 1024
    spack = jnp.pad(jnp.concatenate(parts), (0, spad - int(soffs[-1]))).reshape(8, spad // 8)
    sall = _all_gather("gather_small_grads", spack, F32)
    ssum = _sum_devices(sall).reshape(-1)
    tot = [ssum[int(soffs[i]):int(soffs[i + 1])] for i in range(len(parts))]
    loss = tot[0][0]
    nsink = swa_sinks.shape[1]
    small_g = dict(b_ada=tot[1].reshape(b_ada.shape), g_mix=tot[2].reshape(g_mix.shape), g_mlp=tot[3].reshape(g_mlp.shape),
                   mla_g_q=tot[4].reshape(mla_g_q.shape), mla_g_kv=tot[5].reshape(mla_g_kv.shape),
                   swa_sinks=tot[6][:nsink].reshape(swa_sinks.shape), g_final=tot[7].reshape(g_final.shape),
                   swa_b_qkv=lax.dynamic_slice(tot[8], (me * nbq,), (nbq,)).reshape(swa_b_qkv.shape),
                   swa_b_o=lax.dynamic_slice(tot[9], (me * nbo,), (nbo,)).reshape(swa_b_o.shape))
    small_w = dict(b_ada=b_ada, g_mix=g_mix, g_mlp=g_mlp, mla_g_q=mla_g_q, mla_g_kv=mla_g_kv, swa_sinks=swa_sinks,
                   g_final=g_final, swa_b_qkv=swa_b_qkv, swa_b_o=swa_b_o)
    small_m = dict(b_ada=m_b_ada, g_mix=m_g_mix, g_mlp=m_g_mlp, mla_g_q=m_mla_g_q, mla_g_kv=m_mla_g_kv,
                   swa_sinks=m_swa_sinks, g_final=m_g_final, swa_b_qkv=m_swa_b_qkv, swa_b_o=m_swa_b_o)
    small_v = dict(b_ada=v_b_ada, g_mix=v_g_mix, g_mlp=v_g_mlp, mla_g_q=v_mla_g_q, mla_g_kv=v_mla_g_kv,
                   swa_sinks=v_swa_sinks, g_final=v_g_final, swa_b_qkv=v_swa_b_qkv, swa_b_o=v_swa_b_o)
    SMALL = list(small_w)
    woffs = np.concatenate([[0], np.cumsum([small_w[n].size for n in SMALL])])
    wpad = -(-int(woffs[-1]) // 1024) * 1024

    def spack_of(d):
        flat = jnp.concatenate([d[n].reshape(-1) for n in SMALL])
        return jnp.pad(flat, (0, wpad - int(woffs[-1]))).reshape(8, wpad // 8)

    sm = _adamw_small(spack_of(small_w), spack_of(small_g), spack_of(small_m), spack_of(small_v))
    small_d, small_nm, small_nv = (
        {n: a.reshape(-1)[int(woffs[i]):int(woffs[i + 1])].reshape(small_w[n].shape) for i, n in enumerate(SMALL)}
        for a in sm)

    b_off = int(soffs[1])
    dmod_all = sall.reshape(N_DEV, -1)[:, b_off:b_off + L * N_DEV * NC].reshape(N_DEV, L, N_DEV * NC)
    dmod_cols = jnp.moveaxis(lax.dynamic_slice_in_dim(dmod_all, me * NC, NC, axis=2), 0, 1)
    ada_g, ada_d, ada_nm, ada_nv = _ada_bwd_adamw(c_all.T, dmod_cols, w_ada, m_w_ada, v_w_ada)

    order = ["w_ada", "b_ada", "g_mix", "g_mlp", "mla_w_dq", "mla_g_q", "mla_w_uq", "mla_w_dkv", "mla_g_kv",
             "mla_w_ukv", "mla_w_o", "swa_w_qkv", "swa_b_qkv", "swa_sinks", "swa_w_o", "swa_b_o", "w_ff1", "w_ff2", "g_final"]

    def collect(ada, big, small):
        return [ada if n == "w_ada" else (big[n] if n in big else small[n]) for n in order]

    return (loss, dx0.reshape(x.shape), *collect(ada_g, big_g, small_g), *collect(ada_d, big_d, small_d),
            *collect(ada_nm, big_nm, small_nm), *collect(ada_nv, big_nv, small_nv))
```

```python
import functools

import jax
import jax.numpy as jnp
import numpy as np
from jax import lax
from jax.experimental import pallas as pl
from jax.experimental.pallas import tpu as pltpu

F32 = jnp.float32
BF16 = jnp.bfloat16
MESH_IDS = pl.DeviceIdType.MESH
N_DEV = 8

MLA_HEADS = 8
QK_NOPE = 128
QK_ROPE = 64
QK_DIM = QK_NOPE + QK_ROPE
V_DIM = 128
KV_LORA = 256
ROPE_THETA = 10000.0
SWA_HEADS = 16
SWA_KV_HEADS = 4
SWA_GROUP = SWA_HEADS // SWA_KV_HEADS
SWA_HEAD_DIM = 64
WINDOW = 128
EPS = 1e-6
LOG2E = 1.4426950408889634

ADAM_LR = 0.001
ADAM_B1 = 0.9
ADAM_B2 = 0.999
ADAM_EPS = 1e-08
ADAM_WD = 0.01
ADAM_STEP = 10

PACK_COLS = 1024
RS_CHUNKS = 4
VMEM_LIMIT = 56 << 20
ROW_TILE = 512
ROW_TILE_WIDE = 256
ROW_TILE_BWD = 512
ATT_TILE = 512
ATT_TILE_FWD = 1024
TN_TOKENS = 2048


def _dot(a, b):
    return jnp.dot(a, b, preferred_element_type=F32)


def _dot_nt(a, b):
    return lax.dot_general(a, b, (((1,), (1,)), ((), ())), preferred_element_type=F32)


def _dot_tn(a, b):
    return lax.dot_general(a, b, (((0,), (0,)), ((), ())), preferred_element_type=F32)


def _rstd(x):
    return lax.rsqrt(jnp.mean(x * x, axis=-1, keepdims=True) + EPS)


def _rms_bwd(dn, n, r):
    return r * (dn - n * jnp.mean(dn * n, axis=-1, keepdims=True))


def _modulate(x, g, sc, sh):
    r = _rstd(x)
    return ((x * r) * g) * (1.0 + sc) + sh


def _modulate_bwd(dh, x, g, sc):
    r = _rstd(x)
    n = x * r
    dsh = jnp.sum(dh, axis=0, keepdims=True)
    da = jnp.sum(dh * n, axis=0, keepdims=True)
    dx = _rms_bwd(dh * (g * (1.0 + sc)), n, r)
    return dx, dsh, da


def _first(i):
    return i == 0


def _acc(ref, val, i):
    @pl.when(i == 0)
    def _():
        ref[...] = val

    @pl.when(i != 0)
    def _():
        ref[...] += val


def _row_spec(shape, tm):
    nd = len(shape)
    return pl.BlockSpec(tuple(shape[:nd - 2]) + (tm, shape[-1]), lambda i: (0,) * (nd - 2) + (i, 0))


def _resident_spec(shape, single_buffer):
    nd = len(shape)
    if single_buffer:
        return pl.BlockSpec(tuple(shape), lambda i: (0,) * nd, pipeline_mode=pl.Buffered(1))
    return pl.BlockSpec(tuple(shape), lambda i: (0,) * nd)


def _rowcall(name, body, tokens, tm, row_in, full_in, row_out, acc_out=()):
    tm = min(tm, tokens)
    in_specs = [_row_spec(a.shape, tm) for a in row_in] + [_resident_spec(a.shape, True) for a in full_in]
    row_specs = [s[1] if isinstance(s, tuple) else _row_spec(s.shape, tm) for s in row_out]
    row_out = [s[0] if isinstance(s, tuple) else s for s in row_out]
    out_specs = row_specs + [_resident_spec(s.shape, False) for s in acc_out]
    return pl.pallas_call(
        body, name=name, grid=(tokens // tm,), in_specs=in_specs, out_specs=out_specs,
        out_shape=list(row_out) + list(acc_out),
        compiler_params=pltpu.CompilerParams(dimension_semantics=("arbitrary",), vmem_limit_bytes=VMEM_LIMIT),
    )(*row_in, *full_in)


def _sds(shape, dtype):
    return jax.ShapeDtypeStruct(tuple(shape), dtype)


def _mla_in_fwd(x, cos, sin, g, sc, sh, w_cat, g_q, w_uqx, g_kv, w_ukv, t):
    S, D = x.shape
    QL = g_q.shape[1]
    H = MLA_HEADS
    t = min(t, S)

    def body(x_ref, cos_ref, sin_ref, g_ref, sc_ref, sh_ref, wcat_ref, gq_ref, wuqx_ref, gkv_ref, wukv_ref,
             h_ref, cqp_ref, cq_ref, ckvp_ref, ckv_ref, q_ref, k_ref, v_ref, vt_ref):
        cs, sn = cos_ref[...], sin_ref[...]
        hb = _modulate(x_ref[...], g_ref[...], sc_ref[...], sh_ref[...]).astype(BF16)
        h_ref[...] = hb
        low = _dot(hb, wcat_ref[...])
        cqp = low[:, :QL]
        cqp_ref[...] = cqp
        cq = ((cqp * _rstd(cqp)) * gq_ref[...]).astype(BF16)
        cq_ref[...] = cq
        ckvp = low[:, QL:QL + KV_LORA]
        ckvp_ref[...] = ckvp
        ckv = ((ckvp * _rstd(ckvp)) * gkv_ref[...]).astype(BF16)
        ckv_ref[...] = ckv
        o = QL + KV_LORA
        kr = (low[:, o:o + QK_ROPE] * cs + low[:, o + QK_ROPE:o + 2 * QK_ROPE] * sn).astype(BF16)
        qx = _dot(cq, wuqx_ref[...])
        kv = _dot(ckv, wukv_ref[...])
        for hd in range(H):
            b = hd * 256
            q_ref[hd, :, 0:QK_NOPE] = qx[:, b:b + QK_NOPE].astype(BF16)
            q_ref[hd, :, QK_NOPE:QK_DIM] = (qx[:, b + 128:b + 192] * cs + qx[:, b + 192:b + 256] * sn).astype(BF16)
            k_ref[hd, :, 0:QK_NOPE] = kv[:, b:b + QK_NOPE].astype(BF16)
            k_ref[hd, :, QK_NOPE:QK_DIM] = kr
            vh = kv[:, b + 128:b + 256]
            v_ref[hd] = vh.astype(BF16)
            vt_ref[hd, 0, 0:V_DIM, :] = vh.T.astype(BF16)
            vt_ref[hd, 0, V_DIM:2 * V_DIM, :] = jnp.ones((V_DIM, x_ref.shape[0]), BF16)

    vt_spec = pl.BlockSpec((H, 1, 2 * V_DIM, t), lambda i: (0, i, 0, 0))
    return _rowcall(
        "mla_in_fwd", body, S, t, [x, cos, sin], [g, sc, sh, w_cat, g_q, w_uqx, g_kv, w_ukv],
        [_sds((S, D), BF16), _sds((S, QL), F32), _sds((S, QL), BF16), _sds((S, KV_LORA), F32), _sds((S, KV_LORA), BF16),
         _sds((H, S, QK_DIM), BF16), _sds((H, S, QK_DIM), BF16), _sds((H, S, V_DIM), BF16),
         (_sds((H, S // t, 2 * V_DIM, t), BF16), vt_spec)])


def _mla_attn_fwd(q, k, vt, t, sends):
    H, S, DQ = q.shape
    DV = V_DIM
    vb = vt.shape[-1]
    t = max(min(t, S), vb)
    nb = S // t
    scale = QK_DIM ** -0.5
    c2 = scale * LOG2E

    ns = len(sends)

    def body(q_ref, k_ref, vt_ref, *rest):
        send_refs, (o_ref, lse_ref), gath_refs = rest[:ns], rest[ns:ns + 2], rest[ns + 2:2 * ns + 2]
        m_s, acc_s, s_buf, send_sems, recv_sems, local_sems = rest[2 * ns + 2:]
        hd, qi = pl.program_id(0), pl.program_id(1)

        def gather():
            return _exchange_copies([lambda j, r=r: r for r in send_refs], gath_refs, send_sems, recv_sems, local_sems)

        @pl.when((hd == 0) & (qi == 0))
        def _():
            _start_exchange(gather())

        m_s[...] = jnp.full_like(m_s, -jnp.inf)
        acc_s[...] = jnp.zeros_like(acc_s)

        def scores(j, slot):
            rows = pl.ds(pl.multiple_of(j * t, t), t)
            s_buf[slot] = _dot_nt(k_ref[0, rows, :], q_ref[0])

        def absorb(j, slot, diagonal):
            s = s_buf[slot]
            if diagonal:
                key = lax.broadcasted_iota(jnp.int32, (t, t), 0)
                qry = lax.broadcasted_iota(jnp.int32, (t, t), 1)
                s = jnp.where(key <= qry, s, -jnp.inf)
            m_prev = m_s[...]
            m_new = jnp.maximum(m_prev, jnp.max(s, axis=0, keepdims=True))
            alpha = jnp.exp2((m_prev - m_new) * c2)
            p = jnp.exp2((s - m_new) * c2)
            pb = p.astype(BF16)
            acc = alpha * acc_s[...]
            for u in range(t // vb):
                acc = acc + _dot(vt_ref[0, j * (t // vb) + u], pb[u * vb:(u + 1) * vb, :])
            acc_s[...] = acc
            m_s[...] = m_new

        def pair(i, carry):
            j = 2 * i
            scores(j + 1, 1)
            absorb(j, 0, False)
            scores(j + 2, 0)
            absorb(j + 1, 1, False)
            return carry

        scores(0, 0)
        lax.fori_loop(0, qi // 2, pair, 0)

        @pl.when(qi % 2 == 0)
        def _():
            absorb(qi, 0, True)

        @pl.when(qi % 2 == 1)
        def _():
            scores(qi, 1)
            absorb(qi - 1, 0, False)
            absorb(qi, 1, True)

        acc = acc_s[...]
        o_ref[...] = (acc[:DV] / acc[DV:]).T.astype(BF16)
        lse_ref[0, 0] = m_s[...] * scale + jnp.log(acc[DV:DV + 1])

        @pl.when((hd == H - 1) & (qi == nb - 1))
        def _():
            _finish_exchange(gather())

    outs = pl.pallas_call(
        body, name="mla_attn_fwd", grid=(H, nb),
        in_specs=[pl.BlockSpec((1, t, DQ), lambda h, i: (h, i, 0)),
                  pl.BlockSpec((1, S, DQ), lambda h, i: (h, 0, 0)),
                  pl.BlockSpec((1, S // vb, 2 * DV, vb), lambda h, i: (h, 0, 0, 0))] + [ANY_SPEC] * ns,
        out_specs=[pl.BlockSpec((t, DV), lambda h, i: (i, h)),
                   pl.BlockSpec((1, 1, 1, t), lambda h, i: (h, i, 0, 0))] + [ANY_SPEC] * ns,
        out_shape=[_sds((S, H * DV), BF16), _sds((H, nb, 1, t), F32)]
        + [_sds((N_DEV,) + a.shape, a.dtype) for a in sends],
        scratch_shapes=[pltpu.VMEM((1, t), F32), pltpu.VMEM((2 * DV, t), F32), pltpu.VMEM((2, t, t), F32)]
        + _comm_sems(ns),
        compiler_params=pltpu.CompilerParams(dimension_semantics=("arbitrary", "arbitrary"),
                                             vmem_limit_bytes=VMEM_LIMIT),
    )(q, k, vt, *sends)
    return outs[0], outs[1], outs[2:]


def _attn_out_fwd(o, x, w_o, b_o, gt, g, sc, sh):
    S, D = x.shape

    def body(o_ref, x_ref, wo_ref, bo_ref, gt_ref, g_ref, sc_ref, sh_ref, y_ref, x1_ref, h_ref):
        y = _dot(o_ref[...], wo_ref[...]) + bo_ref[...]
        y_ref[...] = y.astype(BF16)
        x1 = x_ref[...] + gt_ref[...] * y
        x1_ref[...] = x1
        h_ref[...] = _modulate(x1, g_ref[...], sc_ref[...], sh_ref[...]).astype(BF16)

    return _rowcall("attn_out_fwd", body, S, ROW_TILE, [o, x], [w_o, b_o, gt, g, sc, sh],
                    [_sds((S, D), BF16), _sds((S, D), F32), _sds((S, D), BF16)])


def _mlp_fwd(h, x, w1, w2, gt):
    S, D = x.shape
    FF = w1.shape[1]

    def body(h_ref, x_ref, w1_ref, w2_ref, gt_ref, rl_ref, act_ref, y_ref, x2_ref):
        rl = jnp.maximum(_dot(h_ref[...], w1_ref[...]), 0.0)
        rl_ref[...] = rl.astype(BF16)
        act = (rl * rl).astype(BF16)
        act_ref[...] = act
        y = _dot(act, w2_ref[...])
        y_ref[...] = y.astype(BF16)
        x2_ref[...] = x_ref[...] + gt_ref[...] * y

    return _rowcall("mlp_fwd", body, S, ROW_TILE_WIDE, [h, x], [w1, w2, gt],
                    [_sds((S, FF), BF16), _sds((S, FF), BF16), _sds((S, D), BF16), _sds((S, D), F32)])


def _swa_in_fwd(x, g, sc, sh, w_qkv, b_qkv):
    S, D = x.shape
    NQ = SWA_HEADS * SWA_HEAD_DIM
    NK = SWA_KV_HEADS * SWA_HEAD_DIM

    def body(x_ref, g_ref, sc_ref, sh_ref, w_ref, b_ref, h_ref, q_ref, k_ref, v_ref):
        hb = _modulate(x_ref[...], g_ref[...], sc_ref[...], sh_ref[...]).astype(BF16)
        h_ref[...] = hb
        qkv = _dot(hb, w_ref[...]) + b_ref[...]
        q_ref[...] = qkv[:, :NQ].astype(BF16)
        k_ref[...] = qkv[:, NQ:NQ + NK].astype(BF16)
        v_ref[...] = qkv[:, NQ + NK:].astype(BF16)

    return _rowcall("swa_in_fwd", body, S, ROW_TILE, [x], [g, sc, sh, w_qkv, b_qkv],
                    [_sds((S, D), BF16), _sds((S, NQ), BF16), _sds((S, NK), BF16), _sds((S, NK), BF16)])


def _alibi_slope(head):
    return float(np.float32(2.0 ** (-8.0 * (head + 1) / SWA_HEADS)))


def _swa_geometry(n):
    W, G = WINDOW, SWA_GROUP
    key = lax.broadcasted_iota(jnp.int32, (2 * W, G * W), 0)
    qry = lax.broadcasted_iota(jnp.int32, (2 * W, G * W), 1) & (W - 1)
    dist = W + qry - key
    valid = (dist >= 0) & (dist < W) & ((n > 0) | (key >= W))
    return dist.astype(F32), valid


def _swa_group(kh, q_ref, sink_ref):
    W, G, Dh = WINDOW, SWA_GROUP, SWA_HEAD_DIM
    heads = [kh * G + g for g in range(G)]
    q4 = jnp.concatenate([q_ref[:, h * Dh:(h + 1) * Dh] for h in heads], axis=0)
    slopes = jnp.concatenate([jnp.full((1, W), _alibi_slope(h), F32) for h in heads], axis=1)
    sinks = jnp.concatenate([jnp.broadcast_to(sink_ref[:, h:h + 1], (1, W)) for h in heads], axis=1)
    return heads, q4, slopes, sinks


def _swa_band_specs(W, nb, cols):
    prev = pl.BlockSpec((W, cols), lambda n: (jnp.maximum(jnp.minimum(n, nb - 1) - 1, 0), 0))
    cur = pl.BlockSpec((W, cols), lambda n: (jnp.minimum(n, nb - 1), 0))
    return prev, cur


def _swa_attn_fwd(q, k, v, sinks):
    S, NQ = q.shape
    NK = k.shape[1]
    W, Dh, G = WINDOW, SWA_HEAD_DIM, SWA_GROUP
    nb = S // W

    def body(q_ref, kp_ref, kc_ref, vp_ref, vc_ref, sink_ref, o_ref, lse_ref):
        distf, valid = _swa_geometry(pl.program_id(0))
        kband = jnp.concatenate([kp_ref[...], kc_ref[...]], axis=0)
        vband_t = jnp.concatenate([vp_ref[...], vc_ref[...]], axis=0).astype(F32).T.astype(BF16)
        outs = []
        for kh in range(SWA_KV_HEADS):
            kb = kband[:, kh * Dh:(kh + 1) * Dh]
            vbt = vband_t[kh * Dh:(kh + 1) * Dh, :]
            heads, q4, slopes, sinks = _swa_group(kh, q_ref, sink_ref)
            s = _dot_nt(kb, q4) * (Dh ** -0.5) - slopes * distf
            s = jnp.where(valid, s, -jnp.inf)
            m = jnp.maximum(jnp.max(s, axis=0, keepdims=True), sinks)
            p = jnp.exp(s - m)
            denom = jnp.sum(p, axis=0, keepdims=True) + jnp.exp(sinks - m)
            out4 = _dot(vbt, (p * (1.0 / denom)).astype(BF16))
            lse4 = m + jnp.log(denom)
            for g, h in enumerate(heads):
                outs.append(out4[:, g * W:(g + 1) * W])
                lse_ref[h:h + 1, :] = lse4[:, g * W:(g + 1) * W]
        o_ref[...] = jnp.concatenate(outs, axis=0).T.astype(BF16)

    kprev, kcur = _swa_band_specs(W, nb, NK)
    return pl.pallas_call(
        body, name="swa_attn_fwd", grid=(nb,),
        in_specs=[pl.BlockSpec((W, NQ), lambda n: (n, 0)), kprev, kcur, kprev, kcur,
                  pl.BlockSpec((1, SWA_HEADS), lambda n: (0, 0))],
        out_specs=[pl.BlockSpec((W, NQ), lambda n: (n, 0)), pl.BlockSpec((SWA_HEADS, W), lambda n: (0, n))],
        out_shape=[_sds((S, NQ), BF16), _sds((SWA_HEADS, S), F32)],
        compiler_params=pltpu.CompilerParams(dimension_semantics=("arbitrary",), vmem_limit_bytes=VMEM_LIMIT),
    )(q, k, k, v, v, sinks)


def _final_loss(x, target, g):
    S, D = x.shape

    def body(x_ref, t_ref, g_ref, dx_ref, loss_ref, dg_ref):
        i = pl.program_id(0)
        xv = x_ref[...]
        r = _rstd(xv)
        n = xv * r
        err = n * g_ref[...] - t_ref[...]
        part = 0.5 * jnp.sum(jnp.mean(err * err, axis=-1, keepdims=True), axis=0, keepdims=True)
        _acc(loss_ref, jnp.broadcast_to(part, loss_ref.shape), i)
        dout = err / D
        _acc(dg_ref, jnp.sum(dout * n, axis=0, keepdims=True), i)
        dx_ref[...] = _rms_bwd(dout * g_ref[...], n, r)

    return _rowcall("final_loss", body, S, ROW_TILE, [x, target], [g], [_sds((S, D), F32)],
                    [_sds((1, 128), F32), _sds((1, D), F32)])


def _mlp_bwd_a(dx, y, rl, gt, w2):
    S, D = dx.shape
    FF = rl.shape[1]

    def body(dx_ref, y_ref, rl_ref, gt_ref, w2_ref, dy_ref, du_ref, dgt_ref):
        i = pl.program_id(0)
        dxv = dx_ref[...]
        _acc(dgt_ref, jnp.sum(dxv * y_ref[...].astype(F32), axis=0, keepdims=True), i)
        dy = (dxv * gt_ref[...]).astype(BF16)
        dy_ref[...] = dy
        dact = _dot_nt(dy, w2_ref[...])
        du_ref[...] = (dact * (2.0 * rl_ref[...].astype(F32))).astype(BF16)

    return _rowcall("mlp_bwd_a", body, S, ROW_TILE_BWD, [dx, y, rl], [gt, w2],
                    [_sds((S, D), BF16), _sds((S, FF), BF16)], [_sds((1, D), F32)])


def _mlp_bwd_b(du, x, dx, w1, g, sc):
    S, D = x.shape

    def body(du_ref, x_ref, dx_ref, w1_ref, g_ref, sc_ref, dxo_ref, dsh_ref, da_ref):
        i = pl.program_id(0)
        dh = _dot_nt(du_ref[...], w1_ref[...])
        dxn, dsh, da = _modulate_bwd(dh, x_ref[...], g_ref[...], sc_ref[...])
        dxo_ref[...] = dx_ref[...] + dxn
        _acc(dsh_ref, dsh, i)
        _acc(da_ref, da, i)

    return _rowcall("mlp_bwd_b", body, S, ROW_TILE_BWD, [du, x, dx], [w1, g, sc],
                    [_sds((S, D), F32)], [_sds((1, D), F32), _sds((1, D), F32)])


def _attn_out_bwd(dx, y, o, gt, w_o, n_heads):
    S, D = dx.shape
    NO = o.shape[1]
    dh = NO // n_heads
    member = (jnp.arange(NO)[None, :] // dh == jnp.arange(16)[:, None]).astype(BF16)

    def body(dx_ref, y_ref, o_ref, gt_ref, wo_ref, mem_ref, dy_ref, do_ref, dl_ref, dgt_ref, dbo_ref):
        i = pl.program_id(0)
        dxv = dx_ref[...]
        _acc(dgt_ref, jnp.sum(dxv * y_ref[...].astype(F32), axis=0, keepdims=True), i)
        dy = dxv * gt_ref[...]
        _acc(dbo_ref, jnp.sum(dy, axis=0, keepdims=True), i)
        dyb = dy.astype(BF16)
        dy_ref[...] = dyb
        do = _dot_nt(dyb, wo_ref[...])
        do_ref[...] = do.astype(BF16)
        prod = do * o_ref[...].astype(F32)
        hi = prod.astype(BF16)
        lo = (prod - hi.astype(F32)).astype(BF16)
        dl_ref[...] = _dot_nt(mem_ref[...], hi) + _dot_nt(mem_ref[...], lo)

    tm = min(ROW_TILE, S)
    return _rowcall("attn_out_bwd", body, S, ROW_TILE, [dx, y, o], [gt, w_o, member],
                    [_sds((S, D), BF16), _sds((S, NO), BF16),
                     (_sds((16, S), F32), pl.BlockSpec((16, tm), lambda i: (0, i)))],
                    [_sds((1, D), F32), _sds((1, D), F32)])


def _mla_attn_bwd(q, k, v, do, lse, delta, t, gblks):
    H, S, DQ = q.shape
    DV = V_DIM
    t = min(t, S)
    nb = S // t
    scale = QK_DIM ** -0.5
    c2 = scale * LOG2E

    ng = len(gblks)

    def body(q_ref, k_ref, v_ref, do_ref, lse_ref, dl_ref, *rest):
        g_refs, (dq_ref, dk_ref, dv_ref), recv_refs = rest[:ng], rest[ng:ng + 3], rest[ng + 3:2 * ng + 3]
        dk_s, dv_s, s_buf, dp_buf, send_sems, recv_sems, local_sems = rest[2 * ng + 3:]
        hd, kj = pl.program_id(0), pl.program_id(1)

        def scatter():
            return _exchange_copies([lambda j, r=r: r.at[j] for r in g_refs], recv_refs, send_sems, recv_sems,
                                    local_sems)

        @pl.when((hd == 0) & (kj == 0))
        def _():
            _start_exchange(scatter())

        @pl.when(kj == 0)
        def _():
            dq_ref[...] = jnp.zeros_like(dq_ref)

        dk_s[...] = jnp.zeros_like(dk_s)
        dv_s[...] = jnp.zeros_like(dv_s)

        def products(i, slot):
            rows = pl.ds(pl.multiple_of(i * t, t), t)
            s_buf[slot] = _dot_nt(k_ref[0], q_ref[0, rows, :])
            dp_buf[slot] = _dot_nt(v_ref[0], do_ref[rows, :])

        def absorb(i, slot, diagonal):
            rows = pl.ds(pl.multiple_of(i * t, t), t)
            qb, dob = q_ref[0, rows, :], do_ref[rows, :]
            p = jnp.exp2(s_buf[slot] * c2 - lse_ref[0, i])
            if diagonal:
                key = lax.broadcasted_iota(jnp.int32, (t, t), 0)
                qry = lax.broadcasted_iota(jnp.int32, (t, t), 1)
                p = jnp.where(key <= qry, p, 0.0)
            dv_s[...] += _dot(p.astype(BF16), dob)
            ds = (p * (dp_buf[slot] - dl_ref[0, i])).astype(BF16)
            dk_s[...] += _dot(ds, qb)
            dq_ref[0, rows, :] += _dot_tn(ds, k_ref[0])

        n_off = nb - 1 - kj
        first = kj + 1

        def pair(i, carry):
            u = 2 * i
            products(first + u + 1, 1)
            absorb(first + u, 0, False)
            products(jnp.where(u + 2 < n_off, first + u + 2, kj), 0)
            absorb(first + u + 1, 1, False)
            return carry

        products(jnp.where(n_off > 0, first, kj), 0)
        lax.fori_loop(0, n_off // 2, pair, 0)

        @pl.when(n_off % 2 == 0)
        def _():
            absorb(kj, 0, True)

        @pl.when(n_off % 2 == 1)
        def _():
            products(kj, 1)
            absorb(nb - 1, 0, False)
            absorb(kj, 1, True)

        dk_ref[0] = (dk_s[...] * scale).astype(BF16)
        dv_ref[0] = dv_s[...].astype(BF16)

        @pl.when((hd == H - 1) & (kj == nb - 1))
        def _():
            _finish_exchange(scatter())

    rowspec = pl.BlockSpec((1, nb, 1, t), lambda h, j: (h, 0, 0, 0))
    outs = pl.pallas_call(
        body, name="mla_attn_bwd", grid=(H, nb),
        in_specs=[pl.BlockSpec((1, S, DQ), lambda h, j: (h, 0, 0)),
                  pl.BlockSpec((1, t, DQ), lambda h, j: (h, j, 0)),
                  pl.BlockSpec((1, t, DV), lambda h, j: (h, j, 0)),
                  pl.BlockSpec((S, DV), lambda h, j: (0, h)), rowspec, rowspec] + [ANY_SPEC] * ng,
        out_specs=[pl.BlockSpec((1, S, DQ), lambda h, j: (h, 0, 0)),
                   pl.BlockSpec((1, t, DQ), lambda h, j: (h, j, 0)),
                   pl.BlockSpec((1, t, DV), lambda h, j: (h, j, 0))] + [ANY_SPEC] * ng,
        out_shape=[_sds((H, S, DQ), F32), _sds((H, S, DQ), BF16), _sds((H, S, DV), BF16)]
        + [_sds(g.shape, g.dtype) for g in gblks],
        scratch_shapes=[pltpu.VMEM((t, DQ), F32), pltpu.VMEM((t, DV), F32), pltpu.VMEM((2, t, t), F32),
                        pltpu.VMEM((2, t, t), F32)] + _comm_sems(ng),
        compiler_params=pltpu.CompilerParams(dimension_semantics=("arbitrary", "arbitrary"),
                                             vmem_limit_bytes=VMEM_LIMIT),
    )(q, k, v, do, lse, delta, *gblks)
    return outs[0], outs[1], outs[2], outs[3:]


def _swa_attn_bwd(q, k, v, do, lse, delta, sinks):
    S, NQ = q.shape
    NK = k.shape[1]
    W, Dh, G = WINDOW, SWA_HEAD_DIM, SWA_GROUP
    nb = S // W

    def body(q_ref, kp_ref, kc_ref, vp_ref, vc_ref, do_ref, lse_ref, dl_ref, sink_ref,
             dq_ref, dk_ref, dv_ref, dsink_ref, dkc_s, dvc_s):
        n = pl.program_id(0)

        @pl.when(n == 0)
        def _():
            dkc_s[...] = jnp.zeros_like(dkc_s)
            dvc_s[...] = jnp.zeros_like(dvc_s)
            dsink_ref[...] = jnp.zeros_like(dsink_ref)

        @pl.when(n < nb)
        def _():
            distf, valid = _swa_geometry(n)
            kband = jnp.concatenate([kp_ref[...], kc_ref[...]], axis=0)
            vband = jnp.concatenate([vp_ref[...], vc_ref[...]], axis=0)
            kband_t = kband.astype(F32).T.astype(BF16)
            dq_t = []
            for kh in range(SWA_KV_HEADS):
                ck = slice(kh * Dh, (kh + 1) * Dh)
                kb, vb, kbt = kband[:, ck], vband[:, ck], kband_t[ck, :]
                heads, q4, slopes, sinks = _swa_group(kh, q_ref, sink_ref)
                do4 = jnp.concatenate([do_ref[:, h * Dh:(h + 1) * Dh] for h in heads], axis=0)
                lse4 = jnp.concatenate([lse_ref[h:h + 1, :] for h in heads], axis=1)
                dl4 = jnp.concatenate([dl_ref[h:h + 1, :] for h in heads], axis=1)
                s = _dot_nt(kb, q4) * (Dh ** -0.5) - slopes * distf
                p = jnp.where(valid, jnp.exp(s - lse4), 0.0)
                dvb = _dot(p.astype(BF16), do4)
                dp = _dot_nt(vb, do4)
                dsb = ((p * (dp - dl4)) * (Dh ** -0.5)).astype(BF16)
                dq4 = _dot(kbt, dsb)
                dkb = _dot(dsb, q4)
                dsk4 = jnp.exp(sinks - lse4) * dl4
                for g, h in enumerate(heads):
                    dq_t.append(dq4[:, g * W:(g + 1) * W])
                    dsink_ref[:, h:h + 1] += -jnp.sum(dsk4[:, g * W:(g + 1) * W], axis=1, keepdims=True)
                dk_ref[:, ck] = (dkc_s[:, ck] + dkb[:W]).astype(BF16)
                dv_ref[:, ck] = (dvc_s[:, ck] + dvb[:W]).astype(BF16)
                dkc_s[:, ck] = dkb[W:]
                dvc_s[:, ck] = dvb[W:]
            dq_ref[...] = jnp.concatenate(dq_t, axis=0).T.astype(BF16)

        @pl.when(n == nb)
        def _():
            dk_ref[...] = dkc_s[...].astype(BF16)
            dv_ref[...] = dvc_s[...].astype(BF16)

    kprev, kcur = _swa_band_specs(W, nb, NK)
    qspec = lambda cols: pl.BlockSpec((W, cols), lambda n: (jnp.minimum(n, nb - 1), 0))
    kvout = pl.BlockSpec((W, NK), lambda n: (jnp.maximum(n - 1, 0), 0))
    rowspec = pl.BlockSpec((SWA_HEADS, W), lambda n: (0, jnp.minimum(n, nb - 1)))
    return pl.pallas_call(
        body, name="swa_attn_bwd", grid=(nb + 1,),
        in_specs=[qspec(NQ), kprev, kcur, kprev, kcur, qspec(NQ), rowspec, rowspec,
                  pl.BlockSpec((1, SWA_HEADS), lambda n: (0, 0))],
        out_specs=[qspec(NQ), kvout, kvout, pl.BlockSpec((1, 128), lambda n: (0, 0))],
        out_shape=[_sds((S, NQ), BF16), _sds((S, NK), BF16), _sds((S, NK), BF16), _sds((1, 128), F32)],
        scratch_shapes=[pltpu.VMEM((W, NK), F32), pltpu.VMEM((W, NK), F32)],
        compiler_params=pltpu.CompilerParams(dimension_semantics=("arbitrary",), vmem_limit_bytes=VMEM_LIMIT),
    )(q, k, k, v, v, do, lse, delta, sinks)


def _swa_in_bwd(dq, dk, dv, x, dx, w_qkv, g, sc):
    S, D = x.shape
    N = w_qkv.shape[1]

    def body(dq_ref, dk_ref, dv_ref, x_ref, dx_ref, w_ref, g_ref, sc_ref, dqkv_ref, dxo_ref, db_ref, dsh_ref, da_ref):
        i = pl.program_id(0)
        dqkv = jnp.concatenate([dq_ref[...], dk_ref[...], dv_ref[...]], axis=1)
        dqkv_ref[...] = dqkv
        _acc(db_ref, jnp.sum(dqkv.astype(F32), axis=0, keepdims=True), i)
        dh = _dot_nt(dqkv, w_ref[...])
        dxn, dsh, da = _modulate_bwd(dh, x_ref[...], g_ref[...], sc_ref[...])
        dxo_ref[...] = dx_ref[...] + dxn
        _acc(dsh_ref, dsh, i)
        _acc(da_ref, da, i)

    return _rowcall("swa_in_bwd", body, S, ROW_TILE, [dq, dk, dv, x, dx], [w_qkv, g, sc],
                    [_sds((S, N), BF16), _sds((S, D), F32)],
                    [_sds((1, N), F32), _sds((1, D), F32), _sds((1, D), F32)])


def _mla_in_bwd(dq, dk, dv, cos, sin, cqp, ckvp, x, dx, w_uqx, g_q, w_ukv, g_kv, w_cat, g, sc):
    S, D = x.shape
    H = MLA_HEADS
    QL = g_q.shape[1]
    NX = w_uqx.shape[1]
    NC = w_cat.shape[1]

    def body(dq_ref, dk_ref, dv_ref, cos_ref, sin_ref, cqp_ref, ckvp_ref, x_ref, dx_ref,
             wuqx_ref, gq_ref, wukv_ref, gkv_ref, wcat_ref, g_ref, sc_ref,
             dqx_ref, dkv_ref, dcat_ref, dxo_ref, dgq_ref, dgkv_ref, dsh_ref, da_ref):
        i = pl.program_id(0)
        cs, sn = cos_ref[...], sin_ref[...]
        dkr = jnp.zeros(cs.shape, F32)
        for hd in range(H):
            b = hd * 256
            dqh = dq_ref[hd] * (QK_DIM ** -0.5)
            dqx_ref[:, b:b + QK_NOPE] = dqh[:, :QK_NOPE].astype(BF16)
            dqx_ref[:, b + 128:b + 192] = (dqh[:, QK_NOPE:] * cs).astype(BF16)
            dqx_ref[:, b + 192:b + 256] = (dqh[:, QK_NOPE:] * sn).astype(BF16)
            dkh = dk_ref[hd]
            dkv_ref[:, b:b + QK_NOPE] = dkh[:, :QK_NOPE]
            dkv_ref[:, b + 128:b + 256] = dv_ref[hd]
            dkr = dkr + dkh[:, QK_NOPE:].astype(F32)
        dcq = _dot_nt(dqx_ref[...], wuqx_ref[...])
        cqp = cqp_ref[...]
        rq = _rstd(cqp)
        nq = cqp * rq
        _acc(dgq_ref, jnp.sum(dcq * nq, axis=0, keepdims=True), i)
        dcqp = _rms_bwd(dcq * gq_ref[...], nq, rq)
        dckv = _dot_nt(dkv_ref[...], wukv_ref[...])
        ckvp = ckvp_ref[...]
        rk = _rstd(ckvp)
        nk = ckvp * rk
        _acc(dgkv_ref, jnp.sum(dckv * nk, axis=0, keepdims=True), i)
        dckvp = _rms_bwd(dckv * gkv_ref[...], nk, rk)
        dcat_ref[:, :QL] = dcqp.astype(BF16)
        dcat_ref[:, QL:QL + KV_LORA] = dckvp.astype(BF16)
        o = QL + KV_LORA
        dcat_ref[:, o:o + QK_ROPE] = (dkr * cs).astype(BF16)
        dcat_ref[:, o + QK_ROPE:o + 2 * QK_ROPE] = (dkr * sn).astype(BF16)
        dh = _dot_nt(dcat_ref[...], wcat_ref[...])
        dxn, dsh, da = _modulate_bwd(dh, x_ref[...], g_ref[...], sc_ref[...])
        dxo_ref[...] = dx_ref[...] + dxn
        _acc(dsh_ref, dsh, i)
        _acc(da_ref, da, i)

    return _rowcall("mla_in_bwd", body, S, ROW_TILE_WIDE, [dq, dk, dv, cos, sin, cqp, ckvp, x, dx],
                    [w_uqx, g_q, w_ukv, g_kv, w_cat, g, sc],
                    [_sds((S, NX), BF16), _sds((S, NX), BF16), _sds((S, NC), BF16), _sds((S, D), F32)],
                    [_sds((1, QL), F32), _sds((1, KV_LORA), F32), _sds((1, D), F32), _sds((1, D), F32)])


def _matmul_tn(name, a, b):
    S, K = a.shape
    N = b.shape[1]
    tk, tn, ts = min(K, 1024), min(N, 1024), min(S, TN_TOKENS)
    if N % tn:
        tn = 512 if N % 512 == 0 else (384 if N % 384 == 0 else 128)
    if K % tk:
        tk = 512 if K % 512 == 0 else (384 if K % 384 == 0 else 128)
    ns = S // ts

    def body(a_ref, b_ref, o_ref):
        _acc(o_ref, _dot_tn(a_ref[...], b_ref[...]), pl.program_id(2))

    return pl.pallas_call(
        body, name=name, grid=(K // tk, N // tn, ns),
        in_specs=[pl.BlockSpec((ts, tk), lambda i, j, s: (s, i)), pl.BlockSpec((ts, tn), lambda i, j, s: (s, j))],
        out_specs=pl.BlockSpec((tk, tn), lambda i, j, s: (i, j)),
        out_shape=_sds((K, N), F32),
        compiler_params=pltpu.CompilerParams(dimension_semantics=("parallel", "parallel", "arbitrary"),
                                             vmem_limit_bytes=VMEM_LIMIT),
    )(a, b)


def _silu(c):
    return c * jax.nn.sigmoid(c)


def _ada_fwd(c_all, w_ada):
    L, D, NC = w_ada.shape

    def body(c_ref, w_ref, o_ref):
        cond = _silu(c_ref[...]).astype(BF16)
        o_ref[0] = _dot(cond, w_ref[0].astype(BF16))

    return pl.pallas_call(
        body, name="ada_fwd", grid=(L,),
        in_specs=[pl.BlockSpec(c_all.shape, lambda l: (0, 0)), pl.BlockSpec((1, D, NC), lambda l: (l, 0, 0))],
        out_specs=pl.BlockSpec((1, N_DEV, NC), lambda l: (l, 0, 0)),
        out_shape=_sds((L, N_DEV, NC), F32),
        compiler_params=pltpu.CompilerParams(dimension_semantics=("arbitrary",), vmem_limit_bytes=VMEM_LIMIT),
    )(c_all, w_ada)


def _adamw(w, g, m, v):
    m = ADAM_B1 * m + (1.0 - ADAM_B1) * g
    v = ADAM_B2 * v + (1.0 - ADAM_B2) * (g * g)
    m_hat = m / (1.0 - ADAM_B1 ** ADAM_STEP)
    v_hat = v / (1.0 - ADAM_B2 ** ADAM_STEP)
    delta = -ADAM_LR * (m_hat / (jnp.sqrt(v_hat) + ADAM_EPS) + ADAM_WD * w)
    return delta, m, v


def _ada_bwd_adamw(c_all_t, dmod_cols, w, m, v):
    L, D, NC = w.shape
    tr = min(D, 256)

    def body(ct_ref, dm_ref, w_ref, m_ref, v_ref, g_ref, d_ref, mo_ref, vo_ref):
        cond_t = _silu(ct_ref[...])
        dm = dm_ref[0]
        g = cond_t[:, 0:1] * dm[0:1, :]
        for b in range(1, N_DEV):
            g = g + cond_t[:, b:b + 1] * dm[b:b + 1, :]
        g_ref[0] = g
        d_ref[0], mo_ref[0], vo_ref[0] = _adamw(w_ref[0], g, m_ref[0], v_ref[0])

    wspec = pl.BlockSpec((1, tr, NC), lambda l, r: (l, r, 0))
    return pl.pallas_call(
        body, name="ada_bwd_adamw", grid=(L, D // tr),
        in_specs=[pl.BlockSpec((tr, N_DEV), lambda l, r: (r, 0)),
                  pl.BlockSpec((1, N_DEV, NC), lambda l, r: (l, 0, 0)), wspec, wspec, wspec],
        out_specs=[wspec] * 4, out_shape=[_sds(w.shape, F32)] * 4,
        compiler_params=pltpu.CompilerParams(dimension_semantics=("parallel", "parallel"), vmem_limit_bytes=VMEM_LIMIT),
    )(c_all_t, dmod_cols, w, m, v)


def _sum_devices(x):
    def body(x_ref, o_ref):
        s = x_ref[0]
        for j in range(1, N_DEV):
            s = s + x_ref[j]
        o_ref[...] = s

    return pl.pallas_call(body, name="sum_devices", out_shape=_sds(x.shape[1:], F32))(x)


def _adamw_small(w, g, m, v):
    def body(w_ref, g_ref, m_ref, v_ref, d_ref, mo_ref, vo_ref):
        d_ref[...], mo_ref[...], vo_ref[...] = _adamw(w_ref[...], g_ref[...], m_ref[...], v_ref[...])

    return pl.pallas_call(body, name="adamw_small", out_shape=[_sds(w.shape, F32)] * 3)(w, g, m, v)


def _me():
    return lax.axis_index("x") * 4 + lax.axis_index("y") * 2 + lax.axis_index("c")


def _peer(k):
    x, y, c = lax.axis_index("x"), lax.axis_index("y"), lax.axis_index("c")
    px = 1 - x if k & 4 else x
    py = 1 - y if k & 2 else y
    pc = 1 - c if k & 1 else c
    return (px, py, pc), px * 4 + py * 2 + pc


VMEM_SPEC = pl.BlockSpec(memory_space=pltpu.VMEM)
ANY_SPEC = pl.BlockSpec(memory_space=pl.ANY)
def _comm_sems(n):
    return [pltpu.SemaphoreType.DMA((n * (N_DEV - 1),)), pltpu.SemaphoreType.DMA((n * (N_DEV - 1),)),
            pltpu.SemaphoreType.DMA((n,))]


def _exchange_copies(srcs_of, dst_refs, send_sems, recv_sems, local_sems):
    me = _me()
    local, sends, recvs = [], [], []
    for a, (src_of, dst_ref) in enumerate(zip(srcs_of, dst_refs)):
        local.append(pltpu.make_async_copy(src_of(me), dst_ref.at[me], local_sems.at[a]))
        for k in range(1, N_DEV):
            dev, pj = _peer(k)
            i = a * (N_DEV - 1) + k - 1
            sems = dict(send_sem=send_sems.at[i], recv_sem=recv_sems.at[i], device_id=dev, device_id_type=MESH_IDS)
            sends.append(pltpu.make_async_remote_copy(src_ref=src_of(pj), dst_ref=dst_ref.at[me], **sems))
            recvs.append(pltpu.make_async_remote_copy(src_ref=src_of(pj), dst_ref=dst_ref.at[pj], **sems))
    return local, sends, recvs


def _start_exchange(copies):
    local, sends, _ = copies
    for cp in local + sends:
        cp.start()


def _finish_exchange(copies):
    local, sends, recvs = copies
    for cp in recvs:
        cp.wait_recv()
    for cp in sends:
        cp.wait_send()
    for cp in local:
        cp.wait()


def _sum_adamw(recv, w, m, v):
    shape = w.shape
    C = shape[-1]
    R = w.size // C
    rows = max(d for d in range(16, min(R, 512) + 1, 16) if R % d == 0 and d * C <= 256 * 1024)

    def body(r_ref, w_ref, m_ref, v_ref, go_ref, d_ref, mo_ref, vo_ref):
        g = r_ref[0].astype(F32)
        for j in range(1, N_DEV):
            g = g + r_ref[j].astype(F32)
        go_ref[...] = g
        d_ref[...], mo_ref[...], vo_ref[...] = _adamw(w_ref[...], g, m_ref[...], v_ref[...])

    spec = pl.BlockSpec((rows, C), lambda i: (i, 0))
    outs = pl.pallas_call(
        body, name="sum_adamw", grid=(R // rows,),
        in_specs=[pl.BlockSpec((N_DEV, rows, C), lambda i: (0, i, 0)), spec, spec, spec],
        out_specs=[spec] * 4, out_shape=[_sds((R, C), F32)] * 4,
        compiler_params=pltpu.CompilerParams(dimension_semantics=("parallel",), vmem_limit_bytes=VMEM_LIMIT),
    )(recv.reshape(N_DEV, R, C), w.reshape(R, C), m.reshape(R, C), v.reshape(R, C))
    return [o.reshape(shape) for o in outs]


def _all_gather(name, x, out_dtype):
    R, C = x.shape
    cast = out_dtype != x.dtype

    def body(x_ref, out_ref, buf, send_sems, recv_sems, local_sem):
        me = _me()
        if cast:
            buf[...] = x_ref[...].astype(out_dtype)
            src = buf
        else:
            src = x_ref
        local = pltpu.make_async_copy(src, out_ref.at[me], local_sem)
        local.start()
        sends = []
        for k in range(1, N_DEV):
            dev, _ = _peer(k)
            cp = pltpu.make_async_remote_copy(src_ref=src, dst_ref=out_ref.at[me], send_sem=send_sems.at[k - 1],
                                              recv_sem=recv_sems.at[k - 1], device_id=dev, device_id_type=MESH_IDS)
            cp.start()
            sends.append(cp)
        for k in range(1, N_DEV):
            dev, pj = _peer(k)
            pltpu.make_async_remote_copy(src_ref=src, dst_ref=out_ref.at[pj], send_sem=send_sems.at[k - 1],
                                         recv_sem=recv_sems.at[k - 1], device_id=dev, device_id_type=MESH_IDS).wait_recv()
        for cp in sends:
            cp.wait_send()
        local.wait()

    return pl.pallas_call(
        body, name=name, in_specs=[VMEM_SPEC], out_specs=ANY_SPEC, out_shape=_sds((N_DEV, R, C), out_dtype),
        scratch_shapes=[pltpu.VMEM((R, C) if cast else (8, 128), out_dtype),
                        pltpu.SemaphoreType.DMA((N_DEV - 1,)), pltpu.SemaphoreType.DMA((N_DEV - 1,)),
                        pltpu.SemaphoreType.DMA(())],
        compiler_params=pltpu.CompilerParams(vmem_limit_bytes=VMEM_LIMIT),
    )(x)


def _all_to_all(name, x):
    _, R, C = x.shape

    def body(x_ref, out_ref, send_sems, recv_sems, local_sem):
        me = _me()
        local = pltpu.make_async_copy(x_ref.at[me], out_ref.at[me], local_sem)
        local.start()
        sends = []
        for k in range(1, N_DEV):
            dev, pj = _peer(k)
            cp = pltpu.make_async_remote_copy(src_ref=x_ref.at[pj], dst_ref=out_ref.at[me], send_sem=send_sems.at[k - 1],
                                              recv_sem=recv_sems.at[k - 1], device_id=dev, device_id_type=MESH_IDS)
            cp.start()
            sends.append(cp)
        for k in range(1, N_DEV):
            dev, pj = _peer(k)
            pltpu.make_async_remote_copy(src_ref=x_ref.at[pj], dst_ref=out_ref.at[pj], send_sem=send_sems.at[k - 1],
                                         recv_sem=recv_sems.at[k - 1], device_id=dev, device_id_type=MESH_IDS).wait_recv()
        for cp in sends:
            cp.wait_send()
        local.wait()

    return pl.pallas_call(
        body, name=name, in_specs=[VMEM_SPEC], out_specs=VMEM_SPEC, out_shape=_sds(x.shape, x.dtype),
        scratch_shapes=[pltpu.SemaphoreType.DMA((N_DEV - 1,)), pltpu.SemaphoreType.DMA((N_DEV - 1,)),
                        pltpu.SemaphoreType.DMA(())],
    )(x)


def _reduce_scatter_adamw(name, gblk, w, m, v):
    _, R, C = gblk.shape
    rows = 8
    for cand in (136, 128, 80, 64, 40, 32, 16, 8):
        if R % cand == 0:
            rows = cand
            break

    def body(g_ref, w_ref, m_ref, v_ref, go_ref, d_ref, mo_ref, vo_ref, recv, send_sems, recv_sems, local_sem):
        me = _me()
        local = pltpu.make_async_copy(g_ref.at[me], recv.at[me], local_sem)
        local.start()
        sends = []
        for k in range(1, N_DEV):
            dev, pj = _peer(k)
            cp = pltpu.make_async_remote_copy(src_ref=g_ref.at[pj], dst_ref=recv.at[me], send_sem=send_sems.at[k - 1],
                                              recv_sem=recv_sems.at[k - 1], device_id=dev, device_id_type=MESH_IDS)
            cp.start()
            sends.append(cp)
        for k in range(1, N_DEV):
            dev, pj = _peer(k)
            pltpu.make_async_remote_copy(src_ref=g_ref.at[pj], dst_ref=recv.at[pj], send_sem=send_sems.at[k - 1],
                                         recv_sem=recv_sems.at[k - 1], device_id=dev, device_id_type=MESH_IDS).wait_recv()
        local.wait()

        def chunk(i, carry):
            r = pl.ds(pl.multiple_of(i * rows, rows), rows)
            g = recv[0, r, :].astype(F32)
            for j in range(1, N_DEV):
                g = g + recv[j, r, :].astype(F32)
            go_ref[r, :] = g
            d_ref[r, :], mo_ref[r, :], vo_ref[r, :] = _adamw(w_ref[r, :], g, m_ref[r, :], v_ref[r, :])
            return carry

        lax.fori_loop(0, R // rows, chunk, 0)
        for cp in sends:
            cp.wait_send()

    return pl.pallas_call(
        body, name=name, in_specs=[ANY_SPEC, VMEM_SPEC, VMEM_SPEC, VMEM_SPEC], out_specs=[VMEM_SPEC] * 4,
        out_shape=[_sds((R, C), F32)] * 4,
        scratch_shapes=[pltpu.VMEM((N_DEV, R, C), BF16), pltpu.SemaphoreType.DMA((N_DEV - 1,)),
                        pltpu.SemaphoreType.DMA((N_DEV - 1,)), pltpu.SemaphoreType.DMA(())],
        compiler_params=pltpu.CompilerParams(vmem_limit_bytes=VMEM_LIMIT),
    )(gblk, w, m, v)


FIRST_WEIGHTS = ["mla_w_dq", "mla_w_uq", "mla_w_dkv", "mla_w_ukv"]
LATE_WEIGHTS = ["mla_w_o", "swa_w_qkv", "swa_w_o", "w_ff1", "w_ff2"]
ROW_SHARDED = {"mla_w_dq", "mla_w_dkv", "mla_w_o", "swa_w_o", "w_ff2"}


def _unblock(name, blocks):
    sh = blocks.shape[1:]
    if name in ROW_SHARDED:
        return jnp.moveaxis(blocks, 0, 1).reshape(sh[0], N_DEV * sh[1], sh[2])
    return jnp.moveaxis(blocks, 0, 2).reshape(sh[0], sh[1], N_DEV * sh[2])


def _block(name, full):
    L, K, N = full.shape
    if name in ROW_SHARDED:
        return jnp.moveaxis(full.reshape(L, N_DEV, K // N_DEV, N), 1, 0)
    return jnp.moveaxis(full.reshape(L, K, N_DEV, N // N_DEV), 2, 0)


def _rot_cols(w):
    half = QK_ROPE // 2
    return jnp.concatenate([-w[..., half:], w[..., :half]], axis=-1)


def _unrot_cols(gw):
    half = QK_ROPE // 2
    return jnp.concatenate([gw[..., half:], -gw[..., :half]], axis=-1)


def _row(v):
    return v.reshape(1, -1)


def _mlp_block_bwd(dx, sv, w1, w2, g, sc, gt):
    dy, du, dgt = _mlp_bwd_a(dx, sv["y2"], sv["rl"], gt, w2)
    dw2 = _matmul_tn("dw_ff2", sv["act"], dy)
    dw1 = _matmul_tn("dw_ff1", sv["h2"], du)
    dxo, dsh, da = _mlp_bwd_b(du, sv["x1"], dx, w1, g, sc)
    return dxo, dw1, dw2, dsh, da, dgt


def kernel(x, c, positions, w_ada, b_ada, g_mix, g_mlp, mla_w_dq, mla_g_q, mla_w_uq, mla_w_dkv, mla_g_kv, mla_w_ukv, mla_w_o, swa_w_qkv, swa_b_qkv, swa_sinks, swa_w_o, swa_b_o, w_ff1, w_ff2, g_final, loss_target, m_w_ada, m_b_ada, m_g_mix, m_g_mlp, m_mla_w_dq, m_mla_g_q, m_mla_w_uq, m_mla_w_dkv, m_mla_g_kv, m_mla_w_ukv, m_mla_w_o, m_swa_w_qkv, m_swa_b_qkv, m_swa_sinks, m_swa_w_o, m_swa_b_o, m_w_ff1, m_w_ff2, m_g_final, v_w_ada, v_b_ada, v_g_mix, v_g_mlp, v_mla_w_dq, v_mla_g_q, v_mla_w_uq, v_mla_w_dkv, v_mla_g_kv, v_mla_w_ukv, v_mla_w_o, v_swa_w_qkv, v_swa_b_qkv, v_swa_sinks, v_swa_w_o, v_swa_b_o, v_w_ff1, v_w_ff2, v_g_final):
    S, D = x.shape[1], x.shape[2]
    me = _me()
    x0 = x[0]
    target = loss_target[0]
    big_w = dict(mla_w_dq=mla_w_dq, mla_w_uq=mla_w_uq, mla_w_dkv=mla_w_dkv, mla_w_ukv=mla_w_ukv, mla_w_o=mla_w_o,
                 swa_w_qkv=swa_w_qkv, swa_w_o=swa_w_o, w_ff1=w_ff1, w_ff2=w_ff2)
    big_m = dict(mla_w_dq=m_mla_w_dq, mla_w_uq=m_mla_w_uq, mla_w_dkv=m_mla_w_dkv, mla_w_ukv=m_mla_w_ukv,
                 mla_w_o=m_mla_w_o, swa_w_qkv=m_swa_w_qkv, swa_w_o=m_swa_w_o, w_ff1=m_w_ff1, w_ff2=m_w_ff2)
    big_v = dict(mla_w_dq=v_mla_w_dq, mla_w_uq=v_mla_w_uq, mla_w_dkv=v_mla_w_dkv, mla_w_ukv=v_mla_w_ukv,
                 mla_w_o=v_mla_w_o, swa_w_qkv=v_swa_w_qkv, swa_w_o=v_swa_w_o, w_ff1=v_w_ff1, w_ff2=v_w_ff2)
    groups = {"first": FIRST_WEIGHTS, "late": LATE_WEIGHTS}
    wrows = {n: -(-big_w[n].size // (PACK_COLS * 16)) * 16 for n in FIRST_WEIGHTS + LATE_WEIGHTS}
    offs = {g: np.concatenate([[0], np.cumsum([wrows[n] for n in names])]).astype(int) for g, names in groups.items()}

    def as_rows(n, a, lead=()):
        flat = a.reshape(lead + (-1,))
        pad = wrows[n] * PACK_COLS - flat.shape[-1]
        if pad:
            flat = jnp.pad(flat, ((0, 0),) * len(lead) + ((0, pad),))
        return flat.reshape(lead + (wrows[n], PACK_COLS))

    def pack(g, d):
        return jnp.concatenate([as_rows(n, d[n]) for n in groups[g]], axis=0)

    def pack_blocks(g, gfull):
        return jnp.concatenate([as_rows(n, _block(n, gfull[n]).astype(BF16), (N_DEV,)) for n in groups[g]], axis=1)

    def unpack(g, packed, lead=()):
        out = {}
        for i, n in enumerate(groups[g]):
            part = packed[..., int(offs[g][i]):int(offs[g][i + 1]), :].reshape(lead + (-1,))
            out[n] = part[..., :big_w[n].size].reshape(lead + big_w[n].shape)
        return out

    gathered = _all_gather("gather_weights", pack("first", big_w), BF16)
    wfull = {n: _unblock(n, b) for n, b in unpack("first", gathered, (N_DEV,)).items()}
    w_dq, w_dkv = wfull["mla_w_dq"][0], wfull["mla_w_dkv"][0]
    w_cat = jnp.concatenate([w_dq, w_dkv, _rot_cols(w_dkv[:, KV_LORA:])], axis=1)
    QL = w_dq.shape[1]
    w_uq = wfull["mla_w_uq"][0].reshape(QL, MLA_HEADS, QK_DIM)
    w_uqx = jnp.concatenate([w_uq, _rot_cols(w_uq[..., QK_NOPE:])], axis=-1).reshape(QL, MLA_HEADS * 256)
    w_ukv = wfull["mla_w_ukv"][0]

    L = w_ada.shape[0]
    NC = w_ada.shape[2]
    nbq, nbo = swa_b_qkv.shape[1], swa_b_o.shape[1]
    cpad = -(-(D + nbq + nbo) // 1024) * 1024
    cpack = jnp.pad(jnp.concatenate([c[0], swa_b_qkv[0], swa_b_o[0]]), (0, cpad - (D + nbq + nbo))).reshape(8, cpad // 8)
    call = _all_gather("gather_c", cpack, F32).reshape(N_DEV, cpad)
    c_all = call[:, :D]
    b_qkv_full = call[:, D:D + nbq].reshape(1, N_DEV * nbq)
    b_o_full = call[:, D + nbq:D + nbq + nbo].reshape(1, N_DEV * nbo)
    mod_cols = _ada_fwd(c_all, w_ada)
    mpad = -(-(L * NC) // 1024) * 1024
    mod_send = jnp.pad(jnp.moveaxis(mod_cols, 1, 0).reshape(N_DEV, L * NC), ((0, 0), (0, mpad - L * NC)))
    mod_mine = _all_to_all("exchange_mod", mod_send.reshape(N_DEV, 8, mpad // 8)).reshape(N_DEV, mpad)[:, :L * NC]
    mod = jnp.moveaxis(mod_mine.reshape(N_DEV, L, NC), 0, 1).reshape(L, N_DEV * NC) + b_ada
    mods = mod.reshape(L, 6, 1, D)

    half = QK_ROPE // 2
    inv_freq = ROPE_THETA ** (-jnp.arange(half, dtype=F32) / half)
    ang = positions[0].astype(F32)[:, None] * inv_freq
    cos = jnp.concatenate([jnp.cos(ang), jnp.cos(ang)], axis=-1)
    sin = jnp.concatenate([jnp.sin(ang), jnp.sin(ang)], axis=-1)

    T_ATT = ATT_TILE
    zero_bias = jnp.zeros((1, D), F32)

    sh1, sc1, gt1, sh2, sc2, gt2 = [mods[0, i] for i in range(6)]
    gm0, gp0 = _row(g_mix[0]), _row(g_mlp[0])
    h1, cqp, cq, ckvp, ckv, q, k, v, vt = _mla_in_fwd(x0, cos, sin, gm0, sc1, sh1, w_cat, mla_g_q, w_uqx, mla_g_kv,
                                                      w_ukv, ROW_TILE)
    o0, lse0, gathered = _mla_attn_fwd(q, k, vt, ATT_TILE_FWD, [big_w[n].astype(BF16) for n in LATE_WEIGHTS])
    wfull = {n: _unblock(n, b) for n, b in zip(LATE_WEIGHTS, gathered)}
    w_o_mla, w_qkv, w_o_swa = wfull["mla_w_o"][0], wfull["swa_w_qkv"][0], wfull["swa_w_o"][0]
    ff1, ff2 = wfull["w_ff1"], wfull["w_ff2"]
    y1, x1, h2 = _attn_out_fwd(o0, x0, w_o_mla, zero_bias, gt1, gp0, sc2, sh2)
    rl0, act0, y2, x2 = _mlp_fwd(h2, x1, ff1[0], ff2[0], gt2)
    sv0 = dict(y2=y2, rl=rl0, act=act0, h2=h2, x1=x1)

    th1, tc1, tg1, th2, tc2, tg2 = [mods[1, i] for i in range(6)]
    gm1, gp1 = _row(g_mix[1]), _row(g_mlp[1])
    h3, sq, sk, svv = _swa_in_fwd(x2, gm1, tc1, th1, w_qkv, b_qkv_full)
    o1, lse1 = _swa_attn_fwd(sq, sk, svv, swa_sinks)
    y3, x3, h4 = _attn_out_fwd(o1, x2, w_o_swa, b_o_full, tg1, gp1, tc2, th2)
    rl1, act1, y4, x4 = _mlp_fwd(h4, x3, ff1[1], ff2[1], tg2)
    sv1 = dict(y2=y4, rl=rl1, act=act1, h2=h4, x1=x3)
    dx4, loss_part, dg_final = _final_loss(x4, target, _row(g_final))

    dx3, dw1_1, dw2_1, dsh2_1, da2_1, dgt2_1 = _mlp_block_bwd(dx4, sv1, ff1[1], ff2[1], gp1, tc2, tg2)
    dy, do, dl, dgt1_1, db_o = _attn_out_bwd(dx3, y3, o1, tg1, w_o_swa, SWA_HEADS)
    dw_o_swa = _matmul_tn("dw_o", o1, dy)
    dsq, dsk, dsv, dsink = _swa_attn_bwd(sq, sk, svv, do, lse1, dl, swa_sinks)
    dqkv, dx2, db_qkv, dsh1_1, da1_1 = _swa_in_bwd(dsq, dsk, dsv, x2, dx3, w_qkv, gm1, tc1)
    dw_qkv = _matmul_tn("dw_qkv", h3, dqkv)

    dx1, dw1_0, dw2_0, dsh2_0, da2_0, dgt2_0 = _mlp_block_bwd(dx2, sv0, ff1[0], ff2[0], gp0, sc2, gt2)
    dy, do, dl, dgt1_0, _ = _attn_out_bwd(dx1, y1, o0, gt1, w_o_mla, MLA_HEADS)
    dw_o_mla = _matmul_tn("dw_o", o0, dy)
    tb = min(T_ATT, S)
    delta = dl[:MLA_HEADS].reshape(MLA_HEADS, S // tb, 1, tb)
    glate = dict(mla_w_o=dw_o_mla[None], swa_w_qkv=dw_qkv[None], swa_w_o=dw_o_swa[None],
                 w_ff1=jnp.stack([dw1_0, dw1_1]), w_ff2=jnp.stack([dw2_0, dw2_1]))
    lse_rows = (lse0 * LOG2E).reshape(MLA_HEADS, S // tb, 1, tb)
    dq, dk, dv, recv = _mla_attn_bwd(q, k, v, do, lse_rows, delta, T_ATT,
                                     [_block(n, glate[n]).astype(BF16) for n in LATE_WEIGHTS])
    late = {n: _sum_adamw(r, big_w[n], big_m[n], big_v[n]) for n, r in zip(LATE_WEIGHTS, recv)}
    dqx, dkv, dcat, dx0, dg_q, dg_kv, dsh1_0, da1_0 = _mla_in_bwd(
        dq, dk, dv, cos, sin, cqp, ckvp, x0, dx1, w_uqx, mla_g_q, w_ukv, mla_g_kv, w_cat, gm0, sc1)
    dw_uqx = _matmul_tn("dw_uq", cq, dqx).reshape(QL, MLA_HEADS, 256)
    dw_ukv = _matmul_tn("dw_ukv", ckv, dkv)
    dw_cat = _matmul_tn("dw_down", h1, dcat)
    dw_uq = jnp.concatenate([dw_uqx[..., :QK_NOPE], dw_uqx[..., 128:192] + _unrot_cols(dw_uqx[..., 192:256])],
                            axis=-1).reshape(QL, MLA_HEADS * QK_DIM)
    o_kr = QL + KV_LORA
    dw_dkv = jnp.concatenate([dw_cat[:, QL:o_kr],
                              dw_cat[:, o_kr:o_kr + QK_ROPE] + _unrot_cols(dw_cat[:, o_kr + QK_ROPE:])], axis=1)

    gfirst = dict(mla_w_dq=dw_cat[None, :, :QL], mla_w_uq=dw_uq[None], mla_w_dkv=dw_dkv[None], mla_w_ukv=dw_ukv[None])
    first = _reduce_scatter_adamw("grad_exchange_adamw", pack_blocks("first", gfirst), pack("first", big_w),
                                  pack("first", big_m), pack("first", big_v))
    big_g, big_d, big_nm, big_nv = ({**unpack("first", first[j]), **{n: late[n][j] for n in LATE_WEIGHTS}}
                                    for j in range(4))

    dmod = jnp.stack([
        jnp.concatenate([dsh1_0, gm0 * da1_0, dgt1_0, dsh2_0, gp0 * da2_0, dgt2_0], axis=1),
        jnp.concatenate([dsh1_1, gm1 * da1_1, dgt1_1, dsh2_1, gp1 * da2_1, dgt2_1], axis=1)]).reshape(-1)
    dg_mix = jnp.concatenate([(1.0 + sc1) * da1_0, (1.0 + tc1) * da1_1], axis=1).reshape(-1)
    dg_mlp = jnp.concatenate([(1.0 + sc2) * da2_0, (1.0 + tc2) * da2_1], axis=1).reshape(-1)
    parts = [loss_part.reshape(-1), dmod, dg_mix, dg_mlp, dg_q.reshape(-1), dg_kv.reshape(-1), dsink.reshape(-1),
             dg_final.reshape(-1), db_qkv.reshape(-1), db_o.reshape(-1)]
    soffs = np.concatenate([[0], np.cumsum([p.size for p in parts])])
    spad = -(-int(soffs[-1]) // 1024) * 1024
    spack = jnp.pad(jnp.concatenate(parts), (0, spad - int(soffs[-1]))).reshape(8, spad // 8)
    sall = _all_gather("gather_small_grads", spack, F32)
    ssum = _sum_devices(sall).reshape(-1)
    tot = [ssum[int(soffs[i]):int(soffs[i + 1])] for i in range(len(parts))]
    loss = tot[0][0]
    nsink = swa_sinks.shape[1]
    small_g = dict(b_ada=tot[1].reshape(b_ada.shape), g_mix=tot[2].reshape(g_mix.shape), g_mlp=tot[3].reshape(g_mlp.shape),
                   mla_g_q=tot[4].reshape(mla_g_q.shape), mla_g_kv=tot[5].reshape(mla_g_kv.shape),
                   swa_sinks=tot[6][:nsink].reshape(swa_sinks.shape), g_final=tot[7].reshape(g_final.shape),
                   swa_b_qkv=lax.dynamic_slice(tot[8], (me * nbq,), (nbq,)).reshape(swa_b_qkv.shape),
                   swa_b_o=lax.dynamic_slice(tot[9], (me * nbo,), (nbo,)).reshape(swa_b_o.shape))
    small_w = dict(b_ada=b_ada, g_mix=g_mix, g_mlp=g_mlp, mla_g_q=mla_g_q, mla_g_kv=mla_g_kv, swa_sinks=swa_sinks,
                   g_final=g_final, swa_b_qkv=swa_b_qkv, swa_b_o=swa_b_o)
    small_m = dict(b_ada=m_b_ada, g_mix=m_g_mix, g_mlp=m_g_mlp, mla_g_q=m_mla_g_q, mla_g_kv=m_mla_g_kv,
                   swa_sinks=m_swa_sinks, g_final=m_g_final, swa_b_qkv=m_swa_b_qkv, swa_b_o=m_swa_b_o)
    small_v = dict(b_ada=v_b_ada, g_mix=v_g_mix, g_mlp=v_g_mlp, mla_g_q=v_mla_g_q, mla_g_kv=v_mla_g_kv,
                   swa_sinks=v_swa_sinks, g_final=v_g_final, swa_b_qkv=v_swa_b_qkv, swa_b_o=v_swa_b_o)
    SMALL = list(small_w)
    woffs = np.concatenate([[0], np.cumsum([small_w[n].size for n in SMALL])])
    wpad = -(-int(woffs[-1]) // 1024) * 1024

    def spack_of(d):
        flat = jnp.concatenate([d[n].reshape(-1) for n in SMALL])
        return jnp.pad(flat, (0, wpad - int(woffs[-1]))).reshape(8, wpad // 8)

    sm = _adamw_small(spack_of(small_w), spack_of(small_g), spack_of(small_m), spack_of(small_v))
    small_d, small_nm, small_nv = (
        {n: a.reshape(-1)[int(woffs[i]):int(woffs[i + 1])].reshape(small_w[n].shape) for i, n in enumerate(SMALL)}
        for a in sm)

    b_off = int(soffs[1])
    dmod_all = sall.reshape(N_DEV, -1)[:, b_off:b_off + L * N_DEV * NC].reshape(N_DEV, L, N_DEV * NC)
    dmod_cols = jnp.moveaxis(lax.dynamic_slice_in_dim(dmod_all, me * NC, NC, axis=2), 0, 1)
    ada_g, ada_d, ada_nm, ada_nv = _ada_bwd_adamw(c_all.T, dmod_cols, w_ada, m_w_ada, v_w_ada)

    order = ["w_ada", "b_ada", "g_mix", "g_mlp", "mla_w_dq", "mla_g_q", "mla_w_uq", "mla_w_dkv", "mla_g_kv",
             "mla_w_ukv", "mla_w_o", "swa_w_qkv", "swa_b_qkv", "swa_sinks", "swa_w_o", "swa_b_o", "w_ff1", "w_ff2", "g_final"]

    def collect(ada, big, small):
        return [ada if n == "w_ada" else (big[n] if n in big else small[n]) for n in order]

    return (loss, dx0.reshape(x.shape), *collect(ada_g, big_g, small_g), *collect(ada_d, big_d, small_d),
            *collect(ada_nm, big_nm, small_nm), *collect(ada_nv, big_nv, small_nv))
```

```python
import functools

import jax
import jax.numpy as jnp
import numpy as np
from jax import lax
from jax.experimental import pallas as pl
from jax.experimental.pallas import tpu as pltpu

F32 = jnp.float32
BF16 = jnp.bfloat16
MESH_IDS = pl.DeviceIdType.MESH
N_DEV = 8

MLA_HEADS = 8
QK_NOPE = 128
QK_ROPE = 64
QK_DIM = QK_NOPE + QK_ROPE
V_DIM = 128
KV_LORA = 256
ROPE_THETA = 10000.0
SWA_HEADS = 16
SWA_KV_HEADS = 4
SWA_GROUP = SWA_HEADS // SWA_KV_HEADS
SWA_HEAD_DIM = 64
WINDOW = 128
EPS = 1e-6
LOG2E = 1.4426950408889634

ADAM_LR = 0.001
ADAM_B1 = 0.9
ADAM_B2 = 0.999
ADAM_EPS = 1e-08
ADAM_WD = 0.01
ADAM_STEP = 10

PACK_COLS = 1024
RS_CHUNKS = 4
VMEM_LIMIT = 56 << 20
ROW_TILE = 512
ROW_TILE_WIDE = 256
ROW_TILE_BWD = 512
ATT_TILE = 512
ATT_TILE_FWD = 1024
TN_TOKENS = 2048


def _dot(a, b):
    return jnp.dot(a, b, preferred_element_type=F32)


def _dot_nt(a, b):
    return lax.dot_general(a, b, (((1,), (1,)), ((), ())), preferred_element_type=F32)


def _dot_tn(a, b):
    return lax.dot_general(a, b, (((0,), (0,)), ((), ())), preferred_element_type=F32)


def _rstd(x):
    return lax.rsqrt(jnp.mean(x * x, axis=-1, keepdims=True) + EPS)


def _rms_bwd(dn, n, r):
    return r * (dn - n * jnp.mean(dn * n, axis=-1, keepdims=True))


def _modulate(x, g, sc, sh):
    r = _rstd(x)
    return ((x * r) * g) * (1.0 + sc) + sh


def _modulate_bwd(dh, x, g, sc):
    r = _rstd(x)
    n = x * r
    dsh = jnp.sum(dh, axis=0, keepdims=True)
    da = jnp.sum(dh * n, axis=0, keepdims=True)
    dx = _rms_bwd(dh * (g * (1.0 + sc)), n, r)
    return dx, dsh, da


def _first(i):
    return i == 0


def _acc(ref, val, i):
    @pl.when(i == 0)
    def _():
        ref[...] = val

    @pl.when(i != 0)
    def _():
        ref[...] += val


def _row_spec(shape, tm):
    nd = len(shape)
    return pl.BlockSpec(tuple(shape[:nd - 2]) + (tm, shape[-1]), lambda i: (0,) * (nd - 2) + (i, 0))


def _resident_spec(shape, single_buffer):
    nd = len(shape)
    if single_buffer:
        return pl.BlockSpec(tuple(shape), lambda i: (0,) * nd, pipeline_mode=pl.Buffered(1))
    return pl.BlockSpec(tuple(shape), lambda i: (0,) * nd)


def _rowcall(name, body, tokens, tm, row_in, full_in, row_out, acc_out=()):
    tm = min(tm, tokens)
    in_specs = [_row_spec(a.shape, tm) for a in row_in] + [_resident_spec(a.shape, True) for a in full_in]
    row_specs = [s[1] if isinstance(s, tuple) else _row_spec(s.shape, tm) for s in row_out]
    row_out = [s[0] if isinstance(s, tuple) else s for s in row_out]
    out_specs = row_specs + [_resident_spec(s.shape, False) for s in acc_out]
    return pl.pallas_call(
        body, name=name, grid=(tokens // tm,), in_specs=in_specs, out_specs=out_specs,
        out_shape=list(row_out) + list(acc_out),
        compiler_params=pltpu.CompilerParams(dimension_semantics=("arbitrary",), vmem_limit_bytes=VMEM_LIMIT),
    )(*row_in, *full_in)


def _sds(shape, dtype):
    return jax.ShapeDtypeStruct(tuple(shape), dtype)


def _mla_in_fwd(x, cos, sin, g, sc, sh, w_cat, g_q, w_uqx, g_kv, w_ukv, t):
    S, D = x.shape
    QL = g_q.shape[1]
    H = MLA_HEADS
    t = min(t, S)

    def body(x_ref, cos_ref, sin_ref, g_ref, sc_ref, sh_ref, wcat_ref, gq_ref, wuqx_ref, gkv_ref, wukv_ref,
             h_ref, cqp_ref, cq_ref, ckvp_ref, ckv_ref, q_ref, k_ref, v_ref, vt_ref):
        cs, sn = cos_ref[...], sin_ref[...]
        hb = _modulate(x_ref[...], g_ref[...], sc_ref[...], sh_ref[...]).astype(BF16)
        h_ref[...] = hb
        low = _dot(hb, wcat_ref[...])
        cqp = low[:, :QL]
        cqp_ref[...] = cqp
        cq = ((cqp * _rstd(cqp)) * gq_ref[...]).astype(BF16)
        cq_ref[...] = cq
        ckvp = low[:, QL:QL + KV_LORA]
        ckvp_ref[...] = ckvp
        ckv = ((ckvp * _rstd(ckvp)) * gkv_ref[...]).astype(BF16)
        ckv_ref[...] = ckv
        o = QL + KV_LORA
        kr = (low[:, o:o + QK_ROPE] * cs + low[:, o + QK_ROPE:o + 2 * QK_ROPE] * sn).astype(BF16)
        qx = _dot(cq, wuqx_ref[...])
        kv = _dot(ckv, wukv_ref[...])
        for hd in range(H):
            b = hd * 256
            q_ref[hd, :, 0:QK_NOPE] = qx[:, b:b + QK_NOPE].astype(BF16)
            q_ref[hd, :, QK_NOPE:QK_DIM] = (qx[:, b + 128:b + 192] * cs + qx[:, b + 192:b + 256] * sn).astype(BF16)
            k_ref[hd, :, 0:QK_NOPE] = kv[:, b:b + QK_NOPE].astype(BF16)
            k_ref[hd, :, QK_NOPE:QK_DIM] = kr
            vh = kv[:, b + 128:b + 256]
            v_ref[hd] = vh.astype(BF16)
            vt_ref[hd, 0, 0:V_DIM, :] = vh.T.astype(BF16)
            vt_ref[hd, 0, V_DIM:2 * V_DIM, :] = jnp.ones((V_DIM, x_ref.shape[0]), BF16)

    vt_spec = pl.BlockSpec((H, 1, 2 * V_DIM, t), lambda i: (0, i, 0, 0))
    return _rowcall(
        "mla_in_fwd", body, S, t, [x, cos, sin], [g, sc, sh, w_cat, g_q, w_uqx, g_kv, w_ukv],
        [_sds((S, D), BF16), _sds((S, QL), F32), _sds((S, QL), BF16), _sds((S, KV_LORA), F32), _sds((S, KV_LORA), BF16),
         _sds((H, S, QK_DIM), BF16), _sds((H, S, QK_DIM), BF16), _sds((H, S, V_DIM), BF16),
         (_sds((H, S // t, 2 * V_DIM, t), BF16), vt_spec)])


def _mla_attn_fwd(q, k, vt, t, sends):
    H, S, DQ = q.shape
    DV = V_DIM
    vb = vt.shape[-1]
    t = max(min(t, S), vb)
    nb = S // t
    scale = QK_DIM ** -0.5
    c2 = scale * LOG2E

    ns = len(sends)

    def body(q_ref, k_ref, vt_ref, *rest):
        send_refs, (o_ref, lse_ref), gath_refs = rest[:ns], rest[ns:ns + 2], rest[ns + 2:2 * ns + 2]
        m_s, acc_s, s_buf, send_sems, recv_sems, local_sems = rest[2 * ns + 2:]
        hd, qi = pl.program_id(0), pl.program_id(1)

        def gather():
            return _exchange_copies([lambda j, r=r: r for r in send_refs], gath_refs, send_sems, recv_sems, local_sems)

        @pl.when((hd == 0) & (qi == 0))
        def _():
            _start_exchange(gather())

        m_s[...] = jnp.full_like(m_s, -jnp.inf)
        acc_s[...] = jnp.zeros_like(acc_s)

        def scores(j, slot):
            rows = pl.ds(pl.multiple_of(j * t, t), t)
            s_buf[slot] = _dot_nt(k_ref[0, rows, :], q_ref[0])

        def absorb(j, slot, diagonal):
            s = s_buf[slot]
            if diagonal:
                key = lax.broadcasted_iota(jnp.int32, (t, t), 0)
                qry = lax.broadcasted_iota(jnp.int32, (t, t), 1)
                s = jnp.where(key <= qry, s, -jnp.inf)
            m_prev = m_s[...]
            m_new = jnp.maximum(m_prev, jnp.max(s, axis=0, keepdims=True))
            alpha = jnp.exp2((m_prev - m_new) * c2)
            p = jnp.exp2((s - m_new) * c2)
            pb = p.astype(BF16)
            acc = alpha * acc_s[...]
            for u in range(t // vb):
                acc = acc + _dot(vt_ref[0, j * (t // vb) + u], pb[u * vb:(u + 1) * vb, :])
            acc_s[...] = acc
            m_s[...] = m_new

        def pair(i, carry):
            j = 2 * i
            scores(j + 1, 1)
            absorb(j, 0, False)
            scores(j + 2, 0)
            absorb(j + 1, 1, False)
            return carry

        scores(0, 0)
        lax.fori_loop(0, qi // 2, pair, 0)

        @pl.when(qi % 2 == 0)
        def _():
            absorb(qi, 0, True)

        @pl.when(qi % 2 == 1)
        def _():
            scores(qi, 1)
            absorb(qi - 1, 0, False)
            absorb(qi, 1, True)

        acc = acc_s[...]
        o_ref[...] = (acc[:DV] / acc[DV:]).T.astype(BF16)
        lse_ref[0, 0] = m_s[...] * scale + jnp.log(acc[DV:DV + 1])

        @pl.when((hd == H - 1) & (qi == nb - 1))
        def _():
            _finish_exchange(gather())

    outs = pl.pallas_call(
        body, name="mla_attn_fwd", grid=(H, nb),
        in_specs=[pl.BlockSpec((1, t, DQ), lambda h, i: (h, i, 0)),
                  pl.BlockSpec((1, S, DQ), lambda h, i: (h, 0, 0)),
                  pl.BlockSpec((1, S // vb, 2 * DV, vb), lambda h, i: (h, 0, 0, 0))] + [ANY_SPEC] * ns,
        out_specs=[pl.BlockSpec((t, DV), lambda h, i: (i, h)),
                   pl.BlockSpec((1, 1, 1, t), lambda h, i: (h, i, 0, 0))] + [ANY_SPEC] * ns,
        out_shape=[_sds((S, H * DV), BF16), _sds((H, nb, 1, t), F32)]
        + [_sds((N_DEV,) + a.shape, a.dtype) for a in sends],
        scratch_shapes=[pltpu.VMEM((1, t), F32), pltpu.VMEM((2 * DV, t), F32), pltpu.VMEM((2, t, t), F32)]
        + _comm_sems(ns),
        compiler_params=pltpu.CompilerParams(dimension_semantics=("arbitrary", "arbitrary"),
                                             vmem_limit_bytes=VMEM_LIMIT),
    )(q, k, vt, *sends)
    return outs[0], outs[1], outs[2:]


def _attn_out_fwd(o, x, w_o, b_o, gt, g, sc, sh):
    S, D = x.shape

    def body(o_ref, x_ref, wo_ref, bo_ref, gt_ref, g_ref, sc_ref, sh_ref, y_ref, x1_ref, h_ref):
        y = _dot(o_ref[...], wo_ref[...]) + bo_ref[...]
        y_ref[...] = y.astype(BF16)
        x1 = x_ref[...] + gt_ref[...] * y
        x1_ref[...] = x1
        h_ref[...] = _modulate(x1, g_ref[...], sc_ref[...], sh_ref[...]).astype(BF16)

    return _rowcall("attn_out_fwd", body, S, ROW_TILE, [o, x], [w_o, b_o, gt, g, sc, sh],
                    [_sds((S, D), BF16), _sds((S, D), F32), _sds((S, D), BF16)])


def _mlp_fwd(h, x, w1, w2, gt):
    S, D = x.shape
    FF = w1.shape[1]

    def body(h_ref, x_ref, w1_ref, w2_ref, gt_ref, rl_ref, act_ref, y_ref, x2_ref):
        rl = jnp.maximum(_dot(h_ref[...], w1_ref[...]), 0.0)
        rl_ref[...] = rl.astype(BF16)
        act = (rl * rl).astype(BF16)
        act_ref[...] = act
        y = _dot(act, w2_ref[...])
        y_ref[...] = y.astype(BF16)
        x2_ref[...] = x_ref[...] + gt_ref[...] * y

    return _rowcall("mlp_fwd", body, S, ROW_TILE_WIDE, [h, x], [w1, w2, gt],
                    [_sds((S, FF), BF16), _sds((S, FF), BF16), _sds((S, D), BF16), _sds((S, D), F32)])


def _final_norm_loss(xv, target, g, d_model):
    r = _rstd(xv)
    n = xv * r
    err = n * g - target
    part = 0.5 * jnp.sum(jnp.mean(err * err, axis=-1, keepdims=True), axis=0, keepdims=True)
    dout = err / d_model
    return part, _rms_bwd(dout * g, n, r), jnp.sum(dout * n, axis=0, keepdims=True)


def _mlp_fwd_loss(h, x, w1, w2, gt, target, g_final):
    S, D = x.shape
    FF = w1.shape[1]

    def body(h_ref, x_ref, t_ref, w1_ref, w2_ref, gt_ref, g_ref, rl_ref, act_ref, y_ref, dx_ref, loss_ref, dg_ref):
        i = pl.program_id(0)
        rl = jnp.maximum(_dot(h_ref[...], w1_ref[...]), 0.0)
        rl_ref[...] = rl.astype(BF16)
        act = (rl * rl).astype(BF16)
        act_ref[...] = act
        y = _dot(act, w2_ref[...])
        y_ref[...] = y.astype(BF16)
        part, dx, dg = _final_norm_loss(x_ref[...] + gt_ref[...] * y, t_ref[...], g_ref[...], D)
        dx_ref[...] = dx
        _acc(loss_ref, jnp.broadcast_to(part, loss_ref.shape), i)
        _acc(dg_ref, dg, i)

    return _rowcall("mlp_fwd_loss", body, S, ROW_TILE_WIDE, [h, x, target], [w1, w2, gt, g_final],
                    [_sds((S, FF), BF16), _sds((S, FF), BF16), _sds((S, D), BF16), _sds((S, D), F32)],
                    [_sds((1, 128), F32), _sds((1, D), F32)])


def _swa_in_fwd(x, g, sc, sh, w_qkv, b_qkv):
    S, D = x.shape
    NQ = SWA_HEADS * SWA_HEAD_DIM
    NK = SWA_KV_HEADS * SWA_HEAD_DIM

    def body(x_ref, g_ref, sc_ref, sh_ref, w_ref, b_ref, h_ref, q_ref, k_ref, v_ref):
        hb = _modulate(x_ref[...], g_ref[...], sc_ref[...], sh_ref[...]).astype(BF16)
        h_ref[...] = hb
        qkv = _dot(hb, w_ref[...]) + b_ref[...]
        q_ref[...] = qkv[:, :NQ].astype(BF16)
        k_ref[...] = qkv[:, NQ:NQ + NK].astype(BF16)
        v_ref[...] = qkv[:, NQ + NK:].astype(BF16)

    return _rowcall("swa_in_fwd", body, S, ROW_TILE, [x], [g, sc, sh, w_qkv, b_qkv],
                    [_sds((S, D), BF16), _sds((S, NQ), BF16), _sds((S, NK), BF16), _sds((S, NK), BF16)])


def _alibi_slope(head):
    return float(np.float32(2.0 ** (-8.0 * (head + 1) / SWA_HEADS)))


def _swa_geometry(n):
    W, G = WINDOW, SWA_GROUP
    key = lax.broadcasted_iota(jnp.int32, (2 * W, G * W), 0)
    qry = lax.broadcasted_iota(jnp.int32, (2 * W, G * W), 1) & (W - 1)
    dist = W + qry - key
    valid = (dist >= 0) & (dist < W) & ((n > 0) | (key >= W))
    return dist.astype(F32), valid


def _swa_group(kh, q_ref, sink_ref):
    W, G, Dh = WINDOW, SWA_GROUP, SWA_HEAD_DIM
    heads = [kh * G + g for g in range(G)]
    q4 = jnp.concatenate([q_ref[:, h * Dh:(h + 1) * Dh] for h in heads], axis=0)
    slopes = jnp.concatenate([jnp.full((1, W), _alibi_slope(h), F32) for h in heads], axis=1)
    sinks = jnp.concatenate([jnp.broadcast_to(sink_ref[:, h:h + 1], (1, W)) for h in heads], axis=1)
    return heads, q4, slopes, sinks


def _swa_band_specs(W, nb, cols):
    prev = pl.BlockSpec((W, cols), lambda n: (jnp.maximum(jnp.minimum(n, nb - 1) - 1, 0), 0))
    cur = pl.BlockSpec((W, cols), lambda n: (jnp.minimum(n, nb - 1), 0))
    return prev, cur


def _swa_attn_fwd(q, k, v, sinks):
    S, NQ = q.shape
    NK = k.shape[1]
    W, Dh, G = WINDOW, SWA_HEAD_DIM, SWA_GROUP
    nb = S // W

    def body(q_ref, kp_ref, kc_ref, vp_ref, vc_ref, sink_ref, o_ref, lse_ref):
        distf, valid = _swa_geometry(pl.program_id(0))
        kband = jnp.concatenate([kp_ref[...], kc_ref[...]], axis=0)
        vband_t = jnp.concatenate([vp_ref[...], vc_ref[...]], axis=0).astype(F32).T.astype(BF16)
        outs = []
        for kh in range(SWA_KV_HEADS):
            kb = kband[:, kh * Dh:(kh + 1) * Dh]
            vbt = vband_t[kh * Dh:(kh + 1) * Dh, :]
            heads, q4, slopes, sinks = _swa_group(kh, q_ref, sink_ref)
            s = _dot_nt(kb, q4) * (Dh ** -0.5) - slopes * distf
            s = jnp.where(valid, s, -jnp.inf)
            m = jnp.maximum(jnp.max(s, axis=0, keepdims=True), sinks)
            p = jnp.exp(s - m)
            denom = jnp.sum(p, axis=0, keepdims=True) + jnp.exp(sinks - m)
            out4 = _dot(vbt, (p * (1.0 / denom)).astype(BF16))
            lse4 = m + jnp.log(denom)
            for g, h in enumerate(heads):
                outs.append(out4[:, g * W:(g + 1) * W])
                lse_ref[h:h + 1, :] = lse4[:, g * W:(g + 1) * W]
        o_ref[...] = jnp.concatenate(outs, axis=0).T.astype(BF16)

    kprev, kcur = _swa_band_specs(W, nb, NK)
    return pl.pallas_call(
        body, name="swa_attn_fwd", grid=(nb,),
        in_specs=[pl.BlockSpec((W, NQ), lambda n: (n, 0)), kprev, kcur, kprev, kcur,
                  pl.BlockSpec((1, SWA_HEADS), lambda n: (0, 0))],
        out_specs=[pl.BlockSpec((W, NQ), lambda n: (n, 0)), pl.BlockSpec((SWA_HEADS, W), lambda n: (0, n))],
        out_shape=[_sds((S, NQ), BF16), _sds((SWA_HEADS, S), F32)],
        compiler_params=pltpu.CompilerParams(dimension_semantics=("arbitrary",), vmem_limit_bytes=VMEM_LIMIT),
    )(q, k, k, v, v, sinks)


def _mlp_bwd_a(dx, y, rl, gt, w2):
    S, D = dx.shape
    FF = rl.shape[1]

    def body(dx_ref, y_ref, rl_ref, gt_ref, w2_ref, dy_ref, du_ref, dgt_ref):
        i = pl.program_id(0)
        dxv = dx_ref[...]
        _acc(dgt_ref, jnp.sum(dxv * y_ref[...].astype(F32), axis=0, keepdims=True), i)
        dy = (dxv * gt_ref[...]).astype(BF16)
        dy_ref[...] = dy
        dact = _dot_nt(dy, w2_ref[...])
        du_ref[...] = (dact * (2.0 * rl_ref[...].astype(F32))).astype(BF16)

    return _rowcall("mlp_bwd_a", body, S, ROW_TILE_BWD, [dx, y, rl], [gt, w2],
                    [_sds((S, D), BF16), _sds((S, FF), BF16)], [_sds((1, D), F32)])


def _mlp_bwd_b(du, x, dx, w1, g, sc):
    S, D = x.shape

    def body(du_ref, x_ref, dx_ref, w1_ref, g_ref, sc_ref, dxo_ref, dsh_ref, da_ref):
        i = pl.program_id(0)
        dh = _dot_nt(du_ref[...], w1_ref[...])
        dxn, dsh, da = _modulate_bwd(dh, x_ref[...], g_ref[...], sc_ref[...])
        dxo_ref[...] = dx_ref[...] + dxn
        _acc(dsh_ref, dsh, i)
        _acc(da_ref, da, i)

    return _rowcall("mlp_bwd_b", body, S, ROW_TILE_BWD, [du, x, dx], [w1, g, sc],
                    [_sds((S, D), F32)], [_sds((1, D), F32), _sds((1, D), F32)])


def _attn_out_bwd(dx, y, o, gt, w_o, n_heads):
    S, D = dx.shape
    NO = o.shape[1]
    dh = NO // n_heads
    member = (jnp.arange(NO)[None, :] // dh == jnp.arange(16)[:, None]).astype(BF16)

    def body(dx_ref, y_ref, o_ref, gt_ref, wo_ref, mem_ref, dy_ref, do_ref, dl_ref, dgt_ref, dbo_ref):
        i = pl.program_id(0)
        dxv = dx_ref[...]
        _acc(dgt_ref, jnp.sum(dxv * y_ref[...].astype(F32), axis=0, keepdims=True), i)
        dy = dxv * gt_ref[...]
        _acc(dbo_ref, jnp.sum(dy, axis=0, keepdims=True), i)
        dyb = dy.astype(BF16)
        dy_ref[...] = dyb
        do = _dot_nt(dyb, wo_ref[...])
        do_ref[...] = do.astype(BF16)
        prod = do * o_ref[...].astype(F32)
        hi = prod.astype(BF16)
        lo = (prod - hi.astype(F32)).astype(BF16)
        dl_ref[...] = _dot_nt(mem_ref[...], hi) + _dot_nt(mem_ref[...], lo)

    tm = min(ROW_TILE, S)
    return _rowcall("attn_out_bwd", body, S, ROW_TILE, [dx, y, o], [gt, w_o, member],
                    [_sds((S, D), BF16), _sds((S, NO), BF16),
                     (_sds((16, S), F32), pl.BlockSpec((16, tm), lambda i: (0, i)))],
                    [_sds((1, D), F32), _sds((1, D), F32)])


def _mla_attn_bwd(q, k, v, do, lse, delta, t, gblks):
    H, S, DQ = q.shape
    DV = V_DIM
    t = min(t, S // 2)
    tk = 2 * t
    nq, nk = S // t, S // tk
    scale = QK_DIM ** -0.5
    c2 = scale * LOG2E

    ng = len(gblks)

    def body(q_ref, k_ref, v_ref, do_ref, lse_ref, dl_ref, *rest):
        g_refs, (dq_ref, dk_ref, dv_ref), recv_refs = rest[:ng], rest[ng:ng + 3], rest[ng + 3:2 * ng + 3]
        dk_s, dv_s, s_buf, dp_buf, send_sems, recv_sems, local_sems = rest[2 * ng + 3:]
        hd, kj = pl.program_id(0), pl.program_id(1)

        def scatter():
            return _exchange_copies([lambda j, r=r: r.at[j] for r in g_refs], recv_refs, send_sems, recv_sems,
                                    local_sems)

        @pl.when((hd == 0) & (kj == 0))
        def _():
            _start_exchange(scatter())

        @pl.when(kj == 0)
        def _():
            dq_ref[...] = jnp.zeros_like(dq_ref)

        dk_s[...] = jnp.zeros_like(dk_s)
        dv_s[...] = jnp.zeros_like(dv_s)

        def products(i, slot):
            rows = pl.ds(pl.multiple_of(i * t, t), t)
            s_buf[slot] = _dot_nt(k_ref[0], q_ref[0, rows, :])
            dp_buf[slot] = _dot_nt(v_ref[0], do_ref[rows, :])

        def absorb(i, slot, diagonal):
            rows = pl.ds(pl.multiple_of(i * t, t), t)
            qb, dob = q_ref[0, rows, :], do_ref[rows, :]
            p = jnp.exp2(s_buf[slot] * c2 - lse_ref[0, i])
            if diagonal is not None:
                key = lax.broadcasted_iota(jnp.int32, (tk, t), 0)
                qry = lax.broadcasted_iota(jnp.int32, (tk, t), 1) + diagonal * t
                p = jnp.where(key <= qry, p, 0.0)
            dv_s[...] += _dot(p.astype(BF16), dob)
            ds = (p * (dp_buf[slot] - dl_ref[0, i])).astype(BF16)
            dk_s[...] += _dot(ds, qb)
            dq_ref[0, rows, :] += _dot_tn(ds, k_ref[0])

        first = 2 * kj + 2
        n_off = nq - first

        def pair(i, carry):
            u = 2 * i
            products(first + u + 1, 1)
            absorb(first + u, 0, None)
            products(jnp.where(u + 2 < n_off, first + u + 2, 2 * kj), 0)
            absorb(first + u + 1, 1, None)
            return carry

        products(jnp.where(n_off > 0, first, 2 * kj), 0)
        lax.fori_loop(0, n_off // 2, pair, 0)
        products(2 * kj + 1, 1)
        absorb(2 * kj, 0, 0)
        absorb(2 * kj + 1, 1, 1)

        dk_ref[0] = (dk_s[...] * scale).astype(BF16)
        dv_ref[0] = dv_s[...].astype(BF16)

        @pl.when((hd == H - 1) & (kj == nk - 1))
        def _():
            _finish_exchange(scatter())

    rowspec = pl.BlockSpec((1, nq, 1, t), lambda h, j: (h, 0, 0, 0))
    outs = pl.pallas_call(
        body, name="mla_attn_bwd", grid=(H, nk),
        in_specs=[pl.BlockSpec((1, S, DQ), lambda h, j: (h, 0, 0)),
                  pl.BlockSpec((1, tk, DQ), lambda h, j: (h, j, 0)),
                  pl.BlockSpec((1, tk, DV), lambda h, j: (h, j, 0)),
                  pl.BlockSpec((S, DV), lambda h, j: (0, h)), rowspec, rowspec] + [ANY_SPEC] * ng,
        out_specs=[pl.BlockSpec((1, S, DQ), lambda h, j: (h, 0, 0)),
                   pl.BlockSpec((1, tk, DQ), lambda h, j: (h, j, 0)),
                   pl.BlockSpec((1, tk, DV), lambda h, j: (h, j, 0))] + [ANY_SPEC] * ng,
        out_shape=[_sds((H, S, DQ), F32), _sds((H, S, DQ), BF16), _sds((H, S, DV), BF16)]
        + [_sds(g.shape, g.dtype) for g in gblks],
        scratch_shapes=[pltpu.VMEM((tk, DQ), F32), pltpu.VMEM((tk, DV), F32), pltpu.VMEM((2, tk, t), F32),
                        pltpu.VMEM((2, tk, t), F32)] + _comm_sems(ng),
        compiler_params=pltpu.CompilerParams(dimension_semantics=("arbitrary", "arbitrary"),
                                             vmem_limit_bytes=VMEM_LIMIT),
    )(q, k, v, do, lse, delta, *gblks)
    return outs[0], outs[1], outs[2], outs[3:]


def _swa_attn_bwd(q, k, v, do, lse, delta, sinks):
    S, NQ = q.shape
    NK = k.shape[1]
    W, Dh, G = WINDOW, SWA_HEAD_DIM, SWA_GROUP
    nb = S // W

    def body(q_ref, kp_ref, kc_ref, vp_ref, vc_ref, do_ref, lse_ref, dl_ref, sink_ref,
             dq_ref, dk_ref, dv_ref, dsink_ref, dkc_s, dvc_s):
        n = pl.program_id(0)

        @pl.when(n == 0)
        def _():
            dkc_s[...] = jnp.zeros_like(dkc_s)
            dvc_s[...] = jnp.zeros_like(dvc_s)
            dsink_ref[...] = jnp.zeros_like(dsink_ref)

        @pl.when(n < nb)
        def _():
            distf, valid = _swa_geometry(n)
            kband = jnp.concatenate([kp_ref[...], kc_ref[...]], axis=0)
            vband = jnp.concatenate([vp_ref[...], vc_ref[...]], axis=0)
            kband_t = kband.astype(F32).T.astype(BF16)
            dq_t = []
            for kh in range(SWA_KV_HEADS):
                ck = slice(kh * Dh, (kh + 1) * Dh)
                kb, vb, kbt = kband[:, ck], vband[:, ck], kband_t[ck, :]
                heads, q4, slopes, sinks = _swa_group(kh, q_ref, sink_ref)
                do4 = jnp.concatenate([do_ref[:, h * Dh:(h + 1) * Dh] for h in heads], axis=0)
                lse4 = jnp.concatenate([lse_ref[h:h + 1, :] for h in heads], axis=1)
                dl4 = jnp.concatenate([dl_ref[h:h + 1, :] for h in heads], axis=1)
                s = _dot_nt(kb, q4) * (Dh ** -0.5) - slopes * distf
                p = jnp.where(valid, jnp.exp(s - lse4), 0.0)
                dvb = _dot(p.astype(BF16), do4)
                dp = _dot_nt(vb, do4)
                dsb = ((p * (dp - dl4)) * (Dh ** -0.5)).astype(BF16)
                dq4 = _dot(kbt, dsb)
                dkb = _dot(dsb, q4)
                dsk4 = jnp.exp(sinks - lse4) * dl4
                for g, h in enumerate(heads):
                    dq_t.append(dq4[:, g * W:(g + 1) * W])
                    dsink_ref[:, h:h + 1] += -jnp.sum(dsk4[:, g * W:(g + 1) * W], axis=1, keepdims=True)
                dk_ref[:, ck] = (dkc_s[:, ck] + dkb[:W]).astype(BF16)
                dv_ref[:, ck] = (dvc_s[:, ck] + dvb[:W]).astype(BF16)
                dkc_s[:, ck] = dkb[W:]
                dvc_s[:, ck] = dvb[W:]
            dq_ref[...] = jnp.concatenate(dq_t, axis=0).T.astype(BF16)

        @pl.when(n == nb)
        def _():
            dk_ref[...] = dkc_s[...].astype(BF16)
            dv_ref[...] = dvc_s[...].astype(BF16)

    kprev, kcur = _swa_band_specs(W, nb, NK)
    qspec = lambda cols: pl.BlockSpec((W, cols), lambda n: (jnp.minimum(n, nb - 1), 0))
    kvout = pl.BlockSpec((W, NK), lambda n: (jnp.maximum(n - 1, 0), 0))
    rowspec = pl.BlockSpec((SWA_HEADS, W), lambda n: (0, jnp.minimum(n, nb - 1)))
    return pl.pallas_call(
        body, name="swa_attn_bwd", grid=(nb + 1,),
        in_specs=[qspec(NQ), kprev, kcur, kprev, kcur, qspec(NQ), rowspec, rowspec,
                  pl.BlockSpec((1, SWA_HEADS), lambda n: (0, 0))],
        out_specs=[qspec(NQ), kvout, kvout, pl.BlockSpec((1, 128), lambda n: (0, 0))],
        out_shape=[_sds((S, NQ), BF16), _sds((S, NK), BF16), _sds((S, NK), BF16), _sds((1, 128), F32)],
        scratch_shapes=[pltpu.VMEM((W, NK), F32), pltpu.VMEM((W, NK), F32)],
        compiler_params=pltpu.CompilerParams(dimension_semantics=("arbitrary",), vmem_limit_bytes=VMEM_LIMIT),
    )(q, k, k, v, v, do, lse, delta, sinks)


def _swa_in_bwd(dq, dk, dv, x, dx, w_qkv, g, sc):
    S, D = x.shape
    N = w_qkv.shape[1]

    def body(dq_ref, dk_ref, dv_ref, x_ref, dx_ref, w_ref, g_ref, sc_ref, dqkv_ref, dxo_ref, db_ref, dsh_ref, da_ref):
        i = pl.program_id(0)
        dqkv = jnp.concatenate([dq_ref[...], dk_ref[...], dv_ref[...]], axis=1)
        dqkv_ref[...] = dqkv
        _acc(db_ref, jnp.sum(dqkv.astype(F32), axis=0, keepdims=True), i)
        dh = _dot_nt(dqkv, w_ref[...])
        dxn, dsh, da = _modulate_bwd(dh, x_ref[...], g_ref[...], sc_ref[...])
        dxo_ref[...] = dx_ref[...] + dxn
        _acc(dsh_ref, dsh, i)
        _acc(da_ref, da, i)

    return _rowcall("swa_in_bwd", body, S, ROW_TILE, [dq, dk, dv, x, dx], [w_qkv, g, sc],
                    [_sds((S, N), BF16), _sds((S, D), F32)],
                    [_sds((1, N), F32), _sds((1, D), F32), _sds((1, D), F32)])


def _mla_in_bwd(dq, dk, dv, cos, sin, cqp, ckvp, x, dx, w_uqx, g_q, w_ukv, g_kv, w_cat, g, sc):
    S, D = x.shape
    H = MLA_HEADS
    QL = g_q.shape[1]
    NX = w_uqx.shape[1]
    NC = w_cat.shape[1]

    def body(dq_ref, dk_ref, dv_ref, cos_ref, sin_ref, cqp_ref, ckvp_ref, x_ref, dx_ref,
             wuqx_ref, gq_ref, wukv_ref, gkv_ref, wcat_ref, g_ref, sc_ref,
             dqx_ref, dkv_ref, dcat_ref, dxo_ref, dgq_ref, dgkv_ref, dsh_ref, da_ref):
        i = pl.program_id(0)
        cs, sn = cos_ref[...], sin_ref[...]
        dkr = jnp.zeros(cs.shape, F32)
        for hd in range(H):
            b = hd * 256
            dqh = dq_ref[hd] * (QK_DIM ** -0.5)
            dqx_ref[:, b:b + QK_NOPE] = dqh[:, :QK_NOPE].astype(BF16)
            dqx_ref[:, b + 128:b + 192] = (dqh[:, QK_NOPE:] * cs).astype(BF16)
            dqx_ref[:, b + 192:b + 256] = (dqh[:, QK_NOPE:] * sn).astype(BF16)
            dkh = dk_ref[hd]
            dkv_ref[:, b:b + QK_NOPE] = dkh[:, :QK_NOPE]
            dkv_ref[:, b + 128:b + 256] = dv_ref[hd]
            dkr = dkr + dkh[:, QK_NOPE:].astype(F32)
        dcq = _dot_nt(dqx_ref[...], wuqx_ref[...])
        cqp = cqp_ref[...]
        rq = _rstd(cqp)
        nq = cqp * rq
        _acc(dgq_ref, jnp.sum(dcq * nq, axis=0, keepdims=True), i)
        dcqp = _rms_bwd(dcq * gq_ref[...], nq, rq)
        dckv = _dot_nt(dkv_ref[...], wukv_ref[...])
        ckvp = ckvp_ref[...]
        rk = _rstd(ckvp)
        nk = ckvp * rk
        _acc(dgkv_ref, jnp.sum(dckv * nk, axis=0, keepdims=True), i)
        dckvp = _rms_bwd(dckv * gkv_ref[...], nk, rk)
        dcat_ref[:, :QL] = dcqp.astype(BF16)
        dcat_ref[:, QL:QL + KV_LORA] = dckvp.astype(BF16)
        o = QL + KV_LORA
        dcat_ref[:, o:o + QK_ROPE] = (dkr * cs).astype(BF16)
        dcat_ref[:, o + QK_ROPE:o + 2 * QK_ROPE] = (dkr * sn).astype(BF16)
        dh = _dot_nt(dcat_ref[...], wcat_ref[...])
        dxn, dsh, da = _modulate_bwd(dh, x_ref[...], g_ref[...], sc_ref[...])
        dxo_ref[...] = dx_ref[...] + dxn
        _acc(dsh_ref, dsh, i)
        _acc(da_ref, da, i)

    return _rowcall("mla_in_bwd", body, S, ROW_TILE_WIDE, [dq, dk, dv, cos, sin, cqp, ckvp, x, dx],
                    [w_uqx, g_q, w_ukv, g_kv, w_cat, g, sc],
                    [_sds((S, NX), BF16), _sds((S, NX), BF16), _sds((S, NC), BF16), _sds((S, D), F32)],
                    [_sds((1, QL), F32), _sds((1, KV_LORA), F32), _sds((1, D), F32), _sds((1, D), F32)])


def _matmul_tn(name, a, b, out_dtype=F32):
    S, K = a.shape
    N = b.shape[1]
    tk, tn, ts = min(K, 1024), min(N, 1024), min(S, TN_TOKENS)
    if N % tn:
        tn = 512 if N % 512 == 0 else (384 if N % 384 == 0 else 128)
    if K % tk:
        tk = 512 if K % 512 == 0 else (384 if K % 384 == 0 else 128)
    ns = S // ts

    def body(a_ref, b_ref, o_ref, *scratch):
        acc_ref = scratch[0] if scratch else o_ref
        _acc(acc_ref, _dot_tn(a_ref[...], b_ref[...]), pl.program_id(2))
        if scratch:
            @pl.when(pl.program_id(2) == ns - 1)
            def _():
                o_ref[...] = acc_ref[...].astype(out_dtype)

    return pl.pallas_call(
        body, name=name, grid=(K // tk, N // tn, ns),
        in_specs=[pl.BlockSpec((ts, tk), lambda i, j, s: (s, i)), pl.BlockSpec((ts, tn), lambda i, j, s: (s, j))],
        out_specs=pl.BlockSpec((tk, tn), lambda i, j, s: (i, j)),
        out_shape=_sds((K, N), out_dtype),
        scratch_shapes=[] if out_dtype == F32 else [pltpu.VMEM((tk, tn), F32)],
        compiler_params=pltpu.CompilerParams(dimension_semantics=("parallel", "parallel", "arbitrary"),
                                             vmem_limit_bytes=VMEM_LIMIT),
    )(a, b)


def _silu(c):
    return c * jax.nn.sigmoid(c)


def _ada_fwd(c_all, w_ada):
    L, D, NC = w_ada.shape

    def body(c_ref, w_ref, o_ref):
        cond = _silu(c_ref[...]).astype(BF16)
        o_ref[0] = _dot(cond, w_ref[0].astype(BF16))

    return pl.pallas_call(
        body, name="ada_fwd", grid=(L,),
        in_specs=[pl.BlockSpec(c_all.shape, lambda l: (0, 0)), pl.BlockSpec((1, D, NC), lambda l: (l, 0, 0))],
        out_specs=pl.BlockSpec((1, N_DEV, NC), lambda l: (l, 0, 0)),
        out_shape=_sds((L, N_DEV, NC), F32),
        compiler_params=pltpu.CompilerParams(dimension_semantics=("arbitrary",), vmem_limit_bytes=VMEM_LIMIT),
    )(c_all, w_ada)


def _adamw(w, g, m, v):
    m = ADAM_B1 * m + (1.0 - ADAM_B1) * g
    v = ADAM_B2 * v + (1.0 - ADAM_B2) * (g * g)
    m_hat = m / (1.0 - ADAM_B1 ** ADAM_STEP)
    v_hat = v / (1.0 - ADAM_B2 ** ADAM_STEP)
    delta = -ADAM_LR * (m_hat / (jnp.sqrt(v_hat) + ADAM_EPS) + ADAM_WD * w)
    return delta, m, v


def _ada_bwd_adamw(c_all_t, dmod_cols, w, m, v):
    L, D, NC = w.shape
    tr = min(D, 256)

    def body(ct_ref, dm_ref, w_ref, m_ref, v_ref, g_ref, d_ref, mo_ref, vo_ref):
        cond_t = _silu(ct_ref[...])
        dm = dm_ref[0]
        g = cond_t[:, 0:1] * dm[0:1, :]
        for b in range(1, N_DEV):
            g = g + cond_t[:, b:b + 1] * dm[b:b + 1, :]
        g_ref[0] = g
        d_ref[0], mo_ref[0], vo_ref[0] = _adamw(w_ref[0], g, m_ref[0], v_ref[0])

    wspec = pl.BlockSpec((1, tr, NC), lambda l, r: (l, r, 0))
    return pl.pallas_call(
        body, name="ada_bwd_adamw", grid=(L, D // tr),
        in_specs=[pl.BlockSpec((tr, N_DEV), lambda l, r: (r, 0)),
                  pl.BlockSpec((1, N_DEV, NC), lambda l, r: (l, 0, 0)), wspec, wspec, wspec],
        out_specs=[wspec] * 4, out_shape=[_sds(w.shape, F32)] * 4,
        compiler_params=pltpu.CompilerParams(dimension_semantics=("parallel", "parallel"), vmem_limit_bytes=VMEM_LIMIT),
    )(c_all_t, dmod_cols, w, m, v)


def _sum_devices(x):
    def body(x_ref, o_ref):
        s = x_ref[0]
        for j in range(1, N_DEV):
            s = s + x_ref[j]
        o_ref[...] = s

    return pl.pallas_call(body, name="sum_devices", out_shape=_sds(x.shape[1:], F32))(x)


def _adamw_small(w, g, m, v):
    def body(w_ref, g_ref, m_ref, v_ref, d_ref, mo_ref, vo_ref):
        d_ref[...], mo_ref[...], vo_ref[...] = _adamw(w_ref[...], g_ref[...], m_ref[...], v_ref[...])

    return pl.pallas_call(body, name="adamw_small", out_shape=[_sds(w.shape, F32)] * 3)(w, g, m, v)


def _me():
    return lax.axis_index("x") * 4 + lax.axis_index("y") * 2 + lax.axis_index("c")


def _peer(k):
    x, y, c = lax.axis_index("x"), lax.axis_index("y"), lax.axis_index("c")
    px = 1 - x if k & 4 else x
    py = 1 - y if k & 2 else y
    pc = 1 - c if k & 1 else c
    return (px, py, pc), px * 4 + py * 2 + pc


VMEM_SPEC = pl.BlockSpec(memory_space=pltpu.VMEM)
ANY_SPEC = pl.BlockSpec(memory_space=pl.ANY)
def _comm_sems(n):
    return [pltpu.SemaphoreType.DMA((n * (N_DEV - 1),)), pltpu.SemaphoreType.DMA((n * (N_DEV - 1),)),
            pltpu.SemaphoreType.DMA((n,))]


def _exchange_copies(srcs_of, dst_refs, send_sems, recv_sems, local_sems):
    me = _me()
    local, sends, recvs = [], [], []
    for a, (src_of, dst_ref) in enumerate(zip(srcs_of, dst_refs)):
        local.append(pltpu.make_async_copy(src_of(me), dst_ref.at[me], local_sems.at[a]))
        for k in range(1, N_DEV):
            dev, pj = _peer(k)
            i = a * (N_DEV - 1) + k - 1
            sems = dict(send_sem=send_sems.at[i], recv_sem=recv_sems.at[i], device_id=dev, device_id_type=MESH_IDS)
            sends.append(pltpu.make_async_remote_copy(src_ref=src_of(pj), dst_ref=dst_ref.at[me], **sems))
            recvs.append(pltpu.make_async_remote_copy(src_ref=src_of(pj), dst_ref=dst_ref.at[pj], **sems))
    return local, sends, recvs


def _start_exchange(copies):
    local, sends, _ = copies
    for cp in local + sends:
        cp.start()


def _finish_exchange(copies):
    local, sends, recvs = copies
    for cp in recvs:
        cp.wait_recv()
    for cp in sends:
        cp.wait_send()
    for cp in local:
        cp.wait()


def _sum_adamw(recv, w, m, v):
    shape = w.shape
    C = shape[-1]
    R = w.size // C
    rows = max(d for d in range(16, min(R, 512) + 1, 16) if R % d == 0 and d * C <= 256 * 1024)

    def body(r_ref, w_ref, m_ref, v_ref, go_ref, d_ref, mo_ref, vo_ref):
        g = r_ref[0].astype(F32)
        for j in range(1, N_DEV):
            g = g + r_ref[j].astype(F32)
        go_ref[...] = g
        d_ref[...], mo_ref[...], vo_ref[...] = _adamw(w_ref[...], g, m_ref[...], v_ref[...])

    spec = pl.BlockSpec((rows, C), lambda i: (i, 0))
    outs = pl.pallas_call(
        body, name="sum_adamw", grid=(R // rows,),
        in_specs=[pl.BlockSpec((N_DEV, rows, C), lambda i: (0, i, 0)), spec, spec, spec],
        out_specs=[spec] * 4, out_shape=[_sds((R, C), F32)] * 4,
        compiler_params=pltpu.CompilerParams(dimension_semantics=("parallel",), vmem_limit_bytes=VMEM_LIMIT),
    )(recv.reshape(N_DEV, R, C), w.reshape(R, C), m.reshape(R, C), v.reshape(R, C))
    return [o.reshape(shape) for o in outs]


def _all_gather(name, x, out_dtype):
    R, C = x.shape
    cast = out_dtype != x.dtype

    def body(x_ref, out_ref, buf, send_sems, recv_sems, local_sem):
        me = _me()
        if cast:
            buf[...] = x_ref[...].astype(out_dtype)
            src = buf
        else:
            src = x_ref
        local = pltpu.make_async_copy(src, out_ref.at[me], local_sem)
        local.start()
        sends = []
        for k in range(1, N_DEV):
            dev, _ = _peer(k)
            cp = pltpu.make_async_remote_copy(src_ref=src, dst_ref=out_ref.at[me], send_sem=send_sems.at[k - 1],
                                              recv_sem=recv_sems.at[k - 1], device_id=dev, device_id_type=MESH_IDS)
            cp.start()
            sends.append(cp)
        for k in range(1, N_DEV):
            dev, pj = _peer(k)
            pltpu.make_async_remote_copy(src_ref=src, dst_ref=out_ref.at[pj], send_sem=send_sems.at[k - 1],
                                         recv_sem=recv_sems.at[k - 1], device_id=dev, device_id_type=MESH_IDS).wait_recv()
        for cp in sends:
            cp.wait_send()
        local.wait()

    return pl.pallas_call(
        body, name=name, in_specs=[VMEM_SPEC], out_specs=ANY_SPEC, out_shape=_sds((N_DEV, R, C), out_dtype),
        scratch_shapes=[pltpu.VMEM((R, C) if cast else (8, 128), out_dtype),
                        pltpu.SemaphoreType.DMA((N_DEV - 1,)), pltpu.SemaphoreType.DMA((N_DEV - 1,)),
                        pltpu.SemaphoreType.DMA(())],
        compiler_params=pltpu.CompilerParams(vmem_limit_bytes=VMEM_LIMIT),
    )(x)


def _all_to_all(name, x):
    _, R, C = x.shape

    def body(x_ref, out_ref, send_sems, recv_sems, local_sem):
        me = _me()
        local = pltpu.make_async_copy(x_ref.at[me], out_ref.at[me], local_sem)
        local.start()
        sends = []
        for k in range(1, N_DEV):
            dev, pj = _peer(k)
            cp = pltpu.make_async_remote_copy(src_ref=x_ref.at[pj], dst_ref=out_ref.at[me], send_sem=send_sems.at[k - 1],
                                              recv_sem=recv_sems.at[k - 1], device_id=dev, device_id_type=MESH_IDS)
            cp.start()
            sends.append(cp)
        for k in range(1, N_DEV):
            dev, pj = _peer(k)
            pltpu.make_async_remote_copy(src_ref=x_ref.at[pj], dst_ref=out_ref.at[pj], send_sem=send_sems.at[k - 1],
                                         recv_sem=recv_sems.at[k - 1], device_id=dev, device_id_type=MESH_IDS).wait_recv()
        for cp in sends:
            cp.wait_send()
        local.wait()

    return pl.pallas_call(
        body, name=name, in_specs=[VMEM_SPEC], out_specs=VMEM_SPEC, out_shape=_sds(x.shape, x.dtype),
        scratch_shapes=[pltpu.SemaphoreType.DMA((N_DEV - 1,)), pltpu.SemaphoreType.DMA((N_DEV - 1,)),
                        pltpu.SemaphoreType.DMA(())],
    )(x)


def _reduce_scatter_adamw(name, gblk, w, m, v):
    _, R, C = gblk.shape
    rows = 8
    for cand in (136, 128, 80, 64, 40, 32, 16, 8):
        if R % cand == 0:
            rows = cand
            break

    def body(g_ref, w_ref, m_ref, v_ref, go_ref, d_ref, mo_ref, vo_ref, recv, send_sems, recv_sems, local_sem):
        me = _me()
        local = pltpu.make_async_copy(g_ref.at[me], recv.at[me], local_sem)
        local.start()
        sends = []
        for k in range(1, N_DEV):
            dev, pj = _peer(k)
            cp = pltpu.make_async_remote_copy(src_ref=g_ref.at[pj], dst_ref=recv.at[me], send_sem=send_sems.at[k - 1],
                                              recv_sem=recv_sems.at[k - 1], device_id=dev, device_id_type=MESH_IDS)
            cp.start()
            sends.append(cp)
        for k in range(1, N_DEV):
            dev, pj = _peer(k)
            pltpu.make_async_remote_copy(src_ref=g_ref.at[pj], dst_ref=recv.at[pj], send_sem=send_sems.at[k - 1],
                                         recv_sem=recv_sems.at[k - 1], device_id=dev, device_id_type=MESH_IDS).wait_recv()
        local.wait()

        def chunk(i, carry):
            r = pl.ds(pl.multiple_of(i * rows, rows), rows)
            g = recv[0, r, :].astype(F32)
            for j in range(1, N_DEV):
                g = g + recv[j, r, :].astype(F32)
            go_ref[r, :] = g
            d_ref[r, :], mo_ref[r, :], vo_ref[r, :] = _adamw(w_ref[r, :], g, m_ref[r, :], v_ref[r, :])
            return carry

        lax.fori_loop(0, R // rows, chunk, 0)
        for cp in sends:
            cp.wait_send()

    return pl.pallas_call(
        body, name=name, in_specs=[ANY_SPEC, VMEM_SPEC, VMEM_SPEC, VMEM_SPEC], out_specs=[VMEM_SPEC] * 4,
        out_shape=[_sds((R, C), F32)] * 4,
        scratch_shapes=[pltpu.VMEM((N_DEV, R, C), BF16), pltpu.SemaphoreType.DMA((N_DEV - 1,)),
                        pltpu.SemaphoreType.DMA((N_DEV - 1,)), pltpu.SemaphoreType.DMA(())],
        compiler_params=pltpu.CompilerParams(vmem_limit_bytes=VMEM_LIMIT),
    )(gblk, w, m, v)


FIRST_WEIGHTS = ["mla_w_dq", "mla_w_uq", "mla_w_dkv", "mla_w_ukv"]
LATE_WEIGHTS = ["mla_w_o", "swa_w_qkv", "swa_w_o", "w_ff1", "w_ff2"]
ROW_SHARDED = {"mla_w_dq", "mla_w_dkv", "mla_w_o", "swa_w_o", "w_ff2"}


def _unblock(name, blocks):
    sh = blocks.shape[1:]
    if name in ROW_SHARDED:
        return jnp.moveaxis(blocks, 0, 1).reshape(sh[0], N_DEV * sh[1], sh[2])
    return jnp.moveaxis(blocks, 0, 2).reshape(sh[0], sh[1], N_DEV * sh[2])


def _block(name, full):
    L, K, N = full.shape
    if name in ROW_SHARDED:
        return jnp.moveaxis(full.reshape(L, N_DEV, K // N_DEV, N), 1, 0)
    return jnp.moveaxis(full.reshape(L, K, N_DEV, N // N_DEV), 2, 0)


def _rot_cols(w):
    half = QK_ROPE // 2
    return jnp.concatenate([-w[..., half:], w[..., :half]], axis=-1)


def _unrot_cols(gw):
    half = QK_ROPE // 2
    return jnp.concatenate([gw[..., half:], -gw[..., :half]], axis=-1)


def _row(v):
    return v.reshape(1, -1)


def _mlp_block_bwd(dx, sv, w1, w2, g, sc, gt):
    dy, du, dgt = _mlp_bwd_a(dx, sv["y2"], sv["rl"], gt, w2)
    dw2 = _matmul_tn("dw_ff2", sv["act"], dy, BF16)
    dw1 = _matmul_tn("dw_ff1", sv["h2"], du, BF16)
    dxo, dsh, da = _mlp_bwd_b(du, sv["x1"], dx, w1, g, sc)
    return dxo, dw1, dw2, dsh, da, dgt


def kernel(x, c, positions, w_ada, b_ada, g_mix, g_mlp, mla_w_dq, mla_g_q, mla_w_uq, mla_w_dkv, mla_g_kv, mla_w_ukv, mla_w_o, swa_w_qkv, swa_b_qkv, swa_sinks, swa_w_o, swa_b_o, w_ff1, w_ff2, g_final, loss_target, m_w_ada, m_b_ada, m_g_mix, m_g_mlp, m_mla_w_dq, m_mla_g_q, m_mla_w_uq, m_mla_w_dkv, m_mla_g_kv, m_mla_w_ukv, m_mla_w_o, m_swa_w_qkv, m_swa_b_qkv, m_swa_sinks, m_swa_w_o, m_swa_b_o, m_w_ff1, m_w_ff2, m_g_final, v_w_ada, v_b_ada, v_g_mix, v_g_mlp, v_mla_w_dq, v_mla_g_q, v_mla_w_uq, v_mla_w_dkv, v_mla_g_kv, v_mla_w_ukv, v_mla_w_o, v_swa_w_qkv, v_swa_b_qkv, v_swa_sinks, v_swa_w_o, v_swa_b_o, v_w_ff1, v_w_ff2, v_g_final):
    S, D = x.shape[1], x.shape[2]
    me = _me()
    x0 = x[0]
    target = loss_target[0]
    big_w = dict(mla_w_dq=mla_w_dq, mla_w_uq=mla_w_uq, mla_w_dkv=mla_w_dkv, mla_w_ukv=mla_w_ukv, mla_w_o=mla_w_o,
                 swa_w_qkv=swa_w_qkv, swa_w_o=swa_w_o, w_ff1=w_ff1, w_ff2=w_ff2)
    big_m = dict(mla_w_dq=m_mla_w_dq, mla_w_uq=m_mla_w_uq, mla_w_dkv=m_mla_w_dkv, mla_w_ukv=m_mla_w_ukv,
                 mla_w_o=m_mla_w_o, swa_w_qkv=m_swa_w_qkv, swa_w_o=m_swa_w_o, w_ff1=m_w_ff1, w_ff2=m_w_ff2)
    big_v = dict(mla_w_dq=v_mla_w_dq, mla_w_uq=v_mla_w_uq, mla_w_dkv=v_mla_w_dkv, mla_w_ukv=v_mla_w_ukv,
                 mla_w_o=v_mla_w_o, swa_w_qkv=v_swa_w_qkv, swa_w_o=v_swa_w_o, w_ff1=v_w_ff1, w_ff2=v_w_ff2)
    groups = {"first": FIRST_WEIGHTS, "late": LATE_WEIGHTS}
    wrows = {n: -(-big_w[n].size // (PACK_COLS * 16)) * 16 for n in FIRST_WEIGHTS + LATE_WEIGHTS}
    offs = {g: np.concatenate([[0], np.cumsum([wrows[n] for n in names])]).astype(int) for g, names in groups.items()}

    def as_rows(n, a, lead=()):
        flat = a.reshape(lead + (-1,))
        pad = wrows[n] * PACK_COLS - flat.shape[-1]
        if pad:
            flat = jnp.pad(flat, ((0, 0),) * len(lead) + ((0, pad),))
        return flat.reshape(lead + (wrows[n], PACK_COLS))

    def pack(g, d):
        return jnp.concatenate([as_rows(n, d[n]) for n in groups[g]], axis=0)

    def pack_blocks(g, gfull):
        return jnp.concatenate([as_rows(n, _block(n, gfull[n]).astype(BF16), (N_DEV,)) for n in groups[g]], axis=1)

    def unpack(g, packed, lead=()):
        out = {}
        for i, n in enumerate(groups[g]):
            part = packed[..., int(offs[g][i]):int(offs[g][i + 1]), :].reshape(lead + (-1,))
            out[n] = part[..., :big_w[n].size].reshape(lead + big_w[n].shape)
        return out

    gathered = _all_gather("gather_weights", pack("first", big_w), BF16)
    wfull = {n: _unblock(n, b) for n, b in unpack("first", gathered, (N_DEV,)).items()}
    w_dq, w_dkv = wfull["mla_w_dq"][0], wfull["mla_w_dkv"][0]
    w_cat = jnp.concatenate([w_dq, w_dkv, _rot_cols(w_dkv[:, KV_LORA:])], axis=1)
    QL = w_dq.shape[1]
    w_uq = wfull["mla_w_uq"][0].reshape(QL, MLA_HEADS, QK_DIM)
    w_uqx = jnp.concatenate([w_uq, _rot_cols(w_uq[..., QK_NOPE:])], axis=-1).reshape(QL, MLA_HEADS * 256)
    w_ukv = wfull["mla_w_ukv"][0]

    L = w_ada.shape[0]
    NC = w_ada.shape[2]
    nbq, nbo = swa_b_qkv.shape[1], swa_b_o.shape[1]
    cpad = -(-(D + nbq + nbo) // 1024) * 1024
    cpack = jnp.pad(jnp.concatenate([c[0], swa_b_qkv[0], swa_b_o[0]]), (0, cpad - (D + nbq + nbo))).reshape(8, cpad // 8)
    call = _all_gather("gather_c", cpack, F32).reshape(N_DEV, cpad)
    c_all = call[:, :D]
    b_qkv_full = call[:, D:D + nbq].reshape(1, N_DEV * nbq)
    b_o_full = call[:, D + nbq:D + nbq + nbo].reshape(1, N_DEV * nbo)
    mod_cols = _ada_fwd(c_all, w_ada)
    mpad = -(-(L * NC) // 1024) * 1024
    mod_send = jnp.pad(jnp.moveaxis(mod_cols, 1, 0).reshape(N_DEV, L * NC), ((0, 0), (0, mpad - L * NC)))
    mod_mine = _all_to_all("exchange_mod", mod_send.reshape(N_DEV, 8, mpad // 8)).reshape(N_DEV, mpad)[:, :L * NC]
    mod = jnp.moveaxis(mod_mine.reshape(N_DEV, L, NC), 0, 1).reshape(L, N_DEV * NC) + b_ada
    mods = mod.reshape(L, 6, 1, D)

    half = QK_ROPE // 2
    inv_freq = ROPE_THETA ** (-jnp.arange(half, dtype=F32) / half)
    ang = positions[0].astype(F32)[:, None] * inv_freq
    cos = jnp.concatenate([jnp.cos(ang), jnp.cos(ang)], axis=-1)
    sin = jnp.concatenate([jnp.sin(ang), jnp.sin(ang)], axis=-1)

    T_ATT = ATT_TILE
    zero_bias = jnp.zeros((1, D), F32)

    sh1, sc1, gt1, sh2, sc2, gt2 = [mods[0, i] for i in range(6)]
    gm0, gp0 = _row(g_mix[0]), _row(g_mlp[0])
    h1, cqp, cq, ckvp, ckv, q, k, v, vt = _mla_in_fwd(x0, cos, sin, gm0, sc1, sh1, w_cat, mla_g_q, w_uqx, mla_g_kv,
                                                      w_ukv, ROW_TILE)
    o0, lse0, gathered = _mla_attn_fwd(q, k, vt, ATT_TILE_FWD, [big_w[n].astype(BF16) for n in LATE_WEIGHTS])
    wfull = {n: _unblock(n, b) for n, b in zip(LATE_WEIGHTS, gathered)}
    w_o_mla, w_qkv, w_o_swa = wfull["mla_w_o"][0], wfull["swa_w_qkv"][0], wfull["swa_w_o"][0]
    ff1, ff2 = wfull["w_ff1"], wfull["w_ff2"]
    y1, x1, h2 = _attn_out_fwd(o0, x0, w_o_mla, zero_bias, gt1, gp0, sc2, sh2)
    rl0, act0, y2, x2 = _mlp_fwd(h2, x1, ff1[0], ff2[0], gt2)
    sv0 = dict(y2=y2, rl=rl0, act=act0, h2=h2, x1=x1)

    th1, tc1, tg1, th2, tc2, tg2 = [mods[1, i] for i in range(6)]
    gm1, gp1 = _row(g_mix[1]), _row(g_mlp[1])
    h3, sq, sk, svv = _swa_in_fwd(x2, gm1, tc1, th1, w_qkv, b_qkv_full)
    o1, lse1 = _swa_attn_fwd(sq, sk, svv, swa_sinks)
    y3, x3, h4 = _attn_out_fwd(o1, x2, w_o_swa, b_o_full, tg1, gp1, tc2, th2)
    rl1, act1, y4, dx4, loss_part, dg_final = _mlp_fwd_loss(h4, x3, ff1[1], ff2[1], tg2, target, _row(g_final))
    sv1 = dict(y2=y4, rl=rl1, act=act1, h2=h4, x1=x3)

    dx3, dw1_1, dw2_1, dsh2_1, da2_1, dgt2_1 = _mlp_block_bwd(dx4, sv1, ff1[1], ff2[1], gp1, tc2, tg2)
    dy, do, dl, dgt1_1, db_o = _attn_out_bwd(dx3, y3, o1, tg1, w_o_swa, SWA_HEADS)
    dw_o_swa = _matmul_tn("dw_o", o1, dy, BF16)
    dsq, dsk, dsv, dsink = _swa_attn_bwd(sq, sk, svv, do, lse1, dl, swa_sinks)
    dqkv, dx2, db_qkv, dsh1_1, da1_1 = _swa_in_bwd(dsq, dsk, dsv, x2, dx3, w_qkv, gm1, tc1)
    dw_qkv = _matmul_tn("dw_qkv", h3, dqkv, BF16)

    dx1, dw1_0, dw2_0, dsh2_0, da2_0, dgt2_0 = _mlp_block_bwd(dx2, sv0, ff1[0], ff2[0], gp0, sc2, gt2)
    dy, do, dl, dgt1_0, _ = _attn_out_bwd(dx1, y1, o0, gt1, w_o_mla, MLA_HEADS)
    dw_o_mla = _matmul_tn("dw_o", o0, dy, BF16)
    tb = min(T_ATT, S)
    delta = dl[:MLA_HEADS].reshape(MLA_HEADS, S // tb, 1, tb)
    glate = dict(mla_w_o=dw_o_mla[None], swa_w_qkv=dw_qkv[None], swa_w_o=dw_o_swa[None],
                 w_ff1=jnp.stack([dw1_0, dw1_1]), w_ff2=jnp.stack([dw2_0, dw2_1]))
    lse_rows = (lse0 * LOG2E).reshape(MLA_HEADS, S // tb, 1, tb)
    dq, dk, dv, recv = _mla_attn_bwd(q, k, v, do, lse_rows, delta, T_ATT,
                                     [_block(n, glate[n]).astype(BF16) for n in LATE_WEIGHTS])
    late = {n: _sum_adamw(r, big_w[n], big_m[n], big_v[n]) for n, r in zip(LATE_WEIGHTS, recv)}
    dqx, dkv, dcat, dx0, dg_q, dg_kv, dsh1_0, da1_0 = _mla_in_bwd(
        dq, dk, dv, cos, sin, cqp, ckvp, x0, dx1, w_uqx, mla_g_q, w_ukv, mla_g_kv, w_cat, gm0, sc1)
    dw_uqx = _matmul_tn("dw_uq", cq, dqx).reshape(QL, MLA_HEADS, 256)
    dw_ukv = _matmul_tn("dw_ukv", ckv, dkv)
    dw_cat = _matmul_tn("dw_down", h1, dcat)
    dw_uq = jnp.concatenate([dw_uqx[..., :QK_NOPE], dw_uqx[..., 128:192] + _unrot_cols(dw_uqx[..., 192:256])],
                            axis=-1).reshape(QL, MLA_HEADS * QK_DIM)
    o_kr = QL + KV_LORA
    dw_dkv = jnp.concatenate([dw_cat[:, QL:o_kr],
                              dw_cat[:, o_kr:o_kr + QK_ROPE] + _unrot_cols(dw_cat[:, o_kr + QK_ROPE:])], axis=1)

    gfirst = dict(mla_w_dq=dw_cat[None, :, :QL], mla_w_uq=dw_uq[None], mla_w_dkv=dw_dkv[None], mla_w_ukv=dw_ukv[None])
    first = _reduce_scatter_adamw("grad_exchange_adamw", pack_blocks("first", gfirst), pack("first", big_w),
                                  pack("first", big_m), pack("first", big_v))
    big_g, big_d, big_nm, big_nv = ({**unpack("first", first[j]), **{n: late[n][j] for n in LATE_WEIGHTS}}
                                    for j in range(4))

    dmod = jnp.stack([
        jnp.concatenate([dsh1_0, gm0 * da1_0, dgt1_0, dsh2_0, gp0 * da2_0, dgt2_0], axis=1),
        jnp.concatenate([dsh1_1, gm1 * da1_1, dgt1_1, dsh2_1, gp1 * da2_1, dgt2_1], axis=1)]).reshape(-1)
    dg_mix = jnp.concatenate([(1.0 + sc1) * da1_0, (1.0 + tc1) * da1_1], axis=1).reshape(-1)
    dg_mlp = jnp.concatenate([(1.0 + sc2) * da2_0, (1.0 + tc2) * da2_1], axis=1).reshape(-1)
    parts = [loss_part.reshape(-1), dmod, dg_mix, dg_mlp, dg_q.reshape(-1), dg_kv.reshape(-1), dsink.reshape(-1),
             dg_final.reshape(-1), db_qkv.reshape(-1), db_o.reshape(-1)]
    soffs = np.concatenate([[0], np.cumsum([p.size for p in parts])])
    spad = -(-int(soffs[-1]) // 1024) * 1024
    spack = jnp.pad(jnp.concatenate(parts), (0, spad - int(soffs[-1]))).reshape(8, spad // 8)
    sall = _all_gather("gather_small_grads", spack, F32)
    ssum = _sum_devices(sall).reshape(-1)
    tot = [ssum[int(soffs[i]):int(soffs[i + 1])] for i in range(len(parts))]
    loss = tot[0][0]
    nsink = swa_sinks.shape[1]
    small_g = dict(b_ada=tot[1].reshape(b_ada.shape), g_mix=tot[2].reshape(g_mix.shape), g_mlp=tot[3].reshape(g_mlp.shape),
                   mla_g_q=tot[4].reshape(mla_g_q.shape), mla_g_kv=tot[5].reshape(mla_g_kv.shape),
                   swa_sinks=tot[6][:nsink].reshape(swa_sinks.shape), g_final=tot[7].reshape(g_final.shape),
                   swa_b_qkv=lax.dynamic_slice(tot[8], (me * nbq,), (nbq,)).reshape(swa_b_qkv.shape),
                   swa_b_o=lax.dynamic_slice(tot[9], (me * nbo,), (nbo,)).reshape(swa_b_o.shape))
    small_w = dict(b_ada=b_ada, g_mix=g_mix, g_mlp=g_mlp, mla_g_q=mla_g_q, mla_g_kv=mla_g_kv, swa_sinks=swa_sinks,
                   g_final=g_final, swa_b_qkv=swa_b_qkv, swa_b_o=swa_b_o)
    small_m = dict(b_ada=m_b_ada, g_mix=m_g_mix, g_mlp=m_g_mlp, mla_g_q=m_mla_g_q, mla_g_kv=m_mla_g_kv,
                   swa_sinks=m_swa_sinks, g_final=m_g_final, swa_b_qkv=m_swa_b_qkv, swa_b_o=m_swa_b_o)
    small_v = dict(b_ada=v_b_ada, g_mix=v_g_mix, g_mlp=v_g_mlp, mla_g_q=v_mla_g_q, mla_g_kv=v_mla_g_kv,
                   swa_sinks=v_swa_sinks, g_final=v_g_final, swa_b_qkv=v_swa_b_qkv, swa_b_o=v_swa_b_o)
    SMALL = list(small_w)
    woffs = np.concatenate([[0], np.cumsum([small_w[n].size for n in SMALL])])
    wpad = -(-int(woffs[-1]) // 1024) * 1024

    def spack_of(d):
        flat = jnp.concatenate([d[n].reshape(-1) for n in SMALL])
        return jnp.pad(flat, (0, wpad - int(woffs[-1]))).reshape(8, wpad // 8)

    sm = _adamw_small(spack_of(small_w), spack_of(small_g), spack_of(small_m), spack_of(small_v))
    small_d, small_nm, small_nv = (
        {n: a.reshape(-1)[int(woffs[i]):int(woffs[i + 1])].reshape(small_w[n].shape) for i, n in enumerate(SMALL)}
        for a in sm)

    b_off = int(soffs[1])
    dmod_all = sall.reshape(N_DEV, -1)[:, b_off:b_off + L * N_DEV * NC].reshape(N_DEV, L, N_DEV * NC)
    dmod_cols = jnp.moveaxis(lax.dynamic_slice_in_dim(dmod_all, me * NC, NC, axis=2), 0, 1)
    ada_g, ada_d, ada_nm, ada_nv = _ada_bwd_adamw(c_all.T, dmod_cols, w_ada, m_w_ada, v_w_ada)

    order = ["w_ada", "b_ada", "g_mix", "g_mlp", "mla_w_dq", "mla_g_q", "mla_w_uq", "mla_w_dkv", "mla_g_kv",
             "mla_w_ukv", "mla_w_o", "swa_w_qkv", "swa_b_qkv", "swa_sinks", "swa_w_o", "swa_b_o", "w_ff1", "w_ff2", "g_final"]

    def collect(ada, big, small):
        return [ada if n == "w_ada" else (big[n] if n in big else small[n]) for n in order]

    return (loss, dx0.reshape(x.shape), *collect(ada_g, big_g, small_g), *collect(ada_d, big_d, small_d),
            *collect(ada_nm, big_nm, small_nm), *collect(ada_nv, big_nv, small_nv))
```

```python
import jax
import jax.numpy as jnp
import numpy as np
from jax import lax
from jax.experimental import pallas as pl
from jax.experimental.pallas import tpu as pltpu

F32 = jnp.float32
BF16 = jnp.bfloat16
MESH_IDS = pl.DeviceIdType.MESH
N_DEV = 8

MLA_HEADS = 8
QK_NOPE = 128
QK_ROPE = 64
QK_DIM = QK_NOPE + QK_ROPE
V_DIM = 128
KV_LORA = 256
ROPE_THETA = 10000.0
SWA_HEADS = 16
SWA_KV_HEADS = 4
SWA_GROUP = SWA_HEADS // SWA_KV_HEADS
SWA_HEAD_DIM = 64
WINDOW = 128
EPS = 1e-6
LOG2E = 1.4426950408889634

ADAM_LR = 0.001
ADAM_B1 = 0.9
ADAM_B2 = 0.999
ADAM_EPS = 1e-08
ADAM_WD = 0.01
ADAM_STEP = 10

PACK_COLS = 1024
VMEM_LIMIT = 56 << 20
ROW_TILE = 512
ROW_TILE_WIDE = 256
ROW_TILE_BWD = 512
ATT_TILE = 512
ATT_TILE_FWD = 1024
TN_TOKENS = 2048


def _dot(a, b):
    return jnp.dot(a, b, preferred_element_type=F32)


def _dot_nt(a, b):
    return lax.dot_general(a, b, (((1,), (1,)), ((), ())), preferred_element_type=F32)


def _dot_tn(a, b):
    return lax.dot_general(a, b, (((0,), (0,)), ((), ())), preferred_element_type=F32)


def _rstd(x):
    return lax.rsqrt(jnp.mean(x * x, axis=-1, keepdims=True) + EPS)


def _rms_bwd(dn, n, r):
    return r * (dn - n * jnp.mean(dn * n, axis=-1, keepdims=True))


def _modulate(x, g, sc, sh):
    r = _rstd(x)
    return ((x * r) * g) * (1.0 + sc) + sh


def _modulate_bwd(dh, x, g, sc):
    r = _rstd(x)
    n = x * r
    dsh = jnp.sum(dh, axis=0, keepdims=True)
    da = jnp.sum(dh * n, axis=0, keepdims=True)
    dx = _rms_bwd(dh * (g * (1.0 + sc)), n, r)
    return dx, dsh, da


def _acc(ref, val, i):
    @pl.when(i == 0)
    def _():
        ref[...] = val

    @pl.when(i != 0)
    def _():
        ref[...] += val


def _row_spec(shape, tm):
    nd = len(shape)
    return pl.BlockSpec(tuple(shape[:nd - 2]) + (tm, shape[-1]), lambda i: (0,) * (nd - 2) + (i, 0))


def _resident_spec(shape, single_buffer):
    nd = len(shape)
    if single_buffer:
        return pl.BlockSpec(tuple(shape), lambda i: (0,) * nd, pipeline_mode=pl.Buffered(1))
    return pl.BlockSpec(tuple(shape), lambda i: (0,) * nd)


def _rowcall(name, body, tokens, tm, row_in, full_in, row_out, acc_out=()):
    tm = min(tm, tokens)
    in_specs = [_row_spec(a.shape, tm) for a in row_in] + [_resident_spec(a.shape, True) for a in full_in]
    row_specs = [s[1] if isinstance(s, tuple) else _row_spec(s.shape, tm) for s in row_out]
    row_out = [s[0] if isinstance(s, tuple) else s for s in row_out]
    out_specs = row_specs + [_resident_spec(s.shape, False) for s in acc_out]
    return pl.pallas_call(
        body, name=name, grid=(tokens // tm,), in_specs=in_specs, out_specs=out_specs,
        out_shape=list(row_out) + list(acc_out),
        compiler_params=pltpu.CompilerParams(dimension_semantics=("arbitrary",), vmem_limit_bytes=VMEM_LIMIT),
    )(*row_in, *full_in)


def _sds(shape, dtype):
    return jax.ShapeDtypeStruct(tuple(shape), dtype)


def _mla_in_fwd(x, cos, sin, g, sc, sh, w_cat, g_q, w_uqx, g_kv, w_ukv, t):
    S, D = x.shape
    QL = g_q.shape[1]
    H = MLA_HEADS
    t = min(t, S)

    def body(x_ref, cos_ref, sin_ref, g_ref, sc_ref, sh_ref, wcat_ref, gq_ref, wuqx_ref, gkv_ref, wukv_ref,
             h_ref, cqp_ref, cq_ref, ckvp_ref, ckv_ref, q_ref, k_ref, v_ref, vt_ref):
        cs, sn = cos_ref[...], sin_ref[...]
        hb = _modulate(x_ref[...], g_ref[...], sc_ref[...], sh_ref[...]).astype(BF16)
        h_ref[...] = hb
        low = _dot(hb, wcat_ref[...])
        cqp = low[:, :QL]
        cqp_ref[...] = cqp
        cq = ((cqp * _rstd(cqp)) * gq_ref[...]).astype(BF16)
        cq_ref[...] = cq
        ckvp = low[:, QL:QL + KV_LORA]
        ckvp_ref[...] = ckvp
        ckv = ((ckvp * _rstd(ckvp)) * gkv_ref[...]).astype(BF16)
        ckv_ref[...] = ckv
        o = QL + KV_LORA
        kr = (low[:, o:o + QK_ROPE] * cs + low[:, o + QK_ROPE:o + 2 * QK_ROPE] * sn).astype(BF16)
        qx = _dot(cq, wuqx_ref[...])
        kv = _dot(ckv, wukv_ref[...])
        for hd in range(H):
            b = hd * 256
            q_ref[hd, :, 0:QK_NOPE] = qx[:, b:b + QK_NOPE].astype(BF16)
            q_ref[hd, :, QK_NOPE:QK_DIM] = (qx[:, b + 128:b + 192] * cs + qx[:, b + 192:b + 256] * sn).astype(BF16)
            k_ref[hd, :, 0:QK_NOPE] = kv[:, b:b + QK_NOPE].astype(BF16)
            k_ref[hd, :, QK_NOPE:QK_DIM] = kr
            vh = kv[:, b + 128:b + 256]
            v_ref[hd] = vh.astype(BF16)
            vt_ref[hd, 0, 0:V_DIM, :] = vh.T.astype(BF16)
            vt_ref[hd, 0, V_DIM:2 * V_DIM, :] = jnp.ones((V_DIM, x_ref.shape[0]), BF16)

    vt_spec = pl.BlockSpec((H, 1, 2 * V_DIM, t), lambda i: (0, i, 0, 0))
    return _rowcall(
        "mla_in_fwd", body, S, t, [x, cos, sin], [g, sc, sh, w_cat, g_q, w_uqx, g_kv, w_ukv],
        [_sds((S, D), BF16), _sds((S, QL), F32), _sds((S, QL), BF16), _sds((S, KV_LORA), F32), _sds((S, KV_LORA), BF16),
         _sds((H, S, QK_DIM), BF16), _sds((H, S, QK_DIM), BF16), _sds((H, S, V_DIM), BF16),
         (_sds((H, S // t, 2 * V_DIM, t), BF16), vt_spec)])


def _mla_attn_fwd(q, k, vt, t, sends):
    H, S, DQ = q.shape
    DV = V_DIM
    vb = vt.shape[-1]
    t = max(min(t, S), vb)
    nb = S // t
    scale = QK_DIM ** -0.5
    c2 = scale * LOG2E

    ns = len(sends)

    def body(q_ref, k_ref, vt_ref, *rest):
        send_refs, (o_ref, lse_ref), gath_refs = rest[:ns], rest[ns:ns + 2], rest[ns + 2:2 * ns + 2]
        m_s, acc_s, s_buf, send_sems, recv_sems, local_sems = rest[2 * ns + 2:]
        hd, qi = pl.program_id(0), pl.program_id(1)

        def gather():
            return _exchange_copies([lambda j, r=r: r for r in send_refs], gath_refs, send_sems, recv_sems, local_sems)

        @pl.when((hd == 0) & (qi == 0))
        def _():
            _start_exchange(gather())

        m_s[...] = jnp.full_like(m_s, -jnp.inf)
        acc_s[...] = jnp.zeros_like(acc_s)

        def scores(j, slot):
            rows = pl.ds(pl.multiple_of(j * t, t), t)
            s_buf[slot] = _dot_nt(k_ref[0, rows, :], q_ref[0])

        def absorb(j, slot, diagonal):
            s = s_buf[slot]
            if diagonal:
                key = lax.broadcasted_iota(jnp.int32, (t, t), 0)
                qry = lax.broadcasted_iota(jnp.int32, (t, t), 1)
                s = jnp.where(key <= qry, s, -jnp.inf)
            m_prev = m_s[...]
            m_new = jnp.maximum(m_prev, jnp.max(s, axis=0, keepdims=True))
            alpha = jnp.exp2((m_prev - m_new) * c2)
            p = jnp.exp2((s - m_new) * c2)
            pb = p.astype(BF16)
            acc = alpha * acc_s[...]
            for u in range(t // vb):
                acc = acc + _dot(vt_ref[0, j * (t // vb) + u], pb[u * vb:(u + 1) * vb, :])
            acc_s[...] = acc
            m_s[...] = m_new

        def pair(i, carry):
            j = 2 * i
            scores(j + 1, 1)
            absorb(j, 0, False)
            scores(j + 2, 0)
            absorb(j + 1, 1, False)
            return carry

        scores(0, 0)
        lax.fori_loop(0, qi // 2, pair, 0)

        @pl.when(qi % 2 == 0)
        def _():
            absorb(qi, 0, True)

        @pl.when(qi % 2 == 1)
        def _():
            scores(qi, 1)
            absorb(qi - 1, 0, False)
            absorb(qi, 1, True)

        acc = acc_s[...]
        o_ref[...] = (acc[:DV] / acc[DV:]).T.astype(BF16)
        lse_ref[0, 0] = m_s[...] * scale + jnp.log(acc[DV:DV + 1])

        @pl.when((hd == H - 1) & (qi == nb - 1))
        def _():
            _finish_exchange(gather())

    outs = pl.pallas_call(
        body, name="mla_attn_fwd", grid=(H, nb),
        in_specs=[pl.BlockSpec((1, t, DQ), lambda h, i: (h, i, 0)),
                  pl.BlockSpec((1, S, DQ), lambda h, i: (h, 0, 0)),
                  pl.BlockSpec((1, S // vb, 2 * DV, vb), lambda h, i: (h, 0, 0, 0))] + [ANY_SPEC] * ns,
        out_specs=[pl.BlockSpec((t, DV), lambda h, i: (i, h)),
                   pl.BlockSpec((1, 1, 1, t), lambda h, i: (h, i, 0, 0))] + [ANY_SPEC] * ns,
        out_shape=[_sds((S, H * DV), BF16), _sds((H, nb, 1, t), F32)]
        + [_sds((N_DEV,) + a.shape, a.dtype) for a in sends],
        scratch_shapes=[pltpu.VMEM((1, t), F32), pltpu.VMEM((2 * DV, t), F32), pltpu.VMEM((2, t, t), F32)]
        + _comm_sems(ns),
        compiler_params=pltpu.CompilerParams(dimension_semantics=("arbitrary", "arbitrary"),
                                             vmem_limit_bytes=VMEM_LIMIT),
    )(q, k, vt, *sends)
    return outs[0], outs[1], outs[2:]


def _attn_out_fwd(o, x, w_o, b_o, gt, g, sc, sh):
    S, D = x.shape

    def body(o_ref, x_ref, wo_ref, bo_ref, gt_ref, g_ref, sc_ref, sh_ref, y_ref, x1_ref, h_ref):
        y = _dot(o_ref[...], wo_ref[...]) + bo_ref[...]
        y_ref[...] = y.astype(BF16)
        x1 = x_ref[...] + gt_ref[...] * y
        x1_ref[...] = x1
        h_ref[...] = _modulate(x1, g_ref[...], sc_ref[...], sh_ref[...]).astype(BF16)

    return _rowcall("attn_out_fwd", body, S, ROW_TILE, [o, x], [w_o, b_o, gt, g, sc, sh],
                    [_sds((S, D), BF16), _sds((S, D), F32), _sds((S, D), BF16)])


def _mlp_fwd(h, x, w1, w2, gt):
    S, D = x.shape
    FF = w1.shape[1]

    def body(h_ref, x_ref, w1_ref, w2_ref, gt_ref, rl_ref, act_ref, y_ref, x2_ref):
        rl = jnp.maximum(_dot(h_ref[...], w1_ref[...]), 0.0)
        rl_ref[...] = rl.astype(BF16)
        act = (rl * rl).astype(BF16)
        act_ref[...] = act
        y = _dot(act, w2_ref[...])
        y_ref[...] = y.astype(BF16)
        x2_ref[...] = x_ref[...] + gt_ref[...] * y

    return _rowcall("mlp_fwd", body, S, ROW_TILE_WIDE, [h, x], [w1, w2, gt],
                    [_sds((S, FF), BF16), _sds((S, FF), BF16), _sds((S, D), BF16), _sds((S, D), F32)])


def _final_norm_loss(xv, target, g, d_model):
    r = _rstd(xv)
    n = xv * r
    err = n * g - target
    part = 0.5 * jnp.sum(jnp.mean(err * err, axis=-1, keepdims=True), axis=0, keepdims=True)
    dout = err / d_model
    return part, _rms_bwd(dout * g, n, r), jnp.sum(dout * n, axis=0, keepdims=True)


def _mlp_fwd_loss(h, x, w1, w2, gt, target, g_final):
    S, D = x.shape
    FF = w1.shape[1]

    def body(h_ref, x_ref, t_ref, w1_ref, w2_ref, gt_ref, g_ref, rl_ref, act_ref, y_ref, dx_ref, loss_ref, dg_ref):
        i = pl.program_id(0)
        rl = jnp.maximum(_dot(h_ref[...], w1_ref[...]), 0.0)
        rl_ref[...] = rl.astype(BF16)
        act = (rl * rl).astype(BF16)
        act_ref[...] = act
        y = _dot(act, w2_ref[...])
        y_ref[...] = y.astype(BF16)
        part, dx, dg = _final_norm_loss(x_ref[...] + gt_ref[...] * y, t_ref[...], g_ref[...], D)
        dx_ref[...] = dx
        _acc(loss_ref, jnp.broadcast_to(part, loss_ref.shape), i)
        _acc(dg_ref, dg, i)

    return _rowcall("mlp_fwd_loss", body, S, ROW_TILE_WIDE, [h, x, target], [w1, w2, gt, g_final],
                    [_sds((S, FF), BF16), _sds((S, FF), BF16), _sds((S, D), BF16), _sds((S, D), F32)],
                    [_sds((1, 128), F32), _sds((1, D), F32)])


def _swa_in_fwd(x, g, sc, sh, w_qkv, b_qkv):
    S, D = x.shape
    NQ = SWA_HEADS * SWA_HEAD_DIM
    NK = SWA_KV_HEADS * SWA_HEAD_DIM

    def body(x_ref, g_ref, sc_ref, sh_ref, w_ref, b_ref, h_ref, q_ref, k_ref, v_ref):
        hb = _modulate(x_ref[...], g_ref[...], sc_ref[...], sh_ref[...]).astype(BF16)
        h_ref[...] = hb
        qkv = _dot(hb, w_ref[...]) + b_ref[...]
        q_ref[...] = qkv[:, :NQ].astype(BF16)
        k_ref[...] = qkv[:, NQ:NQ + NK].astype(BF16)
        v_ref[...] = qkv[:, NQ + NK:].astype(BF16)

    return _rowcall("swa_in_fwd", body, S, ROW_TILE, [x], [g, sc, sh, w_qkv, b_qkv],
                    [_sds((S, D), BF16), _sds((S, NQ), BF16), _sds((S, NK), BF16), _sds((S, NK), BF16)])


def _alibi_slope(head):
    return float(np.float32(2.0 ** (-8.0 * (head + 1) / SWA_HEADS)))


def _swa_geometry(n):
    W, G = WINDOW, SWA_GROUP
    key = lax.broadcasted_iota(jnp.int32, (2 * W, G * W), 0)
    qry = lax.broadcasted_iota(jnp.int32, (2 * W, G * W), 1) & (W - 1)
    dist = W + qry - key
    valid = (dist >= 0) & (dist < W) & ((n > 0) | (key >= W))
    return dist.astype(F32), valid


def _swa_group(kh, q_ref, sink_ref):
    W, G, Dh = WINDOW, SWA_GROUP, SWA_HEAD_DIM
    heads = [kh * G + g for g in range(G)]
    q4 = jnp.concatenate([q_ref[:, h * Dh:(h + 1) * Dh] for h in heads], axis=0)
    slopes = jnp.concatenate([jnp.full((1, W), _alibi_slope(h), F32) for h in heads], axis=1)
    sinks = jnp.concatenate([jnp.broadcast_to(sink_ref[:, h:h + 1], (1, W)) for h in heads], axis=1)
    return heads, q4, slopes, sinks


def _swa_band_specs(W, nb, cols):
    prev = pl.BlockSpec((W, cols), lambda n: (jnp.maximum(jnp.minimum(n, nb - 1) - 1, 0), 0))
    cur = pl.BlockSpec((W, cols), lambda n: (jnp.minimum(n, nb - 1), 0))
    return prev, cur


def _swa_attn_fwd(q, k, v, sinks):
    S, NQ = q.shape
    NK = k.shape[1]
    W, Dh, G = WINDOW, SWA_HEAD_DIM, SWA_GROUP
    nb = S // W

    def body(q_ref, kp_ref, kc_ref, vp_ref, vc_ref, sink_ref, o_ref, lse_ref):
        distf, valid = _swa_geometry(pl.program_id(0))
        kband = jnp.concatenate([kp_ref[...], kc_ref[...]], axis=0)
        vband_t = jnp.concatenate([vp_ref[...], vc_ref[...]], axis=0).astype(F32).T.astype(BF16)
        outs = []
        for kh in range(SWA_KV_HEADS):
            kb = kband[:, kh * Dh:(kh + 1) * Dh]
            vbt = vband_t[kh * Dh:(kh + 1) * Dh, :]
            heads, q4, slopes, sinks = _swa_group(kh, q_ref, sink_ref)
            s = _dot_nt(kb, q4) * (Dh ** -0.5) - slopes * distf
            s = jnp.where(valid, s, -jnp.inf)
            m = jnp.maximum(jnp.max(s, axis=0, keepdims=True), sinks)
            p = jnp.exp(s - m)
            denom = jnp.sum(p, axis=0, keepdims=True) + jnp.exp(sinks - m)
            out4 = _dot(vbt, (p * (1.0 / denom)).astype(BF16))
            lse4 = m + jnp.log(denom)
            for g, h in enumerate(heads):
                outs.append(out4[:, g * W:(g + 1) * W])
                lse_ref[h:h + 1, :] = lse4[:, g * W:(g + 1) * W]
        o_ref[...] = jnp.concatenate(outs, axis=0).T.astype(BF16)

    kprev, kcur = _swa_band_specs(W, nb, NK)
    return pl.pallas_call(
        body, name="swa_attn_fwd", grid=(nb,),
        in_specs=[pl.BlockSpec((W, NQ), lambda n: (n, 0)), kprev, kcur, kprev, kcur,
                  pl.BlockSpec((1, SWA_HEADS), lambda n: (0, 0))],
        out_specs=[pl.BlockSpec((W, NQ), lambda n: (n, 0)), pl.BlockSpec((SWA_HEADS, W), lambda n: (0, n))],
        out_shape=[_sds((S, NQ), BF16), _sds((SWA_HEADS, S), F32)],
        compiler_params=pltpu.CompilerParams(dimension_semantics=("arbitrary",), vmem_limit_bytes=VMEM_LIMIT),
    )(q, k, k, v, v, sinks)


def _mlp_bwd_a(dx, y, rl, gt, w2):
    S, D = dx.shape
    FF = rl.shape[1]

    def body(dx_ref, y_ref, rl_ref, gt_ref, w2_ref, dy_ref, du_ref, dgt_ref):
        i = pl.program_id(0)
        dxv = dx_ref[...]
        _acc(dgt_ref, jnp.sum(dxv * y_ref[...].astype(F32), axis=0, keepdims=True), i)
        dy = (dxv * gt_ref[...]).astype(BF16)
        dy_ref[...] = dy
        dact = _dot_nt(dy, w2_ref[...])
        du_ref[...] = (dact * (2.0 * rl_ref[...].astype(F32))).astype(BF16)

    return _rowcall("mlp_bwd_a", body, S, ROW_TILE_BWD, [dx, y, rl], [gt, w2],
                    [_sds((S, D), BF16), _sds((S, FF), BF16)], [_sds((1, D), F32)])


def _mlp_bwd_b(du, x, dx, w1, g, sc):
    S, D = x.shape

    def body(du_ref, x_ref, dx_ref, w1_ref, g_ref, sc_ref, dxo_ref, dsh_ref, da_ref):
        i = pl.program_id(0)
        dh = _dot_nt(du_ref[...], w1_ref[...])
        dxn, dsh, da = _modulate_bwd(dh, x_ref[...], g_ref[...], sc_ref[...])
        dxo_ref[...] = dx_ref[...] + dxn
        _acc(dsh_ref, dsh, i)
        _acc(da_ref, da, i)

    return _rowcall("mlp_bwd_b", body, S, ROW_TILE_BWD, [du, x, dx], [w1, g, sc],
                    [_sds((S, D), F32)], [_sds((1, D), F32), _sds((1, D), F32)])


def _attn_out_bwd(dx, y, o, gt, w_o, n_heads):
    S, D = dx.shape
    NO = o.shape[1]
    dh = NO // n_heads
    member = (jnp.arange(NO)[None, :] // dh == jnp.arange(16)[:, None]).astype(BF16)

    def body(dx_ref, y_ref, o_ref, gt_ref, wo_ref, mem_ref, dy_ref, do_ref, dl_ref, dgt_ref, dbo_ref):
        i = pl.program_id(0)
        dxv = dx_ref[...]
        _acc(dgt_ref, jnp.sum(dxv * y_ref[...].astype(F32), axis=0, keepdims=True), i)
        dy = dxv * gt_ref[...]
        _acc(dbo_ref, jnp.sum(dy, axis=0, keepdims=True), i)
        dyb = dy.astype(BF16)
        dy_ref[...] = dyb
        do = _dot_nt(dyb, wo_ref[...])
        do_ref[...] = do.astype(BF16)
        prod = do * o_ref[...].astype(F32)
        hi = prod.astype(BF16)
        lo = (prod - hi.astype(F32)).astype(BF16)
        dl_ref[...] = _dot_nt(mem_ref[...], hi) + _dot_nt(mem_ref[...], lo)

    tm = min(ROW_TILE, S)
    return _rowcall("attn_out_bwd", body, S, ROW_TILE, [dx, y, o], [gt, w_o, member],
                    [_sds((S, D), BF16), _sds((S, NO), BF16),
                     (_sds((16, S), F32), pl.BlockSpec((16, tm), lambda i: (0, i)))],
                    [_sds((1, D), F32), _sds((1, D), F32)])


def _mla_attn_bwd(q, k, v, do, lse, delta, t, gblks):
    H, S, DQ = q.shape
    DV = V_DIM
    t = min(t, S // 2)
    tk = 2 * t
    nq, nk = S // t, S // tk
    scale = QK_DIM ** -0.5
    c2 = scale * LOG2E

    ng = len(gblks)

    def body(q_ref, k_ref, v_ref, do_ref, lse_ref, dl_ref, *rest):
        g_refs, (dq_ref, dk_ref, dv_ref), recv_refs = rest[:ng], rest[ng:ng + 3], rest[ng + 3:2 * ng + 3]
        dk_s, dv_s, s_buf, dp_buf, send_sems, recv_sems, local_sems = rest[2 * ng + 3:]
        hd, kj = pl.program_id(0), pl.program_id(1)

        def scatter():
            return _exchange_copies([lambda j, r=r: r.at[j] for r in g_refs], recv_refs, send_sems, recv_sems,
                                    local_sems)

        @pl.when((hd == 0) & (kj == 0))
        def _():
            _start_exchange(scatter())

        @pl.when(kj == 0)
        def _():
            dq_ref[...] = jnp.zeros_like(dq_ref)

        dk_s[...] = jnp.zeros_like(dk_s)
        dv_s[...] = jnp.zeros_like(dv_s)

        def products(i, slot):
            rows = pl.ds(pl.multiple_of(i * t, t), t)
            s_buf[slot] = _dot_nt(k_ref[0], q_ref[0, rows, :])
            dp_buf[slot] = _dot_nt(v_ref[0], do_ref[rows, :])

        def absorb(i, slot, diagonal):
            rows = pl.ds(pl.multiple_of(i * t, t), t)
            qb, dob = q_ref[0, rows, :], do_ref[rows, :]
            p = jnp.exp2(s_buf[slot] * c2 - lse_ref[0, i])
            if diagonal is not None:
                key = lax.broadcasted_iota(jnp.int32, (tk, t), 0)
                qry = lax.broadcasted_iota(jnp.int32, (tk, t), 1) + diagonal * t
                p = jnp.where(key <= qry, p, 0.0)
            dv_s[...] += _dot(p.astype(BF16), dob)
            ds = (p * (dp_buf[slot] - dl_ref[0, i])).astype(BF16)
            dk_s[...] += _dot(ds, qb)
            dq_ref[0, rows, :] += _dot_tn(ds, k_ref[0])

        first = 2 * kj + 2
        n_off = nq - first

        def pair(i, carry):
            u = 2 * i
            products(first + u + 1, 1)
            absorb(first + u, 0, None)
            products(jnp.where(u + 2 < n_off, first + u + 2, 2 * kj), 0)
            absorb(first + u + 1, 1, None)
            return carry

        products(jnp.where(n_off > 0, first, 2 * kj), 0)
        lax.fori_loop(0, n_off // 2, pair, 0)
        products(2 * kj + 1, 1)
        absorb(2 * kj, 0, 0)
        absorb(2 * kj + 1, 1, 1)

        dk_ref[0] = (dk_s[...] * scale).astype(BF16)
        dv_ref[0] = dv_s[...].astype(BF16)

        @pl.when((hd == H - 1) & (kj == nk - 1))
        def _():
            _finish_exchange(scatter())

    rowspec = pl.BlockSpec((1, nq, 1, t), lambda h, j: (h, 0, 0, 0))
    outs = pl.pallas_call(
        body, name="mla_attn_bwd", grid=(H, nk),
        in_specs=[pl.BlockSpec((1, S, DQ), lambda h, j: (h, 0, 0)),
                  pl.BlockSpec((1, tk, DQ), lambda h, j: (h, j, 0)),
                  pl.BlockSpec((1, tk, DV), lambda h, j: (h, j, 0)),
                  pl.BlockSpec((S, DV), lambda h, j: (0, h)), rowspec, rowspec] + [ANY_SPEC] * ng,
        out_specs=[pl.BlockSpec((1, S, DQ), lambda h, j: (h, 0, 0)),
                   pl.BlockSpec((1, tk, DQ), lambda h, j: (h, j, 0)),
                   pl.BlockSpec((1, tk, DV), lambda h, j: (h, j, 0))] + [ANY_SPEC] * ng,
        out_shape=[_sds((H, S, DQ), F32), _sds((H, S, DQ), BF16), _sds((H, S, DV), BF16)]
        + [_sds(g.shape, g.dtype) for g in gblks],
        scratch_shapes=[pltpu.VMEM((tk, DQ), F32), pltpu.VMEM((tk, DV), F32), pltpu.VMEM((2, tk, t), F32),
                        pltpu.VMEM((2, tk, t), F32)] + _comm_sems(ng),
        compiler_params=pltpu.CompilerParams(dimension_semantics=("arbitrary", "arbitrary"),
                                             vmem_limit_bytes=VMEM_LIMIT),
    )(q, k, v, do, lse, delta, *gblks)
    return outs[0], outs[1], outs[2], outs[3:]


def _swa_attn_bwd(q, k, v, do, lse, delta, sinks):
    S, NQ = q.shape
    NK = k.shape[1]
    W, Dh, G = WINDOW, SWA_HEAD_DIM, SWA_GROUP
    nb = S // W

    def body(q_ref, kp_ref, kc_ref, vp_ref, vc_ref, do_ref, lse_ref, dl_ref, sink_ref,
             dq_ref, dk_ref, dv_ref, dsink_ref, dkc_s, dvc_s):
        n = pl.program_id(0)

        @pl.when(n == 0)
        def _():
            dkc_s[...] = jnp.zeros_like(dkc_s)
            dvc_s[...] = jnp.zeros_like(dvc_s)
            dsink_ref[...] = jnp.zeros_like(dsink_ref)

        @pl.when(n < nb)
        def _():
            distf, valid = _swa_geometry(n)
            kband = jnp.concatenate([kp_ref[...], kc_ref[...]], axis=0)
            vband = jnp.concatenate([vp_ref[...], vc_ref[...]], axis=0)
            kband_t = kband.astype(F32).T.astype(BF16)
            dq_t = []
            for kh in range(SWA_KV_HEADS):
                ck = slice(kh * Dh, (kh + 1) * Dh)
                kb, vb, kbt = kband[:, ck], vband[:, ck], kband_t[ck, :]
                heads, q4, slopes, sinks = _swa_group(kh, q_ref, sink_ref)
                do4 = jnp.concatenate([do_ref[:, h * Dh:(h + 1) * Dh] for h in heads], axis=0)
                lse4 = jnp.concatenate([lse_ref[h:h + 1, :] for h in heads], axis=1)
                dl4 = jnp.concatenate([dl_ref[h:h + 1, :] for h in heads], axis=1)
                s = _dot_nt(kb, q4) * (Dh ** -0.5) - slopes * distf
                p = jnp.where(valid, jnp.exp(s - lse4), 0.0)
                dvb = _dot(p.astype(BF16), do4)
                dp = _dot_nt(vb, do4)
                dsb = ((p * (dp - dl4)) * (Dh ** -0.5)).astype(BF16)
                dq4 = _dot(kbt, dsb)
                dkb = _dot(dsb, q4)
                dsk4 = jnp.exp(sinks - lse4) * dl4
                for g, h in enumerate(heads):
                    dq_t.append(dq4[:, g * W:(g + 1) * W])
                    dsink_ref[:, h:h + 1] += -jnp.sum(dsk4[:, g * W:(g + 1) * W], axis=1, keepdims=True)
                dk_ref[:, ck] = (dkc_s[:, ck] + dkb[:W]).astype(BF16)
                dv_ref[:, ck] = (dvc_s[:, ck] + dvb[:W]).astype(BF16)
                dkc_s[:, ck] = dkb[W:]
                dvc_s[:, ck] = dvb[W:]
            dq_ref[...] = jnp.concatenate(dq_t, axis=0).T.astype(BF16)

        @pl.when(n == nb)
        def _():
            dk_ref[...] = dkc_s[...].astype(BF16)
            dv_ref[...] = dvc_s[...].astype(BF16)

    kprev, kcur = _swa_band_specs(W, nb, NK)
    qspec = lambda cols: pl.BlockSpec((W, cols), lambda n: (jnp.minimum(n, nb - 1), 0))
    kvout = pl.BlockSpec((W, NK), lambda n: (jnp.maximum(n - 1, 0), 0))
    rowspec = pl.BlockSpec((SWA_HEADS, W), lambda n: (0, jnp.minimum(n, nb - 1)))
    return pl.pallas_call(
        body, name="swa_attn_bwd", grid=(nb + 1,),
        in_specs=[qspec(NQ), kprev, kcur, kprev, kcur, qspec(NQ), rowspec, rowspec,
                  pl.BlockSpec((1, SWA_HEADS), lambda n: (0, 0))],
        out_specs=[qspec(NQ), kvout, kvout, pl.BlockSpec((1, 128), lambda n: (0, 0))],
        out_shape=[_sds((S, NQ), BF16), _sds((S, NK), BF16), _sds((S, NK), BF16), _sds((1, 128), F32)],
        scratch_shapes=[pltpu.VMEM((W, NK), F32), pltpu.VMEM((W, NK), F32)],
        compiler_params=pltpu.CompilerParams(dimension_semantics=("arbitrary",), vmem_limit_bytes=VMEM_LIMIT),
    )(q, k, k, v, v, do, lse, delta, sinks)


def _swa_in_bwd(dq, dk, dv, x, dx, w_qkv, g, sc):
    S, D = x.shape
    N = w_qkv.shape[1]

    def body(dq_ref, dk_ref, dv_ref, x_ref, dx_ref, w_ref, g_ref, sc_ref, dqkv_ref, dxo_ref, db_ref, dsh_ref, da_ref):
        i = pl.program_id(0)
        dqkv = jnp.concatenate([dq_ref[...], dk_ref[...], dv_ref[...]], axis=1)
        dqkv_ref[...] = dqkv
        _acc(db_ref, jnp.sum(dqkv.astype(F32), axis=0, keepdims=True), i)
        dh = _dot_nt(dqkv, w_ref[...])
        dxn, dsh, da = _modulate_bwd(dh, x_ref[...], g_ref[...], sc_ref[...])
        dxo_ref[...] = dx_ref[...] + dxn
        _acc(dsh_ref, dsh, i)
        _acc(da_ref, da, i)

    return _rowcall("swa_in_bwd", body, S, ROW_TILE, [dq, dk, dv, x, dx], [w_qkv, g, sc],
                    [_sds((S, N), BF16), _sds((S, D), F32)],
                    [_sds((1, N), F32), _sds((1, D), F32), _sds((1, D), F32)])


def _mla_in_bwd(dq, dk, dv, cos, sin, cqp, ckvp, x, dx, w_uqx, g_q, w_ukv, g_kv, w_cat, g, sc):
    S, D = x.shape
    H = MLA_HEADS
    QL = g_q.shape[1]
    NX = w_uqx.shape[1]
    NC = w_cat.shape[1]

    def body(dq_ref, dk_ref, dv_ref, cos_ref, sin_ref, cqp_ref, ckvp_ref, x_ref, dx_ref,
             wuqx_ref, gq_ref, wukv_ref, gkv_ref, wcat_ref, g_ref, sc_ref,
             dqx_ref, dkv_ref, dcat_ref, dxo_ref, dgq_ref, dgkv_ref, dsh_ref, da_ref):
        i = pl.program_id(0)
        cs, sn = cos_ref[...], sin_ref[...]
        dkr = jnp.zeros(cs.shape, F32)
        for hd in range(H):
            b = hd * 256
            dqh = dq_ref[hd] * (QK_DIM ** -0.5)
            dqx_ref[:, b:b + QK_NOPE] = dqh[:, :QK_NOPE].astype(BF16)
            dqx_ref[:, b + 128:b + 192] = (dqh[:, QK_NOPE:] * cs).astype(BF16)
            dqx_ref[:, b + 192:b + 256] = (dqh[:, QK_NOPE:] * sn).astype(BF16)
            dkh = dk_ref[hd]
            dkv_ref[:, b:b + QK_NOPE] = dkh[:, :QK_NOPE]
            dkv_ref[:, b + 128:b + 256] = dv_ref[hd]
            dkr = dkr + dkh[:, QK_NOPE:].astype(F32)
        dcq = _dot_nt(dqx_ref[...], wuqx_ref[...])
        cqp = cqp_ref[...]
        rq = _rstd(cqp)
        nq = cqp * rq
        _acc(dgq_ref, jnp.sum(dcq * nq, axis=0, keepdims=True), i)
        dcqp = _rms_bwd(dcq * gq_ref[...], nq, rq)
        dckv = _dot_nt(dkv_ref[...], wukv_ref[...])
        ckvp = ckvp_ref[...]
        rk = _rstd(ckvp)
        nk = ckvp * rk
        _acc(dgkv_ref, jnp.sum(dckv * nk, axis=0, keepdims=True), i)
        dckvp = _rms_bwd(dckv * gkv_ref[...], nk, rk)
        dcat_ref[:, :QL] = dcqp.astype(BF16)
        dcat_ref[:, QL:QL + KV_LORA] = dckvp.astype(BF16)
        o = QL + KV_LORA
        dcat_ref[:, o:o + QK_ROPE] = (dkr * cs).astype(BF16)
        dcat_ref[:, o + QK_ROPE:o + 2 * QK_ROPE] = (dkr * sn).astype(BF16)
        dh = _dot_nt(dcat_ref[...], wcat_ref[...])
        dxn, dsh, da = _modulate_bwd(dh, x_ref[...], g_ref[...], sc_ref[...])
        dxo_ref[...] = dx_ref[...] + dxn
        _acc(dsh_ref, dsh, i)
        _acc(da_ref, da, i)

    return _rowcall("mla_in_bwd", body, S, ROW_TILE_WIDE, [dq, dk, dv, cos, sin, cqp, ckvp, x, dx],
                    [w_uqx, g_q, w_ukv, g_kv, w_cat, g, sc],
                    [_sds((S, NX), BF16), _sds((S, NX), BF16), _sds((S, NC), BF16), _sds((S, D), F32)],
                    [_sds((1, QL), F32), _sds((1, KV_LORA), F32), _sds((1, D), F32), _sds((1, D), F32)])


def _matmul_tn(name, a, b, out_dtype=F32, column_blocks=False):
    S, K = a.shape
    N = b.shape[1]
    tk, tn, ts = min(K, 1024), min(N, 1024), min(S, TN_TOKENS)
    if column_blocks:
        tn = N // N_DEV
    if N % tn:
        tn = 512 if N % 512 == 0 else (384 if N % 384 == 0 else 128)
    if K % tk:
        tk = 512 if K % 512 == 0 else (384 if K % 384 == 0 else 128)
    ns = S // ts

    def body(a_ref, b_ref, o_ref, *scratch):
        acc_ref = scratch[0] if scratch else o_ref
        _acc(acc_ref, _dot_tn(a_ref[...], b_ref[...]), pl.program_id(2))
        if scratch:
            @pl.when(pl.program_id(2) == ns - 1)
            def _():
                o_ref[...] = acc_ref[...].astype(out_dtype)

    if column_blocks:
        out_spec = pl.BlockSpec((None, tk, tn), lambda i, j, s: (j, i, 0))
        out_shape = _sds((N_DEV, K, tn), out_dtype)
    else:
        out_spec = pl.BlockSpec((tk, tn), lambda i, j, s: (i, j))
        out_shape = _sds((K, N), out_dtype)
    return pl.pallas_call(
        body, name=name, grid=(K // tk, N // tn, ns),
        in_specs=[pl.BlockSpec((ts, tk), lambda i, j, s: (s, i)), pl.BlockSpec((ts, tn), lambda i, j, s: (s, j))],
        out_specs=out_spec, out_shape=out_shape,
        scratch_shapes=[] if out_dtype == F32 else [pltpu.VMEM((tk, tn), F32)],
        compiler_params=pltpu.CompilerParams(dimension_semantics=("parallel", "parallel", "arbitrary"),
                                             vmem_limit_bytes=VMEM_LIMIT),
    )(a, b)


def _silu(c):
    return c * jax.nn.sigmoid(c)


def _ada_fwd(c_all, w_ada):
    L, D, NC = w_ada.shape

    def body(c_ref, w_ref, o_ref):
        cond = _silu(c_ref[...]).astype(BF16)
        o_ref[0] = _dot(cond, w_ref[0].astype(BF16))

    return pl.pallas_call(
        body, name="ada_fwd", grid=(L,),
        in_specs=[pl.BlockSpec(c_all.shape, lambda l: (0, 0)), pl.BlockSpec((1, D, NC), lambda l: (l, 0, 0))],
        out_specs=pl.BlockSpec((1, N_DEV, NC), lambda l: (l, 0, 0)),
        out_shape=_sds((L, N_DEV, NC), F32),
        compiler_params=pltpu.CompilerParams(dimension_semantics=("arbitrary",), vmem_limit_bytes=VMEM_LIMIT),
    )(c_all, w_ada)


def _adamw(w, g, m, v):
    m = ADAM_B1 * m + (1.0 - ADAM_B1) * g
    v = ADAM_B2 * v + (1.0 - ADAM_B2) * (g * g)
    m_hat = m / (1.0 - ADAM_B1 ** ADAM_STEP)
    v_hat = v / (1.0 - ADAM_B2 ** ADAM_STEP)
    delta = -ADAM_LR * (m_hat / (jnp.sqrt(v_hat) + ADAM_EPS) + ADAM_WD * w)
    return delta, m, v


def _ada_bwd_adamw(c_all_t, dmod_cols, w, m, v):
    L, D, NC = w.shape
    tr = min(D, 256)

    def body(ct_ref, dm_ref, w_ref, m_ref, v_ref, g_ref, d_ref, mo_ref, vo_ref):
        cond_t = _silu(ct_ref[...])
        dm = dm_ref[0]
        g = cond_t[:, 0:1] * dm[0:1, :]
        for b in range(1, N_DEV):
            g = g + cond_t[:, b:b + 1] * dm[b:b + 1, :]
        g_ref[0] = g
        d_ref[0], mo_ref[0], vo_ref[0] = _adamw(w_ref[0], g, m_ref[0], v_ref[0])

    wspec = pl.BlockSpec((1, tr, NC), lambda l, r: (l, r, 0))
    return pl.pallas_call(
        body, name="ada_bwd_adamw", grid=(L, D // tr),
        in_specs=[pl.BlockSpec((tr, N_DEV), lambda l, r: (r, 0)),
                  pl.BlockSpec((1, N_DEV, NC), lambda l, r: (l, 0, 0)), wspec, wspec, wspec],
        out_specs=[wspec] * 4, out_shape=[_sds(w.shape, F32)] * 4,
        compiler_params=pltpu.CompilerParams(dimension_semantics=("parallel", "parallel"), vmem_limit_bytes=VMEM_LIMIT),
    )(c_all_t, dmod_cols, w, m, v)


def _sum_devices(x):
    def body(x_ref, o_ref):
        s = x_ref[0]
        for j in range(1, N_DEV):
            s = s + x_ref[j]
        o_ref[...] = s

    return pl.pallas_call(body, name="sum_devices", out_shape=_sds(x.shape[1:], F32))(x)


def _adamw_small(w, g, m, v):
    def body(w_ref, g_ref, m_ref, v_ref, d_ref, mo_ref, vo_ref):
        d_ref[...], mo_ref[...], vo_ref[...] = _adamw(w_ref[...], g_ref[...], m_ref[...], v_ref[...])

    return pl.pallas_call(body, name="adamw_small", out_shape=[_sds(w.shape, F32)] * 3)(w, g, m, v)


def _me():
    return lax.axis_index("x") * 4 + lax.axis_index("y") * 2 + lax.axis_index("c")


def _peer(k):
    x, y, c = lax.axis_index("x"), lax.axis_index("y"), lax.axis_index("c")
    px = 1 - x if k & 4 else x
    py = 1 - y if k & 2 else y
    pc = 1 - c if k & 1 else c
    return (px, py, pc), px * 4 + py * 2 + pc


VMEM_SPEC = pl.BlockSpec(memory_space=pltpu.VMEM)
ANY_SPEC = pl.BlockSpec(memory_space=pl.ANY)
def _comm_sems(n):
    return [pltpu.SemaphoreType.DMA((n * (N_DEV - 1),)), pltpu.SemaphoreType.DMA((n * (N_DEV - 1),)),
            pltpu.SemaphoreType.DMA((n,))]


def _exchange_copies(srcs_of, dst_refs, send_sems, recv_sems, local_sems):
    me = _me()
    local, sends, recvs = [], [], []
    for a, (src_of, dst_ref) in enumerate(zip(srcs_of, dst_refs)):
        local.append(pltpu.make_async_copy(src_of(me), dst_ref.at[me], local_sems.at[a]))
        for k in range(1, N_DEV):
            dev, pj = _peer(k)
            i = a * (N_DEV - 1) + k - 1
            sems = dict(send_sem=send_sems.at[i], recv_sem=recv_sems.at[i], device_id=dev, device_id_type=MESH_IDS)
            sends.append(pltpu.make_async_remote_copy(src_ref=src_of(pj), dst_ref=dst_ref.at[me], **sems))
            recvs.append(pltpu.make_async_remote_copy(src_ref=src_of(pj), dst_ref=dst_ref.at[pj], **sems))
    return local, sends, recvs


def _start_exchange(copies):
    local, sends, _ = copies
    for cp in local + sends:
        cp.start()


def _finish_exchange(copies):
    local, sends, recvs = copies
    for cp in recvs:
        cp.wait_recv()
    for cp in sends:
        cp.wait_send()
    for cp in local:
        cp.wait()


def _sum_adamw(recv, w, m, v):
    shape = w.shape
    C = shape[-1]
    R = w.size // C
    rows = max(d for d in range(16, min(R, 512) + 1, 16) if R % d == 0 and d * C <= 256 * 1024)

    def body(r_ref, w_ref, m_ref, v_ref, go_ref, d_ref, mo_ref, vo_ref):
        g = r_ref[0].astype(F32)
        for j in range(1, N_DEV):
            g = g + r_ref[j].astype(F32)
        go_ref[...] = g
        d_ref[...], mo_ref[...], vo_ref[...] = _adamw(w_ref[...], g, m_ref[...], v_ref[...])

    spec = pl.BlockSpec((rows, C), lambda i: (i, 0))
    outs = pl.pallas_call(
        body, name="sum_adamw", grid=(R // rows,),
        in_specs=[pl.BlockSpec((N_DEV, rows, C), lambda i: (0, i, 0)), spec, spec, spec],
        out_specs=[spec] * 4, out_shape=[_sds((R, C), F32)] * 4,
        compiler_params=pltpu.CompilerParams(dimension_semantics=("parallel",), vmem_limit_bytes=VMEM_LIMIT),
    )(recv.reshape(N_DEV, R, C), w.reshape(R, C), m.reshape(R, C), v.reshape(R, C))
    return [o.reshape(shape) for o in outs]


def _all_gather(name, x, out_dtype):
    R, C = x.shape
    cast = out_dtype != x.dtype

    def body(x_ref, out_ref, buf, send_sems, recv_sems, local_sem):
        me = _me()
        if cast:
            buf[...] = x_ref[...].astype(out_dtype)
            src = buf
        else:
            src = x_ref
        local = pltpu.make_async_copy(src, out_ref.at[me], local_sem)
        local.start()
        sends = []
        for k in range(1, N_DEV):
            dev, _ = _peer(k)
            cp = pltpu.make_async_remote_copy(src_ref=src, dst_ref=out_ref.at[me], send_sem=send_sems.at[k - 1],
                                              recv_sem=recv_sems.at[k - 1], device_id=dev, device_id_type=MESH_IDS)
            cp.start()
            sends.append(cp)
        for k in range(1, N_DEV):
            dev, pj = _peer(k)
            pltpu.make_async_remote_copy(src_ref=src, dst_ref=out_ref.at[pj], send_sem=send_sems.at[k - 1],
                                         recv_sem=recv_sems.at[k - 1], device_id=dev, device_id_type=MESH_IDS).wait_recv()
        for cp in sends:
            cp.wait_send()
        local.wait()

    return pl.pallas_call(
        body, name=name, in_specs=[VMEM_SPEC], out_specs=ANY_SPEC, out_shape=_sds((N_DEV, R, C), out_dtype),
        scratch_shapes=[pltpu.VMEM((R, C) if cast else (8, 128), out_dtype),
                        pltpu.SemaphoreType.DMA((N_DEV - 1,)), pltpu.SemaphoreType.DMA((N_DEV - 1,)),
                        pltpu.SemaphoreType.DMA(())],
        compiler_params=pltpu.CompilerParams(vmem_limit_bytes=VMEM_LIMIT),
    )(x)


def _all_to_all(name, x):
    _, R, C = x.shape

    def body(x_ref, out_ref, send_sems, recv_sems, local_sem):
        me = _me()
        local = pltpu.make_async_copy(x_ref.at[me], out_ref.at[me], local_sem)
        local.start()
        sends = []
        for k in range(1, N_DEV):
            dev, pj = _peer(k)
            cp = pltpu.make_async_remote_copy(src_ref=x_ref.at[pj], dst_ref=out_ref.at[me], send_sem=send_sems.at[k - 1],
                                              recv_sem=recv_sems.at[k - 1], device_id=dev, device_id_type=MESH_IDS)
            cp.start()
            sends.append(cp)
        for k in range(1, N_DEV):
            dev, pj = _peer(k)
            pltpu.make_async_remote_copy(src_ref=x_ref.at[pj], dst_ref=out_ref.at[pj], send_sem=send_sems.at[k - 1],
                                         recv_sem=recv_sems.at[k - 1], device_id=dev, device_id_type=MESH_IDS).wait_recv()
        for cp in sends:
            cp.wait_send()
        local.wait()

    return pl.pallas_call(
        body, name=name, in_specs=[VMEM_SPEC], out_specs=VMEM_SPEC, out_shape=_sds(x.shape, x.dtype),
        scratch_shapes=[pltpu.SemaphoreType.DMA((N_DEV - 1,)), pltpu.SemaphoreType.DMA((N_DEV - 1,)),
                        pltpu.SemaphoreType.DMA(())],
    )(x)


def _reduce_scatter_adamw(name, gblk, w, m, v):
    _, R, C = gblk.shape
    rows = 8
    for cand in (136, 128, 80, 64, 40, 32, 16, 8):
        if R % cand == 0:
            rows = cand
            break

    def body(g_ref, w_ref, m_ref, v_ref, go_ref, d_ref, mo_ref, vo_ref, recv, send_sems, recv_sems, local_sem):
        me = _me()
        local = pltpu.make_async_copy(g_ref.at[me], recv.at[me], local_sem)
        local.start()
        sends = []
        for k in range(1, N_DEV):
            dev, pj = _peer(k)
            cp = pltpu.make_async_remote_copy(src_ref=g_ref.at[pj], dst_ref=recv.at[me], send_sem=send_sems.at[k - 1],
                                              recv_sem=recv_sems.at[k - 1], device_id=dev, device_id_type=MESH_IDS)
            cp.start()
            sends.append(cp)
        for k in range(1, N_DEV):
            dev, pj = _peer(k)
            pltpu.make_async_remote_copy(src_ref=g_ref.at[pj], dst_ref=recv.at[pj], send_sem=send_sems.at[k - 1],
                                         recv_sem=recv_sems.at[k - 1], device_id=dev, device_id_type=MESH_IDS).wait_recv()
        local.wait()

        def chunk(i, carry):
            r = pl.ds(pl.multiple_of(i * rows, rows), rows)
            g = recv[0, r, :].astype(F32)
            for j in range(1, N_DEV):
                g = g + recv[j, r, :].astype(F32)
            go_ref[r, :] = g
            d_ref[r, :], mo_ref[r, :], vo_ref[r, :] = _adamw(w_ref[r, :], g, m_ref[r, :], v_ref[r, :])
            return carry

        lax.fori_loop(0, R // rows, chunk, 0)
        for cp in sends:
            cp.wait_send()

    return pl.pallas_call(
        body, name=name, in_specs=[ANY_SPEC, VMEM_SPEC, VMEM_SPEC, VMEM_SPEC], out_specs=[VMEM_SPEC] * 4,
        out_shape=[_sds((R, C), F32)] * 4,
        scratch_shapes=[pltpu.VMEM((N_DEV, R, C), BF16), pltpu.SemaphoreType.DMA((N_DEV - 1,)),
                        pltpu.SemaphoreType.DMA((N_DEV - 1,)), pltpu.SemaphoreType.DMA(())],
        compiler_params=pltpu.CompilerParams(vmem_limit_bytes=VMEM_LIMIT),
    )(gblk, w, m, v)


FIRST_WEIGHTS = ["mla_w_dq", "mla_w_uq", "mla_w_dkv", "mla_w_ukv"]
LATE_WEIGHTS = ["mla_w_o", "swa_w_qkv", "swa_w_o", "w_ff1", "w_ff2"]
ROW_SHARDED = {"mla_w_dq", "mla_w_dkv", "mla_w_o", "swa_w_o", "w_ff2"}


def _unblock(name, blocks):
    sh = blocks.shape[1:]
    if name in ROW_SHARDED:
        return jnp.moveaxis(blocks, 0, 1).reshape(sh[0], N_DEV * sh[1], sh[2])
    return jnp.moveaxis(blocks, 0, 2).reshape(sh[0], sh[1], N_DEV * sh[2])


def _block(name, full):
    L, K, N = full.shape
    if name in ROW_SHARDED:
        return jnp.moveaxis(full.reshape(L, N_DEV, K // N_DEV, N), 1, 0)
    return jnp.moveaxis(full.reshape(L, K, N_DEV, N // N_DEV), 2, 0)


def _rot_cols(w):
    half = QK_ROPE // 2
    return jnp.concatenate([-w[..., half:], w[..., :half]], axis=-1)


def _unrot_cols(gw):
    half = QK_ROPE // 2
    return jnp.concatenate([gw[..., half:], -gw[..., :half]], axis=-1)


def _row(v):
    return v.reshape(1, -1)


def _mlp_block_bwd(dx, sv, w1, w2, g, sc, gt):
    dy, du, dgt = _mlp_bwd_a(dx, sv["y2"], sv["rl"], gt, w2)
    dw2 = _matmul_tn("dw_ff2", sv["act"], dy, BF16)
    dw1 = _matmul_tn("dw_ff1", sv["h2"], du, BF16, column_blocks=True)
    dxo, dsh, da = _mlp_bwd_b(du, sv["x1"], dx, w1, g, sc)
    return dxo, dw1, dw2, dsh, da, dgt


def kernel(x, c, positions, w_ada, b_ada, g_mix, g_mlp, mla_w_dq, mla_g_q, mla_w_uq, mla_w_dkv, mla_g_kv, mla_w_ukv, mla_w_o, swa_w_qkv, swa_b_qkv, swa_sinks, swa_w_o, swa_b_o, w_ff1, w_ff2, g_final, loss_target, m_w_ada, m_b_ada, m_g_mix, m_g_mlp, m_mla_w_dq, m_mla_g_q, m_mla_w_uq, m_mla_w_dkv, m_mla_g_kv, m_mla_w_ukv, m_mla_w_o, m_swa_w_qkv, m_swa_b_qkv, m_swa_sinks, m_swa_w_o, m_swa_b_o, m_w_ff1, m_w_ff2, m_g_final, v_w_ada, v_b_ada, v_g_mix, v_g_mlp, v_mla_w_dq, v_mla_g_q, v_mla_w_uq, v_mla_w_dkv, v_mla_g_kv, v_mla_w_ukv, v_mla_w_o, v_swa_w_qkv, v_swa_b_qkv, v_swa_sinks, v_swa_w_o, v_swa_b_o, v_w_ff1, v_w_ff2, v_g_final):
    S, D = x.shape[1], x.shape[2]
    me = _me()
    x0 = x[0]
    target = loss_target[0]
    big_w = dict(mla_w_dq=mla_w_dq, mla_w_uq=mla_w_uq, mla_w_dkv=mla_w_dkv, mla_w_ukv=mla_w_ukv, mla_w_o=mla_w_o,
                 swa_w_qkv=swa_w_qkv, swa_w_o=swa_w_o, w_ff1=w_ff1, w_ff2=w_ff2)
    big_m = dict(mla_w_dq=m_mla_w_dq, mla_w_uq=m_mla_w_uq, mla_w_dkv=m_mla_w_dkv, mla_w_ukv=m_mla_w_ukv,
                 mla_w_o=m_mla_w_o, swa_w_qkv=m_swa_w_qkv, swa_w_o=m_swa_w_o, w_ff1=m_w_ff1, w_ff2=m_w_ff2)
    big_v = dict(mla_w_dq=v_mla_w_dq, mla_w_uq=v_mla_w_uq, mla_w_dkv=v_mla_w_dkv, mla_w_ukv=v_mla_w_ukv,
                 mla_w_o=v_mla_w_o, swa_w_qkv=v_swa_w_qkv, swa_w_o=v_swa_w_o, w_ff1=v_w_ff1, w_ff2=v_w_ff2)
    groups = {"first": FIRST_WEIGHTS}
    wrows = {n: -(-big_w[n].size // (PACK_COLS * 16)) * 16 for n in FIRST_WEIGHTS}
    offs = {g: np.concatenate([[0], np.cumsum([wrows[n] for n in names])]).astype(int) for g, names in groups.items()}

    def as_rows(n, a, lead=()):
        flat = a.reshape(lead + (-1,))
        pad = wrows[n] * PACK_COLS - flat.shape[-1]
        if pad:
            flat = jnp.pad(flat, ((0, 0),) * len(lead) + ((0, pad),))
        return flat.reshape(lead + (wrows[n], PACK_COLS))

    def pack(g, d):
        return jnp.concatenate([as_rows(n, d[n]) for n in groups[g]], axis=0)

    def pack_blocks(g, gfull):
        return jnp.concatenate([as_rows(n, _block(n, gfull[n]).astype(BF16), (N_DEV,)) for n in groups[g]], axis=1)

    def unpack(g, packed, lead=()):
        out = {}
        for i, n in enumerate(groups[g]):
            part = packed[..., int(offs[g][i]):int(offs[g][i + 1]), :].reshape(lead + (-1,))
            out[n] = part[..., :big_w[n].size].reshape(lead + big_w[n].shape)
        return out

    gathered = _all_gather("gather_weights", pack("first", big_w), BF16)
    wfull = {n: _unblock(n, b) for n, b in unpack("first", gathered, (N_DEV,)).items()}
    w_dq, w_dkv = wfull["mla_w_dq"][0], wfull["mla_w_dkv"][0]
    w_cat = jnp.concatenate([w_dq, w_dkv, _rot_cols(w_dkv[:, KV_LORA:])], axis=1)
    QL = w_dq.shape[1]
    w_uq = wfull["mla_w_uq"][0].reshape(QL, MLA_HEADS, QK_DIM)
    w_uqx = jnp.concatenate([w_uq, _rot_cols(w_uq[..., QK_NOPE:])], axis=-1).reshape(QL, MLA_HEADS * 256)
    w_ukv = wfull["mla_w_ukv"][0]

    L = w_ada.shape[0]
    NC = w_ada.shape[2]
    nbq, nbo = swa_b_qkv.shape[1], swa_b_o.shape[1]
    cpad = -(-(D + nbq + nbo) // 1024) * 1024
    cpack = jnp.pad(jnp.concatenate([c[0], swa_b_qkv[0], swa_b_o[0]]), (0, cpad - (D + nbq + nbo))).reshape(8, cpad // 8)
    call = _all_gather("gather_c", cpack, F32).reshape(N_DEV, cpad)
    c_all = call[:, :D]
    b_qkv_full = call[:, D:D + nbq].reshape(1, N_DEV * nbq)
    b_o_full = call[:, D + nbq:D + nbq + nbo].reshape(1, N_DEV * nbo)
    mod_cols = _ada_fwd(c_all, w_ada)
    mpad = -(-(L * NC) // 1024) * 1024
    mod_send = jnp.pad(jnp.moveaxis(mod_cols, 1, 0).reshape(N_DEV, L * NC), ((0, 0), (0, mpad - L * NC)))
    mod_mine = _all_to_all("exchange_mod", mod_send.reshape(N_DEV, 8, mpad // 8)).reshape(N_DEV, mpad)[:, :L * NC]
    mod = jnp.moveaxis(mod_mine.reshape(N_DEV, L, NC), 0, 1).reshape(L, N_DEV * NC) + b_ada
    mods = mod.reshape(L, 6, 1, D)

    half = QK_ROPE // 2
    inv_freq = ROPE_THETA ** (-jnp.arange(half, dtype=F32) / half)
    ang = positions[0].astype(F32)[:, None] * inv_freq
    cos = jnp.concatenate([jnp.cos(ang), jnp.cos(ang)], axis=-1)
    sin = jnp.concatenate([jnp.sin(ang), jnp.sin(ang)], axis=-1)

    T_ATT = ATT_TILE
    zero_bias = jnp.zeros((1, D), F32)

    sh1, sc1, gt1, sh2, sc2, gt2 = [mods[0, i] for i in range(6)]
    gm0, gp0 = _row(g_mix[0]), _row(g_mlp[0])
    h1, cqp, cq, ckvp, ckv, q, k, v, vt = _mla_in_fwd(x0, cos, sin, gm0, sc1, sh1, w_cat, mla_g_q, w_uqx, mla_g_kv,
                                                      w_ukv, ROW_TILE)
    o0, lse0, gathered = _mla_attn_fwd(q, k, vt, ATT_TILE_FWD, [big_w[n].astype(BF16) for n in LATE_WEIGHTS])
    wfull = {n: _unblock(n, b) for n, b in zip(LATE_WEIGHTS, gathered)}
    w_o_mla, w_qkv, w_o_swa = wfull["mla_w_o"][0], wfull["swa_w_qkv"][0], wfull["swa_w_o"][0]
    ff1, ff2 = wfull["w_ff1"], wfull["w_ff2"]
    y1, x1, h2 = _attn_out_fwd(o0, x0, w_o_mla, zero_bias, gt1, gp0, sc2, sh2)
    rl0, act0, y2, x2 = _mlp_fwd(h2, x1, ff1[0], ff2[0], gt2)
    sv0 = dict(y2=y2, rl=rl0, act=act0, h2=h2, x1=x1)

    th1, tc1, tg1, th2, tc2, tg2 = [mods[1, i] for i in range(6)]
    gm1, gp1 = _row(g_mix[1]), _row(g_mlp[1])
    h3, sq, sk, svv = _swa_in_fwd(x2, gm1, tc1, th1, w_qkv, b_qkv_full)
    o1, lse1 = _swa_attn_fwd(sq, sk, svv, swa_sinks)
    y3, x3, h4 = _attn_out_fwd(o1, x2, w_o_swa, b_o_full, tg1, gp1, tc2, th2)
    rl1, act1, y4, dx4, loss_part, dg_final = _mlp_fwd_loss(h4, x3, ff1[1], ff2[1], tg2, target, _row(g_final))
    sv1 = dict(y2=y4, rl=rl1, act=act1, h2=h4, x1=x3)

    dx3, dw1_1, dw2_1, dsh2_1, da2_1, dgt2_1 = _mlp_block_bwd(dx4, sv1, ff1[1], ff2[1], gp1, tc2, tg2)
    dy, do, dl, dgt1_1, db_o = _attn_out_bwd(dx3, y3, o1, tg1, w_o_swa, SWA_HEADS)
    dw_o_swa = _matmul_tn("dw_o", o1, dy, BF16)
    dsq, dsk, dsv, dsink = _swa_attn_bwd(sq, sk, svv, do, lse1, dl, swa_sinks)
    dqkv, dx2, db_qkv, dsh1_1, da1_1 = _swa_in_bwd(dsq, dsk, dsv, x2, dx3, w_qkv, gm1, tc1)
    dw_qkv = _matmul_tn("dw_qkv", h3, dqkv, BF16)

    dx1, dw1_0, dw2_0, dsh2_0, da2_0, dgt2_0 = _mlp_block_bwd(dx2, sv0, ff1[0], ff2[0], gp0, sc2, gt2)
    dy, do, dl, dgt1_0, _ = _attn_out_bwd(dx1, y1, o0, gt1, w_o_mla, MLA_HEADS)
    dw_o_mla = _matmul_tn("dw_o", o0, dy, BF16)
    tb = min(T_ATT, S)
    delta = dl[:MLA_HEADS].reshape(MLA_HEADS, S // tb, 1, tb)
    glate = dict(mla_w_o=dw_o_mla[None], swa_w_qkv=dw_qkv[None], swa_w_o=dw_o_swa[None],
                 w_ff2=jnp.stack([dw2_0, dw2_1]))
    gblocks = {n: _block(n, g).astype(BF16) for n, g in glate.items()}
    gblocks["w_ff1"] = jnp.stack([dw1_0, dw1_1], axis=1)
    lse_rows = (lse0 * LOG2E).reshape(MLA_HEADS, S // tb, 1, tb)
    dq, dk, dv, recv = _mla_attn_bwd(q, k, v, do, lse_rows, delta, T_ATT, [gblocks[n] for n in LATE_WEIGHTS])
    late = {n: _sum_adamw(r, big_w[n], big_m[n], big_v[n]) for n, r in zip(LATE_WEIGHTS, recv)}
    dqx, dkv, dcat, dx0, dg_q, dg_kv, dsh1_0, da1_0 = _mla_in_bwd(
        dq, dk, dv, cos, sin, cqp, ckvp, x0, dx1, w_uqx, mla_g_q, w_ukv, mla_g_kv, w_cat, gm0, sc1)
    dw_uqx = _matmul_tn("dw_uq", cq, dqx).reshape(QL, MLA_HEADS, 256)
    dw_ukv = _matmul_tn("dw_ukv", ckv, dkv)
    dw_cat = _matmul_tn("dw_down", h1, dcat)
    dw_uq = jnp.concatenate([dw_uqx[..., :QK_NOPE], dw_uqx[..., 128:192] + _unrot_cols(dw_uqx[..., 192:256])],
                            axis=-1).reshape(QL, MLA_HEADS * QK_DIM)
    o_kr = QL + KV_LORA
    dw_dkv = jnp.concatenate([dw_cat[:, QL:o_kr],
                              dw_cat[:, o_kr:o_kr + QK_ROPE] + _unrot_cols(dw_cat[:, o_kr + QK_ROPE:])], axis=1)

    gfirst = dict(mla_w_dq=dw_cat[None, :, :QL], mla_w_uq=dw_uq[None], mla_w_dkv=dw_dkv[None], mla_w_ukv=dw_ukv[None])
    first = _reduce_scatter_adamw("grad_exchange_adamw", pack_blocks("first", gfirst), pack("first", big_w),
                                  pack("first", big_m), pack("first", big_v))
    big_g, big_d, big_nm, big_nv = ({**unpack("first", first[j]), **{n: late[n][j] for n in LATE_WEIGHTS}}
                                    for j in range(4))

    dmod = jnp.stack([
        jnp.concatenate([dsh1_0, gm0 * da1_0, dgt1_0, dsh2_0, gp0 * da2_0, dgt2_0], axis=1),
        jnp.concatenate([dsh1_1, gm1 * da1_1, dgt1_1, dsh2_1, gp1 * da2_1, dgt2_1], axis=1)]).reshape(-1)
    dg_mix = jnp.concatenate([(1.0 + sc1) * da1_0, (1.0 + tc1) * da1_1], axis=1).reshape(-1)
    dg_mlp = jnp.concatenate([(1.0 + sc2) * da2_0, (1.0 + tc2) * da2_1], axis=1).reshape(-1)
    parts = [loss_part.reshape(-1), dmod, dg_mix, dg_mlp, dg_q.reshape(-1), dg_kv.reshape(-1), dsink.reshape(-1),
             dg_final.reshape(-1), db_qkv.reshape(-1), db_o.reshape(-1)]
    soffs = np.concatenate([[0], np.cumsum([p.size for p in parts])])
    spad = -(-int(soffs[-1]) // 1024) * 1024
    spack = jnp.pad(jnp.concatenate(parts), (0, spad - int(soffs[-1]))).reshape(8, spad // 8)
    sall = _all_gather("gather_small_grads", spack, F32)
    ssum = _sum_devices(sall).reshape(-1)
    tot = [ssum[int(soffs[i]):int(soffs[i + 1])] for i in range(len(parts))]
    loss = tot[0][0]
    nsink = swa_sinks.shape[1]
    small_g = dict(b_ada=tot[1].reshape(b_ada.shape), g_mix=tot[2].reshape(g_mix.shape), g_mlp=tot[3].reshape(g_mlp.shape),
                   mla_g_q=tot[4].reshape(mla_g_q.shape), mla_g_kv=tot[5].reshape(mla_g_kv.shape),
                   swa_sinks=tot[6][:nsink].reshape(swa_sinks.shape), g_final=tot[7].reshape(g_final.shape),
                   swa_b_qkv=lax.dynamic_slice(tot[8], (me * nbq,), (nbq,)).reshape(swa_b_qkv.shape),
                   swa_b_o=lax.dynamic_slice(tot[9], (me * nbo,), (nbo,)).reshape(swa_b_o.shape))
    small_w = dict(b_ada=b_ada, g_mix=g_mix, g_mlp=g_mlp, mla_g_q=mla_g_q, mla_g_kv=mla_g_kv, swa_sinks=swa_sinks,
                   g_final=g_final, swa_b_qkv=swa_b_qkv, swa_b_o=swa_b_o)
    small_m = dict(b_ada=m_b_ada, g_mix=m_g_mix, g_mlp=m_g_mlp, mla_g_q=m_mla_g_q, mla_g_kv=m_mla_g_kv,
                   swa_sinks=m_swa_sinks, g_final=m_g_final, swa_b_qkv=m_swa_b_qkv, swa_b_o=m_swa_b_o)
    small_v = dict(b_ada=v_b_ada, g_mix=v_g_mix, g_mlp=v_g_mlp, mla_g_q=v_mla_g_q, mla_g_kv=v_mla_g_kv,
                   swa_sinks=v_swa_sinks, g_final=v_g_final, swa_b_qkv=v_swa_b_qkv, swa_b_o=v_swa_b_o)
    SMALL = list(small_w)
    woffs = np.concatenate([[0], np.cumsum([small_w[n].size for n in SMALL])])
    wpad = -(-int(woffs[-1]) // 1024) * 1024

    def spack_of(d):
        flat = jnp.concatenate([d[n].reshape(-1) for n in SMALL])
        return jnp.pad(flat, (0, wpad - int(woffs[-1]))).reshape(8, wpad // 8)

    sm = _adamw_small(spack_of(small_w), spack_of(small_g), spack_of(small_m), spack_of(small_v))
    small_d, small_nm, small_nv = (
        {n: a.reshape(-1)[int(woffs[i]):int(woffs[i + 1])].reshape(small_w[n].shape) for i, n in enumerate(SMALL)}
        for a in sm)

    b_off = int(soffs[1])
    dmod_all = sall.reshape(N_DEV, -1)[:, b_off:b_off + L * N_DEV * NC].reshape(N_DEV, L, N_DEV * NC)
    dmod_cols = jnp.moveaxis(lax.dynamic_slice_in_dim(dmod_all, me * NC, NC, axis=2), 0, 1)
    ada_g, ada_d, ada_nm, ada_nv = _ada_bwd_adamw(c_all.T, dmod_cols, w_ada, m_w_ada, v_w_ada)

    order = ["w_ada", "b_ada", "g_mix", "g_mlp", "mla_w_dq", "mla_g_q", "mla_w_uq", "mla_w_dkv", "mla_g_kv",
             "mla_w_ukv", "mla_w_o", "swa_w_qkv", "swa_b_qkv", "swa_sinks", "swa_w_o", "swa_b_o", "w_ff1", "w_ff2", "g_final"]

    def collect(ada, big, small):
        return [ada if n == "w_ada" else (big[n] if n in big else small[n]) for n in order]

    return (loss, dx0.reshape(x.shape), *collect(ada_g, big_g, small_g), *collect(ada_d, big_d, small_d),
            *collect(ada_nm, big_nm, small_nm), *collect(ada_nv, big_nv, small_nv))
```

```python
import jax
import jax.numpy as jnp
import numpy as np
from jax import lax
from jax.experimental import pallas as pl
from jax.experimental.pallas import tpu as pltpu

F32 = jnp.float32
BF16 = jnp.bfloat16
MESH_IDS = pl.DeviceIdType.MESH
N_DEV = 8

MLA_HEADS = 8
QK_NOPE = 128
QK_ROPE = 64
QK_DIM = QK_NOPE + QK_ROPE
V_DIM = 128
KV_LORA = 256
ROPE_THETA = 10000.0
SWA_HEADS = 16
SWA_KV_HEADS = 4
SWA_GROUP = SWA_HEADS // SWA_KV_HEADS
SWA_HEAD_DIM = 64
WINDOW = 128
EPS = 1e-6
LOG2E = 1.4426950408889634

ADAM_LR = 0.001
ADAM_B1 = 0.9
ADAM_B2 = 0.999
ADAM_EPS = 1e-08
ADAM_WD = 0.01
ADAM_STEP = 10

PACK_COLS = 1024
VMEM_LIMIT = 56 << 20
ROW_TILE = 512
ROW_TILE_WIDE = 256
ROW_TILE_BWD = 512
ATT_TILE = 512
ATT_TILE_FWD = 1024
TN_TOKENS = 2048


def _dot(a, b):
    return jnp.dot(a, b, preferred_element_type=F32)


def _dot_nt(a, b):
    return lax.dot_general(a, b, (((1,), (1,)), ((), ())), preferred_element_type=F32)


def _dot_tn(a, b):
    return lax.dot_general(a, b, (((0,), (0,)), ((), ())), preferred_element_type=F32)


def _rstd(x):
    return lax.rsqrt(jnp.mean(x * x, axis=-1, keepdims=True) + EPS)


def _rms_bwd(dn, n, r):
    return r * (dn - n * jnp.mean(dn * n, axis=-1, keepdims=True))


def _modulate(x, g, sc, sh):
    r = _rstd(x)
    return ((x * r) * g) * (1.0 + sc) + sh


def _modulate_bwd(dh, x, g, sc):
    r = _rstd(x)
    n = x * r
    dsh = jnp.sum(dh, axis=0, keepdims=True)
    da = jnp.sum(dh * n, axis=0, keepdims=True)
    dx = _rms_bwd(dh * (g * (1.0 + sc)), n, r)
    return dx, dsh, da


def _acc(ref, val, i):
    @pl.when(i == 0)
    def _():
        ref[...] = val

    @pl.when(i != 0)
    def _():
        ref[...] += val


def _row_spec(shape, tm):
    nd = len(shape)
    return pl.BlockSpec(tuple(shape[:nd - 2]) + (tm, shape[-1]), lambda i: (0,) * (nd - 2) + (i, 0))


def _resident_spec(shape, single_buffer):
    nd = len(shape)
    if single_buffer:
        return pl.BlockSpec(tuple(shape), lambda i: (0,) * nd, pipeline_mode=pl.Buffered(1))
    return pl.BlockSpec(tuple(shape), lambda i: (0,) * nd)


def _rowcall(name, body, tokens, tm, row_in, full_in, row_out, acc_out=()):
    tm = min(tm, tokens)
    in_specs = [_row_spec(a.shape, tm) for a in row_in] + [_resident_spec(a.shape, True) for a in full_in]
    row_specs = [s[1] if isinstance(s, tuple) else _row_spec(s.shape, tm) for s in row_out]
    row_out = [s[0] if isinstance(s, tuple) else s for s in row_out]
    out_specs = row_specs + [_resident_spec(s.shape, False) for s in acc_out]
    return pl.pallas_call(
        body, name=name, grid=(tokens // tm,), in_specs=in_specs, out_specs=out_specs,
        out_shape=list(row_out) + list(acc_out),
        compiler_params=pltpu.CompilerParams(dimension_semantics=("arbitrary",), vmem_limit_bytes=VMEM_LIMIT),
    )(*row_in, *full_in)


def _sds(shape, dtype):
    return jax.ShapeDtypeStruct(tuple(shape), dtype)


def _mla_in_fwd(x, cos, sin, g, sc, sh, w_cat, g_q, w_uqx, g_kv, w_ukv, t):
    S, D = x.shape
    QL = g_q.shape[1]
    H = MLA_HEADS
    t = min(t, S)

    def body(x_ref, cos_ref, sin_ref, g_ref, sc_ref, sh_ref, wcat_ref, gq_ref, wuqx_ref, gkv_ref, wukv_ref,
             h_ref, cqp_ref, cq_ref, ckvp_ref, ckv_ref, q_ref, k_ref, v_ref, vt_ref):
        cs, sn = cos_ref[...], sin_ref[...]
        hb = _modulate(x_ref[...], g_ref[...], sc_ref[...], sh_ref[...]).astype(BF16)
        h_ref[...] = hb
        low = _dot(hb, wcat_ref[...])
        cqp = low[:, :QL]
        cqp_ref[...] = cqp
        cq = ((cqp * _rstd(cqp)) * gq_ref[...]).astype(BF16)
        cq_ref[...] = cq
        ckvp = low[:, QL:QL + KV_LORA]
        ckvp_ref[...] = ckvp
        ckv = ((ckvp * _rstd(ckvp)) * gkv_ref[...]).astype(BF16)
        ckv_ref[...] = ckv
        o = QL + KV_LORA
        kr = (low[:, o:o + QK_ROPE] * cs + low[:, o + QK_ROPE:o + 2 * QK_ROPE] * sn).astype(BF16)
        qx = _dot(cq, wuqx_ref[...])
        kv = _dot(ckv, wukv_ref[...])
        for hd in range(H):
            b = hd * 256
            q_ref[hd, :, 0:QK_NOPE] = qx[:, b:b + QK_NOPE].astype(BF16)
            q_ref[hd, :, QK_NOPE:QK_DIM] = (qx[:, b + 128:b + 192] * cs + qx[:, b + 192:b + 256] * sn).astype(BF16)
            k_ref[hd, :, 0:QK_NOPE] = kv[:, b:b + QK_NOPE].astype(BF16)
            k_ref[hd, :, QK_NOPE:QK_DIM] = kr
            vh = kv[:, b + 128:b + 256]
            v_ref[hd] = vh.astype(BF16)
            vt_ref[hd, 0, 0:V_DIM, :] = vh.T.astype(BF16)
            vt_ref[hd, 0, V_DIM:2 * V_DIM, :] = jnp.ones((V_DIM, x_ref.shape[0]), BF16)

    vt_spec = pl.BlockSpec((H, 1, 2 * V_DIM, t), lambda i: (0, i, 0, 0))
    return _rowcall(
        "mla_in_fwd", body, S, t, [x, cos, sin], [g, sc, sh, w_cat, g_q, w_uqx, g_kv, w_ukv],
        [_sds((S, D), BF16), _sds((S, QL), F32), _sds((S, QL), BF16), _sds((S, KV_LORA), F32), _sds((S, KV_LORA), BF16),
         _sds((H, S, QK_DIM), BF16), _sds((H, S, QK_DIM), BF16), _sds((H, S, V_DIM), BF16),
         (_sds((H, S // t, 2 * V_DIM, t), BF16), vt_spec)])


def _mla_attn_fwd(q, k, vt, t, sends):
    H, S, DQ = q.shape
    DV = V_DIM
    vb = vt.shape[-1]
    t = max(min(t, S), vb)
    nb = S // t
    scale = QK_DIM ** -0.5
    c2 = scale * LOG2E

    ns = len(sends)

    def body(q_ref, k_ref, vt_ref, *rest):
        send_refs, (o_ref, lse_ref), gath_refs = rest[:ns], rest[ns:ns + 2], rest[ns + 2:2 * ns + 2]
        m_s, acc_s, s_buf, send_sems, recv_sems, local_sems = rest[2 * ns + 2:]
        hd, qi = pl.program_id(0), pl.program_id(1)

        def gather():
            return _exchange_copies([lambda j, r=r: r for r in send_refs], gath_refs, send_sems, recv_sems, local_sems)

        @pl.when((hd == 0) & (qi == 0))
        def _():
            _start_exchange(gather())

        m_s[...] = jnp.full_like(m_s, -jnp.inf)
        acc_s[...] = jnp.zeros_like(acc_s)

        def scores(j, slot):
            rows = pl.ds(pl.multiple_of(j * t, t), t)
            s_buf[slot] = _dot_nt(k_ref[0, rows, :], q_ref[0])

        def absorb(j, slot, diagonal):
            s = s_buf[slot]
            if diagonal:
                key = lax.broadcasted_iota(jnp.int32, (t, t), 0)
                qry = lax.broadcasted_iota(jnp.int32, (t, t), 1)
                s = jnp.where(key <= qry, s, -jnp.inf)
            m_prev = m_s[...]
            m_new = jnp.maximum(m_prev, jnp.max(s, axis=0, keepdims=True))
            alpha = jnp.exp2((m_prev - m_new) * c2)
            p = jnp.exp2((s - m_new) * c2)
            pb = p.astype(BF16)
            acc = alpha * acc_s[...]
            for u in range(t // vb):
                acc = acc + _dot(vt_ref[0, j * (t // vb) + u], pb[u * vb:(u + 1) * vb, :])
            acc_s[...] = acc
            m_s[...] = m_new

        def pair(i, carry):
            j = 2 * i
            scores(j + 1, 1)
            absorb(j, 0, False)
            scores(j + 2, 0)
            absorb(j + 1, 1, False)
            return carry

        scores(0, 0)
        lax.fori_loop(0, qi // 2, pair, 0)

        @pl.when(qi % 2 == 0)
        def _():
            absorb(qi, 0, True)

        @pl.when(qi % 2 == 1)
        def _():
            scores(qi, 1)
            absorb(qi - 1, 0, False)
            absorb(qi, 1, True)

        acc = acc_s[...]
        o_ref[...] = (acc[:DV] / acc[DV:]).T.astype(BF16)
        lse_ref[0, 0] = m_s[...] * scale + jnp.log(acc[DV:DV + 1])

        @pl.when((hd == H - 1) & (qi == nb - 1))
        def _():
            _finish_exchange(gather())

    outs = pl.pallas_call(
        body, name="mla_attn_fwd", grid=(H, nb),
        in_specs=[pl.BlockSpec((1, t, DQ), lambda h, i: (h, i, 0)),
                  pl.BlockSpec((1, S, DQ), lambda h, i: (h, 0, 0)),
                  pl.BlockSpec((1, S // vb, 2 * DV, vb), lambda h, i: (h, 0, 0, 0))] + [ANY_SPEC] * ns,
        out_specs=[pl.BlockSpec((t, DV), lambda h, i: (i, h)),
                   pl.BlockSpec((1, 1, 1, t), lambda h, i: (h, i, 0, 0))] + [ANY_SPEC] * ns,
        out_shape=[_sds((S, H * DV), BF16), _sds((H, nb, 1, t), F32)]
        + [_sds((N_DEV,) + a.shape, a.dtype) for a in sends],
        scratch_shapes=[pltpu.VMEM((1, t), F32), pltpu.VMEM((2 * DV, t), F32), pltpu.VMEM((2, t, t), F32)]
        + _comm_sems(ns),
        compiler_params=pltpu.CompilerParams(dimension_semantics=("arbitrary", "arbitrary"),
                                             vmem_limit_bytes=VMEM_LIMIT),
    )(q, k, vt, *sends)
    return outs[0], outs[1], outs[2:]


def _attn_out_fwd(o, x, w_o, b_o, gt, g, sc, sh):
    S, D = x.shape

    def body(o_ref, x_ref, wo_ref, bo_ref, gt_ref, g_ref, sc_ref, sh_ref, y_ref, x1_ref, h_ref):
        y = _dot(o_ref[...], wo_ref[...]) + bo_ref[...]
        y_ref[...] = y.astype(BF16)
        x1 = x_ref[...] + gt_ref[...] * y
        x1_ref[...] = x1
        h_ref[...] = _modulate(x1, g_ref[...], sc_ref[...], sh_ref[...]).astype(BF16)

    return _rowcall("attn_out_fwd", body, S, ROW_TILE, [o, x], [w_o, b_o, gt, g, sc, sh],
                    [_sds((S, D), BF16), _sds((S, D), F32), _sds((S, D), BF16)])


def _mlp_fwd(h, x, w1, w2, gt):
    S, D = x.shape
    FF = w1.shape[1]

    def body(h_ref, x_ref, w1_ref, w2_ref, gt_ref, rl_ref, act_ref, y_ref, x2_ref):
        rl = jnp.maximum(_dot(h_ref[...], w1_ref[...]), 0.0)
        rl_ref[...] = rl.astype(BF16)
        act = (rl * rl).astype(BF16)
        act_ref[...] = act
        y = _dot(act, w2_ref[...])
        y_ref[...] = y.astype(BF16)
        x2_ref[...] = x_ref[...] + gt_ref[...] * y

    return _rowcall("mlp_fwd", body, S, ROW_TILE_WIDE, [h, x], [w1, w2, gt],
                    [_sds((S, FF), BF16), _sds((S, FF), BF16), _sds((S, D), BF16), _sds((S, D), F32)])


def _final_norm_loss(xv, target, g, d_model):
    r = _rstd(xv)
    n = xv * r
    err = n * g - target
    part = 0.5 * jnp.sum(jnp.mean(err * err, axis=-1, keepdims=True), axis=0, keepdims=True)
    dout = err / d_model
    return part, _rms_bwd(dout * g, n, r), jnp.sum(dout * n, axis=0, keepdims=True)


def _mlp_fwd_loss(h, x, w1, w2, gt, target, g_final):
    S, D = x.shape
    FF = w1.shape[1]

    def body(h_ref, x_ref, t_ref, w1_ref, w2_ref, gt_ref, g_ref, rl_ref, act_ref, y_ref, dx_ref, loss_ref, dg_ref):
        i = pl.program_id(0)
        rl = jnp.maximum(_dot(h_ref[...], w1_ref[...]), 0.0)
        rl_ref[...] = rl.astype(BF16)
        act = (rl * rl).astype(BF16)
        act_ref[...] = act
        y = _dot(act, w2_ref[...])
        y_ref[...] = y.astype(BF16)
        part, dx, dg = _final_norm_loss(x_ref[...] + gt_ref[...] * y, t_ref[...], g_ref[...], D)
        dx_ref[...] = dx
        _acc(loss_ref, jnp.broadcast_to(part, loss_ref.shape), i)
        _acc(dg_ref, dg, i)

    return _rowcall("mlp_fwd_loss", body, S, ROW_TILE_WIDE, [h, x, target], [w1, w2, gt, g_final],
                    [_sds((S, FF), BF16), _sds((S, FF), BF16), _sds((S, D), BF16), _sds((S, D), F32)],
                    [_sds((1, 128), F32), _sds((1, D), F32)])


def _swa_in_fwd(x, g, sc, sh, w_qkv, b_qkv):
    S, D = x.shape
    NQ = SWA_HEADS * SWA_HEAD_DIM
    NK = SWA_KV_HEADS * SWA_HEAD_DIM

    def body(x_ref, g_ref, sc_ref, sh_ref, w_ref, b_ref, h_ref, q_ref, k_ref, v_ref):
        hb = _modulate(x_ref[...], g_ref[...], sc_ref[...], sh_ref[...]).astype(BF16)
        h_ref[...] = hb
        qkv = _dot(hb, w_ref[...]) + b_ref[...]
        q_ref[...] = qkv[:, :NQ].astype(BF16)
        k_ref[...] = qkv[:, NQ:NQ + NK].astype(BF16)
        v_ref[...] = qkv[:, NQ + NK:].astype(BF16)

    return _rowcall("swa_in_fwd", body, S, ROW_TILE, [x], [g, sc, sh, w_qkv, b_qkv],
                    [_sds((S, D), BF16), _sds((S, NQ), BF16), _sds((S, NK), BF16), _sds((S, NK), BF16)])


def _alibi_slope(head):
    return float(np.float32(2.0 ** (-8.0 * (head + 1) / SWA_HEADS)))


def _swa_geometry(n):
    W, G = WINDOW, SWA_GROUP
    key = lax.broadcasted_iota(jnp.int32, (2 * W, G * W), 0)
    qry = lax.broadcasted_iota(jnp.int32, (2 * W, G * W), 1) & (W - 1)
    dist = W + qry - key
    valid = (dist >= 0) & (dist < W) & ((n > 0) | (key >= W))
    return dist.astype(F32), valid


def _swa_group(kh, q_ref, sink_ref):
    W, G, Dh = WINDOW, SWA_GROUP, SWA_HEAD_DIM
    heads = [kh * G + g for g in range(G)]
    q4 = jnp.concatenate([q_ref[:, h * Dh:(h + 1) * Dh] for h in heads], axis=0)
    slopes = jnp.concatenate([jnp.full((1, W), _alibi_slope(h), F32) for h in heads], axis=1)
    sinks = jnp.concatenate([jnp.broadcast_to(sink_ref[:, h:h + 1], (1, W)) for h in heads], axis=1)
    return heads, q4, slopes, sinks


def _swa_band_specs(W, nb, cols):
    prev = pl.BlockSpec((W, cols), lambda n: (jnp.maximum(jnp.minimum(n, nb - 1) - 1, 0), 0))
    cur = pl.BlockSpec((W, cols), lambda n: (jnp.minimum(n, nb - 1), 0))
    return prev, cur


def _swa_attn_fwd(q, k, v, sinks):
    S, NQ = q.shape
    NK = k.shape[1]
    W, Dh, G = WINDOW, SWA_HEAD_DIM, SWA_GROUP
    nb = S // W

    def body(q_ref, kp_ref, kc_ref, vp_ref, vc_ref, sink_ref, o_ref, lse_ref):
        distf, valid = _swa_geometry(pl.program_id(0))
        kband = jnp.concatenate([kp_ref[...], kc_ref[...]], axis=0)
        vband_t = jnp.concatenate([vp_ref[...], vc_ref[...]], axis=0).astype(F32).T.astype(BF16)
        outs = []
        for kh in range(SWA_KV_HEADS):
            kb = kband[:, kh * Dh:(kh + 1) * Dh]
            vbt = vband_t[kh * Dh:(kh + 1) * Dh, :]
            heads, q4, slopes, sinks = _swa_group(kh, q_ref, sink_ref)
            s = _dot_nt(kb, q4) * (Dh ** -0.5) - slopes * distf
            s = jnp.where(valid, s, -jnp.inf)
            m = jnp.maximum(jnp.max(s, axis=0, keepdims=True), sinks)
            p = jnp.exp(s - m)
            denom = jnp.sum(p, axis=0, keepdims=True) + jnp.exp(sinks - m)
            out4 = _dot(vbt, (p * (1.0 / denom)).astype(BF16))
            lse4 = m + jnp.log(denom)
            for g, h in enumerate(heads):
                outs.append(out4[:, g * W:(g + 1) * W])
                lse_ref[h:h + 1, :] = lse4[:, g * W:(g + 1) * W]
        o_ref[...] = jnp.concatenate(outs, axis=0).T.astype(BF16)

    kprev, kcur = _swa_band_specs(W, nb, NK)
    return pl.pallas_call(
        body, name="swa_attn_fwd", grid=(nb,),
        in_specs=[pl.BlockSpec((W, NQ), lambda n: (n, 0)), kprev, kcur, kprev, kcur,
                  pl.BlockSpec((1, SWA_HEADS), lambda n: (0, 0))],
        out_specs=[pl.BlockSpec((W, NQ), lambda n: (n, 0)), pl.BlockSpec((SWA_HEADS, W), lambda n: (0, n))],
        out_shape=[_sds((S, NQ), BF16), _sds((SWA_HEADS, S), F32)],
        compiler_params=pltpu.CompilerParams(dimension_semantics=("arbitrary",), vmem_limit_bytes=VMEM_LIMIT),
    )(q, k, k, v, v, sinks)


def _mlp_bwd_a(dx, y, rl, gt, w2):
    S, D = dx.shape
    FF = rl.shape[1]

    def body(dx_ref, y_ref, rl_ref, gt_ref, w2_ref, dy_ref, du_ref, dgt_ref):
        i = pl.program_id(0)
        dxv = dx_ref[...]
        _acc(dgt_ref, jnp.sum(dxv * y_ref[...].astype(F32), axis=0, keepdims=True), i)
        dy = (dxv * gt_ref[...]).astype(BF16)
        dy_ref[...] = dy
        dact = _dot_nt(dy, w2_ref[...])
        du_ref[...] = (dact * (2.0 * rl_ref[...].astype(F32))).astype(BF16)

    return _rowcall("mlp_bwd_a", body, S, ROW_TILE_BWD, [dx, y, rl], [gt, w2],
                    [_sds((S, D), BF16), _sds((S, FF), BF16)], [_sds((1, D), F32)])


def _mlp_bwd_b(du, x, dx, w1, g, sc):
    S, D = x.shape

    def body(du_ref, x_ref, dx_ref, w1_ref, g_ref, sc_ref, dxo_ref, dsh_ref, da_ref):
        i = pl.program_id(0)
        dh = _dot_nt(du_ref[...], w1_ref[...])
        dxn, dsh, da = _modulate_bwd(dh, x_ref[...], g_ref[...], sc_ref[...])
        dxo_ref[...] = dx_ref[...] + dxn
        _acc(dsh_ref, dsh, i)
        _acc(da_ref, da, i)

    return _rowcall("mlp_bwd_b", body, S, ROW_TILE_BWD, [du, x, dx], [w1, g, sc],
                    [_sds((S, D), F32)], [_sds((1, D), F32), _sds((1, D), F32)])


def _attn_out_bwd(dx, y, o, gt, w_o, n_heads):
    S, D = dx.shape
    NO = o.shape[1]
    dh = NO // n_heads
    member = (jnp.arange(NO)[None, :] // dh == jnp.arange(16)[:, None]).astype(BF16)

    def body(dx_ref, y_ref, o_ref, gt_ref, wo_ref, mem_ref, dy_ref, do_ref, dl_ref, dgt_ref, dbo_ref):
        i = pl.program_id(0)
        dxv = dx_ref[...]
        _acc(dgt_ref, jnp.sum(dxv * y_ref[...].astype(F32), axis=0, keepdims=True), i)
        dy = dxv * gt_ref[...]
        _acc(dbo_ref, jnp.sum(dy, axis=0, keepdims=True), i)
        dyb = dy.astype(BF16)
        dy_ref[...] = dyb
        do = _dot_nt(dyb, wo_ref[...])
        do_ref[...] = do.astype(BF16)
        prod = do * o_ref[...].astype(F32)
        hi = prod.astype(BF16)
        lo = (prod - hi.astype(F32)).astype(BF16)
        dl_ref[...] = _dot_nt(mem_ref[...], hi) + _dot_nt(mem_ref[...], lo)

    tm = min(ROW_TILE, S)
    return _rowcall("attn_out_bwd", body, S, ROW_TILE, [dx, y, o], [gt, w_o, member],
                    [_sds((S, D), BF16), _sds((S, NO), BF16),
                     (_sds((16, S), F32), pl.BlockSpec((16, tm), lambda i: (0, i)))],
                    [_sds((1, D), F32), _sds((1, D), F32)])


def _mla_attn_bwd(q, k, v, do, lse, delta, t, gblks):
    H, S, DQ = q.shape
    DV = V_DIM
    t = min(t, S // 2)
    tk = 2 * t
    nq, nk = S // t, S // tk
    scale = QK_DIM ** -0.5
    c2 = scale * LOG2E

    ng = len(gblks)

    def body(q_ref, k_ref, v_ref, do_ref, lse_ref, dl_ref, *rest):
        g_refs, (dq_ref, dk_ref, dv_ref), recv_refs = rest[:ng], rest[ng:ng + 3], rest[ng + 3:2 * ng + 3]
        dk_s, dv_s, s_buf, dp_buf, send_sems, recv_sems, local_sems = rest[2 * ng + 3:]
        hd, kj = pl.program_id(0), pl.program_id(1)

        def scatter():
            return _exchange_copies([lambda j, r=r: r.at[j] for r in g_refs], recv_refs, send_sems, recv_sems,
                                    local_sems)

        @pl.when((hd == 0) & (kj == 0))
        def _():
            _start_exchange(scatter())

        @pl.when(kj == 0)
        def _():
            dq_ref[...] = jnp.zeros_like(dq_ref)

        dk_s[...] = jnp.zeros_like(dk_s)
        dv_s[...] = jnp.zeros_like(dv_s)

        def products(i, slot, keys=tk):
            rows = pl.ds(pl.multiple_of(i * t, t), t)
            s_buf[slot, :keys] = _dot_nt(k_ref[0, :keys], q_ref[0, rows, :])
            dp_buf[slot, :keys] = _dot_nt(v_ref[0, :keys], do_ref[rows, :])

        def absorb(i, slot, diagonal, keys=tk):
            rows = pl.ds(pl.multiple_of(i * t, t), t)
            qb, dob = q_ref[0, rows, :], do_ref[rows, :]
            p = jnp.exp2(s_buf[slot, :keys] * c2 - lse_ref[0, i])
            if diagonal is not None:
                key = lax.broadcasted_iota(jnp.int32, (keys, t), 0)
                qry = lax.broadcasted_iota(jnp.int32, (keys, t), 1) + diagonal * t
                p = jnp.where(key <= qry, p, 0.0)
            dv_s[:keys] += _dot(p.astype(BF16), dob)
            ds = (p * (dp_buf[slot, :keys] - dl_ref[0, i])).astype(BF16)
            dk_s[:keys] += _dot(ds, qb)
            dq_ref[0, rows, :] += _dot_tn(ds, k_ref[0, :keys])

        first = 2 * kj + 2
        n_off = nq - first

        def pair(i, carry):
            u = 2 * i
            products(first + u + 1, 1)
            absorb(first + u, 0, None)
            products(jnp.where(u + 2 < n_off, first + u + 2, 2 * kj + 1), 0)
            absorb(first + u + 1, 1, None)
            return carry

        products(jnp.where(n_off > 0, first, 2 * kj + 1), 0)
        lax.fori_loop(0, n_off // 2, pair, 0)
        products(2 * kj, 1, t)
        absorb(2 * kj + 1, 0, 1)
        absorb(2 * kj, 1, 0, t)

        dk_ref[0] = (dk_s[...] * scale).astype(BF16)
        dv_ref[0] = dv_s[...].astype(BF16)

        @pl.when((hd == H - 1) & (kj == nk - 1))
        def _():
            _finish_exchange(scatter())

    rowspec = pl.BlockSpec((1, nq, 1, t), lambda h, j: (h, 0, 0, 0))
    outs = pl.pallas_call(
        body, name="mla_attn_bwd", grid=(H, nk),
        in_specs=[pl.BlockSpec((1, S, DQ), lambda h, j: (h, 0, 0)),
                  pl.BlockSpec((1, tk, DQ), lambda h, j: (h, j, 0)),
                  pl.BlockSpec((1, tk, DV), lambda h, j: (h, j, 0)),
                  pl.BlockSpec((S, DV), lambda h, j: (0, h)), rowspec, rowspec] + [ANY_SPEC] * ng,
        out_specs=[pl.BlockSpec((1, S, DQ), lambda h, j: (h, 0, 0)),
                   pl.BlockSpec((1, tk, DQ), lambda h, j: (h, j, 0)),
                   pl.BlockSpec((1, tk, DV), lambda h, j: (h, j, 0))] + [ANY_SPEC] * ng,
        out_shape=[_sds((H, S, DQ), F32), _sds((H, S, DQ), BF16), _sds((H, S, DV), BF16)]
        + [_sds(g.shape, g.dtype) for g in gblks],
        scratch_shapes=[pltpu.VMEM((tk, DQ), F32), pltpu.VMEM((tk, DV), F32), pltpu.VMEM((2, tk, t), F32),
                        pltpu.VMEM((2, tk, t), F32)] + _comm_sems(ng),
        compiler_params=pltpu.CompilerParams(dimension_semantics=("arbitrary", "arbitrary"),
                                             vmem_limit_bytes=VMEM_LIMIT),
    )(q, k, v, do, lse, delta, *gblks)
    return outs[0], outs[1], outs[2], outs[3:]


def _swa_attn_bwd(q, k, v, do, lse, delta, sinks):
    S, NQ = q.shape
    NK = k.shape[1]
    W, Dh, G = WINDOW, SWA_HEAD_DIM, SWA_GROUP
    nb = S // W

    def body(q_ref, kp_ref, kc_ref, vp_ref, vc_ref, do_ref, lse_ref, dl_ref, sink_ref,
             dq_ref, dk_ref, dv_ref, dsink_ref, dkc_s, dvc_s):
        n = pl.program_id(0)

        @pl.when(n == 0)
        def _():
            dkc_s[...] = jnp.zeros_like(dkc_s)
            dvc_s[...] = jnp.zeros_like(dvc_s)
            dsink_ref[...] = jnp.zeros_like(dsink_ref)

        @pl.when(n < nb)
        def _():
            distf, valid = _swa_geometry(n)
            kband = jnp.concatenate([kp_ref[...], kc_ref[...]], axis=0)
            vband = jnp.concatenate([vp_ref[...], vc_ref[...]], axis=0)
            kband_t = kband.astype(F32).T.astype(BF16)
            dq_t = []
            for kh in range(SWA_KV_HEADS):
                ck = slice(kh * Dh, (kh + 1) * Dh)
                kb, vb, kbt = kband[:, ck], vband[:, ck], kband_t[ck, :]
                heads, q4, slopes, sinks = _swa_group(kh, q_ref, sink_ref)
                do4 = jnp.concatenate([do_ref[:, h * Dh:(h + 1) * Dh] for h in heads], axis=0)
                lse4 = jnp.concatenate([lse_ref[h:h + 1, :] for h in heads], axis=1)
                dl4 = jnp.concatenate([dl_ref[h:h + 1, :] for h in heads], axis=1)
                s = _dot_nt(kb, q4) * (Dh ** -0.5) - slopes * distf
                p = jnp.where(valid, jnp.exp(s - lse4), 0.0)
                dvb = _dot(p.astype(BF16), do4)
                dp = _dot_nt(vb, do4)
                dsb = ((p * (dp - dl4)) * (Dh ** -0.5)).astype(BF16)
                dq4 = _dot(kbt, dsb)
                dkb = _dot(dsb, q4)
                dsk4 = jnp.exp(sinks - lse4) * dl4
                for g, h in enumerate(heads):
                    dq_t.append(dq4[:, g * W:(g + 1) * W])
                    dsink_ref[:, h:h + 1] += -jnp.sum(dsk4[:, g * W:(g + 1) * W], axis=1, keepdims=True)
                dk_ref[:, ck] = (dkc_s[:, ck] + dkb[:W]).astype(BF16)
                dv_ref[:, ck] = (dvc_s[:, ck] + dvb[:W]).astype(BF16)
                dkc_s[:, ck] = dkb[W:]
                dvc_s[:, ck] = dvb[W:]
            dq_ref[...] = jnp.concatenate(dq_t, axis=0).T.astype(BF16)

        @pl.when(n == nb)
        def _():
            dk_ref[...] = dkc_s[...].astype(BF16)
            dv_ref[...] = dvc_s[...].astype(BF16)

    kprev, kcur = _swa_band_specs(W, nb, NK)
    qspec = lambda cols: pl.BlockSpec((W, cols), lambda n: (jnp.minimum(n, nb - 1), 0))
    kvout = pl.BlockSpec((W, NK), lambda n: (jnp.maximum(n - 1, 0), 0))
    rowspec = pl.BlockSpec((SWA_HEADS, W), lambda n: (0, jnp.minimum(n, nb - 1)))
    return pl.pallas_call(
        body, name="swa_attn_bwd", grid=(nb + 1,),
        in_specs=[qspec(NQ), kprev, kcur, kprev, kcur, qspec(NQ), rowspec, rowspec,
                  pl.BlockSpec((1, SWA_HEADS), lambda n: (0, 0))],
        out_specs=[qspec(NQ), kvout, kvout, pl.BlockSpec((1, 128), lambda n: (0, 0))],
        out_shape=[_sds((S, NQ), BF16), _sds((S, NK), BF16), _sds((S, NK), BF16), _sds((1, 128), F32)],
        scratch_shapes=[pltpu.VMEM((W, NK), F32), pltpu.VMEM((W, NK), F32)],
        compiler_params=pltpu.CompilerParams(dimension_semantics=("arbitrary",), vmem_limit_bytes=VMEM_LIMIT),
    )(q, k, k, v, v, do, lse, delta, sinks)


def _swa_in_bwd(dq, dk, dv, x, dx, w_qkv, g, sc):
    S, D = x.shape
    N = w_qkv.shape[1]

    def body(dq_ref, dk_ref, dv_ref, x_ref, dx_ref, w_ref, g_ref, sc_ref, dqkv_ref, dxo_ref, db_ref, dsh_ref, da_ref):
        i = pl.program_id(0)
        dqkv = jnp.concatenate([dq_ref[...], dk_ref[...], dv_ref[...]], axis=1)
        dqkv_ref[...] = dqkv
        _acc(db_ref, jnp.sum(dqkv.astype(F32), axis=0, keepdims=True), i)
        dh = _dot_nt(dqkv, w_ref[...])
        dxn, dsh, da = _modulate_bwd(dh, x_ref[...], g_ref[...], sc_ref[...])
        dxo_ref[...] = dx_ref[...] + dxn
        _acc(dsh_ref, dsh, i)
        _acc(da_ref, da, i)

    return _rowcall("swa_in_bwd", body, S, ROW_TILE, [dq, dk, dv, x, dx], [w_qkv, g, sc],
                    [_sds((S, N), BF16), _sds((S, D), F32)],
                    [_sds((1, N), F32), _sds((1, D), F32), _sds((1, D), F32)])


def _mla_in_bwd(dq, dk, dv, cos, sin, cqp, ckvp, x, dx, w_uqx, g_q, w_ukv, g_kv, w_cat, g, sc):
    S, D = x.shape
    H = MLA_HEADS
    QL = g_q.shape[1]
    NX = w_uqx.shape[1]
    NC = w_cat.shape[1]

    def body(dq_ref, dk_ref, dv_ref, cos_ref, sin_ref, cqp_ref, ckvp_ref, x_ref, dx_ref,
             wuqx_ref, gq_ref, wukv_ref, gkv_ref, wcat_ref, g_ref, sc_ref,
             dqx_ref, dkv_ref, dcat_ref, dxo_ref, dgq_ref, dgkv_ref, dsh_ref, da_ref):
        i = pl.program_id(0)
        cs, sn = cos_ref[...], sin_ref[...]
        dkr = jnp.zeros(cs.shape, F32)
        for hd in range(H):
            b = hd * 256
            dqh = dq_ref[hd] * (QK_DIM ** -0.5)
            dqx_ref[:, b:b + QK_NOPE] = dqh[:, :QK_NOPE].astype(BF16)
            dqx_ref[:, b + 128:b + 192] = (dqh[:, QK_NOPE:] * cs).astype(BF16)
            dqx_ref[:, b + 192:b + 256] = (dqh[:, QK_NOPE:] * sn).astype(BF16)
            dkh = dk_ref[hd]
            dkv_ref[:, b:b + QK_NOPE] = dkh[:, :QK_NOPE]
            dkv_ref[:, b + 128:b + 256] = dv_ref[hd]
            dkr = dkr + dkh[:, QK_NOPE:].astype(F32)
        dcq = _dot_nt(dqx_ref[...], wuqx_ref[...])
        cqp = cqp_ref[...]
        rq = _rstd(cqp)
        nq = cqp * rq
        _acc(dgq_ref, jnp.sum(dcq * nq, axis=0, keepdims=True), i)
        dcqp = _rms_bwd(dcq * gq_ref[...], nq, rq)
        dckv = _dot_nt(dkv_ref[...], wukv_ref[...])
        ckvp = ckvp_ref[...]
        rk = _rstd(ckvp)
        nk = ckvp * rk
        _acc(dgkv_ref, jnp.sum(dckv * nk, axis=0, keepdims=True), i)
        dckvp = _rms_bwd(dckv * gkv_ref[...], nk, rk)
        dcat_ref[:, :QL] = dcqp.astype(BF16)
        dcat_ref[:, QL:QL + KV_LORA] = dckvp.astype(BF16)
        o = QL + KV_LORA
        dcat_ref[:, o:o + QK_ROPE] = (dkr * cs).astype(BF16)
        dcat_ref[:, o + QK_ROPE:o + 2 * QK_ROPE] = (dkr * sn).astype(BF16)
        dh = _dot_nt(dcat_ref[...], wcat_ref[...])
        dxn, dsh, da = _modulate_bwd(dh, x_ref[...], g_ref[...], sc_ref[...])
        dxo_ref[...] = dx_ref[...] + dxn
        _acc(dsh_ref, dsh, i)
        _acc(da_ref, da, i)

    return _rowcall("mla_in_bwd", body, S, ROW_TILE_WIDE, [dq, dk, dv, cos, sin, cqp, ckvp, x, dx],
                    [w_uqx, g_q, w_ukv, g_kv, w_cat, g, sc],
                    [_sds((S, NX), BF16), _sds((S, NX), BF16), _sds((S, NC), BF16), _sds((S, D), F32)],
                    [_sds((1, QL), F32), _sds((1, KV_LORA), F32), _sds((1, D), F32), _sds((1, D), F32)])


def _matmul_tn(name, a, b, out_dtype=F32, column_blocks=False):
    S, K = a.shape
    N = b.shape[1]
    tk, tn, ts = min(K, 1024), min(N, 1024), min(S, TN_TOKENS)
    if column_blocks:
        tn = N // N_DEV
    if N % tn:
        tn = 512 if N % 512 == 0 else (384 if N % 384 == 0 else 128)
    if K % tk:
        tk = 512 if K % 512 == 0 else (384 if K % 384 == 0 else 128)
    ns = S // ts

    def body(a_ref, b_ref, o_ref, *scratch):
        acc_ref = scratch[0] if scratch else o_ref
        _acc(acc_ref, _dot_tn(a_ref[...], b_ref[...]), pl.program_id(2))
        if scratch:
            @pl.when(pl.program_id(2) == ns - 1)
            def _():
                o_ref[...] = acc_ref[...].astype(out_dtype)

    if column_blocks:
        out_spec = pl.BlockSpec((None, tk, tn), lambda i, j, s: (j, i, 0))
        out_shape = _sds((N_DEV, K, tn), out_dtype)
    else:
        out_spec = pl.BlockSpec((tk, tn), lambda i, j, s: (i, j))
        out_shape = _sds((K, N), out_dtype)
    return pl.pallas_call(
        body, name=name, grid=(K // tk, N // tn, ns),
        in_specs=[pl.BlockSpec((ts, tk), lambda i, j, s: (s, i)), pl.BlockSpec((ts, tn), lambda i, j, s: (s, j))],
        out_specs=out_spec, out_shape=out_shape,
        scratch_shapes=[] if out_dtype == F32 else [pltpu.VMEM((tk, tn), F32)],
        compiler_params=pltpu.CompilerParams(dimension_semantics=("parallel", "parallel", "arbitrary"),
                                             vmem_limit_bytes=VMEM_LIMIT),
    )(a, b)


def _silu(c):
    return c * jax.nn.sigmoid(c)


def _ada_fwd(c_all, w_ada):
    L, D, NC = w_ada.shape

    def body(c_ref, w_ref, o_ref):
        cond = _silu(c_ref[...]).astype(BF16)
        o_ref[0] = _dot(cond, w_ref[0].astype(BF16))

    return pl.pallas_call(
        body, name="ada_fwd", grid=(L,),
        in_specs=[pl.BlockSpec(c_all.shape, lambda l: (0, 0)), pl.BlockSpec((1, D, NC), lambda l: (l, 0, 0))],
        out_specs=pl.BlockSpec((1, N_DEV, NC), lambda l: (l, 0, 0)),
        out_shape=_sds((L, N_DEV, NC), F32),
        compiler_params=pltpu.CompilerParams(dimension_semantics=("arbitrary",), vmem_limit_bytes=VMEM_LIMIT),
    )(c_all, w_ada)


def _adamw(w, g, m, v):
    m = ADAM_B1 * m + (1.0 - ADAM_B1) * g
    v = ADAM_B2 * v + (1.0 - ADAM_B2) * (g * g)
    m_hat = m / (1.0 - ADAM_B1 ** ADAM_STEP)
    v_hat = v / (1.0 - ADAM_B2 ** ADAM_STEP)
    delta = -ADAM_LR * (m_hat / (jnp.sqrt(v_hat) + ADAM_EPS) + ADAM_WD * w)
    return delta, m, v


def _ada_bwd_adamw(c_all_t, dmod_cols, w, m, v):
    L, D, NC = w.shape
    tr = min(D, 256)

    def body(ct_ref, dm_ref, w_ref, m_ref, v_ref, g_ref, d_ref, mo_ref, vo_ref):
        cond_t = _silu(ct_ref[...])
        dm = dm_ref[0]
        g = cond_t[:, 0:1] * dm[0:1, :]
        for b in range(1, N_DEV):
            g = g + cond_t[:, b:b + 1] * dm[b:b + 1, :]
        g_ref[0] = g
        d_ref[0], mo_ref[0], vo_ref[0] = _adamw(w_ref[0], g, m_ref[0], v_ref[0])

    wspec = pl.BlockSpec((1, tr, NC), lambda l, r: (l, r, 0))
    return pl.pallas_call(
        body, name="ada_bwd_adamw", grid=(L, D // tr),
        in_specs=[pl.BlockSpec((tr, N_DEV), lambda l, r: (r, 0)),
                  pl.BlockSpec((1, N_DEV, NC), lambda l, r: (l, 0, 0)), wspec, wspec, wspec],
        out_specs=[wspec] * 4, out_shape=[_sds(w.shape, F32)] * 4,
        compiler_params=pltpu.CompilerParams(dimension_semantics=("parallel", "parallel"), vmem_limit_bytes=VMEM_LIMIT),
    )(c_all_t, dmod_cols, w, m, v)


def _sum_devices(x):
    def body(x_ref, o_ref):
        s = x_ref[0]
        for j in range(1, N_DEV):
            s = s + x_ref[j]
        o_ref[...] = s

    return pl.pallas_call(body, name="sum_devices", out_shape=_sds(x.shape[1:], F32))(x)


def _adamw_small(w, g, m, v):
    def body(w_ref, g_ref, m_ref, v_ref, d_ref, mo_ref, vo_ref):
        d_ref[...], mo_ref[...], vo_ref[...] = _adamw(w_ref[...], g_ref[...], m_ref[...], v_ref[...])

    return pl.pallas_call(body, name="adamw_small", out_shape=[_sds(w.shape, F32)] * 3)(w, g, m, v)


def _me():
    return lax.axis_index("x") * 4 + lax.axis_index("y") * 2 + lax.axis_index("c")


def _peer(k):
    x, y, c = lax.axis_index("x"), lax.axis_index("y"), lax.axis_index("c")
    px = 1 - x if k & 4 else x
    py = 1 - y if k & 2 else y
    pc = 1 - c if k & 1 else c
    return (px, py, pc), px * 4 + py * 2 + pc


VMEM_SPEC = pl.BlockSpec(memory_space=pltpu.VMEM)
ANY_SPEC = pl.BlockSpec(memory_space=pl.ANY)
def _comm_sems(n):
    return [pltpu.SemaphoreType.DMA((n * (N_DEV - 1),)), pltpu.SemaphoreType.DMA((n * (N_DEV - 1),)),
            pltpu.SemaphoreType.DMA((n,))]


def _exchange_copies(srcs_of, dst_refs, send_sems, recv_sems, local_sems):
    me = _me()
    local, sends, recvs = [], [], []
    for a, (src_of, dst_ref) in enumerate(zip(srcs_of, dst_refs)):
        local.append(pltpu.make_async_copy(src_of(me), dst_ref.at[me], local_sems.at[a]))
        for k in range(1, N_DEV):
            dev, pj = _peer(k)
            i = a * (N_DEV - 1) + k - 1
            sems = dict(send_sem=send_sems.at[i], recv_sem=recv_sems.at[i], device_id=dev, device_id_type=MESH_IDS)
            sends.append(pltpu.make_async_remote_copy(src_ref=src_of(pj), dst_ref=dst_ref.at[me], **sems))
            recvs.append(pltpu.make_async_remote_copy(src_ref=src_of(pj), dst_ref=dst_ref.at[pj], **sems))
    return local, sends, recvs


def _start_exchange(copies):
    local, sends, _ = copies
    for cp in local + sends:
        cp.start()


def _finish_exchange(copies):
    local, sends, recvs = copies
    for cp in recvs:
        cp.wait_recv()
    for cp in sends:
        cp.wait_send()
    for cp in local:
        cp.wait()


def _sum_adamw(recv, w, m, v):
    shape = w.shape
    C = shape[-1]
    R = w.size // C
    rows = max(d for d in range(16, min(R, 512) + 1, 16) if R % d == 0 and d * C <= 256 * 1024)

    def body(r_ref, w_ref, m_ref, v_ref, go_ref, d_ref, mo_ref, vo_ref):
        g = r_ref[0].astype(F32)
        for j in range(1, N_DEV):
            g = g + r_ref[j].astype(F32)
        go_ref[...] = g
        d_ref[...], mo_ref[...], vo_ref[...] = _adamw(w_ref[...], g, m_ref[...], v_ref[...])

    spec = pl.BlockSpec((rows, C), lambda i: (i, 0))
    outs = pl.pallas_call(
        body, name="sum_adamw", grid=(R // rows,),
        in_specs=[pl.BlockSpec((N_DEV, rows, C), lambda i: (0, i, 0)), spec, spec, spec],
        out_specs=[spec] * 4, out_shape=[_sds((R, C), F32)] * 4,
        compiler_params=pltpu.CompilerParams(dimension_semantics=("parallel",), vmem_limit_bytes=VMEM_LIMIT),
    )(recv.reshape(N_DEV, R, C), w.reshape(R, C), m.reshape(R, C), v.reshape(R, C))
    return [o.reshape(shape) for o in outs]


def _all_gather(name, x, out_dtype):
    R, C = x.shape
    cast = out_dtype != x.dtype

    def body(x_ref, out_ref, buf, send_sems, recv_sems, local_sem):
        me = _me()
        if cast:
            buf[...] = x_ref[...].astype(out_dtype)
            src = buf
        else:
            src = x_ref
        local = pltpu.make_async_copy(src, out_ref.at[me], local_sem)
        local.start()
        sends = []
        for k in range(1, N_DEV):
            dev, _ = _peer(k)
            cp = pltpu.make_async_remote_copy(src_ref=src, dst_ref=out_ref.at[me], send_sem=send_sems.at[k - 1],
                                              recv_sem=recv_sems.at[k - 1], device_id=dev, device_id_type=MESH_IDS)
            cp.start()
            sends.append(cp)
        for k in range(1, N_DEV):
            dev, pj = _peer(k)
            pltpu.make_async_remote_copy(src_ref=src, dst_ref=out_ref.at[pj], send_sem=send_sems.at[k - 1],
                                         recv_sem=recv_sems.at[k - 1], device_id=dev, device_id_type=MESH_IDS).wait_recv()
        for cp in sends:
            cp.wait_send()
        local.wait()

    return pl.pallas_call(
        body, name=name, in_specs=[VMEM_SPEC], out_specs=ANY_SPEC, out_shape=_sds((N_DEV, R, C), out_dtype),
        scratch_shapes=[pltpu.VMEM((R, C) if cast else (8, 128), out_dtype),
                        pltpu.SemaphoreType.DMA((N_DEV - 1,)), pltpu.SemaphoreType.DMA((N_DEV - 1,)),
                        pltpu.SemaphoreType.DMA(())],
        compiler_params=pltpu.CompilerParams(vmem_limit_bytes=VMEM_LIMIT),
    )(x)


def _all_to_all(name, x):
    _, R, C = x.shape

    def body(x_ref, out_ref, send_sems, recv_sems, local_sem):
        me = _me()
        local = pltpu.make_async_copy(x_ref.at[me], out_ref.at[me], local_sem)
        local.start()
        sends = []
        for k in range(1, N_DEV):
            dev, pj = _peer(k)
            cp = pltpu.make_async_remote_copy(src_ref=x_ref.at[pj], dst_ref=out_ref.at[me], send_sem=send_sems.at[k - 1],
                                              recv_sem=recv_sems.at[k - 1], device_id=dev, device_id_type=MESH_IDS)
            cp.start()
            sends.append(cp)
        for k in range(1, N_DEV):
            dev, pj = _peer(k)
            pltpu.make_async_remote_copy(src_ref=x_ref.at[pj], dst_ref=out_ref.at[pj], send_sem=send_sems.at[k - 1],
                                         recv_sem=recv_sems.at[k - 1], device_id=dev, device_id_type=MESH_IDS).wait_recv()
        for cp in sends:
            cp.wait_send()
        local.wait()

    return pl.pallas_call(
        body, name=name, in_specs=[VMEM_SPEC], out_specs=VMEM_SPEC, out_shape=_sds(x.shape, x.dtype),
        scratch_shapes=[pltpu.SemaphoreType.DMA((N_DEV - 1,)), pltpu.SemaphoreType.DMA((N_DEV - 1,)),
                        pltpu.SemaphoreType.DMA(())],
    )(x)


def _reduce_scatter_adamw(name, gblk, w, m, v):
    _, R, C = gblk.shape
    rows = 8
    for cand in (136, 128, 80, 64, 40, 32, 16, 8):
        if R % cand == 0:
            rows = cand
            break

    def body(g_ref, w_ref, m_ref, v_ref, go_ref, d_ref, mo_ref, vo_ref, recv, send_sems, recv_sems, local_sem):
        me = _me()
        local = pltpu.make_async_copy(g_ref.at[me], recv.at[me], local_sem)
        local.start()
        sends = []
        for k in range(1, N_DEV):
            dev, pj = _peer(k)
            cp = pltpu.make_async_remote_copy(src_ref=g_ref.at[pj], dst_ref=recv.at[me], send_sem=send_sems.at[k - 1],
                                              recv_sem=recv_sems.at[k - 1], device_id=dev, device_id_type=MESH_IDS)
            cp.start()
            sends.append(cp)
        for k in range(1, N_DEV):
            dev, pj = _peer(k)
            pltpu.make_async_remote_copy(src_ref=g_ref.at[pj], dst_ref=recv.at[pj], send_sem=send_sems.at[k - 1],
                                         recv_sem=recv_sems.at[k - 1], device_id=dev, device_id_type=MESH_IDS).wait_recv()
        local.wait()

        def chunk(i, carry):
            r = pl.ds(pl.multiple_of(i * rows, rows), rows)
            g = recv[0, r, :].astype(F32)
            for j in range(1, N_DEV):
                g = g + recv[j, r, :].astype(F32)
            go_ref[r, :] = g
            d_ref[r, :], mo_ref[r, :], vo_ref[r, :] = _adamw(w_ref[r, :], g, m_ref[r, :], v_ref[r, :])
            return carry

        lax.fori_loop(0, R // rows, chunk, 0)
        for cp in sends:
            cp.wait_send()

    return pl.pallas_call(
        body, name=name, in_specs=[ANY_SPEC, VMEM_SPEC, VMEM_SPEC, VMEM_SPEC], out_specs=[VMEM_SPEC] * 4,
        out_shape=[_sds((R, C), F32)] * 4,
        scratch_shapes=[pltpu.VMEM((N_DEV, R, C), BF16), pltpu.SemaphoreType.DMA((N_DEV - 1,)),
                        pltpu.SemaphoreType.DMA((N_DEV - 1,)), pltpu.SemaphoreType.DMA(())],
        compiler_params=pltpu.CompilerParams(vmem_limit_bytes=VMEM_LIMIT),
    )(gblk, w, m, v)


FIRST_WEIGHTS = ["mla_w_dq", "mla_w_uq", "mla_w_dkv", "mla_w_ukv"]
LATE_WEIGHTS = ["mla_w_o", "swa_w_qkv", "swa_w_o", "w_ff1", "w_ff2"]
ROW_SHARDED = {"mla_w_dq", "mla_w_dkv", "mla_w_o", "swa_w_o", "w_ff2"}


def _unblock(name, blocks):
    sh = blocks.shape[1:]
    if name in ROW_SHARDED:
        return jnp.moveaxis(blocks, 0, 1).reshape(sh[0], N_DEV * sh[1], sh[2])
    return jnp.moveaxis(blocks, 0, 2).reshape(sh[0], sh[1], N_DEV * sh[2])


def _block(name, full):
    L, K, N = full.shape
    if name in ROW_SHARDED:
        return jnp.moveaxis(full.reshape(L, N_DEV, K // N_DEV, N), 1, 0)
    return jnp.moveaxis(full.reshape(L, K, N_DEV, N // N_DEV), 2, 0)


def _rot_cols(w):
    half = QK_ROPE // 2
    return jnp.concatenate([-w[..., half:], w[..., :half]], axis=-1)


def _unrot_cols(gw):
    half = QK_ROPE // 2
    return jnp.concatenate([gw[..., half:], -gw[..., :half]], axis=-1)


def _row(v):
    return v.reshape(1, -1)


def _mlp_block_bwd(dx, sv, w1, w2, g, sc, gt):
    dy, du, dgt = _mlp_bwd_a(dx, sv["y2"], sv["rl"], gt, w2)
    dw2 = _matmul_tn("dw_ff2", sv["act"], dy, BF16)
    dw1 = _matmul_tn("dw_ff1", sv["h2"], du, BF16, column_blocks=True)
    dxo, dsh, da = _mlp_bwd_b(du, sv["x1"], dx, w1, g, sc)
    return dxo, dw1, dw2, dsh, da, dgt


def kernel(x, c, positions, w_ada, b_ada, g_mix, g_mlp, mla_w_dq, mla_g_q, mla_w_uq, mla_w_dkv, mla_g_kv, mla_w_ukv, mla_w_o, swa_w_qkv, swa_b_qkv, swa_sinks, swa_w_o, swa_b_o, w_ff1, w_ff2, g_final, loss_target, m_w_ada, m_b_ada, m_g_mix, m_g_mlp, m_mla_w_dq, m_mla_g_q, m_mla_w_uq, m_mla_w_dkv, m_mla_g_kv, m_mla_w_ukv, m_mla_w_o, m_swa_w_qkv, m_swa_b_qkv, m_swa_sinks, m_swa_w_o, m_swa_b_o, m_w_ff1, m_w_ff2, m_g_final, v_w_ada, v_b_ada, v_g_mix, v_g_mlp, v_mla_w_dq, v_mla_g_q, v_mla_w_uq, v_mla_w_dkv, v_mla_g_kv, v_mla_w_ukv, v_mla_w_o, v_swa_w_qkv, v_swa_b_qkv, v_swa_sinks, v_swa_w_o, v_swa_b_o, v_w_ff1, v_w_ff2, v_g_final):
    S, D = x.shape[1], x.shape[2]
    me = _me()
    x0 = x[0]
    target = loss_target[0]
    big_w = dict(mla_w_dq=mla_w_dq, mla_w_uq=mla_w_uq, mla_w_dkv=mla_w_dkv, mla_w_ukv=mla_w_ukv, mla_w_o=mla_w_o,
                 swa_w_qkv=swa_w_qkv, swa_w_o=swa_w_o, w_ff1=w_ff1, w_ff2=w_ff2)
    big_m = dict(mla_w_dq=m_mla_w_dq, mla_w_uq=m_mla_w_uq, mla_w_dkv=m_mla_w_dkv, mla_w_ukv=m_mla_w_ukv,
                 mla_w_o=m_mla_w_o, swa_w_qkv=m_swa_w_qkv, swa_w_o=m_swa_w_o, w_ff1=m_w_ff1, w_ff2=m_w_ff2)
    big_v = dict(mla_w_dq=v_mla_w_dq, mla_w_uq=v_mla_w_uq, mla_w_dkv=v_mla_w_dkv, mla_w_ukv=v_mla_w_ukv,
                 mla_w_o=v_mla_w_o, swa_w_qkv=v_swa_w_qkv, swa_w_o=v_swa_w_o, w_ff1=v_w_ff1, w_ff2=v_w_ff2)
    groups = {"first": FIRST_WEIGHTS}
    wrows = {n: -(-big_w[n].size // (PACK_COLS * 16)) * 16 for n in FIRST_WEIGHTS}
    offs = {g: np.concatenate([[0], np.cumsum([wrows[n] for n in names])]).astype(int) for g, names in groups.items()}

    def as_rows(n, a, lead=()):
        flat = a.reshape(lead + (-1,))
        pad = wrows[n] * PACK_COLS - flat.shape[-1]
        if pad:
            flat = jnp.pad(flat, ((0, 0),) * len(lead) + ((0, pad),))
        return flat.reshape(lead + (wrows[n], PACK_COLS))

    def pack(g, d):
        return jnp.concatenate([as_rows(n, d[n]) for n in groups[g]], axis=0)

    def pack_blocks(g, gfull):
        return jnp.concatenate([as_rows(n, _block(n, gfull[n]).astype(BF16), (N_DEV,)) for n in groups[g]], axis=1)

    def unpack(g, packed, lead=()):
        out = {}
        for i, n in enumerate(groups[g]):
            part = packed[..., int(offs[g][i]):int(offs[g][i + 1]), :].reshape(lead + (-1,))
            out[n] = part[..., :big_w[n].size].reshape(lead + big_w[n].shape)
        return out

    gathered = _all_gather("gather_weights", pack("first", big_w), BF16)
    wfull = {n: _unblock(n, b) for n, b in unpack("first", gathered, (N_DEV,)).items()}
    w_dq, w_dkv = wfull["mla_w_dq"][0], wfull["mla_w_dkv"][0]
    w_cat = jnp.concatenate([w_dq, w_dkv, _rot_cols(w_dkv[:, KV_LORA:])], axis=1)
    QL = w_dq.shape[1]
    w_uq = wfull["mla_w_uq"][0].reshape(QL, MLA_HEADS, QK_DIM)
    w_uqx = jnp.concatenate([w_uq, _rot_cols(w_uq[..., QK_NOPE:])], axis=-1).reshape(QL, MLA_HEADS * 256)
    w_ukv = wfull["mla_w_ukv"][0]

    L = w_ada.shape[0]
    NC = w_ada.shape[2]
    nbq, nbo = swa_b_qkv.shape[1], swa_b_o.shape[1]
    cpad = -(-(D + nbq + nbo) // 1024) * 1024
    cpack = jnp.pad(jnp.concatenate([c[0], swa_b_qkv[0], swa_b_o[0]]), (0, cpad - (D + nbq + nbo))).reshape(8, cpad // 8)
    call = _all_gather("gather_c", cpack, F32).reshape(N_DEV, cpad)
    c_all = call[:, :D]
    b_qkv_full = call[:, D:D + nbq].reshape(1, N_DEV * nbq)
    b_o_full = call[:, D + nbq:D + nbq + nbo].reshape(1, N_DEV * nbo)
    mod_cols = _ada_fwd(c_all, w_ada)
    mpad = -(-(L * NC) // 1024) * 1024
    mod_send = jnp.pad(jnp.moveaxis(mod_cols, 1, 0).reshape(N_DEV, L * NC), ((0, 0), (0, mpad - L * NC)))
    mod_mine = _all_to_all("exchange_mod", mod_send.reshape(N_DEV, 8, mpad // 8)).reshape(N_DEV, mpad)[:, :L * NC]
    mod = jnp.moveaxis(mod_mine.reshape(N_DEV, L, NC), 0, 1).reshape(L, N_DEV * NC) + b_ada
    mods = mod.reshape(L, 6, 1, D)

    half = QK_ROPE // 2
    inv_freq = ROPE_THETA ** (-jnp.arange(half, dtype=F32) / half)
    ang = positions[0].astype(F32)[:, None] * inv_freq
    cos = jnp.concatenate([jnp.cos(ang), jnp.cos(ang)], axis=-1)
    sin = jnp.concatenate([jnp.sin(ang), jnp.sin(ang)], axis=-1)

    T_ATT = ATT_TILE
    zero_bias = jnp.zeros((1, D), F32)

    sh1, sc1, gt1, sh2, sc2, gt2 = [mods[0, i] for i in range(6)]
    gm0, gp0 = _row(g_mix[0]), _row(g_mlp[0])
    h1, cqp, cq, ckvp, ckv, q, k, v, vt = _mla_in_fwd(x0, cos, sin, gm0, sc1, sh1, w_cat, mla_g_q, w_uqx, mla_g_kv,
                                                      w_ukv, ROW_TILE)
    o0, lse0, gathered = _mla_attn_fwd(q, k, vt, ATT_TILE_FWD, [big_w[n].astype(BF16) for n in LATE_WEIGHTS])
    wfull = {n: _unblock(n, b) for n, b in zip(LATE_WEIGHTS, gathered)}
    w_o_mla, w_qkv, w_o_swa = wfull["mla_w_o"][0], wfull["swa_w_qkv"][0], wfull["swa_w_o"][0]
    ff1, ff2 = wfull["w_ff1"], wfull["w_ff2"]
    y1, x1, h2 = _attn_out_fwd(o0, x0, w_o_mla, zero_bias, gt1, gp0, sc2, sh2)
    rl0, act0, y2, x2 = _mlp_fwd(h2, x1, ff1[0], ff2[0], gt2)
    sv0 = dict(y2=y2, rl=rl0, act=act0, h2=h2, x1=x1)

    th1, tc1, tg1, th2, tc2, tg2 = [mods[1, i] for i in range(6)]
    gm1, gp1 = _row(g_mix[1]), _row(g_mlp[1])
    h3, sq, sk, svv = _swa_in_fwd(x2, gm1, tc1, th1, w_qkv, b_qkv_full)
    o1, lse1 = _swa_attn_fwd(sq, sk, svv, swa_sinks)
    y3, x3, h4 = _attn_out_fwd(o1, x2, w_o_swa, b_o_full, tg1, gp1, tc2, th2)
    rl1, act1, y4, dx4, loss_part, dg_final = _mlp_fwd_loss(h4, x3, ff1[1], ff2[1], tg2, target, _row(g_final))
    sv1 = dict(y2=y4, rl=rl1, act=act1, h2=h4, x1=x3)

    dx3, dw1_1, dw2_1, dsh2_1, da2_1, dgt2_1 = _mlp_block_bwd(dx4, sv1, ff1[1], ff2[1], gp1, tc2, tg2)
    dy, do, dl, dgt1_1, db_o = _attn_out_bwd(dx3, y3, o1, tg1, w_o_swa, SWA_HEADS)
    dw_o_swa = _matmul_tn("dw_o", o1, dy, BF16)
    dsq, dsk, dsv, dsink = _swa_attn_bwd(sq, sk, svv, do, lse1, dl, swa_sinks)
    dqkv, dx2, db_qkv, dsh1_1, da1_1 = _swa_in_bwd(dsq, dsk, dsv, x2, dx3, w_qkv, gm1, tc1)
    dw_qkv = _matmul_tn("dw_qkv", h3, dqkv, BF16)

    dx1, dw1_0, dw2_0, dsh2_0, da2_0, dgt2_0 = _mlp_block_bwd(dx2, sv0, ff1[0], ff2[0], gp0, sc2, gt2)
    dy, do, dl, dgt1_0, _ = _attn_out_bwd(dx1, y1, o0, gt1, w_o_mla, MLA_HEADS)
    dw_o_mla = _matmul_tn("dw_o", o0, dy, BF16)
    tb = min(T_ATT, S)
    delta = dl[:MLA_HEADS].reshape(MLA_HEADS, S // tb, 1, tb)
    glate = dict(mla_w_o=dw_o_mla[None], swa_w_qkv=dw_qkv[None], swa_w_o=dw_o_swa[None],
                 w_ff2=jnp.stack([dw2_0, dw2_1]))
    gblocks = {n: _block(n, g).astype(BF16) for n, g in glate.items()}
    gblocks["w_ff1"] = jnp.stack([dw1_0, dw1_1], axis=1)
    lse_rows = (lse0 * LOG2E).reshape(MLA_HEADS, S // tb, 1, tb)
    dq, dk, dv, recv = _mla_attn_bwd(q, k, v, do, lse_rows, delta, T_ATT, [gblocks[n] for n in LATE_WEIGHTS])
    late = {n: _sum_adamw(r, big_w[n], big_m[n], big_v[n]) for n, r in zip(LATE_WEIGHTS, recv)}
    dqx, dkv, dcat, dx0, dg_q, dg_kv, dsh1_0, da1_0 = _mla_in_bwd(
        dq, dk, dv, cos, sin, cqp, ckvp, x0, dx1, w_uqx, mla_g_q, w_ukv, mla_g_kv, w_cat, gm0, sc1)
    dw_uqx = _matmul_tn("dw_uq", cq, dqx).reshape(QL, MLA_HEADS, 256)
    dw_ukv = _matmul_tn("dw_ukv", ckv, dkv)
    dw_cat = _matmul_tn("dw_down", h1, dcat)
    dw_uq = jnp.concatenate([dw_uqx[..., :QK_NOPE], dw_uqx[..., 128:192] + _unrot_cols(dw_uqx[..., 192:256])],
                            axis=-1).reshape(QL, MLA_HEADS * QK_DIM)
    o_kr = QL + KV_LORA
    dw_dkv = jnp.concatenate([dw_cat[:, QL:o_kr],
                              dw_cat[:, o_kr:o_kr + QK_ROPE] + _unrot_cols(dw_cat[:, o_kr + QK_ROPE:])], axis=1)

    gfirst = dict(mla_w_dq=dw_cat[None, :, :QL], mla_w_uq=dw_uq[None], mla_w_dkv=dw_dkv[None], mla_w_ukv=dw_ukv[None])
    first = _reduce_scatter_adamw("grad_exchange_adamw", pack_blocks("first", gfirst), pack("first", big_w),
                                  pack("first", big_m), pack("first", big_v))
    big_g, big_d, big_nm, big_nv = ({**unpack("first", first[j]), **{n: late[n][j] for n in LATE_WEIGHTS}}
                                    for j in range(4))

    dmod = jnp.stack([
        jnp.concatenate([dsh1_0, gm0 * da1_0, dgt1_0, dsh2_0, gp0 * da2_0, dgt2_0], axis=1),
        jnp.concatenate([dsh1_1, gm1 * da1_1, dgt1_1, dsh2_1, gp1 * da2_1, dgt2_1], axis=1)]).reshape(-1)
    dg_mix = jnp.concatenate([(1.0 + sc1) * da1_0, (1.0 + tc1) * da1_1], axis=1).reshape(-1)
    dg_mlp = jnp.concatenate([(1.0 + sc2) * da2_0, (1.0 + tc2) * da2_1], axis=1).reshape(-1)
    parts = [loss_part.reshape(-1), dmod, dg_mix, dg_mlp, dg_q.reshape(-1), dg_kv.reshape(-1), dsink.reshape(-1),
             dg_final.reshape(-1), db_qkv.reshape(-1), db_o.reshape(-1)]
    soffs = np.concatenate([[0], np.cumsum([p.size for p in parts])])
    spad = -(-int(soffs[-1]) // 1024) * 1024
    spack = jnp.pad(jnp.concatenate(parts), (0, spad - int(soffs[-1]))).reshape(8, spad // 8)
    sall = _all_gather("gather_small_grads", spack, F32)
    ssum = _sum_devices(sall).reshape(-1)
    tot = [ssum[int(soffs[i]):int(soffs[i + 1])] for i in range(len(parts))]
    loss = tot[0][0]
    nsink = swa_sinks.shape[1]
    small_g = dict(b_ada=tot[1].reshape(b_ada.shape), g_mix=tot[2].reshape(g_mix.shape), g_mlp=tot[3].reshape(g_mlp.shape),
                   mla_g_q=tot[4].reshape(mla_g_q.shape), mla_g_kv=tot[5].reshape(mla_g_kv.shape),
                   swa_sinks=tot[6][:nsink].reshape(swa_sinks.shape), g_final=tot[7].reshape(g_final.shape),
                   swa_b_qkv=lax.dynamic_slice(tot[8], (me * nbq,), (nbq,)).reshape(swa_b_qkv.shape),
                   swa_b_o=lax.dynamic_slice(tot[9], (me * nbo,), (nbo,)).reshape(swa_b_o.shape))
    small_w = dict(b_ada=b_ada, g_mix=g_mix, g_mlp=g_mlp, mla_g_q=mla_g_q, mla_g_kv=mla_g_kv, swa_sinks=swa_sinks,
                   g_final=g_final, swa_b_qkv=swa_b_qkv, swa_b_o=swa_b_o)
    small_m = dict(b_ada=m_b_ada, g_mix=m_g_mix, g_mlp=m_g_mlp, mla_g_q=m_mla_g_q, mla_g_kv=m_mla_g_kv,
                   swa_sinks=m_swa_sinks, g_final=m_g_final, swa_b_qkv=m_swa_b_qkv, swa_b_o=m_swa_b_o)
    small_v = dict(b_ada=v_b_ada, g_mix=v_g_mix, g_mlp=v_g_mlp, mla_g_q=v_mla_g_q, mla_g_kv=v_mla_g_kv,
                   swa_sinks=v_swa_sinks, g_final=v_g_final, swa_b_qkv=v_swa_b_qkv, swa_b_o=v_swa_b_o)
    SMALL = list(small_w)
    woffs = np.concatenate([[0], np.cumsum([small_w[n].size for n in SMALL])])
    wpad = -(-int(woffs[-1]) // 1024) * 1024

    def spack_of(d):
        flat = jnp.concatenate([d[n].reshape(-1) for n in SMALL])
        return jnp.pad(flat, (0, wpad - int(woffs[-1]))).reshape(8, wpad // 8)

    sm = _adamw_small(spack_of(small_w), spack_of(small_g), spack_of(small_m), spack_of(small_v))
    small_d, small_nm, small_nv = (
        {n: a.reshape(-1)[int(woffs[i]):int(woffs[i + 1])].reshape(small_w[n].shape) for i, n in enumerate(SMALL)}
        for a in sm)

    b_off = int(soffs[1])
    dmod_all = sall.reshape(N_DEV, -1)[:, b_off:b_off + L * N_DEV * NC].reshape(N_DEV, L, N_DEV * NC)
    dmod_cols = jnp.moveaxis(lax.dynamic_slice_in_dim(dmod_all, me * NC, NC, axis=2), 0, 1)
    ada_g, ada_d, ada_nm, ada_nv = _ada_bwd_adamw(c_all.T, dmod_cols, w_ada, m_w_ada, v_w_ada)

    order = ["w_ada", "b_ada", "g_mix", "g_mlp", "mla_w_dq", "mla_g_q", "mla_w_uq", "mla_w_dkv", "mla_g_kv",
             "mla_w_ukv", "mla_w_o", "swa_w_qkv", "swa_b_qkv", "swa_sinks", "swa_w_o", "swa_b_o", "w_ff1", "w_ff2", "g_final"]

    def collect(ada, big, small):
        return [ada if n == "w_ada" else (big[n] if n in big else small[n]) for n in order]

    return (loss, dx0.reshape(x.shape), *collect(ada_g, big_g, small_g), *collect(ada_d, big_d, small_d),
            *collect(ada_nm, big_nm, small_nm), *collect(ada_nv, big_nv, small_nv))
```

```python
import jax
import jax.numpy as jnp
import numpy as np
from jax import lax
from jax.experimental import pallas as pl
from jax.experimental.pallas import tpu as pltpu

F32 = jnp.float32
BF16 = jnp.bfloat16
MESH_IDS = pl.DeviceIdType.MESH
N_DEV = 8

MLA_HEADS = 8
QK_NOPE = 128
QK_ROPE = 64
QK_DIM = QK_NOPE + QK_ROPE
V_DIM = 128
KV_LORA = 256
ROPE_THETA = 10000.0
SWA_HEADS = 16
SWA_KV_HEADS = 4
SWA_GROUP = SWA_HEADS // SWA_KV_HEADS
SWA_HEAD_DIM = 64
WINDOW = 128
EPS = 1e-6
LOG2E = 1.4426950408889634

ADAM_LR = 0.001
ADAM_B1 = 0.9
ADAM_B2 = 0.999
ADAM_EPS = 1e-08
ADAM_WD = 0.01
ADAM_STEP = 10

PACK_COLS = 1024
VMEM_LIMIT = 56 << 20
ROW_TILE = 512
ROW_TILE_WIDE = 256
ROW_TILE_BWD = 512
ATT_TILE = 512
ATT_TILE_FWD = 1024
TN_TOKENS = 4096


def _dot(a, b):
    return jnp.dot(a, b, preferred_element_type=F32)


def _dot_nt(a, b):
    return lax.dot_general(a, b, (((1,), (1,)), ((), ())), preferred_element_type=F32)


def _dot_tn(a, b):
    return lax.dot_general(a, b, (((0,), (0,)), ((), ())), preferred_element_type=F32)


def _rstd(x):
    return lax.rsqrt(jnp.mean(x * x, axis=-1, keepdims=True) + EPS)


def _rms_bwd(dn, n, r):
    return r * (dn - n * jnp.mean(dn * n, axis=-1, keepdims=True))


def _modulate(x, g, sc, sh):
    r = _rstd(x)
    return ((x * r) * g) * (1.0 + sc) + sh


def _modulate_bwd(dh, x, g, sc):
    r = _rstd(x)
    n = x * r
    dsh = jnp.sum(dh, axis=0, keepdims=True)
    da = jnp.sum(dh * n, axis=0, keepdims=True)
    dx = _rms_bwd(dh * (g * (1.0 + sc)), n, r)
    return dx, dsh, da


def _acc(ref, val, i):
    @pl.when(i == 0)
    def _():
        ref[...] = val

    @pl.when(i != 0)
    def _():
        ref[...] += val


def _row_spec(shape, tm):
    nd = len(shape)
    return pl.BlockSpec(tuple(shape[:nd - 2]) + (tm, shape[-1]), lambda i: (0,) * (nd - 2) + (i, 0))


def _resident_spec(shape, single_buffer):
    nd = len(shape)
    if single_buffer:
        return pl.BlockSpec(tuple(shape), lambda i: (0,) * nd, pipeline_mode=pl.Buffered(1))
    return pl.BlockSpec(tuple(shape), lambda i: (0,) * nd)


def _rowcall(name, body, tokens, tm, row_in, full_in, row_out, acc_out=()):
    tm = min(tm, tokens)
    in_specs = [_row_spec(a.shape, tm) for a in row_in] + [_resident_spec(a.shape, True) for a in full_in]
    row_specs = [s[1] if isinstance(s, tuple) else _row_spec(s.shape, tm) for s in row_out]
    row_out = [s[0] if isinstance(s, tuple) else s for s in row_out]
    out_specs = row_specs + [_resident_spec(s.shape, False) for s in acc_out]
    return pl.pallas_call(
        body, name=name, grid=(tokens // tm,), in_specs=in_specs, out_specs=out_specs,
        out_shape=list(row_out) + list(acc_out),
        compiler_params=pltpu.CompilerParams(dimension_semantics=("arbitrary",), vmem_limit_bytes=VMEM_LIMIT),
    )(*row_in, *full_in)


def _sds(shape, dtype):
    return jax.ShapeDtypeStruct(tuple(shape), dtype)


def _mla_in_fwd(x, cos, sin, g, sc, sh, w_cat, g_q, w_uqx, g_kv, w_ukv, t):
    S, D = x.shape
    QL = g_q.shape[1]
    H = MLA_HEADS
    t = min(t, S)

    def body(x_ref, cos_ref, sin_ref, g_ref, sc_ref, sh_ref, wcat_ref, gq_ref, wuqx_ref, gkv_ref, wukv_ref,
             h_ref, cqp_ref, cq_ref, ckvp_ref, ckv_ref, q_ref, k_ref, v_ref, vt_ref):
        cs, sn = cos_ref[...], sin_ref[...]
        hb = _modulate(x_ref[...], g_ref[...], sc_ref[...], sh_ref[...]).astype(BF16)
        h_ref[...] = hb
        low = _dot(hb, wcat_ref[...])
        cqp = low[:, :QL]
        cqp_ref[...] = cqp
        cq = ((cqp * _rstd(cqp)) * gq_ref[...]).astype(BF16)
        cq_ref[...] = cq
        ckvp = low[:, QL:QL + KV_LORA]
        ckvp_ref[...] = ckvp
        ckv = ((ckvp * _rstd(ckvp)) * gkv_ref[...]).astype(BF16)
        ckv_ref[...] = ckv
        o = QL + KV_LORA
        kr = (low[:, o:o + QK_ROPE] * cs + low[:, o + QK_ROPE:o + 2 * QK_ROPE] * sn).astype(BF16)
        qx = _dot(cq, wuqx_ref[...])
        kv = _dot(ckv, wukv_ref[...])
        for hd in range(H):
            b = hd * 256
            q_ref[hd, :, 0:QK_NOPE] = qx[:, b:b + QK_NOPE].astype(BF16)
            q_ref[hd, :, QK_NOPE:QK_DIM] = (qx[:, b + 128:b + 192] * cs + qx[:, b + 192:b + 256] * sn).astype(BF16)
            k_ref[hd, :, 0:QK_NOPE] = kv[:, b:b + QK_NOPE].astype(BF16)
            k_ref[hd, :, QK_NOPE:QK_DIM] = kr
            vh = kv[:, b + 128:b + 256]
            v_ref[hd] = vh.astype(BF16)
            vt_ref[hd, 0, 0:V_DIM, :] = vh.T.astype(BF16)
            vt_ref[hd, 0, V_DIM:2 * V_DIM, :] = jnp.ones((V_DIM, x_ref.shape[0]), BF16)

    vt_spec = pl.BlockSpec((H, 1, 2 * V_DIM, t), lambda i: (0, i, 0, 0))
    return _rowcall(
        "mla_in_fwd", body, S, t, [x, cos, sin], [g, sc, sh, w_cat, g_q, w_uqx, g_kv, w_ukv],
        [_sds((S, D), BF16), _sds((S, QL), F32), _sds((S, QL), BF16), _sds((S, KV_LORA), F32), _sds((S, KV_LORA), BF16),
         _sds((H, S, QK_DIM), BF16), _sds((H, S, QK_DIM), BF16), _sds((H, S, V_DIM), BF16),
         (_sds((H, S // t, 2 * V_DIM, t), BF16), vt_spec)])


def _mla_attn_fwd(q, k, vt, t, sends):
    H, S, DQ = q.shape
    DV = V_DIM
    vb = vt.shape[-1]
    t = max(min(t, S), vb)
    nb = S // t
    scale = QK_DIM ** -0.5
    c2 = scale * LOG2E

    ns = len(sends)

    def body(q_ref, k_ref, vt_ref, *rest):
        send_refs, (o_ref, lse_ref), gath_refs = rest[:ns], rest[ns:ns + 2], rest[ns + 2:2 * ns + 2]
        m_s, acc_s, s_buf, send_sems, recv_sems, local_sems = rest[2 * ns + 2:]
        hd, qi = pl.program_id(0), pl.program_id(1)

        def gather():
            return _exchange_copies([lambda j, r=r: r for r in send_refs], gath_refs, send_sems, recv_sems, local_sems)

        @pl.when((hd == 0) & (qi == 0))
        def _():
            _start_exchange(gather())

        m_s[...] = jnp.full_like(m_s, -jnp.inf)
        acc_s[...] = jnp.zeros_like(acc_s)

        def scores(j, slot):
            rows = pl.ds(pl.multiple_of(j * t, t), t)
            s_buf[slot] = _dot_nt(k_ref[0, rows, :], q_ref[0])

        nvb = t // vb

        def update(s, j, blocks, cols):
            m_prev = m_s[:, cols]
            m_new = jnp.maximum(m_prev, jnp.max(s, axis=0, keepdims=True))
            alpha = jnp.exp2((m_prev - m_new) * c2)
            pb = jnp.exp2((s - m_new) * c2).astype(BF16)
            acc = alpha * acc_s[:, cols]
            for n, u in enumerate(blocks):
                acc = acc + _dot(vt_ref[0, j * nvb + u], pb[n * vb:(n + 1) * vb, :])
            acc_s[:, cols] = acc
            m_s[:, cols] = m_new

        def causal(s, shape):
            key = lax.broadcasted_iota(jnp.int32, shape, 0)
            qry = lax.broadcasted_iota(jnp.int32, shape, 1)
            return jnp.where(key <= qry, s, -jnp.inf)

        def absorb(j, slot, diagonal):
            if not diagonal:
                update(s_buf[slot], j, range(nvb), slice(None))
            elif nvb % 2:
                update(causal(s_buf[slot], (t, t)), j, range(nvb), slice(None))
            else:
                half = t // 2
                update(causal(s_buf[slot, :half], (half, t)), j, range(nvb // 2), slice(None))
                update(causal(s_buf[slot, half:, half:], (half, half)), j, range(nvb // 2, nvb), slice(half, t))

        def pair(i, carry):
            j = 2 * i
            scores(j + 1, 1)
            absorb(j, 0, False)
            scores(j + 2, 0)
            absorb(j + 1, 1, False)
            return carry

        scores(0, 0)
        lax.fori_loop(0, qi // 2, pair, 0)

        @pl.when(qi % 2 == 0)
        def _():
            absorb(qi, 0, True)

        @pl.when(qi % 2 == 1)
        def _():
            scores(qi, 1)
            absorb(qi - 1, 0, False)
            absorb(qi, 1, True)

        acc = acc_s[...]
        o_ref[...] = (acc[:DV] / acc[DV:]).T.astype(BF16)
        lse_ref[0, 0] = m_s[...] * scale + jnp.log(acc[DV:DV + 1])

        @pl.when((hd == H - 1) & (qi == nb - 1))
        def _():
            _finish_exchange(gather())

    outs = pl.pallas_call(
        body, name="mla_attn_fwd", grid=(H, nb),
        in_specs=[pl.BlockSpec((1, t, DQ), lambda h, i: (h, i, 0)),
                  pl.BlockSpec((1, S, DQ), lambda h, i: (h, 0, 0)),
                  pl.BlockSpec((1, S // vb, 2 * DV, vb), lambda h, i: (h, 0, 0, 0))] + [ANY_SPEC] * ns,
        out_specs=[pl.BlockSpec((t, DV), lambda h, i: (i, h)),
                   pl.BlockSpec((1, 1, 1, t), lambda h, i: (h, i, 0, 0))] + [ANY_SPEC] * ns,
        out_shape=[_sds((S, H * DV), BF16), _sds((H, nb, 1, t), F32)]
        + [_sds((N_DEV,) + a.shape, a.dtype) for a in sends],
        scratch_shapes=[pltpu.VMEM((1, t), F32), pltpu.VMEM((2 * DV, t), F32), pltpu.VMEM((2, t, t), F32)]
        + _comm_sems(ns),
        compiler_params=pltpu.CompilerParams(dimension_semantics=("arbitrary", "arbitrary"),
                                             vmem_limit_bytes=VMEM_LIMIT),
    )(q, k, vt, *sends)
    return outs[0], outs[1], outs[2:]


def _attn_out_fwd(o, x, w_o, b_o, gt, g, sc, sh):
    S, D = x.shape

    def body(o_ref, x_ref, wo_ref, bo_ref, gt_ref, g_ref, sc_ref, sh_ref, y_ref, x1_ref, h_ref):
        y = _dot(o_ref[...], wo_ref[...]) + bo_ref[...]
        y_ref[...] = y.astype(BF16)
        x1 = x_ref[...] + gt_ref[...] * y
        x1_ref[...] = x1
        h_ref[...] = _modulate(x1, g_ref[...], sc_ref[...], sh_ref[...]).astype(BF16)

    return _rowcall("attn_out_fwd", body, S, ROW_TILE, [o, x], [w_o, b_o, gt, g, sc, sh],
                    [_sds((S, D), BF16), _sds((S, D), F32), _sds((S, D), BF16)])


def _mlp_fwd(h, x, w1, w2, gt):
    S, D = x.shape
    FF = w1.shape[1]

    def body(h_ref, x_ref, w1_ref, w2_ref, gt_ref, rl_ref, act_ref, y_ref, x2_ref):
        rl = jnp.maximum(_dot(h_ref[...], w1_ref[...]), 0.0)
        rl_ref[...] = rl.astype(BF16)
        act = (rl * rl).astype(BF16)
        act_ref[...] = act
        y = _dot(act, w2_ref[...])
        y_ref[...] = y.astype(BF16)
        x2_ref[...] = x_ref[...] + gt_ref[...] * y

    return _rowcall("mlp_fwd", body, S, ROW_TILE_WIDE, [h, x], [w1, w2, gt],
                    [_sds((S, FF), BF16), _sds((S, FF), BF16), _sds((S, D), BF16), _sds((S, D), F32)])


def _final_norm_loss(xv, target, g, d_model):
    r = _rstd(xv)
    n = xv * r
    err = n * g - target
    part = 0.5 * jnp.sum(jnp.mean(err * err, axis=-1, keepdims=True), axis=0, keepdims=True)
    dout = err / d_model
    return part, _rms_bwd(dout * g, n, r), jnp.sum(dout * n, axis=0, keepdims=True)


def _mlp_fwd_loss(h, x, w1, w2, gt, target, g_final):
    S, D = x.shape
    FF = w1.shape[1]

    def body(h_ref, x_ref, t_ref, w1_ref, w2_ref, gt_ref, g_ref, rl_ref, act_ref, y_ref, dx_ref, loss_ref, dg_ref):
        i = pl.program_id(0)
        rl = jnp.maximum(_dot(h_ref[...], w1_ref[...]), 0.0)
        rl_ref[...] = rl.astype(BF16)
        act = (rl * rl).astype(BF16)
        act_ref[...] = act
        y = _dot(act, w2_ref[...])
        y_ref[...] = y.astype(BF16)
        part, dx, dg = _final_norm_loss(x_ref[...] + gt_ref[...] * y, t_ref[...], g_ref[...], D)
        dx_ref[...] = dx
        _acc(loss_ref, jnp.broadcast_to(part, loss_ref.shape), i)
        _acc(dg_ref, dg, i)

    return _rowcall("mlp_fwd_loss", body, S, ROW_TILE_WIDE, [h, x, target], [w1, w2, gt, g_final],
                    [_sds((S, FF), BF16), _sds((S, FF), BF16), _sds((S, D), BF16), _sds((S, D), F32)],
                    [_sds((1, 128), F32), _sds((1, D), F32)])


def _swa_in_fwd(x, g, sc, sh, w_qkv, b_qkv):
    S, D = x.shape
    NQ = SWA_HEADS * SWA_HEAD_DIM
    NK = SWA_KV_HEADS * SWA_HEAD_DIM

    def body(x_ref, g_ref, sc_ref, sh_ref, w_ref, b_ref, h_ref, q_ref, k_ref, v_ref):
        hb = _modulate(x_ref[...], g_ref[...], sc_ref[...], sh_ref[...]).astype(BF16)
        h_ref[...] = hb
        qkv = _dot(hb, w_ref[...]) + b_ref[...]
        q_ref[...] = qkv[:, :NQ].astype(BF16)
        k_ref[...] = qkv[:, NQ:NQ + NK].astype(BF16)
        v_ref[...] = qkv[:, NQ + NK:].astype(BF16)

    return _rowcall("swa_in_fwd", body, S, ROW_TILE, [x], [g, sc, sh, w_qkv, b_qkv],
                    [_sds((S, D), BF16), _sds((S, NQ), BF16), _sds((S, NK), BF16), _sds((S, NK), BF16)])


def _alibi_slope(head):
    return float(np.float32(2.0 ** (-8.0 * (head + 1) / SWA_HEADS)))


def _swa_geometry(n):
    W, G = WINDOW, SWA_GROUP
    key = lax.broadcasted_iota(jnp.int32, (2 * W, G * W), 0)
    qry = lax.broadcasted_iota(jnp.int32, (2 * W, G * W), 1) & (W - 1)
    dist = W + qry - key
    valid = (dist >= 0) & (dist < W) & ((n > 0) | (key >= W))
    return dist.astype(F32), valid


def _swa_group(kh, q_ref, sink_ref):
    W, G, Dh = WINDOW, SWA_GROUP, SWA_HEAD_DIM
    heads = [kh * G + g for g in range(G)]
    q4 = jnp.concatenate([q_ref[:, h * Dh:(h + 1) * Dh] for h in heads], axis=0)
    slopes = jnp.concatenate([jnp.full((1, W), _alibi_slope(h), F32) for h in heads], axis=1)
    sinks = jnp.concatenate([jnp.broadcast_to(sink_ref[:, h:h + 1], (1, W)) for h in heads], axis=1)
    return heads, q4, slopes, sinks


def _swa_band_specs(W, nb, cols):
    prev = pl.BlockSpec((W, cols), lambda n: (jnp.maximum(jnp.minimum(n, nb - 1) - 1, 0), 0))
    cur = pl.BlockSpec((W, cols), lambda n: (jnp.minimum(n, nb - 1), 0))
    return prev, cur


def _swa_attn_fwd(q, k, v, sinks):
    S, NQ = q.shape
    NK = k.shape[1]
    W, Dh, G = WINDOW, SWA_HEAD_DIM, SWA_GROUP
    nb = S // W

    def body(q_ref, kp_ref, kc_ref, vp_ref, vc_ref, sink_ref, o_ref, lse_ref):
        distf, valid = _swa_geometry(pl.program_id(0))
        kband = jnp.concatenate([kp_ref[...], kc_ref[...]], axis=0)
        vband_t = jnp.concatenate([vp_ref[...], vc_ref[...]], axis=0).astype(F32).T.astype(BF16)
        outs = []
        for kh in range(SWA_KV_HEADS):
            kb = kband[:, kh * Dh:(kh + 1) * Dh]
            vbt = vband_t[kh * Dh:(kh + 1) * Dh, :]
            heads, q4, slopes, sinks = _swa_group(kh, q_ref, sink_ref)
            s = _dot_nt(kb, q4) * (Dh ** -0.5) - slopes * distf
            s = jnp.where(valid, s, -jnp.inf)
            m = jnp.maximum(jnp.max(s, axis=0, keepdims=True), sinks)
            p = jnp.exp(s - m)
            denom = jnp.sum(p, axis=0, keepdims=True) + jnp.exp(sinks - m)
            out4 = _dot(vbt, (p * (1.0 / denom)).astype(BF16))
            lse4 = m + jnp.log(denom)
            for g, h in enumerate(heads):
                outs.append(out4[:, g * W:(g + 1) * W])
                lse_ref[h:h + 1, :] = lse4[:, g * W:(g + 1) * W]
        o_ref[...] = jnp.concatenate(outs, axis=0).T.astype(BF16)

    kprev, kcur = _swa_band_specs(W, nb, NK)
    return pl.pallas_call(
        body, name="swa_attn_fwd", grid=(nb,),
        in_specs=[pl.BlockSpec((W, NQ), lambda n: (n, 0)), kprev, kcur, kprev, kcur,
                  pl.BlockSpec((1, SWA_HEADS), lambda n: (0, 0))],
        out_specs=[pl.BlockSpec((W, NQ), lambda n: (n, 0)), pl.BlockSpec((SWA_HEADS, W), lambda n: (0, n))],
        out_shape=[_sds((S, NQ), BF16), _sds((SWA_HEADS, S), F32)],
        compiler_params=pltpu.CompilerParams(dimension_semantics=("arbitrary",), vmem_limit_bytes=VMEM_LIMIT),
    )(q, k, k, v, v, sinks)


def _mlp_bwd_a(dx, y, rl, gt, w2):
    S, D = dx.shape
    FF = rl.shape[1]

    def body(dx_ref, y_ref, rl_ref, gt_ref, w2_ref, dy_ref, du_ref, dgt_ref):
        i = pl.program_id(0)
        dxv = dx_ref[...]
        _acc(dgt_ref, jnp.sum(dxv * y_ref[...].astype(F32), axis=0, keepdims=True), i)
        dy = (dxv * gt_ref[...]).astype(BF16)
        dy_ref[...] = dy
        dact = _dot_nt(dy, w2_ref[...])
        du_ref[...] = (dact * (2.0 * rl_ref[...].astype(F32))).astype(BF16)

    return _rowcall("mlp_bwd_a", body, S, ROW_TILE_BWD, [dx, y, rl], [gt, w2],
                    [_sds((S, D), BF16), _sds((S, FF), BF16)], [_sds((1, D), F32)])


def _mlp_bwd_b(du, x, dx, w1, g, sc):
    S, D = x.shape

    def body(du_ref, x_ref, dx_ref, w1_ref, g_ref, sc_ref, dxo_ref, dsh_ref, da_ref):
        i = pl.program_id(0)
        dh = _dot_nt(du_ref[...], w1_ref[...])
        dxn, dsh, da = _modulate_bwd(dh, x_ref[...], g_ref[...], sc_ref[...])
        dxo_ref[...] = dx_ref[...] + dxn
        _acc(dsh_ref, dsh, i)
        _acc(da_ref, da, i)

    return _rowcall("mlp_bwd_b", body, S, ROW_TILE_BWD, [du, x, dx], [w1, g, sc],
                    [_sds((S, D), F32)], [_sds((1, D), F32), _sds((1, D), F32)])


def _attn_out_bwd(dx, y, o, gt, w_o, n_heads):
    S, D = dx.shape
    NO = o.shape[1]
    dh = NO // n_heads
    member = (jnp.arange(NO)[None, :] // dh == jnp.arange(16)[:, None]).astype(BF16)

    def body(dx_ref, y_ref, o_ref, gt_ref, wo_ref, mem_ref, dy_ref, do_ref, dl_ref, dgt_ref, dbo_ref):
        i = pl.program_id(0)
        dxv = dx_ref[...]
        _acc(dgt_ref, jnp.sum(dxv * y_ref[...].astype(F32), axis=0, keepdims=True), i)
        dy = dxv * gt_ref[...]
        _acc(dbo_ref, jnp.sum(dy, axis=0, keepdims=True), i)
        dyb = dy.astype(BF16)
        dy_ref[...] = dyb
        do = _dot_nt(dyb, wo_ref[...])
        do_ref[...] = do.astype(BF16)
        prod = do * o_ref[...].astype(F32)
        hi = prod.astype(BF16)
        lo = (prod - hi.astype(F32)).astype(BF16)
        dl_ref[...] = _dot_nt(mem_ref[...], hi) + _dot_nt(mem_ref[...], lo)

    tm = min(ROW_TILE, S)
    return _rowcall("attn_out_bwd", body, S, ROW_TILE, [dx, y, o], [gt, w_o, member],
                    [_sds((S, D), BF16), _sds((S, NO), BF16),
                     (_sds((16, S), F32), pl.BlockSpec((16, tm), lambda i: (0, i)))],
                    [_sds((1, D), F32), _sds((1, D), F32)])


def _mla_attn_bwd(q, k, v, do, lse, delta, t, gblks):
    H, S, DQ = q.shape
    DV = V_DIM
    t = min(t, S // 2)
    tk = 2 * t
    nq, nk = S // t, S // tk
    scale = QK_DIM ** -0.5
    c2 = scale * LOG2E

    ng = len(gblks)

    def body(q_ref, k_ref, v_ref, do_ref, lse_ref, dl_ref, *rest):
        g_refs, (dq_ref, dk_ref, dv_ref), recv_refs = rest[:ng], rest[ng:ng + 3], rest[ng + 3:2 * ng + 3]
        dk_s, dv_s, s_buf, dp_buf, send_sems, recv_sems, local_sems = rest[2 * ng + 3:]
        hd, kj = pl.program_id(0), pl.program_id(1)

        def scatter():
            return _exchange_copies([lambda j, r=r: r.at[j] for r in g_refs], recv_refs, send_sems, recv_sems,
                                    local_sems)

        @pl.when((hd == 0) & (kj == 0))
        def _():
            _start_exchange(scatter())

        @pl.when(kj == 0)
        def _():
            dq_ref[...] = jnp.zeros_like(dq_ref)

        dk_s[...] = jnp.zeros_like(dk_s)
        dv_s[...] = jnp.zeros_like(dv_s)

        def products(i, slot, keys=tk):
            rows = pl.ds(pl.multiple_of(i * t, t), t)
            s_buf[slot, :keys] = _dot_nt(k_ref[0, :keys], q_ref[0, rows, :])
            dp_buf[slot, :keys] = _dot_nt(v_ref[0, :keys], do_ref[rows, :])

        def absorb(i, slot, diagonal, keys=tk):
            rows = pl.ds(pl.multiple_of(i * t, t), t)
            qb, dob = q_ref[0, rows, :], do_ref[rows, :]
            p = jnp.exp2(s_buf[slot, :keys] * c2 - lse_ref[0, i])
            if diagonal is not None:
                key = lax.broadcasted_iota(jnp.int32, (keys, t), 0)
                qry = lax.broadcasted_iota(jnp.int32, (keys, t), 1) + diagonal * t
                p = jnp.where(key <= qry, p, 0.0)
            dv_s[:keys] += _dot(p.astype(BF16), dob)
            ds = (p * (dp_buf[slot, :keys] - dl_ref[0, i])).astype(BF16)
            dk_s[:keys] += _dot(ds, qb)
            dq_ref[0, rows, :] += _dot_tn(ds, k_ref[0, :keys])

        first = 2 * kj + 2
        n_off = nq - first

        def pair(i, carry):
            u = 2 * i
            products(first + u + 1, 1)
            absorb(first + u, 0, None)
            products(jnp.where(u + 2 < n_off, first + u + 2, 2 * kj + 1), 0)
            absorb(first + u + 1, 1, None)
            return carry

        products(jnp.where(n_off > 0, first, 2 * kj + 1), 0)
        lax.fori_loop(0, n_off // 2, pair, 0)
        products(2 * kj, 1, t)
        absorb(2 * kj + 1, 0, 1)
        absorb(2 * kj, 1, 0, t)

        dk_ref[0] = (dk_s[...] * scale).astype(BF16)
        dv_ref[0] = dv_s[...].astype(BF16)

        @pl.when((hd == H - 1) & (kj == nk - 1))
        def _():
            _finish_exchange(scatter())

    rowspec = pl.BlockSpec((1, nq, 1, t), lambda h, j: (h, 0, 0, 0))
    outs = pl.pallas_call(
        body, name="mla_attn_bwd", grid=(H, nk),
        in_specs=[pl.BlockSpec((1, S, DQ), lambda h, j: (h, 0, 0)),
                  pl.BlockSpec((1, tk, DQ), lambda h, j: (h, j, 0)),
                  pl.BlockSpec((1, tk, DV), lambda h, j: (h, j, 0)),
                  pl.BlockSpec((S, DV), lambda h, j: (0, h)), rowspec, rowspec] + [ANY_SPEC] * ng,
        out_specs=[pl.BlockSpec((1, S, DQ), lambda h, j: (h, 0, 0)),
                   pl.BlockSpec((1, tk, DQ), lambda h, j: (h, j, 0)),
                   pl.BlockSpec((1, tk, DV), lambda h, j: (h, j, 0))] + [ANY_SPEC] * ng,
        out_shape=[_sds((H, S, DQ), F32), _sds((H, S, DQ), BF16), _sds((H, S, DV), BF16)]
        + [_sds(g.shape, g.dtype) for g in gblks],
        scratch_shapes=[pltpu.VMEM((tk, DQ), F32), pltpu.VMEM((tk, DV), F32), pltpu.VMEM((2, tk, t), F32),
                        pltpu.VMEM((2, tk, t), F32)] + _comm_sems(ng),
        compiler_params=pltpu.CompilerParams(dimension_semantics=("arbitrary", "arbitrary"),
                                             vmem_limit_bytes=VMEM_LIMIT),
    )(q, k, v, do, lse, delta, *gblks)
    return outs[0], outs[1], outs[2], outs[3:]


def _swa_attn_bwd(q, k, v, do, lse, delta, sinks):
    S, NQ = q.shape
    NK = k.shape[1]
    W, Dh, G = WINDOW, SWA_HEAD_DIM, SWA_GROUP
    nb = S // W

    def body(q_ref, kp_ref, kc_ref, vp_ref, vc_ref, do_ref, lse_ref, dl_ref, sink_ref,
             dq_ref, dk_ref, dv_ref, dsink_ref, dkc_s, dvc_s):
        n = pl.program_id(0)

        @pl.when(n == 0)
        def _():
            dkc_s[...] = jnp.zeros_like(dkc_s)
            dvc_s[...] = jnp.zeros_like(dvc_s)
            dsink_ref[...] = jnp.zeros_like(dsink_ref)

        @pl.when(n < nb)
        def _():
            distf, valid = _swa_geometry(n)
            kband = jnp.concatenate([kp_ref[...], kc_ref[...]], axis=0)
            vband = jnp.concatenate([vp_ref[...], vc_ref[...]], axis=0)
            kband_t = kband.astype(F32).T.astype(BF16)
            dq_t = []
            for kh in range(SWA_KV_HEADS):
                ck = slice(kh * Dh, (kh + 1) * Dh)
                kb, vb, kbt = kband[:, ck], vband[:, ck], kband_t[ck, :]
                heads, q4, slopes, sinks = _swa_group(kh, q_ref, sink_ref)
                do4 = jnp.concatenate([do_ref[:, h * Dh:(h + 1) * Dh] for h in heads], axis=0)
                lse4 = jnp.concatenate([lse_ref[h:h + 1, :] for h in heads], axis=1)
                dl4 = jnp.concatenate([dl_ref[h:h + 1, :] for h in heads], axis=1)
                s = _dot_nt(kb, q4) * (Dh ** -0.5) - slopes * distf
                p = jnp.where(valid, jnp.exp(s - lse4), 0.0)
                dvb = _dot(p.astype(BF16), do4)
                dp = _dot_nt(vb, do4)
                dsb = ((p * (dp - dl4)) * (Dh ** -0.5)).astype(BF16)
                dq4 = _dot(kbt, dsb)
                dkb = _dot(dsb, q4)
                dsk4 = jnp.exp(sinks - lse4) * dl4
                for g, h in enumerate(heads):
                    dq_t.append(dq4[:, g * W:(g + 1) * W])
                    dsink_ref[:, h:h + 1] += -jnp.sum(dsk4[:, g * W:(g + 1) * W], axis=1, keepdims=True)
                dk_ref[:, ck] = (dkc_s[:, ck] + dkb[:W]).astype(BF16)
                dv_ref[:, ck] = (dvc_s[:, ck] + dvb[:W]).astype(BF16)
                dkc_s[:, ck] = dkb[W:]
                dvc_s[:, ck] = dvb[W:]
            dq_ref[...] = jnp.concatenate(dq_t, axis=0).T.astype(BF16)

        @pl.when(n == nb)
        def _():
            dk_ref[...] = dkc_s[...].astype(BF16)
            dv_ref[...] = dvc_s[...].astype(BF16)

    kprev, kcur = _swa_band_specs(W, nb, NK)
    qspec = lambda cols: pl.BlockSpec((W, cols), lambda n: (jnp.minimum(n, nb - 1), 0))
    kvout = pl.BlockSpec((W, NK), lambda n: (jnp.maximum(n - 1, 0), 0))
    rowspec = pl.BlockSpec((SWA_HEADS, W), lambda n: (0, jnp.minimum(n, nb - 1)))
    return pl.pallas_call(
        body, name="swa_attn_bwd", grid=(nb + 1,),
        in_specs=[qspec(NQ), kprev, kcur, kprev, kcur, qspec(NQ), rowspec, rowspec,
                  pl.BlockSpec((1, SWA_HEADS), lambda n: (0, 0))],
        out_specs=[qspec(NQ), kvout, kvout, pl.BlockSpec((1, 128), lambda n: (0, 0))],
        out_shape=[_sds((S, NQ), BF16), _sds((S, NK), BF16), _sds((S, NK), BF16), _sds((1, 128), F32)],
        scratch_shapes=[pltpu.VMEM((W, NK), F32), pltpu.VMEM((W, NK), F32)],
        compiler_params=pltpu.CompilerParams(dimension_semantics=("arbitrary",), vmem_limit_bytes=VMEM_LIMIT),
    )(q, k, k, v, v, do, lse, delta, sinks)


def _swa_in_bwd(dq, dk, dv, x, dx, w_qkv, g, sc):
    S, D = x.shape
    N = w_qkv.shape[1]

    def body(dq_ref, dk_ref, dv_ref, x_ref, dx_ref, w_ref, g_ref, sc_ref, dqkv_ref, dxo_ref, db_ref, dsh_ref, da_ref):
        i = pl.program_id(0)
        dqkv = jnp.concatenate([dq_ref[...], dk_ref[...], dv_ref[...]], axis=1)
        dqkv_ref[...] = dqkv
        _acc(db_ref, jnp.sum(dqkv.astype(F32), axis=0, keepdims=True), i)
        dh = _dot_nt(dqkv, w_ref[...])
        dxn, dsh, da = _modulate_bwd(dh, x_ref[...], g_ref[...], sc_ref[...])
        dxo_ref[...] = dx_ref[...] + dxn
        _acc(dsh_ref, dsh, i)
        _acc(da_ref, da, i)

    return _rowcall("swa_in_bwd", body, S, ROW_TILE, [dq, dk, dv, x, dx], [w_qkv, g, sc],
                    [_sds((S, N), BF16), _sds((S, D), F32)],
                    [_sds((1, N), F32), _sds((1, D), F32), _sds((1, D), F32)])


def _mla_in_bwd(dq, dk, dv, cos, sin, cqp, ckvp, x, dx, w_uqx, g_q, w_ukv, g_kv, w_cat, g, sc):
    S, D = x.shape
    H = MLA_HEADS
    QL = g_q.shape[1]
    NX = w_uqx.shape[1]
    NC = w_cat.shape[1]

    def body(dq_ref, dk_ref, dv_ref, cos_ref, sin_ref, cqp_ref, ckvp_ref, x_ref, dx_ref,
             wuqx_ref, gq_ref, wukv_ref, gkv_ref, wcat_ref, g_ref, sc_ref,
             dqx_ref, dkv_ref, dcat_ref, dxo_ref, dgq_ref, dgkv_ref, dsh_ref, da_ref):
        i = pl.program_id(0)
        cs, sn = cos_ref[...], sin_ref[...]
        dkr = jnp.zeros(cs.shape, F32)
        for hd in range(H):
            b = hd * 256
            dqh = dq_ref[hd] * (QK_DIM ** -0.5)
            dqx_ref[:, b:b + QK_NOPE] = dqh[:, :QK_NOPE].astype(BF16)
            dqx_ref[:, b + 128:b + 192] = (dqh[:, QK_NOPE:] * cs).astype(BF16)
            dqx_ref[:, b + 192:b + 256] = (dqh[:, QK_NOPE:] * sn).astype(BF16)
            dkh = dk_ref[hd]
            dkv_ref[:, b:b + QK_NOPE] = dkh[:, :QK_NOPE]
            dkv_ref[:, b + 128:b + 256] = dv_ref[hd]
            dkr = dkr + dkh[:, QK_NOPE:].astype(F32)
        dcq = _dot_nt(dqx_ref[...], wuqx_ref[...])
        cqp = cqp_ref[...]
        rq = _rstd(cqp)
        nq = cqp * rq
        _acc(dgq_ref, jnp.sum(dcq * nq, axis=0, keepdims=True), i)
        dcqp = _rms_bwd(dcq * gq_ref[...], nq, rq)
        dckv = _dot_nt(dkv_ref[...], wukv_ref[...])
        ckvp = ckvp_ref[...]
        rk = _rstd(ckvp)
        nk = ckvp * rk
        _acc(dgkv_ref, jnp.sum(dckv * nk, axis=0, keepdims=True), i)
        dckvp = _rms_bwd(dckv * gkv_ref[...], nk, rk)
        dcat_ref[:, :QL] = dcqp.astype(BF16)
        dcat_ref[:, QL:QL + KV_LORA] = dckvp.astype(BF16)
        o = QL + KV_LORA
        dcat_ref[:, o:o + QK_ROPE] = (dkr * cs).astype(BF16)
        dcat_ref[:, o + QK_ROPE:o + 2 * QK_ROPE] = (dkr * sn).astype(BF16)
        dh = _dot_nt(dcat_ref[...], wcat_ref[...])
        dxn, dsh, da = _modulate_bwd(dh, x_ref[...], g_ref[...], sc_ref[...])
        dxo_ref[...] = dx_ref[...] + dxn
        _acc(dsh_ref, dsh, i)
        _acc(da_ref, da, i)

    return _rowcall("mla_in_bwd", body, S, ROW_TILE_WIDE, [dq, dk, dv, cos, sin, cqp, ckvp, x, dx],
                    [w_uqx, g_q, w_ukv, g_kv, w_cat, g, sc],
                    [_sds((S, NX), BF16), _sds((S, NX), BF16), _sds((S, NC), BF16), _sds((S, D), F32)],
                    [_sds((1, QL), F32), _sds((1, KV_LORA), F32), _sds((1, D), F32), _sds((1, D), F32)])


def _matmul_tn(name, a, b, out_dtype=F32, column_blocks=False):
    S, K = a.shape
    N = b.shape[1]
    tk, tn, ts = min(K, 1024), min(N, 1024), min(S, TN_TOKENS)
    if column_blocks:
        tn = N // N_DEV
    if N % tn:
        tn = 512 if N % 512 == 0 else (384 if N % 384 == 0 else 128)
    if K % tk:
        tk = 512 if K % 512 == 0 else (384 if K % 384 == 0 else 128)
    ns = S // ts

    def body(a_ref, b_ref, o_ref, *scratch):
        acc_ref = scratch[0] if scratch else o_ref
        _acc(acc_ref, _dot_tn(a_ref[...], b_ref[...]), pl.program_id(2))
        if scratch:
            @pl.when(pl.program_id(2) == ns - 1)
            def _():
                o_ref[...] = acc_ref[...].astype(out_dtype)

    if column_blocks:
        out_spec = pl.BlockSpec((None, tk, tn), lambda i, j, s: (j, i, 0))
        out_shape = _sds((N_DEV, K, tn), out_dtype)
    else:
        out_spec = pl.BlockSpec((tk, tn), lambda i, j, s: (i, j))
        out_shape = _sds((K, N), out_dtype)
    return pl.pallas_call(
        body, name=name, grid=(K // tk, N // tn, ns),
        in_specs=[pl.BlockSpec((ts, tk), lambda i, j, s: (s, i)), pl.BlockSpec((ts, tn), lambda i, j, s: (s, j))],
        out_specs=out_spec, out_shape=out_shape,
        scratch_shapes=[] if out_dtype == F32 else [pltpu.VMEM((tk, tn), F32)],
        compiler_params=pltpu.CompilerParams(dimension_semantics=("parallel", "parallel", "arbitrary"),
                                             vmem_limit_bytes=VMEM_LIMIT),
    )(a, b)


def _silu(c):
    return c * jax.nn.sigmoid(c)


def _ada_fwd(c_all, w_ada):
    L, D, NC = w_ada.shape

    def body(c_ref, w_ref, o_ref):
        cond = _silu(c_ref[...]).astype(BF16)
        o_ref[0] = _dot(cond, w_ref[0].astype(BF16))

    return pl.pallas_call(
        body, name="ada_fwd", grid=(L,),
        in_specs=[pl.BlockSpec(c_all.shape, lambda l: (0, 0)), pl.BlockSpec((1, D, NC), lambda l: (l, 0, 0))],
        out_specs=pl.BlockSpec((1, N_DEV, NC), lambda l: (l, 0, 0)),
        out_shape=_sds((L, N_DEV, NC), F32),
        compiler_params=pltpu.CompilerParams(dimension_semantics=("arbitrary",), vmem_limit_bytes=VMEM_LIMIT),
    )(c_all, w_ada)


def _adamw(w, g, m, v):
    m = ADAM_B1 * m + (1.0 - ADAM_B1) * g
    v = ADAM_B2 * v + (1.0 - ADAM_B2) * (g * g)
    m_hat = m / (1.0 - ADAM_B1 ** ADAM_STEP)
    v_hat = v / (1.0 - ADAM_B2 ** ADAM_STEP)
    delta = -ADAM_LR * (m_hat / (jnp.sqrt(v_hat) + ADAM_EPS) + ADAM_WD * w)
    return delta, m, v


def _ada_bwd_adamw(c_all_t, dmod_cols, w, m, v):
    L, D, NC = w.shape
    tr = min(D, 256)

    def body(ct_ref, dm_ref, w_ref, m_ref, v_ref, g_ref, d_ref, mo_ref, vo_ref):
        cond_t = _silu(ct_ref[...])
        dm = dm_ref[0]
        g = cond_t[:, 0:1] * dm[0:1, :]
        for b in range(1, N_DEV):
            g = g + cond_t[:, b:b + 1] * dm[b:b + 1, :]
        g_ref[0] = g
        d_ref[0], mo_ref[0], vo_ref[0] = _adamw(w_ref[0], g, m_ref[0], v_ref[0])

    wspec = pl.BlockSpec((1, tr, NC), lambda l, r: (l, r, 0))
    return pl.pallas_call(
        body, name="ada_bwd_adamw", grid=(L, D // tr),
        in_specs=[pl.BlockSpec((tr, N_DEV), lambda l, r: (r, 0)),
                  pl.BlockSpec((1, N_DEV, NC), lambda l, r: (l, 0, 0)), wspec, wspec, wspec],
        out_specs=[wspec] * 4, out_shape=[_sds(w.shape, F32)] * 4,
        compiler_params=pltpu.CompilerParams(dimension_semantics=("parallel", "parallel"), vmem_limit_bytes=VMEM_LIMIT),
    )(c_all_t, dmod_cols, w, m, v)


def _sum_devices(x):
    def body(x_ref, o_ref):
        s = x_ref[0]
        for j in range(1, N_DEV):
            s = s + x_ref[j]
        o_ref[...] = s

    return pl.pallas_call(body, name="sum_devices", out_shape=_sds(x.shape[1:], F32))(x)


def _adamw_small(w, g, m, v):
    def body(w_ref, g_ref, m_ref, v_ref, d_ref, mo_ref, vo_ref):
        d_ref[...], mo_ref[...], vo_ref[...] = _adamw(w_ref[...], g_ref[...], m_ref[...], v_ref[...])

    return pl.pallas_call(body, name="adamw_small", out_shape=[_sds(w.shape, F32)] * 3)(w, g, m, v)


def _me():
    return lax.axis_index("x") * 4 + lax.axis_index("y") * 2 + lax.axis_index("c")


def _peer(k):
    x, y, c = lax.axis_index("x"), lax.axis_index("y"), lax.axis_index("c")
    px = 1 - x if k & 4 else x
    py = 1 - y if k & 2 else y
    pc = 1 - c if k & 1 else c
    return (px, py, pc), px * 4 + py * 2 + pc


VMEM_SPEC = pl.BlockSpec(memory_space=pltpu.VMEM)
ANY_SPEC = pl.BlockSpec(memory_space=pl.ANY)
def _comm_sems(n):
    return [pltpu.SemaphoreType.DMA((n * (N_DEV - 1),)), pltpu.SemaphoreType.DMA((n * (N_DEV - 1),)),
            pltpu.SemaphoreType.DMA((n,))]


def _exchange_copies(srcs_of, dst_refs, send_sems, recv_sems, local_sems):
    me = _me()
    local, sends, recvs = [], [], []
    for a, (src_of, dst_ref) in enumerate(zip(srcs_of, dst_refs)):
        local.append(pltpu.make_async_copy(src_of(me), dst_ref.at[me], local_sems.at[a]))
        for k in range(1, N_DEV):
            dev, pj = _peer(k)
            i = a * (N_DEV - 1) + k - 1
            sems = dict(send_sem=send_sems.at[i], recv_sem=recv_sems.at[i], device_id=dev, device_id_type=MESH_IDS)
            sends.append(pltpu.make_async_remote_copy(src_ref=src_of(pj), dst_ref=dst_ref.at[me], **sems))
            recvs.append(pltpu.make_async_remote_copy(src_ref=src_of(pj), dst_ref=dst_ref.at[pj], **sems))
    return local, sends, recvs


def _start_exchange(copies):
    local, sends, _ = copies
    for cp in local + sends:
        cp.start()


def _finish_exchange(copies):
    local, sends, recvs = copies
    for cp in recvs:
        cp.wait_recv()
    for cp in sends:
        cp.wait_send()
    for cp in local:
        cp.wait()


def _sum_adamw(recv, w, m, v):
    shape = w.shape
    C = shape[-1]
    R = w.size // C
    rows = max(d for d in range(16, min(R, 512) + 1, 16) if R % d == 0 and d * C <= 256 * 1024)

    def body(r_ref, w_ref, m_ref, v_ref, go_ref, d_ref, mo_ref, vo_ref):
        g = r_ref[0].astype(F32)
        for j in range(1, N_DEV):
            g = g + r_ref[j].astype(F32)
        go_ref[...] = g
        d_ref[...], mo_ref[...], vo_ref[...] = _adamw(w_ref[...], g, m_ref[...], v_ref[...])

    spec = pl.BlockSpec((rows, C), lambda i: (i, 0))
    outs = pl.pallas_call(
        body, name="sum_adamw", grid=(R // rows,),
        in_specs=[pl.BlockSpec((N_DEV, rows, C), lambda i: (0, i, 0)), spec, spec, spec],
        out_specs=[spec] * 4, out_shape=[_sds((R, C), F32)] * 4,
        compiler_params=pltpu.CompilerParams(dimension_semantics=("parallel",), vmem_limit_bytes=VMEM_LIMIT),
    )(recv.reshape(N_DEV, R, C), w.reshape(R, C), m.reshape(R, C), v.reshape(R, C))
    return [o.reshape(shape) for o in outs]


def _all_gather(name, x, out_dtype):
    R, C = x.shape
    cast = out_dtype != x.dtype

    def body(x_ref, out_ref, buf, send_sems, recv_sems, local_sem):
        me = _me()
        if cast:
            buf[...] = x_ref[...].astype(out_dtype)
            src = buf
        else:
            src = x_ref
        local = pltpu.make_async_copy(src, out_ref.at[me], local_sem)
        local.start()
        sends = []
        for k in range(1, N_DEV):
            dev, _ = _peer(k)
            cp = pltpu.make_async_remote_copy(src_ref=src, dst_ref=out_ref.at[me], send_sem=send_sems.at[k - 1],
                                              recv_sem=recv_sems.at[k - 1], device_id=dev, device_id_type=MESH_IDS)
            cp.start()
            sends.append(cp)
        for k in range(1, N_DEV):
            dev, pj = _peer(k)
            pltpu.make_async_remote_copy(src_ref=src, dst_ref=out_ref.at[pj], send_sem=send_sems.at[k - 1],
                                         recv_sem=recv_sems.at[k - 1], device_id=dev, device_id_type=MESH_IDS).wait_recv()
        for cp in sends:
            cp.wait_send()
        local.wait()

    return pl.pallas_call(
        body, name=name, in_specs=[VMEM_SPEC], out_specs=ANY_SPEC, out_shape=_sds((N_DEV, R, C), out_dtype),
        scratch_shapes=[pltpu.VMEM((R, C) if cast else (8, 128), out_dtype),
                        pltpu.SemaphoreType.DMA((N_DEV - 1,)), pltpu.SemaphoreType.DMA((N_DEV - 1,)),
                        pltpu.SemaphoreType.DMA(())],
        compiler_params=pltpu.CompilerParams(vmem_limit_bytes=VMEM_LIMIT),
    )(x)


def _all_to_all(name, x):
    _, R, C = x.shape

    def body(x_ref, out_ref, send_sems, recv_sems, local_sem):
        me = _me()
        local = pltpu.make_async_copy(x_ref.at[me], out_ref.at[me], local_sem)
        local.start()
        sends = []
        for k in range(1, N_DEV):
            dev, pj = _peer(k)
            cp = pltpu.make_async_remote_copy(src_ref=x_ref.at[pj], dst_ref=out_ref.at[me], send_sem=send_sems.at[k - 1],
                                              recv_sem=recv_sems.at[k - 1], device_id=dev, device_id_type=MESH_IDS)
            cp.start()
            sends.append(cp)
        for k in range(1, N_DEV):
            dev, pj = _peer(k)
            pltpu.make_async_remote_copy(src_ref=x_ref.at[pj], dst_ref=out_ref.at[pj], send_sem=send_sems.at[k - 1],
                                         recv_sem=recv_sems.at[k - 1], device_id=dev, device_id_type=MESH_IDS).wait_recv()
        for cp in sends:
            cp.wait_send()
        local.wait()

    return pl.pallas_call(
        body, name=name, in_specs=[VMEM_SPEC], out_specs=VMEM_SPEC, out_shape=_sds(x.shape, x.dtype),
        scratch_shapes=[pltpu.SemaphoreType.DMA((N_DEV - 1,)), pltpu.SemaphoreType.DMA((N_DEV - 1,)),
                        pltpu.SemaphoreType.DMA(())],
    )(x)


def _reduce_scatter_adamw(name, gblk, w, m, v):
    _, R, C = gblk.shape
    rows = 8
    for cand in (136, 128, 80, 64, 40, 32, 16, 8):
        if R % cand == 0:
            rows = cand
            break

    def body(g_ref, w_ref, m_ref, v_ref, go_ref, d_ref, mo_ref, vo_ref, recv, send_sems, recv_sems, local_sem):
        me = _me()
        local = pltpu.make_async_copy(g_ref.at[me], recv.at[me], local_sem)
        local.start()
        sends = []
        for k in range(1, N_DEV):
            dev, pj = _peer(k)
            cp = pltpu.make_async_remote_copy(src_ref=g_ref.at[pj], dst_ref=recv.at[me], send_sem=send_sems.at[k - 1],
                                              recv_sem=recv_sems.at[k - 1], device_id=dev, device_id_type=MESH_IDS)
            cp.start()
            sends.append(cp)
        for k in range(1, N_DEV):
            dev, pj = _peer(k)
            pltpu.make_async_remote_copy(src_ref=g_ref.at[pj], dst_ref=recv.at[pj], send_sem=send_sems.at[k - 1],
                                         recv_sem=recv_sems.at[k - 1], device_id=dev, device_id_type=MESH_IDS).wait_recv()
        local.wait()

        def chunk(i, carry):
            r = pl.ds(pl.multiple_of(i * rows, rows), rows)
            g = recv[0, r, :].astype(F32)
            for j in range(1, N_DEV):
                g = g + recv[j, r, :].astype(F32)
            go_ref[r, :] = g
            d_ref[r, :], mo_ref[r, :], vo_ref[r, :] = _adamw(w_ref[r, :], g, m_ref[r, :], v_ref[r, :])
            return carry

        lax.fori_loop(0, R // rows, chunk, 0)
        for cp in sends:
            cp.wait_send()

    return pl.pallas_call(
        body, name=name, in_specs=[ANY_SPEC, VMEM_SPEC, VMEM_SPEC, VMEM_SPEC], out_specs=[VMEM_SPEC] * 4,
        out_shape=[_sds((R, C), F32)] * 4,
        scratch_shapes=[pltpu.VMEM((N_DEV, R, C), BF16), pltpu.SemaphoreType.DMA((N_DEV - 1,)),
                        pltpu.SemaphoreType.DMA((N_DEV - 1,)), pltpu.SemaphoreType.DMA(())],
        compiler_params=pltpu.CompilerParams(vmem_limit_bytes=VMEM_LIMIT),
    )(gblk, w, m, v)


FIRST_WEIGHTS = ["mla_w_dq", "mla_w_uq", "mla_w_dkv", "mla_w_ukv"]
LATE_WEIGHTS = ["mla_w_o", "swa_w_qkv", "swa_w_o", "w_ff1", "w_ff2"]
ROW_SHARDED = {"mla_w_dq", "mla_w_dkv", "mla_w_o", "swa_w_o", "w_ff2"}


def _unblock(name, blocks):
    sh = blocks.shape[1:]
    if name in ROW_SHARDED:
        return jnp.moveaxis(blocks, 0, 1).reshape(sh[0], N_DEV * sh[1], sh[2])
    return jnp.moveaxis(blocks, 0, 2).reshape(sh[0], sh[1], N_DEV * sh[2])


def _block(name, full):
    L, K, N = full.shape
    if name in ROW_SHARDED:
        return jnp.moveaxis(full.reshape(L, N_DEV, K // N_DEV, N), 1, 0)
    return jnp.moveaxis(full.reshape(L, K, N_DEV, N // N_DEV), 2, 0)


def _rot_cols(w):
    half = QK_ROPE // 2
    return jnp.concatenate([-w[..., half:], w[..., :half]], axis=-1)


def _unrot_cols(gw):
    half = QK_ROPE // 2
    return jnp.concatenate([gw[..., half:], -gw[..., :half]], axis=-1)


def _row(v):
    return v.reshape(1, -1)


def _mlp_block_bwd(dx, sv, w1, w2, g, sc, gt):
    dy, du, dgt = _mlp_bwd_a(dx, sv["y2"], sv["rl"], gt, w2)
    dw2 = _matmul_tn("dw_ff2", sv["act"], dy, BF16)
    dw1 = _matmul_tn("dw_ff1", sv["h2"], du, BF16, column_blocks=True)
    dxo, dsh, da = _mlp_bwd_b(du, sv["x1"], dx, w1, g, sc)
    return dxo, dw1, dw2, dsh, da, dgt


def kernel(x, c, positions, w_ada, b_ada, g_mix, g_mlp, mla_w_dq, mla_g_q, mla_w_uq, mla_w_dkv, mla_g_kv, mla_w_ukv, mla_w_o, swa_w_qkv, swa_b_qkv, swa_sinks, swa_w_o, swa_b_o, w_ff1, w_ff2, g_final, loss_target, m_w_ada, m_b_ada, m_g_mix, m_g_mlp, m_mla_w_dq, m_mla_g_q, m_mla_w_uq, m_mla_w_dkv, m_mla_g_kv, m_mla_w_ukv, m_mla_w_o, m_swa_w_qkv, m_swa_b_qkv, m_swa_sinks, m_swa_w_o, m_swa_b_o, m_w_ff1, m_w_ff2, m_g_final, v_w_ada, v_b_ada, v_g_mix, v_g_mlp, v_mla_w_dq, v_mla_g_q, v_mla_w_uq, v_mla_w_dkv, v_mla_g_kv, v_mla_w_ukv, v_mla_w_o, v_swa_w_qkv, v_swa_b_qkv, v_swa_sinks, v_swa_w_o, v_swa_b_o, v_w_ff1, v_w_ff2, v_g_final):
    S, D = x.shape[1], x.shape[2]
    me = _me()
    x0 = x[0]
    target = loss_target[0]
    big_w = dict(mla_w_dq=mla_w_dq, mla_w_uq=mla_w_uq, mla_w_dkv=mla_w_dkv, mla_w_ukv=mla_w_ukv, mla_w_o=mla_w_o,
                 swa_w_qkv=swa_w_qkv, swa_w_o=swa_w_o, w_ff1=w_ff1, w_ff2=w_ff2)
    big_m = dict(mla_w_dq=m_mla_w_dq, mla_w_uq=m_mla_w_uq, mla_w_dkv=m_mla_w_dkv, mla_w_ukv=m_mla_w_ukv,
                 mla_w_o=m_mla_w_o, swa_w_qkv=m_swa_w_qkv, swa_w_o=m_swa_w_o, w_ff1=m_w_ff1, w_ff2=m_w_ff2)
    big_v = dict(mla_w_dq=v_mla_w_dq, mla_w_uq=v_mla_w_uq, mla_w_dkv=v_mla_w_dkv, mla_w_ukv=v_mla_w_ukv,
                 mla_w_o=v_mla_w_o, swa_w_qkv=v_swa_w_qkv, swa_w_o=v_swa_w_o, w_ff1=v_w_ff1, w_ff2=v_w_ff2)
    groups = {"first": FIRST_WEIGHTS}
    wrows = {n: -(-big_w[n].size // (PACK_COLS * 16)) * 16 for n in FIRST_WEIGHTS}
    offs = {g: np.concatenate([[0], np.cumsum([wrows[n] for n in names])]).astype(int) for g, names in groups.items()}

    def as_rows(n, a, lead=()):
        flat = a.reshape(lead + (-1,))
        pad = wrows[n] * PACK_COLS - flat.shape[-1]
        if pad:
            flat = jnp.pad(flat, ((0, 0),) * len(lead) + ((0, pad),))
        return flat.reshape(lead + (wrows[n], PACK_COLS))

    def pack(g, d):
        return jnp.concatenate([as_rows(n, d[n]) for n in groups[g]], axis=0)

    def pack_blocks(g, gfull):
        return jnp.concatenate([as_rows(n, _block(n, gfull[n]).astype(BF16), (N_DEV,)) for n in groups[g]], axis=1)

    def unpack(g, packed, lead=()):
        out = {}
        for i, n in enumerate(groups[g]):
            part = packed[..., int(offs[g][i]):int(offs[g][i + 1]), :].reshape(lead + (-1,))
            out[n] = part[..., :big_w[n].size].reshape(lead + big_w[n].shape)
        return out

    gathered = _all_gather("gather_weights", pack("first", big_w), BF16)
    wfull = {n: _unblock(n, b) for n, b in unpack("first", gathered, (N_DEV,)).items()}
    w_dq, w_dkv = wfull["mla_w_dq"][0], wfull["mla_w_dkv"][0]
    w_cat = jnp.concatenate([w_dq, w_dkv, _rot_cols(w_dkv[:, KV_LORA:])], axis=1)
    QL = w_dq.shape[1]
    w_uq = wfull["mla_w_uq"][0].reshape(QL, MLA_HEADS, QK_DIM)
    w_uqx = jnp.concatenate([w_uq, _rot_cols(w_uq[..., QK_NOPE:])], axis=-1).reshape(QL, MLA_HEADS * 256)
    w_ukv = wfull["mla_w_ukv"][0]

    L = w_ada.shape[0]
    NC = w_ada.shape[2]
    nbq, nbo = swa_b_qkv.shape[1], swa_b_o.shape[1]
    cpad = -(-(D + nbq + nbo) // 1024) * 1024
    cpack = jnp.pad(jnp.concatenate([c[0], swa_b_qkv[0], swa_b_o[0]]), (0, cpad - (D + nbq + nbo))).reshape(8, cpad // 8)
    call = _all_gather("gather_c", cpack, F32).reshape(N_DEV, cpad)
    c_all = call[:, :D]
    b_qkv_full = call[:, D:D + nbq].reshape(1, N_DEV * nbq)
    b_o_full = call[:, D + nbq:D + nbq + nbo].reshape(1, N_DEV * nbo)
    mod_cols = _ada_fwd(c_all, w_ada)
    mpad = -(-(L * NC) // 1024) * 1024
    mod_send = jnp.pad(jnp.moveaxis(mod_cols, 1, 0).reshape(N_DEV, L * NC), ((0, 0), (0, mpad - L * NC)))
    mod_mine = _all_to_all("exchange_mod", mod_send.reshape(N_DEV, 8, mpad // 8)).reshape(N_DEV, mpad)[:, :L * NC]
    mod = jnp.moveaxis(mod_mine.reshape(N_DEV, L, NC), 0, 1).reshape(L, N_DEV * NC) + b_ada
    mods = mod.reshape(L, 6, 1, D)

    half = QK_ROPE // 2
    inv_freq = ROPE_THETA ** (-jnp.arange(half, dtype=F32) / half)
    ang = positions[0].astype(F32)[:, None] * inv_freq
    cos = jnp.concatenate([jnp.cos(ang), jnp.cos(ang)], axis=-1)
    sin = jnp.concatenate([jnp.sin(ang), jnp.sin(ang)], axis=-1)

    T_ATT = ATT_TILE
    zero_bias = jnp.zeros((1, D), F32)

    sh1, sc1, gt1, sh2, sc2, gt2 = [mods[0, i] for i in range(6)]
    gm0, gp0 = _row(g_mix[0]), _row(g_mlp[0])
    h1, cqp, cq, ckvp, ckv, q, k, v, vt = _mla_in_fwd(x0, cos, sin, gm0, sc1, sh1, w_cat, mla_g_q, w_uqx, mla_g_kv,
                                                      w_ukv, ROW_TILE)
    o0, lse0, gathered = _mla_attn_fwd(q, k, vt, ATT_TILE_FWD, [big_w[n].astype(BF16) for n in LATE_WEIGHTS])
    wfull = {n: _unblock(n, b) for n, b in zip(LATE_WEIGHTS, gathered)}
    w_o_mla, w_qkv, w_o_swa = wfull["mla_w_o"][0], wfull["swa_w_qkv"][0], wfull["swa_w_o"][0]
    ff1, ff2 = wfull["w_ff1"], wfull["w_ff2"]
    y1, x1, h2 = _attn_out_fwd(o0, x0, w_o_mla, zero_bias, gt1, gp0, sc2, sh2)
    rl0, act0, y2, x2 = _mlp_fwd(h2, x1, ff1[0], ff2[0], gt2)
    sv0 = dict(y2=y2, rl=rl0, act=act0, h2=h2, x1=x1)

    th1, tc1, tg1, th2, tc2, tg2 = [mods[1, i] for i in range(6)]
    gm1, gp1 = _row(g_mix[1]), _row(g_mlp[1])
    h3, sq, sk, svv = _swa_in_fwd(x2, gm1, tc1, th1, w_qkv, b_qkv_full)
    o1, lse1 = _swa_attn_fwd(sq, sk, svv, swa_sinks)
    y3, x3, h4 = _attn_out_fwd(o1, x2, w_o_swa, b_o_full, tg1, gp1, tc2, th2)
    rl1, act1, y4, dx4, loss_part, dg_final = _mlp_fwd_loss(h4, x3, ff1[1], ff2[1], tg2, target, _row(g_final))
    sv1 = dict(y2=y4, rl=rl1, act=act1, h2=h4, x1=x3)

    dx3, dw1_1, dw2_1, dsh2_1, da2_1, dgt2_1 = _mlp_block_bwd(dx4, sv1, ff1[1], ff2[1], gp1, tc2, tg2)
    dy, do, dl, dgt1_1, db_o = _attn_out_bwd(dx3, y3, o1, tg1, w_o_swa, SWA_HEADS)
    dw_o_swa = _matmul_tn("dw_o", o1, dy, BF16)
    dsq, dsk, dsv, dsink = _swa_attn_bwd(sq, sk, svv, do, lse1, dl, swa_sinks)
    dqkv, dx2, db_qkv, dsh1_1, da1_1 = _swa_in_bwd(dsq, dsk, dsv, x2, dx3, w_qkv, gm1, tc1)
    dw_qkv = _matmul_tn("dw_qkv", h3, dqkv, BF16)

    dx1, dw1_0, dw2_0, dsh2_0, da2_0, dgt2_0 = _mlp_block_bwd(dx2, sv0, ff1[0], ff2[0], gp0, sc2, gt2)
    dy, do, dl, dgt1_0, _ = _attn_out_bwd(dx1, y1, o0, gt1, w_o_mla, MLA_HEADS)
    dw_o_mla = _matmul_tn("dw_o", o0, dy, BF16)
    tb = min(T_ATT, S)
    delta = dl[:MLA_HEADS].reshape(MLA_HEADS, S // tb, 1, tb)
    glate = dict(mla_w_o=dw_o_mla[None], swa_w_qkv=dw_qkv[None], swa_w_o=dw_o_swa[None],
                 w_ff2=jnp.stack([dw2_0, dw2_1]))
    gblocks = {n: _block(n, g).astype(BF16) for n, g in glate.items()}
    gblocks["w_ff1"] = jnp.stack([dw1_0, dw1_1], axis=1)
    lse_rows = (lse0 * LOG2E).reshape(MLA_HEADS, S // tb, 1, tb)
    dq, dk, dv, recv = _mla_attn_bwd(q, k, v, do, lse_rows, delta, T_ATT, [gblocks[n] for n in LATE_WEIGHTS])
    late = {n: _sum_adamw(r, big_w[n], big_m[n], big_v[n]) for n, r in zip(LATE_WEIGHTS, recv)}
    dqx, dkv, dcat, dx0, dg_q, dg_kv, dsh1_0, da1_0 = _mla_in_bwd(
        dq, dk, dv, cos, sin, cqp, ckvp, x0, dx1, w_uqx, mla_g_q, w_ukv, mla_g_kv, w_cat, gm0, sc1)
    dw_uqx = _matmul_tn("dw_uq", cq, dqx).reshape(QL, MLA_HEADS, 256)
    dw_ukv = _matmul_tn("dw_ukv", ckv, dkv)
    dw_cat = _matmul_tn("dw_down", h1, dcat)
    dw_uq = jnp.concatenate([dw_uqx[..., :QK_NOPE], dw_uqx[..., 128:192] + _unrot_cols(dw_uqx[..., 192:256])],
                            axis=-1).reshape(QL, MLA_HEADS * QK_DIM)
    o_kr = QL + KV_LORA
    dw_dkv = jnp.concatenate([dw_cat[:, QL:o_kr],
                              dw_cat[:, o_kr:o_kr + QK_ROPE] + _unrot_cols(dw_cat[:, o_kr + QK_ROPE:])], axis=1)

    gfirst = dict(mla_w_dq=dw_cat[None, :, :QL], mla_w_uq=dw_uq[None], mla_w_dkv=dw_dkv[None], mla_w_ukv=dw_ukv[None])
    first = _reduce_scatter_adamw("grad_exchange_adamw", pack_blocks("first", gfirst), pack("first", big_w),
                                  pack("first", big_m), pack("first", big_v))
    big_g, big_d, big_nm, big_nv = ({**unpack("first", first[j]), **{n: late[n][j] for n in LATE_WEIGHTS}}
                                    for j in range(4))

    dmod = jnp.stack([
        jnp.concatenate([dsh1_0, gm0 * da1_0, dgt1_0, dsh2_0, gp0 * da2_0, dgt2_0], axis=1),
        jnp.concatenate([dsh1_1, gm1 * da1_1, dgt1_1, dsh2_1, gp1 * da2_1, dgt2_1], axis=1)]).reshape(-1)
    dg_mix = jnp.concatenate([(1.0 + sc1) * da1_0, (1.0 + tc1) * da1_1], axis=1).reshape(-1)
    dg_mlp = jnp.concatenate([(1.0 + sc2) * da2_0, (1.0 + tc2) * da2_1], axis=1).reshape(-1)
    parts = [loss_part.reshape(-1), dmod, dg_mix, dg_mlp, dg_q.reshape(-1), dg_kv.reshape(-1), dsink.reshape(-1),
             dg_final.reshape(-1), db_qkv.reshape(-1), db_o.reshape(-1)]
    soffs = np.concatenate([[0], np.cumsum([p.size for p in parts])])
    spad = -(-int(soffs[-1]) // 1024) * 1024
    spack = jnp.pad(jnp.concatenate(parts), (0, spad - int(soffs[-1]))).reshape(8, spad // 8)
    sall = _all_gather("gather_small_grads", spack, F32)
    ssum = _sum_devices(sall).reshape(-1)
    tot = [ssum[int(soffs[i]):int(soffs[i + 1])] for i in range(len(parts))]
    loss = tot[0][0]
    nsink = swa_sinks.shape[1]
    small_g = dict(b_ada=tot[1].reshape(b_ada.shape), g_mix=tot[2].reshape(g_mix.shape), g_mlp=tot[3].reshape(g_mlp.shape),
                   mla_g_q=tot[4].reshape(mla_g_q.shape), mla_g_kv=tot[5].reshape(mla_g_kv.shape),
                   swa_sinks=tot[6][:nsink].reshape(swa_sinks.shape), g_final=tot[7].reshape(g_final.shape),
                   swa_b_qkv=lax.dynamic_slice(tot[8], (me * nbq,), (nbq,)).reshape(swa_b_qkv.shape),
                   swa_b_o=lax.dynamic_slice(tot[9], (me * nbo,), (nbo,)).reshape(swa_b_o.shape))
    small_w = dict(b_ada=b_ada, g_mix=g_mix, g_mlp=g_mlp, mla_g_q=mla_g_q, mla_g_kv=mla_g_kv, swa_sinks=swa_sinks,
                   g_final=g_final, swa_b_qkv=swa_b_qkv, swa_b_o=swa_b_o)
    small_m = dict(b_ada=m_b_ada, g_mix=m_g_mix, g_mlp=m_g_mlp, mla_g_q=m_mla_g_q, mla_g_kv=m_mla_g_kv,
                   swa_sinks=m_swa_sinks, g_final=m_g_final, swa_b_qkv=m_swa_b_qkv, swa_b_o=m_swa_b_o)
    small_v = dict(b_ada=v_b_ada, g_mix=v_g_mix, g_mlp=v_g_mlp, mla_g_q=v_mla_g_q, mla_g_kv=v_mla_g_kv,
                   swa_sinks=v_swa_sinks, g_final=v_g_final, swa_b_qkv=v_swa_b_qkv, swa_b_o=v_swa_b_o)
    SMALL = list(small_w)
    woffs = np.concatenate([[0], np.cumsum([small_w[n].size for n in SMALL])])
    wpad = -(-int(woffs[-1]) // 1024) * 1024

    def spack_of(d):
        flat = jnp.concatenate([d[n].reshape(-1) for n in SMALL])
        return jnp.pad(flat, (0, wpad - int(woffs[-1]))).reshape(8, wpad // 8)

    sm = _adamw_small(spack_of(small_w), spack_of(small_g), spack_of(small_m), spack_of(small_v))
    small_d, small_nm, small_nv = (
        {n: a.reshape(-1)[int(woffs[i]):int(woffs[i + 1])].reshape(small_w[n].shape) for i, n in enumerate(SMALL)}
        for a in sm)

    b_off = int(soffs[1])
    dmod_all = sall.reshape(N_DEV, -1)[:, b_off:b_off + L * N_DEV * NC].reshape(N_DEV, L, N_DEV * NC)
    dmod_cols = jnp.moveaxis(lax.dynamic_slice_in_dim(dmod_all, me * NC, NC, axis=2), 0, 1)
    ada_g, ada_d, ada_nm, ada_nv = _ada_bwd_adamw(c_all.T, dmod_cols, w_ada, m_w_ada, v_w_ada)

    order = ["w_ada", "b_ada", "g_mix", "g_mlp", "mla_w_dq", "mla_g_q", "mla_w_uq", "mla_w_dkv", "mla_g_kv",
             "mla_w_ukv", "mla_w_o", "swa_w_qkv", "swa_b_qkv", "swa_sinks", "swa_w_o", "swa_b_o", "w_ff1", "w_ff2", "g_final"]

    def collect(ada, big, small):
        return [ada if n == "w_ada" else (big[n] if n in big else small[n]) for n in order]

    return (loss, dx0.reshape(x.shape), *collect(ada_g, big_g, small_g), *collect(ada_d, big_d, small_d),
            *collect(ada_nm, big_nm, small_nm), *collect(ada_nv, big_nv, small_nv))
```

```python
import jax
import jax.numpy as jnp
import numpy as np
from jax import lax
from jax.experimental import pallas as pl
from jax.experimental.pallas import tpu as pltpu

F32 = jnp.float32
BF16 = jnp.bfloat16
MESH_IDS = pl.DeviceIdType.MESH
N_DEV = 8

MLA_HEADS = 8
QK_NOPE = 128
QK_ROPE = 64
QK_DIM = QK_NOPE + QK_ROPE
V_DIM = 128
KV_LORA = 256
ROPE_THETA = 10000.0
SWA_HEADS = 16
SWA_KV_HEADS = 4
SWA_GROUP = SWA_HEADS // SWA_KV_HEADS
SWA_HEAD_DIM = 64
WINDOW = 128
EPS = 1e-6
LOG2E = 1.4426950408889634

ADAM_LR = 0.001
ADAM_B1 = 0.9
ADAM_B2 = 0.999
ADAM_EPS = 1e-08
ADAM_WD = 0.01
ADAM_STEP = 10

PACK_COLS = 1024
VMEM_LIMIT = 56 << 20
ROW_TILE = 512
ROW_TILE_WIDE = 256
ROW_TILE_BWD = 512
ATT_TILE = 512
ATT_TILE_FWD = 1024
TN_TOKENS = 4096


def _dot(a, b):
    return jnp.dot(a, b, preferred_element_type=F32)


def _dot_nt(a, b):
    return lax.dot_general(a, b, (((1,), (1,)), ((), ())), preferred_element_type=F32)


def _dot_tn(a, b):
    return lax.dot_general(a, b, (((0,), (0,)), ((), ())), preferred_element_type=F32)


def _rstd(x):
    return lax.rsqrt(jnp.mean(x * x, axis=-1, keepdims=True) + EPS)


def _rms_bwd(dn, n, r):
    return r * (dn - n * jnp.mean(dn * n, axis=-1, keepdims=True))


def _modulate(x, g, sc, sh):
    r = _rstd(x)
    return ((x * r) * g) * (1.0 + sc) + sh


def _modulate_bwd(dh, x, g, sc):
    r = _rstd(x)
    n = x * r
    dsh = jnp.sum(dh, axis=0, keepdims=True)
    da = jnp.sum(dh * n, axis=0, keepdims=True)
    dx = _rms_bwd(dh * (g * (1.0 + sc)), n, r)
    return dx, dsh, da


def _acc(ref, val, i):
    @pl.when(i == 0)
    def _():
        ref[...] = val

    @pl.when(i != 0)
    def _():
        ref[...] += val


def _row_spec(shape, tm):
    nd = len(shape)
    return pl.BlockSpec(tuple(shape[:nd - 2]) + (tm, shape[-1]), lambda i: (0,) * (nd - 2) + (i, 0))


def _resident_spec(shape, single_buffer):
    nd = len(shape)
    if single_buffer:
        return pl.BlockSpec(tuple(shape), lambda i: (0,) * nd, pipeline_mode=pl.Buffered(1))
    return pl.BlockSpec(tuple(shape), lambda i: (0,) * nd)


def _rowcall(name, body, tokens, tm, row_in, full_in, row_out, acc_out=()):
    tm = min(tm, tokens)
    in_specs = [_row_spec(a.shape, tm) for a in row_in] + [_resident_spec(a.shape, True) for a in full_in]
    row_specs = [s[1] if isinstance(s, tuple) else _row_spec(s.shape, tm) for s in row_out]
    row_out = [s[0] if isinstance(s, tuple) else s for s in row_out]
    out_specs = row_specs + [_resident_spec(s.shape, False) for s in acc_out]
    return pl.pallas_call(
        body, name=name, grid=(tokens // tm,), in_specs=in_specs, out_specs=out_specs,
        out_shape=list(row_out) + list(acc_out),
        compiler_params=pltpu.CompilerParams(dimension_semantics=("arbitrary",), vmem_limit_bytes=VMEM_LIMIT),
    )(*row_in, *full_in)


def _sds(shape, dtype):
    return jax.ShapeDtypeStruct(tuple(shape), dtype)


def _mla_in_fwd(x, cos, sin, g, sc, sh, w_cat, g_q, w_uqx, g_kv, w_ukv, t):
    S, D = x.shape
    QL = g_q.shape[1]
    H = MLA_HEADS
    t = min(t, S)

    def body(x_ref, cos_ref, sin_ref, g_ref, sc_ref, sh_ref, wcat_ref, gq_ref, wuqx_ref, gkv_ref, wukv_ref,
             h_ref, cqp_ref, cq_ref, ckvp_ref, ckv_ref, q_ref, k_ref, v_ref, vt_ref):
        cs, sn = cos_ref[...], sin_ref[...]
        hb = _modulate(x_ref[...], g_ref[...], sc_ref[...], sh_ref[...]).astype(BF16)
        h_ref[...] = hb
        low = _dot(hb, wcat_ref[...])
        cqp = low[:, :QL]
        cqp_ref[...] = cqp
        cq = ((cqp * _rstd(cqp)) * gq_ref[...]).astype(BF16)
        cq_ref[...] = cq
        ckvp = low[:, QL:QL + KV_LORA]
        ckvp_ref[...] = ckvp
        ckv = ((ckvp * _rstd(ckvp)) * gkv_ref[...]).astype(BF16)
        ckv_ref[...] = ckv
        o = QL + KV_LORA
        kr = (low[:, o:o + QK_ROPE] * cs + low[:, o + QK_ROPE:o + 2 * QK_ROPE] * sn).astype(BF16)
        qx = _dot(cq, wuqx_ref[...])
        kv = _dot(ckv, wukv_ref[...])
        for hd in range(H):
            b = hd * 256
            q_ref[hd, :, 0:QK_NOPE] = qx[:, b:b + QK_NOPE].astype(BF16)
            q_ref[hd, :, QK_NOPE:QK_DIM] = (qx[:, b + 128:b + 192] * cs + qx[:, b + 192:b + 256] * sn).astype(BF16)
            k_ref[hd, :, 0:QK_NOPE] = kv[:, b:b + QK_NOPE].astype(BF16)
            k_ref[hd, :, QK_NOPE:QK_DIM] = kr
            vh = kv[:, b + 128:b + 256]
            v_ref[hd] = vh.astype(BF16)
            vt_ref[hd, 0, 0:V_DIM, :] = vh.T.astype(BF16)
            vt_ref[hd, 0, V_DIM:2 * V_DIM, :] = jnp.ones((V_DIM, x_ref.shape[0]), BF16)

    vt_spec = pl.BlockSpec((H, 1, 2 * V_DIM, t), lambda i: (0, i, 0, 0))
    return _rowcall(
        "mla_in_fwd", body, S, t, [x, cos, sin], [g, sc, sh, w_cat, g_q, w_uqx, g_kv, w_ukv],
        [_sds((S, D), BF16), _sds((S, QL), F32), _sds((S, QL), BF16), _sds((S, KV_LORA), F32), _sds((S, KV_LORA), BF16),
         _sds((H, S, QK_DIM), BF16), _sds((H, S, QK_DIM), BF16), _sds((H, S, V_DIM), BF16),
         (_sds((H, S // t, 2 * V_DIM, t), BF16), vt_spec)])


def _mla_attn_fwd(q, k, vt, t, sends):
    H, S, DQ = q.shape
    DV = V_DIM
    vb = vt.shape[-1]
    t = max(min(t, S), vb)
    nb = S // t
    scale = QK_DIM ** -0.5
    c2 = scale * LOG2E

    ns = len(sends)

    def body(q_ref, k_ref, vt_ref, *rest):
        send_refs, (o_ref, lse_ref), gath_refs = rest[:ns], rest[ns:ns + 2], rest[ns + 2:2 * ns + 2]
        m_s, acc_s, s_buf, send_sems, recv_sems, local_sems = rest[2 * ns + 2:]
        hd, qi = pl.program_id(0), pl.program_id(1)

        def gather():
            return _exchange_copies([lambda j, r=r: r for r in send_refs], gath_refs, send_sems, recv_sems, local_sems)

        @pl.when((hd == 0) & (qi == 0))
        def _():
            _start_exchange(gather())

        m_s[...] = jnp.full_like(m_s, -jnp.inf)
        acc_s[...] = jnp.zeros_like(acc_s)

        def scores(j, slot):
            rows = pl.ds(pl.multiple_of(j * t, t), t)
            s_buf[slot] = _dot_nt(k_ref[0, rows, :], q_ref[0])

        nvb = t // vb

        def update(s, j, blocks, cols):
            m_prev = m_s[:, cols]
            m_new = jnp.maximum(m_prev, jnp.max(s, axis=0, keepdims=True))
            alpha = jnp.exp2((m_prev - m_new) * c2)
            pb = jnp.exp2((s - m_new) * c2).astype(BF16)
            acc = alpha * acc_s[:, cols]
            for n, u in enumerate(blocks):
                acc = acc + _dot(vt_ref[0, j * nvb + u], pb[n * vb:(n + 1) * vb, :])
            acc_s[:, cols] = acc
            m_s[:, cols] = m_new

        def causal(s, shape):
            key = lax.broadcasted_iota(jnp.int32, shape, 0)
            qry = lax.broadcasted_iota(jnp.int32, shape, 1)
            return jnp.where(key <= qry, s, -jnp.inf)

        def absorb(j, slot, diagonal):
            if not diagonal:
                update(s_buf[slot], j, range(nvb), slice(None))
            elif nvb % 2:
                update(causal(s_buf[slot], (t, t)), j, range(nvb), slice(None))
            else:
                half = t // 2
                update(causal(s_buf[slot, :half], (half, t)), j, range(nvb // 2), slice(None))
                update(causal(s_buf[slot, half:, half:], (half, half)), j, range(nvb // 2, nvb), slice(half, t))

        def pair(i, carry):
            j = 2 * i
            scores(j + 1, 1)
            absorb(j, 0, False)
            scores(j + 2, 0)
            absorb(j + 1, 1, False)
            return carry

        scores(0, 0)
        lax.fori_loop(0, qi // 2, pair, 0)

        @pl.when(qi % 2 == 0)
        def _():
            absorb(qi, 0, True)

        @pl.when(qi % 2 == 1)
        def _():
            scores(qi, 1)
            absorb(qi - 1, 0, False)
            absorb(qi, 1, True)

        acc = acc_s[...]
        o_ref[...] = (acc[:DV] / acc[DV:]).T.astype(BF16)
        lse_ref[0, 0] = m_s[...] * scale + jnp.log(acc[DV:DV + 1])

        @pl.when((hd == H - 1) & (qi == nb - 1))
        def _():
            _finish_exchange(gather())

    outs = pl.pallas_call(
        body, name="mla_attn_fwd", grid=(H, nb),
        in_specs=[pl.BlockSpec((1, t, DQ), lambda h, i: (h, i, 0)),
                  pl.BlockSpec((1, S, DQ), lambda h, i: (h, 0, 0)),
                  pl.BlockSpec((1, S // vb, 2 * DV, vb), lambda h, i: (h, 0, 0, 0))] + [ANY_SPEC] * ns,
        out_specs=[pl.BlockSpec((t, DV), lambda h, i: (i, h)),
                   pl.BlockSpec((1, 1, 1, t), lambda h, i: (h, i, 0, 0))] + [ANY_SPEC] * ns,
        out_shape=[_sds((S, H * DV), BF16), _sds((H, nb, 1, t), F32)]
        + [_sds((N_DEV,) + a.shape, a.dtype) for a in sends],
        scratch_shapes=[pltpu.VMEM((1, t), F32), pltpu.VMEM((2 * DV, t), F32), pltpu.VMEM((2, t, t), F32)]
        + _comm_sems(ns),
        compiler_params=pltpu.CompilerParams(dimension_semantics=("arbitrary", "arbitrary"),
                                             vmem_limit_bytes=VMEM_LIMIT),
    )(q, k, vt, *sends)
    return outs[0], outs[1], outs[2:]


def _attn_out_fwd(o, x, w_o, b_o, gt, g, sc, sh):
    S, D = x.shape

    def body(o_ref, x_ref, wo_ref, bo_ref, gt_ref, g_ref, sc_ref, sh_ref, y_ref, x1_ref, h_ref):
        y = _dot(o_ref[...], wo_ref[...]) + bo_ref[...]
        y_ref[...] = y.astype(BF16)
        x1 = x_ref[...] + gt_ref[...] * y
        x1_ref[...] = x1
        h_ref[...] = _modulate(x1, g_ref[...], sc_ref[...], sh_ref[...]).astype(BF16)

    return _rowcall("attn_out_fwd", body, S, ROW_TILE, [o, x], [w_o, b_o, gt, g, sc, sh],
                    [_sds((S, D), BF16), _sds((S, D), F32), _sds((S, D), BF16)])


def _mlp_fwd(h, x, w1, w2, gt):
    S, D = x.shape
    FF = w1.shape[1]

    def body(h_ref, x_ref, w1_ref, w2_ref, gt_ref, rl_ref, act_ref, y_ref, x2_ref):
        rl = jnp.maximum(_dot(h_ref[...], w1_ref[...]), 0.0)
        rl_ref[...] = rl.astype(BF16)
        act = (rl * rl).astype(BF16)
        act_ref[...] = act
        y = _dot(act, w2_ref[...])
        y_ref[...] = y.astype(BF16)
        x2_ref[...] = x_ref[...] + gt_ref[...] * y

    return _rowcall("mlp_fwd", body, S, ROW_TILE_WIDE, [h, x], [w1, w2, gt],
                    [_sds((S, FF), BF16), _sds((S, FF), BF16), _sds((S, D), BF16), _sds((S, D), F32)])


def _final_norm_loss(xv, target, g, d_model):
    r = _rstd(xv)
    n = xv * r
    err = n * g - target
    part = 0.5 * jnp.sum(jnp.mean(err * err, axis=-1, keepdims=True), axis=0, keepdims=True)
    dout = err / d_model
    return part, _rms_bwd(dout * g, n, r), jnp.sum(dout * n, axis=0, keepdims=True)


def _mlp_fwd_loss(h, x, w1, w2, gt, target, g_final):
    S, D = x.shape
    FF = w1.shape[1]

    def body(h_ref, x_ref, t_ref, w1_ref, w2_ref, gt_ref, g_ref, rl_ref, act_ref, y_ref, dx_ref, loss_ref, dg_ref):
        i = pl.program_id(0)
        rl = jnp.maximum(_dot(h_ref[...], w1_ref[...]), 0.0)
        rl_ref[...] = rl.astype(BF16)
        act = (rl * rl).astype(BF16)
        act_ref[...] = act
        y = _dot(act, w2_ref[...])
        y_ref[...] = y.astype(BF16)
        part, dx, dg = _final_norm_loss(x_ref[...] + gt_ref[...] * y, t_ref[...], g_ref[...], D)
        dx_ref[...] = dx
        _acc(loss_ref, jnp.broadcast_to(part, loss_ref.shape), i)
        _acc(dg_ref, dg, i)

    return _rowcall("mlp_fwd_loss", body, S, ROW_TILE_WIDE, [h, x, target], [w1, w2, gt, g_final],
                    [_sds((S, FF), BF16), _sds((S, FF), BF16), _sds((S, D), BF16), _sds((S, D), F32)],
                    [_sds((1, 128), F32), _sds((1, D), F32)])


def _swa_in_fwd(x, g, sc, sh, w_qkv, b_qkv):
    S, D = x.shape
    NQ = SWA_HEADS * SWA_HEAD_DIM
    NK = SWA_KV_HEADS * SWA_HEAD_DIM

    def body(x_ref, g_ref, sc_ref, sh_ref, w_ref, b_ref, h_ref, q_ref, k_ref, v_ref):
        hb = _modulate(x_ref[...], g_ref[...], sc_ref[...], sh_ref[...]).astype(BF16)
        h_ref[...] = hb
        qkv = _dot(hb, w_ref[...]) + b_ref[...]
        q_ref[...] = qkv[:, :NQ].astype(BF16)
        k_ref[...] = qkv[:, NQ:NQ + NK].astype(BF16)
        v_ref[...] = qkv[:, NQ + NK:].astype(BF16)

    return _rowcall("swa_in_fwd", body, S, ROW_TILE, [x], [g, sc, sh, w_qkv, b_qkv],
                    [_sds((S, D), BF16), _sds((S, NQ), BF16), _sds((S, NK), BF16), _sds((S, NK), BF16)])


def _alibi_slope(head):
    return float(np.float32(2.0 ** (-8.0 * (head + 1) / SWA_HEADS)))


def _swa_geometry(n):
    W, G = WINDOW, SWA_GROUP
    key = lax.broadcasted_iota(jnp.int32, (2 * W, G * W), 0)
    qry = lax.broadcasted_iota(jnp.int32, (2 * W, G * W), 1) & (W - 1)
    dist = W + qry - key
    valid = (dist >= 0) & (dist < W) & ((n > 0) | (key >= W))
    return dist.astype(F32), valid


def _swa_group(kh, q_ref, sink_ref):
    W, G, Dh = WINDOW, SWA_GROUP, SWA_HEAD_DIM
    heads = [kh * G + g for g in range(G)]
    q4 = jnp.concatenate([q_ref[:, h * Dh:(h + 1) * Dh] for h in heads], axis=0)
    slopes = jnp.concatenate([jnp.full((1, W), _alibi_slope(h), F32) for h in heads], axis=1)
    sinks = jnp.concatenate([jnp.broadcast_to(sink_ref[:, h:h + 1], (1, W)) for h in heads], axis=1)
    return heads, q4, slopes, sinks


def _swa_band_specs(W, nb, cols):
    prev = pl.BlockSpec((W, cols), lambda n: (jnp.maximum(jnp.minimum(n, nb - 1) - 1, 0), 0))
    cur = pl.BlockSpec((W, cols), lambda n: (jnp.minimum(n, nb - 1), 0))
    return prev, cur


def _swa_attn_fwd(q, k, v, sinks, x, w_o, b_o, gt, g, sc, sh):
    S, NQ = q.shape
    NK = k.shape[1]
    D = x.shape[1]
    W, Dh, G = WINDOW, SWA_HEAD_DIM, SWA_GROUP
    nb = S // W

    def body(q_ref, kp_ref, kc_ref, vp_ref, vc_ref, sink_ref, x_ref, wo_ref, bo_ref, gt_ref, g_ref, sc_ref, sh_ref,
             o_ref, lse_ref, y_ref, x1_ref, h_ref):
        distf, valid = _swa_geometry(pl.program_id(0))
        kband = jnp.concatenate([kp_ref[...], kc_ref[...]], axis=0)
        vband_t = jnp.concatenate([vp_ref[...], vc_ref[...]], axis=0).astype(F32).T.astype(BF16)
        outs = []
        for kh in range(SWA_KV_HEADS):
            kb = kband[:, kh * Dh:(kh + 1) * Dh]
            vbt = vband_t[kh * Dh:(kh + 1) * Dh, :]
            heads, q4, slopes, sinks = _swa_group(kh, q_ref, sink_ref)
            s = _dot_nt(kb, q4) * (Dh ** -0.5) - slopes * distf
            s = jnp.where(valid, s, -jnp.inf)
            m = jnp.maximum(jnp.max(s, axis=0, keepdims=True), sinks)
            p = jnp.exp(s - m)
            denom = jnp.sum(p, axis=0, keepdims=True) + jnp.exp(sinks - m)
            out4 = _dot(vbt, (p * (1.0 / denom)).astype(BF16))
            lse4 = m + jnp.log(denom)
            for g, h in enumerate(heads):
                outs.append(out4[:, g * W:(g + 1) * W])
                lse_ref[h:h + 1, :] = lse4[:, g * W:(g + 1) * W]
        ob = jnp.concatenate(outs, axis=0).T.astype(BF16)
        o_ref[...] = ob
        y = _dot(ob, wo_ref[...]) + bo_ref[...]
        y_ref[...] = y.astype(BF16)
        x1 = x_ref[...] + gt_ref[...] * y
        x1_ref[...] = x1
        h_ref[...] = _modulate(x1, g_ref[...], sc_ref[...], sh_ref[...]).astype(BF16)

    kprev, kcur = _swa_band_specs(W, nb, NK)
    blk = lambda cols: pl.BlockSpec((W, cols), lambda n: (n, 0))
    row = pl.BlockSpec((1, D), lambda n: (0, 0))
    return pl.pallas_call(
        body, name="swa_attn_fwd", grid=(nb,),
        in_specs=[blk(NQ), kprev, kcur, kprev, kcur, pl.BlockSpec((1, SWA_HEADS), lambda n: (0, 0)), blk(D),
                  pl.BlockSpec(w_o.shape, lambda n: (0, 0), pipeline_mode=pl.Buffered(1)), row, row, row, row, row],
        out_specs=[blk(NQ), pl.BlockSpec((SWA_HEADS, W), lambda n: (0, n)), blk(D), blk(D), blk(D)],
        out_shape=[_sds((S, NQ), BF16), _sds((SWA_HEADS, S), F32), _sds((S, D), BF16), _sds((S, D), F32),
                   _sds((S, D), BF16)],
        compiler_params=pltpu.CompilerParams(dimension_semantics=("arbitrary",), vmem_limit_bytes=VMEM_LIMIT),
    )(q, k, k, v, v, sinks, x, w_o, b_o, gt, g, sc, sh)


def _mlp_bwd_a(dx, y, rl, gt, w2):
    S, D = dx.shape
    FF = rl.shape[1]

    def body(dx_ref, y_ref, rl_ref, gt_ref, w2_ref, dy_ref, du_ref, dgt_ref):
        i = pl.program_id(0)
        dxv = dx_ref[...]
        _acc(dgt_ref, jnp.sum(dxv * y_ref[...].astype(F32), axis=0, keepdims=True), i)
        dy = (dxv * gt_ref[...]).astype(BF16)
        dy_ref[...] = dy
        dact = _dot_nt(dy, w2_ref[...])
        du_ref[...] = (dact * (2.0 * rl_ref[...].astype(F32))).astype(BF16)

    return _rowcall("mlp_bwd_a", body, S, ROW_TILE_BWD, [dx, y, rl], [gt, w2],
                    [_sds((S, D), BF16), _sds((S, FF), BF16)], [_sds((1, D), F32)])


def _mlp_bwd_b(du, x, dx, w1, g, sc):
    S, D = x.shape

    def body(du_ref, x_ref, dx_ref, w1_ref, g_ref, sc_ref, dxo_ref, dsh_ref, da_ref):
        i = pl.program_id(0)
        dh = _dot_nt(du_ref[...], w1_ref[...])
        dxn, dsh, da = _modulate_bwd(dh, x_ref[...], g_ref[...], sc_ref[...])
        dxo_ref[...] = dx_ref[...] + dxn
        _acc(dsh_ref, dsh, i)
        _acc(da_ref, da, i)

    return _rowcall("mlp_bwd_b", body, S, ROW_TILE_BWD, [du, x, dx], [w1, g, sc],
                    [_sds((S, D), F32)], [_sds((1, D), F32), _sds((1, D), F32)])


def _attn_out_bwd(dx, y, o, gt, w_o, n_heads):
    S, D = dx.shape
    NO = o.shape[1]
    dh = NO // n_heads
    member = (jnp.arange(NO)[None, :] // dh == jnp.arange(16)[:, None]).astype(BF16)

    def body(dx_ref, y_ref, o_ref, gt_ref, wo_ref, mem_ref, dy_ref, do_ref, dl_ref, dgt_ref, dbo_ref):
        i = pl.program_id(0)
        dxv = dx_ref[...]
        _acc(dgt_ref, jnp.sum(dxv * y_ref[...].astype(F32), axis=0, keepdims=True), i)
        dy = dxv * gt_ref[...]
        _acc(dbo_ref, jnp.sum(dy, axis=0, keepdims=True), i)
        dyb = dy.astype(BF16)
        dy_ref[...] = dyb
        do = _dot_nt(dyb, wo_ref[...])
        do_ref[...] = do.astype(BF16)
        prod = do * o_ref[...].astype(F32)
        hi = prod.astype(BF16)
        lo = (prod - hi.astype(F32)).astype(BF16)
        dl_ref[...] = _dot_nt(mem_ref[...], hi) + _dot_nt(mem_ref[...], lo)

    tm = min(ROW_TILE, S)
    return _rowcall("attn_out_bwd", body, S, ROW_TILE, [dx, y, o], [gt, w_o, member],
                    [_sds((S, D), BF16), _sds((S, NO), BF16),
                     (_sds((16, S), F32), pl.BlockSpec((16, tm), lambda i: (0, i)))],
                    [_sds((1, D), F32), _sds((1, D), F32)])


def _mla_attn_bwd(q, k, v, do, lse, delta, t, gblks):
    H, S, DQ = q.shape
    DV = V_DIM
    t = min(t, S // 2)
    tk = 2 * t
    nq, nk = S // t, S // tk
    scale = QK_DIM ** -0.5
    c2 = scale * LOG2E

    ng = len(gblks)

    def body(q_ref, k_ref, v_ref, do_ref, lse_ref, dl_ref, *rest):
        g_refs, (dq_ref, dk_ref, dv_ref), recv_refs = rest[:ng], rest[ng:ng + 3], rest[ng + 3:2 * ng + 3]
        dk_s, dv_s, s_buf, dp_buf, send_sems, recv_sems, local_sems = rest[2 * ng + 3:]
        hd, kj = pl.program_id(0), pl.program_id(1)

        def scatter():
            return _exchange_copies([lambda j, r=r: r.at[j] for r in g_refs], recv_refs, send_sems, recv_sems,
                                    local_sems)

        @pl.when((hd == 0) & (kj == 0))
        def _():
            _start_exchange(scatter())

        @pl.when(kj == 0)
        def _():
            dq_ref[...] = jnp.zeros_like(dq_ref)

        dk_s[...] = jnp.zeros_like(dk_s)
        dv_s[...] = jnp.zeros_like(dv_s)

        def products(i, slot, keys=tk):
            rows = pl.ds(pl.multiple_of(i * t, t), t)
            s_buf[slot, :keys] = _dot_nt(k_ref[0, :keys], q_ref[0, rows, :])
            dp_buf[slot, :keys] = _dot_nt(v_ref[0, :keys], do_ref[rows, :])

        def absorb(i, slot, diagonal, keys=tk):
            rows = pl.ds(pl.multiple_of(i * t, t), t)
            qb, dob = q_ref[0, rows, :], do_ref[rows, :]
            p = jnp.exp2(s_buf[slot, :keys] * c2 - lse_ref[0, i])
            if diagonal is not None:
                key = lax.broadcasted_iota(jnp.int32, (keys, t), 0)
                qry = lax.broadcasted_iota(jnp.int32, (keys, t), 1) + diagonal * t
                p = jnp.where(key <= qry, p, 0.0)
            dv_s[:keys] += _dot(p.astype(BF16), dob)
            ds = (p * (dp_buf[slot, :keys] - dl_ref[0, i])).astype(BF16)
            dk_s[:keys] += _dot(ds, qb)
            dq_ref[0, rows, :] += _dot_tn(ds, k_ref[0, :keys])

        first = 2 * kj + 2
        n_off = nq - first

        def pair(i, carry):
            u = 2 * i
            products(first + u + 1, 1)
            absorb(first + u, 0, None)
            products(jnp.where(u + 2 < n_off, first + u + 2, 2 * kj + 1), 0)
            absorb(first + u + 1, 1, None)
            return carry

        products(jnp.where(n_off > 0, first, 2 * kj + 1), 0)
        lax.fori_loop(0, n_off // 2, pair, 0)
        products(2 * kj, 1, t)
        absorb(2 * kj + 1, 0, 1)
        absorb(2 * kj, 1, 0, t)

        dk_ref[0] = (dk_s[...] * scale).astype(BF16)
        dv_ref[0] = dv_s[...].astype(BF16)

        @pl.when((hd == H - 1) & (kj == nk - 1))
        def _():
            _finish_exchange(scatter())

    rowspec = pl.BlockSpec((1, nq, 1, t), lambda h, j: (h, 0, 0, 0))
    outs = pl.pallas_call(
        body, name="mla_attn_bwd", grid=(H, nk),
        in_specs=[pl.BlockSpec((1, S, DQ), lambda h, j: (h, 0, 0)),
                  pl.BlockSpec((1, tk, DQ), lambda h, j: (h, j, 0)),
                  pl.BlockSpec((1, tk, DV), lambda h, j: (h, j, 0)),
                  pl.BlockSpec((S, DV), lambda h, j: (0, h)), rowspec, rowspec] + [ANY_SPEC] * ng,
        out_specs=[pl.BlockSpec((1, S, DQ), lambda h, j: (h, 0, 0)),
                   pl.BlockSpec((1, tk, DQ), lambda h, j: (h, j, 0)),
                   pl.BlockSpec((1, tk, DV), lambda h, j: (h, j, 0))] + [ANY_SPEC] * ng,
        out_shape=[_sds((H, S, DQ), F32), _sds((H, S, DQ), BF16), _sds((H, S, DV), BF16)]
        + [_sds(g.shape, g.dtype) for g in gblks],
        scratch_shapes=[pltpu.VMEM((tk, DQ), F32), pltpu.VMEM((tk, DV), F32), pltpu.VMEM((2, tk, t), F32),
                        pltpu.VMEM((2, tk, t), F32)] + _comm_sems(ng),
        compiler_params=pltpu.CompilerParams(dimension_semantics=("arbitrary", "arbitrary"),
                                             vmem_limit_bytes=VMEM_LIMIT),
    )(q, k, v, do, lse, delta, *gblks)
    return outs[0], outs[1], outs[2], outs[3:]


def _swa_attn_bwd(q, k, v, lse, sinks, dx, y, o, gt, w_o):
    S, NQ = q.shape
    NK = k.shape[1]
    D = dx.shape[1]
    W, Dh, G = WINDOW, SWA_HEAD_DIM, SWA_GROUP
    nb = S // W
    member = (jnp.arange(NQ)[None, :] // Dh == jnp.arange(SWA_HEADS)[:, None]).astype(BF16)

    def body(q_ref, kp_ref, kc_ref, vp_ref, vc_ref, lse_ref, sink_ref, dx_ref, y_ref, o_ref, gt_ref, wo_ref, mem_ref,
             dq_ref, dk_ref, dv_ref, dsink_ref, dy_ref, dgt_ref, dbo_ref, dkc_s, dvc_s):
        n = pl.program_id(0)

        @pl.when(n == 0)
        def _():
            dkc_s[...] = jnp.zeros_like(dkc_s)
            dvc_s[...] = jnp.zeros_like(dvc_s)
            dsink_ref[...] = jnp.zeros_like(dsink_ref)

        @pl.when(n < nb)
        def _():
            dxv = dx_ref[...]
            _acc(dgt_ref, jnp.sum(dxv * y_ref[...].astype(F32), axis=0, keepdims=True), n)
            dy = dxv * gt_ref[...]
            _acc(dbo_ref, jnp.sum(dy, axis=0, keepdims=True), n)
            dyb = dy.astype(BF16)
            dy_ref[...] = dyb
            do = _dot_nt(dyb, wo_ref[...])
            dob = do.astype(BF16)
            prod = do * o_ref[...].astype(F32)
            hi = prod.astype(BF16)
            lo = (prod - hi.astype(F32)).astype(BF16)
            dl = _dot_nt(mem_ref[...], hi) + _dot_nt(mem_ref[...], lo)
            distf, valid = _swa_geometry(n)
            kband = jnp.concatenate([kp_ref[...], kc_ref[...]], axis=0)
            vband = jnp.concatenate([vp_ref[...], vc_ref[...]], axis=0)
            kband_t = kband.astype(F32).T.astype(BF16)
            dq_t = []
            for kh in range(SWA_KV_HEADS):
                ck = slice(kh * Dh, (kh + 1) * Dh)
                kb, vb, kbt = kband[:, ck], vband[:, ck], kband_t[ck, :]
                heads, q4, slopes, sinks = _swa_group(kh, q_ref, sink_ref)
                do4 = jnp.concatenate([dob[:, h * Dh:(h + 1) * Dh] for h in heads], axis=0)
                lse4 = jnp.concatenate([lse_ref[h:h + 1, :] for h in heads], axis=1)
                dl4 = jnp.concatenate([dl[h:h + 1, :] for h in heads], axis=1)
                s = _dot_nt(kb, q4) * (Dh ** -0.5) - slopes * distf
                p = jnp.where(valid, jnp.exp(s - lse4), 0.0)
                dvb = _dot(p.astype(BF16), do4)
                dp = _dot_nt(vb, do4)
                dsb = ((p * (dp - dl4)) * (Dh ** -0.5)).astype(BF16)
                dq4 = _dot(kbt, dsb)
                dkb = _dot(dsb, q4)
                dsk4 = jnp.exp(sinks - lse4) * dl4
                for g, h in enumerate(heads):
                    dq_t.append(dq4[:, g * W:(g + 1) * W])
                    dsink_ref[:, h:h + 1] += -jnp.sum(dsk4[:, g * W:(g + 1) * W], axis=1, keepdims=True)
                dk_ref[:, ck] = (dkc_s[:, ck] + dkb[:W]).astype(BF16)
                dv_ref[:, ck] = (dvc_s[:, ck] + dvb[:W]).astype(BF16)
                dkc_s[:, ck] = dkb[W:]
                dvc_s[:, ck] = dvb[W:]
            dq_ref[...] = jnp.concatenate(dq_t, axis=0).T.astype(BF16)

        @pl.when(n == nb)
        def _():
            dk_ref[...] = dkc_s[...].astype(BF16)
            dv_ref[...] = dvc_s[...].astype(BF16)

    kprev, kcur = _swa_band_specs(W, nb, NK)
    qspec = lambda cols: pl.BlockSpec((W, cols), lambda n: (jnp.minimum(n, nb - 1), 0))
    kvout = pl.BlockSpec((W, NK), lambda n: (jnp.maximum(n - 1, 0), 0))
    rowspec = pl.BlockSpec((SWA_HEADS, W), lambda n: (0, jnp.minimum(n, nb - 1)))
    fixed = lambda shape: pl.BlockSpec(shape, lambda n: (0, 0))
    return pl.pallas_call(
        body, name="swa_attn_bwd", grid=(nb + 1,),
        in_specs=[qspec(NQ), kprev, kcur, kprev, kcur, rowspec, pl.BlockSpec((1, SWA_HEADS), lambda n: (0, 0)),
                  qspec(D), qspec(D), qspec(NQ), fixed((1, D)), fixed(w_o.shape), fixed(member.shape)],
        out_specs=[qspec(NQ), kvout, kvout, fixed((1, 128)), qspec(D), fixed((1, D)), fixed((1, D))],
        out_shape=[_sds((S, NQ), BF16), _sds((S, NK), BF16), _sds((S, NK), BF16), _sds((1, 128), F32),
                   _sds((S, D), BF16), _sds((1, D), F32), _sds((1, D), F32)],
        scratch_shapes=[pltpu.VMEM((W, NK), F32), pltpu.VMEM((W, NK), F32)],
        compiler_params=pltpu.CompilerParams(dimension_semantics=("arbitrary",), vmem_limit_bytes=VMEM_LIMIT),
    )(q, k, k, v, v, lse, sinks, dx, y, o, gt, w_o, member)


def _swa_in_bwd(dq, dk, dv, x, dx, w_qkv, g, sc):
    S, D = x.shape
    N = w_qkv.shape[1]

    def body(dq_ref, dk_ref, dv_ref, x_ref, dx_ref, w_ref, g_ref, sc_ref, dqkv_ref, dxo_ref, db_ref, dsh_ref, da_ref):
        i = pl.program_id(0)
        dqkv = jnp.concatenate([dq_ref[...], dk_ref[...], dv_ref[...]], axis=1)
        dqkv_ref[...] = dqkv
        _acc(db_ref, jnp.sum(dqkv.astype(F32), axis=0, keepdims=True), i)
        dh = _dot_nt(dqkv, w_ref[...])
        dxn, dsh, da = _modulate_bwd(dh, x_ref[...], g_ref[...], sc_ref[...])
        dxo_ref[...] = dx_ref[...] + dxn
        _acc(dsh_ref, dsh, i)
        _acc(da_ref, da, i)

    return _rowcall("swa_in_bwd", body, S, ROW_TILE, [dq, dk, dv, x, dx], [w_qkv, g, sc],
                    [_sds((S, N), BF16), _sds((S, D), F32)],
                    [_sds((1, N), F32), _sds((1, D), F32), _sds((1, D), F32)])


def _mla_in_bwd(dq, dk, dv, cos, sin, cqp, ckvp, x, dx, w_uqx, g_q, w_ukv, g_kv, w_cat, g, sc):
    S, D = x.shape
    H = MLA_HEADS
    QL = g_q.shape[1]
    NX = w_uqx.shape[1]
    NC = w_cat.shape[1]

    def body(dq_ref, dk_ref, dv_ref, cos_ref, sin_ref, cqp_ref, ckvp_ref, x_ref, dx_ref,
             wuqx_ref, gq_ref, wukv_ref, gkv_ref, wcat_ref, g_ref, sc_ref,
             dqx_ref, dkv_ref, dcat_ref, dxo_ref, dgq_ref, dgkv_ref, dsh_ref, da_ref):
        i = pl.program_id(0)
        cs, sn = cos_ref[...], sin_ref[...]
        dkr = jnp.zeros(cs.shape, F32)
        for hd in range(H):
            b = hd * 256
            dqh = dq_ref[hd] * (QK_DIM ** -0.5)
            dqx_ref[:, b:b + QK_NOPE] = dqh[:, :QK_NOPE].astype(BF16)
            dqx_ref[:, b + 128:b + 192] = (dqh[:, QK_NOPE:] * cs).astype(BF16)
            dqx_ref[:, b + 192:b + 256] = (dqh[:, QK_NOPE:] * sn).astype(BF16)
            dkh = dk_ref[hd]
            dkv_ref[:, b:b + QK_NOPE] = dkh[:, :QK_NOPE]
            dkv_ref[:, b + 128:b + 256] = dv_ref[hd]
            dkr = dkr + dkh[:, QK_NOPE:].astype(F32)
        dcq = _dot_nt(dqx_ref[...], wuqx_ref[...])
        cqp = cqp_ref[...]
        rq = _rstd(cqp)
        nq = cqp * rq
        _acc(dgq_ref, jnp.sum(dcq * nq, axis=0, keepdims=True), i)
        dcqp = _rms_bwd(dcq * gq_ref[...], nq, rq)
        dckv = _dot_nt(dkv_ref[...], wukv_ref[...])
        ckvp = ckvp_ref[...]
        rk = _rstd(ckvp)
        nk = ckvp * rk
        _acc(dgkv_ref, jnp.sum(dckv * nk, axis=0, keepdims=True), i)
        dckvp = _rms_bwd(dckv * gkv_ref[...], nk, rk)
        dcat_ref[:, :QL] = dcqp.astype(BF16)
        dcat_ref[:, QL:QL + KV_LORA] = dckvp.astype(BF16)
        o = QL + KV_LORA
        dcat_ref[:, o:o + QK_ROPE] = (dkr * cs).astype(BF16)
        dcat_ref[:, o + QK_ROPE:o + 2 * QK_ROPE] = (dkr * sn).astype(BF16)
        dh = _dot_nt(dcat_ref[...], wcat_ref[...])
        dxn, dsh, da = _modulate_bwd(dh, x_ref[...], g_ref[...], sc_ref[...])
        dxo_ref[...] = dx_ref[...] + dxn
        _acc(dsh_ref, dsh, i)
        _acc(da_ref, da, i)

    return _rowcall("mla_in_bwd", body, S, ROW_TILE, [dq, dk, dv, cos, sin, cqp, ckvp, x, dx],
                    [w_uqx, g_q, w_ukv, g_kv, w_cat, g, sc],
                    [_sds((S, NX), BF16), _sds((S, NX), BF16), _sds((S, NC), BF16), _sds((S, D), F32)],
                    [_sds((1, QL), F32), _sds((1, KV_LORA), F32), _sds((1, D), F32), _sds((1, D), F32)])


def _matmul_tn(name, a, b, out_dtype=F32, column_blocks=False):
    S, K = a.shape
    N = b.shape[1]
    tk, tn, ts = min(K, 1024), min(N, 1024), min(S, TN_TOKENS)
    if column_blocks:
        tn = N // N_DEV
    if N % tn:
        tn = 512 if N % 512 == 0 else (384 if N % 384 == 0 else 128)
    if K % tk:
        tk = 512 if K % 512 == 0 else (384 if K % 384 == 0 else 128)
    ns = S // ts

    def body(a_ref, b_ref, o_ref, *scratch):
        acc_ref = scratch[0] if scratch else o_ref
        _acc(acc_ref, _dot_tn(a_ref[...], b_ref[...]), pl.program_id(2))
        if scratch:
            @pl.when(pl.program_id(2) == ns - 1)
            def _():
                o_ref[...] = acc_ref[...].astype(out_dtype)

    if column_blocks:
        out_spec = pl.BlockSpec((None, tk, tn), lambda i, j, s: (j, i, 0))
        out_shape = _sds((N_DEV, K, tn), out_dtype)
    else:
        out_spec = pl.BlockSpec((tk, tn), lambda i, j, s: (i, j))
        out_shape = _sds((K, N), out_dtype)
    return pl.pallas_call(
        body, name=name, grid=(K // tk, N // tn, ns),
        in_specs=[pl.BlockSpec((ts, tk), lambda i, j, s: (s, i)), pl.BlockSpec((ts, tn), lambda i, j, s: (s, j))],
        out_specs=out_spec, out_shape=out_shape,
        scratch_shapes=[] if out_dtype == F32 else [pltpu.VMEM((tk, tn), F32)],
        compiler_params=pltpu.CompilerParams(dimension_semantics=("parallel", "parallel", "arbitrary"),
                                             vmem_limit_bytes=VMEM_LIMIT),
    )(a, b)


def _silu(c):
    return c * jax.nn.sigmoid(c)


def _ada_fwd(c_all, w_ada):
    L, D, NC = w_ada.shape

    def body(c_ref, w_ref, o_ref):
        cond = _silu(c_ref[...]).astype(BF16)
        o_ref[0] = _dot(cond, w_ref[0].astype(BF16))

    return pl.pallas_call(
        body, name="ada_fwd", grid=(L,),
        in_specs=[pl.BlockSpec(c_all.shape, lambda l: (0, 0)), pl.BlockSpec((1, D, NC), lambda l: (l, 0, 0))],
        out_specs=pl.BlockSpec((1, N_DEV, NC), lambda l: (l, 0, 0)),
        out_shape=_sds((L, N_DEV, NC), F32),
        compiler_params=pltpu.CompilerParams(dimension_semantics=("arbitrary",), vmem_limit_bytes=VMEM_LIMIT),
    )(c_all, w_ada)


def _adamw(w, g, m, v):
    m = ADAM_B1 * m + (1.0 - ADAM_B1) * g
    v = ADAM_B2 * v + (1.0 - ADAM_B2) * (g * g)
    m_hat = m / (1.0 - ADAM_B1 ** ADAM_STEP)
    v_hat = v / (1.0 - ADAM_B2 ** ADAM_STEP)
    delta = -ADAM_LR * (m_hat / (jnp.sqrt(v_hat) + ADAM_EPS) + ADAM_WD * w)
    return delta, m, v


def _ada_bwd_adamw(c_all_t, dmod_cols, w, m, v):
    L, D, NC = w.shape
    tr = min(D, 256)

    def body(ct_ref, dm_ref, w_ref, m_ref, v_ref, g_ref, d_ref, mo_ref, vo_ref):
        cond_t = _silu(ct_ref[...])
        dm = dm_ref[0]
        g = cond_t[:, 0:1] * dm[0:1, :]
        for b in range(1, N_DEV):
            g = g + cond_t[:, b:b + 1] * dm[b:b + 1, :]
        g_ref[0] = g
        d_ref[0], mo_ref[0], vo_ref[0] = _adamw(w_ref[0], g, m_ref[0], v_ref[0])

    wspec = pl.BlockSpec((1, tr, NC), lambda l, r: (l, r, 0))
    return pl.pallas_call(
        body, name="ada_bwd_adamw", grid=(L, D // tr),
        in_specs=[pl.BlockSpec((tr, N_DEV), lambda l, r: (r, 0)),
                  pl.BlockSpec((1, N_DEV, NC), lambda l, r: (l, 0, 0)), wspec, wspec, wspec],
        out_specs=[wspec] * 4, out_shape=[_sds(w.shape, F32)] * 4,
        compiler_params=pltpu.CompilerParams(dimension_semantics=("parallel", "parallel"), vmem_limit_bytes=VMEM_LIMIT),
    )(c_all_t, dmod_cols, w, m, v)


def _sum_devices(x):
    def body(x_ref, o_ref):
        s = x_ref[0]
        for j in range(1, N_DEV):
            s = s + x_ref[j]
        o_ref[...] = s

    return pl.pallas_call(body, name="sum_devices", out_shape=_sds(x.shape[1:], F32))(x)


def _adamw_small(w, g, m, v):
    def body(w_ref, g_ref, m_ref, v_ref, d_ref, mo_ref, vo_ref):
        d_ref[...], mo_ref[...], vo_ref[...] = _adamw(w_ref[...], g_ref[...], m_ref[...], v_ref[...])

    return pl.pallas_call(body, name="adamw_small", out_shape=[_sds(w.shape, F32)] * 3)(w, g, m, v)


def _me():
    return lax.axis_index("x") * 4 + lax.axis_index("y") * 2 + lax.axis_index("c")


def _peer(k):
    x, y, c = lax.axis_index("x"), lax.axis_index("y"), lax.axis_index("c")
    px = 1 - x if k & 4 else x
    py = 1 - y if k & 2 else y
    pc = 1 - c if k & 1 else c
    return (px, py, pc), px * 4 + py * 2 + pc


VMEM_SPEC = pl.BlockSpec(memory_space=pltpu.VMEM)
ANY_SPEC = pl.BlockSpec(memory_space=pl.ANY)
def _comm_sems(n):
    return [pltpu.SemaphoreType.DMA((n * (N_DEV - 1),)), pltpu.SemaphoreType.DMA((n * (N_DEV - 1),)),
            pltpu.SemaphoreType.DMA((n,))]


def _exchange_copies(srcs_of, dst_refs, send_sems, recv_sems, local_sems):
    me = _me()
    local, sends, recvs = [], [], []
    for a, (src_of, dst_ref) in enumerate(zip(srcs_of, dst_refs)):
        local.append(pltpu.make_async_copy(src_of(me), dst_ref.at[me], local_sems.at[a]))
        for k in range(1, N_DEV):
            dev, pj = _peer(k)
            i = a * (N_DEV - 1) + k - 1
            sems = dict(send_sem=send_sems.at[i], recv_sem=recv_sems.at[i], device_id=dev, device_id_type=MESH_IDS)
            sends.append(pltpu.make_async_remote_copy(src_ref=src_of(pj), dst_ref=dst_ref.at[me], **sems))
            recvs.append(pltpu.make_async_remote_copy(src_ref=src_of(pj), dst_ref=dst_ref.at[pj], **sems))
    return local, sends, recvs


def _start_exchange(copies):
    local, sends, _ = copies
    for cp in local + sends:
        cp.start()


def _finish_exchange(copies):
    local, sends, recvs = copies
    for cp in recvs:
        cp.wait_recv()
    for cp in sends:
        cp.wait_send()
    for cp in local:
        cp.wait()


def _sum_adamw(recv, w, m, v):
    shape = w.shape
    C = shape[-1]
    R = w.size // C
    rows = max(d for d in range(16, min(R, 512) + 1, 16) if R % d == 0 and d * C <= 256 * 1024)

    def body(r_ref, w_ref, m_ref, v_ref, go_ref, d_ref, mo_ref, vo_ref):
        g = r_ref[0].astype(F32)
        for j in range(1, N_DEV):
            g = g + r_ref[j].astype(F32)
        go_ref[...] = g
        d_ref[...], mo_ref[...], vo_ref[...] = _adamw(w_ref[...], g, m_ref[...], v_ref[...])

    spec = pl.BlockSpec((rows, C), lambda i: (i, 0))
    outs = pl.pallas_call(
        body, name="sum_adamw", grid=(R // rows,),
        in_specs=[pl.BlockSpec((N_DEV, rows, C), lambda i: (0, i, 0)), spec, spec, spec],
        out_specs=[spec] * 4, out_shape=[_sds((R, C), F32)] * 4,
        compiler_params=pltpu.CompilerParams(dimension_semantics=("parallel",), vmem_limit_bytes=VMEM_LIMIT),
    )(recv.reshape(N_DEV, R, C), w.reshape(R, C), m.reshape(R, C), v.reshape(R, C))
    return [o.reshape(shape) for o in outs]


def _all_gather(name, x, out_dtype):
    R, C = x.shape
    cast = out_dtype != x.dtype

    def body(x_ref, out_ref, buf, send_sems, recv_sems, local_sem):
        me = _me()
        if cast:
            buf[...] = x_ref[...].astype(out_dtype)
            src = buf
        else:
            src = x_ref
        local = pltpu.make_async_copy(src, out_ref.at[me], local_sem)
        local.start()
        sends = []
        for k in range(1, N_DEV):
            dev, _ = _peer(k)
            cp = pltpu.make_async_remote_copy(src_ref=src, dst_ref=out_ref.at[me], send_sem=send_sems.at[k - 1],
                                              recv_sem=recv_sems.at[k - 1], device_id=dev, device_id_type=MESH_IDS)
            cp.start()
            sends.append(cp)
        for k in range(1, N_DEV):
            dev, pj = _peer(k)
            pltpu.make_async_remote_copy(src_ref=src, dst_ref=out_ref.at[pj], send_sem=send_sems.at[k - 1],
                                         recv_sem=recv_sems.at[k - 1], device_id=dev, device_id_type=MESH_IDS).wait_recv()
        for cp in sends:
            cp.wait_send()
        local.wait()

    return pl.pallas_call(
        body, name=name, in_specs=[VMEM_SPEC], out_specs=ANY_SPEC, out_shape=_sds((N_DEV, R, C), out_dtype),
        scratch_shapes=[pltpu.VMEM((R, C) if cast else (8, 128), out_dtype),
                        pltpu.SemaphoreType.DMA((N_DEV - 1,)), pltpu.SemaphoreType.DMA((N_DEV - 1,)),
                        pltpu.SemaphoreType.DMA(())],
        compiler_params=pltpu.CompilerParams(vmem_limit_bytes=VMEM_LIMIT),
    )(x)


def _all_to_all(name, x):
    _, R, C = x.shape

    def body(x_ref, out_ref, send_sems, recv_sems, local_sem):
        me = _me()
        local = pltpu.make_async_copy(x_ref.at[me], out_ref.at[me], local_sem)
        local.start()
        sends = []
        for k in range(1, N_DEV):
            dev, pj = _peer(k)
            cp = pltpu.make_async_remote_copy(src_ref=x_ref.at[pj], dst_ref=out_ref.at[me], send_sem=send_sems.at[k - 1],
                                              recv_sem=recv_sems.at[k - 1], device_id=dev, device_id_type=MESH_IDS)
            cp.start()
            sends.append(cp)
        for k in range(1, N_DEV):
            dev, pj = _peer(k)
            pltpu.make_async_remote_copy(src_ref=x_ref.at[pj], dst_ref=out_ref.at[pj], send_sem=send_sems.at[k - 1],
                                         recv_sem=recv_sems.at[k - 1], device_id=dev, device_id_type=MESH_IDS).wait_recv()
        for cp in sends:
            cp.wait_send()
        local.wait()

    return pl.pallas_call(
        body, name=name, in_specs=[VMEM_SPEC], out_specs=VMEM_SPEC, out_shape=_sds(x.shape, x.dtype),
        scratch_shapes=[pltpu.SemaphoreType.DMA((N_DEV - 1,)), pltpu.SemaphoreType.DMA((N_DEV - 1,)),
                        pltpu.SemaphoreType.DMA(())],
    )(x)


def _reduce_scatter_adamw(name, gblk, w, m, v):
    _, R, C = gblk.shape
    rows = 8
    for cand in (136, 128, 80, 64, 40, 32, 16, 8):
        if R % cand == 0:
            rows = cand
            break

    def body(g_ref, w_ref, m_ref, v_ref, go_ref, d_ref, mo_ref, vo_ref, recv, send_sems, recv_sems, local_sem):
        me = _me()
        local = pltpu.make_async_copy(g_ref.at[me], recv.at[me], local_sem)
        local.start()
        sends = []
        for k in range(1, N_DEV):
            dev, pj = _peer(k)
            cp = pltpu.make_async_remote_copy(src_ref=g_ref.at[pj], dst_ref=recv.at[me], send_sem=send_sems.at[k - 1],
                                              recv_sem=recv_sems.at[k - 1], device_id=dev, device_id_type=MESH_IDS)
            cp.start()
            sends.append(cp)
        for k in range(1, N_DEV):
            dev, pj = _peer(k)
            pltpu.make_async_remote_copy(src_ref=g_ref.at[pj], dst_ref=recv.at[pj], send_sem=send_sems.at[k - 1],
                                         recv_sem=recv_sems.at[k - 1], device_id=dev, device_id_type=MESH_IDS).wait_recv()
        local.wait()

        def chunk(i, carry):
            r = pl.ds(pl.multiple_of(i * rows, rows), rows)
            g = recv[0, r, :].astype(F32)
            for j in range(1, N_DEV):
                g = g + recv[j, r, :].astype(F32)
            go_ref[r, :] = g
            d_ref[r, :], mo_ref[r, :], vo_ref[r, :] = _adamw(w_ref[r, :], g, m_ref[r, :], v_ref[r, :])
            return carry

        lax.fori_loop(0, R // rows, chunk, 0)
        for cp in sends:
            cp.wait_send()

    return pl.pallas_call(
        body, name=name, in_specs=[ANY_SPEC, VMEM_SPEC, VMEM_SPEC, VMEM_SPEC], out_specs=[VMEM_SPEC] * 4,
        out_shape=[_sds((R, C), F32)] * 4,
        scratch_shapes=[pltpu.VMEM((N_DEV, R, C), BF16), pltpu.SemaphoreType.DMA((N_DEV - 1,)),
                        pltpu.SemaphoreType.DMA((N_DEV - 1,)), pltpu.SemaphoreType.DMA(())],
        compiler_params=pltpu.CompilerParams(vmem_limit_bytes=VMEM_LIMIT),
    )(gblk, w, m, v)


FIRST_WEIGHTS = ["mla_w_dq", "mla_w_uq", "mla_w_dkv", "mla_w_ukv"]
LATE_WEIGHTS = ["mla_w_o", "swa_w_qkv", "swa_w_o", "w_ff1", "w_ff2"]
ROW_SHARDED = {"mla_w_dq", "mla_w_dkv", "mla_w_o", "swa_w_o", "w_ff2"}


def _unblock(name, blocks):
    sh = blocks.shape[1:]
    if name in ROW_SHARDED:
        return jnp.moveaxis(blocks, 0, 1).reshape(sh[0], N_DEV * sh[1], sh[2])
    return jnp.moveaxis(blocks, 0, 2).reshape(sh[0], sh[1], N_DEV * sh[2])


def _block(name, full):
    L, K, N = full.shape
    if name in ROW_SHARDED:
        return jnp.moveaxis(full.reshape(L, N_DEV, K // N_DEV, N), 1, 0)
    return jnp.moveaxis(full.reshape(L, K, N_DEV, N // N_DEV), 2, 0)


def _rot_cols(w):
    half = QK_ROPE // 2
    return jnp.concatenate([-w[..., half:], w[..., :half]], axis=-1)


def _unrot_cols(gw):
    half = QK_ROPE // 2
    return jnp.concatenate([gw[..., half:], -gw[..., :half]], axis=-1)


def _row(v):
    return v.reshape(1, -1)


def _mlp_block_bwd(dx, sv, w1, w2, g, sc, gt):
    dy, du, dgt = _mlp_bwd_a(dx, sv["y2"], sv["rl"], gt, w2)
    dw2 = _matmul_tn("dw_ff2", sv["act"], dy, BF16)
    dw1 = _matmul_tn("dw_ff1", sv["h2"], du, BF16, column_blocks=True)
    dxo, dsh, da = _mlp_bwd_b(du, sv["x1"], dx, w1, g, sc)
    return dxo, dw1, dw2, dsh, da, dgt


def kernel(x, c, positions, w_ada, b_ada, g_mix, g_mlp, mla_w_dq, mla_g_q, mla_w_uq, mla_w_dkv, mla_g_kv, mla_w_ukv, mla_w_o, swa_w_qkv, swa_b_qkv, swa_sinks, swa_w_o, swa_b_o, w_ff1, w_ff2, g_final, loss_target, m_w_ada, m_b_ada, m_g_mix, m_g_mlp, m_mla_w_dq, m_mla_g_q, m_mla_w_uq, m_mla_w_dkv, m_mla_g_kv, m_mla_w_ukv, m_mla_w_o, m_swa_w_qkv, m_swa_b_qkv, m_swa_sinks, m_swa_w_o, m_swa_b_o, m_w_ff1, m_w_ff2, m_g_final, v_w_ada, v_b_ada, v_g_mix, v_g_mlp, v_mla_w_dq, v_mla_g_q, v_mla_w_uq, v_mla_w_dkv, v_mla_g_kv, v_mla_w_ukv, v_mla_w_o, v_swa_w_qkv, v_swa_b_qkv, v_swa_sinks, v_swa_w_o, v_swa_b_o, v_w_ff1, v_w_ff2, v_g_final):
    S, D = x.shape[1], x.shape[2]
    me = _me()
    x0 = x[0]
    target = loss_target[0]
    big_w = dict(mla_w_dq=mla_w_dq, mla_w_uq=mla_w_uq, mla_w_dkv=mla_w_dkv, mla_w_ukv=mla_w_ukv, mla_w_o=mla_w_o,
                 swa_w_qkv=swa_w_qkv, swa_w_o=swa_w_o, w_ff1=w_ff1, w_ff2=w_ff2)
    big_m = dict(mla_w_dq=m_mla_w_dq, mla_w_uq=m_mla_w_uq, mla_w_dkv=m_mla_w_dkv, mla_w_ukv=m_mla_w_ukv,
                 mla_w_o=m_mla_w_o, swa_w_qkv=m_swa_w_qkv, swa_w_o=m_swa_w_o, w_ff1=m_w_ff1, w_ff2=m_w_ff2)
    big_v = dict(mla_w_dq=v_mla_w_dq, mla_w_uq=v_mla_w_uq, mla_w_dkv=v_mla_w_dkv, mla_w_ukv=v_mla_w_ukv,
                 mla_w_o=v_mla_w_o, swa_w_qkv=v_swa_w_qkv, swa_w_o=v_swa_w_o, w_ff1=v_w_ff1, w_ff2=v_w_ff2)
    groups = {"first": FIRST_WEIGHTS}
    wrows = {n: -(-big_w[n].size // (PACK_COLS * 16)) * 16 for n in FIRST_WEIGHTS}
    offs = {g: np.concatenate([[0], np.cumsum([wrows[n] for n in names])]).astype(int) for g, names in groups.items()}

    def as_rows(n, a, lead=()):
        flat = a.reshape(lead + (-1,))
        pad = wrows[n] * PACK_COLS - flat.shape[-1]
        if pad:
            flat = jnp.pad(flat, ((0, 0),) * len(lead) + ((0, pad),))
        return flat.reshape(lead + (wrows[n], PACK_COLS))

    def pack(g, d):
        return jnp.concatenate([as_rows(n, d[n]) for n in groups[g]], axis=0)

    def pack_blocks(g, gfull):
        return jnp.concatenate([as_rows(n, _block(n, gfull[n]).astype(BF16), (N_DEV,)) for n in groups[g]], axis=1)

    def unpack(g, packed, lead=()):
        out = {}
        for i, n in enumerate(groups[g]):
            part = packed[..., int(offs[g][i]):int(offs[g][i + 1]), :].reshape(lead + (-1,))
            out[n] = part[..., :big_w[n].size].reshape(lead + big_w[n].shape)
        return out

    gathered = _all_gather("gather_weights", pack("first", big_w), BF16)
    wfull = {n: _unblock(n, b) for n, b in unpack("first", gathered, (N_DEV,)).items()}
    w_dq, w_dkv = wfull["mla_w_dq"][0], wfull["mla_w_dkv"][0]
    w_cat = jnp.concatenate([w_dq, w_dkv, _rot_cols(w_dkv[:, KV_LORA:])], axis=1)
    QL = w_dq.shape[1]
    w_uq = wfull["mla_w_uq"][0].reshape(QL, MLA_HEADS, QK_DIM)
    w_uqx = jnp.concatenate([w_uq, _rot_cols(w_uq[..., QK_NOPE:])], axis=-1).reshape(QL, MLA_HEADS * 256)
    w_ukv = wfull["mla_w_ukv"][0]

    L = w_ada.shape[0]
    NC = w_ada.shape[2]
    nbq, nbo = swa_b_qkv.shape[1], swa_b_o.shape[1]
    cpad = -(-(D + nbq + nbo) // 1024) * 1024
    cpack = jnp.pad(jnp.concatenate([c[0], swa_b_qkv[0], swa_b_o[0]]), (0, cpad - (D + nbq + nbo))).reshape(8, cpad // 8)
    call = _all_gather("gather_c", cpack, F32).reshape(N_DEV, cpad)
    c_all = call[:, :D]
    b_qkv_full = call[:, D:D + nbq].reshape(1, N_DEV * nbq)
    b_o_full = call[:, D + nbq:D + nbq + nbo].reshape(1, N_DEV * nbo)
    mod_cols = _ada_fwd(c_all, w_ada)
    mpad = -(-(L * NC) // 1024) * 1024
    mod_send = jnp.pad(jnp.moveaxis(mod_cols, 1, 0).reshape(N_DEV, L * NC), ((0, 0), (0, mpad - L * NC)))
    mod_mine = _all_to_all("exchange_mod", mod_send.reshape(N_DEV, 8, mpad // 8)).reshape(N_DEV, mpad)[:, :L * NC]
    mod = jnp.moveaxis(mod_mine.reshape(N_DEV, L, NC), 0, 1).reshape(L, N_DEV * NC) + b_ada
    mods = mod.reshape(L, 6, 1, D)

    half = QK_ROPE // 2
    inv_freq = ROPE_THETA ** (-jnp.arange(half, dtype=F32) / half)
    ang = positions[0].astype(F32)[:, None] * inv_freq
    cos = jnp.concatenate([jnp.cos(ang), jnp.cos(ang)], axis=-1)
    sin = jnp.concatenate([jnp.sin(ang), jnp.sin(ang)], axis=-1)

    T_ATT = ATT_TILE
    zero_bias = jnp.zeros((1, D), F32)

    sh1, sc1, gt1, sh2, sc2, gt2 = [mods[0, i] for i in range(6)]
    gm0, gp0 = _row(g_mix[0]), _row(g_mlp[0])
    h1, cqp, cq, ckvp, ckv, q, k, v, vt = _mla_in_fwd(x0, cos, sin, gm0, sc1, sh1, w_cat, mla_g_q, w_uqx, mla_g_kv,
                                                      w_ukv, ROW_TILE)
    o0, lse0, gathered = _mla_attn_fwd(q, k, vt, ATT_TILE_FWD, [big_w[n].astype(BF16) for n in LATE_WEIGHTS])
    wfull = {n: _unblock(n, b) for n, b in zip(LATE_WEIGHTS, gathered)}
    w_o_mla, w_qkv, w_o_swa = wfull["mla_w_o"][0], wfull["swa_w_qkv"][0], wfull["swa_w_o"][0]
    ff1, ff2 = wfull["w_ff1"], wfull["w_ff2"]
    y1, x1, h2 = _attn_out_fwd(o0, x0, w_o_mla, zero_bias, gt1, gp0, sc2, sh2)
    rl0, act0, y2, x2 = _mlp_fwd(h2, x1, ff1[0], ff2[0], gt2)
    sv0 = dict(y2=y2, rl=rl0, act=act0, h2=h2, x1=x1)

    th1, tc1, tg1, th2, tc2, tg2 = [mods[1, i] for i in range(6)]
    gm1, gp1 = _row(g_mix[1]), _row(g_mlp[1])
    h3, sq, sk, svv = _swa_in_fwd(x2, gm1, tc1, th1, w_qkv, b_qkv_full)
    o1, lse1, y3, x3, h4 = _swa_attn_fwd(sq, sk, svv, swa_sinks, x2, w_o_swa, b_o_full, tg1, gp1, tc2, th2)
    rl1, act1, y4, dx4, loss_part, dg_final = _mlp_fwd_loss(h4, x3, ff1[1], ff2[1], tg2, target, _row(g_final))
    sv1 = dict(y2=y4, rl=rl1, act=act1, h2=h4, x1=x3)

    dx3, dw1_1, dw2_1, dsh2_1, da2_1, dgt2_1 = _mlp_block_bwd(dx4, sv1, ff1[1], ff2[1], gp1, tc2, tg2)
    dsq, dsk, dsv, dsink, dy, dgt1_1, db_o = _swa_attn_bwd(sq, sk, svv, lse1, swa_sinks, dx3, y3, o1, tg1, w_o_swa)
    dw_o_swa = _matmul_tn("dw_o", o1, dy, BF16)
    dqkv, dx2, db_qkv, dsh1_1, da1_1 = _swa_in_bwd(dsq, dsk, dsv, x2, dx3, w_qkv, gm1, tc1)
    dw_qkv = _matmul_tn("dw_qkv", h3, dqkv, BF16)

    dx1, dw1_0, dw2_0, dsh2_0, da2_0, dgt2_0 = _mlp_block_bwd(dx2, sv0, ff1[0], ff2[0], gp0, sc2, gt2)
    dy, do, dl, dgt1_0, _ = _attn_out_bwd(dx1, y1, o0, gt1, w_o_mla, MLA_HEADS)
    dw_o_mla = _matmul_tn("dw_o", o0, dy, BF16)
    tb = min(T_ATT, S)
    delta = dl[:MLA_HEADS].reshape(MLA_HEADS, S // tb, 1, tb)
    glate = dict(mla_w_o=dw_o_mla[None], swa_w_qkv=dw_qkv[None], swa_w_o=dw_o_swa[None],
                 w_ff2=jnp.stack([dw2_0, dw2_1]))
    gblocks = {n: _block(n, g).astype(BF16) for n, g in glate.items()}
    gblocks["w_ff1"] = jnp.stack([dw1_0, dw1_1], axis=1)
    lse_rows = (lse0 * LOG2E).reshape(MLA_HEADS, S // tb, 1, tb)
    dq, dk, dv, recv = _mla_attn_bwd(q, k, v, do, lse_rows, delta, T_ATT, [gblocks[n] for n in LATE_WEIGHTS])
    late = {n: _sum_adamw(r, big_w[n], big_m[n], big_v[n]) for n, r in zip(LATE_WEIGHTS, recv)}
    dqx, dkv, dcat, dx0, dg_q, dg_kv, dsh1_0, da1_0 = _mla_in_bwd(
        dq, dk, dv, cos, sin, cqp, ckvp, x0, dx1, w_uqx, mla_g_q, w_ukv, mla_g_kv, w_cat, gm0, sc1)
    dw_uqx = _matmul_tn("dw_uq", cq, dqx).reshape(QL, MLA_HEADS, 256)
    dw_ukv = _matmul_tn("dw_ukv", ckv, dkv)
    dw_cat = _matmul_tn("dw_down", h1, dcat)
    dw_uq = jnp.concatenate([dw_uqx[..., :QK_NOPE], dw_uqx[..., 128:192] + _unrot_cols(dw_uqx[..., 192:256])],
                            axis=-1).reshape(QL, MLA_HEADS * QK_DIM)
    o_kr = QL + KV_LORA
    dw_dkv = jnp.concatenate([dw_cat[:, QL:o_kr],
                              dw_cat[:, o_kr:o_kr + QK_ROPE] + _unrot_cols(dw_cat[:, o_kr + QK_ROPE:])], axis=1)

    gfirst = dict(mla_w_dq=dw_cat[None, :, :QL], mla_w_uq=dw_uq[None], mla_w_dkv=dw_dkv[None], mla_w_ukv=dw_ukv[None])
    first = _reduce_scatter_adamw("grad_exchange_adamw", pack_blocks("first", gfirst), pack("first", big_w),
                                  pack("first", big_m), pack("first", big_v))
    big_g, big_d, big_nm, big_nv = ({**unpack("first", first[j]), **{n: late[n][j] for n in LATE_WEIGHTS}}
                                    for j in range(4))

    dmod = jnp.stack([
        jnp.concatenate([dsh1_0, gm0 * da1_0, dgt1_0, dsh2_0, gp0 * da2_0, dgt2_0], axis=1),
        jnp.concatenate([dsh1_1, gm1 * da1_1, dgt1_1, dsh2_1, gp1 * da2_1, dgt2_1], axis=1)]).reshape(-1)
    dg_mix = jnp.concatenate([(1.0 + sc1) * da1_0, (1.0 + tc1) * da1_1], axis=1).reshape(-1)
    dg_mlp = jnp.concatenate([(1.0 + sc2) * da2_0, (1.0 + tc2) * da2_1], axis=1).reshape(-1)
    parts = [loss_part.reshape(-1), dmod, dg_mix, dg_mlp, dg_q.reshape(-1), dg_kv.reshape(-1), dsink.reshape(-1),
             dg_final.reshape(-1), db_qkv.reshape(-1), db_o.reshape(-1)]
    soffs = np.concatenate([[0], np.cumsum([p.size for p in parts])])
    spad = -(-int(soffs[-1]) // 1024) * 1024
    spack = jnp.pad(jnp.concatenate(parts), (0, spad - int(soffs[-1]))).reshape(8, spad // 8)
    sall = _all_gather("gather_small_grads", spack, F32)
    ssum = _sum_devices(sall).reshape(-1)
    tot = [ssum[int(soffs[i]):int(soffs[i + 1])] for i in range(len(parts))]
    loss = tot[0][0]
    nsink = swa_sinks.shape[1]
    small_g = dict(b_ada=tot[1].reshape(b_ada.shape), g_mix=tot[2].reshape(g_mix.shape), g_mlp=tot[3].reshape(g_mlp.shape),
                   mla_g_q=tot[4].reshape(mla_g_q.shape), mla_g_kv=tot[5].reshape(mla_g_kv.shape),
                   swa_sinks=tot[6][:nsink].reshape(swa_sinks.shape), g_final=tot[7].reshape(g_final.shape),
                   swa_b_qkv=lax.dynamic_slice(tot[8], (me * nbq,), (nbq,)).reshape(swa_b_qkv.shape),
                   swa_b_o=lax.dynamic_slice(tot[9], (me * nbo,), (nbo,)).reshape(swa_b_o.shape))
    small_w = dict(b_ada=b_ada, g_mix=g_mix, g_mlp=g_mlp, mla_g_q=mla_g_q, mla_g_kv=mla_g_kv, swa_sinks=swa_sinks,
                   g_final=g_final, swa_b_qkv=swa_b_qkv, swa_b_o=swa_b_o)
    small_m = dict(b_ada=m_b_ada, g_mix=m_g_mix, g_mlp=m_g_mlp, mla_g_q=m_mla_g_q, mla_g_kv=m_mla_g_kv,
                   swa_sinks=m_swa_sinks, g_final=m_g_final, swa_b_qkv=m_swa_b_qkv, swa_b_o=m_swa_b_o)
    small_v = dict(b_ada=v_b_ada, g_mix=v_g_mix, g_mlp=v_g_mlp, mla_g_q=v_mla_g_q, mla_g_kv=v_mla_g_kv,
                   swa_sinks=v_swa_sinks, g_final=v_g_final, swa_b_qkv=v_swa_b_qkv, swa_b_o=v_swa_b_o)
    SMALL = list(small_w)
    woffs = np.concatenate([[0], np.cumsum([small_w[n].size for n in SMALL])])
    wpad = -(-int(woffs[-1]) // 1024) * 1024

    def spack_of(d):
        flat = jnp.concatenate([d[n].reshape(-1) for n in SMALL])
        return jnp.pad(flat, (0, wpad - int(woffs[-1]))).reshape(8, wpad // 8)

    sm = _adamw_small(spack_of(small_w), spack_of(small_g), spack_of(small_m), spack_of(small_v))
    small_d, small_nm, small_nv = (
        {n: a.reshape(-1)[int(woffs[i]):int(woffs[i + 1])].reshape(small_w[n].shape) for i, n in enumerate(SMALL)}
        for a in sm)

    b_off = int(soffs[1])
    dmod_all = sall.reshape(N_DEV, -1)[:, b_off:b_off + L * N_DEV * NC].reshape(N_DEV, L, N_DEV * NC)
    dmod_cols = jnp.moveaxis(lax.dynamic_slice_in_dim(dmod_all, me * NC, NC, axis=2), 0, 1)
    ada_g, ada_d, ada_nm, ada_nv = _ada_bwd_adamw(c_all.T, dmod_cols, w_ada, m_w_ada, v_w_ada)

    order = ["w_ada", "b_ada", "g_mix", "g_mlp", "mla_w_dq", "mla_g_q", "mla_w_uq", "mla_w_dkv", "mla_g_kv",
             "mla_w_ukv", "mla_w_o", "swa_w_qkv", "swa_b_qkv", "swa_sinks", "swa_w_o", "swa_b_o", "w_ff1", "w_ff2", "g_final"]

    def collect(ada, big, small):
        return [ada if n == "w_ada" else (big[n] if n in big else small[n]) for n in order]

    return (loss, dx0.reshape(x.shape), *collect(ada_g, big_g, small_g), *collect(ada_d, big_d, small_d),
            *collect(ada_nm, big_nm, small_nm), *collect(ada_nv, big_nv, small_nv))
```

```python
import jax
import jax.numpy as jnp
import numpy as np
from jax import lax
from jax.experimental import pallas as pl
from jax.experimental.pallas import tpu as pltpu

F32 = jnp.float32
BF16 = jnp.bfloat16
MESH_IDS = pl.DeviceIdType.MESH
N_DEV = 8

MLA_HEADS = 8
QK_NOPE = 128
QK_ROPE = 64
QK_DIM = QK_NOPE + QK_ROPE
V_DIM = 128
KV_LORA = 256
ROPE_THETA = 10000.0
SWA_HEADS = 16
SWA_KV_HEADS = 4
SWA_GROUP = SWA_HEADS // SWA_KV_HEADS
SWA_HEAD_DIM = 64
WINDOW = 128
EPS = 1e-6
LOG2E = 1.4426950408889634

ADAM_LR = 0.001
ADAM_B1 = 0.9
ADAM_B2 = 0.999
ADAM_EPS = 1e-08
ADAM_WD = 0.01
ADAM_STEP = 10

PACK_COLS = 1024
VMEM_LIMIT = 56 << 20
ROW_TILE = 512
ROW_TILE_WIDE = 256
ROW_TILE_BWD = 512
ATT_TILE = 512
ATT_TILE_FWD = 1024
TN_TOKENS = 4096


def _dot(a, b):
    return jnp.dot(a, b, preferred_element_type=F32)


def _dot_nt(a, b):
    return lax.dot_general(a, b, (((1,), (1,)), ((), ())), preferred_element_type=F32)


def _dot_tn(a, b):
    return lax.dot_general(a, b, (((0,), (0,)), ((), ())), preferred_element_type=F32)


def _rstd(x):
    return lax.rsqrt(jnp.mean(x * x, axis=-1, keepdims=True) + EPS)


def _rms_bwd(dn, n, r):
    return r * (dn - n * jnp.mean(dn * n, axis=-1, keepdims=True))


def _modulate(x, g, sc, sh):
    r = _rstd(x)
    return ((x * r) * g) * (1.0 + sc) + sh


def _modulate_bwd(dh, x, g, sc):
    r = _rstd(x)
    n = x * r
    dsh = jnp.sum(dh, axis=0, keepdims=True)
    da = jnp.sum(dh * n, axis=0, keepdims=True)
    dx = _rms_bwd(dh * (g * (1.0 + sc)), n, r)
    return dx, dsh, da


def _acc(ref, val, i):
    @pl.when(i == 0)
    def _():
        ref[...] = val

    @pl.when(i != 0)
    def _():
        ref[...] += val


def _row_spec(shape, tm):
    nd = len(shape)
    return pl.BlockSpec(tuple(shape[:nd - 2]) + (tm, shape[-1]), lambda i: (0,) * (nd - 2) + (i, 0))


def _resident_spec(shape, single_buffer):
    nd = len(shape)
    if single_buffer:
        return pl.BlockSpec(tuple(shape), lambda i: (0,) * nd, pipeline_mode=pl.Buffered(1))
    return pl.BlockSpec(tuple(shape), lambda i: (0,) * nd)


def _rowcall(name, body, tokens, tm, row_in, full_in, row_out, acc_out=()):
    tm = min(tm, tokens)
    in_specs = [_row_spec(a.shape, tm) for a in row_in] + [_resident_spec(a.shape, True) for a in full_in]
    row_specs = [s[1] if isinstance(s, tuple) else _row_spec(s.shape, tm) for s in row_out]
    row_out = [s[0] if isinstance(s, tuple) else s for s in row_out]
    out_specs = row_specs + [_resident_spec(s.shape, False) for s in acc_out]
    return pl.pallas_call(
        body, name=name, grid=(tokens // tm,), in_specs=in_specs, out_specs=out_specs,
        out_shape=list(row_out) + list(acc_out),
        compiler_params=pltpu.CompilerParams(dimension_semantics=("arbitrary",), vmem_limit_bytes=VMEM_LIMIT),
    )(*row_in, *full_in)


def _sds(shape, dtype):
    return jax.ShapeDtypeStruct(tuple(shape), dtype)


def _mla_in_fwd(x, cos, sin, g, sc, sh, w_cat, g_q, w_uqx, g_kv, w_ukv, t):
    S, D = x.shape
    QL = g_q.shape[1]
    H = MLA_HEADS
    t = min(t, S)

    def body(x_ref, cos_ref, sin_ref, g_ref, sc_ref, sh_ref, wcat_ref, gq_ref, wuqx_ref, gkv_ref, wukv_ref,
             h_ref, cqp_ref, cq_ref, ckvp_ref, ckv_ref, q_ref, k_ref, v_ref, vt_ref):
        cs, sn = cos_ref[...], sin_ref[...]
        hb = _modulate(x_ref[...], g_ref[...], sc_ref[...], sh_ref[...]).astype(BF16)
        h_ref[...] = hb
        low = _dot(hb, wcat_ref[...])
        cqp = low[:, :QL]
        cqp_ref[...] = cqp
        cq = ((cqp * _rstd(cqp)) * gq_ref[...]).astype(BF16)
        cq_ref[...] = cq
        ckvp = low[:, QL:QL + KV_LORA]
        ckvp_ref[...] = ckvp
        ckv = ((ckvp * _rstd(ckvp)) * gkv_ref[...]).astype(BF16)
        ckv_ref[...] = ckv
        o = QL + KV_LORA
        kr = (low[:, o:o + QK_ROPE] * cs + low[:, o + QK_ROPE:o + 2 * QK_ROPE] * sn).astype(BF16)
        qx = _dot(cq, wuqx_ref[...])
        kv = _dot(ckv, wukv_ref[...])
        for hd in range(H):
            b = hd * 256
            q_ref[hd, :, 0:QK_NOPE] = qx[:, b:b + QK_NOPE].astype(BF16)
            q_ref[hd, :, QK_NOPE:QK_DIM] = (qx[:, b + 128:b + 192] * cs + qx[:, b + 192:b + 256] * sn).astype(BF16)
            k_ref[hd, :, 0:QK_NOPE] = kv[:, b:b + QK_NOPE].astype(BF16)
            k_ref[hd, :, QK_NOPE:QK_DIM] = kr
            vh = kv[:, b + 128:b + 256]
            v_ref[hd] = vh.astype(BF16)
            vt_ref[hd, 0, 0:V_DIM, :] = vh.T.astype(BF16)
            vt_ref[hd, 0, V_DIM:2 * V_DIM, :] = jnp.ones((V_DIM, x_ref.shape[0]), BF16)

    vt_spec = pl.BlockSpec((H, 1, 2 * V_DIM, t), lambda i: (0, i, 0, 0))
    return _rowcall(
        "mla_in_fwd", body, S, t, [x, cos, sin], [g, sc, sh, w_cat, g_q, w_uqx, g_kv, w_ukv],
        [_sds((S, D), BF16), _sds((S, QL), F32), _sds((S, QL), BF16), _sds((S, KV_LORA), F32), _sds((S, KV_LORA), BF16),
         _sds((H, S, QK_DIM), BF16), _sds((H, S, QK_DIM), BF16), _sds((H, S, V_DIM), BF16),
         (_sds((H, S // t, 2 * V_DIM, t), BF16), vt_spec)])


def _mla_attn_fwd(q, k, vt, t, sends):
    H, S, DQ = q.shape
    DV = V_DIM
    vb = vt.shape[-1]
    t = max(min(t, S), vb)
    nb = S // t
    scale = QK_DIM ** -0.5
    c2 = scale * LOG2E

    ns = len(sends)

    def body(q_ref, k_ref, vt_ref, *rest):
        send_refs, (o_ref, lse_ref), gath_refs = rest[:ns], rest[ns:ns + 2], rest[ns + 2:2 * ns + 2]
        m_s, acc_s, s_buf, send_sems, recv_sems, local_sems = rest[2 * ns + 2:]
        hd, qi = pl.program_id(0), pl.program_id(1)

        def gather():
            return _exchange_copies([lambda j, r=r: r for r in send_refs], gath_refs, send_sems, recv_sems, local_sems)

        @pl.when((hd == 0) & (qi == 0))
        def _():
            _start_exchange(gather())

        m_s[...] = jnp.full_like(m_s, -jnp.inf)
        acc_s[...] = jnp.zeros_like(acc_s)

        def scores(j, slot):
            rows = pl.ds(pl.multiple_of(j * t, t), t)
            s_buf[slot] = _dot_nt(k_ref[0, rows, :], q_ref[0])

        nvb = t // vb

        def update(s, j, blocks, cols):
            m_prev = m_s[:, cols]
            m_new = jnp.maximum(m_prev, jnp.max(s, axis=0, keepdims=True))
            alpha = jnp.exp2((m_prev - m_new) * c2)
            pb = jnp.exp2((s - m_new) * c2).astype(BF16)
            acc = alpha * acc_s[:, cols]
            for n, u in enumerate(blocks):
                acc = acc + _dot(vt_ref[0, j * nvb + u], pb[n * vb:(n + 1) * vb, :])
            acc_s[:, cols] = acc
            m_s[:, cols] = m_new

        def causal(s, shape):
            key = lax.broadcasted_iota(jnp.int32, shape, 0)
            qry = lax.broadcasted_iota(jnp.int32, shape, 1)
            return jnp.where(key <= qry, s, -jnp.inf)

        def absorb(j, slot, diagonal):
            if not diagonal:
                update(s_buf[slot], j, range(nvb), slice(None))
            elif nvb % 2:
                update(causal(s_buf[slot], (t, t)), j, range(nvb), slice(None))
            else:
                half = t // 2
                update(causal(s_buf[slot, :half], (half, t)), j, range(nvb // 2), slice(None))
                update(causal(s_buf[slot, half:, half:], (half, half)), j, range(nvb // 2, nvb), slice(half, t))

        def pair(i, carry):
            j = 2 * i
            scores(j + 1, 1)
            absorb(j, 0, False)
            scores(j + 2, 0)
            absorb(j + 1, 1, False)
            return carry

        scores(0, 0)
        lax.fori_loop(0, qi // 2, pair, 0)

        @pl.when(qi % 2 == 0)
        def _():
            absorb(qi, 0, True)

        @pl.when(qi % 2 == 1)
        def _():
            scores(qi, 1)
            absorb(qi - 1, 0, False)
            absorb(qi, 1, True)

        acc = acc_s[...]
        o_ref[...] = (acc[:DV] / acc[DV:]).T.astype(BF16)
        lse_ref[0, 0] = m_s[...] * scale + jnp.log(acc[DV:DV + 1])

        @pl.when((hd == H - 1) & (qi == nb - 1))
        def _():
            _finish_exchange(gather())

    outs = pl.pallas_call(
        body, name="mla_attn_fwd", grid=(H, nb),
        in_specs=[pl.BlockSpec((1, t, DQ), lambda h, i: (h, i, 0)),
                  pl.BlockSpec((1, S, DQ), lambda h, i: (h, 0, 0)),
                  pl.BlockSpec((1, S // vb, 2 * DV, vb), lambda h, i: (h, 0, 0, 0))] + [ANY_SPEC] * ns,
        out_specs=[pl.BlockSpec((t, DV), lambda h, i: (i, h)),
                   pl.BlockSpec((1, 1, 1, t), lambda h, i: (h, i, 0, 0))] + [ANY_SPEC] * ns,
        out_shape=[_sds((S, H * DV), BF16), _sds((H, nb, 1, t), F32)]
        + [_sds((N_DEV,) + a.shape, a.dtype) for a in sends],
        scratch_shapes=[pltpu.VMEM((1, t), F32), pltpu.VMEM((2 * DV, t), F32), pltpu.VMEM((2, t, t), F32)]
        + _comm_sems(ns),
        compiler_params=pltpu.CompilerParams(dimension_semantics=("arbitrary", "arbitrary"),
                                             vmem_limit_bytes=VMEM_LIMIT),
    )(q, k, vt, *sends)
    return outs[0], outs[1], outs[2:]


def _attn_out_fwd(o, x, w_o, b_o, gt, g, sc, sh):
    S, D = x.shape

    def body(o_ref, x_ref, wo_ref, bo_ref, gt_ref, g_ref, sc_ref, sh_ref, y_ref, x1_ref, h_ref):
        y = _dot(o_ref[...], wo_ref[...]) + bo_ref[...]
        y_ref[...] = y.astype(BF16)
        x1 = x_ref[...] + gt_ref[...] * y
        x1_ref[...] = x1
        h_ref[...] = _modulate(x1, g_ref[...], sc_ref[...], sh_ref[...]).astype(BF16)

    return _rowcall("attn_out_fwd", body, S, ROW_TILE, [o, x], [w_o, b_o, gt, g, sc, sh],
                    [_sds((S, D), BF16), _sds((S, D), F32), _sds((S, D), BF16)])


def _mlp_fwd(h, x, w1, w2, gt):
    S, D = x.shape
    FF = w1.shape[1]

    def body(h_ref, x_ref, w1_ref, w2_ref, gt_ref, rl_ref, act_ref, y_ref, x2_ref):
        rl = jnp.maximum(_dot(h_ref[...], w1_ref[...]), 0.0)
        rl_ref[...] = rl.astype(BF16)
        act = (rl * rl).astype(BF16)
        act_ref[...] = act
        y = _dot(act, w2_ref[...])
        y_ref[...] = y.astype(BF16)
        x2_ref[...] = x_ref[...] + gt_ref[...] * y

    return _rowcall("mlp_fwd", body, S, ROW_TILE_WIDE, [h, x], [w1, w2, gt],
                    [_sds((S, FF), BF16), _sds((S, FF), BF16), _sds((S, D), BF16), _sds((S, D), F32)])


def _final_norm_loss(xv, target, g, d_model):
    r = _rstd(xv)
    n = xv * r
    err = n * g - target
    part = 0.5 * jnp.sum(jnp.mean(err * err, axis=-1, keepdims=True), axis=0, keepdims=True)
    dout = err / d_model
    return part, _rms_bwd(dout * g, n, r), jnp.sum(dout * n, axis=0, keepdims=True)


def _mlp_fwd_loss(h, x, w1, w2, gt, target, g_final):
    S, D = x.shape
    FF = w1.shape[1]

    def body(h_ref, x_ref, t_ref, w1_ref, w2_ref, gt_ref, g_ref, rl_ref, act_ref, y_ref, dx_ref, loss_ref, dg_ref):
        i = pl.program_id(0)
        rl = jnp.maximum(_dot(h_ref[...], w1_ref[...]), 0.0)
        rl_ref[...] = rl.astype(BF16)
        act = (rl * rl).astype(BF16)
        act_ref[...] = act
        y = _dot(act, w2_ref[...])
        y_ref[...] = y.astype(BF16)
        part, dx, dg = _final_norm_loss(x_ref[...] + gt_ref[...] * y, t_ref[...], g_ref[...], D)
        dx_ref[...] = dx
        _acc(loss_ref, jnp.broadcast_to(part, loss_ref.shape), i)
        _acc(dg_ref, dg, i)

    return _rowcall("mlp_fwd_loss", body, S, ROW_TILE_WIDE, [h, x, target], [w1, w2, gt, g_final],
                    [_sds((S, FF), BF16), _sds((S, FF), BF16), _sds((S, D), BF16), _sds((S, D), F32)],
                    [_sds((1, 128), F32), _sds((1, D), F32)])


def _swa_in_fwd(x, g, sc, sh, w_qkv, b_qkv):
    S, D = x.shape
    NQ = SWA_HEADS * SWA_HEAD_DIM
    NK = SWA_KV_HEADS * SWA_HEAD_DIM

    def body(x_ref, g_ref, sc_ref, sh_ref, w_ref, b_ref, h_ref, q_ref, k_ref, v_ref):
        hb = _modulate(x_ref[...], g_ref[...], sc_ref[...], sh_ref[...]).astype(BF16)
        h_ref[...] = hb
        qkv = _dot(hb, w_ref[...]) + b_ref[...]
        q_ref[...] = qkv[:, :NQ].astype(BF16)
        k_ref[...] = qkv[:, NQ:NQ + NK].astype(BF16)
        v_ref[...] = qkv[:, NQ + NK:].astype(BF16)

    return _rowcall("swa_in_fwd", body, S, ROW_TILE, [x], [g, sc, sh, w_qkv, b_qkv],
                    [_sds((S, D), BF16), _sds((S, NQ), BF16), _sds((S, NK), BF16), _sds((S, NK), BF16)])


def _alibi_slope(head):
    return float(np.float32(2.0 ** (-8.0 * (head + 1) / SWA_HEADS)))


def _swa_geometry(n):
    W, G = WINDOW, SWA_GROUP
    key = lax.broadcasted_iota(jnp.int32, (2 * W, G * W), 0)
    qry = lax.broadcasted_iota(jnp.int32, (2 * W, G * W), 1) & (W - 1)
    dist = W + qry - key
    valid = (dist >= 0) & (dist < W) & ((n > 0) | (key >= W))
    return dist.astype(F32), valid


def _swa_group(kh, q_ref, sink_ref):
    W, G, Dh = WINDOW, SWA_GROUP, SWA_HEAD_DIM
    heads = [kh * G + g for g in range(G)]
    q4 = jnp.concatenate([q_ref[:, h * Dh:(h + 1) * Dh] for h in heads], axis=0)
    slopes = jnp.concatenate([jnp.full((1, W), _alibi_slope(h), F32) for h in heads], axis=1)
    sinks = jnp.concatenate([jnp.broadcast_to(sink_ref[:, h:h + 1], (1, W)) for h in heads], axis=1)
    return heads, q4, slopes, sinks


def _swa_band_specs(W, nb, cols):
    prev = pl.BlockSpec((W, cols), lambda n: (jnp.maximum(jnp.minimum(n, nb - 1) - 1, 0), 0))
    cur = pl.BlockSpec((W, cols), lambda n: (jnp.minimum(n, nb - 1), 0))
    return prev, cur


def _swa_attn_fwd(q, k, v, sinks, x, w_o, b_o, gt, g, sc, sh):
    S, NQ = q.shape
    NK = k.shape[1]
    D = x.shape[1]
    W, Dh, G = WINDOW, SWA_HEAD_DIM, SWA_GROUP
    nb = S // W

    def body(q_ref, kp_ref, kc_ref, vp_ref, vc_ref, sink_ref, x_ref, wo_ref, bo_ref, gt_ref, g_ref, sc_ref, sh_ref,
             o_ref, lse_ref, y_ref, x1_ref, h_ref):
        distf, valid = _swa_geometry(pl.program_id(0))
        kband = jnp.concatenate([kp_ref[...], kc_ref[...]], axis=0)
        vband_t = jnp.concatenate([vp_ref[...], vc_ref[...]], axis=0).astype(F32).T.astype(BF16)
        outs = []
        for kh in range(SWA_KV_HEADS):
            kb = kband[:, kh * Dh:(kh + 1) * Dh]
            vbt = vband_t[kh * Dh:(kh + 1) * Dh, :]
            heads, q4, slopes, sinks = _swa_group(kh, q_ref, sink_ref)
            s = _dot_nt(kb, q4) * (Dh ** -0.5) - slopes * distf
            s = jnp.where(valid, s, -jnp.inf)
            m = jnp.maximum(jnp.max(s, axis=0, keepdims=True), sinks)
            p = jnp.exp(s - m)
            denom = jnp.sum(p, axis=0, keepdims=True) + jnp.exp(sinks - m)
            out4 = _dot(vbt, (p * (1.0 / denom)).astype(BF16))
            lse4 = m + jnp.log(denom)
            for g, h in enumerate(heads):
                outs.append(out4[:, g * W:(g + 1) * W])
                lse_ref[h:h + 1, :] = lse4[:, g * W:(g + 1) * W]
        ob = jnp.concatenate(outs, axis=0).T.astype(BF16)
        o_ref[...] = ob
        y = _dot(ob, wo_ref[...]) + bo_ref[...]
        y_ref[...] = y.astype(BF16)
        x1 = x_ref[...] + gt_ref[...] * y
        x1_ref[...] = x1
        h_ref[...] = _modulate(x1, g_ref[...], sc_ref[...], sh_ref[...]).astype(BF16)

    kprev, kcur = _swa_band_specs(W, nb, NK)
    blk = lambda cols: pl.BlockSpec((W, cols), lambda n: (n, 0))
    row = pl.BlockSpec((1, D), lambda n: (0, 0))
    return pl.pallas_call(
        body, name="swa_attn_fwd", grid=(nb,),
        in_specs=[blk(NQ), kprev, kcur, kprev, kcur, pl.BlockSpec((1, SWA_HEADS), lambda n: (0, 0)), blk(D),
                  pl.BlockSpec(w_o.shape, lambda n: (0, 0), pipeline_mode=pl.Buffered(1)), row, row, row, row, row],
        out_specs=[blk(NQ), pl.BlockSpec((SWA_HEADS, W), lambda n: (0, n)), blk(D), blk(D), blk(D)],
        out_shape=[_sds((S, NQ), BF16), _sds((SWA_HEADS, S), F32), _sds((S, D), BF16), _sds((S, D), F32),
                   _sds((S, D), BF16)],
        compiler_params=pltpu.CompilerParams(dimension_semantics=("arbitrary",), vmem_limit_bytes=VMEM_LIMIT),
    )(q, k, k, v, v, sinks, x, w_o, b_o, gt, g, sc, sh)


def _mlp_bwd_a(dx, y, rl, gt, w2):
    S, D = dx.shape
    FF = rl.shape[1]

    def body(dx_ref, y_ref, rl_ref, gt_ref, w2_ref, dy_ref, du_ref, dgt_ref):
        i = pl.program_id(0)
        dxv = dx_ref[...]
        _acc(dgt_ref, jnp.sum(dxv * y_ref[...].astype(F32), axis=0, keepdims=True), i)
        dy = (dxv * gt_ref[...]).astype(BF16)
        dy_ref[...] = dy
        dact = _dot_nt(dy, w2_ref[...])
        du_ref[...] = (dact * (2.0 * rl_ref[...].astype(F32))).astype(BF16)

    return _rowcall("mlp_bwd_a", body, S, ROW_TILE_BWD, [dx, y, rl], [gt, w2],
                    [_sds((S, D), BF16), _sds((S, FF), BF16)], [_sds((1, D), F32)])


def _mlp_bwd_b(du, x, dx, w1, g, sc):
    S, D = x.shape

    def body(du_ref, x_ref, dx_ref, w1_ref, g_ref, sc_ref, dxo_ref, dsh_ref, da_ref):
        i = pl.program_id(0)
        dh = _dot_nt(du_ref[...], w1_ref[...])
        dxn, dsh, da = _modulate_bwd(dh, x_ref[...], g_ref[...], sc_ref[...])
        dxo_ref[...] = dx_ref[...] + dxn
        _acc(dsh_ref, dsh, i)
        _acc(da_ref, da, i)

    return _rowcall("mlp_bwd_b", body, S, ROW_TILE_BWD, [du, x, dx], [w1, g, sc],
                    [_sds((S, D), F32)], [_sds((1, D), F32), _sds((1, D), F32)])


def _attn_out_bwd(dx, y, o, gt, w_o, n_heads):
    S, D = dx.shape
    NO = o.shape[1]
    dh = NO // n_heads
    member = (jnp.arange(NO)[None, :] // dh == jnp.arange(16)[:, None]).astype(BF16)

    def body(dx_ref, y_ref, o_ref, gt_ref, wo_ref, mem_ref, dy_ref, do_ref, dl_ref, dgt_ref, dbo_ref):
        i = pl.program_id(0)
        dxv = dx_ref[...]
        _acc(dgt_ref, jnp.sum(dxv * y_ref[...].astype(F32), axis=0, keepdims=True), i)
        dy = dxv * gt_ref[...]
        _acc(dbo_ref, jnp.sum(dy, axis=0, keepdims=True), i)
        dyb = dy.astype(BF16)
        dy_ref[...] = dyb
        do = _dot_nt(dyb, wo_ref[...])
        do_ref[...] = do.astype(BF16)
        prod = do * o_ref[...].astype(F32)
        hi = prod.astype(BF16)
        lo = (prod - hi.astype(F32)).astype(BF16)
        dl_ref[...] = _dot_nt(mem_ref[...], hi) + _dot_nt(mem_ref[...], lo)

    tm = min(ROW_TILE, S)
    return _rowcall("attn_out_bwd", body, S, ROW_TILE, [dx, y, o], [gt, w_o, member],
                    [_sds((S, D), BF16), _sds((S, NO), BF16),
                     (_sds((16, S), F32), pl.BlockSpec((16, tm), lambda i: (0, i)))],
                    [_sds((1, D), F32), _sds((1, D), F32)])


def _mla_attn_bwd(q, k, v, do, lse, delta, t, gblks):
    H, S, DQ = q.shape
    DV = V_DIM
    t = min(t, S // 2)
    tk = 2 * t
    nq, nk = S // t, S // tk
    scale = QK_DIM ** -0.5
    c2 = scale * LOG2E

    ng = len(gblks)

    def body(q_ref, k_ref, v_ref, do_ref, lse_ref, dl_ref, *rest):
        g_refs, (dq_ref, dk_ref, dv_ref), recv_refs = rest[:ng], rest[ng:ng + 3], rest[ng + 3:2 * ng + 3]
        dk_s, dv_s, s_buf, dp_buf, send_sems, recv_sems, local_sems = rest[2 * ng + 3:]
        hd, kj = pl.program_id(0), pl.program_id(1)

        def scatter():
            return _exchange_copies([lambda j, r=r: r.at[j] for r in g_refs], recv_refs, send_sems, recv_sems,
                                    local_sems)

        @pl.when((hd == 0) & (kj == 0))
        def _():
            _start_exchange(scatter())

        @pl.when(kj == 0)
        def _():
            dq_ref[...] = jnp.zeros_like(dq_ref)

        dk_s[...] = jnp.zeros_like(dk_s)
        dv_s[...] = jnp.zeros_like(dv_s)

        def products(i, slot, keys=tk):
            rows = pl.ds(pl.multiple_of(i * t, t), t)
            s_buf[slot, :keys] = _dot_nt(k_ref[0, :keys], q_ref[0, rows, :])
            dp_buf[slot, :keys] = _dot_nt(v_ref[0, :keys], do_ref[rows, :])

        def absorb(i, slot, diagonal, keys=tk):
            rows = pl.ds(pl.multiple_of(i * t, t), t)
            qb, dob = q_ref[0, rows, :], do_ref[rows, :]
            p = jnp.exp2(s_buf[slot, :keys] * c2 - lse_ref[0, i])
            if diagonal is not None:
                key = lax.broadcasted_iota(jnp.int32, (keys, t), 0)
                qry = lax.broadcasted_iota(jnp.int32, (keys, t), 1) + diagonal * t
                p = jnp.where(key <= qry, p, 0.0)
            dv_s[:keys] += _dot(p.astype(BF16), dob)
            ds = (p * (dp_buf[slot, :keys] - dl_ref[0, i])).astype(BF16)
            dk_s[:keys] += _dot(ds, qb)
            dq_ref[0, rows, :] += _dot_tn(ds, k_ref[0, :keys])

        first = 2 * kj + 2
        n_off = nq - first

        def pair(i, carry):
            u = 2 * i
            products(first + u + 1, 1)
            absorb(first + u, 0, None)
            products(jnp.where(u + 2 < n_off, first + u + 2, 2 * kj + 1), 0)
            absorb(first + u + 1, 1, None)
            return carry

        products(jnp.where(n_off > 0, first, 2 * kj + 1), 0)
        lax.fori_loop(0, n_off // 2, pair, 0)
        products(2 * kj, 1, t)
        absorb(2 * kj + 1, 0, 1)
        absorb(2 * kj, 1, 0, t)

        dk_ref[0] = (dk_s[...] * scale).astype(BF16)
        dv_ref[0] = dv_s[...].astype(BF16)

        @pl.when((hd == H - 1) & (kj == nk - 1))
        def _():
            _finish_exchange(scatter())

    rowspec = pl.BlockSpec((1, nq, 1, t), lambda h, j: (h, 0, 0, 0))
    outs = pl.pallas_call(
        body, name="mla_attn_bwd", grid=(H, nk),
        in_specs=[pl.BlockSpec((1, S, DQ), lambda h, j: (h, 0, 0)),
                  pl.BlockSpec((1, tk, DQ), lambda h, j: (h, j, 0)),
                  pl.BlockSpec((1, tk, DV), lambda h, j: (h, j, 0)),
                  pl.BlockSpec((S, DV), lambda h, j: (0, h)), rowspec, rowspec] + [ANY_SPEC] * ng,
        out_specs=[pl.BlockSpec((1, S, DQ), lambda h, j: (h, 0, 0)),
                   pl.BlockSpec((1, tk, DQ), lambda h, j: (h, j, 0)),
                   pl.BlockSpec((1, tk, DV), lambda h, j: (h, j, 0))] + [ANY_SPEC] * ng,
        out_shape=[_sds((H, S, DQ), F32), _sds((H, S, DQ), BF16), _sds((H, S, DV), BF16)]
        + [_sds(g.shape, g.dtype) for g in gblks],
        scratch_shapes=[pltpu.VMEM((tk, DQ), F32), pltpu.VMEM((tk, DV), F32), pltpu.VMEM((2, tk, t), F32),
                        pltpu.VMEM((2, tk, t), F32)] + _comm_sems(ng),
        compiler_params=pltpu.CompilerParams(dimension_semantics=("arbitrary", "arbitrary"),
                                             vmem_limit_bytes=VMEM_LIMIT),
    )(q, k, v, do, lse, delta, *gblks)
    return outs[0], outs[1], outs[2], outs[3:]


def _swa_attn_bwd(q, k, v, lse, sinks, dx, y, o, gt, w_o):
    S, NQ = q.shape
    NK = k.shape[1]
    D = dx.shape[1]
    W, Dh, G = WINDOW, SWA_HEAD_DIM, SWA_GROUP
    nb = S // W
    member = (jnp.arange(NQ)[None, :] // Dh == jnp.arange(SWA_HEADS)[:, None]).astype(BF16)

    def body(q_ref, kp_ref, kc_ref, vp_ref, vc_ref, lse_ref, sink_ref, dx_ref, y_ref, o_ref, gt_ref, wo_ref, mem_ref,
             dq_ref, dk_ref, dv_ref, dsink_ref, dy_ref, dgt_ref, dbo_ref, dkc_s, dvc_s):
        n = pl.program_id(0)

        @pl.when(n == 0)
        def _():
            dkc_s[...] = jnp.zeros_like(dkc_s)
            dvc_s[...] = jnp.zeros_like(dvc_s)
            dsink_ref[...] = jnp.zeros_like(dsink_ref)

        @pl.when(n < nb)
        def _():
            dxv = dx_ref[...]
            _acc(dgt_ref, jnp.sum(dxv * y_ref[...].astype(F32), axis=0, keepdims=True), n)
            dy = dxv * gt_ref[...]
            _acc(dbo_ref, jnp.sum(dy, axis=0, keepdims=True), n)
            dyb = dy.astype(BF16)
            dy_ref[...] = dyb
            do = _dot_nt(dyb, wo_ref[...])
            dob = do.astype(BF16)
            prod = do * o_ref[...].astype(F32)
            hi = prod.astype(BF16)
            lo = (prod - hi.astype(F32)).astype(BF16)
            dl = _dot_nt(mem_ref[...], hi) + _dot_nt(mem_ref[...], lo)
            distf, valid = _swa_geometry(n)
            kband = jnp.concatenate([kp_ref[...], kc_ref[...]], axis=0)
            vband = jnp.concatenate([vp_ref[...], vc_ref[...]], axis=0)
            kband_t = kband.astype(F32).T.astype(BF16)
            dq_t = []
            for kh in range(SWA_KV_HEADS):
                ck = slice(kh * Dh, (kh + 1) * Dh)
                kb, vb, kbt = kband[:, ck], vband[:, ck], kband_t[ck, :]
                heads, q4, slopes, sinks = _swa_group(kh, q_ref, sink_ref)
                do4 = jnp.concatenate([dob[:, h * Dh:(h + 1) * Dh] for h in heads], axis=0)
                lse4 = jnp.concatenate([lse_ref[h:h + 1, :] for h in heads], axis=1)
                dl4 = jnp.concatenate([dl[h:h + 1, :] for h in heads], axis=1)
                s = _dot_nt(kb, q4) * (Dh ** -0.5) - slopes * distf
                p = jnp.where(valid, jnp.exp(s - lse4), 0.0)
                dvb = _dot(p.astype(BF16), do4)
                dp = _dot_nt(vb, do4)
                dsb = ((p * (dp - dl4)) * (Dh ** -0.5)).astype(BF16)
                dq4 = _dot(kbt, dsb)
                dkb = _dot(dsb, q4)
                dsk4 = jnp.exp(sinks - lse4) * dl4
                for g, h in enumerate(heads):
                    dq_t.append(dq4[:, g * W:(g + 1) * W])
                    dsink_ref[:, h:h + 1] += -jnp.sum(dsk4[:, g * W:(g + 1) * W], axis=1, keepdims=True)
                dk_ref[:, ck] = (dkc_s[:, ck] + dkb[:W]).astype(BF16)
                dv_ref[:, ck] = (dvc_s[:, ck] + dvb[:W]).astype(BF16)
                dkc_s[:, ck] = dkb[W:]
                dvc_s[:, ck] = dvb[W:]
            dq_ref[...] = jnp.concatenate(dq_t, axis=0).T.astype(BF16)

        @pl.when(n == nb)
        def _():
            dk_ref[...] = dkc_s[...].astype(BF16)
            dv_ref[...] = dvc_s[...].astype(BF16)

    kprev, kcur = _swa_band_specs(W, nb, NK)
    qspec = lambda cols: pl.BlockSpec((W, cols), lambda n: (jnp.minimum(n, nb - 1), 0))
    kvout = pl.BlockSpec((W, NK), lambda n: (jnp.maximum(n - 1, 0), 0))
    rowspec = pl.BlockSpec((SWA_HEADS, W), lambda n: (0, jnp.minimum(n, nb - 1)))
    fixed = lambda shape: pl.BlockSpec(shape, lambda n: (0, 0))
    return pl.pallas_call(
        body, name="swa_attn_bwd", grid=(nb + 1,),
        in_specs=[qspec(NQ), kprev, kcur, kprev, kcur, rowspec, pl.BlockSpec((1, SWA_HEADS), lambda n: (0, 0)),
                  qspec(D), qspec(D), qspec(NQ), fixed((1, D)), fixed(w_o.shape), fixed(member.shape)],
        out_specs=[qspec(NQ), kvout, kvout, fixed((1, 128)), qspec(D), fixed((1, D)), fixed((1, D))],
        out_shape=[_sds((S, NQ), BF16), _sds((S, NK), BF16), _sds((S, NK), BF16), _sds((1, 128), F32),
                   _sds((S, D), BF16), _sds((1, D), F32), _sds((1, D), F32)],
        scratch_shapes=[pltpu.VMEM((W, NK), F32), pltpu.VMEM((W, NK), F32)],
        compiler_params=pltpu.CompilerParams(dimension_semantics=("arbitrary",), vmem_limit_bytes=VMEM_LIMIT),
    )(q, k, k, v, v, lse, sinks, dx, y, o, gt, w_o, member)


def _swa_in_bwd(dq, dk, dv, x, dx, w_qkv, g, sc):
    S, D = x.shape
    N = w_qkv.shape[1]

    def body(dq_ref, dk_ref, dv_ref, x_ref, dx_ref, w_ref, g_ref, sc_ref, dqkv_ref, dxo_ref, db_ref, dsh_ref, da_ref):
        i = pl.program_id(0)
        dqkv = jnp.concatenate([dq_ref[...], dk_ref[...], dv_ref[...]], axis=1)
        dqkv_ref[...] = dqkv
        _acc(db_ref, jnp.sum(dqkv.astype(F32), axis=0, keepdims=True), i)
        dh = _dot_nt(dqkv, w_ref[...])
        dxn, dsh, da = _modulate_bwd(dh, x_ref[...], g_ref[...], sc_ref[...])
        dxo_ref[...] = dx_ref[...] + dxn
        _acc(dsh_ref, dsh, i)
        _acc(da_ref, da, i)

    return _rowcall("swa_in_bwd", body, S, ROW_TILE, [dq, dk, dv, x, dx], [w_qkv, g, sc],
                    [_sds((S, N), BF16), _sds((S, D), F32)],
                    [_sds((1, N), F32), _sds((1, D), F32), _sds((1, D), F32)])


def _mla_in_bwd(dq, dk, dv, cos, sin, cqp, ckvp, x, dx, w_uqx, g_q, w_ukv, g_kv, w_cat, g, sc):
    S, D = x.shape
    H = MLA_HEADS
    QL = g_q.shape[1]
    NX = w_uqx.shape[1]
    NC = w_cat.shape[1]

    def body(dq_ref, dk_ref, dv_ref, cos_ref, sin_ref, cqp_ref, ckvp_ref, x_ref, dx_ref,
             wuqx_ref, gq_ref, wukv_ref, gkv_ref, wcat_ref, g_ref, sc_ref,
             dqx_ref, dkv_ref, dcat_ref, dxo_ref, dgq_ref, dgkv_ref, dsh_ref, da_ref):
        i = pl.program_id(0)
        cs, sn = cos_ref[...], sin_ref[...]
        dkr = jnp.zeros(cs.shape, F32)
        for hd in range(H):
            b = hd * 256
            dqh = dq_ref[hd] * (QK_DIM ** -0.5)
            dqx_ref[:, b:b + QK_NOPE] = dqh[:, :QK_NOPE].astype(BF16)
            dqx_ref[:, b + 128:b + 192] = (dqh[:, QK_NOPE:] * cs).astype(BF16)
            dqx_ref[:, b + 192:b + 256] = (dqh[:, QK_NOPE:] * sn).astype(BF16)
            dkh = dk_ref[hd]
            dkv_ref[:, b:b + QK_NOPE] = dkh[:, :QK_NOPE]
            dkv_ref[:, b + 128:b + 256] = dv_ref[hd]
            dkr = dkr + dkh[:, QK_NOPE:].astype(F32)
        dcq = _dot_nt(dqx_ref[...], wuqx_ref[...])
        cqp = cqp_ref[...]
        rq = _rstd(cqp)
        nq = cqp * rq
        _acc(dgq_ref, jnp.sum(dcq * nq, axis=0, keepdims=True), i)
        dcqp = _rms_bwd(dcq * gq_ref[...], nq, rq)
        dckv = _dot_nt(dkv_ref[...], wukv_ref[...])
        ckvp = ckvp_ref[...]
        rk = _rstd(ckvp)
        nk = ckvp * rk
        _acc(dgkv_ref, jnp.sum(dckv * nk, axis=0, keepdims=True), i)
        dckvp = _rms_bwd(dckv * gkv_ref[...], nk, rk)
        dcat_ref[:, :QL] = dcqp.astype(BF16)
        dcat_ref[:, QL:QL + KV_LORA] = dckvp.astype(BF16)
        o = QL + KV_LORA
        dcat_ref[:, o:o + QK_ROPE] = (dkr * cs).astype(BF16)
        dcat_ref[:, o + QK_ROPE:o + 2 * QK_ROPE] = (dkr * sn).astype(BF16)
        dh = _dot_nt(dcat_ref[...], wcat_ref[...])
        dxn, dsh, da = _modulate_bwd(dh, x_ref[...], g_ref[...], sc_ref[...])
        dxo_ref[...] = dx_ref[...] + dxn
        _acc(dsh_ref, dsh, i)
        _acc(da_ref, da, i)

    return _rowcall("mla_in_bwd", body, S, ROW_TILE, [dq, dk, dv, cos, sin, cqp, ckvp, x, dx],
                    [w_uqx, g_q, w_ukv, g_kv, w_cat, g, sc],
                    [_sds((S, NX), BF16), _sds((S, NX), BF16), _sds((S, NC), BF16), _sds((S, D), F32)],
                    [_sds((1, QL), F32), _sds((1, KV_LORA), F32), _sds((1, D), F32), _sds((1, D), F32)])


def _matmul_tn(name, a, b, out_dtype=F32, column_blocks=False):
    S, K = a.shape
    N = b.shape[1]
    tk, tn, ts = min(K, 1024), min(N, 1024), min(S, TN_TOKENS)
    if column_blocks:
        tn = N // N_DEV
    if N % tn:
        tn = 512 if N % 512 == 0 else (384 if N % 384 == 0 else 128)
    if K % tk:
        tk = 512 if K % 512 == 0 else (384 if K % 384 == 0 else 128)
    ns = S // ts

    def body(a_ref, b_ref, o_ref, *scratch):
        acc_ref = scratch[0] if scratch else o_ref
        _acc(acc_ref, _dot_tn(a_ref[...], b_ref[...]), pl.program_id(2))
        if scratch:
            @pl.when(pl.program_id(2) == ns - 1)
            def _():
                o_ref[...] = acc_ref[...].astype(out_dtype)

    if column_blocks:
        out_spec = pl.BlockSpec((None, tk, tn), lambda i, j, s: (j, i, 0))
        out_shape = _sds((N_DEV, K, tn), out_dtype)
    else:
        out_spec = pl.BlockSpec((tk, tn), lambda i, j, s: (i, j))
        out_shape = _sds((K, N), out_dtype)
    return pl.pallas_call(
        body, name=name, grid=(K // tk, N // tn, ns),
        in_specs=[pl.BlockSpec((ts, tk), lambda i, j, s: (s, i)), pl.BlockSpec((ts, tn), lambda i, j, s: (s, j))],
        out_specs=out_spec, out_shape=out_shape,
        scratch_shapes=[] if out_dtype == F32 else [pltpu.VMEM((tk, tn), F32)],
        compiler_params=pltpu.CompilerParams(dimension_semantics=("parallel", "parallel", "arbitrary"),
                                             vmem_limit_bytes=VMEM_LIMIT),
    )(a, b)


def _silu(c):
    return c * jax.nn.sigmoid(c)


def _ada_fwd(c_all, w_ada):
    L, D, NC = w_ada.shape

    def body(c_ref, w_ref, o_ref):
        cond = _silu(c_ref[...]).astype(BF16)
        o_ref[0] = _dot(cond, w_ref[0].astype(BF16))

    return pl.pallas_call(
        body, name="ada_fwd", grid=(L,),
        in_specs=[pl.BlockSpec(c_all.shape, lambda l: (0, 0)), pl.BlockSpec((1, D, NC), lambda l: (l, 0, 0))],
        out_specs=pl.BlockSpec((1, N_DEV, NC), lambda l: (l, 0, 0)),
        out_shape=_sds((L, N_DEV, NC), F32),
        compiler_params=pltpu.CompilerParams(dimension_semantics=("arbitrary",), vmem_limit_bytes=VMEM_LIMIT),
    )(c_all, w_ada)


def _adamw(w, g, m, v):
    m = ADAM_B1 * m + (1.0 - ADAM_B1) * g
    v = ADAM_B2 * v + (1.0 - ADAM_B2) * (g * g)
    m_hat = m / (1.0 - ADAM_B1 ** ADAM_STEP)
    v_hat = v / (1.0 - ADAM_B2 ** ADAM_STEP)
    delta = -ADAM_LR * (m_hat / (jnp.sqrt(v_hat) + ADAM_EPS) + ADAM_WD * w)
    return delta, m, v


def _ada_bwd_adamw(c_all_t, dmod_cols, w, m, v):
    L, D, NC = w.shape
    tr = min(D, 256)

    def body(ct_ref, dm_ref, w_ref, m_ref, v_ref, g_ref, d_ref, mo_ref, vo_ref):
        cond_t = _silu(ct_ref[...])
        dm = dm_ref[0]
        g = cond_t[:, 0:1] * dm[0:1, :]
        for b in range(1, N_DEV):
            g = g + cond_t[:, b:b + 1] * dm[b:b + 1, :]
        g_ref[0] = g
        d_ref[0], mo_ref[0], vo_ref[0] = _adamw(w_ref[0], g, m_ref[0], v_ref[0])

    wspec = pl.BlockSpec((1, tr, NC), lambda l, r: (l, r, 0))
    return pl.pallas_call(
        body, name="ada_bwd_adamw", grid=(L, D // tr),
        in_specs=[pl.BlockSpec((tr, N_DEV), lambda l, r: (r, 0)),
                  pl.BlockSpec((1, N_DEV, NC), lambda l, r: (l, 0, 0)), wspec, wspec, wspec],
        out_specs=[wspec] * 4, out_shape=[_sds(w.shape, F32)] * 4,
        compiler_params=pltpu.CompilerParams(dimension_semantics=("parallel", "parallel"), vmem_limit_bytes=VMEM_LIMIT),
    )(c_all_t, dmod_cols, w, m, v)


def _sum_devices(x):
    def body(x_ref, o_ref):
        s = x_ref[0]
        for j in range(1, N_DEV):
            s = s + x_ref[j]
        o_ref[...] = s

    return pl.pallas_call(body, name="sum_devices", out_shape=_sds(x.shape[1:], F32))(x)


def _adamw_small(w, g, m, v):
    def body(w_ref, g_ref, m_ref, v_ref, d_ref, mo_ref, vo_ref):
        d_ref[...], mo_ref[...], vo_ref[...] = _adamw(w_ref[...], g_ref[...], m_ref[...], v_ref[...])

    return pl.pallas_call(body, name="adamw_small", out_shape=[_sds(w.shape, F32)] * 3)(w, g, m, v)


def _me():
    return lax.axis_index("x") * 4 + lax.axis_index("y") * 2 + lax.axis_index("c")


def _peer(k):
    x, y, c = lax.axis_index("x"), lax.axis_index("y"), lax.axis_index("c")
    px = 1 - x if k & 4 else x
    py = 1 - y if k & 2 else y
    pc = 1 - c if k & 1 else c
    return (px, py, pc), px * 4 + py * 2 + pc


VMEM_SPEC = pl.BlockSpec(memory_space=pltpu.VMEM)
ANY_SPEC = pl.BlockSpec(memory_space=pl.ANY)
def _comm_sems(n):
    return [pltpu.SemaphoreType.DMA((n * (N_DEV - 1),)), pltpu.SemaphoreType.DMA((n * (N_DEV - 1),)),
            pltpu.SemaphoreType.DMA((n,))]


def _exchange_copies(srcs_of, dst_refs, send_sems, recv_sems, local_sems):
    me = _me()
    local, sends, recvs = [], [], []
    for a, (src_of, dst_ref) in enumerate(zip(srcs_of, dst_refs)):
        local.append(pltpu.make_async_copy(src_of(me), dst_ref.at[me], local_sems.at[a]))
        for k in range(1, N_DEV):
            dev, pj = _peer(k)
            i = a * (N_DEV - 1) + k - 1
            sems = dict(send_sem=send_sems.at[i], recv_sem=recv_sems.at[i], device_id=dev, device_id_type=MESH_IDS)
            sends.append(pltpu.make_async_remote_copy(src_ref=src_of(pj), dst_ref=dst_ref.at[me], **sems))
            recvs.append(pltpu.make_async_remote_copy(src_ref=src_of(pj), dst_ref=dst_ref.at[pj], **sems))
    return local, sends, recvs


def _start_exchange(copies):
    local, sends, _ = copies
    for cp in local + sends:
        cp.start()


def _finish_exchange(copies):
    local, sends, recvs = copies
    for cp in recvs:
        cp.wait_recv()
    for cp in sends:
        cp.wait_send()
    for cp in local:
        cp.wait()


def _sum_adamw(recv, w, m, v):
    shape = w.shape
    C = shape[-1]
    R = w.size // C
    rows = max(d for d in range(16, min(R, 512) + 1, 16) if R % d == 0 and d * C <= 256 * 1024)

    def body(r_ref, w_ref, m_ref, v_ref, go_ref, d_ref, mo_ref, vo_ref):
        g = r_ref[0].astype(F32)
        for j in range(1, N_DEV):
            g = g + r_ref[j].astype(F32)
        go_ref[...] = g
        d_ref[...], mo_ref[...], vo_ref[...] = _adamw(w_ref[...], g, m_ref[...], v_ref[...])

    spec = pl.BlockSpec((rows, C), lambda i: (i, 0))
    outs = pl.pallas_call(
        body, name="sum_adamw", grid=(R // rows,),
        in_specs=[pl.BlockSpec((N_DEV, rows, C), lambda i: (0, i, 0)), spec, spec, spec],
        out_specs=[spec] * 4, out_shape=[_sds((R, C), F32)] * 4,
        compiler_params=pltpu.CompilerParams(dimension_semantics=("parallel",), vmem_limit_bytes=VMEM_LIMIT),
    )(recv.reshape(N_DEV, R, C), w.reshape(R, C), m.reshape(R, C), v.reshape(R, C))
    return [o.reshape(shape) for o in outs]


def _all_gather(name, x, out_dtype):
    R, C = x.shape
    cast = out_dtype != x.dtype

    def body(x_ref, out_ref, buf, send_sems, recv_sems, local_sem):
        me = _me()
        if cast:
            buf[...] = x_ref[...].astype(out_dtype)
            src = buf
        else:
            src = x_ref
        local = pltpu.make_async_copy(src, out_ref.at[me], local_sem)
        local.start()
        sends = []
        for k in range(1, N_DEV):
            dev, _ = _peer(k)
            cp = pltpu.make_async_remote_copy(src_ref=src, dst_ref=out_ref.at[me], send_sem=send_sems.at[k - 1],
                                              recv_sem=recv_sems.at[k - 1], device_id=dev, device_id_type=MESH_IDS)
            cp.start()
            sends.append(cp)
        for k in range(1, N_DEV):
            dev, pj = _peer(k)
            pltpu.make_async_remote_copy(src_ref=src, dst_ref=out_ref.at[pj], send_sem=send_sems.at[k - 1],
                                         recv_sem=recv_sems.at[k - 1], device_id=dev, device_id_type=MESH_IDS).wait_recv()
        for cp in sends:
            cp.wait_send()
        local.wait()

    return pl.pallas_call(
        body, name=name, in_specs=[VMEM_SPEC], out_specs=ANY_SPEC, out_shape=_sds((N_DEV, R, C), out_dtype),
        scratch_shapes=[pltpu.VMEM((R, C) if cast else (8, 128), out_dtype),
                        pltpu.SemaphoreType.DMA((N_DEV - 1,)), pltpu.SemaphoreType.DMA((N_DEV - 1,)),
                        pltpu.SemaphoreType.DMA(())],
        compiler_params=pltpu.CompilerParams(vmem_limit_bytes=VMEM_LIMIT),
    )(x)


def _all_gather_two_level(name, x, out_dtype):
    R, C = x.shape

    def body(x_ref, out_ref, buf, send_sems, recv_sems, local_sem):
        x_, y_, c_ = lax.axis_index("x"), lax.axis_index("y"), lax.axis_index("c")
        me, sibling = (x_, y_, c_), (x_, y_, 1 - c_)
        chips = [(1 - x_, y_), (x_, 1 - y_), (1 - x_, 1 - y_)]
        buf[...] = x_ref[...].astype(out_dtype)

        def slot(px, py, pc):
            return out_ref.at[4 * px + 2 * py + pc]

        def copy(k, block, to, src=None):
            return pltpu.make_async_remote_copy(src_ref=slot(*block) if src is None else src, dst_ref=slot(*block),
                                                send_sem=send_sems.at[k], recv_sem=recv_sems.at[k], device_id=to,
                                                device_id_type=MESH_IDS)

        mine = pltpu.make_async_copy(buf, slot(*me), local_sem)
        mine.start()
        first = [copy(0, me, sibling, src=buf)] + [copy(1 + j, me, (*chip, c_), src=buf) for j, chip in enumerate(chips)]
        for cp in first:
            cp.start()
        passed = [copy(4 + j, (*chip, c_), sibling) for j, chip in enumerate(chips)]
        for j, chip in enumerate(chips):
            copy(1 + j, (*chip, c_), me).wait_recv()
            passed[j].start()
        copy(0, sibling, me).wait_recv()
        for j, chip in enumerate(chips):
            copy(4 + j, (*chip, 1 - c_), me).wait_recv()
        for cp in first + passed:
            cp.wait_send()
        mine.wait()

    return pl.pallas_call(
        body, name=name, in_specs=[VMEM_SPEC], out_specs=ANY_SPEC, out_shape=_sds((N_DEV, R, C), out_dtype),
        scratch_shapes=[pltpu.VMEM((R, C), out_dtype), pltpu.SemaphoreType.DMA((N_DEV - 1,)),
                        pltpu.SemaphoreType.DMA((N_DEV - 1,)), pltpu.SemaphoreType.DMA(())],
        compiler_params=pltpu.CompilerParams(vmem_limit_bytes=VMEM_LIMIT),
    )(x)


def _all_to_all(name, x):
    _, R, C = x.shape

    def body(x_ref, out_ref, send_sems, recv_sems, local_sem):
        me = _me()
        local = pltpu.make_async_copy(x_ref.at[me], out_ref.at[me], local_sem)
        local.start()
        sends = []
        for k in range(1, N_DEV):
            dev, pj = _peer(k)
            cp = pltpu.make_async_remote_copy(src_ref=x_ref.at[pj], dst_ref=out_ref.at[me], send_sem=send_sems.at[k - 1],
                                              recv_sem=recv_sems.at[k - 1], device_id=dev, device_id_type=MESH_IDS)
            cp.start()
            sends.append(cp)
        for k in range(1, N_DEV):
            dev, pj = _peer(k)
            pltpu.make_async_remote_copy(src_ref=x_ref.at[pj], dst_ref=out_ref.at[pj], send_sem=send_sems.at[k - 1],
                                         recv_sem=recv_sems.at[k - 1], device_id=dev, device_id_type=MESH_IDS).wait_recv()
        for cp in sends:
            cp.wait_send()
        local.wait()

    return pl.pallas_call(
        body, name=name, in_specs=[VMEM_SPEC], out_specs=VMEM_SPEC, out_shape=_sds(x.shape, x.dtype),
        scratch_shapes=[pltpu.SemaphoreType.DMA((N_DEV - 1,)), pltpu.SemaphoreType.DMA((N_DEV - 1,)),
                        pltpu.SemaphoreType.DMA(())],
    )(x)


def _reduce_scatter_adamw(name, gblk, w, m, v):
    _, R, C = gblk.shape
    rows = 8
    for cand in (136, 128, 80, 64, 40, 32, 16, 8):
        if R % cand == 0:
            rows = cand
            break

    def body(g_ref, w_ref, m_ref, v_ref, go_ref, d_ref, mo_ref, vo_ref, recv, send_sems, recv_sems, local_sem):
        me = _me()
        local = pltpu.make_async_copy(g_ref.at[me], recv.at[me], local_sem)
        local.start()
        sends = []
        for k in range(1, N_DEV):
            dev, pj = _peer(k)
            cp = pltpu.make_async_remote_copy(src_ref=g_ref.at[pj], dst_ref=recv.at[me], send_sem=send_sems.at[k - 1],
                                              recv_sem=recv_sems.at[k - 1], device_id=dev, device_id_type=MESH_IDS)
            cp.start()
            sends.append(cp)
        for k in range(1, N_DEV):
            dev, pj = _peer(k)
            pltpu.make_async_remote_copy(src_ref=g_ref.at[pj], dst_ref=recv.at[pj], send_sem=send_sems.at[k - 1],
                                         recv_sem=recv_sems.at[k - 1], device_id=dev, device_id_type=MESH_IDS).wait_recv()
        local.wait()

        def chunk(i, carry):
            r = pl.ds(pl.multiple_of(i * rows, rows), rows)
            g = recv[0, r, :].astype(F32)
            for j in range(1, N_DEV):
                g = g + recv[j, r, :].astype(F32)
            go_ref[r, :] = g
            d_ref[r, :], mo_ref[r, :], vo_ref[r, :] = _adamw(w_ref[r, :], g, m_ref[r, :], v_ref[r, :])
            return carry

        lax.fori_loop(0, R // rows, chunk, 0)
        for cp in sends:
            cp.wait_send()

    return pl.pallas_call(
        body, name=name, in_specs=[ANY_SPEC, VMEM_SPEC, VMEM_SPEC, VMEM_SPEC], out_specs=[VMEM_SPEC] * 4,
        out_shape=[_sds((R, C), F32)] * 4,
        scratch_shapes=[pltpu.VMEM((N_DEV, R, C), BF16), pltpu.SemaphoreType.DMA((N_DEV - 1,)),
                        pltpu.SemaphoreType.DMA((N_DEV - 1,)), pltpu.SemaphoreType.DMA(())],
        compiler_params=pltpu.CompilerParams(vmem_limit_bytes=VMEM_LIMIT),
    )(gblk, w, m, v)


FIRST_WEIGHTS = ["mla_w_dq", "mla_w_uq", "mla_w_dkv", "mla_w_ukv"]
LATE_WEIGHTS = ["mla_w_o", "swa_w_qkv", "swa_w_o", "w_ff1", "w_ff2"]
ROW_SHARDED = {"mla_w_dq", "mla_w_dkv", "mla_w_o", "swa_w_o", "w_ff2"}


def _unblock(name, blocks):
    sh = blocks.shape[1:]
    if name in ROW_SHARDED:
        return jnp.moveaxis(blocks, 0, 1).reshape(sh[0], N_DEV * sh[1], sh[2])
    return jnp.moveaxis(blocks, 0, 2).reshape(sh[0], sh[1], N_DEV * sh[2])


def _block(name, full):
    L, K, N = full.shape
    if name in ROW_SHARDED:
        return jnp.moveaxis(full.reshape(L, N_DEV, K // N_DEV, N), 1, 0)
    return jnp.moveaxis(full.reshape(L, K, N_DEV, N // N_DEV), 2, 0)


def _rot_cols(w):
    half = QK_ROPE // 2
    return jnp.concatenate([-w[..., half:], w[..., :half]], axis=-1)


def _unrot_cols(gw):
    half = QK_ROPE // 2
    return jnp.concatenate([gw[..., half:], -gw[..., :half]], axis=-1)


def _row(v):
    return v.reshape(1, -1)


def _mlp_block_bwd(dx, sv, w1, w2, g, sc, gt):
    dy, du, dgt = _mlp_bwd_a(dx, sv["y2"], sv["rl"], gt, w2)
    dw2 = _matmul_tn("dw_ff2", sv["act"], dy, BF16)
    dw1 = _matmul_tn("dw_ff1", sv["h2"], du, BF16, column_blocks=True)
    dxo, dsh, da = _mlp_bwd_b(du, sv["x1"], dx, w1, g, sc)
    return dxo, dw1, dw2, dsh, da, dgt


def kernel(x, c, positions, w_ada, b_ada, g_mix, g_mlp, mla_w_dq, mla_g_q, mla_w_uq, mla_w_dkv, mla_g_kv, mla_w_ukv, mla_w_o, swa_w_qkv, swa_b_qkv, swa_sinks, swa_w_o, swa_b_o, w_ff1, w_ff2, g_final, loss_target, m_w_ada, m_b_ada, m_g_mix, m_g_mlp, m_mla_w_dq, m_mla_g_q, m_mla_w_uq, m_mla_w_dkv, m_mla_g_kv, m_mla_w_ukv, m_mla_w_o, m_swa_w_qkv, m_swa_b_qkv, m_swa_sinks, m_swa_w_o, m_swa_b_o, m_w_ff1, m_w_ff2, m_g_final, v_w_ada, v_b_ada, v_g_mix, v_g_mlp, v_mla_w_dq, v_mla_g_q, v_mla_w_uq, v_mla_w_dkv, v_mla_g_kv, v_mla_w_ukv, v_mla_w_o, v_swa_w_qkv, v_swa_b_qkv, v_swa_sinks, v_swa_w_o, v_swa_b_o, v_w_ff1, v_w_ff2, v_g_final):
    S, D = x.shape[1], x.shape[2]
    me = _me()
    x0 = x[0]
    target = loss_target[0]
    big_w = dict(mla_w_dq=mla_w_dq, mla_w_uq=mla_w_uq, mla_w_dkv=mla_w_dkv, mla_w_ukv=mla_w_ukv, mla_w_o=mla_w_o,
                 swa_w_qkv=swa_w_qkv, swa_w_o=swa_w_o, w_ff1=w_ff1, w_ff2=w_ff2)
    big_m = dict(mla_w_dq=m_mla_w_dq, mla_w_uq=m_mla_w_uq, mla_w_dkv=m_mla_w_dkv, mla_w_ukv=m_mla_w_ukv,
                 mla_w_o=m_mla_w_o, swa_w_qkv=m_swa_w_qkv, swa_w_o=m_swa_w_o, w_ff1=m_w_ff1, w_ff2=m_w_ff2)
    big_v = dict(mla_w_dq=v_mla_w_dq, mla_w_uq=v_mla_w_uq, mla_w_dkv=v_mla_w_dkv, mla_w_ukv=v_mla_w_ukv,
                 mla_w_o=v_mla_w_o, swa_w_qkv=v_swa_w_qkv, swa_w_o=v_swa_w_o, w_ff1=v_w_ff1, w_ff2=v_w_ff2)
    groups = {"first": FIRST_WEIGHTS}
    wrows = {n: -(-big_w[n].size // (PACK_COLS * 16)) * 16 for n in FIRST_WEIGHTS}
    offs = {g: np.concatenate([[0], np.cumsum([wrows[n] for n in names])]).astype(int) for g, names in groups.items()}

    def as_rows(n, a, lead=()):
        flat = a.reshape(lead + (-1,))
        pad = wrows[n] * PACK_COLS - flat.shape[-1]
        if pad:
            flat = jnp.pad(flat, ((0, 0),) * len(lead) + ((0, pad),))
        return flat.reshape(lead + (wrows[n], PACK_COLS))

    def pack(g, d):
        return jnp.concatenate([as_rows(n, d[n]) for n in groups[g]], axis=0)

    def pack_blocks(g, gfull):
        return jnp.concatenate([as_rows(n, _block(n, gfull[n]).astype(BF16), (N_DEV,)) for n in groups[g]], axis=1)

    def unpack(g, packed, lead=()):
        out = {}
        for i, n in enumerate(groups[g]):
            part = packed[..., int(offs[g][i]):int(offs[g][i + 1]), :].reshape(lead + (-1,))
            out[n] = part[..., :big_w[n].size].reshape(lead + big_w[n].shape)
        return out

    gathered = _all_gather_two_level("gather_weights", pack("first", big_w), BF16)
    wfull = {n: _unblock(n, b) for n, b in unpack("first", gathered, (N_DEV,)).items()}
    w_dq, w_dkv = wfull["mla_w_dq"][0], wfull["mla_w_dkv"][0]
    w_cat = jnp.concatenate([w_dq, w_dkv, _rot_cols(w_dkv[:, KV_LORA:])], axis=1)
    QL = w_dq.shape[1]
    w_uq = wfull["mla_w_uq"][0].reshape(QL, MLA_HEADS, QK_DIM)
    w_uqx = jnp.concatenate([w_uq, _rot_cols(w_uq[..., QK_NOPE:])], axis=-1).reshape(QL, MLA_HEADS * 256)
    w_ukv = wfull["mla_w_ukv"][0]

    L = w_ada.shape[0]
    NC = w_ada.shape[2]
    nbq, nbo = swa_b_qkv.shape[1], swa_b_o.shape[1]
    cpad = -(-(D + nbq + nbo) // 1024) * 1024
    cpack = jnp.pad(jnp.concatenate([c[0], swa_b_qkv[0], swa_b_o[0]]), (0, cpad - (D + nbq + nbo))).reshape(8, cpad // 8)
    call = _all_gather("gather_c", cpack, F32).reshape(N_DEV, cpad)
    c_all = call[:, :D]
    b_qkv_full = call[:, D:D + nbq].reshape(1, N_DEV * nbq)
    b_o_full = call[:, D + nbq:D + nbq + nbo].reshape(1, N_DEV * nbo)
    mod_cols = _ada_fwd(c_all, w_ada)
    mpad = -(-(L * NC) // 1024) * 1024
    mod_send = jnp.pad(jnp.moveaxis(mod_cols, 1, 0).reshape(N_DEV, L * NC), ((0, 0), (0, mpad - L * NC)))
    mod_mine = _all_to_all("exchange_mod", mod_send.reshape(N_DEV, 8, mpad // 8)).reshape(N_DEV, mpad)[:, :L * NC]
    mod = jnp.moveaxis(mod_mine.reshape(N_DEV, L, NC), 0, 1).reshape(L, N_DEV * NC) + b_ada
    mods = mod.reshape(L, 6, 1, D)

    half = QK_ROPE // 2
    inv_freq = ROPE_THETA ** (-jnp.arange(half, dtype=F32) / half)
    ang = positions[0].astype(F32)[:, None] * inv_freq
    cos = jnp.concatenate([jnp.cos(ang), jnp.cos(ang)], axis=-1)
    sin = jnp.concatenate([jnp.sin(ang), jnp.sin(ang)], axis=-1)

    T_ATT = ATT_TILE
    zero_bias = jnp.zeros((1, D), F32)

    sh1, sc1, gt1, sh2, sc2, gt2 = [mods[0, i] for i in range(6)]
    gm0, gp0 = _row(g_mix[0]), _row(g_mlp[0])
    h1, cqp, cq, ckvp, ckv, q, k, v, vt = _mla_in_fwd(x0, cos, sin, gm0, sc1, sh1, w_cat, mla_g_q, w_uqx, mla_g_kv,
                                                      w_ukv, ROW_TILE)
    o0, lse0, gathered = _mla_attn_fwd(q, k, vt, ATT_TILE_FWD, [big_w[n].astype(BF16) for n in LATE_WEIGHTS])
    wfull = {n: _unblock(n, b) for n, b in zip(LATE_WEIGHTS, gathered)}
    w_o_mla, w_qkv, w_o_swa = wfull["mla_w_o"][0], wfull["swa_w_qkv"][0], wfull["swa_w_o"][0]
    ff1, ff2 = wfull["w_ff1"], wfull["w_ff2"]
    y1, x1, h2 = _attn_out_fwd(o0, x0, w_o_mla, zero_bias, gt1, gp0, sc2, sh2)
    rl0, act0, y2, x2 = _mlp_fwd(h2, x1, ff1[0], ff2[0], gt2)
    sv0 = dict(y2=y2, rl=rl0, act=act0, h2=h2, x1=x1)

    th1, tc1, tg1, th2, tc2, tg2 = [mods[1, i] for i in range(6)]
    gm1, gp1 = _row(g_mix[1]), _row(g_mlp[1])
    h3, sq, sk, svv = _swa_in_fwd(x2, gm1, tc1, th1, w_qkv, b_qkv_full)
    o1, lse1, y3, x3, h4 = _swa_attn_fwd(sq, sk, svv, swa_sinks, x2, w_o_swa, b_o_full, tg1, gp1, tc2, th2)
    rl1, act1, y4, dx4, loss_part, dg_final = _mlp_fwd_loss(h4, x3, ff1[1], ff2[1], tg2, target, _row(g_final))
    sv1 = dict(y2=y4, rl=rl1, act=act1, h2=h4, x1=x3)

    dx3, dw1_1, dw2_1, dsh2_1, da2_1, dgt2_1 = _mlp_block_bwd(dx4, sv1, ff1[1], ff2[1], gp1, tc2, tg2)
    dsq, dsk, dsv, dsink, dy, dgt1_1, db_o = _swa_attn_bwd(sq, sk, svv, lse1, swa_sinks, dx3, y3, o1, tg1, w_o_swa)
    dw_o_swa = _matmul_tn("dw_o", o1, dy, BF16)
    dqkv, dx2, db_qkv, dsh1_1, da1_1 = _swa_in_bwd(dsq, dsk, dsv, x2, dx3, w_qkv, gm1, tc1)
    dw_qkv = _matmul_tn("dw_qkv", h3, dqkv, BF16)

    dx1, dw1_0, dw2_0, dsh2_0, da2_0, dgt2_0 = _mlp_block_bwd(dx2, sv0, ff1[0], ff2[0], gp0, sc2, gt2)
    dy, do, dl, dgt1_0, _ = _attn_out_bwd(dx1, y1, o0, gt1, w_o_mla, MLA_HEADS)
    dw_o_mla = _matmul_tn("dw_o", o0, dy, BF16)
    tb = min(T_ATT, S)
    delta = dl[:MLA_HEADS].reshape(MLA_HEADS, S // tb, 1, tb)
    glate = dict(mla_w_o=dw_o_mla[None], swa_w_qkv=dw_qkv[None], swa_w_o=dw_o_swa[None],
                 w_ff2=jnp.stack([dw2_0, dw2_1]))
    gblocks = {n: _block(n, g).astype(BF16) for n, g in glate.items()}
    gblocks["w_ff1"] = jnp.stack([dw1_0, dw1_1], axis=1)
    lse_rows = (lse0 * LOG2E).reshape(MLA_HEADS, S // tb, 1, tb)
    dq, dk, dv, recv = _mla_attn_bwd(q, k, v, do, lse_rows, delta, T_ATT, [gblocks[n] for n in LATE_WEIGHTS])
    late = {n: _sum_adamw(r, big_w[n], big_m[n], big_v[n]) for n, r in zip(LATE_WEIGHTS, recv)}
    dqx, dkv, dcat, dx0, dg_q, dg_kv, dsh1_0, da1_0 = _mla_in_bwd(
        dq, dk, dv, cos, sin, cqp, ckvp, x0, dx1, w_uqx, mla_g_q, w_ukv, mla_g_kv, w_cat, gm0, sc1)
    dw_uqx = _matmul_tn("dw_uq", cq, dqx).reshape(QL, MLA_HEADS, 256)
    dw_ukv = _matmul_tn("dw_ukv", ckv, dkv)
    dw_cat = _matmul_tn("dw_down", h1, dcat)
    dw_uq = jnp.concatenate([dw_uqx[..., :QK_NOPE], dw_uqx[..., 128:192] + _unrot_cols(dw_uqx[..., 192:256])],
                            axis=-1).reshape(QL, MLA_HEADS * QK_DIM)
    o_kr = QL + KV_LORA
    dw_dkv = jnp.concatenate([dw_cat[:, QL:o_kr],
                              dw_cat[:, o_kr:o_kr + QK_ROPE] + _unrot_cols(dw_cat[:, o_kr + QK_ROPE:])], axis=1)

    gfirst = dict(mla_w_dq=dw_cat[None, :, :QL], mla_w_uq=dw_uq[None], mla_w_dkv=dw_dkv[None], mla_w_ukv=dw_ukv[None])
    first = _reduce_scatter_adamw("grad_exchange_adamw", pack_blocks("first", gfirst), pack("first", big_w),
                                  pack("first", big_m), pack("first", big_v))
    big_g, big_d, big_nm, big_nv = ({**unpack("first", first[j]), **{n: late[n][j] for n in LATE_WEIGHTS}}
                                    for j in range(4))

    dmod = jnp.stack([
        jnp.concatenate([dsh1_0, gm0 * da1_0, dgt1_0, dsh2_0, gp0 * da2_0, dgt2_0], axis=1),
        jnp.concatenate([dsh1_1, gm1 * da1_1, dgt1_1, dsh2_1, gp1 * da2_1, dgt2_1], axis=1)]).reshape(-1)
    dg_mix = jnp.concatenate([(1.0 + sc1) * da1_0, (1.0 + tc1) * da1_1], axis=1).reshape(-1)
    dg_mlp = jnp.concatenate([(1.0 + sc2) * da2_0, (1.0 + tc2) * da2_1], axis=1).reshape(-1)
    parts = [loss_part.reshape(-1), dmod, dg_mix, dg_mlp, dg_q.reshape(-1), dg_kv.reshape(-1), dsink.reshape(-1),
             dg_final.reshape(-1), db_qkv.reshape(-1), db_o.reshape(-1)]
    soffs = np.concatenate([[0], np.cumsum([p.size for p in parts])])
    spad = -(-int(soffs[-1]) // 1024) * 1024
    spack = jnp.pad(jnp.concatenate(parts), (0, spad - int(soffs[-1]))).reshape(8, spad // 8)
    sall = _all_gather("gather_small_grads", spack, F32)
    ssum = _sum_devices(sall).reshape(-1)
    tot = [ssum[int(soffs[i]):int(soffs[i + 1])] for i in range(len(parts))]
    loss = tot[0][0]
    nsink = swa_sinks.shape[1]
    small_g = dict(b_ada=tot[1].reshape(b_ada.shape), g_mix=tot[2].reshape(g_mix.shape), g_mlp=tot[3].reshape(g_mlp.shape),
                   mla_g_q=tot[4].reshape(mla_g_q.shape), mla_g_kv=tot[5].reshape(mla_g_kv.shape),
                   swa_sinks=tot[6][:nsink].reshape(swa_sinks.shape), g_final=tot[7].reshape(g_final.shape),
                   swa_b_qkv=lax.dynamic_slice(tot[8], (me * nbq,), (nbq,)).reshape(swa_b_qkv.shape),
                   swa_b_o=lax.dynamic_slice(tot[9], (me * nbo,), (nbo,)).reshape(swa_b_o.shape))
    small_w = dict(b_ada=b_ada, g_mix=g_mix, g_mlp=g_mlp, mla_g_q=mla_g_q, mla_g_kv=mla_g_kv, swa_sinks=swa_sinks,
                   g_final=g_final, swa_b_qkv=swa_b_qkv, swa_b_o=swa_b_o)
    small_m = dict(b_ada=m_b_ada, g_mix=m_g_mix, g_mlp=m_g_mlp, mla_g_q=m_mla_g_q, mla_g_kv=m_mla_g_kv,
                   swa_sinks=m_swa_sinks, g_final=m_g_final, swa_b_qkv=m_swa_b_qkv, swa_b_o=m_swa_b_o)
    small_v = dict(b_ada=v_b_ada, g_mix=v_g_mix, g_mlp=v_g_mlp, mla_g_q=v_mla_g_q, mla_g_kv=v_mla_g_kv,
                   swa_sinks=v_swa_sinks, g_final=v_g_final, swa_b_qkv=v_swa_b_qkv, swa_b_o=v_swa_b_o)
    SMALL = list(small_w)
    woffs = np.concatenate([[0], np.cumsum([small_w[n].size for n in SMALL])])
    wpad = -(-int(woffs[-1]) // 1024) * 1024

    def spack_of(d):
        flat = jnp.concatenate([d[n].reshape(-1) for n in SMALL])
        return jnp.pad(flat, (0, wpad - int(woffs[-1]))).reshape(8, wpad // 8)

    sm = _adamw_small(spack_of(small_w), spack_of(small_g), spack_of(small_m), spack_of(small_v))
    small_d, small_nm, small_nv = (
        {n: a.reshape(-1)[int(woffs[i]):int(woffs[i + 1])].reshape(small_w[n].shape) for i, n in enumerate(SMALL)}
        for a in sm)

    b_off = int(soffs[1])
    dmod_all = sall.reshape(N_DEV, -1)[:, b_off:b_off + L * N_DEV * NC].reshape(N_DEV, L, N_DEV * NC)
    dmod_cols = jnp.moveaxis(lax.dynamic_slice_in_dim(dmod_all, me * NC, NC, axis=2), 0, 1)
    ada_g, ada_d, ada_nm, ada_nv = _ada_bwd_adamw(c_all.T, dmod_cols, w_ada, m_w_ada, v_w_ada)

    order = ["w_ada", "b_ada", "g_mix", "g_mlp", "mla_w_dq", "mla_g_q", "mla_w_uq", "mla_w_dkv", "mla_g_kv",
             "mla_w_ukv", "mla_w_o", "swa_w_qkv", "swa_b_qkv", "swa_sinks", "swa_w_o", "swa_b_o", "w_ff1", "w_ff2", "g_final"]

    def collect(ada, big, small):
        return [ada if n == "w_ada" else (big[n] if n in big else small[n]) for n in order]

    return (loss, dx0.reshape(x.shape), *collect(ada_g, big_g, small_g), *collect(ada_d, big_d, small_d),
            *collect(ada_nm, big_nm, small_nm), *collect(ada_nv, big_nv, small_nv))
```

```python
import jax
import jax.numpy as jnp
import numpy as np
from jax import lax
from jax.experimental import pallas as pl
from jax.experimental.pallas import tpu as pltpu

F32 = jnp.float32
BF16 = jnp.bfloat16
MESH_IDS = pl.DeviceIdType.MESH
N_DEV = 8

MLA_HEADS = 8
QK_NOPE = 128
QK_ROPE = 64
QK_DIM = QK_NOPE + QK_ROPE
V_DIM = 128
KV_LORA = 256
ROPE_THETA = 10000.0
SWA_HEADS = 16
SWA_KV_HEADS = 4
SWA_GROUP = SWA_HEADS // SWA_KV_HEADS
SWA_HEAD_DIM = 64
WINDOW = 128
EPS = 1e-6
LOG2E = 1.4426950408889634

ADAM_LR = 0.001
ADAM_B1 = 0.9
ADAM_B2 = 0.999
ADAM_EPS = 1e-08
ADAM_WD = 0.01
ADAM_STEP = 10

PACK_COLS = 1024
VMEM_LIMIT = 56 << 20
ROW_TILE = 512
ROW_TILE_WIDE = 256
ROW_TILE_BWD = 512
ATT_TILE = 512
ATT_TILE_FWD = 1024
TN_TOKENS = 4096


def _dot(a, b):
    return jnp.dot(a, b, preferred_element_type=F32)


def _dot_nt(a, b):
    return lax.dot_general(a, b, (((1,), (1,)), ((), ())), preferred_element_type=F32)


def _dot_tn(a, b):
    return lax.dot_general(a, b, (((0,), (0,)), ((), ())), preferred_element_type=F32)


def _rstd(x):
    return lax.rsqrt(jnp.mean(x * x, axis=-1, keepdims=True) + EPS)


def _rms_bwd(dn, n, r):
    return r * (dn - n * jnp.mean(dn * n, axis=-1, keepdims=True))


def _modulate(x, g, sc, sh):
    r = _rstd(x)
    return ((x * r) * g) * (1.0 + sc) + sh


def _modulate_bwd(dh, x, g, sc):
    r = _rstd(x)
    n = x * r
    dsh = jnp.sum(dh, axis=0, keepdims=True)
    da = jnp.sum(dh * n, axis=0, keepdims=True)
    dx = _rms_bwd(dh * (g * (1.0 + sc)), n, r)
    return dx, dsh, da


def _acc(ref, val, i):
    @pl.when(i == 0)
    def _():
        ref[...] = val

    @pl.when(i != 0)
    def _():
        ref[...] += val


def _row_spec(shape, tm):
    nd = len(shape)
    return pl.BlockSpec(tuple(shape[:nd - 2]) + (tm, shape[-1]), lambda i: (0,) * (nd - 2) + (i, 0))


def _resident_spec(shape, single_buffer):
    nd = len(shape)
    if single_buffer:
        return pl.BlockSpec(tuple(shape), lambda i: (0,) * nd, pipeline_mode=pl.Buffered(1))
    return pl.BlockSpec(tuple(shape), lambda i: (0,) * nd)


def _rowcall(name, body, tokens, tm, row_in, full_in, row_out, acc_out=()):
    tm = min(tm, tokens)
    in_specs = [_row_spec(a.shape, tm) for a in row_in] + [_resident_spec(a.shape, True) for a in full_in]
    row_specs = [s[1] if isinstance(s, tuple) else _row_spec(s.shape, tm) for s in row_out]
    row_out = [s[0] if isinstance(s, tuple) else s for s in row_out]
    out_specs = row_specs + [_resident_spec(s.shape, False) for s in acc_out]
    return pl.pallas_call(
        body, name=name, grid=(tokens // tm,), in_specs=in_specs, out_specs=out_specs,
        out_shape=list(row_out) + list(acc_out),
        compiler_params=pltpu.CompilerParams(dimension_semantics=("arbitrary",), vmem_limit_bytes=VMEM_LIMIT),
    )(*row_in, *full_in)


def _sds(shape, dtype):
    return jax.ShapeDtypeStruct(tuple(shape), dtype)


def _mla_in_fwd(x, cos, sin, g, sc, sh, w_cat, g_q, w_uqx, g_kv, w_ukv, t):
    S, D = x.shape
    QL = g_q.shape[1]
    H = MLA_HEADS
    t = min(t, S)

    def body(x_ref, cos_ref, sin_ref, g_ref, sc_ref, sh_ref, wcat_ref, gq_ref, wuqx_ref, gkv_ref, wukv_ref,
             h_ref, cqp_ref, cq_ref, ckvp_ref, ckv_ref, q_ref, k_ref, v_ref, vt_ref):
        cs, sn = cos_ref[...], sin_ref[...]
        hb = _modulate(x_ref[...], g_ref[...], sc_ref[...], sh_ref[...]).astype(BF16)
        h_ref[...] = hb
        low = _dot(hb, wcat_ref[...])
        cqp = low[:, :QL]
        cqp_ref[...] = cqp
        cq = ((cqp * _rstd(cqp)) * gq_ref[...]).astype(BF16)
        cq_ref[...] = cq
        ckvp = low[:, QL:QL + KV_LORA]
        ckvp_ref[...] = ckvp
        ckv = ((ckvp * _rstd(ckvp)) * gkv_ref[...]).astype(BF16)
        ckv_ref[...] = ckv
        o = QL + KV_LORA
        kr = (low[:, o:o + QK_ROPE] * cs + low[:, o + QK_ROPE:o + 2 * QK_ROPE] * sn).astype(BF16)
        qx = _dot(cq, wuqx_ref[...])
        kv = _dot(ckv, wukv_ref[...])
        for hd in range(H):
            b = hd * 256
            q_ref[hd, :, 0:QK_NOPE] = qx[:, b:b + QK_NOPE].astype(BF16)
            q_ref[hd, :, QK_NOPE:QK_DIM] = (qx[:, b + 128:b + 192] * cs + qx[:, b + 192:b + 256] * sn).astype(BF16)
            k_ref[hd, :, 0:QK_NOPE] = kv[:, b:b + QK_NOPE].astype(BF16)
            k_ref[hd, :, QK_NOPE:QK_DIM] = kr
            vh = kv[:, b + 128:b + 256]
            v_ref[hd] = vh.astype(BF16)
            vt_ref[hd, 0, 0:V_DIM, :] = vh.T.astype(BF16)
            vt_ref[hd, 0, V_DIM:2 * V_DIM, :] = jnp.ones((V_DIM, x_ref.shape[0]), BF16)

    vt_spec = pl.BlockSpec((H, 1, 2 * V_DIM, t), lambda i: (0, i, 0, 0))
    return _rowcall(
        "mla_in_fwd", body, S, t, [x, cos, sin], [g, sc, sh, w_cat, g_q, w_uqx, g_kv, w_ukv],
        [_sds((S, D), BF16), _sds((S, QL), F32), _sds((S, QL), BF16), _sds((S, KV_LORA), F32), _sds((S, KV_LORA), BF16),
         _sds((H, S, QK_DIM), BF16), _sds((H, S, QK_DIM), BF16), _sds((H, S, V_DIM), BF16),
         (_sds((H, S // t, 2 * V_DIM, t), BF16), vt_spec)])


def _mla_attn_fwd(q, k, vt, t, sends):
    H, S, DQ = q.shape
    DV = V_DIM
    vb = vt.shape[-1]
    t = max(min(t, S), vb)
    nb = S // t
    scale = QK_DIM ** -0.5
    c2 = scale * LOG2E

    ns = len(sends)

    def body(q_ref, k_ref, vt_ref, *rest):
        send_refs, (o_ref, lse_ref), gath_refs = rest[:ns], rest[ns:ns + 2], rest[ns + 2:2 * ns + 2]
        m_s, acc_s, s_buf, send_sems, recv_sems, local_sems = rest[2 * ns + 2:]
        hd, qi = pl.program_id(0), pl.program_id(1)

        def gather():
            return _exchange_copies([lambda j, r=r: r for r in send_refs], gath_refs, send_sems, recv_sems, local_sems)

        @pl.when((hd == 0) & (qi == 0))
        def _():
            _start_exchange(gather())

        m_s[...] = jnp.full_like(m_s, -jnp.inf)
        acc_s[...] = jnp.zeros_like(acc_s)

        def scores(j, slot):
            rows = pl.ds(pl.multiple_of(j * t, t), t)
            s_buf[slot] = _dot_nt(k_ref[0, rows, :], q_ref[0])

        nvb = t // vb

        def update(s, j, blocks, cols):
            m_prev = m_s[:, cols]
            m_new = jnp.maximum(m_prev, jnp.max(s, axis=0, keepdims=True))
            alpha = jnp.exp2((m_prev - m_new) * c2)
            pb = jnp.exp2((s - m_new) * c2).astype(BF16)
            acc = alpha * acc_s[:, cols]
            for n, u in enumerate(blocks):
                acc = acc + _dot(vt_ref[0, j * nvb + u], pb[n * vb:(n + 1) * vb, :])
            acc_s[:, cols] = acc
            m_s[:, cols] = m_new

        def causal(s, shape):
            key = lax.broadcasted_iota(jnp.int32, shape, 0)
            qry = lax.broadcasted_iota(jnp.int32, shape, 1)
            return jnp.where(key <= qry, s, -jnp.inf)

        def absorb(j, slot, diagonal):
            if not diagonal:
                update(s_buf[slot], j, range(nvb), slice(None))
            elif nvb % 2:
                update(causal(s_buf[slot], (t, t)), j, range(nvb), slice(None))
            else:
                half = t // 2
                update(causal(s_buf[slot, :half], (half, t)), j, range(nvb // 2), slice(None))
                update(causal(s_buf[slot, half:, half:], (half, half)), j, range(nvb // 2, nvb), slice(half, t))

        def pair(i, carry):
            j = 2 * i
            scores(j + 1, 1)
            absorb(j, 0, False)
            scores(j + 2, 0)
            absorb(j + 1, 1, False)
            return carry

        scores(0, 0)
        lax.fori_loop(0, qi // 2, pair, 0)

        @pl.when(qi % 2 == 0)
        def _():
            absorb(qi, 0, True)

        @pl.when(qi % 2 == 1)
        def _():
            scores(qi, 1)
            absorb(qi - 1, 0, False)
            absorb(qi, 1, True)

        acc = acc_s[...]
        o_ref[...] = (acc[:DV] / acc[DV:]).T.astype(BF16)
        lse_ref[0, 0] = m_s[...] * scale + jnp.log(acc[DV:DV + 1])

        @pl.when((hd == H - 1) & (qi == nb - 1))
        def _():
            _finish_exchange(gather())

    outs = pl.pallas_call(
        body, name="mla_attn_fwd", grid=(H, nb),
        in_specs=[pl.BlockSpec((1, t, DQ), lambda h, i: (h, i, 0)),
                  pl.BlockSpec((1, S, DQ), lambda h, i: (h, 0, 0)),
                  pl.BlockSpec((1, S // vb, 2 * DV, vb), lambda h, i: (h, 0, 0, 0))] + [ANY_SPEC] * ns,
        out_specs=[pl.BlockSpec((t, DV), lambda h, i: (i, h)),
                   pl.BlockSpec((1, 1, 1, t), lambda h, i: (h, i, 0, 0))] + [ANY_SPEC] * ns,
        out_shape=[_sds((S, H * DV), BF16), _sds((H, nb, 1, t), F32)]
        + [_sds((N_DEV,) + a.shape, a.dtype) for a in sends],
        scratch_shapes=[pltpu.VMEM((1, t), F32), pltpu.VMEM((2 * DV, t), F32), pltpu.VMEM((2, t, t), F32)]
        + _comm_sems(ns),
        compiler_params=pltpu.CompilerParams(dimension_semantics=("arbitrary", "arbitrary"),
                                             vmem_limit_bytes=VMEM_LIMIT),
    )(q, k, vt, *sends)
    return outs[0], outs[1], outs[2:]


def _attn_out_fwd(o, x, w_o, b_o, gt, g, sc, sh):
    S, D = x.shape

    def body(o_ref, x_ref, wo_ref, bo_ref, gt_ref, g_ref, sc_ref, sh_ref, y_ref, x1_ref, h_ref):
        y = _dot(o_ref[...], wo_ref[...]) + bo_ref[...]
        y_ref[...] = y.astype(BF16)
        x1 = x_ref[...] + gt_ref[...] * y
        x1_ref[...] = x1
        h_ref[...] = _modulate(x1, g_ref[...], sc_ref[...], sh_ref[...]).astype(BF16)

    return _rowcall("attn_out_fwd", body, S, ROW_TILE, [o, x], [w_o, b_o, gt, g, sc, sh],
                    [_sds((S, D), BF16), _sds((S, D), F32), _sds((S, D), BF16)])


def _mlp_fwd(h, x, w1, w2, gt):
    S, D = x.shape
    FF = w1.shape[1]

    def body(h_ref, x_ref, w1_ref, w2_ref, gt_ref, rl_ref, act_ref, y_ref, x2_ref):
        rl = jnp.maximum(_dot(h_ref[...], w1_ref[...]), 0.0)
        rl_ref[...] = rl.astype(BF16)
        act = (rl * rl).astype(BF16)
        act_ref[...] = act
        y = _dot(act, w2_ref[...])
        y_ref[...] = y.astype(BF16)
        x2_ref[...] = x_ref[...] + gt_ref[...] * y

    return _rowcall("mlp_fwd", body, S, ROW_TILE_WIDE, [h, x], [w1, w2, gt],
                    [_sds((S, FF), BF16), _sds((S, FF), BF16), _sds((S, D), BF16), _sds((S, D), F32)])


def _final_norm_loss(xv, target, g, d_model):
    r = _rstd(xv)
    n = xv * r
    err = n * g - target
    part = 0.5 * jnp.sum(jnp.mean(err * err, axis=-1, keepdims=True), axis=0, keepdims=True)
    dout = err / d_model
    return part, _rms_bwd(dout * g, n, r), jnp.sum(dout * n, axis=0, keepdims=True)


def _mlp_fwd_loss(h, x, w1, w2, gt, target, g_final):
    S, D = x.shape
    FF = w1.shape[1]

    def body(h_ref, x_ref, t_ref, w1_ref, w2_ref, gt_ref, g_ref, rl_ref, act_ref, y_ref, dx_ref, loss_ref, dg_ref):
        i = pl.program_id(0)
        rl = jnp.maximum(_dot(h_ref[...], w1_ref[...]), 0.0)
        rl_ref[...] = rl.astype(BF16)
        act = (rl * rl).astype(BF16)
        act_ref[...] = act
        y = _dot(act, w2_ref[...])
        y_ref[...] = y.astype(BF16)
        part, dx, dg = _final_norm_loss(x_ref[...] + gt_ref[...] * y, t_ref[...], g_ref[...], D)
        dx_ref[...] = dx
        _acc(loss_ref, jnp.broadcast_to(part, loss_ref.shape), i)
        _acc(dg_ref, dg, i)

    return _rowcall("mlp_fwd_loss", body, S, ROW_TILE_WIDE, [h, x, target], [w1, w2, gt, g_final],
                    [_sds((S, FF), BF16), _sds((S, FF), BF16), _sds((S, D), BF16), _sds((S, D), F32)],
                    [_sds((1, 128), F32), _sds((1, D), F32)])


def _swa_in_fwd(x, g, sc, sh, w_qkv, b_qkv):
    S, D = x.shape
    NQ = SWA_HEADS * SWA_HEAD_DIM
    NK = SWA_KV_HEADS * SWA_HEAD_DIM

    def body(x_ref, g_ref, sc_ref, sh_ref, w_ref, b_ref, h_ref, q_ref, k_ref, v_ref):
        hb = _modulate(x_ref[...], g_ref[...], sc_ref[...], sh_ref[...]).astype(BF16)
        h_ref[...] = hb
        qkv = _dot(hb, w_ref[...]) + b_ref[...]
        q_ref[...] = qkv[:, :NQ].astype(BF16)
        k_ref[...] = qkv[:, NQ:NQ + NK].astype(BF16)
        v_ref[...] = qkv[:, NQ + NK:].astype(BF16)

    return _rowcall("swa_in_fwd", body, S, ROW_TILE, [x], [g, sc, sh, w_qkv, b_qkv],
                    [_sds((S, D), BF16), _sds((S, NQ), BF16), _sds((S, NK), BF16), _sds((S, NK), BF16)])


def _alibi_slope(head):
    return float(np.float32(2.0 ** (-8.0 * (head + 1) / SWA_HEADS)))


def _swa_geometry(n):
    W, G = WINDOW, SWA_GROUP
    key = lax.broadcasted_iota(jnp.int32, (2 * W, G * W), 0)
    qry = lax.broadcasted_iota(jnp.int32, (2 * W, G * W), 1) & (W - 1)
    dist = W + qry - key
    valid = (dist >= 0) & (dist < W) & ((n > 0) | (key >= W))
    return dist.astype(F32), valid


def _swa_group(kh, q_ref, sink_ref):
    W, G, Dh = WINDOW, SWA_GROUP, SWA_HEAD_DIM
    heads = [kh * G + g for g in range(G)]
    q4 = jnp.concatenate([q_ref[:, h * Dh:(h + 1) * Dh] for h in heads], axis=0)
    slopes = jnp.concatenate([jnp.full((1, W), _alibi_slope(h), F32) for h in heads], axis=1)
    sinks = jnp.concatenate([jnp.broadcast_to(sink_ref[:, h:h + 1], (1, W)) for h in heads], axis=1)
    return heads, q4, slopes, sinks


def _swa_band_specs(W, nb, cols):
    prev = pl.BlockSpec((W, cols), lambda n: (jnp.maximum(jnp.minimum(n, nb - 1) - 1, 0), 0))
    cur = pl.BlockSpec((W, cols), lambda n: (jnp.minimum(n, nb - 1), 0))
    return prev, cur


def _swa_attn_fwd(q, k, v, sinks, x, w_o, b_o, gt, g, sc, sh):
    S, NQ = q.shape
    NK = k.shape[1]
    D = x.shape[1]
    W, Dh, G = WINDOW, SWA_HEAD_DIM, SWA_GROUP
    nb = S // W

    def body(q_ref, kp_ref, kc_ref, vp_ref, vc_ref, sink_ref, x_ref, wo_ref, bo_ref, gt_ref, g_ref, sc_ref, sh_ref,
             o_ref, lse_ref, y_ref, x1_ref, h_ref):
        distf, valid = _swa_geometry(pl.program_id(0))
        kband = jnp.concatenate([kp_ref[...], kc_ref[...]], axis=0)
        vband_t = jnp.concatenate([vp_ref[...], vc_ref[...]], axis=0).astype(F32).T.astype(BF16)
        outs = []
        for kh in range(SWA_KV_HEADS):
            kb = kband[:, kh * Dh:(kh + 1) * Dh]
            vbt = vband_t[kh * Dh:(kh + 1) * Dh, :]
            heads, q4, slopes, sinks = _swa_group(kh, q_ref, sink_ref)
            s = _dot_nt(kb, q4) * (Dh ** -0.5) - slopes * distf
            s = jnp.where(valid, s, -jnp.inf)
            m = jnp.maximum(jnp.max(s, axis=0, keepdims=True), sinks)
            p = jnp.exp(s - m)
            denom = jnp.sum(p, axis=0, keepdims=True) + jnp.exp(sinks - m)
            out4 = _dot(vbt, (p * (1.0 / denom)).astype(BF16))
            lse4 = m + jnp.log(denom)
            for g, h in enumerate(heads):
                outs.append(out4[:, g * W:(g + 1) * W])
                lse_ref[h:h + 1, :] = lse4[:, g * W:(g + 1) * W]
        ob = jnp.concatenate(outs, axis=0).T.astype(BF16)
        o_ref[...] = ob
        y = _dot(ob, wo_ref[...]) + bo_ref[...]
        y_ref[...] = y.astype(BF16)
        x1 = x_ref[...] + gt_ref[...] * y
        x1_ref[...] = x1
        h_ref[...] = _modulate(x1, g_ref[...], sc_ref[...], sh_ref[...]).astype(BF16)

    kprev, kcur = _swa_band_specs(W, nb, NK)
    blk = lambda cols: pl.BlockSpec((W, cols), lambda n: (n, 0))
    row = pl.BlockSpec((1, D), lambda n: (0, 0))
    return pl.pallas_call(
        body, name="swa_attn_fwd", grid=(nb,),
        in_specs=[blk(NQ), kprev, kcur, kprev, kcur, pl.BlockSpec((1, SWA_HEADS), lambda n: (0, 0)), blk(D),
                  pl.BlockSpec(w_o.shape, lambda n: (0, 0), pipeline_mode=pl.Buffered(1)), row, row, row, row, row],
        out_specs=[blk(NQ), pl.BlockSpec((SWA_HEADS, W), lambda n: (0, n)), blk(D), blk(D), blk(D)],
        out_shape=[_sds((S, NQ), BF16), _sds((SWA_HEADS, S), F32), _sds((S, D), BF16), _sds((S, D), F32),
                   _sds((S, D), BF16)],
        compiler_params=pltpu.CompilerParams(dimension_semantics=("arbitrary",), vmem_limit_bytes=VMEM_LIMIT),
    )(q, k, k, v, v, sinks, x, w_o, b_o, gt, g, sc, sh)


def _mlp_bwd_a(dx, y, rl, gt, w2):
    S, D = dx.shape
    FF = rl.shape[1]

    def body(dx_ref, y_ref, rl_ref, gt_ref, w2_ref, dy_ref, du_ref, dgt_ref):
        i = pl.program_id(0)
        dxv = dx_ref[...]
        _acc(dgt_ref, jnp.sum(dxv * y_ref[...].astype(F32), axis=0, keepdims=True), i)
        dy = (dxv * gt_ref[...]).astype(BF16)
        dy_ref[...] = dy
        dact = _dot_nt(dy, w2_ref[...])
        du_ref[...] = (dact * (2.0 * rl_ref[...].astype(F32))).astype(BF16)

    return _rowcall("mlp_bwd_a", body, S, ROW_TILE_BWD, [dx, y, rl], [gt, w2],
                    [_sds((S, D), BF16), _sds((S, FF), BF16)], [_sds((1, D), F32)])


def _mlp_bwd_b(du, x, dx, w1, g, sc):
    S, D = x.shape

    def body(du_ref, x_ref, dx_ref, w1_ref, g_ref, sc_ref, dxo_ref, dsh_ref, da_ref):
        i = pl.program_id(0)
        dh = _dot_nt(du_ref[...], w1_ref[...])
        dxn, dsh, da = _modulate_bwd(dh, x_ref[...], g_ref[...], sc_ref[...])
        dxo_ref[...] = dx_ref[...] + dxn
        _acc(dsh_ref, dsh, i)
        _acc(da_ref, da, i)

    return _rowcall("mlp_bwd_b", body, S, ROW_TILE_BWD, [du, x, dx], [w1, g, sc],
                    [_sds((S, D), F32)], [_sds((1, D), F32), _sds((1, D), F32)])


def _attn_out_bwd(dx, y, o, gt, w_o, n_heads):
    S, D = dx.shape
    NO = o.shape[1]
    dh = NO // n_heads
    member = (jnp.arange(NO)[None, :] // dh == jnp.arange(16)[:, None]).astype(BF16)

    def body(dx_ref, y_ref, o_ref, gt_ref, wo_ref, mem_ref, dy_ref, do_ref, dl_ref, dgt_ref, dbo_ref):
        i = pl.program_id(0)
        dxv = dx_ref[...]
        _acc(dgt_ref, jnp.sum(dxv * y_ref[...].astype(F32), axis=0, keepdims=True), i)
        dy = dxv * gt_ref[...]
        _acc(dbo_ref, jnp.sum(dy, axis=0, keepdims=True), i)
        dyb = dy.astype(BF16)
        dy_ref[...] = dyb
        do = _dot_nt(dyb, wo_ref[...])
        do_ref[...] = do.astype(BF16)
        prod = do * o_ref[...].astype(F32)
        hi = prod.astype(BF16)
        lo = (prod - hi.astype(F32)).astype(BF16)
        dl_ref[...] = _dot_nt(mem_ref[...], hi) + _dot_nt(mem_ref[...], lo)

    tm = min(ROW_TILE, S)
    return _rowcall("attn_out_bwd", body, S, ROW_TILE, [dx, y, o], [gt, w_o, member],
                    [_sds((S, D), BF16), _sds((S, NO), BF16),
                     (_sds((16, S), F32), pl.BlockSpec((16, tm), lambda i: (0, i)))],
                    [_sds((1, D), F32), _sds((1, D), F32)])


def _mla_attn_bwd(q, k, v, do, lse, delta, t, gblks):
    H, S, DQ = q.shape
    DV = V_DIM
    t = min(t, S // 2)
    tk = 2 * t
    nq, nk = S // t, S // tk
    scale = QK_DIM ** -0.5
    c2 = scale * LOG2E

    ng = len(gblks)

    def body(q_ref, k_ref, v_ref, do_ref, lse_ref, dl_ref, *rest):
        g_refs, (dq_ref, dk_ref, dv_ref), recv_refs = rest[:ng], rest[ng:ng + 3], rest[ng + 3:2 * ng + 3]
        dk_s, dv_s, s_buf, dp_buf, send_sems, recv_sems, local_sems = rest[2 * ng + 3:]
        hd, kj = pl.program_id(0), pl.program_id(1)

        def scatter():
            return _exchange_copies([lambda j, r=r: r.at[j] for r in g_refs], recv_refs, send_sems, recv_sems,
                                    local_sems)

        @pl.when((hd == 0) & (kj == 0))
        def _():
            _start_exchange(scatter())

        @pl.when(kj == 0)
        def _():
            dq_ref[...] = jnp.zeros_like(dq_ref)

        dk_s[...] = jnp.zeros_like(dk_s)
        dv_s[...] = jnp.zeros_like(dv_s)

        def products(i, slot, keys=tk):
            rows = pl.ds(pl.multiple_of(i * t, t), t)
            s_buf[slot, :keys] = _dot_nt(k_ref[0, :keys], q_ref[0, rows, :])
            dp_buf[slot, :keys] = _dot_nt(v_ref[0, :keys], do_ref[rows, :])

        def absorb(i, slot, diagonal, keys=tk):
            rows = pl.ds(pl.multiple_of(i * t, t), t)
            qb, dob = q_ref[0, rows, :], do_ref[rows, :]
            p = jnp.exp2(s_buf[slot, :keys] * c2 - lse_ref[0, i])
            if diagonal is not None:
                key = lax.broadcasted_iota(jnp.int32, (keys, t), 0)
                qry = lax.broadcasted_iota(jnp.int32, (keys, t), 1) + diagonal * t
                p = jnp.where(key <= qry, p, 0.0)
            dv_s[:keys] += _dot(p.astype(BF16), dob)
            ds = (p * (dp_buf[slot, :keys] - dl_ref[0, i])).astype(BF16)
            dk_s[:keys] += _dot(ds, qb)
            dq_ref[0, rows, :] += _dot_tn(ds, k_ref[0, :keys])

        first = 2 * kj + 2
        n_off = nq - first

        def pair(i, carry):
            u = 2 * i
            products(first + u + 1, 1)
            absorb(first + u, 0, None)
            products(jnp.where(u + 2 < n_off, first + u + 2, 2 * kj + 1), 0)
            absorb(first + u + 1, 1, None)
            return carry

        products(jnp.where(n_off > 0, first, 2 * kj + 1), 0)
        lax.fori_loop(0, n_off // 2, pair, 0)
        products(2 * kj, 1, t)
        absorb(2 * kj + 1, 0, 1)
        absorb(2 * kj, 1, 0, t)

        dk_ref[0] = (dk_s[...] * scale).astype(BF16)
        dv_ref[0] = dv_s[...].astype(BF16)

        @pl.when((hd == H - 1) & (kj == nk - 1))
        def _():
            _finish_exchange(scatter())

    rowspec = pl.BlockSpec((1, nq, 1, t), lambda h, j: (h, 0, 0, 0))
    outs = pl.pallas_call(
        body, name="mla_attn_bwd", grid=(H, nk),
        in_specs=[pl.BlockSpec((1, S, DQ), lambda h, j: (h, 0, 0)),
                  pl.BlockSpec((1, tk, DQ), lambda h, j: (h, j, 0)),
                  pl.BlockSpec((1, tk, DV), lambda h, j: (h, j, 0)),
                  pl.BlockSpec((S, DV), lambda h, j: (0, h)), rowspec, rowspec] + [ANY_SPEC] * ng,
        out_specs=[pl.BlockSpec((1, S, DQ), lambda h, j: (h, 0, 0)),
                   pl.BlockSpec((1, tk, DQ), lambda h, j: (h, j, 0)),
                   pl.BlockSpec((1, tk, DV), lambda h, j: (h, j, 0))] + [ANY_SPEC] * ng,
        out_shape=[_sds((H, S, DQ), F32), _sds((H, S, DQ), BF16), _sds((H, S, DV), BF16)]
        + [_sds(g.shape, g.dtype) for g in gblks],
        scratch_shapes=[pltpu.VMEM((tk, DQ), F32), pltpu.VMEM((tk, DV), F32), pltpu.VMEM((2, tk, t), F32),
                        pltpu.VMEM((2, tk, t), F32)] + _comm_sems(ng),
        compiler_params=pltpu.CompilerParams(dimension_semantics=("arbitrary", "arbitrary"),
                                             vmem_limit_bytes=VMEM_LIMIT),
    )(q, k, v, do, lse, delta, *gblks)
    return outs[0], outs[1], outs[2], outs[3:]


def _swa_attn_bwd(q, k, v, lse, sinks, dx, y, o, gt, w_o):
    S, NQ = q.shape
    NK = k.shape[1]
    D = dx.shape[1]
    W, Dh, G = WINDOW, SWA_HEAD_DIM, SWA_GROUP
    nb = S // W
    member = (jnp.arange(NQ)[None, :] // Dh == jnp.arange(SWA_HEADS)[:, None]).astype(BF16)

    def body(q_ref, kp_ref, kc_ref, vp_ref, vc_ref, lse_ref, sink_ref, dx_ref, y_ref, o_ref, gt_ref, wo_ref, mem_ref,
             dq_ref, dk_ref, dv_ref, dsink_ref, dy_ref, dgt_ref, dbo_ref, dkc_s, dvc_s):
        n = pl.program_id(0)

        @pl.when(n == 0)
        def _():
            dkc_s[...] = jnp.zeros_like(dkc_s)
            dvc_s[...] = jnp.zeros_like(dvc_s)
            dsink_ref[...] = jnp.zeros_like(dsink_ref)

        @pl.when(n < nb)
        def _():
            dxv = dx_ref[...]
            _acc(dgt_ref, jnp.sum(dxv * y_ref[...].astype(F32), axis=0, keepdims=True), n)
            dy = dxv * gt_ref[...]
            _acc(dbo_ref, jnp.sum(dy, axis=0, keepdims=True), n)
            dyb = dy.astype(BF16)
            dy_ref[...] = dyb
            do = _dot_nt(dyb, wo_ref[...])
            dob = do.astype(BF16)
            prod = do * o_ref[...].astype(F32)
            hi = prod.astype(BF16)
            lo = (prod - hi.astype(F32)).astype(BF16)
            dl = _dot_nt(mem_ref[...], hi) + _dot_nt(mem_ref[...], lo)
            distf, valid = _swa_geometry(n)
            kband = jnp.concatenate([kp_ref[...], kc_ref[...]], axis=0)
            vband = jnp.concatenate([vp_ref[...], vc_ref[...]], axis=0)
            kband_t = kband.astype(F32).T.astype(BF16)
            dq_t = []
            for kh in range(SWA_KV_HEADS):
                ck = slice(kh * Dh, (kh + 1) * Dh)
                kb, vb, kbt = kband[:, ck], vband[:, ck], kband_t[ck, :]
                heads, q4, slopes, sinks = _swa_group(kh, q_ref, sink_ref)
                do4 = jnp.concatenate([dob[:, h * Dh:(h + 1) * Dh] for h in heads], axis=0)
                lse4 = jnp.concatenate([lse_ref[h:h + 1, :] for h in heads], axis=1)
                dl4 = jnp.concatenate([dl[h:h + 1, :] for h in heads], axis=1)
                s = _dot_nt(kb, q4) * (Dh ** -0.5) - slopes * distf
                p = jnp.where(valid, jnp.exp(s - lse4), 0.0)
                dvb = _dot(p.astype(BF16), do4)
                dp = _dot_nt(vb, do4)
                dsb = ((p * (dp - dl4)) * (Dh ** -0.5)).astype(BF16)
                dq4 = _dot(kbt, dsb)
                dkb = _dot(dsb, q4)
                dsk4 = jnp.exp(sinks - lse4) * dl4
                for g, h in enumerate(heads):
                    dq_t.append(dq4[:, g * W:(g + 1) * W])
                    dsink_ref[:, h:h + 1] += -jnp.sum(dsk4[:, g * W:(g + 1) * W], axis=1, keepdims=True)
                dk_ref[:, ck] = (dkc_s[:, ck] + dkb[:W]).astype(BF16)
                dv_ref[:, ck] = (dvc_s[:, ck] + dvb[:W]).astype(BF16)
                dkc_s[:, ck] = dkb[W:]
                dvc_s[:, ck] = dvb[W:]
            dq_ref[...] = jnp.concatenate(dq_t, axis=0).T.astype(BF16)

        @pl.when(n == nb)
        def _():
            dk_ref[...] = dkc_s[...].astype(BF16)
            dv_ref[...] = dvc_s[...].astype(BF16)

    kprev, kcur = _swa_band_specs(W, nb, NK)
    qspec = lambda cols: pl.BlockSpec((W, cols), lambda n: (jnp.minimum(n, nb - 1), 0))
    kvout = pl.BlockSpec((W, NK), lambda n: (jnp.maximum(n - 1, 0), 0))
    rowspec = pl.BlockSpec((SWA_HEADS, W), lambda n: (0, jnp.minimum(n, nb - 1)))
    fixed = lambda shape: pl.BlockSpec(shape, lambda n: (0, 0))
    return pl.pallas_call(
        body, name="swa_attn_bwd", grid=(nb + 1,),
        in_specs=[qspec(NQ), kprev, kcur, kprev, kcur, rowspec, pl.BlockSpec((1, SWA_HEADS), lambda n: (0, 0)),
                  qspec(D), qspec(D), qspec(NQ), fixed((1, D)), fixed(w_o.shape), fixed(member.shape)],
        out_specs=[qspec(NQ), kvout, kvout, fixed((1, 128)), qspec(D), fixed((1, D)), fixed((1, D))],
        out_shape=[_sds((S, NQ), BF16), _sds((S, NK), BF16), _sds((S, NK), BF16), _sds((1, 128), F32),
                   _sds((S, D), BF16), _sds((1, D), F32), _sds((1, D), F32)],
        scratch_shapes=[pltpu.VMEM((W, NK), F32), pltpu.VMEM((W, NK), F32)],
        compiler_params=pltpu.CompilerParams(dimension_semantics=("arbitrary",), vmem_limit_bytes=VMEM_LIMIT),
    )(q, k, k, v, v, lse, sinks, dx, y, o, gt, w_o, member)


def _swa_in_bwd(dq, dk, dv, x, dx, w_qkv, g, sc):
    S, D = x.shape
    N = w_qkv.shape[1]

    def body(dq_ref, dk_ref, dv_ref, x_ref, dx_ref, w_ref, g_ref, sc_ref, dqkv_ref, dxo_ref, db_ref, dsh_ref, da_ref):
        i = pl.program_id(0)
        dqkv = jnp.concatenate([dq_ref[...], dk_ref[...], dv_ref[...]], axis=1)
        dqkv_ref[...] = dqkv
        _acc(db_ref, jnp.sum(dqkv.astype(F32), axis=0, keepdims=True), i)
        dh = _dot_nt(dqkv, w_ref[...])
        dxn, dsh, da = _modulate_bwd(dh, x_ref[...], g_ref[...], sc_ref[...])
        dxo_ref[...] = dx_ref[...] + dxn
        _acc(dsh_ref, dsh, i)
        _acc(da_ref, da, i)

    return _rowcall("swa_in_bwd", body, S, ROW_TILE, [dq, dk, dv, x, dx], [w_qkv, g, sc],
                    [_sds((S, N), BF16), _sds((S, D), F32)],
                    [_sds((1, N), F32), _sds((1, D), F32), _sds((1, D), F32)])


def _mla_in_bwd(dq, dk, dv, cos, sin, cqp, ckvp, x, dx, w_uqx, g_q, w_ukv, g_kv, w_cat, g, sc):
    S, D = x.shape
    H = MLA_HEADS
    QL = g_q.shape[1]
    NX = w_uqx.shape[1]
    NC = w_cat.shape[1]

    def body(dq_ref, dk_ref, dv_ref, cos_ref, sin_ref, cqp_ref, ckvp_ref, x_ref, dx_ref,
             wuqx_ref, gq_ref, wukv_ref, gkv_ref, wcat_ref, g_ref, sc_ref,
             dqx_ref, dkv_ref, dcat_ref, dxo_ref, dgq_ref, dgkv_ref, dsh_ref, da_ref):
        i = pl.program_id(0)
        cs, sn = cos_ref[...], sin_ref[...]
        dkr = jnp.zeros(cs.shape, F32)
        for hd in range(H):
            b = hd * 256
            dqh = dq_ref[hd] * (QK_DIM ** -0.5)
            dqx_ref[:, b:b + QK_NOPE] = dqh[:, :QK_NOPE].astype(BF16)
            dqx_ref[:, b + 128:b + 192] = (dqh[:, QK_NOPE:] * cs).astype(BF16)
            dqx_ref[:, b + 192:b + 256] = (dqh[:, QK_NOPE:] * sn).astype(BF16)
            dkh = dk_ref[hd]
            dkv_ref[:, b:b + QK_NOPE] = dkh[:, :QK_NOPE]
            dkv_ref[:, b + 128:b + 256] = dv_ref[hd]
            dkr = dkr + dkh[:, QK_NOPE:].astype(F32)
        dcq = _dot_nt(dqx_ref[...], wuqx_ref[...])
        cqp = cqp_ref[...]
        rq = _rstd(cqp)
        nq = cqp * rq
        _acc(dgq_ref, jnp.sum(dcq * nq, axis=0, keepdims=True), i)
        dcqp = _rms_bwd(dcq * gq_ref[...], nq, rq)
        dckv = _dot_nt(dkv_ref[...], wukv_ref[...])
        ckvp = ckvp_ref[...]
        rk = _rstd(ckvp)
        nk = ckvp * rk
        _acc(dgkv_ref, jnp.sum(dckv * nk, axis=0, keepdims=True), i)
        dckvp = _rms_bwd(dckv * gkv_ref[...], nk, rk)
        dcat_ref[:, :QL] = dcqp.astype(BF16)
        dcat_ref[:, QL:QL + KV_LORA] = dckvp.astype(BF16)
        o = QL + KV_LORA
        dcat_ref[:, o:o + QK_ROPE] = (dkr * cs).astype(BF16)
        dcat_ref[:, o + QK_ROPE:o + 2 * QK_ROPE] = (dkr * sn).astype(BF16)
        dh = _dot_nt(dcat_ref[...], wcat_ref[...])
        dxn, dsh, da = _modulate_bwd(dh, x_ref[...], g_ref[...], sc_ref[...])
        dxo_ref[...] = dx_ref[...] + dxn
        _acc(dsh_ref, dsh, i)
        _acc(da_ref, da, i)

    return _rowcall("mla_in_bwd", body, S, ROW_TILE, [dq, dk, dv, cos, sin, cqp, ckvp, x, dx],
                    [w_uqx, g_q, w_ukv, g_kv, w_cat, g, sc],
                    [_sds((S, NX), BF16), _sds((S, NX), BF16), _sds((S, NC), BF16), _sds((S, D), F32)],
                    [_sds((1, QL), F32), _sds((1, KV_LORA), F32), _sds((1, D), F32), _sds((1, D), F32)])


def _matmul_tn(name, a, b, out_dtype=F32, column_blocks=False):
    S, K = a.shape
    N = b.shape[1]
    tk, tn, ts = min(K, 1024), min(N, 1024), min(S, TN_TOKENS)
    if column_blocks:
        tn = N // N_DEV
    if N % tn:
        tn = 512 if N % 512 == 0 else (384 if N % 384 == 0 else 128)
    if K % tk:
        tk = 512 if K % 512 == 0 else (384 if K % 384 == 0 else 128)
    ns = S // ts

    def body(a_ref, b_ref, o_ref, *scratch):
        acc_ref = scratch[0] if scratch else o_ref
        _acc(acc_ref, _dot_tn(a_ref[...], b_ref[...]), pl.program_id(2))
        if scratch:
            @pl.when(pl.program_id(2) == ns - 1)
            def _():
                o_ref[...] = acc_ref[...].astype(out_dtype)

    if column_blocks:
        out_spec = pl.BlockSpec((None, tk, tn), lambda i, j, s: (j, i, 0))
        out_shape = _sds((N_DEV, K, tn), out_dtype)
    else:
        out_spec = pl.BlockSpec((tk, tn), lambda i, j, s: (i, j))
        out_shape = _sds((K, N), out_dtype)
    return pl.pallas_call(
        body, name=name, grid=(K // tk, N // tn, ns),
        in_specs=[pl.BlockSpec((ts, tk), lambda i, j, s: (s, i)), pl.BlockSpec((ts, tn), lambda i, j, s: (s, j))],
        out_specs=out_spec, out_shape=out_shape,
        scratch_shapes=[] if out_dtype == F32 else [pltpu.VMEM((tk, tn), F32)],
        compiler_params=pltpu.CompilerParams(dimension_semantics=("parallel", "parallel", "arbitrary"),
                                             vmem_limit_bytes=VMEM_LIMIT),
    )(a, b)


def _silu(c):
    return c * jax.nn.sigmoid(c)


def _ada_fwd(c_all, w_ada):
    L, D, NC = w_ada.shape

    def body(c_ref, w_ref, o_ref):
        cond = _silu(c_ref[...]).astype(BF16)
        o_ref[0] = _dot(cond, w_ref[0].astype(BF16))

    return pl.pallas_call(
        body, name="ada_fwd", grid=(L,),
        in_specs=[pl.BlockSpec(c_all.shape, lambda l: (0, 0)), pl.BlockSpec((1, D, NC), lambda l: (l, 0, 0))],
        out_specs=pl.BlockSpec((1, N_DEV, NC), lambda l: (l, 0, 0)),
        out_shape=_sds((L, N_DEV, NC), F32),
        compiler_params=pltpu.CompilerParams(dimension_semantics=("arbitrary",), vmem_limit_bytes=VMEM_LIMIT),
    )(c_all, w_ada)


def _adamw(w, g, m, v):
    m = ADAM_B1 * m + (1.0 - ADAM_B1) * g
    v = ADAM_B2 * v + (1.0 - ADAM_B2) * (g * g)
    m_hat = m / (1.0 - ADAM_B1 ** ADAM_STEP)
    v_hat = v / (1.0 - ADAM_B2 ** ADAM_STEP)
    delta = -ADAM_LR * (m_hat / (jnp.sqrt(v_hat) + ADAM_EPS) + ADAM_WD * w)
    return delta, m, v


def _ada_bwd_adamw(c_all_t, dmod_cols, w, m, v):
    L, D, NC = w.shape
    tr = min(D, 256)

    def body(ct_ref, dm_ref, w_ref, m_ref, v_ref, g_ref, d_ref, mo_ref, vo_ref):
        cond_t = _silu(ct_ref[...])
        dm = dm_ref[0]
        g = cond_t[:, 0:1] * dm[0:1, :]
        for b in range(1, N_DEV):
            g = g + cond_t[:, b:b + 1] * dm[b:b + 1, :]
        g_ref[0] = g
        d_ref[0], mo_ref[0], vo_ref[0] = _adamw(w_ref[0], g, m_ref[0], v_ref[0])

    wspec = pl.BlockSpec((1, tr, NC), lambda l, r: (l, r, 0))
    return pl.pallas_call(
        body, name="ada_bwd_adamw", grid=(L, D // tr),
        in_specs=[pl.BlockSpec((tr, N_DEV), lambda l, r: (r, 0)),
                  pl.BlockSpec((1, N_DEV, NC), lambda l, r: (l, 0, 0)), wspec, wspec, wspec],
        out_specs=[wspec] * 4, out_shape=[_sds(w.shape, F32)] * 4,
        compiler_params=pltpu.CompilerParams(dimension_semantics=("parallel", "parallel"), vmem_limit_bytes=VMEM_LIMIT),
    )(c_all_t, dmod_cols, w, m, v)


def _sum_devices(x):
    def body(x_ref, o_ref):
        s = x_ref[0]
        for j in range(1, N_DEV):
            s = s + x_ref[j]
        o_ref[...] = s

    return pl.pallas_call(body, name="sum_devices", out_shape=_sds(x.shape[1:], F32))(x)


def _adamw_small(w, g, m, v):
    def body(w_ref, g_ref, m_ref, v_ref, d_ref, mo_ref, vo_ref):
        d_ref[...], mo_ref[...], vo_ref[...] = _adamw(w_ref[...], g_ref[...], m_ref[...], v_ref[...])

    return pl.pallas_call(body, name="adamw_small", out_shape=[_sds(w.shape, F32)] * 3)(w, g, m, v)


def _me():
    return lax.axis_index("x") * 4 + lax.axis_index("y") * 2 + lax.axis_index("c")


def _peer(k):
    x, y, c = lax.axis_index("x"), lax.axis_index("y"), lax.axis_index("c")
    px = 1 - x if k & 4 else x
    py = 1 - y if k & 2 else y
    pc = 1 - c if k & 1 else c
    return (px, py, pc), px * 4 + py * 2 + pc


VMEM_SPEC = pl.BlockSpec(memory_space=pltpu.VMEM)
ANY_SPEC = pl.BlockSpec(memory_space=pl.ANY)
def _comm_sems(n):
    return [pltpu.SemaphoreType.DMA((n * (N_DEV - 1),)), pltpu.SemaphoreType.DMA((n * (N_DEV - 1),)),
            pltpu.SemaphoreType.DMA((n,))]


def _exchange_copies(srcs_of, dst_refs, send_sems, recv_sems, local_sems):
    me = _me()
    local, sends, recvs = [], [], []
    for a, (src_of, dst_ref) in enumerate(zip(srcs_of, dst_refs)):
        local.append(pltpu.make_async_copy(src_of(me), dst_ref.at[me], local_sems.at[a]))
        for k in range(1, N_DEV):
            dev, pj = _peer(k)
            i = a * (N_DEV - 1) + k - 1
            sems = dict(send_sem=send_sems.at[i], recv_sem=recv_sems.at[i], device_id=dev, device_id_type=MESH_IDS)
            sends.append(pltpu.make_async_remote_copy(src_ref=src_of(pj), dst_ref=dst_ref.at[me], **sems))
            recvs.append(pltpu.make_async_remote_copy(src_ref=src_of(pj), dst_ref=dst_ref.at[pj], **sems))
    return local, sends, recvs


def _start_exchange(copies):
    local, sends, _ = copies
    for cp in local + sends:
        cp.start()


def _finish_exchange(copies):
    local, sends, recvs = copies
    for cp in recvs:
        cp.wait_recv()
    for cp in sends:
        cp.wait_send()
    for cp in local:
        cp.wait()


def _sum_adamw(recv, w, m, v):
    shape = w.shape
    C = shape[-1]
    R = w.size // C
    rows = max(d for d in range(16, min(R, 512) + 1, 16) if R % d == 0 and d * C <= 256 * 1024)

    def body(r_ref, w_ref, m_ref, v_ref, go_ref, d_ref, mo_ref, vo_ref):
        g = r_ref[0].astype(F32)
        for j in range(1, N_DEV):
            g = g + r_ref[j].astype(F32)
        go_ref[...] = g
        d_ref[...], mo_ref[...], vo_ref[...] = _adamw(w_ref[...], g, m_ref[...], v_ref[...])

    spec = pl.BlockSpec((rows, C), lambda i: (i, 0))
    outs = pl.pallas_call(
        body, name="sum_adamw", grid=(R // rows,),
        in_specs=[pl.BlockSpec((N_DEV, rows, C), lambda i: (0, i, 0)), spec, spec, spec],
        out_specs=[spec] * 4, out_shape=[_sds((R, C), F32)] * 4,
        compiler_params=pltpu.CompilerParams(dimension_semantics=("parallel",), vmem_limit_bytes=VMEM_LIMIT),
    )(recv.reshape(N_DEV, R, C), w.reshape(R, C), m.reshape(R, C), v.reshape(R, C))
    return [o.reshape(shape) for o in outs]


def _all_gather(name, x, out_dtype):
    R, C = x.shape
    cast = out_dtype != x.dtype

    def body(x_ref, out_ref, buf, send_sems, recv_sems, local_sem):
        me = _me()
        if cast:
            buf[...] = x_ref[...].astype(out_dtype)
            src = buf
        else:
            src = x_ref
        local = pltpu.make_async_copy(src, out_ref.at[me], local_sem)
        local.start()
        sends = []
        for k in range(1, N_DEV):
            dev, _ = _peer(k)
            cp = pltpu.make_async_remote_copy(src_ref=src, dst_ref=out_ref.at[me], send_sem=send_sems.at[k - 1],
                                              recv_sem=recv_sems.at[k - 1], device_id=dev, device_id_type=MESH_IDS)
            cp.start()
            sends.append(cp)
        for k in range(1, N_DEV):
            dev, pj = _peer(k)
            pltpu.make_async_remote_copy(src_ref=src, dst_ref=out_ref.at[pj], send_sem=send_sems.at[k - 1],
                                         recv_sem=recv_sems.at[k - 1], device_id=dev, device_id_type=MESH_IDS).wait_recv()
        for cp in sends:
            cp.wait_send()
        local.wait()

    return pl.pallas_call(
        body, name=name, in_specs=[VMEM_SPEC], out_specs=ANY_SPEC, out_shape=_sds((N_DEV, R, C), out_dtype),
        scratch_shapes=[pltpu.VMEM((R, C) if cast else (8, 128), out_dtype),
                        pltpu.SemaphoreType.DMA((N_DEV - 1,)), pltpu.SemaphoreType.DMA((N_DEV - 1,)),
                        pltpu.SemaphoreType.DMA(())],
        compiler_params=pltpu.CompilerParams(vmem_limit_bytes=VMEM_LIMIT),
    )(x)


def _all_gather_two_level(name, x, out_dtype):
    R, C = x.shape

    def body(x_ref, out_ref, buf, send_sems, recv_sems, local_sem):
        x_, y_, c_ = lax.axis_index("x"), lax.axis_index("y"), lax.axis_index("c")
        me, sibling = (x_, y_, c_), (x_, y_, 1 - c_)
        chips = [(1 - x_, y_), (x_, 1 - y_), (1 - x_, 1 - y_)]
        buf[...] = x_ref[...].astype(out_dtype)

        def slot(px, py, pc):
            return out_ref.at[4 * px + 2 * py + pc]

        def copy(k, block, to, src=None):
            return pltpu.make_async_remote_copy(src_ref=slot(*block) if src is None else src, dst_ref=slot(*block),
                                                send_sem=send_sems.at[k], recv_sem=recv_sems.at[k], device_id=to,
                                                device_id_type=MESH_IDS)

        mine = pltpu.make_async_copy(buf, slot(*me), local_sem)
        mine.start()
        first = [copy(0, me, sibling, src=buf)] + [copy(1 + j, me, (*chip, c_), src=buf) for j, chip in enumerate(chips)]
        for cp in first:
            cp.start()
        passed = [copy(4 + j, (*chip, c_), sibling) for j, chip in enumerate(chips)]
        for j, chip in enumerate(chips):
            copy(1 + j, (*chip, c_), me).wait_recv()
            passed[j].start()
        copy(0, sibling, me).wait_recv()
        for j, chip in enumerate(chips):
            copy(4 + j, (*chip, 1 - c_), me).wait_recv()
        for cp in first + passed:
            cp.wait_send()
        mine.wait()

    return pl.pallas_call(
        body, name=name, in_specs=[VMEM_SPEC], out_specs=ANY_SPEC, out_shape=_sds((N_DEV, R, C), out_dtype),
        scratch_shapes=[pltpu.VMEM((R, C), out_dtype), pltpu.SemaphoreType.DMA((N_DEV - 1,)),
                        pltpu.SemaphoreType.DMA((N_DEV - 1,)), pltpu.SemaphoreType.DMA(())],
        compiler_params=pltpu.CompilerParams(vmem_limit_bytes=VMEM_LIMIT),
    )(x)


def _all_to_all(name, x):
    _, R, C = x.shape

    def body(x_ref, out_ref, send_sems, recv_sems, local_sem):
        me = _me()
        local = pltpu.make_async_copy(x_ref.at[me], out_ref.at[me], local_sem)
        local.start()
        sends = []
        for k in range(1, N_DEV):
            dev, pj = _peer(k)
            cp = pltpu.make_async_remote_copy(src_ref=x_ref.at[pj], dst_ref=out_ref.at[me], send_sem=send_sems.at[k - 1],
                                              recv_sem=recv_sems.at[k - 1], device_id=dev, device_id_type=MESH_IDS)
            cp.start()
            sends.append(cp)
        for k in range(1, N_DEV):
            dev, pj = _peer(k)
            pltpu.make_async_remote_copy(src_ref=x_ref.at[pj], dst_ref=out_ref.at[pj], send_sem=send_sems.at[k - 1],
                                         recv_sem=recv_sems.at[k - 1], device_id=dev, device_id_type=MESH_IDS).wait_recv()
        for cp in sends:
            cp.wait_send()
        local.wait()

    return pl.pallas_call(
        body, name=name, in_specs=[VMEM_SPEC], out_specs=VMEM_SPEC, out_shape=_sds(x.shape, x.dtype),
        scratch_shapes=[pltpu.SemaphoreType.DMA((N_DEV - 1,)), pltpu.SemaphoreType.DMA((N_DEV - 1,)),
                        pltpu.SemaphoreType.DMA(())],
    )(x)


def _reduce_scatter_adamw(name, gblk, w, m, v):
    _, R, C = gblk.shape
    rows = 8
    for cand in (136, 128, 80, 64, 40, 32, 16, 8):
        if R % cand == 0:
            rows = cand
            break

    def body(g_ref, w_ref, m_ref, v_ref, go_ref, d_ref, mo_ref, vo_ref, recv, send_sems, recv_sems, local_sem):
        me = _me()
        local = pltpu.make_async_copy(g_ref.at[me], recv.at[me], local_sem)
        local.start()
        sends = []
        for k in range(1, N_DEV):
            dev, pj = _peer(k)
            cp = pltpu.make_async_remote_copy(src_ref=g_ref.at[pj], dst_ref=recv.at[me], send_sem=send_sems.at[k - 1],
                                              recv_sem=recv_sems.at[k - 1], device_id=dev, device_id_type=MESH_IDS)
            cp.start()
            sends.append(cp)
        for k in range(1, N_DEV):
            dev, pj = _peer(k)
            pltpu.make_async_remote_copy(src_ref=g_ref.at[pj], dst_ref=recv.at[pj], send_sem=send_sems.at[k - 1],
                                         recv_sem=recv_sems.at[k - 1], device_id=dev, device_id_type=MESH_IDS).wait_recv()
        local.wait()

        def chunk(i, carry):
            r = pl.ds(pl.multiple_of(i * rows, rows), rows)
            g = recv[0, r, :].astype(F32)
            for j in range(1, N_DEV):
                g = g + recv[j, r, :].astype(F32)
            go_ref[r, :] = g
            d_ref[r, :], mo_ref[r, :], vo_ref[r, :] = _adamw(w_ref[r, :], g, m_ref[r, :], v_ref[r, :])
            return carry

        lax.fori_loop(0, R // rows, chunk, 0)
        for cp in sends:
            cp.wait_send()

    return pl.pallas_call(
        body, name=name, in_specs=[ANY_SPEC, VMEM_SPEC, VMEM_SPEC, VMEM_SPEC], out_specs=[VMEM_SPEC] * 4,
        out_shape=[_sds((R, C), F32)] * 4,
        scratch_shapes=[pltpu.VMEM((N_DEV, R, C), BF16), pltpu.SemaphoreType.DMA((N_DEV - 1,)),
                        pltpu.SemaphoreType.DMA((N_DEV - 1,)), pltpu.SemaphoreType.DMA(())],
        compiler_params=pltpu.CompilerParams(vmem_limit_bytes=VMEM_LIMIT),
    )(gblk, w, m, v)


def _reduce_scatter_adamw_two_level(name, gblk, w, m, v):
    _, R, C = gblk.shape
    rows = max(d for d in range(16, min(R, 128) + 1, 16) if R % d == 0)

    def body(g_ref, w_ref, m_ref, v_ref, go_ref, d_ref, mo_ref, vo_ref, from_sib, to_chips, from_chips, own,
             send_sems, recv_sems):
        x_, y_, c_ = lax.axis_index("x"), lax.axis_index("y"), lax.axis_index("c")
        sibling = (x_, y_, 1 - c_)
        chips = [(x_, y_), (1 - x_, y_), (x_, 1 - y_), (1 - x_, 1 - y_)]

        def idx(chip, core):
            return 4 * chip[0] + 2 * chip[1] + core

        def remote(k, src, dst, to):
            return pltpu.make_async_remote_copy(src_ref=src, dst_ref=dst, send_sem=send_sems.at[k],
                                                recv_sem=recv_sems.at[k], device_id=to, device_id_type=MESH_IDS)

        step1 = [remote(q, g_ref.at[idx(chip, 1 - c_)], from_sib.at[q], sibling) for q, chip in enumerate(chips)]
        for cp in step1:
            cp.start()
        for cp in step1:
            cp.wait_recv()

        def chip_sums(i, carry):
            r = pl.ds(pl.multiple_of(i * rows, rows), rows)
            for q, chip in enumerate(chips):
                part = g_ref[idx(chip, c_), r, :].astype(F32) + from_sib[q, r, :].astype(F32)
                if q == 0:
                    own[r, :] = part
                else:
                    to_chips[q - 1, r, :] = part.astype(BF16)
            return carry

        lax.fori_loop(0, R // rows, chip_sums, 0)
        step2 = [remote(4 + j, to_chips.at[j], from_chips.at[j], (*chip, c_)) for j, chip in enumerate(chips[1:])]
        for cp in step2:
            cp.start()
        for cp in step2:
            cp.wait_recv()

        def finish(i, carry):
            r = pl.ds(pl.multiple_of(i * rows, rows), rows)
            g = own[r, :]
            for j in range(3):
                g = g + from_chips[j, r, :].astype(F32)
            go_ref[r, :] = g
            d_ref[r, :], mo_ref[r, :], vo_ref[r, :] = _adamw(w_ref[r, :], g, m_ref[r, :], v_ref[r, :])
            return carry

        lax.fori_loop(0, R // rows, finish, 0)
        for cp in step1 + step2:
            cp.wait_send()

    return pl.pallas_call(
        body, name=name, in_specs=[VMEM_SPEC] * 4, out_specs=[VMEM_SPEC] * 4, out_shape=[_sds((R, C), F32)] * 4,
        scratch_shapes=[pltpu.VMEM((4, R, C), BF16), pltpu.VMEM((3, R, C), BF16), pltpu.VMEM((3, R, C), BF16),
                        pltpu.VMEM((R, C), F32), pltpu.SemaphoreType.DMA((N_DEV - 1,)),
                        pltpu.SemaphoreType.DMA((N_DEV - 1,))],
        compiler_params=pltpu.CompilerParams(vmem_limit_bytes=VMEM_LIMIT),
    )(gblk, w, m, v)


FIRST_WEIGHTS = ["mla_w_dq", "mla_w_uq", "mla_w_dkv", "mla_w_ukv"]
LATE_WEIGHTS = ["mla_w_o", "swa_w_qkv", "swa_w_o", "w_ff1", "w_ff2"]
ROW_SHARDED = {"mla_w_dq", "mla_w_dkv", "mla_w_o", "swa_w_o", "w_ff2"}


def _unblock(name, blocks):
    sh = blocks.shape[1:]
    if name in ROW_SHARDED:
        return jnp.moveaxis(blocks, 0, 1).reshape(sh[0], N_DEV * sh[1], sh[2])
    return jnp.moveaxis(blocks, 0, 2).reshape(sh[0], sh[1], N_DEV * sh[2])


def _block(name, full):
    L, K, N = full.shape
    if name in ROW_SHARDED:
        return jnp.moveaxis(full.reshape(L, N_DEV, K // N_DEV, N), 1, 0)
    return jnp.moveaxis(full.reshape(L, K, N_DEV, N // N_DEV), 2, 0)


def _rot_cols(w):
    half = QK_ROPE // 2
    return jnp.concatenate([-w[..., half:], w[..., :half]], axis=-1)


def _unrot_cols(gw):
    half = QK_ROPE // 2
    return jnp.concatenate([gw[..., half:], -gw[..., :half]], axis=-1)


def _row(v):
    return v.reshape(1, -1)


def _mlp_block_bwd(dx, sv, w1, w2, g, sc, gt):
    dy, du, dgt = _mlp_bwd_a(dx, sv["y2"], sv["rl"], gt, w2)
    dw2 = _matmul_tn("dw_ff2", sv["act"], dy, BF16)
    dw1 = _matmul_tn("dw_ff1", sv["h2"], du, BF16, column_blocks=True)
    dxo, dsh, da = _mlp_bwd_b(du, sv["x1"], dx, w1, g, sc)
    return dxo, dw1, dw2, dsh, da, dgt


def kernel(x, c, positions, w_ada, b_ada, g_mix, g_mlp, mla_w_dq, mla_g_q, mla_w_uq, mla_w_dkv, mla_g_kv, mla_w_ukv, mla_w_o, swa_w_qkv, swa_b_qkv, swa_sinks, swa_w_o, swa_b_o, w_ff1, w_ff2, g_final, loss_target, m_w_ada, m_b_ada, m_g_mix, m_g_mlp, m_mla_w_dq, m_mla_g_q, m_mla_w_uq, m_mla_w_dkv, m_mla_g_kv, m_mla_w_ukv, m_mla_w_o, m_swa_w_qkv, m_swa_b_qkv, m_swa_sinks, m_swa_w_o, m_swa_b_o, m_w_ff1, m_w_ff2, m_g_final, v_w_ada, v_b_ada, v_g_mix, v_g_mlp, v_mla_w_dq, v_mla_g_q, v_mla_w_uq, v_mla_w_dkv, v_mla_g_kv, v_mla_w_ukv, v_mla_w_o, v_swa_w_qkv, v_swa_b_qkv, v_swa_sinks, v_swa_w_o, v_swa_b_o, v_w_ff1, v_w_ff2, v_g_final):
    S, D = x.shape[1], x.shape[2]
    me = _me()
    x0 = x[0]
    target = loss_target[0]
    big_w = dict(mla_w_dq=mla_w_dq, mla_w_uq=mla_w_uq, mla_w_dkv=mla_w_dkv, mla_w_ukv=mla_w_ukv, mla_w_o=mla_w_o,
                 swa_w_qkv=swa_w_qkv, swa_w_o=swa_w_o, w_ff1=w_ff1, w_ff2=w_ff2)
    big_m = dict(mla_w_dq=m_mla_w_dq, mla_w_uq=m_mla_w_uq, mla_w_dkv=m_mla_w_dkv, mla_w_ukv=m_mla_w_ukv,
                 mla_w_o=m_mla_w_o, swa_w_qkv=m_swa_w_qkv, swa_w_o=m_swa_w_o, w_ff1=m_w_ff1, w_ff2=m_w_ff2)
    big_v = dict(mla_w_dq=v_mla_w_dq, mla_w_uq=v_mla_w_uq, mla_w_dkv=v_mla_w_dkv, mla_w_ukv=v_mla_w_ukv,
                 mla_w_o=v_mla_w_o, swa_w_qkv=v_swa_w_qkv, swa_w_o=v_swa_w_o, w_ff1=v_w_ff1, w_ff2=v_w_ff2)
    groups = {"first": FIRST_WEIGHTS}
    wrows = {n: -(-big_w[n].size // (PACK_COLS * 16)) * 16 for n in FIRST_WEIGHTS}
    offs = {g: np.concatenate([[0], np.cumsum([wrows[n] for n in names])]).astype(int) for g, names in groups.items()}

    def as_rows(n, a, lead=()):
        flat = a.reshape(lead + (-1,))
        pad = wrows[n] * PACK_COLS - flat.shape[-1]
        if pad:
            flat = jnp.pad(flat, ((0, 0),) * len(lead) + ((0, pad),))
        return flat.reshape(lead + (wrows[n], PACK_COLS))

    def pack(g, d):
        return jnp.concatenate([as_rows(n, d[n]) for n in groups[g]], axis=0)

    def pack_blocks(g, gfull):
        return jnp.concatenate([as_rows(n, _block(n, gfull[n]).astype(BF16), (N_DEV,)) for n in groups[g]], axis=1)

    def unpack(g, packed, lead=()):
        out = {}
        for i, n in enumerate(groups[g]):
            part = packed[..., int(offs[g][i]):int(offs[g][i + 1]), :].reshape(lead + (-1,))
            out[n] = part[..., :big_w[n].size].reshape(lead + big_w[n].shape)
        return out

    gathered = _all_gather_two_level("gather_weights", pack("first", big_w), BF16)
    wfull = {n: _unblock(n, b) for n, b in unpack("first", gathered, (N_DEV,)).items()}
    w_dq, w_dkv = wfull["mla_w_dq"][0], wfull["mla_w_dkv"][0]
    w_cat = jnp.concatenate([w_dq, w_dkv, _rot_cols(w_dkv[:, KV_LORA:])], axis=1)
    QL = w_dq.shape[1]
    w_uq = wfull["mla_w_uq"][0].reshape(QL, MLA_HEADS, QK_DIM)
    w_uqx = jnp.concatenate([w_uq, _rot_cols(w_uq[..., QK_NOPE:])], axis=-1).reshape(QL, MLA_HEADS * 256)
    w_ukv = wfull["mla_w_ukv"][0]

    L = w_ada.shape[0]
    NC = w_ada.shape[2]
    nbq, nbo = swa_b_qkv.shape[1], swa_b_o.shape[1]
    cpad = -(-(D + nbq + nbo) // 1024) * 1024
    cpack = jnp.pad(jnp.concatenate([c[0], swa_b_qkv[0], swa_b_o[0]]), (0, cpad - (D + nbq + nbo))).reshape(8, cpad // 8)
    call = _all_gather("gather_c", cpack, F32).reshape(N_DEV, cpad)
    c_all = call[:, :D]
    b_qkv_full = call[:, D:D + nbq].reshape(1, N_DEV * nbq)
    b_o_full = call[:, D + nbq:D + nbq + nbo].reshape(1, N_DEV * nbo)
    mod_cols = _ada_fwd(c_all, w_ada)
    mpad = -(-(L * NC) // 1024) * 1024
    mod_send = jnp.pad(jnp.moveaxis(mod_cols, 1, 0).reshape(N_DEV, L * NC), ((0, 0), (0, mpad - L * NC)))
    mod_mine = _all_to_all("exchange_mod", mod_send.reshape(N_DEV, 8, mpad // 8)).reshape(N_DEV, mpad)[:, :L * NC]
    mod = jnp.moveaxis(mod_mine.reshape(N_DEV, L, NC), 0, 1).reshape(L, N_DEV * NC) + b_ada
    mods = mod.reshape(L, 6, 1, D)

    half = QK_ROPE // 2
    inv_freq = ROPE_THETA ** (-jnp.arange(half, dtype=F32) / half)
    ang = positions[0].astype(F32)[:, None] * inv_freq
    cos = jnp.concatenate([jnp.cos(ang), jnp.cos(ang)], axis=-1)
    sin = jnp.concatenate([jnp.sin(ang), jnp.sin(ang)], axis=-1)

    T_ATT = ATT_TILE
    zero_bias = jnp.zeros((1, D), F32)

    sh1, sc1, gt1, sh2, sc2, gt2 = [mods[0, i] for i in range(6)]
    gm0, gp0 = _row(g_mix[0]), _row(g_mlp[0])
    h1, cqp, cq, ckvp, ckv, q, k, v, vt = _mla_in_fwd(x0, cos, sin, gm0, sc1, sh1, w_cat, mla_g_q, w_uqx, mla_g_kv,
                                                      w_ukv, ROW_TILE)
    o0, lse0, gathered = _mla_attn_fwd(q, k, vt, ATT_TILE_FWD, [big_w[n].astype(BF16) for n in LATE_WEIGHTS])
    wfull = {n: _unblock(n, b) for n, b in zip(LATE_WEIGHTS, gathered)}
    w_o_mla, w_qkv, w_o_swa = wfull["mla_w_o"][0], wfull["swa_w_qkv"][0], wfull["swa_w_o"][0]
    ff1, ff2 = wfull["w_ff1"], wfull["w_ff2"]
    y1, x1, h2 = _attn_out_fwd(o0, x0, w_o_mla, zero_bias, gt1, gp0, sc2, sh2)
    rl0, act0, y2, x2 = _mlp_fwd(h2, x1, ff1[0], ff2[0], gt2)
    sv0 = dict(y2=y2, rl=rl0, act=act0, h2=h2, x1=x1)

    th1, tc1, tg1, th2, tc2, tg2 = [mods[1, i] for i in range(6)]
    gm1, gp1 = _row(g_mix[1]), _row(g_mlp[1])
    h3, sq, sk, svv = _swa_in_fwd(x2, gm1, tc1, th1, w_qkv, b_qkv_full)
    o1, lse1, y3, x3, h4 = _swa_attn_fwd(sq, sk, svv, swa_sinks, x2, w_o_swa, b_o_full, tg1, gp1, tc2, th2)
    rl1, act1, y4, dx4, loss_part, dg_final = _mlp_fwd_loss(h4, x3, ff1[1], ff2[1], tg2, target, _row(g_final))
    sv1 = dict(y2=y4, rl=rl1, act=act1, h2=h4, x1=x3)

    dx3, dw1_1, dw2_1, dsh2_1, da2_1, dgt2_1 = _mlp_block_bwd(dx4, sv1, ff1[1], ff2[1], gp1, tc2, tg2)
    dsq, dsk, dsv, dsink, dy, dgt1_1, db_o = _swa_attn_bwd(sq, sk, svv, lse1, swa_sinks, dx3, y3, o1, tg1, w_o_swa)
    dw_o_swa = _matmul_tn("dw_o", o1, dy, BF16)
    dqkv, dx2, db_qkv, dsh1_1, da1_1 = _swa_in_bwd(dsq, dsk, dsv, x2, dx3, w_qkv, gm1, tc1)
    dw_qkv = _matmul_tn("dw_qkv", h3, dqkv, BF16)

    dx1, dw1_0, dw2_0, dsh2_0, da2_0, dgt2_0 = _mlp_block_bwd(dx2, sv0, ff1[0], ff2[0], gp0, sc2, gt2)
    dy, do, dl, dgt1_0, _ = _attn_out_bwd(dx1, y1, o0, gt1, w_o_mla, MLA_HEADS)
    dw_o_mla = _matmul_tn("dw_o", o0, dy, BF16)
    tb = min(T_ATT, S)
    delta = dl[:MLA_HEADS].reshape(MLA_HEADS, S // tb, 1, tb)
    glate = dict(mla_w_o=dw_o_mla[None], swa_w_qkv=dw_qkv[None], swa_w_o=dw_o_swa[None],
                 w_ff2=jnp.stack([dw2_0, dw2_1]))
    gblocks = {n: _block(n, g).astype(BF16) for n, g in glate.items()}
    gblocks["w_ff1"] = jnp.stack([dw1_0, dw1_1], axis=1)
    lse_rows = (lse0 * LOG2E).reshape(MLA_HEADS, S // tb, 1, tb)
    dq, dk, dv, recv = _mla_attn_bwd(q, k, v, do, lse_rows, delta, T_ATT, [gblocks[n] for n in LATE_WEIGHTS])
    late = {n: _sum_adamw(r, big_w[n], big_m[n], big_v[n]) for n, r in zip(LATE_WEIGHTS, recv)}
    dqx, dkv, dcat, dx0, dg_q, dg_kv, dsh1_0, da1_0 = _mla_in_bwd(
        dq, dk, dv, cos, sin, cqp, ckvp, x0, dx1, w_uqx, mla_g_q, w_ukv, mla_g_kv, w_cat, gm0, sc1)
    dw_uqx = _matmul_tn("dw_uq", cq, dqx).reshape(QL, MLA_HEADS, 256)
    dw_ukv = _matmul_tn("dw_ukv", ckv, dkv)
    dw_cat = _matmul_tn("dw_down", h1, dcat)
    dw_uq = jnp.concatenate([dw_uqx[..., :QK_NOPE], dw_uqx[..., 128:192] + _unrot_cols(dw_uqx[..., 192:256])],
                            axis=-1).reshape(QL, MLA_HEADS * QK_DIM)
    o_kr = QL + KV_LORA
    dw_dkv = jnp.concatenate([dw_cat[:, QL:o_kr],
                              dw_cat[:, o_kr:o_kr + QK_ROPE] + _unrot_cols(dw_cat[:, o_kr + QK_ROPE:])], axis=1)

    gfirst = dict(mla_w_dq=dw_cat[None, :, :QL], mla_w_uq=dw_uq[None], mla_w_dkv=dw_dkv[None], mla_w_ukv=dw_ukv[None])
    first = _reduce_scatter_adamw_two_level("grad_exchange_adamw", pack_blocks("first", gfirst), pack("first", big_w),
                                  pack("first", big_m), pack("first", big_v))
    big_g, big_d, big_nm, big_nv = ({**unpack("first", first[j]), **{n: late[n][j] for n in LATE_WEIGHTS}}
                                    for j in range(4))

    dmod = jnp.stack([
        jnp.concatenate([dsh1_0, gm0 * da1_0, dgt1_0, dsh2_0, gp0 * da2_0, dgt2_0], axis=1),
        jnp.concatenate([dsh1_1, gm1 * da1_1, dgt1_1, dsh2_1, gp1 * da2_1, dgt2_1], axis=1)]).reshape(-1)
    dg_mix = jnp.concatenate([(1.0 + sc1) * da1_0, (1.0 + tc1) * da1_1], axis=1).reshape(-1)
    dg_mlp = jnp.concatenate([(1.0 + sc2) * da2_0, (1.0 + tc2) * da2_1], axis=1).reshape(-1)
    parts = [loss_part.reshape(-1), dmod, dg_mix, dg_mlp, dg_q.reshape(-1), dg_kv.reshape(-1), dsink.reshape(-1),
             dg_final.reshape(-1), db_qkv.reshape(-1), db_o.reshape(-1)]
    soffs = np.concatenate([[0], np.cumsum([p.size for p in parts])])
    spad = -(-int(soffs[-1]) // 1024) * 1024
    spack = jnp.pad(jnp.concatenate(parts), (0, spad - int(soffs[-1]))).reshape(8, spad // 8)
    sall = _all_gather("gather_small_grads", spack, F32)
    ssum = _sum_devices(sall).reshape(-1)
    tot = [ssum[int(soffs[i]):int(soffs[i + 1])] for i in range(len(parts))]
    loss = tot[0][0]
    nsink = swa_sinks.shape[1]
    small_g = dict(b_ada=tot[1].reshape(b_ada.shape), g_mix=tot[2].reshape(g_mix.shape), g_mlp=tot[3].reshape(g_mlp.shape),
                   mla_g_q=tot[4].reshape(mla_g_q.shape), mla_g_kv=tot[5].reshape(mla_g_kv.shape),
                   swa_sinks=tot[6][:nsink].reshape(swa_sinks.shape), g_final=tot[7].reshape(g_final.shape),
                   swa_b_qkv=lax.dynamic_slice(tot[8], (me * nbq,), (nbq,)).reshape(swa_b_qkv.shape),
                   swa_b_o=lax.dynamic_slice(tot[9], (me * nbo,), (nbo,)).reshape(swa_b_o.shape))
    small_w = dict(b_ada=b_ada, g_mix=g_mix, g_mlp=g_mlp, mla_g_q=mla_g_q, mla_g_kv=mla_g_kv, swa_sinks=swa_sinks,
                   g_final=g_final, swa_b_qkv=swa_b_qkv, swa_b_o=swa_b_o)
    small_m = dict(b_ada=m_b_ada, g_mix=m_g_mix, g_mlp=m_g_mlp, mla_g_q=m_mla_g_q, mla_g_kv=m_mla_g_kv,
                   swa_sinks=m_swa_sinks, g_final=m_g_final, swa_b_qkv=m_swa_b_qkv, swa_b_o=m_swa_b_o)
    small_v = dict(b_ada=v_b_ada, g_mix=v_g_mix, g_mlp=v_g_mlp, mla_g_q=v_mla_g_q, mla_g_kv=v_mla_g_kv,
                   swa_sinks=v_swa_sinks, g_final=v_g_final, swa_b_qkv=v_swa_b_qkv, swa_b_o=v_swa_b_o)
    SMALL = list(small_w)
    woffs = np.concatenate([[0], np.cumsum([small_w[n].size for n in SMALL])])
    wpad = -(-int(woffs[-1]) // 1024) * 1024

    def spack_of(d):
        flat = jnp.concatenate([d[n].reshape(-1) for n in SMALL])
        return jnp.pad(flat, (0, wpad - int(woffs[-1]))).reshape(8, wpad // 8)

    sm = _adamw_small(spack_of(small_w), spack_of(small_g), spack_of(small_m), spack_of(small_v))
    small_d, small_nm, small_nv = (
        {n: a.reshape(-1)[int(woffs[i]):int(woffs[i + 1])].reshape(small_w[n].shape) for i, n in enumerate(SMALL)}
        for a in sm)

    b_off = int(soffs[1])
    dmod_all = sall.reshape(N_DEV, -1)[:, b_off:b_off + L * N_DEV * NC].reshape(N_DEV, L, N_DEV * NC)
    dmod_cols = jnp.moveaxis(lax.dynamic_slice_in_dim(dmod_all, me * NC, NC, axis=2), 0, 1)
    ada_g, ada_d, ada_nm, ada_nv = _ada_bwd_adamw(c_all.T, dmod_cols, w_ada, m_w_ada, v_w_ada)

    order = ["w_ada", "b_ada", "g_mix", "g_mlp", "mla_w_dq", "mla_g_q", "mla_w_uq", "mla_w_dkv", "mla_g_kv",
             "mla_w_ukv", "mla_w_o", "swa_w_qkv", "swa_b_qkv", "swa_sinks", "swa_w_o", "swa_b_o", "w_ff1", "w_ff2", "g_final"]

    def collect(ada, big, small):
        return [ada if n == "w_ada" else (big[n] if n in big else small[n]) for n in order]

    return (loss, dx0.reshape(x.shape), *collect(ada_g, big_g, small_g), *collect(ada_d, big_d, small_d),
            *collect(ada_nm, big_nm, small_nm), *collect(ada_nv, big_nv, small_nv))
```

```python
import jax
import jax.numpy as jnp
import numpy as np
from jax import lax
from jax.experimental import pallas as pl
from jax.experimental.pallas import tpu as pltpu

F32 = jnp.float32
BF16 = jnp.bfloat16
MESH_IDS = pl.DeviceIdType.MESH
N_DEV = 8

MLA_HEADS = 8
QK_NOPE = 128
QK_ROPE = 64
QK_DIM = QK_NOPE + QK_ROPE
V_DIM = 128
KV_LORA = 256
ROPE_THETA = 10000.0
SWA_HEADS = 16
SWA_KV_HEADS = 4
SWA_GROUP = SWA_HEADS // SWA_KV_HEADS
SWA_HEAD_DIM = 64
WINDOW = 128
EPS = 1e-6
LOG2E = 1.4426950408889634

ADAM_LR = 0.001
ADAM_B1 = 0.9
ADAM_B2 = 0.999
ADAM_EPS = 1e-08
ADAM_WD = 0.01
ADAM_STEP = 10

PACK_COLS = 1024
VMEM_LIMIT = 56 << 20
ROW_TILE = 512
ROW_TILE_WIDE = 512
ROW_TILE_BWD = 512
ATT_TILE = 512
ATT_TILE_FWD = 1024
TN_TOKENS = 4096


def _dot(a, b):
    return jnp.dot(a, b, preferred_element_type=F32)


def _dot_nt(a, b):
    return lax.dot_general(a, b, (((1,), (1,)), ((), ())), preferred_element_type=F32)


def _dot_tn(a, b):
    return lax.dot_general(a, b, (((0,), (0,)), ((), ())), preferred_element_type=F32)


def _rstd(x):
    return lax.rsqrt(jnp.mean(x * x, axis=-1, keepdims=True) + EPS)


def _rms_bwd(dn, n, r):
    return r * (dn - n * jnp.mean(dn * n, axis=-1, keepdims=True))


def _modulate(x, g, sc, sh):
    r = _rstd(x)
    return ((x * r) * g) * (1.0 + sc) + sh


def _modulate_bwd(dh, x, g, sc):
    r = _rstd(x)
    n = x * r
    dsh = jnp.sum(dh, axis=0, keepdims=True)
    da = jnp.sum(dh * n, axis=0, keepdims=True)
    dx = _rms_bwd(dh * (g * (1.0 + sc)), n, r)
    return dx, dsh, da


def _acc(ref, val, i):
    @pl.when(i == 0)
    def _():
        ref[...] = val

    @pl.when(i != 0)
    def _():
        ref[...] += val


def _row_spec(shape, tm):
    nd = len(shape)
    return pl.BlockSpec(tuple(shape[:nd - 2]) + (tm, shape[-1]), lambda i: (0,) * (nd - 2) + (i, 0))


def _resident_spec(shape, single_buffer):
    nd = len(shape)
    if single_buffer:
        return pl.BlockSpec(tuple(shape), lambda i: (0,) * nd, pipeline_mode=pl.Buffered(1))
    return pl.BlockSpec(tuple(shape), lambda i: (0,) * nd)


def _rowcall(name, body, tokens, tm, row_in, full_in, row_out, acc_out=()):
    tm = min(tm, tokens)
    in_specs = [_row_spec(a.shape, tm) for a in row_in] + [_resident_spec(a.shape, True) for a in full_in]
    row_specs = [s[1] if isinstance(s, tuple) else _row_spec(s.shape, tm) for s in row_out]
    row_out = [s[0] if isinstance(s, tuple) else s for s in row_out]
    out_specs = row_specs + [_resident_spec(s.shape, False) for s in acc_out]
    return pl.pallas_call(
        body, name=name, grid=(tokens // tm,), in_specs=in_specs, out_specs=out_specs,
        out_shape=list(row_out) + list(acc_out),
        compiler_params=pltpu.CompilerParams(dimension_semantics=("arbitrary",), vmem_limit_bytes=VMEM_LIMIT),
    )(*row_in, *full_in)


def _sds(shape, dtype):
    return jax.ShapeDtypeStruct(tuple(shape), dtype)


def _mla_in_fwd(x, cos, sin, g, sc, sh, w_cat, g_q, w_uqx, g_kv, w_ukv, t):
    S, D = x.shape
    QL = g_q.shape[1]
    H = MLA_HEADS
    t = min(t, S)

    def body(x_ref, cos_ref, sin_ref, g_ref, sc_ref, sh_ref, wcat_ref, gq_ref, wuqx_ref, gkv_ref, wukv_ref,
             h_ref, cqp_ref, cq_ref, ckvp_ref, ckv_ref, q_ref, k_ref, v_ref, vt_ref):
        cs, sn = cos_ref[...], sin_ref[...]
        hb = _modulate(x_ref[...], g_ref[...], sc_ref[...], sh_ref[...]).astype(BF16)
        h_ref[...] = hb
        low = _dot(hb, wcat_ref[...])
        cqp = low[:, :QL]
        cqp_ref[...] = cqp
        cq = ((cqp * _rstd(cqp)) * gq_ref[...]).astype(BF16)
        cq_ref[...] = cq
        ckvp = low[:, QL:QL + KV_LORA]
        ckvp_ref[...] = ckvp
        ckv = ((ckvp * _rstd(ckvp)) * gkv_ref[...]).astype(BF16)
        ckv_ref[...] = ckv
        o = QL + KV_LORA
        kr = (low[:, o:o + QK_ROPE] * cs + low[:, o + QK_ROPE:o + 2 * QK_ROPE] * sn).astype(BF16)
        qx = _dot(cq, wuqx_ref[...])
        kv = _dot(ckv, wukv_ref[...])
        for hd in range(H):
            b = hd * 256
            q_ref[hd, :, 0:QK_NOPE] = qx[:, b:b + QK_NOPE].astype(BF16)
            q_ref[hd, :, QK_NOPE:QK_DIM] = (qx[:, b + 128:b + 192] * cs + qx[:, b + 192:b + 256] * sn).astype(BF16)
            k_ref[hd, :, 0:QK_NOPE] = kv[:, b:b + QK_NOPE].astype(BF16)
            k_ref[hd, :, QK_NOPE:QK_DIM] = kr
            vh = kv[:, b + 128:b + 256]
            v_ref[hd] = vh.astype(BF16)
            vt_ref[hd, 0, 0:V_DIM, :] = vh.T.astype(BF16)
            vt_ref[hd, 0, V_DIM:2 * V_DIM, :] = jnp.ones((V_DIM, x_ref.shape[0]), BF16)

    vt_spec = pl.BlockSpec((H, 1, 2 * V_DIM, t), lambda i: (0, i, 0, 0))
    return _rowcall(
        "mla_in_fwd", body, S, t, [x, cos, sin], [g, sc, sh, w_cat, g_q, w_uqx, g_kv, w_ukv],
        [_sds((S, D), BF16), _sds((S, QL), F32), _sds((S, QL), BF16), _sds((S, KV_LORA), F32), _sds((S, KV_LORA), BF16),
         _sds((H, S, QK_DIM), BF16), _sds((H, S, QK_DIM), BF16), _sds((H, S, V_DIM), BF16),
         (_sds((H, S // t, 2 * V_DIM, t), BF16), vt_spec)])


def _mla_attn_fwd(q, k, vt, t, sends):
    H, S, DQ = q.shape
    DV = V_DIM
    vb = vt.shape[-1]
    t = max(min(t, S), vb)
    nb = S // t
    scale = QK_DIM ** -0.5
    c2 = scale * LOG2E

    ns = len(sends)

    def body(q_ref, k_ref, vt_ref, *rest):
        send_refs, (o_ref, lse_ref), gath_refs = rest[:ns], rest[ns:ns + 2], rest[ns + 2:2 * ns + 2]
        m_s, acc_s, s_buf, send_sems, recv_sems, local_sems = rest[2 * ns + 2:]
        hd, qi = pl.program_id(0), pl.program_id(1)

        def gather():
            return _exchange_copies([lambda j, r=r: r for r in send_refs], gath_refs, send_sems, recv_sems, local_sems)

        @pl.when((hd == 0) & (qi == 0))
        def _():
            _start_exchange(gather())

        m_s[...] = jnp.full_like(m_s, -jnp.inf)
        acc_s[...] = jnp.zeros_like(acc_s)

        def scores(j, slot):
            rows = pl.ds(pl.multiple_of(j * t, t), t)
            s_buf[slot] = _dot_nt(k_ref[0, rows, :], q_ref[0])

        nvb = t // vb

        def update(s, j, blocks, cols):
            m_prev = m_s[:, cols]
            m_new = jnp.maximum(m_prev, jnp.max(s, axis=0, keepdims=True))
            alpha = jnp.exp2((m_prev - m_new) * c2)
            pb = jnp.exp2((s - m_new) * c2).astype(BF16)
            acc = alpha * acc_s[:, cols]
            for n, u in enumerate(blocks):
                acc = acc + _dot(vt_ref[0, j * nvb + u], pb[n * vb:(n + 1) * vb, :])
            acc_s[:, cols] = acc
            m_s[:, cols] = m_new

        def causal(s, shape):
            key = lax.broadcasted_iota(jnp.int32, shape, 0)
            qry = lax.broadcasted_iota(jnp.int32, shape, 1)
            return jnp.where(key <= qry, s, -jnp.inf)

        def absorb(j, slot, diagonal):
            if not diagonal:
                update(s_buf[slot], j, range(nvb), slice(None))
            elif nvb % 2:
                update(causal(s_buf[slot], (t, t)), j, range(nvb), slice(None))
            else:
                half = t // 2
                update(causal(s_buf[slot, :half], (half, t)), j, range(nvb // 2), slice(None))
                update(causal(s_buf[slot, half:, half:], (half, half)), j, range(nvb // 2, nvb), slice(half, t))

        def pair(i, carry):
            j = 2 * i
            scores(j + 1, 1)
            absorb(j, 0, False)
            scores(j + 2, 0)
            absorb(j + 1, 1, False)
            return carry

        scores(0, 0)
        lax.fori_loop(0, qi // 2, pair, 0)

        @pl.when(qi % 2 == 0)
        def _():
            absorb(qi, 0, True)

        @pl.when(qi % 2 == 1)
        def _():
            scores(qi, 1)
            absorb(qi - 1, 0, False)
            absorb(qi, 1, True)

        acc = acc_s[...]
        o_ref[...] = (acc[:DV] / acc[DV:]).T.astype(BF16)
        lse_ref[0, 0] = m_s[...] * scale + jnp.log(acc[DV:DV + 1])

        @pl.when((hd == H - 1) & (qi == nb - 1))
        def _():
            _finish_exchange(gather())

    outs = pl.pallas_call(
        body, name="mla_attn_fwd", grid=(H, nb),
        in_specs=[pl.BlockSpec((1, t, DQ), lambda h, i: (h, i, 0)),
                  pl.BlockSpec((1, S, DQ), lambda h, i: (h, 0, 0)),
                  pl.BlockSpec((1, S // vb, 2 * DV, vb), lambda h, i: (h, 0, 0, 0))] + [ANY_SPEC] * ns,
        out_specs=[pl.BlockSpec((t, DV), lambda h, i: (i, h)),
                   pl.BlockSpec((1, 1, 1, t), lambda h, i: (h, i, 0, 0))] + [ANY_SPEC] * ns,
        out_shape=[_sds((S, H * DV), BF16), _sds((H, nb, 1, t), F32)]
        + [_sds((N_DEV,) + a.shape, a.dtype) for a in sends],
        scratch_shapes=[pltpu.VMEM((1, t), F32), pltpu.VMEM((2 * DV, t), F32), pltpu.VMEM((2, t, t), F32)]
        + _comm_sems(ns),
        compiler_params=pltpu.CompilerParams(dimension_semantics=("arbitrary", "arbitrary"),
                                             vmem_limit_bytes=VMEM_LIMIT),
    )(q, k, vt, *sends)
    return outs[0], outs[1], outs[2:]


def _attn_out_fwd(o, x, w_o, b_o, gt, g, sc, sh):
    S, D = x.shape

    def body(o_ref, x_ref, wo_ref, bo_ref, gt_ref, g_ref, sc_ref, sh_ref, y_ref, x1_ref, h_ref):
        y = _dot(o_ref[...], wo_ref[...]) + bo_ref[...]
        y_ref[...] = y.astype(BF16)
        x1 = x_ref[...] + gt_ref[...] * y
        x1_ref[...] = x1
        h_ref[...] = _modulate(x1, g_ref[...], sc_ref[...], sh_ref[...]).astype(BF16)

    return _rowcall("attn_out_fwd", body, S, ROW_TILE, [o, x], [w_o, b_o, gt, g, sc, sh],
                    [_sds((S, D), BF16), _sds((S, D), F32), _sds((S, D), BF16)])


def _mlp_fwd(h, x, w1, w2, gt):
    S, D = x.shape
    FF = w1.shape[1]

    def body(h_ref, x_ref, w1_ref, w2_ref, gt_ref, rl_ref, act_ref, y_ref, x2_ref):
        rl = jnp.maximum(_dot(h_ref[...], w1_ref[...]), 0.0)
        rl_ref[...] = rl.astype(BF16)
        act = (rl * rl).astype(BF16)
        act_ref[...] = act
        y = _dot(act, w2_ref[...])
        y_ref[...] = y.astype(BF16)
        x2_ref[...] = x_ref[...] + gt_ref[...] * y

    return _rowcall("mlp_fwd", body, S, ROW_TILE_WIDE, [h, x], [w1, w2, gt],
                    [_sds((S, FF), BF16), _sds((S, FF), BF16), _sds((S, D), BF16), _sds((S, D), F32)])


def _final_norm_loss(xv, target, g, d_model):
    r = _rstd(xv)
    n = xv * r
    err = n * g - target
    part = 0.5 * jnp.sum(jnp.mean(err * err, axis=-1, keepdims=True), axis=0, keepdims=True)
    dout = err / d_model
    return part, _rms_bwd(dout * g, n, r), jnp.sum(dout * n, axis=0, keepdims=True)


def _mlp_fwd_loss(h, x, w1, w2, gt, target, g_final):
    S, D = x.shape
    FF = w1.shape[1]

    def body(h_ref, x_ref, t_ref, w1_ref, w2_ref, gt_ref, g_ref, rl_ref, act_ref, y_ref, dx_ref, loss_ref, dg_ref):
        i = pl.program_id(0)
        rl = jnp.maximum(_dot(h_ref[...], w1_ref[...]), 0.0)
        rl_ref[...] = rl.astype(BF16)
        act = (rl * rl).astype(BF16)
        act_ref[...] = act
        y = _dot(act, w2_ref[...])
        y_ref[...] = y.astype(BF16)
        part, dx, dg = _final_norm_loss(x_ref[...] + gt_ref[...] * y, t_ref[...], g_ref[...], D)
        dx_ref[...] = dx
        _acc(loss_ref, jnp.broadcast_to(part, loss_ref.shape), i)
        _acc(dg_ref, dg, i)

    return _rowcall("mlp_fwd_loss", body, S, ROW_TILE_WIDE, [h, x, target], [w1, w2, gt, g_final],
                    [_sds((S, FF), BF16), _sds((S, FF), BF16), _sds((S, D), BF16), _sds((S, D), F32)],
                    [_sds((1, 128), F32), _sds((1, D), F32)])


def _swa_in_fwd(x, g, sc, sh, w_qkv, b_qkv):
    S, D = x.shape
    NQ = SWA_HEADS * SWA_HEAD_DIM
    NK = SWA_KV_HEADS * SWA_HEAD_DIM

    def body(x_ref, g_ref, sc_ref, sh_ref, w_ref, b_ref, h_ref, q_ref, k_ref, v_ref):
        hb = _modulate(x_ref[...], g_ref[...], sc_ref[...], sh_ref[...]).astype(BF16)
        h_ref[...] = hb
        qkv = _dot(hb, w_ref[...]) + b_ref[...]
        q_ref[...] = qkv[:, :NQ].astype(BF16)
        k_ref[...] = qkv[:, NQ:NQ + NK].astype(BF16)
        v_ref[...] = qkv[:, NQ + NK:].astype(BF16)

    return _rowcall("swa_in_fwd", body, S, ROW_TILE, [x], [g, sc, sh, w_qkv, b_qkv],
                    [_sds((S, D), BF16), _sds((S, NQ), BF16), _sds((S, NK), BF16), _sds((S, NK), BF16)])


def _alibi_slope(head):
    return float(np.float32(2.0 ** (-8.0 * (head + 1) / SWA_HEADS)))


def _swa_geometry(n):
    W, G = WINDOW, SWA_GROUP
    key = lax.broadcasted_iota(jnp.int32, (2 * W, G * W), 0)
    qry = lax.broadcasted_iota(jnp.int32, (2 * W, G * W), 1) & (W - 1)
    dist = W + qry - key
    valid = (dist >= 0) & (dist < W) & ((n > 0) | (key >= W))
    return dist.astype(F32), valid


def _swa_group(kh, q_ref, sink_ref):
    W, G, Dh = WINDOW, SWA_GROUP, SWA_HEAD_DIM
    heads = [kh * G + g for g in range(G)]
    q4 = jnp.concatenate([q_ref[:, h * Dh:(h + 1) * Dh] for h in heads], axis=0)
    slopes = jnp.concatenate([jnp.full((1, W), _alibi_slope(h), F32) for h in heads], axis=1)
    sinks = jnp.concatenate([jnp.broadcast_to(sink_ref[:, h:h + 1], (1, W)) for h in heads], axis=1)
    return heads, q4, slopes, sinks


def _swa_band_specs(W, nb, cols):
    prev = pl.BlockSpec((W, cols), lambda n: (jnp.maximum(jnp.minimum(n, nb - 1) - 1, 0), 0))
    cur = pl.BlockSpec((W, cols), lambda n: (jnp.minimum(n, nb - 1), 0))
    return prev, cur


def _swa_attn_fwd(q, k, v, sinks, x, w_o, b_o, gt, g, sc, sh):
    S, NQ = q.shape
    NK = k.shape[1]
    D = x.shape[1]
    W, Dh, G = WINDOW, SWA_HEAD_DIM, SWA_GROUP
    nb = S // W

    def body(q_ref, kp_ref, kc_ref, vp_ref, vc_ref, sink_ref, x_ref, wo_ref, bo_ref, gt_ref, g_ref, sc_ref, sh_ref,
             o_ref, lse_ref, y_ref, x1_ref, h_ref):
        distf, valid = _swa_geometry(pl.program_id(0))
        kband = jnp.concatenate([kp_ref[...], kc_ref[...]], axis=0)
        vband_t = jnp.concatenate([vp_ref[...], vc_ref[...]], axis=0).astype(F32).T.astype(BF16)
        outs = []
        for kh in range(SWA_KV_HEADS):
            kb = kband[:, kh * Dh:(kh + 1) * Dh]
            vbt = vband_t[kh * Dh:(kh + 1) * Dh, :]
            heads, q4, slopes, sinks = _swa_group(kh, q_ref, sink_ref)
            s = _dot_nt(kb, q4) * (Dh ** -0.5) - slopes * distf
            s = jnp.where(valid, s, -jnp.inf)
            m = jnp.maximum(jnp.max(s, axis=0, keepdims=True), sinks)
            p = jnp.exp(s - m)
            denom = jnp.sum(p, axis=0, keepdims=True) + jnp.exp(sinks - m)
            out4 = _dot(vbt, (p * (1.0 / denom)).astype(BF16))
            lse4 = m + jnp.log(denom)
            for g, h in enumerate(heads):
                outs.append(out4[:, g * W:(g + 1) * W])
                lse_ref[h:h + 1, :] = lse4[:, g * W:(g + 1) * W]
        ob = jnp.concatenate(outs, axis=0).T.astype(BF16)
        o_ref[...] = ob
        y = _dot(ob, wo_ref[...]) + bo_ref[...]
        y_ref[...] = y.astype(BF16)
        x1 = x_ref[...] + gt_ref[...] * y
        x1_ref[...] = x1
        h_ref[...] = _modulate(x1, g_ref[...], sc_ref[...], sh_ref[...]).astype(BF16)

    kprev, kcur = _swa_band_specs(W, nb, NK)
    blk = lambda cols: pl.BlockSpec((W, cols), lambda n: (n, 0))
    row = pl.BlockSpec((1, D), lambda n: (0, 0))
    return pl.pallas_call(
        body, name="swa_attn_fwd", grid=(nb,),
        in_specs=[blk(NQ), kprev, kcur, kprev, kcur, pl.BlockSpec((1, SWA_HEADS), lambda n: (0, 0)), blk(D),
                  pl.BlockSpec(w_o.shape, lambda n: (0, 0), pipeline_mode=pl.Buffered(1)), row, row, row, row, row],
        out_specs=[blk(NQ), pl.BlockSpec((SWA_HEADS, W), lambda n: (0, n)), blk(D), blk(D), blk(D)],
        out_shape=[_sds((S, NQ), BF16), _sds((SWA_HEADS, S), F32), _sds((S, D), BF16), _sds((S, D), F32),
                   _sds((S, D), BF16)],
        compiler_params=pltpu.CompilerParams(dimension_semantics=("arbitrary",), vmem_limit_bytes=VMEM_LIMIT),
    )(q, k, k, v, v, sinks, x, w_o, b_o, gt, g, sc, sh)


def _mlp_bwd_a(dx, y, rl, gt, w2):
    S, D = dx.shape
    FF = rl.shape[1]

    def body(dx_ref, y_ref, rl_ref, gt_ref, w2_ref, dy_ref, du_ref, dgt_ref):
        i = pl.program_id(0)
        dxv = dx_ref[...]
        _acc(dgt_ref, jnp.sum(dxv * y_ref[...].astype(F32), axis=0, keepdims=True), i)
        dy = (dxv * gt_ref[...]).astype(BF16)
        dy_ref[...] = dy
        dact = _dot_nt(dy, w2_ref[...])
        du_ref[...] = (dact * (2.0 * rl_ref[...].astype(F32))).astype(BF16)

    return _rowcall("mlp_bwd_a", body, S, ROW_TILE_BWD, [dx, y, rl], [gt, w2],
                    [_sds((S, D), BF16), _sds((S, FF), BF16)], [_sds((1, D), F32)])


def _mlp_bwd_b(du, x, dx, w1, g, sc):
    S, D = x.shape

    def body(du_ref, x_ref, dx_ref, w1_ref, g_ref, sc_ref, dxo_ref, dsh_ref, da_ref):
        i = pl.program_id(0)
        dh = _dot_nt(du_ref[...], w1_ref[...])
        dxn, dsh, da = _modulate_bwd(dh, x_ref[...], g_ref[...], sc_ref[...])
        dxo_ref[...] = dx_ref[...] + dxn
        _acc(dsh_ref, dsh, i)
        _acc(da_ref, da, i)

    return _rowcall("mlp_bwd_b", body, S, ROW_TILE_BWD, [du, x, dx], [w1, g, sc],
                    [_sds((S, D), F32)], [_sds((1, D), F32), _sds((1, D), F32)])


def _attn_out_bwd(dx, y, o, gt, w_o, n_heads):
    S, D = dx.shape
    NO = o.shape[1]
    dh = NO // n_heads
    member = (jnp.arange(NO)[None, :] // dh == jnp.arange(16)[:, None]).astype(BF16)

    def body(dx_ref, y_ref, o_ref, gt_ref, wo_ref, mem_ref, dy_ref, do_ref, dl_ref, dgt_ref, dbo_ref):
        i = pl.program_id(0)
        dxv = dx_ref[...]
        _acc(dgt_ref, jnp.sum(dxv * y_ref[...].astype(F32), axis=0, keepdims=True), i)
        dy = dxv * gt_ref[...]
        _acc(dbo_ref, jnp.sum(dy, axis=0, keepdims=True), i)
        dyb = dy.astype(BF16)
        dy_ref[...] = dyb
        do = _dot_nt(dyb, wo_ref[...])
        do_ref[...] = do.astype(BF16)
        prod = do * o_ref[...].astype(F32)
        hi = prod.astype(BF16)
        lo = (prod - hi.astype(F32)).astype(BF16)
        dl_ref[...] = _dot_nt(mem_ref[...], hi) + _dot_nt(mem_ref[...], lo)

    tm = min(ROW_TILE, S)
    return _rowcall("attn_out_bwd", body, S, ROW_TILE, [dx, y, o], [gt, w_o, member],
                    [_sds((S, D), BF16), _sds((S, NO), BF16),
                     (_sds((16, S), F32), pl.BlockSpec((16, tm), lambda i: (0, i)))],
                    [_sds((1, D), F32), _sds((1, D), F32)])


def _mla_attn_bwd(q, k, v, do, lse, delta, t, gblks):
    H, S, DQ = q.shape
    DV = V_DIM
    t = min(t, S // 2)
    tk = 2 * t
    nq, nk = S // t, S // tk
    scale = QK_DIM ** -0.5
    c2 = scale * LOG2E

    ng = len(gblks)

    def body(q_ref, k_ref, v_ref, do_ref, lse_ref, dl_ref, *rest):
        g_refs, (dq_ref, dk_ref, dv_ref), recv_refs = rest[:ng], rest[ng:ng + 3], rest[ng + 3:2 * ng + 3]
        dk_s, dv_s, s_buf, dp_buf, send_sems, recv_sems, local_sems = rest[2 * ng + 3:]
        hd, kj = pl.program_id(0), pl.program_id(1)

        def scatter():
            return _exchange_copies([lambda j, r=r: r.at[j] for r in g_refs], recv_refs, send_sems, recv_sems,
                                    local_sems)

        @pl.when((hd == 0) & (kj == 0))
        def _():
            _start_exchange(scatter())

        @pl.when(kj == 0)
        def _():
            dq_ref[...] = jnp.zeros_like(dq_ref)

        dk_s[...] = jnp.zeros_like(dk_s)
        dv_s[...] = jnp.zeros_like(dv_s)

        def products(i, slot, keys=tk):
            rows = pl.ds(pl.multiple_of(i * t, t), t)
            s_buf[slot, :keys] = _dot_nt(k_ref[0, :keys], q_ref[0, rows, :])
            dp_buf[slot, :keys] = _dot_nt(v_ref[0, :keys], do_ref[rows, :])

        def absorb(i, slot, diagonal, keys=tk):
            rows = pl.ds(pl.multiple_of(i * t, t), t)
            qb, dob = q_ref[0, rows, :], do_ref[rows, :]
            p = jnp.exp2(s_buf[slot, :keys] * c2 - lse_ref[0, i])
            if diagonal is not None:
                key = lax.broadcasted_iota(jnp.int32, (keys, t), 0)
                qry = lax.broadcasted_iota(jnp.int32, (keys, t), 1) + diagonal * t
                p = jnp.where(key <= qry, p, 0.0)
            dv_s[:keys] += _dot(p.astype(BF16), dob)
            ds = (p * (dp_buf[slot, :keys] - dl_ref[0, i])).astype(BF16)
            dk_s[:keys] += _dot(ds, qb)
            dq_ref[0, rows, :] += _dot_tn(ds, k_ref[0, :keys])

        first = 2 * kj + 2
        n_off = nq - first

        def pair(i, carry):
            u = 2 * i
            products(first + u + 1, 1)
            absorb(first + u, 0, None)
            products(jnp.where(u + 2 < n_off, first + u + 2, 2 * kj + 1), 0)
            absorb(first + u + 1, 1, None)
            return carry

        products(jnp.where(n_off > 0, first, 2 * kj + 1), 0)
        lax.fori_loop(0, n_off // 2, pair, 0)
        products(2 * kj, 1, t)
        absorb(2 * kj + 1, 0, 1)
        absorb(2 * kj, 1, 0, t)

        dk_ref[0] = (dk_s[...] * scale).astype(BF16)
        dv_ref[0] = dv_s[...].astype(BF16)

        @pl.when((hd == H - 1) & (kj == nk - 1))
        def _():
            _finish_exchange(scatter())

    rowspec = pl.BlockSpec((1, nq, 1, t), lambda h, j: (h, 0, 0, 0))
    outs = pl.pallas_call(
        body, name="mla_attn_bwd", grid=(H, nk),
        in_specs=[pl.BlockSpec((1, S, DQ), lambda h, j: (h, 0, 0)),
                  pl.BlockSpec((1, tk, DQ), lambda h, j: (h, j, 0)),
                  pl.BlockSpec((1, tk, DV), lambda h, j: (h, j, 0)),
                  pl.BlockSpec((S, DV), lambda h, j: (0, h)), rowspec, rowspec] + [ANY_SPEC] * ng,
        out_specs=[pl.BlockSpec((1, S, DQ), lambda h, j: (h, 0, 0)),
                   pl.BlockSpec((1, tk, DQ), lambda h, j: (h, j, 0)),
                   pl.BlockSpec((1, tk, DV), lambda h, j: (h, j, 0))] + [ANY_SPEC] * ng,
        out_shape=[_sds((H, S, DQ), F32), _sds((H, S, DQ), BF16), _sds((H, S, DV), BF16)]
        + [_sds(g.shape, g.dtype) for g in gblks],
        scratch_shapes=[pltpu.VMEM((tk, DQ), F32), pltpu.VMEM((tk, DV), F32), pltpu.VMEM((2, tk, t), F32),
                        pltpu.VMEM((2, tk, t), F32)] + _comm_sems(ng),
        compiler_params=pltpu.CompilerParams(dimension_semantics=("arbitrary", "arbitrary"),
                                             vmem_limit_bytes=VMEM_LIMIT),
    )(q, k, v, do, lse, delta, *gblks)
    return outs[0], outs[1], outs[2], outs[3:]


def _swa_attn_bwd(q, k, v, lse, sinks, dx, y, o, gt, w_o):
    S, NQ = q.shape
    NK = k.shape[1]
    D = dx.shape[1]
    W, Dh, G = WINDOW, SWA_HEAD_DIM, SWA_GROUP
    nb = S // W
    member = (jnp.arange(NQ)[None, :] // Dh == jnp.arange(SWA_HEADS)[:, None]).astype(BF16)

    def body(q_ref, kp_ref, kc_ref, vp_ref, vc_ref, lse_ref, sink_ref, dx_ref, y_ref, o_ref, gt_ref, wo_ref, mem_ref,
             dq_ref, dk_ref, dv_ref, dsink_ref, dy_ref, dgt_ref, dbo_ref, dkc_s, dvc_s):
        n = pl.program_id(0)

        @pl.when(n == 0)
        def _():
            dkc_s[...] = jnp.zeros_like(dkc_s)
            dvc_s[...] = jnp.zeros_like(dvc_s)
            dsink_ref[...] = jnp.zeros_like(dsink_ref)

        @pl.when(n < nb)
        def _():
            dxv = dx_ref[...]
            _acc(dgt_ref, jnp.sum(dxv * y_ref[...].astype(F32), axis=0, keepdims=True), n)
            dy = dxv * gt_ref[...]
            _acc(dbo_ref, jnp.sum(dy, axis=0, keepdims=True), n)
            dyb = dy.astype(BF16)
            dy_ref[...] = dyb
            do = _dot_nt(dyb, wo_ref[...])
            dob = do.astype(BF16)
            prod = do * o_ref[...].astype(F32)
            hi = prod.astype(BF16)
            lo = (prod - hi.astype(F32)).astype(BF16)
            dl = _dot_nt(mem_ref[...], hi) + _dot_nt(mem_ref[...], lo)
            distf, valid = _swa_geometry(n)
            kband = jnp.concatenate([kp_ref[...], kc_ref[...]], axis=0)
            vband = jnp.concatenate([vp_ref[...], vc_ref[...]], axis=0)
            kband_t = kband.astype(F32).T.astype(BF16)
            dq_t = []
            for kh in range(SWA_KV_HEADS):
                ck = slice(kh * Dh, (kh + 1) * Dh)
                kb, vb, kbt = kband[:, ck], vband[:, ck], kband_t[ck, :]
                heads, q4, slopes, sinks = _swa_group(kh, q_ref, sink_ref)
                do4 = jnp.concatenate([dob[:, h * Dh:(h + 1) * Dh] for h in heads], axis=0)
                lse4 = jnp.concatenate([lse_ref[h:h + 1, :] for h in heads], axis=1)
                dl4 = jnp.concatenate([dl[h:h + 1, :] for h in heads], axis=1)
                s = _dot_nt(kb, q4) * (Dh ** -0.5) - slopes * distf
                p = jnp.where(valid, jnp.exp(s - lse4), 0.0)
                dvb = _dot(p.astype(BF16), do4)
                dp = _dot_nt(vb, do4)
                dsb = ((p * (dp - dl4)) * (Dh ** -0.5)).astype(BF16)
                dq4 = _dot(kbt, dsb)
                dkb = _dot(dsb, q4)
                dsk4 = jnp.exp(sinks - lse4) * dl4
                for g, h in enumerate(heads):
                    dq_t.append(dq4[:, g * W:(g + 1) * W])
                    dsink_ref[:, h:h + 1] += -jnp.sum(dsk4[:, g * W:(g + 1) * W], axis=1, keepdims=True)
                dk_ref[:, ck] = (dkc_s[:, ck] + dkb[:W]).astype(BF16)
                dv_ref[:, ck] = (dvc_s[:, ck] + dvb[:W]).astype(BF16)
                dkc_s[:, ck] = dkb[W:]
                dvc_s[:, ck] = dvb[W:]
            dq_ref[...] = jnp.concatenate(dq_t, axis=0).T.astype(BF16)

        @pl.when(n == nb)
        def _():
            dk_ref[...] = dkc_s[...].astype(BF16)
            dv_ref[...] = dvc_s[...].astype(BF16)

    kprev, kcur = _swa_band_specs(W, nb, NK)
    qspec = lambda cols: pl.BlockSpec((W, cols), lambda n: (jnp.minimum(n, nb - 1), 0))
    kvout = pl.BlockSpec((W, NK), lambda n: (jnp.maximum(n - 1, 0), 0))
    rowspec = pl.BlockSpec((SWA_HEADS, W), lambda n: (0, jnp.minimum(n, nb - 1)))
    fixed = lambda shape: pl.BlockSpec(shape, lambda n: (0, 0))
    return pl.pallas_call(
        body, name="swa_attn_bwd", grid=(nb + 1,),
        in_specs=[qspec(NQ), kprev, kcur, kprev, kcur, rowspec, pl.BlockSpec((1, SWA_HEADS), lambda n: (0, 0)),
                  qspec(D), qspec(D), qspec(NQ), fixed((1, D)), fixed(w_o.shape), fixed(member.shape)],
        out_specs=[qspec(NQ), kvout, kvout, fixed((1, 128)), qspec(D), fixed((1, D)), fixed((1, D))],
        out_shape=[_sds((S, NQ), BF16), _sds((S, NK), BF16), _sds((S, NK), BF16), _sds((1, 128), F32),
                   _sds((S, D), BF16), _sds((1, D), F32), _sds((1, D), F32)],
        scratch_shapes=[pltpu.VMEM((W, NK), F32), pltpu.VMEM((W, NK), F32)],
        compiler_params=pltpu.CompilerParams(dimension_semantics=("arbitrary",), vmem_limit_bytes=VMEM_LIMIT),
    )(q, k, k, v, v, lse, sinks, dx, y, o, gt, w_o, member)


def _swa_in_bwd(dq, dk, dv, x, dx, w_qkv, g, sc):
    S, D = x.shape
    N = w_qkv.shape[1]

    def body(dq_ref, dk_ref, dv_ref, x_ref, dx_ref, w_ref, g_ref, sc_ref, dqkv_ref, dxo_ref, db_ref, dsh_ref, da_ref):
        i = pl.program_id(0)
        dqkv = jnp.concatenate([dq_ref[...], dk_ref[...], dv_ref[...]], axis=1)
        dqkv_ref[...] = dqkv
        _acc(db_ref, jnp.sum(dqkv.astype(F32), axis=0, keepdims=True), i)
        dh = _dot_nt(dqkv, w_ref[...])
        dxn, dsh, da = _modulate_bwd(dh, x_ref[...], g_ref[...], sc_ref[...])
        dxo_ref[...] = dx_ref[...] + dxn
        _acc(dsh_ref, dsh, i)
        _acc(da_ref, da, i)

    return _rowcall("swa_in_bwd", body, S, ROW_TILE, [dq, dk, dv, x, dx], [w_qkv, g, sc],
                    [_sds((S, N), BF16), _sds((S, D), F32)],
                    [_sds((1, N), F32), _sds((1, D), F32), _sds((1, D), F32)])


def _mla_in_bwd(dq, dk, dv, cos, sin, cqp, ckvp, x, dx, w_uqx, g_q, w_ukv, g_kv, w_cat, g, sc):
    S, D = x.shape
    H = MLA_HEADS
    QL = g_q.shape[1]
    NX = w_uqx.shape[1]
    NC = w_cat.shape[1]

    def body(dq_ref, dk_ref, dv_ref, cos_ref, sin_ref, cqp_ref, ckvp_ref, x_ref, dx_ref,
             wuqx_ref, gq_ref, wukv_ref, gkv_ref, wcat_ref, g_ref, sc_ref,
             dqx_ref, dkv_ref, dcat_ref, dxo_ref, dgq_ref, dgkv_ref, dsh_ref, da_ref):
        i = pl.program_id(0)
        cs, sn = cos_ref[...], sin_ref[...]
        dkr = jnp.zeros(cs.shape, F32)
        for hd in range(H):
            b = hd * 256
            dqh = dq_ref[hd] * (QK_DIM ** -0.5)
            dqx_ref[:, b:b + QK_NOPE] = dqh[:, :QK_NOPE].astype(BF16)
            dqx_ref[:, b + 128:b + 192] = (dqh[:, QK_NOPE:] * cs).astype(BF16)
            dqx_ref[:, b + 192:b + 256] = (dqh[:, QK_NOPE:] * sn).astype(BF16)
            dkh = dk_ref[hd]
            dkv_ref[:, b:b + QK_NOPE] = dkh[:, :QK_NOPE]
            dkv_ref[:, b + 128:b + 256] = dv_ref[hd]
            dkr = dkr + dkh[:, QK_NOPE:].astype(F32)
        dcq = _dot_nt(dqx_ref[...], wuqx_ref[...])
        cqp = cqp_ref[...]
        rq = _rstd(cqp)
        nq = cqp * rq
        _acc(dgq_ref, jnp.sum(dcq * nq, axis=0, keepdims=True), i)
        dcqp = _rms_bwd(dcq * gq_ref[...], nq, rq)
        dckv = _dot_nt(dkv_ref[...], wukv_ref[...])
        ckvp = ckvp_ref[...]
        rk = _rstd(ckvp)
        nk = ckvp * rk
        _acc(dgkv_ref, jnp.sum(dckv * nk, axis=0, keepdims=True), i)
        dckvp = _rms_bwd(dckv * gkv_ref[...], nk, rk)
        dcat_ref[:, :QL] = dcqp.astype(BF16)
        dcat_ref[:, QL:QL + KV_LORA] = dckvp.astype(BF16)
        o = QL + KV_LORA
        dcat_ref[:, o:o + QK_ROPE] = (dkr * cs).astype(BF16)
        dcat_ref[:, o + QK_ROPE:o + 2 * QK_ROPE] = (dkr * sn).astype(BF16)
        dh = _dot_nt(dcat_ref[...], wcat_ref[...])
        dxn, dsh, da = _modulate_bwd(dh, x_ref[...], g_ref[...], sc_ref[...])
        dxo_ref[...] = dx_ref[...] + dxn
        _acc(dsh_ref, dsh, i)
        _acc(da_ref, da, i)

    return _rowcall("mla_in_bwd", body, S, ROW_TILE, [dq, dk, dv, cos, sin, cqp, ckvp, x, dx],
                    [w_uqx, g_q, w_ukv, g_kv, w_cat, g, sc],
                    [_sds((S, NX), BF16), _sds((S, NX), BF16), _sds((S, NC), BF16), _sds((S, D), F32)],
                    [_sds((1, QL), F32), _sds((1, KV_LORA), F32), _sds((1, D), F32), _sds((1, D), F32)])


def _matmul_tn(name, a, b, out_dtype=F32, column_blocks=False):
    S, K = a.shape
    N = b.shape[1]
    tk, tn, ts = min(K, 1024), min(N, 1024), min(S, TN_TOKENS)
    if column_blocks:
        tn = N // N_DEV
    if N % tn:
        tn = 512 if N % 512 == 0 else (384 if N % 384 == 0 else 128)
    if K % tk:
        tk = 512 if K % 512 == 0 else (384 if K % 384 == 0 else 128)
    ns = S // ts

    def body(a_ref, b_ref, o_ref, *scratch):
        acc_ref = scratch[0] if scratch else o_ref
        _acc(acc_ref, _dot_tn(a_ref[...], b_ref[...]), pl.program_id(2))
        if scratch:
            @pl.when(pl.program_id(2) == ns - 1)
            def _():
                o_ref[...] = acc_ref[...].astype(out_dtype)

    if column_blocks:
        out_spec = pl.BlockSpec((None, tk, tn), lambda i, j, s: (j, i, 0))
        out_shape = _sds((N_DEV, K, tn), out_dtype)
    else:
        out_spec = pl.BlockSpec((tk, tn), lambda i, j, s: (i, j))
        out_shape = _sds((K, N), out_dtype)
    return pl.pallas_call(
        body, name=name, grid=(K // tk, N // tn, ns),
        in_specs=[pl.BlockSpec((ts, tk), lambda i, j, s: (s, i)), pl.BlockSpec((ts, tn), lambda i, j, s: (s, j))],
        out_specs=out_spec, out_shape=out_shape,
        scratch_shapes=[] if out_dtype == F32 else [pltpu.VMEM((tk, tn), F32)],
        compiler_params=pltpu.CompilerParams(dimension_semantics=("parallel", "parallel", "arbitrary"),
                                             vmem_limit_bytes=VMEM_LIMIT),
    )(a, b)


def _silu(c):
    return c * jax.nn.sigmoid(c)


def _ada_fwd(c_all, w_ada):
    L, D, NC = w_ada.shape

    def body(c_ref, w_ref, o_ref):
        cond = _silu(c_ref[...]).astype(BF16)
        o_ref[0] = _dot(cond, w_ref[0].astype(BF16))

    return pl.pallas_call(
        body, name="ada_fwd", grid=(L,),
        in_specs=[pl.BlockSpec(c_all.shape, lambda l: (0, 0)), pl.BlockSpec((1, D, NC), lambda l: (l, 0, 0))],
        out_specs=pl.BlockSpec((1, N_DEV, NC), lambda l: (l, 0, 0)),
        out_shape=_sds((L, N_DEV, NC), F32),
        compiler_params=pltpu.CompilerParams(dimension_semantics=("arbitrary",), vmem_limit_bytes=VMEM_LIMIT),
    )(c_all, w_ada)


def _adamw(w, g, m, v):
    m = ADAM_B1 * m + (1.0 - ADAM_B1) * g
    v = ADAM_B2 * v + (1.0 - ADAM_B2) * (g * g)
    m_hat = m / (1.0 - ADAM_B1 ** ADAM_STEP)
    v_hat = v / (1.0 - ADAM_B2 ** ADAM_STEP)
    delta = -ADAM_LR * (m_hat / (jnp.sqrt(v_hat) + ADAM_EPS) + ADAM_WD * w)
    return delta, m, v


def _ada_bwd_adamw(c_all_t, dmod_cols, w, m, v):
    L, D, NC = w.shape
    tr = min(D, 256)

    def body(ct_ref, dm_ref, w_ref, m_ref, v_ref, g_ref, d_ref, mo_ref, vo_ref):
        cond_t = _silu(ct_ref[...])
        dm = dm_ref[0]
        g = cond_t[:, 0:1] * dm[0:1, :]
        for b in range(1, N_DEV):
            g = g + cond_t[:, b:b + 1] * dm[b:b + 1, :]
        g_ref[0] = g
        d_ref[0], mo_ref[0], vo_ref[0] = _adamw(w_ref[0], g, m_ref[0], v_ref[0])

    wspec = pl.BlockSpec((1, tr, NC), lambda l, r: (l, r, 0))
    return pl.pallas_call(
        body, name="ada_bwd_adamw", grid=(L, D // tr),
        in_specs=[pl.BlockSpec((tr, N_DEV), lambda l, r: (r, 0)),
                  pl.BlockSpec((1, N_DEV, NC), lambda l, r: (l, 0, 0)), wspec, wspec, wspec],
        out_specs=[wspec] * 4, out_shape=[_sds(w.shape, F32)] * 4,
        compiler_params=pltpu.CompilerParams(dimension_semantics=("parallel", "parallel"), vmem_limit_bytes=VMEM_LIMIT),
    )(c_all_t, dmod_cols, w, m, v)


def _sum_devices(x):
    def body(x_ref, o_ref):
        s = x_ref[0]
        for j in range(1, N_DEV):
            s = s + x_ref[j]
        o_ref[...] = s

    return pl.pallas_call(body, name="sum_devices", out_shape=_sds(x.shape[1:], F32))(x)


def _adamw_small(w, g, m, v):
    def body(w_ref, g_ref, m_ref, v_ref, d_ref, mo_ref, vo_ref):
        d_ref[...], mo_ref[...], vo_ref[...] = _adamw(w_ref[...], g_ref[...], m_ref[...], v_ref[...])

    return pl.pallas_call(body, name="adamw_small", out_shape=[_sds(w.shape, F32)] * 3)(w, g, m, v)


def _me():
    return lax.axis_index("x") * 4 + lax.axis_index("y") * 2 + lax.axis_index("c")


def _peer(k):
    x, y, c = lax.axis_index("x"), lax.axis_index("y"), lax.axis_index("c")
    px = 1 - x if k & 4 else x
    py = 1 - y if k & 2 else y
    pc = 1 - c if k & 1 else c
    return (px, py, pc), px * 4 + py * 2 + pc


VMEM_SPEC = pl.BlockSpec(memory_space=pltpu.VMEM)
ANY_SPEC = pl.BlockSpec(memory_space=pl.ANY)
def _comm_sems(n):
    return [pltpu.SemaphoreType.DMA((n * (N_DEV - 1),)), pltpu.SemaphoreType.DMA((n * (N_DEV - 1),)),
            pltpu.SemaphoreType.DMA((n,))]


def _exchange_copies(srcs_of, dst_refs, send_sems, recv_sems, local_sems):
    me = _me()
    local, sends, recvs = [], [], []
    for a, (src_of, dst_ref) in enumerate(zip(srcs_of, dst_refs)):
        local.append(pltpu.make_async_copy(src_of(me), dst_ref.at[me], local_sems.at[a]))
        for k in range(1, N_DEV):
            dev, pj = _peer(k)
            i = a * (N_DEV - 1) + k - 1
            sems = dict(send_sem=send_sems.at[i], recv_sem=recv_sems.at[i], device_id=dev, device_id_type=MESH_IDS)
            sends.append(pltpu.make_async_remote_copy(src_ref=src_of(pj), dst_ref=dst_ref.at[me], **sems))
            recvs.append(pltpu.make_async_remote_copy(src_ref=src_of(pj), dst_ref=dst_ref.at[pj], **sems))
    return local, sends, recvs


def _start_exchange(copies):
    local, sends, _ = copies
    for cp in local + sends:
        cp.start()


def _finish_exchange(copies):
    local, sends, recvs = copies
    for cp in recvs:
        cp.wait_recv()
    for cp in sends:
        cp.wait_send()
    for cp in local:
        cp.wait()


def _sum_adamw(recv, w, m, v):
    shape = w.shape
    C = shape[-1]
    R = w.size // C
    rows = max(d for d in range(16, min(R, 512) + 1, 16) if R % d == 0 and d * C <= 256 * 1024)

    def body(r_ref, w_ref, m_ref, v_ref, go_ref, d_ref, mo_ref, vo_ref):
        g = r_ref[0].astype(F32)
        for j in range(1, N_DEV):
            g = g + r_ref[j].astype(F32)
        go_ref[...] = g
        d_ref[...], mo_ref[...], vo_ref[...] = _adamw(w_ref[...], g, m_ref[...], v_ref[...])

    spec = pl.BlockSpec((rows, C), lambda i: (i, 0))
    outs = pl.pallas_call(
        body, name="sum_adamw", grid=(R // rows,),
        in_specs=[pl.BlockSpec((N_DEV, rows, C), lambda i: (0, i, 0)), spec, spec, spec],
        out_specs=[spec] * 4, out_shape=[_sds((R, C), F32)] * 4,
        compiler_params=pltpu.CompilerParams(dimension_semantics=("parallel",), vmem_limit_bytes=VMEM_LIMIT),
    )(recv.reshape(N_DEV, R, C), w.reshape(R, C), m.reshape(R, C), v.reshape(R, C))
    return [o.reshape(shape) for o in outs]


def _all_gather(name, x, out_dtype):
    R, C = x.shape
    cast = out_dtype != x.dtype

    def body(x_ref, out_ref, buf, send_sems, recv_sems, local_sem):
        me = _me()
        if cast:
            buf[...] = x_ref[...].astype(out_dtype)
            src = buf
        else:
            src = x_ref
        local = pltpu.make_async_copy(src, out_ref.at[me], local_sem)
        local.start()
        sends = []
        for k in range(1, N_DEV):
            dev, _ = _peer(k)
            cp = pltpu.make_async_remote_copy(src_ref=src, dst_ref=out_ref.at[me], send_sem=send_sems.at[k - 1],
                                              recv_sem=recv_sems.at[k - 1], device_id=dev, device_id_type=MESH_IDS)
            cp.start()
            sends.append(cp)
        for k in range(1, N_DEV):
            dev, pj = _peer(k)
            pltpu.make_async_remote_copy(src_ref=src, dst_ref=out_ref.at[pj], send_sem=send_sems.at[k - 1],
                                         recv_sem=recv_sems.at[k - 1], device_id=dev, device_id_type=MESH_IDS).wait_recv()
        for cp in sends:
            cp.wait_send()
        local.wait()

    return pl.pallas_call(
        body, name=name, in_specs=[VMEM_SPEC], out_specs=ANY_SPEC, out_shape=_sds((N_DEV, R, C), out_dtype),
        scratch_shapes=[pltpu.VMEM((R, C) if cast else (8, 128), out_dtype),
                        pltpu.SemaphoreType.DMA((N_DEV - 1,)), pltpu.SemaphoreType.DMA((N_DEV - 1,)),
                        pltpu.SemaphoreType.DMA(())],
        compiler_params=pltpu.CompilerParams(vmem_limit_bytes=VMEM_LIMIT),
    )(x)


def _all_gather_two_level(name, x, out_dtype):
    R, C = x.shape

    def body(x_ref, out_ref, buf, send_sems, recv_sems, local_sem):
        x_, y_, c_ = lax.axis_index("x"), lax.axis_index("y"), lax.axis_index("c")
        me, sibling = (x_, y_, c_), (x_, y_, 1 - c_)
        chips = [(1 - x_, y_), (x_, 1 - y_), (1 - x_, 1 - y_)]
        buf[...] = x_ref[...].astype(out_dtype)

        def slot(px, py, pc):
            return out_ref.at[4 * px + 2 * py + pc]

        def copy(k, block, to, src=None):
            return pltpu.make_async_remote_copy(src_ref=slot(*block) if src is None else src, dst_ref=slot(*block),
                                                send_sem=send_sems.at[k], recv_sem=recv_sems.at[k], device_id=to,
                                                device_id_type=MESH_IDS)

        mine = pltpu.make_async_copy(buf, slot(*me), local_sem)
        mine.start()
        first = [copy(0, me, sibling, src=buf)] + [copy(1 + j, me, (*chip, c_), src=buf) for j, chip in enumerate(chips)]
        for cp in first:
            cp.start()
        passed = [copy(4 + j, (*chip, c_), sibling) for j, chip in enumerate(chips)]
        for j, chip in enumerate(chips):
            copy(1 + j, (*chip, c_), me).wait_recv()
            passed[j].start()
        copy(0, sibling, me).wait_recv()
        for j, chip in enumerate(chips):
            copy(4 + j, (*chip, 1 - c_), me).wait_recv()
        for cp in first + passed:
            cp.wait_send()
        mine.wait()

    return pl.pallas_call(
        body, name=name, in_specs=[VMEM_SPEC], out_specs=ANY_SPEC, out_shape=_sds((N_DEV, R, C), out_dtype),
        scratch_shapes=[pltpu.VMEM((R, C), out_dtype), pltpu.SemaphoreType.DMA((N_DEV - 1,)),
                        pltpu.SemaphoreType.DMA((N_DEV - 1,)), pltpu.SemaphoreType.DMA(())],
        compiler_params=pltpu.CompilerParams(vmem_limit_bytes=VMEM_LIMIT),
    )(x)


def _all_to_all(name, x):
    _, R, C = x.shape

    def body(x_ref, out_ref, send_sems, recv_sems, local_sem):
        me = _me()
        local = pltpu.make_async_copy(x_ref.at[me], out_ref.at[me], local_sem)
        local.start()
        sends = []
        for k in range(1, N_DEV):
            dev, pj = _peer(k)
            cp = pltpu.make_async_remote_copy(src_ref=x_ref.at[pj], dst_ref=out_ref.at[me], send_sem=send_sems.at[k - 1],
                                              recv_sem=recv_sems.at[k - 1], device_id=dev, device_id_type=MESH_IDS)
            cp.start()
            sends.append(cp)
        for k in range(1, N_DEV):
            dev, pj = _peer(k)
            pltpu.make_async_remote_copy(src_ref=x_ref.at[pj], dst_ref=out_ref.at[pj], send_sem=send_sems.at[k - 1],
                                         recv_sem=recv_sems.at[k - 1], device_id=dev, device_id_type=MESH_IDS).wait_recv()
        for cp in sends:
            cp.wait_send()
        local.wait()

    return pl.pallas_call(
        body, name=name, in_specs=[VMEM_SPEC], out_specs=VMEM_SPEC, out_shape=_sds(x.shape, x.dtype),
        scratch_shapes=[pltpu.SemaphoreType.DMA((N_DEV - 1,)), pltpu.SemaphoreType.DMA((N_DEV - 1,)),
                        pltpu.SemaphoreType.DMA(())],
    )(x)


def _reduce_scatter_adamw(name, gblk, w, m, v):
    _, R, C = gblk.shape
    rows = 8
    for cand in (136, 128, 80, 64, 40, 32, 16, 8):
        if R % cand == 0:
            rows = cand
            break

    def body(g_ref, w_ref, m_ref, v_ref, go_ref, d_ref, mo_ref, vo_ref, recv, send_sems, recv_sems, local_sem):
        me = _me()
        local = pltpu.make_async_copy(g_ref.at[me], recv.at[me], local_sem)
        local.start()
        sends = []
        for k in range(1, N_DEV):
            dev, pj = _peer(k)
            cp = pltpu.make_async_remote_copy(src_ref=g_ref.at[pj], dst_ref=recv.at[me], send_sem=send_sems.at[k - 1],
                                              recv_sem=recv_sems.at[k - 1], device_id=dev, device_id_type=MESH_IDS)
            cp.start()
            sends.append(cp)
        for k in range(1, N_DEV):
            dev, pj = _peer(k)
            pltpu.make_async_remote_copy(src_ref=g_ref.at[pj], dst_ref=recv.at[pj], send_sem=send_sems.at[k - 1],
                                         recv_sem=recv_sems.at[k - 1], device_id=dev, device_id_type=MESH_IDS).wait_recv()
        local.wait()

        def chunk(i, carry):
            r = pl.ds(pl.multiple_of(i * rows, rows), rows)
            g = recv[0, r, :].astype(F32)
            for j in range(1, N_DEV):
                g = g + recv[j, r, :].astype(F32)
            go_ref[r, :] = g
            d_ref[r, :], mo_ref[r, :], vo_ref[r, :] = _adamw(w_ref[r, :], g, m_ref[r, :], v_ref[r, :])
            return carry

        lax.fori_loop(0, R // rows, chunk, 0)
        for cp in sends:
            cp.wait_send()

    return pl.pallas_call(
        body, name=name, in_specs=[ANY_SPEC, VMEM_SPEC, VMEM_SPEC, VMEM_SPEC], out_specs=[VMEM_SPEC] * 4,
        out_shape=[_sds((R, C), F32)] * 4,
        scratch_shapes=[pltpu.VMEM((N_DEV, R, C), BF16), pltpu.SemaphoreType.DMA((N_DEV - 1,)),
                        pltpu.SemaphoreType.DMA((N_DEV - 1,)), pltpu.SemaphoreType.DMA(())],
        compiler_params=pltpu.CompilerParams(vmem_limit_bytes=VMEM_LIMIT),
    )(gblk, w, m, v)


def _reduce_scatter_adamw_two_level(name, gblk, w, m, v):
    _, R, C = gblk.shape
    rows = max(d for d in range(16, min(R, 128) + 1, 16) if R % d == 0)

    def body(g_ref, w_ref, m_ref, v_ref, go_ref, d_ref, mo_ref, vo_ref, from_sib, to_chips, from_chips, own,
             send_sems, recv_sems):
        x_, y_, c_ = lax.axis_index("x"), lax.axis_index("y"), lax.axis_index("c")
        sibling = (x_, y_, 1 - c_)
        chips = [(x_, y_), (1 - x_, y_), (x_, 1 - y_), (1 - x_, 1 - y_)]

        def idx(chip, core):
            return 4 * chip[0] + 2 * chip[1] + core

        def remote(k, src, dst, to):
            return pltpu.make_async_remote_copy(src_ref=src, dst_ref=dst, send_sem=send_sems.at[k],
                                                recv_sem=recv_sems.at[k], device_id=to, device_id_type=MESH_IDS)

        step1 = [remote(q, g_ref.at[idx(chip, 1 - c_)], from_sib.at[q], sibling) for q, chip in enumerate(chips)]
        for cp in step1:
            cp.start()
        for cp in step1:
            cp.wait_recv()

        def chip_sums(i, carry):
            r = pl.ds(pl.multiple_of(i * rows, rows), rows)
            for q, chip in enumerate(chips):
                part = g_ref[idx(chip, c_), r, :].astype(F32) + from_sib[q, r, :].astype(F32)
                if q == 0:
                    own[r, :] = part
                else:
                    to_chips[q - 1, r, :] = part.astype(BF16)
            return carry

        lax.fori_loop(0, R // rows, chip_sums, 0)
        step2 = [remote(4 + j, to_chips.at[j], from_chips.at[j], (*chip, c_)) for j, chip in enumerate(chips[1:])]
        for cp in step2:
            cp.start()
        for cp in step2:
            cp.wait_recv()

        def finish(i, carry):
            r = pl.ds(pl.multiple_of(i * rows, rows), rows)
            g = own[r, :]
            for j in range(3):
                g = g + from_chips[j, r, :].astype(F32)
            go_ref[r, :] = g
            d_ref[r, :], mo_ref[r, :], vo_ref[r, :] = _adamw(w_ref[r, :], g, m_ref[r, :], v_ref[r, :])
            return carry

        lax.fori_loop(0, R // rows, finish, 0)
        for cp in step1 + step2:
            cp.wait_send()

    return pl.pallas_call(
        body, name=name, in_specs=[VMEM_SPEC] * 4, out_specs=[VMEM_SPEC] * 4, out_shape=[_sds((R, C), F32)] * 4,
        scratch_shapes=[pltpu.VMEM((4, R, C), BF16), pltpu.VMEM((3, R, C), BF16), pltpu.VMEM((3, R, C), BF16),
                        pltpu.VMEM((R, C), F32), pltpu.SemaphoreType.DMA((N_DEV - 1,)),
                        pltpu.SemaphoreType.DMA((N_DEV - 1,))],
        compiler_params=pltpu.CompilerParams(vmem_limit_bytes=VMEM_LIMIT),
    )(gblk, w, m, v)


FIRST_WEIGHTS = ["mla_w_dq", "mla_w_uq", "mla_w_dkv", "mla_w_ukv"]
LATE_WEIGHTS = ["mla_w_o", "swa_w_qkv", "swa_w_o", "w_ff1", "w_ff2"]
ROW_SHARDED = {"mla_w_dq", "mla_w_dkv", "mla_w_o", "swa_w_o", "w_ff2"}


def _unblock(name, blocks):
    sh = blocks.shape[1:]
    if name in ROW_SHARDED:
        return jnp.moveaxis(blocks, 0, 1).reshape(sh[0], N_DEV * sh[1], sh[2])
    return jnp.moveaxis(blocks, 0, 2).reshape(sh[0], sh[1], N_DEV * sh[2])


def _block(name, full):
    L, K, N = full.shape
    if name in ROW_SHARDED:
        return jnp.moveaxis(full.reshape(L, N_DEV, K // N_DEV, N), 1, 0)
    return jnp.moveaxis(full.reshape(L, K, N_DEV, N // N_DEV), 2, 0)


def _rot_cols(w):
    half = QK_ROPE // 2
    return jnp.concatenate([-w[..., half:], w[..., :half]], axis=-1)


def _unrot_cols(gw):
    half = QK_ROPE // 2
    return jnp.concatenate([gw[..., half:], -gw[..., :half]], axis=-1)


def _row(v):
    return v.reshape(1, -1)


def _mlp_block_bwd(dx, sv, w1, w2, g, sc, gt):
    dy, du, dgt = _mlp_bwd_a(dx, sv["y2"], sv["rl"], gt, w2)
    dw2 = _matmul_tn("dw_ff2", sv["act"], dy, BF16)
    dw1 = _matmul_tn("dw_ff1", sv["h2"], du, BF16, column_blocks=True)
    dxo, dsh, da = _mlp_bwd_b(du, sv["x1"], dx, w1, g, sc)
    return dxo, dw1, dw2, dsh, da, dgt


def kernel(x, c, positions, w_ada, b_ada, g_mix, g_mlp, mla_w_dq, mla_g_q, mla_w_uq, mla_w_dkv, mla_g_kv, mla_w_ukv, mla_w_o, swa_w_qkv, swa_b_qkv, swa_sinks, swa_w_o, swa_b_o, w_ff1, w_ff2, g_final, loss_target, m_w_ada, m_b_ada, m_g_mix, m_g_mlp, m_mla_w_dq, m_mla_g_q, m_mla_w_uq, m_mla_w_dkv, m_mla_g_kv, m_mla_w_ukv, m_mla_w_o, m_swa_w_qkv, m_swa_b_qkv, m_swa_sinks, m_swa_w_o, m_swa_b_o, m_w_ff1, m_w_ff2, m_g_final, v_w_ada, v_b_ada, v_g_mix, v_g_mlp, v_mla_w_dq, v_mla_g_q, v_mla_w_uq, v_mla_w_dkv, v_mla_g_kv, v_mla_w_ukv, v_mla_w_o, v_swa_w_qkv, v_swa_b_qkv, v_swa_sinks, v_swa_w_o, v_swa_b_o, v_w_ff1, v_w_ff2, v_g_final):
    S, D = x.shape[1], x.shape[2]
    me = _me()
    x0 = x[0]
    target = loss_target[0]
    big_w = dict(mla_w_dq=mla_w_dq, mla_w_uq=mla_w_uq, mla_w_dkv=mla_w_dkv, mla_w_ukv=mla_w_ukv, mla_w_o=mla_w_o,
                 swa_w_qkv=swa_w_qkv, swa_w_o=swa_w_o, w_ff1=w_ff1, w_ff2=w_ff2)
    big_m = dict(mla_w_dq=m_mla_w_dq, mla_w_uq=m_mla_w_uq, mla_w_dkv=m_mla_w_dkv, mla_w_ukv=m_mla_w_ukv,
                 mla_w_o=m_mla_w_o, swa_w_qkv=m_swa_w_qkv, swa_w_o=m_swa_w_o, w_ff1=m_w_ff1, w_ff2=m_w_ff2)
    big_v = dict(mla_w_dq=v_mla_w_dq, mla_w_uq=v_mla_w_uq, mla_w_dkv=v_mla_w_dkv, mla_w_ukv=v_mla_w_ukv,
                 mla_w_o=v_mla_w_o, swa_w_qkv=v_swa_w_qkv, swa_w_o=v_swa_w_o, w_ff1=v_w_ff1, w_ff2=v_w_ff2)
    groups = {"first": FIRST_WEIGHTS}
    wrows = {n: -(-big_w[n].size // (PACK_COLS * 16)) * 16 for n in FIRST_WEIGHTS}
    offs = {g: np.concatenate([[0], np.cumsum([wrows[n] for n in names])]).astype(int) for g, names in groups.items()}

    def as_rows(n, a, lead=()):
        flat = a.reshape(lead + (-1,))
        pad = wrows[n] * PACK_COLS - flat.shape[-1]
        if pad:
            flat = jnp.pad(flat, ((0, 0),) * len(lead) + ((0, pad),))
        return flat.reshape(lead + (wrows[n], PACK_COLS))

    def pack(g, d):
        return jnp.concatenate([as_rows(n, d[n]) for n in groups[g]], axis=0)

    def pack_blocks(g, gfull):
        return jnp.concatenate([as_rows(n, _block(n, gfull[n]).astype(BF16), (N_DEV,)) for n in groups[g]], axis=1)

    def unpack(g, packed, lead=()):
        out = {}
        for i, n in enumerate(groups[g]):
            part = packed[..., int(offs[g][i]):int(offs[g][i + 1]), :].reshape(lead + (-1,))
            out[n] = part[..., :big_w[n].size].reshape(lead + big_w[n].shape)
        return out

    gathered = _all_gather_two_level("gather_weights", pack("first", big_w), BF16)
    wfull = {n: _unblock(n, b) for n, b in unpack("first", gathered, (N_DEV,)).items()}
    w_dq, w_dkv = wfull["mla_w_dq"][0], wfull["mla_w_dkv"][0]
    w_cat = jnp.concatenate([w_dq, w_dkv, _rot_cols(w_dkv[:, KV_LORA:])], axis=1)
    QL = w_dq.shape[1]
    w_uq = wfull["mla_w_uq"][0].reshape(QL, MLA_HEADS, QK_DIM)
    w_uqx = jnp.concatenate([w_uq, _rot_cols(w_uq[..., QK_NOPE:])], axis=-1).reshape(QL, MLA_HEADS * 256)
    w_ukv = wfull["mla_w_ukv"][0]

    L = w_ada.shape[0]
    NC = w_ada.shape[2]
    nbq, nbo = swa_b_qkv.shape[1], swa_b_o.shape[1]
    cpad = -(-(D + nbq + nbo) // 1024) * 1024
    cpack = jnp.pad(jnp.concatenate([c[0], swa_b_qkv[0], swa_b_o[0]]), (0, cpad - (D + nbq + nbo))).reshape(8, cpad // 8)
    call = _all_gather("gather_c", cpack, F32).reshape(N_DEV, cpad)
    c_all = call[:, :D]
    b_qkv_full = call[:, D:D + nbq].reshape(1, N_DEV * nbq)
    b_o_full = call[:, D + nbq:D + nbq + nbo].reshape(1, N_DEV * nbo)
    mod_cols = _ada_fwd(c_all, w_ada)
    mpad = -(-(L * NC) // 1024) * 1024
    mod_send = jnp.pad(jnp.moveaxis(mod_cols, 1, 0).reshape(N_DEV, L * NC), ((0, 0), (0, mpad - L * NC)))
    mod_mine = _all_to_all("exchange_mod", mod_send.reshape(N_DEV, 8, mpad // 8)).reshape(N_DEV, mpad)[:, :L * NC]
    mod = jnp.moveaxis(mod_mine.reshape(N_DEV, L, NC), 0, 1).reshape(L, N_DEV * NC) + b_ada
    mods = mod.reshape(L, 6, 1, D)

    half = QK_ROPE // 2
    inv_freq = ROPE_THETA ** (-jnp.arange(half, dtype=F32) / half)
    ang = positions[0].astype(F32)[:, None] * inv_freq
    cos = jnp.concatenate([jnp.cos(ang), jnp.cos(ang)], axis=-1)
    sin = jnp.concatenate([jnp.sin(ang), jnp.sin(ang)], axis=-1)

    T_ATT = ATT_TILE
    zero_bias = jnp.zeros((1, D), F32)

    sh1, sc1, gt1, sh2, sc2, gt2 = [mods[0, i] for i in range(6)]
    gm0, gp0 = _row(g_mix[0]), _row(g_mlp[0])
    h1, cqp, cq, ckvp, ckv, q, k, v, vt = _mla_in_fwd(x0, cos, sin, gm0, sc1, sh1, w_cat, mla_g_q, w_uqx, mla_g_kv,
                                                      w_ukv, ROW_TILE)
    o0, lse0, gathered = _mla_attn_fwd(q, k, vt, ATT_TILE_FWD, [big_w[n].astype(BF16) for n in LATE_WEIGHTS])
    wfull = {n: _unblock(n, b) for n, b in zip(LATE_WEIGHTS, gathered)}
    w_o_mla, w_qkv, w_o_swa = wfull["mla_w_o"][0], wfull["swa_w_qkv"][0], wfull["swa_w_o"][0]
    ff1, ff2 = wfull["w_ff1"], wfull["w_ff2"]
    y1, x1, h2 = _attn_out_fwd(o0, x0, w_o_mla, zero_bias, gt1, gp0, sc2, sh2)
    rl0, act0, y2, x2 = _mlp_fwd(h2, x1, ff1[0], ff2[0], gt2)
    sv0 = dict(y2=y2, rl=rl0, act=act0, h2=h2, x1=x1)

    th1, tc1, tg1, th2, tc2, tg2 = [mods[1, i] for i in range(6)]
    gm1, gp1 = _row(g_mix[1]), _row(g_mlp[1])
    h3, sq, sk, svv = _swa_in_fwd(x2, gm1, tc1, th1, w_qkv, b_qkv_full)
    o1, lse1, y3, x3, h4 = _swa_attn_fwd(sq, sk, svv, swa_sinks, x2, w_o_swa, b_o_full, tg1, gp1, tc2, th2)
    rl1, act1, y4, dx4, loss_part, dg_final = _mlp_fwd_loss(h4, x3, ff1[1], ff2[1], tg2, target, _row(g_final))
    sv1 = dict(y2=y4, rl=rl1, act=act1, h2=h4, x1=x3)

    dx3, dw1_1, dw2_1, dsh2_1, da2_1, dgt2_1 = _mlp_block_bwd(dx4, sv1, ff1[1], ff2[1], gp1, tc2, tg2)
    dsq, dsk, dsv, dsink, dy, dgt1_1, db_o = _swa_attn_bwd(sq, sk, svv, lse1, swa_sinks, dx3, y3, o1, tg1, w_o_swa)
    dw_o_swa = _matmul_tn("dw_o", o1, dy, BF16)
    dqkv, dx2, db_qkv, dsh1_1, da1_1 = _swa_in_bwd(dsq, dsk, dsv, x2, dx3, w_qkv, gm1, tc1)
    dw_qkv = _matmul_tn("dw_qkv", h3, dqkv, BF16)

    dx1, dw1_0, dw2_0, dsh2_0, da2_0, dgt2_0 = _mlp_block_bwd(dx2, sv0, ff1[0], ff2[0], gp0, sc2, gt2)
    dy, do, dl, dgt1_0, _ = _attn_out_bwd(dx1, y1, o0, gt1, w_o_mla, MLA_HEADS)
    dw_o_mla = _matmul_tn("dw_o", o0, dy, BF16)
    tb = min(T_ATT, S)
    delta = dl[:MLA_HEADS].reshape(MLA_HEADS, S // tb, 1, tb)
    glate = dict(mla_w_o=dw_o_mla[None], swa_w_qkv=dw_qkv[None], swa_w_o=dw_o_swa[None],
                 w_ff2=jnp.stack([dw2_0, dw2_1]))
    gblocks = {n: _block(n, g).astype(BF16) for n, g in glate.items()}
    gblocks["w_ff1"] = jnp.stack([dw1_0, dw1_1], axis=1)
    lse_rows = (lse0 * LOG2E).reshape(MLA_HEADS, S // tb, 1, tb)
    dq, dk, dv, recv = _mla_attn_bwd(q, k, v, do, lse_rows, delta, T_ATT, [gblocks[n] for n in LATE_WEIGHTS])
    late = {n: _sum_adamw(r, big_w[n], big_m[n], big_v[n]) for n, r in zip(LATE_WEIGHTS, recv)}
    dqx, dkv, dcat, dx0, dg_q, dg_kv, dsh1_0, da1_0 = _mla_in_bwd(
        dq, dk, dv, cos, sin, cqp, ckvp, x0, dx1, w_uqx, mla_g_q, w_ukv, mla_g_kv, w_cat, gm0, sc1)
    dw_uqx = _matmul_tn("dw_uq", cq, dqx).reshape(QL, MLA_HEADS, 256)
    dw_ukv = _matmul_tn("dw_ukv", ckv, dkv)
    dw_cat = _matmul_tn("dw_down", h1, dcat)
    dw_uq = jnp.concatenate([dw_uqx[..., :QK_NOPE], dw_uqx[..., 128:192] + _unrot_cols(dw_uqx[..., 192:256])],
                            axis=-1).reshape(QL, MLA_HEADS * QK_DIM)
    o_kr = QL + KV_LORA
    dw_dkv = jnp.concatenate([dw_cat[:, QL:o_kr],
                              dw_cat[:, o_kr:o_kr + QK_ROPE] + _unrot_cols(dw_cat[:, o_kr + QK_ROPE:])], axis=1)

    gfirst = dict(mla_w_dq=dw_cat[None, :, :QL], mla_w_uq=dw_uq[None], mla_w_dkv=dw_dkv[None], mla_w_ukv=dw_ukv[None])
    first = _reduce_scatter_adamw_two_level("grad_exchange_adamw", pack_blocks("first", gfirst), pack("first", big_w),
                                  pack("first", big_m), pack("first", big_v))
    big_g, big_d, big_nm, big_nv = ({**unpack("first", first[j]), **{n: late[n][j] for n in LATE_WEIGHTS}}
                                    for j in range(4))

    dmod = jnp.stack([
        jnp.concatenate([dsh1_0, gm0 * da1_0, dgt1_0, dsh2_0, gp0 * da2_0, dgt2_0], axis=1),
        jnp.concatenate([dsh1_1, gm1 * da1_1, dgt1_1, dsh2_1, gp1 * da2_1, dgt2_1], axis=1)]).reshape(-1)
    dg_mix = jnp.concatenate([(1.0 + sc1) * da1_0, (1.0 + tc1) * da1_1], axis=1).reshape(-1)
    dg_mlp = jnp.concatenate([(1.0 + sc2) * da2_0, (1.0 + tc2) * da2_1], axis=1).reshape(-1)
    parts = [loss_part.reshape(-1), dmod, dg_mix, dg_mlp, dg_q.reshape(-1), dg_kv.reshape(-1), dsink.reshape(-1),
             dg_final.reshape(-1), db_qkv.reshape(-1), db_o.reshape(-1)]
    soffs = np.concatenate([[0], np.cumsum([p.size for p in parts])])
    spad = -(-int(soffs[-1]) // 1024) * 1024
    spack = jnp.pad(jnp.concatenate(parts), (0, spad - int(soffs[-1]))).reshape(8, spad // 8)
    sall = _all_gather("gather_small_grads", spack, F32)
    ssum = _sum_devices(sall).reshape(-1)
    tot = [ssum[int(soffs[i]):int(soffs[i + 1])] for i in range(len(parts))]
    loss = tot[0][0]
    nsink = swa_sinks.shape[1]
    small_g = dict(b_ada=tot[1].reshape(b_ada.shape), g_mix=tot[2].reshape(g_mix.shape), g_mlp=tot[3].reshape(g_mlp.shape),
                   mla_g_q=tot[4].reshape(mla_g_q.shape), mla_g_kv=tot[5].reshape(mla_g_kv.shape),
                   swa_sinks=tot[6][:nsink].reshape(swa_sinks.shape), g_final=tot[7].reshape(g_final.shape),
                   swa_b_qkv=lax.dynamic_slice(tot[8], (me * nbq,), (nbq,)).reshape(swa_b_qkv.shape),
                   swa_b_o=lax.dynamic_slice(tot[9], (me * nbo,), (nbo,)).reshape(swa_b_o.shape))
    small_w = dict(b_ada=b_ada, g_mix=g_mix, g_mlp=g_mlp, mla_g_q=mla_g_q, mla_g_kv=mla_g_kv, swa_sinks=swa_sinks,
                   g_final=g_final, swa_b_qkv=swa_b_qkv, swa_b_o=swa_b_o)
    small_m = dict(b_ada=m_b_ada, g_mix=m_g_mix, g_mlp=m_g_mlp, mla_g_q=m_mla_g_q, mla_g_kv=m_mla_g_kv,
                   swa_sinks=m_swa_sinks, g_final=m_g_final, swa_b_qkv=m_swa_b_qkv, swa_b_o=m_swa_b_o)
    small_v = dict(b_ada=v_b_ada, g_mix=v_g_mix, g_mlp=v_g_mlp, mla_g_q=v_mla_g_q, mla_g_kv=v_mla_g_kv,
                   swa_sinks=v_swa_sinks, g_final=v_g_final, swa_b_qkv=v_swa_b_qkv, swa_b_o=v_swa_b_o)
    SMALL = list(small_w)
    woffs = np.concatenate([[0], np.cumsum([small_w[n].size for n in SMALL])])
    wpad = -(-int(woffs[-1]) // 1024) * 1024

    def spack_of(d):
        flat = jnp.concatenate([d[n].reshape(-1) for n in SMALL])
        return jnp.pad(flat, (0, wpad - int(woffs[-1]))).reshape(8, wpad // 8)

    sm = _adamw_small(spack_of(small_w), spack_of(small_g), spack_of(small_m), spack_of(small_v))
    small_d, small_nm, small_nv = (
        {n: a.reshape(-1)[int(woffs[i]):int(woffs[i + 1])].reshape(small_w[n].shape) for i, n in enumerate(SMALL)}
        for a in sm)

    b_off = int(soffs[1])
    dmod_all = sall.reshape(N_DEV, -1)[:, b_off:b_off + L * N_DEV * NC].reshape(N_DEV, L, N_DEV * NC)
    dmod_cols = jnp.moveaxis(lax.dynamic_slice_in_dim(dmod_all, me * NC, NC, axis=2), 0, 1)
    ada_g, ada_d, ada_nm, ada_nv = _ada_bwd_adamw(c_all.T, dmod_cols, w_ada, m_w_ada, v_w_ada)

    order = ["w_ada", "b_ada", "g_mix", "g_mlp", "mla_w_dq", "mla_g_q", "mla_w_uq", "mla_w_dkv", "mla_g_kv",
             "mla_w_ukv", "mla_w_o", "swa_w_qkv", "swa_b_qkv", "swa_sinks", "swa_w_o", "swa_b_o", "w_ff1", "w_ff2", "g_final"]

    def collect(ada, big, small):
        return [ada if n == "w_ada" else (big[n] if n in big else small[n]) for n in order]

    return (loss, dx0.reshape(x.shape), *collect(ada_g, big_g, small_g), *collect(ada_d, big_d, small_d),
            *collect(ada_nm, big_nm, small_nm), *collect(ada_nv, big_nv, small_nv))
```

```python
import jax
import jax.numpy as jnp
import numpy as np
from jax import lax
from jax.experimental import pallas as pl
from jax.experimental.pallas import tpu as pltpu

F32 = jnp.float32
BF16 = jnp.bfloat16
MESH_IDS = pl.DeviceIdType.MESH
N_DEV = 8

MLA_HEADS = 8
QK_NOPE = 128
QK_ROPE = 64
QK_DIM = QK_NOPE + QK_ROPE
V_DIM = 128
KV_LORA = 256
ROPE_THETA = 10000.0
SWA_HEADS = 16
SWA_KV_HEADS = 4
SWA_GROUP = SWA_HEADS // SWA_KV_HEADS
SWA_HEAD_DIM = 64
WINDOW = 128
EPS = 1e-6
LOG2E = 1.4426950408889634

ADAM_LR = 0.001
ADAM_B1 = 0.9
ADAM_B2 = 0.999
ADAM_EPS = 1e-08
ADAM_WD = 0.01
ADAM_STEP = 10

PACK_COLS = 1024
VMEM_LIMIT = 56 << 20
ROW_TILE = 512
ROW_TILE_IO = 1024
ROW_TILE_WIDE = 512
ROW_TILE_BWD = 512
ATT_TILE = 512
ATT_TILE_FWD = 1024
TN_TOKENS = 4096


def _dot(a, b):
    return jnp.dot(a, b, preferred_element_type=F32)


def _dot_nt(a, b):
    return lax.dot_general(a, b, (((1,), (1,)), ((), ())), preferred_element_type=F32)


def _dot_tn(a, b):
    return lax.dot_general(a, b, (((0,), (0,)), ((), ())), preferred_element_type=F32)


def _rstd(x):
    return lax.rsqrt(jnp.mean(x * x, axis=-1, keepdims=True) + EPS)


def _rms_bwd(dn, n, r):
    return r * (dn - n * jnp.mean(dn * n, axis=-1, keepdims=True))


def _modulate(x, g, sc, sh):
    r = _rstd(x)
    return ((x * r) * g) * (1.0 + sc) + sh


def _modulate_bwd(dh, x, g, sc):
    r = _rstd(x)
    n = x * r
    dsh = jnp.sum(dh, axis=0, keepdims=True)
    da = jnp.sum(dh * n, axis=0, keepdims=True)
    dx = _rms_bwd(dh * (g * (1.0 + sc)), n, r)
    return dx, dsh, da


def _acc(ref, val, i):
    @pl.when(i == 0)
    def _():
        ref[...] = val

    @pl.when(i != 0)
    def _():
        ref[...] += val


def _row_spec(shape, tm):
    nd = len(shape)
    return pl.BlockSpec(tuple(shape[:nd - 2]) + (tm, shape[-1]), lambda i: (0,) * (nd - 2) + (i, 0))


def _resident_spec(shape, single_buffer):
    nd = len(shape)
    if single_buffer:
        return pl.BlockSpec(tuple(shape), lambda i: (0,) * nd, pipeline_mode=pl.Buffered(1))
    return pl.BlockSpec(tuple(shape), lambda i: (0,) * nd)


def _rowcall(name, body, tokens, tm, row_in, full_in, row_out, acc_out=()):
    tm = min(tm, tokens)
    in_specs = [_row_spec(a.shape, tm) for a in row_in] + [_resident_spec(a.shape, True) for a in full_in]
    row_specs = [s[1] if isinstance(s, tuple) else _row_spec(s.shape, tm) for s in row_out]
    row_out = [s[0] if isinstance(s, tuple) else s for s in row_out]
    out_specs = row_specs + [_resident_spec(s.shape, False) for s in acc_out]
    return pl.pallas_call(
        body, name=name, grid=(tokens // tm,), in_specs=in_specs, out_specs=out_specs,
        out_shape=list(row_out) + list(acc_out),
        compiler_params=pltpu.CompilerParams(dimension_semantics=("arbitrary",), vmem_limit_bytes=VMEM_LIMIT),
    )(*row_in, *full_in)


def _sds(shape, dtype):
    return jax.ShapeDtypeStruct(tuple(shape), dtype)


def _mla_in_fwd(x, cos, sin, g, sc, sh, w_cat, g_q, w_uqx, g_kv, w_ukv, t):
    S, D = x.shape
    QL = g_q.shape[1]
    H = MLA_HEADS
    t = min(t, S)

    def body(x_ref, cos_ref, sin_ref, g_ref, sc_ref, sh_ref, wcat_ref, gq_ref, wuqx_ref, gkv_ref, wukv_ref,
             h_ref, cqp_ref, cq_ref, ckvp_ref, ckv_ref, q_ref, k_ref, v_ref, vt_ref):
        cs, sn = cos_ref[...], sin_ref[...]
        hb = _modulate(x_ref[...], g_ref[...], sc_ref[...], sh_ref[...]).astype(BF16)
        h_ref[...] = hb
        low = _dot(hb, wcat_ref[...])
        cqp = low[:, :QL]
        cqp_ref[...] = cqp
        cq = ((cqp * _rstd(cqp)) * gq_ref[...]).astype(BF16)
        cq_ref[...] = cq
        ckvp = low[:, QL:QL + KV_LORA]
        ckvp_ref[...] = ckvp
        ckv = ((ckvp * _rstd(ckvp)) * gkv_ref[...]).astype(BF16)
        ckv_ref[...] = ckv
        o = QL + KV_LORA
        kr = (low[:, o:o + QK_ROPE] * cs + low[:, o + QK_ROPE:o + 2 * QK_ROPE] * sn).astype(BF16)
        qx = _dot(cq, wuqx_ref[...])
        kv = _dot(ckv, wukv_ref[...])
        for hd in range(H):
            b = hd * 256
            q_ref[hd, :, 0:QK_NOPE] = qx[:, b:b + QK_NOPE].astype(BF16)
            q_ref[hd, :, QK_NOPE:QK_DIM] = (qx[:, b + 128:b + 192] * cs + qx[:, b + 192:b + 256] * sn).astype(BF16)
            k_ref[hd, :, 0:QK_NOPE] = kv[:, b:b + QK_NOPE].astype(BF16)
            k_ref[hd, :, QK_NOPE:QK_DIM] = kr
            vh = kv[:, b + 128:b + 256]
            v_ref[hd] = vh.astype(BF16)
            vt_ref[hd, 0, 0:V_DIM, :] = vh.T.astype(BF16)
            vt_ref[hd, 0, V_DIM:2 * V_DIM, :] = jnp.ones((V_DIM, x_ref.shape[0]), BF16)

    vt_spec = pl.BlockSpec((H, 1, 2 * V_DIM, t), lambda i: (0, i, 0, 0))
    return _rowcall(
        "mla_in_fwd", body, S, t, [x, cos, sin], [g, sc, sh, w_cat, g_q, w_uqx, g_kv, w_ukv],
        [_sds((S, D), BF16), _sds((S, QL), F32), _sds((S, QL), BF16), _sds((S, KV_LORA), F32), _sds((S, KV_LORA), BF16),
         _sds((H, S, QK_DIM), BF16), _sds((H, S, QK_DIM), BF16), _sds((H, S, V_DIM), BF16),
         (_sds((H, S // t, 2 * V_DIM, t), BF16), vt_spec)])


def _mla_attn_fwd(q, k, vt, t, sends):
    H, S, DQ = q.shape
    DV = V_DIM
    vb = vt.shape[-1]
    t = max(min(t, S), vb)
    nb = S // t
    scale = QK_DIM ** -0.5
    c2 = scale * LOG2E

    ns = len(sends)

    def body(q_ref, k_ref, vt_ref, *rest):
        send_refs, (o_ref, lse_ref), gath_refs = rest[:ns], rest[ns:ns + 2], rest[ns + 2:2 * ns + 2]
        m_s, acc_s, s_buf, send_sems, recv_sems, local_sems = rest[2 * ns + 2:]
        hd, qi = pl.program_id(0), pl.program_id(1)

        def gather():
            return _exchange_copies([lambda j, r=r: r for r in send_refs], gath_refs, send_sems, recv_sems, local_sems)

        @pl.when((hd == 0) & (qi == 0))
        def _():
            _start_exchange(gather())

        m_s[...] = jnp.full_like(m_s, -jnp.inf)
        acc_s[...] = jnp.zeros_like(acc_s)

        def scores(j, slot):
            rows = pl.ds(pl.multiple_of(j * t, t), t)
            s_buf[slot] = _dot_nt(k_ref[0, rows, :], q_ref[0])

        nvb = t // vb

        def update(s, j, blocks, cols):
            m_prev = m_s[:, cols]
            m_new = jnp.maximum(m_prev, jnp.max(s, axis=0, keepdims=True))
            alpha = jnp.exp2((m_prev - m_new) * c2)
            pb = jnp.exp2((s - m_new) * c2).astype(BF16)
            acc = alpha * acc_s[:, cols]
            for n, u in enumerate(blocks):
                acc = acc + _dot(vt_ref[0, j * nvb + u], pb[n * vb:(n + 1) * vb, :])
            acc_s[:, cols] = acc
            m_s[:, cols] = m_new

        def causal(s, shape):
            key = lax.broadcasted_iota(jnp.int32, shape, 0)
            qry = lax.broadcasted_iota(jnp.int32, shape, 1)
            return jnp.where(key <= qry, s, -jnp.inf)

        def absorb(j, slot, diagonal):
            if not diagonal:
                update(s_buf[slot], j, range(nvb), slice(None))
            elif nvb % 2:
                update(causal(s_buf[slot], (t, t)), j, range(nvb), slice(None))
            else:
                half = t // 2
                update(causal(s_buf[slot, :half], (half, t)), j, range(nvb // 2), slice(None))
                update(causal(s_buf[slot, half:, half:], (half, half)), j, range(nvb // 2, nvb), slice(half, t))

        def pair(i, carry):
            j = 2 * i
            scores(j + 1, 1)
            absorb(j, 0, False)
            scores(j + 2, 0)
            absorb(j + 1, 1, False)
            return carry

        scores(0, 0)
        lax.fori_loop(0, qi // 2, pair, 0)

        @pl.when(qi % 2 == 0)
        def _():
            absorb(qi, 0, True)

        @pl.when(qi % 2 == 1)
        def _():
            scores(qi, 1)
            absorb(qi - 1, 0, False)
            absorb(qi, 1, True)

        acc = acc_s[...]
        o_ref[...] = (acc[:DV] / acc[DV:]).T.astype(BF16)
        lse_ref[0, 0] = m_s[...] * scale + jnp.log(acc[DV:DV + 1])

        @pl.when((hd == H - 1) & (qi == nb - 1))
        def _():
            _finish_exchange(gather())

    outs = pl.pallas_call(
        body, name="mla_attn_fwd", grid=(H, nb),
        in_specs=[pl.BlockSpec((1, t, DQ), lambda h, i: (h, i, 0)),
                  pl.BlockSpec((1, S, DQ), lambda h, i: (h, 0, 0)),
                  pl.BlockSpec((1, S // vb, 2 * DV, vb), lambda h, i: (h, 0, 0, 0))] + [ANY_SPEC] * ns,
        out_specs=[pl.BlockSpec((t, DV), lambda h, i: (i, h)),
                   pl.BlockSpec((1, 1, 1, t), lambda h, i: (h, i, 0, 0))] + [ANY_SPEC] * ns,
        out_shape=[_sds((S, H * DV), BF16), _sds((H, nb, 1, t), F32)]
        + [_sds((N_DEV,) + a.shape, a.dtype) for a in sends],
        scratch_shapes=[pltpu.VMEM((1, t), F32), pltpu.VMEM((2 * DV, t), F32), pltpu.VMEM((2, t, t), F32)]
        + _comm_sems(ns),
        compiler_params=pltpu.CompilerParams(dimension_semantics=("arbitrary", "arbitrary"),
                                             vmem_limit_bytes=VMEM_LIMIT),
    )(q, k, vt, *sends)
    return outs[0], outs[1], outs[2:]


def _attn_out_fwd(o, x, w_o, b_o, gt, g, sc, sh):
    S, D = x.shape

    def body(o_ref, x_ref, wo_ref, bo_ref, gt_ref, g_ref, sc_ref, sh_ref, y_ref, x1_ref, h_ref):
        y = _dot(o_ref[...], wo_ref[...]) + bo_ref[...]
        y_ref[...] = y.astype(BF16)
        x1 = x_ref[...] + gt_ref[...] * y
        x1_ref[...] = x1
        h_ref[...] = _modulate(x1, g_ref[...], sc_ref[...], sh_ref[...]).astype(BF16)

    return _rowcall("attn_out_fwd", body, S, ROW_TILE_IO, [o, x], [w_o, b_o, gt, g, sc, sh],
                    [_sds((S, D), BF16), _sds((S, D), F32), _sds((S, D), BF16)])


def _mlp_fwd(h, x, w1, w2, gt):
    S, D = x.shape
    FF = w1.shape[1]

    def body(h_ref, x_ref, w1_ref, w2_ref, gt_ref, rl_ref, act_ref, y_ref, x2_ref):
        rl = jnp.maximum(_dot(h_ref[...], w1_ref[...]), 0.0)
        rl_ref[...] = rl.astype(BF16)
        act = (rl * rl).astype(BF16)
        act_ref[...] = act
        y = _dot(act, w2_ref[...])
        y_ref[...] = y.astype(BF16)
        x2_ref[...] = x_ref[...] + gt_ref[...] * y

    return _rowcall("mlp_fwd", body, S, ROW_TILE_WIDE, [h, x], [w1, w2, gt],
                    [_sds((S, FF), BF16), _sds((S, FF), BF16), _sds((S, D), BF16), _sds((S, D), F32)])


def _final_norm_loss(xv, target, g, d_model):
    r = _rstd(xv)
    n = xv * r
    err = n * g - target
    part = 0.5 * jnp.sum(jnp.mean(err * err, axis=-1, keepdims=True), axis=0, keepdims=True)
    dout = err / d_model
    return part, _rms_bwd(dout * g, n, r), jnp.sum(dout * n, axis=0, keepdims=True)


def _mlp_fwd_loss(h, x, w1, w2, gt, target, g_final):
    S, D = x.shape
    FF = w1.shape[1]

    def body(h_ref, x_ref, t_ref, w1_ref, w2_ref, gt_ref, g_ref, rl_ref, act_ref, y_ref, dx_ref, loss_ref, dg_ref):
        i = pl.program_id(0)
        rl = jnp.maximum(_dot(h_ref[...], w1_ref[...]), 0.0)
        rl_ref[...] = rl.astype(BF16)
        act = (rl * rl).astype(BF16)
        act_ref[...] = act
        y = _dot(act, w2_ref[...])
        y_ref[...] = y.astype(BF16)
        part, dx, dg = _final_norm_loss(x_ref[...] + gt_ref[...] * y, t_ref[...], g_ref[...], D)
        dx_ref[...] = dx
        _acc(loss_ref, jnp.broadcast_to(part, loss_ref.shape), i)
        _acc(dg_ref, dg, i)

    return _rowcall("mlp_fwd_loss", body, S, ROW_TILE_WIDE, [h, x, target], [w1, w2, gt, g_final],
                    [_sds((S, FF), BF16), _sds((S, FF), BF16), _sds((S, D), BF16), _sds((S, D), F32)],
                    [_sds((1, 128), F32), _sds((1, D), F32)])


def _swa_in_fwd(x, g, sc, sh, w_qkv, b_qkv):
    S, D = x.shape
    NQ = SWA_HEADS * SWA_HEAD_DIM
    NK = SWA_KV_HEADS * SWA_HEAD_DIM

    def body(x_ref, g_ref, sc_ref, sh_ref, w_ref, b_ref, h_ref, q_ref, k_ref, v_ref):
        hb = _modulate(x_ref[...], g_ref[...], sc_ref[...], sh_ref[...]).astype(BF16)
        h_ref[...] = hb
        qkv = _dot(hb, w_ref[...]) + b_ref[...]
        q_ref[...] = qkv[:, :NQ].astype(BF16)
        k_ref[...] = qkv[:, NQ:NQ + NK].astype(BF16)
        v_ref[...] = qkv[:, NQ + NK:].astype(BF16)

    return _rowcall("swa_in_fwd", body, S, ROW_TILE_IO, [x], [g, sc, sh, w_qkv, b_qkv],
                    [_sds((S, D), BF16), _sds((S, NQ), BF16), _sds((S, NK), BF16), _sds((S, NK), BF16)])


def _alibi_slope(head):
    return float(np.float32(2.0 ** (-8.0 * (head + 1) / SWA_HEADS)))


def _swa_geometry(n):
    W, G = WINDOW, SWA_GROUP
    key = lax.broadcasted_iota(jnp.int32, (2 * W, G * W), 0)
    qry = lax.broadcasted_iota(jnp.int32, (2 * W, G * W), 1) & (W - 1)
    dist = W + qry - key
    valid = (dist >= 0) & (dist < W) & ((n > 0) | (key >= W))
    return dist.astype(F32), valid


def _swa_group(kh, q_ref, sink_ref):
    W, G, Dh = WINDOW, SWA_GROUP, SWA_HEAD_DIM
    heads = [kh * G + g for g in range(G)]
    q4 = jnp.concatenate([q_ref[:, h * Dh:(h + 1) * Dh] for h in heads], axis=0)
    slopes = jnp.concatenate([jnp.full((1, W), _alibi_slope(h), F32) for h in heads], axis=1)
    sinks = jnp.concatenate([jnp.broadcast_to(sink_ref[:, h:h + 1], (1, W)) for h in heads], axis=1)
    return heads, q4, slopes, sinks


def _swa_band_specs(W, nb, cols):
    prev = pl.BlockSpec((W, cols), lambda n: (jnp.maximum(jnp.minimum(n, nb - 1) - 1, 0), 0))
    cur = pl.BlockSpec((W, cols), lambda n: (jnp.minimum(n, nb - 1), 0))
    return prev, cur


def _swa_attn_fwd(q, k, v, sinks, x, w_o, b_o, gt, g, sc, sh):
    S, NQ = q.shape
    NK = k.shape[1]
    D = x.shape[1]
    W, Dh, G = WINDOW, SWA_HEAD_DIM, SWA_GROUP
    nb = S // W

    def body(q_ref, kp_ref, kc_ref, vp_ref, vc_ref, sink_ref, x_ref, wo_ref, bo_ref, gt_ref, g_ref, sc_ref, sh_ref,
             o_ref, lse_ref, y_ref, x1_ref, h_ref):
        distf, valid = _swa_geometry(pl.program_id(0))
        kband = jnp.concatenate([kp_ref[...], kc_ref[...]], axis=0)
        vband_t = jnp.concatenate([vp_ref[...], vc_ref[...]], axis=0).astype(F32).T.astype(BF16)
        outs = []
        for kh in range(SWA_KV_HEADS):
            kb = kband[:, kh * Dh:(kh + 1) * Dh]
            vbt = vband_t[kh * Dh:(kh + 1) * Dh, :]
            heads, q4, slopes, sinks = _swa_group(kh, q_ref, sink_ref)
            s = _dot_nt(kb, q4) * (Dh ** -0.5) - slopes * distf
            s = jnp.where(valid, s, -jnp.inf)
            m = jnp.maximum(jnp.max(s, axis=0, keepdims=True), sinks)
            p = jnp.exp(s - m)
            denom = jnp.sum(p, axis=0, keepdims=True) + jnp.exp(sinks - m)
            out4 = _dot(vbt, (p * (1.0 / denom)).astype(BF16))
            lse4 = m + jnp.log(denom)
            for g, h in enumerate(heads):
                outs.append(out4[:, g * W:(g + 1) * W])
                lse_ref[h:h + 1, :] = lse4[:, g * W:(g + 1) * W]
        ob = jnp.concatenate(outs, axis=0).T.astype(BF16)
        o_ref[...] = ob
        y = _dot(ob, wo_ref[...]) + bo_ref[...]
        y_ref[...] = y.astype(BF16)
        x1 = x_ref[...] + gt_ref[...] * y
        x1_ref[...] = x1
        h_ref[...] = _modulate(x1, g_ref[...], sc_ref[...], sh_ref[...]).astype(BF16)

    kprev, kcur = _swa_band_specs(W, nb, NK)
    blk = lambda cols: pl.BlockSpec((W, cols), lambda n: (n, 0))
    row = pl.BlockSpec((1, D), lambda n: (0, 0))
    return pl.pallas_call(
        body, name="swa_attn_fwd", grid=(nb,),
        in_specs=[blk(NQ), kprev, kcur, kprev, kcur, pl.BlockSpec((1, SWA_HEADS), lambda n: (0, 0)), blk(D),
                  pl.BlockSpec(w_o.shape, lambda n: (0, 0), pipeline_mode=pl.Buffered(1)), row, row, row, row, row],
        out_specs=[blk(NQ), pl.BlockSpec((SWA_HEADS, W), lambda n: (0, n)), blk(D), blk(D), blk(D)],
        out_shape=[_sds((S, NQ), BF16), _sds((SWA_HEADS, S), F32), _sds((S, D), BF16), _sds((S, D), F32),
                   _sds((S, D), BF16)],
        compiler_params=pltpu.CompilerParams(dimension_semantics=("arbitrary",), vmem_limit_bytes=VMEM_LIMIT),
    )(q, k, k, v, v, sinks, x, w_o, b_o, gt, g, sc, sh)


def _mlp_bwd_a(dx, y, rl, gt, w2):
    S, D = dx.shape
    FF = rl.shape[1]

    def body(dx_ref, y_ref, rl_ref, gt_ref, w2_ref, dy_ref, du_ref, dgt_ref):
        i = pl.program_id(0)
        dxv = dx_ref[...]
        _acc(dgt_ref, jnp.sum(dxv * y_ref[...].astype(F32), axis=0, keepdims=True), i)
        dy = (dxv * gt_ref[...]).astype(BF16)
        dy_ref[...] = dy
        dact = _dot_nt(dy, w2_ref[...])
        du_ref[...] = (dact * (2.0 * rl_ref[...].astype(F32))).astype(BF16)

    return _rowcall("mlp_bwd_a", body, S, ROW_TILE_BWD, [dx, y, rl], [gt, w2],
                    [_sds((S, D), BF16), _sds((S, FF), BF16)], [_sds((1, D), F32)])


def _mlp_bwd_b(du, x, dx, w1, g, sc):
    S, D = x.shape

    def body(du_ref, x_ref, dx_ref, w1_ref, g_ref, sc_ref, dxo_ref, dsh_ref, da_ref):
        i = pl.program_id(0)
        dh = _dot_nt(du_ref[...], w1_ref[...])
        dxn, dsh, da = _modulate_bwd(dh, x_ref[...], g_ref[...], sc_ref[...])
        dxo_ref[...] = dx_ref[...] + dxn
        _acc(dsh_ref, dsh, i)
        _acc(da_ref, da, i)

    return _rowcall("mlp_bwd_b", body, S, ROW_TILE_BWD, [du, x, dx], [w1, g, sc],
                    [_sds((S, D), F32)], [_sds((1, D), F32), _sds((1, D), F32)])


def _attn_out_bwd(dx, y, o, gt, w_o, n_heads):
    S, D = dx.shape
    NO = o.shape[1]
    dh = NO // n_heads
    member = (jnp.arange(NO)[None, :] // dh == jnp.arange(16)[:, None]).astype(BF16)

    def body(dx_ref, y_ref, o_ref, gt_ref, wo_ref, mem_ref, dy_ref, do_ref, dl_ref, dgt_ref, dbo_ref):
        i = pl.program_id(0)
        dxv = dx_ref[...]
        _acc(dgt_ref, jnp.sum(dxv * y_ref[...].astype(F32), axis=0, keepdims=True), i)
        dy = dxv * gt_ref[...]
        _acc(dbo_ref, jnp.sum(dy, axis=0, keepdims=True), i)
        dyb = dy.astype(BF16)
        dy_ref[...] = dyb
        do = _dot_nt(dyb, wo_ref[...])
        do_ref[...] = do.astype(BF16)
        prod = do * o_ref[...].astype(F32)
        hi = prod.astype(BF16)
        lo = (prod - hi.astype(F32)).astype(BF16)
        dl_ref[...] = _dot_nt(mem_ref[...], hi) + _dot_nt(mem_ref[...], lo)

    tm = min(ROW_TILE_IO, S)
    return _rowcall("attn_out_bwd", body, S, ROW_TILE_IO, [dx, y, o], [gt, w_o, member],
                    [_sds((S, D), BF16), _sds((S, NO), BF16),
                     (_sds((16, S), F32), pl.BlockSpec((16, tm), lambda i: (0, i)))],
                    [_sds((1, D), F32), _sds((1, D), F32)])


def _mla_attn_bwd(q, k, v, do, lse, delta, t, gblks):
    H, S, DQ = q.shape
    DV = V_DIM
    t = min(t, S // 2)
    tk = 2 * t
    nq, nk = S // t, S // tk
    scale = QK_DIM ** -0.5
    c2 = scale * LOG2E

    ng = len(gblks)

    def body(q_ref, k_ref, v_ref, do_ref, lse_ref, dl_ref, *rest):
        g_refs, (dq_ref, dk_ref, dv_ref), recv_refs = rest[:ng], rest[ng:ng + 3], rest[ng + 3:2 * ng + 3]
        dk_s, dv_s, s_buf, dp_buf, send_sems, recv_sems, local_sems = rest[2 * ng + 3:]
        hd, kj = pl.program_id(0), pl.program_id(1)

        def scatter():
            return _exchange_copies([lambda j, r=r: r.at[j] for r in g_refs], recv_refs, send_sems, recv_sems,
                                    local_sems)

        @pl.when((hd == 0) & (kj == 0))
        def _():
            _start_exchange(scatter())

        @pl.when(kj == 0)
        def _():
            dq_ref[...] = jnp.zeros_like(dq_ref)

        dk_s[...] = jnp.zeros_like(dk_s)
        dv_s[...] = jnp.zeros_like(dv_s)

        def products(i, slot, keys=tk):
            rows = pl.ds(pl.multiple_of(i * t, t), t)
            s_buf[slot, :keys] = _dot_nt(k_ref[0, :keys], q_ref[0, rows, :])
            dp_buf[slot, :keys] = _dot_nt(v_ref[0, :keys], do_ref[rows, :])

        def absorb(i, slot, diagonal, keys=tk):
            rows = pl.ds(pl.multiple_of(i * t, t), t)
            qb, dob = q_ref[0, rows, :], do_ref[rows, :]
            p = jnp.exp2(s_buf[slot, :keys] * c2 - lse_ref[0, i])
            if diagonal is not None:
                key = lax.broadcasted_iota(jnp.int32, (keys, t), 0)
                qry = lax.broadcasted_iota(jnp.int32, (keys, t), 1) + diagonal * t
                p = jnp.where(key <= qry, p, 0.0)
            dv_s[:keys] += _dot(p.astype(BF16), dob)
            ds = (p * (dp_buf[slot, :keys] - dl_ref[0, i])).astype(BF16)
            dk_s[:keys] += _dot(ds, qb)
            dq_ref[0, rows, :] += _dot_tn(ds, k_ref[0, :keys])

        first = 2 * kj + 2
        n_off = nq - first

        def pair(i, carry):
            u = 2 * i
            products(first + u + 1, 1)
            absorb(first + u, 0, None)
            products(jnp.where(u + 2 < n_off, first + u + 2, 2 * kj + 1), 0)
            absorb(first + u + 1, 1, None)
            return carry

        products(jnp.where(n_off > 0, first, 2 * kj + 1), 0)
        lax.fori_loop(0, n_off // 2, pair, 0)
        products(2 * kj, 1, t)
        absorb(2 * kj + 1, 0, 1)
        absorb(2 * kj, 1, 0, t)

        dk_ref[0] = (dk_s[...] * scale).astype(BF16)
        dv_ref[0] = dv_s[...].astype(BF16)

        @pl.when((hd == H - 1) & (kj == nk - 1))
        def _():
            _finish_exchange(scatter())

    rowspec = pl.BlockSpec((1, nq, 1, t), lambda h, j: (h, 0, 0, 0))
    outs = pl.pallas_call(
        body, name="mla_attn_bwd", grid=(H, nk),
        in_specs=[pl.BlockSpec((1, S, DQ), lambda h, j: (h, 0, 0)),
                  pl.BlockSpec((1, tk, DQ), lambda h, j: (h, j, 0)),
                  pl.BlockSpec((1, tk, DV), lambda h, j: (h, j, 0)),
                  pl.BlockSpec((S, DV), lambda h, j: (0, h)), rowspec, rowspec] + [ANY_SPEC] * ng,
        out_specs=[pl.BlockSpec((1, S, DQ), lambda h, j: (h, 0, 0)),
                   pl.BlockSpec((1, tk, DQ), lambda h, j: (h, j, 0)),
                   pl.BlockSpec((1, tk, DV), lambda h, j: (h, j, 0))] + [ANY_SPEC] * ng,
        out_shape=[_sds((H, S, DQ), F32), _sds((H, S, DQ), BF16), _sds((H, S, DV), BF16)]
        + [_sds(g.shape, g.dtype) for g in gblks],
        scratch_shapes=[pltpu.VMEM((tk, DQ), F32), pltpu.VMEM((tk, DV), F32), pltpu.VMEM((2, tk, t), F32),
                        pltpu.VMEM((2, tk, t), F32)] + _comm_sems(ng),
        compiler_params=pltpu.CompilerParams(dimension_semantics=("arbitrary", "arbitrary"),
                                             vmem_limit_bytes=VMEM_LIMIT),
    )(q, k, v, do, lse, delta, *gblks)
    return outs[0], outs[1], outs[2], outs[3:]


def _swa_attn_bwd(q, k, v, lse, sinks, dx, y, o, gt, w_o):
    S, NQ = q.shape
    NK = k.shape[1]
    D = dx.shape[1]
    W, Dh, G = WINDOW, SWA_HEAD_DIM, SWA_GROUP
    nb = S // W
    member = (jnp.arange(NQ)[None, :] // Dh == jnp.arange(SWA_HEADS)[:, None]).astype(BF16)

    def body(q_ref, kp_ref, kc_ref, vp_ref, vc_ref, lse_ref, sink_ref, dx_ref, y_ref, o_ref, gt_ref, wo_ref, mem_ref,
             dq_ref, dk_ref, dv_ref, dsink_ref, dy_ref, dgt_ref, dbo_ref, dkc_s, dvc_s):
        n = pl.program_id(0)

        @pl.when(n == 0)
        def _():
            dkc_s[...] = jnp.zeros_like(dkc_s)
            dvc_s[...] = jnp.zeros_like(dvc_s)
            dsink_ref[...] = jnp.zeros_like(dsink_ref)

        @pl.when(n < nb)
        def _():
            dxv = dx_ref[...]
            _acc(dgt_ref, jnp.sum(dxv * y_ref[...].astype(F32), axis=0, keepdims=True), n)
            dy = dxv * gt_ref[...]
            _acc(dbo_ref, jnp.sum(dy, axis=0, keepdims=True), n)
            dyb = dy.astype(BF16)
            dy_ref[...] = dyb
            do = _dot_nt(dyb, wo_ref[...])
            dob = do.astype(BF16)
            prod = do * o_ref[...].astype(F32)
            hi = prod.astype(BF16)
            lo = (prod - hi.astype(F32)).astype(BF16)
            dl = _dot_nt(mem_ref[...], hi) + _dot_nt(mem_ref[...], lo)
            distf, valid = _swa_geometry(n)
            kband = jnp.concatenate([kp_ref[...], kc_ref[...]], axis=0)
            vband = jnp.concatenate([vp_ref[...], vc_ref[...]], axis=0)
            kband_t = kband.astype(F32).T.astype(BF16)
            dq_t = []
            for kh in range(SWA_KV_HEADS):
                ck = slice(kh * Dh, (kh + 1) * Dh)
                kb, vb, kbt = kband[:, ck], vband[:, ck], kband_t[ck, :]
                heads, q4, slopes, sinks = _swa_group(kh, q_ref, sink_ref)
                do4 = jnp.concatenate([dob[:, h * Dh:(h + 1) * Dh] for h in heads], axis=0)
                lse4 = jnp.concatenate([lse_ref[h:h + 1, :] for h in heads], axis=1)
                dl4 = jnp.concatenate([dl[h:h + 1, :] for h in heads], axis=1)
                s = _dot_nt(kb, q4) * (Dh ** -0.5) - slopes * distf
                p = jnp.where(valid, jnp.exp(s - lse4), 0.0)
                dvb = _dot(p.astype(BF16), do4)
                dp = _dot_nt(vb, do4)
                dsb = ((p * (dp - dl4)) * (Dh ** -0.5)).astype(BF16)
                dq4 = _dot(kbt, dsb)
                dkb = _dot(dsb, q4)
                dsk4 = jnp.exp(sinks - lse4) * dl4
                for g, h in enumerate(heads):
                    dq_t.append(dq4[:, g * W:(g + 1) * W])
                    dsink_ref[:, h:h + 1] += -jnp.sum(dsk4[:, g * W:(g + 1) * W], axis=1, keepdims=True)
                dk_ref[:, ck] = (dkc_s[:, ck] + dkb[:W]).astype(BF16)
                dv_ref[:, ck] = (dvc_s[:, ck] + dvb[:W]).astype(BF16)
                dkc_s[:, ck] = dkb[W:]
                dvc_s[:, ck] = dvb[W:]
            dq_ref[...] = jnp.concatenate(dq_t, axis=0).T.astype(BF16)

        @pl.when(n == nb)
        def _():
            dk_ref[...] = dkc_s[...].astype(BF16)
            dv_ref[...] = dvc_s[...].astype(BF16)

    kprev, kcur = _swa_band_specs(W, nb, NK)
    qspec = lambda cols: pl.BlockSpec((W, cols), lambda n: (jnp.minimum(n, nb - 1), 0))
    kvout = pl.BlockSpec((W, NK), lambda n: (jnp.maximum(n - 1, 0), 0))
    rowspec = pl.BlockSpec((SWA_HEADS, W), lambda n: (0, jnp.minimum(n, nb - 1)))
    fixed = lambda shape: pl.BlockSpec(shape, lambda n: (0, 0))
    return pl.pallas_call(
        body, name="swa_attn_bwd", grid=(nb + 1,),
        in_specs=[qspec(NQ), kprev, kcur, kprev, kcur, rowspec, pl.BlockSpec((1, SWA_HEADS), lambda n: (0, 0)),
                  qspec(D), qspec(D), qspec(NQ), fixed((1, D)), fixed(w_o.shape), fixed(member.shape)],
        out_specs=[qspec(NQ), kvout, kvout, fixed((1, 128)), qspec(D), fixed((1, D)), fixed((1, D))],
        out_shape=[_sds((S, NQ), BF16), _sds((S, NK), BF16), _sds((S, NK), BF16), _sds((1, 128), F32),
                   _sds((S, D), BF16), _sds((1, D), F32), _sds((1, D), F32)],
        scratch_shapes=[pltpu.VMEM((W, NK), F32), pltpu.VMEM((W, NK), F32)],
        compiler_params=pltpu.CompilerParams(dimension_semantics=("arbitrary",), vmem_limit_bytes=VMEM_LIMIT),
    )(q, k, k, v, v, lse, sinks, dx, y, o, gt, w_o, member)


def _swa_in_bwd(dq, dk, dv, x, dx, w_qkv, g, sc):
    S, D = x.shape
    N = w_qkv.shape[1]

    def body(dq_ref, dk_ref, dv_ref, x_ref, dx_ref, w_ref, g_ref, sc_ref, dqkv_ref, dxo_ref, db_ref, dsh_ref, da_ref):
        i = pl.program_id(0)
        dqkv = jnp.concatenate([dq_ref[...], dk_ref[...], dv_ref[...]], axis=1)
        dqkv_ref[...] = dqkv
        _acc(db_ref, jnp.sum(dqkv.astype(F32), axis=0, keepdims=True), i)
        dh = _dot_nt(dqkv, w_ref[...])
        dxn, dsh, da = _modulate_bwd(dh, x_ref[...], g_ref[...], sc_ref[...])
        dxo_ref[...] = dx_ref[...] + dxn
        _acc(dsh_ref, dsh, i)
        _acc(da_ref, da, i)

    return _rowcall("swa_in_bwd", body, S, ROW_TILE_IO, [dq, dk, dv, x, dx], [w_qkv, g, sc],
                    [_sds((S, N), BF16), _sds((S, D), F32)],
                    [_sds((1, N), F32), _sds((1, D), F32), _sds((1, D), F32)])


def _mla_in_bwd(dq, dk, dv, cos, sin, cqp, ckvp, x, dx, w_uqx, g_q, w_ukv, g_kv, w_cat, g, sc):
    S, D = x.shape
    H = MLA_HEADS
    QL = g_q.shape[1]
    NX = w_uqx.shape[1]
    NC = w_cat.shape[1]

    def body(dq_ref, dk_ref, dv_ref, cos_ref, sin_ref, cqp_ref, ckvp_ref, x_ref, dx_ref,
             wuqx_ref, gq_ref, wukv_ref, gkv_ref, wcat_ref, g_ref, sc_ref,
             dqx_ref, dkv_ref, dcat_ref, dxo_ref, dgq_ref, dgkv_ref, dsh_ref, da_ref):
        i = pl.program_id(0)
        cs, sn = cos_ref[...], sin_ref[...]
        dkr = jnp.zeros(cs.shape, F32)
        for hd in range(H):
            b = hd * 256
            dqh = dq_ref[hd] * (QK_DIM ** -0.5)
            dqx_ref[:, b:b + QK_NOPE] = dqh[:, :QK_NOPE].astype(BF16)
            dqx_ref[:, b + 128:b + 192] = (dqh[:, QK_NOPE:] * cs).astype(BF16)
            dqx_ref[:, b + 192:b + 256] = (dqh[:, QK_NOPE:] * sn).astype(BF16)
            dkh = dk_ref[hd]
            dkv_ref[:, b:b + QK_NOPE] = dkh[:, :QK_NOPE]
            dkv_ref[:, b + 128:b + 256] = dv_ref[hd]
            dkr = dkr + dkh[:, QK_NOPE:].astype(F32)
        dcq = _dot_nt(dqx_ref[...], wuqx_ref[...])
        cqp = cqp_ref[...]
        rq = _rstd(cqp)
        nq = cqp * rq
        _acc(dgq_ref, jnp.sum(dcq * nq, axis=0, keepdims=True), i)
        dcqp = _rms_bwd(dcq * gq_ref[...], nq, rq)
        dckv = _dot_nt(dkv_ref[...], wukv_ref[...])
        ckvp = ckvp_ref[...]
        rk = _rstd(ckvp)
        nk = ckvp * rk
        _acc(dgkv_ref, jnp.sum(dckv * nk, axis=0, keepdims=True), i)
        dckvp = _rms_bwd(dckv * gkv_ref[...], nk, rk)
        dcat_ref[:, :QL] = dcqp.astype(BF16)
        dcat_ref[:, QL:QL + KV_LORA] = dckvp.astype(BF16)
        o = QL + KV_LORA
        dcat_ref[:, o:o + QK_ROPE] = (dkr * cs).astype(BF16)
        dcat_ref[:, o + QK_ROPE:o + 2 * QK_ROPE] = (dkr * sn).astype(BF16)
        dh = _dot_nt(dcat_ref[...], wcat_ref[...])
        dxn, dsh, da = _modulate_bwd(dh, x_ref[...], g_ref[...], sc_ref[...])
        dxo_ref[...] = dx_ref[...] + dxn
        _acc(dsh_ref, dsh, i)
        _acc(da_ref, da, i)

    return _rowcall("mla_in_bwd", body, S, ROW_TILE, [dq, dk, dv, cos, sin, cqp, ckvp, x, dx],
                    [w_uqx, g_q, w_ukv, g_kv, w_cat, g, sc],
                    [_sds((S, NX), BF16), _sds((S, NX), BF16), _sds((S, NC), BF16), _sds((S, D), F32)],
                    [_sds((1, QL), F32), _sds((1, KV_LORA), F32), _sds((1, D), F32), _sds((1, D), F32)])


def _matmul_tn(name, a, b, out_dtype=F32, column_blocks=False):
    S, K = a.shape
    N = b.shape[1]
    tk, tn, ts = min(K, 1024), min(N, 1024), min(S, TN_TOKENS)
    if column_blocks:
        tn = N // N_DEV
    if N % tn:
        tn = 512 if N % 512 == 0 else (384 if N % 384 == 0 else 128)
    if K % tk:
        tk = 512 if K % 512 == 0 else (384 if K % 384 == 0 else 128)
    ns = S // ts

    def body(a_ref, b_ref, o_ref, *scratch):
        acc_ref = scratch[0] if scratch else o_ref
        _acc(acc_ref, _dot_tn(a_ref[...], b_ref[...]), pl.program_id(2))
        if scratch:
            @pl.when(pl.program_id(2) == ns - 1)
            def _():
                o_ref[...] = acc_ref[...].astype(out_dtype)

    if column_blocks:
        out_spec = pl.BlockSpec((None, tk, tn), lambda i, j, s: (j, i, 0))
        out_shape = _sds((N_DEV, K, tn), out_dtype)
    else:
        out_spec = pl.BlockSpec((tk, tn), lambda i, j, s: (i, j))
        out_shape = _sds((K, N), out_dtype)
    return pl.pallas_call(
        body, name=name, grid=(K // tk, N // tn, ns),
        in_specs=[pl.BlockSpec((ts, tk), lambda i, j, s: (s, i)), pl.BlockSpec((ts, tn), lambda i, j, s: (s, j))],
        out_specs=out_spec, out_shape=out_shape,
        scratch_shapes=[] if out_dtype == F32 else [pltpu.VMEM((tk, tn), F32)],
        compiler_params=pltpu.CompilerParams(dimension_semantics=("parallel", "parallel", "arbitrary"),
                                             vmem_limit_bytes=VMEM_LIMIT),
    )(a, b)


def _silu(c):
    return c * jax.nn.sigmoid(c)


def _ada_fwd(c_all, w_ada):
    L, D, NC = w_ada.shape

    def body(c_ref, w_ref, o_ref):
        cond = _silu(c_ref[...]).astype(BF16)
        o_ref[0] = _dot(cond, w_ref[0].astype(BF16))

    return pl.pallas_call(
        body, name="ada_fwd", grid=(L,),
        in_specs=[pl.BlockSpec(c_all.shape, lambda l: (0, 0)), pl.BlockSpec((1, D, NC), lambda l: (l, 0, 0))],
        out_specs=pl.BlockSpec((1, N_DEV, NC), lambda l: (l, 0, 0)),
        out_shape=_sds((L, N_DEV, NC), F32),
        compiler_params=pltpu.CompilerParams(dimension_semantics=("arbitrary",), vmem_limit_bytes=VMEM_LIMIT),
    )(c_all, w_ada)


def _adamw(w, g, m, v):
    m = ADAM_B1 * m + (1.0 - ADAM_B1) * g
    v = ADAM_B2 * v + (1.0 - ADAM_B2) * (g * g)
    m_hat = m / (1.0 - ADAM_B1 ** ADAM_STEP)
    v_hat = v / (1.0 - ADAM_B2 ** ADAM_STEP)
    delta = -ADAM_LR * (m_hat / (jnp.sqrt(v_hat) + ADAM_EPS) + ADAM_WD * w)
    return delta, m, v


def _ada_bwd_adamw(c_all_t, dmod_cols, w, m, v):
    L, D, NC = w.shape
    tr = min(D, 256)

    def body(ct_ref, dm_ref, w_ref, m_ref, v_ref, g_ref, d_ref, mo_ref, vo_ref):
        cond_t = _silu(ct_ref[...])
        dm = dm_ref[0]
        g = cond_t[:, 0:1] * dm[0:1, :]
        for b in range(1, N_DEV):
            g = g + cond_t[:, b:b + 1] * dm[b:b + 1, :]
        g_ref[0] = g
        d_ref[0], mo_ref[0], vo_ref[0] = _adamw(w_ref[0], g, m_ref[0], v_ref[0])

    wspec = pl.BlockSpec((1, tr, NC), lambda l, r: (l, r, 0))
    return pl.pallas_call(
        body, name="ada_bwd_adamw", grid=(L, D // tr),
        in_specs=[pl.BlockSpec((tr, N_DEV), lambda l, r: (r, 0)),
                  pl.BlockSpec((1, N_DEV, NC), lambda l, r: (l, 0, 0)), wspec, wspec, wspec],
        out_specs=[wspec] * 4, out_shape=[_sds(w.shape, F32)] * 4,
        compiler_params=pltpu.CompilerParams(dimension_semantics=("parallel", "parallel"), vmem_limit_bytes=VMEM_LIMIT),
    )(c_all_t, dmod_cols, w, m, v)


def _sum_devices(x):
    def body(x_ref, o_ref):
        s = x_ref[0]
        for j in range(1, N_DEV):
            s = s + x_ref[j]
        o_ref[...] = s

    return pl.pallas_call(body, name="sum_devices", out_shape=_sds(x.shape[1:], F32))(x)


def _adamw_small(w, g, m, v):
    def body(w_ref, g_ref, m_ref, v_ref, d_ref, mo_ref, vo_ref):
        d_ref[...], mo_ref[...], vo_ref[...] = _adamw(w_ref[...], g_ref[...], m_ref[...], v_ref[...])

    return pl.pallas_call(body, name="adamw_small", out_shape=[_sds(w.shape, F32)] * 3)(w, g, m, v)


def _me():
    return lax.axis_index("x") * 4 + lax.axis_index("y") * 2 + lax.axis_index("c")


def _peer(k):
    x, y, c = lax.axis_index("x"), lax.axis_index("y"), lax.axis_index("c")
    px = 1 - x if k & 4 else x
    py = 1 - y if k & 2 else y
    pc = 1 - c if k & 1 else c
    return (px, py, pc), px * 4 + py * 2 + pc


VMEM_SPEC = pl.BlockSpec(memory_space=pltpu.VMEM)
ANY_SPEC = pl.BlockSpec(memory_space=pl.ANY)
def _comm_sems(n):
    return [pltpu.SemaphoreType.DMA((n * (N_DEV - 1),)), pltpu.SemaphoreType.DMA((n * (N_DEV - 1),)),
            pltpu.SemaphoreType.DMA((n,))]


def _exchange_copies(srcs_of, dst_refs, send_sems, recv_sems, local_sems):
    me = _me()
    local, sends, recvs = [], [], []
    for a, (src_of, dst_ref) in enumerate(zip(srcs_of, dst_refs)):
        local.append(pltpu.make_async_copy(src_of(me), dst_ref.at[me], local_sems.at[a]))
        for k in range(1, N_DEV):
            dev, pj = _peer(k)
            i = a * (N_DEV - 1) + k - 1
            sems = dict(send_sem=send_sems.at[i], recv_sem=recv_sems.at[i], device_id=dev, device_id_type=MESH_IDS)
            sends.append(pltpu.make_async_remote_copy(src_ref=src_of(pj), dst_ref=dst_ref.at[me], **sems))
            recvs.append(pltpu.make_async_remote_copy(src_ref=src_of(pj), dst_ref=dst_ref.at[pj], **sems))
    return local, sends, recvs


def _start_exchange(copies):
    local, sends, _ = copies
    for cp in local + sends:
        cp.start()


def _finish_exchange(copies):
    local, sends, recvs = copies
    for cp in recvs:
        cp.wait_recv()
    for cp in sends:
        cp.wait_send()
    for cp in local:
        cp.wait()


def _sum_adamw(recv, w, m, v):
    shape = w.shape
    C = shape[-1]
    R = w.size // C
    rows = max(d for d in range(16, min(R, 512) + 1, 16) if R % d == 0 and d * C <= 256 * 1024)

    def body(r_ref, w_ref, m_ref, v_ref, go_ref, d_ref, mo_ref, vo_ref):
        g = r_ref[0].astype(F32)
        for j in range(1, N_DEV):
            g = g + r_ref[j].astype(F32)
        go_ref[...] = g
        d_ref[...], mo_ref[...], vo_ref[...] = _adamw(w_ref[...], g, m_ref[...], v_ref[...])

    spec = pl.BlockSpec((rows, C), lambda i: (i, 0))
    outs = pl.pallas_call(
        body, name="sum_adamw", grid=(R // rows,),
        in_specs=[pl.BlockSpec((N_DEV, rows, C), lambda i: (0, i, 0)), spec, spec, spec],
        out_specs=[spec] * 4, out_shape=[_sds((R, C), F32)] * 4,
        compiler_params=pltpu.CompilerParams(dimension_semantics=("parallel",), vmem_limit_bytes=VMEM_LIMIT),
    )(recv.reshape(N_DEV, R, C), w.reshape(R, C), m.reshape(R, C), v.reshape(R, C))
    return [o.reshape(shape) for o in outs]


def _all_gather(name, x, out_dtype):
    R, C = x.shape
    cast = out_dtype != x.dtype

    def body(x_ref, out_ref, buf, send_sems, recv_sems, local_sem):
        me = _me()
        if cast:
            buf[...] = x_ref[...].astype(out_dtype)
            src = buf
        else:
            src = x_ref
        local = pltpu.make_async_copy(src, out_ref.at[me], local_sem)
        local.start()
        sends = []
        for k in range(1, N_DEV):
            dev, _ = _peer(k)
            cp = pltpu.make_async_remote_copy(src_ref=src, dst_ref=out_ref.at[me], send_sem=send_sems.at[k - 1],
                                              recv_sem=recv_sems.at[k - 1], device_id=dev, device_id_type=MESH_IDS)
            cp.start()
            sends.append(cp)
        for k in range(1, N_DEV):
            dev, pj = _peer(k)
            pltpu.make_async_remote_copy(src_ref=src, dst_ref=out_ref.at[pj], send_sem=send_sems.at[k - 1],
                                         recv_sem=recv_sems.at[k - 1], device_id=dev, device_id_type=MESH_IDS).wait_recv()
        for cp in sends:
            cp.wait_send()
        local.wait()

    return pl.pallas_call(
        body, name=name, in_specs=[VMEM_SPEC], out_specs=ANY_SPEC, out_shape=_sds((N_DEV, R, C), out_dtype),
        scratch_shapes=[pltpu.VMEM((R, C) if cast else (8, 128), out_dtype),
                        pltpu.SemaphoreType.DMA((N_DEV - 1,)), pltpu.SemaphoreType.DMA((N_DEV - 1,)),
                        pltpu.SemaphoreType.DMA(())],
        compiler_params=pltpu.CompilerParams(vmem_limit_bytes=VMEM_LIMIT),
    )(x)


def _all_gather_two_level(name, x, out_dtype):
    R, C = x.shape

    def body(x_ref, out_ref, buf, send_sems, recv_sems, local_sem):
        x_, y_, c_ = lax.axis_index("x"), lax.axis_index("y"), lax.axis_index("c")
        me, sibling = (x_, y_, c_), (x_, y_, 1 - c_)
        chips = [(1 - x_, y_), (x_, 1 - y_), (1 - x_, 1 - y_)]
        buf[...] = x_ref[...].astype(out_dtype)

        def slot(px, py, pc):
            return out_ref.at[4 * px + 2 * py + pc]

        def copy(k, block, to, src=None):
            return pltpu.make_async_remote_copy(src_ref=slot(*block) if src is None else src, dst_ref=slot(*block),
                                                send_sem=send_sems.at[k], recv_sem=recv_sems.at[k], device_id=to,
                                                device_id_type=MESH_IDS)

        mine = pltpu.make_async_copy(buf, slot(*me), local_sem)
        mine.start()
        first = [copy(0, me, sibling, src=buf)] + [copy(1 + j, me, (*chip, c_), src=buf) for j, chip in enumerate(chips)]
        for cp in first:
            cp.start()
        passed = [copy(4 + j, (*chip, c_), sibling) for j, chip in enumerate(chips)]
        for j, chip in enumerate(chips):
            copy(1 + j, (*chip, c_), me).wait_recv()
            passed[j].start()
        copy(0, sibling, me).wait_recv()
        for j, chip in enumerate(chips):
            copy(4 + j, (*chip, 1 - c_), me).wait_recv()
        for cp in first + passed:
            cp.wait_send()
        mine.wait()

    return pl.pallas_call(
        body, name=name, in_specs=[VMEM_SPEC], out_specs=ANY_SPEC, out_shape=_sds((N_DEV, R, C), out_dtype),
        scratch_shapes=[pltpu.VMEM((R, C), out_dtype), pltpu.SemaphoreType.DMA((N_DEV - 1,)),
                        pltpu.SemaphoreType.DMA((N_DEV - 1,)), pltpu.SemaphoreType.DMA(())],
        compiler_params=pltpu.CompilerParams(vmem_limit_bytes=VMEM_LIMIT),
    )(x)


def _all_to_all(name, x):
    _, R, C = x.shape

    def body(x_ref, out_ref, send_sems, recv_sems, local_sem):
        me = _me()
        local = pltpu.make_async_copy(x_ref.at[me], out_ref.at[me], local_sem)
        local.start()
        sends = []
        for k in range(1, N_DEV):
            dev, pj = _peer(k)
            cp = pltpu.make_async_remote_copy(src_ref=x_ref.at[pj], dst_ref=out_ref.at[me], send_sem=send_sems.at[k - 1],
                                              recv_sem=recv_sems.at[k - 1], device_id=dev, device_id_type=MESH_IDS)
            cp.start()
            sends.append(cp)
        for k in range(1, N_DEV):
            dev, pj = _peer(k)
            pltpu.make_async_remote_copy(src_ref=x_ref.at[pj], dst_ref=out_ref.at[pj], send_sem=send_sems.at[k - 1],
                                         recv_sem=recv_sems.at[k - 1], device_id=dev, device_id_type=MESH_IDS).wait_recv()
        for cp in sends:
            cp.wait_send()
        local.wait()

    return pl.pallas_call(
        body, name=name, in_specs=[VMEM_SPEC], out_specs=VMEM_SPEC, out_shape=_sds(x.shape, x.dtype),
        scratch_shapes=[pltpu.SemaphoreType.DMA((N_DEV - 1,)), pltpu.SemaphoreType.DMA((N_DEV - 1,)),
                        pltpu.SemaphoreType.DMA(())],
    )(x)


def _reduce_scatter_adamw(name, gblk, w, m, v):
    _, R, C = gblk.shape
    rows = 8
    for cand in (136, 128, 80, 64, 40, 32, 16, 8):
        if R % cand == 0:
            rows = cand
            break

    def body(g_ref, w_ref, m_ref, v_ref, go_ref, d_ref, mo_ref, vo_ref, recv, send_sems, recv_sems, local_sem):
        me = _me()
        local = pltpu.make_async_copy(g_ref.at[me], recv.at[me], local_sem)
        local.start()
        sends = []
        for k in range(1, N_DEV):
            dev, pj = _peer(k)
            cp = pltpu.make_async_remote_copy(src_ref=g_ref.at[pj], dst_ref=recv.at[me], send_sem=send_sems.at[k - 1],
                                              recv_sem=recv_sems.at[k - 1], device_id=dev, device_id_type=MESH_IDS)
            cp.start()
            sends.append(cp)
        for k in range(1, N_DEV):
            dev, pj = _peer(k)
            pltpu.make_async_remote_copy(src_ref=g_ref.at[pj], dst_ref=recv.at[pj], send_sem=send_sems.at[k - 1],
                                         recv_sem=recv_sems.at[k - 1], device_id=dev, device_id_type=MESH_IDS).wait_recv()
        local.wait()

        def chunk(i, carry):
            r = pl.ds(pl.multiple_of(i * rows, rows), rows)
            g = recv[0, r, :].astype(F32)
            for j in range(1, N_DEV):
                g = g + recv[j, r, :].astype(F32)
            go_ref[r, :] = g
            d_ref[r, :], mo_ref[r, :], vo_ref[r, :] = _adamw(w_ref[r, :], g, m_ref[r, :], v_ref[r, :])
            return carry

        lax.fori_loop(0, R // rows, chunk, 0)
        for cp in sends:
            cp.wait_send()

    return pl.pallas_call(
        body, name=name, in_specs=[ANY_SPEC, VMEM_SPEC, VMEM_SPEC, VMEM_SPEC], out_specs=[VMEM_SPEC] * 4,
        out_shape=[_sds((R, C), F32)] * 4,
        scratch_shapes=[pltpu.VMEM((N_DEV, R, C), BF16), pltpu.SemaphoreType.DMA((N_DEV - 1,)),
                        pltpu.SemaphoreType.DMA((N_DEV - 1,)), pltpu.SemaphoreType.DMA(())],
        compiler_params=pltpu.CompilerParams(vmem_limit_bytes=VMEM_LIMIT),
    )(gblk, w, m, v)


def _reduce_scatter_adamw_two_level(name, gblk, w, m, v):
    _, R, C = gblk.shape
    rows = max(d for d in range(16, min(R, 128) + 1, 16) if R % d == 0)

    def body(g_ref, w_ref, m_ref, v_ref, go_ref, d_ref, mo_ref, vo_ref, from_sib, to_chips, from_chips, own,
             send_sems, recv_sems):
        x_, y_, c_ = lax.axis_index("x"), lax.axis_index("y"), lax.axis_index("c")
        sibling = (x_, y_, 1 - c_)
        chips = [(x_, y_), (1 - x_, y_), (x_, 1 - y_), (1 - x_, 1 - y_)]

        def idx(chip, core):
            return 4 * chip[0] + 2 * chip[1] + core

        def remote(k, src, dst, to):
            return pltpu.make_async_remote_copy(src_ref=src, dst_ref=dst, send_sem=send_sems.at[k],
                                                recv_sem=recv_sems.at[k], device_id=to, device_id_type=MESH_IDS)

        step1 = [remote(q, g_ref.at[idx(chip, 1 - c_)], from_sib.at[q], sibling) for q, chip in enumerate(chips)]
        for cp in step1:
            cp.start()
        for cp in step1:
            cp.wait_recv()

        def chip_sums(i, carry):
            r = pl.ds(pl.multiple_of(i * rows, rows), rows)
            for q, chip in enumerate(chips):
                part = g_ref[idx(chip, c_), r, :].astype(F32) + from_sib[q, r, :].astype(F32)
                if q == 0:
                    own[r, :] = part
                else:
                    to_chips[q - 1, r, :] = part.astype(BF16)
            return carry

        lax.fori_loop(0, R // rows, chip_sums, 0)
        step2 = [remote(4 + j, to_chips.at[j], from_chips.at[j], (*chip, c_)) for j, chip in enumerate(chips[1:])]
        for cp in step2:
            cp.start()
        for cp in step2:
            cp.wait_recv()

        def finish(i, carry):
            r = pl.ds(pl.multiple_of(i * rows, rows), rows)
            g = own[r, :]
            for j in range(3):
                g = g + from_chips[j, r, :].astype(F32)
            go_ref[r, :] = g
            d_ref[r, :], mo_ref[r, :], vo_ref[r, :] = _adamw(w_ref[r, :], g, m_ref[r, :], v_ref[r, :])
            return carry

        lax.fori_loop(0, R // rows, finish, 0)
        for cp in step1 + step2:
            cp.wait_send()

    return pl.pallas_call(
        body, name=name, in_specs=[VMEM_SPEC] * 4, out_specs=[VMEM_SPEC] * 4, out_shape=[_sds((R, C), F32)] * 4,
        scratch_shapes=[pltpu.VMEM((4, R, C), BF16), pltpu.VMEM((3, R, C), BF16), pltpu.VMEM((3, R, C), BF16),
                        pltpu.VMEM((R, C), F32), pltpu.SemaphoreType.DMA((N_DEV - 1,)),
                        pltpu.SemaphoreType.DMA((N_DEV - 1,))],
        compiler_params=pltpu.CompilerParams(vmem_limit_bytes=VMEM_LIMIT),
    )(gblk, w, m, v)


FIRST_WEIGHTS = ["mla_w_dq", "mla_w_uq", "mla_w_dkv", "mla_w_ukv"]
LATE_WEIGHTS = ["mla_w_o", "swa_w_qkv", "swa_w_o", "w_ff1", "w_ff2"]
ROW_SHARDED = {"mla_w_dq", "mla_w_dkv", "mla_w_o", "swa_w_o", "w_ff2"}


def _unblock(name, blocks):
    sh = blocks.shape[1:]
    if name in ROW_SHARDED:
        return jnp.moveaxis(blocks, 0, 1).reshape(sh[0], N_DEV * sh[1], sh[2])
    return jnp.moveaxis(blocks, 0, 2).reshape(sh[0], sh[1], N_DEV * sh[2])


def _block(name, full):
    L, K, N = full.shape
    if name in ROW_SHARDED:
        return jnp.moveaxis(full.reshape(L, N_DEV, K // N_DEV, N), 1, 0)
    return jnp.moveaxis(full.reshape(L, K, N_DEV, N // N_DEV), 2, 0)


def _rot_cols(w):
    half = QK_ROPE // 2
    return jnp.concatenate([-w[..., half:], w[..., :half]], axis=-1)


def _unrot_cols(gw):
    half = QK_ROPE // 2
    return jnp.concatenate([gw[..., half:], -gw[..., :half]], axis=-1)


def _row(v):
    return v.reshape(1, -1)


def _mlp_block_bwd(dx, sv, w1, w2, g, sc, gt):
    dy, du, dgt = _mlp_bwd_a(dx, sv["y2"], sv["rl"], gt, w2)
    dw2 = _matmul_tn("dw_ff2", sv["act"], dy, BF16)
    dw1 = _matmul_tn("dw_ff1", sv["h2"], du, BF16, column_blocks=True)
    dxo, dsh, da = _mlp_bwd_b(du, sv["x1"], dx, w1, g, sc)
    return dxo, dw1, dw2, dsh, da, dgt


def kernel(x, c, positions, w_ada, b_ada, g_mix, g_mlp, mla_w_dq, mla_g_q, mla_w_uq, mla_w_dkv, mla_g_kv, mla_w_ukv, mla_w_o, swa_w_qkv, swa_b_qkv, swa_sinks, swa_w_o, swa_b_o, w_ff1, w_ff2, g_final, loss_target, m_w_ada, m_b_ada, m_g_mix, m_g_mlp, m_mla_w_dq, m_mla_g_q, m_mla_w_uq, m_mla_w_dkv, m_mla_g_kv, m_mla_w_ukv, m_mla_w_o, m_swa_w_qkv, m_swa_b_qkv, m_swa_sinks, m_swa_w_o, m_swa_b_o, m_w_ff1, m_w_ff2, m_g_final, v_w_ada, v_b_ada, v_g_mix, v_g_mlp, v_mla_w_dq, v_mla_g_q, v_mla_w_uq, v_mla_w_dkv, v_mla_g_kv, v_mla_w_ukv, v_mla_w_o, v_swa_w_qkv, v_swa_b_qkv, v_swa_sinks, v_swa_w_o, v_swa_b_o, v_w_ff1, v_w_ff2, v_g_final):
    S, D = x.shape[1], x.shape[2]
    me = _me()
    x0 = x[0]
    target = loss_target[0]
    big_w = dict(mla_w_dq=mla_w_dq, mla_w_uq=mla_w_uq, mla_w_dkv=mla_w_dkv, mla_w_ukv=mla_w_ukv, mla_w_o=mla_w_o,
                 swa_w_qkv=swa_w_qkv, swa_w_o=swa_w_o, w_ff1=w_ff1, w_ff2=w_ff2)
    big_m = dict(mla_w_dq=m_mla_w_dq, mla_w_uq=m_mla_w_uq, mla_w_dkv=m_mla_w_dkv, mla_w_ukv=m_mla_w_ukv,
                 mla_w_o=m_mla_w_o, swa_w_qkv=m_swa_w_qkv, swa_w_o=m_swa_w_o, w_ff1=m_w_ff1, w_ff2=m_w_ff2)
    big_v = dict(mla_w_dq=v_mla_w_dq, mla_w_uq=v_mla_w_uq, mla_w_dkv=v_mla_w_dkv, mla_w_ukv=v_mla_w_ukv,
                 mla_w_o=v_mla_w_o, swa_w_qkv=v_swa_w_qkv, swa_w_o=v_swa_w_o, w_ff1=v_w_ff1, w_ff2=v_w_ff2)
    groups = {"first": FIRST_WEIGHTS}
    wrows = {n: -(-big_w[n].size // (PACK_COLS * 16)) * 16 for n in FIRST_WEIGHTS}
    offs = {g: np.concatenate([[0], np.cumsum([wrows[n] for n in names])]).astype(int) for g, names in groups.items()}

    def as_rows(n, a, lead=()):
        flat = a.reshape(lead + (-1,))
        pad = wrows[n] * PACK_COLS - flat.shape[-1]
        if pad:
            flat = jnp.pad(flat, ((0, 0),) * len(lead) + ((0, pad),))
        return flat.reshape(lead + (wrows[n], PACK_COLS))

    def pack(g, d):
        return jnp.concatenate([as_rows(n, d[n]) for n in groups[g]], axis=0)

    def pack_blocks(g, gfull):
        return jnp.concatenate([as_rows(n, _block(n, gfull[n]).astype(BF16), (N_DEV,)) for n in groups[g]], axis=1)

    def unpack(g, packed, lead=()):
        out = {}
        for i, n in enumerate(groups[g]):
            part = packed[..., int(offs[g][i]):int(offs[g][i + 1]), :].reshape(lead + (-1,))
            out[n] = part[..., :big_w[n].size].reshape(lead + big_w[n].shape)
        return out

    gathered = _all_gather_two_level("gather_weights", pack("first", big_w), BF16)
    wfull = {n: _unblock(n, b) for n, b in unpack("first", gathered, (N_DEV,)).items()}
    w_dq, w_dkv = wfull["mla_w_dq"][0], wfull["mla_w_dkv"][0]
    w_cat = jnp.concatenate([w_dq, w_dkv, _rot_cols(w_dkv[:, KV_LORA:])], axis=1)
    QL = w_dq.shape[1]
    w_uq = wfull["mla_w_uq"][0].reshape(QL, MLA_HEADS, QK_DIM)
    w_uqx = jnp.concatenate([w_uq, _rot_cols(w_uq[..., QK_NOPE:])], axis=-1).reshape(QL, MLA_HEADS * 256)
    w_ukv = wfull["mla_w_ukv"][0]

    L = w_ada.shape[0]
    NC = w_ada.shape[2]
    nbq, nbo = swa_b_qkv.shape[1], swa_b_o.shape[1]
    cpad = -(-(D + nbq + nbo) // 1024) * 1024
    cpack = jnp.pad(jnp.concatenate([c[0], swa_b_qkv[0], swa_b_o[0]]), (0, cpad - (D + nbq + nbo))).reshape(8, cpad // 8)
    call = _all_gather("gather_c", cpack, F32).reshape(N_DEV, cpad)
    c_all = call[:, :D]
    b_qkv_full = call[:, D:D + nbq].reshape(1, N_DEV * nbq)
    b_o_full = call[:, D + nbq:D + nbq + nbo].reshape(1, N_DEV * nbo)
    mod_cols = _ada_fwd(c_all, w_ada)
    mpad = -(-(L * NC) // 1024) * 1024
    mod_send = jnp.pad(jnp.moveaxis(mod_cols, 1, 0).reshape(N_DEV, L * NC), ((0, 0), (0, mpad - L * NC)))
    mod_mine = _all_to_all("exchange_mod", mod_send.reshape(N_DEV, 8, mpad // 8)).reshape(N_DEV, mpad)[:, :L * NC]
    mod = jnp.moveaxis(mod_mine.reshape(N_DEV, L, NC), 0, 1).reshape(L, N_DEV * NC) + b_ada
    mods = mod.reshape(L, 6, 1, D)

    half = QK_ROPE // 2
    inv_freq = ROPE_THETA ** (-jnp.arange(half, dtype=F32) / half)
    ang = positions[0].astype(F32)[:, None] * inv_freq
    cos = jnp.concatenate([jnp.cos(ang), jnp.cos(ang)], axis=-1)
    sin = jnp.concatenate([jnp.sin(ang), jnp.sin(ang)], axis=-1)

    T_ATT = ATT_TILE
    zero_bias = jnp.zeros((1, D), F32)

    sh1, sc1, gt1, sh2, sc2, gt2 = [mods[0, i] for i in range(6)]
    gm0, gp0 = _row(g_mix[0]), _row(g_mlp[0])
    h1, cqp, cq, ckvp, ckv, q, k, v, vt = _mla_in_fwd(x0, cos, sin, gm0, sc1, sh1, w_cat, mla_g_q, w_uqx, mla_g_kv,
                                                      w_ukv, ROW_TILE)
    o0, lse0, gathered = _mla_attn_fwd(q, k, vt, ATT_TILE_FWD, [big_w[n].astype(BF16) for n in LATE_WEIGHTS])
    wfull = {n: _unblock(n, b) for n, b in zip(LATE_WEIGHTS, gathered)}
    w_o_mla, w_qkv, w_o_swa = wfull["mla_w_o"][0], wfull["swa_w_qkv"][0], wfull["swa_w_o"][0]
    ff1, ff2 = wfull["w_ff1"], wfull["w_ff2"]
    y1, x1, h2 = _attn_out_fwd(o0, x0, w_o_mla, zero_bias, gt1, gp0, sc2, sh2)
    rl0, act0, y2, x2 = _mlp_fwd(h2, x1, ff1[0], ff2[0], gt2)
    sv0 = dict(y2=y2, rl=rl0, act=act0, h2=h2, x1=x1)

    th1, tc1, tg1, th2, tc2, tg2 = [mods[1, i] for i in range(6)]
    gm1, gp1 = _row(g_mix[1]), _row(g_mlp[1])
    h3, sq, sk, svv = _swa_in_fwd(x2, gm1, tc1, th1, w_qkv, b_qkv_full)
    o1, lse1, y3, x3, h4 = _swa_attn_fwd(sq, sk, svv, swa_sinks, x2, w_o_swa, b_o_full, tg1, gp1, tc2, th2)
    rl1, act1, y4, dx4, loss_part, dg_final = _mlp_fwd_loss(h4, x3, ff1[1], ff2[1], tg2, target, _row(g_final))
    sv1 = dict(y2=y4, rl=rl1, act=act1, h2=h4, x1=x3)

    dx3, dw1_1, dw2_1, dsh2_1, da2_1, dgt2_1 = _mlp_block_bwd(dx4, sv1, ff1[1], ff2[1], gp1, tc2, tg2)
    dsq, dsk, dsv, dsink, dy, dgt1_1, db_o = _swa_attn_bwd(sq, sk, svv, lse1, swa_sinks, dx3, y3, o1, tg1, w_o_swa)
    dw_o_swa = _matmul_tn("dw_o", o1, dy, BF16)
    dqkv, dx2, db_qkv, dsh1_1, da1_1 = _swa_in_bwd(dsq, dsk, dsv, x2, dx3, w_qkv, gm1, tc1)
    dw_qkv = _matmul_tn("dw_qkv", h3, dqkv, BF16)

    dx1, dw1_0, dw2_0, dsh2_0, da2_0, dgt2_0 = _mlp_block_bwd(dx2, sv0, ff1[0], ff2[0], gp0, sc2, gt2)
    dy, do, dl, dgt1_0, _ = _attn_out_bwd(dx1, y1, o0, gt1, w_o_mla, MLA_HEADS)
    dw_o_mla = _matmul_tn("dw_o", o0, dy, BF16)
    tb = min(T_ATT, S)
    delta = dl[:MLA_HEADS].reshape(MLA_HEADS, S // tb, 1, tb)
    glate = dict(mla_w_o=dw_o_mla[None], swa_w_qkv=dw_qkv[None], swa_w_o=dw_o_swa[None],
                 w_ff2=jnp.stack([dw2_0, dw2_1]))
    gblocks = {n: _block(n, g).astype(BF16) for n, g in glate.items()}
    gblocks["w_ff1"] = jnp.stack([dw1_0, dw1_1], axis=1)
    lse_rows = (lse0 * LOG2E).reshape(MLA_HEADS, S // tb, 1, tb)
    dq, dk, dv, recv = _mla_attn_bwd(q, k, v, do, lse_rows, delta, T_ATT, [gblocks[n] for n in LATE_WEIGHTS])
    late = {n: _sum_adamw(r, big_w[n], big_m[n], big_v[n]) for n, r in zip(LATE_WEIGHTS, recv)}
    dqx, dkv, dcat, dx0, dg_q, dg_kv, dsh1_0, da1_0 = _mla_in_bwd(
        dq, dk, dv, cos, sin, cqp, ckvp, x0, dx1, w_uqx, mla_g_q, w_ukv, mla_g_kv, w_cat, gm0, sc1)
    dw_uqx = _matmul_tn("dw_uq", cq, dqx).reshape(QL, MLA_HEADS, 256)
    dw_ukv = _matmul_tn("dw_ukv", ckv, dkv)
    dw_cat = _matmul_tn("dw_down", h1, dcat)
    dw_uq = jnp.concatenate([dw_uqx[..., :QK_NOPE], dw_uqx[..., 128:192] + _unrot_cols(dw_uqx[..., 192:256])],
                            axis=-1).reshape(QL, MLA_HEADS * QK_DIM)
    o_kr = QL + KV_LORA
    dw_dkv = jnp.concatenate([dw_cat[:, QL:o_kr],
                              dw_cat[:, o_kr:o_kr + QK_ROPE] + _unrot_cols(dw_cat[:, o_kr + QK_ROPE:])], axis=1)

    gfirst = dict(mla_w_dq=dw_cat[None, :, :QL], mla_w_uq=dw_uq[None], mla_w_dkv=dw_dkv[None], mla_w_ukv=dw_ukv[None])
    first = _reduce_scatter_adamw_two_level("grad_exchange_adamw", pack_blocks("first", gfirst), pack("first", big_w),
                                  pack("first", big_m), pack("first", big_v))
    big_g, big_d, big_nm, big_nv = ({**unpack("first", first[j]), **{n: late[n][j] for n in LATE_WEIGHTS}}
                                    for j in range(4))

    dmod = jnp.stack([
        jnp.concatenate([dsh1_0, gm0 * da1_0, dgt1_0, dsh2_0, gp0 * da2_0, dgt2_0], axis=1),
        jnp.concatenate([dsh1_1, gm1 * da1_1, dgt1_1, dsh2_1, gp1 * da2_1, dgt2_1], axis=1)]).reshape(-1)
    dg_mix = jnp.concatenate([(1.0 + sc1) * da1_0, (1.0 + tc1) * da1_1], axis=1).reshape(-1)
    dg_mlp = jnp.concatenate([(1.0 + sc2) * da2_0, (1.0 + tc2) * da2_1], axis=1).reshape(-1)
    parts = [loss_part.reshape(-1), dmod, dg_mix, dg_mlp, dg_q.reshape(-1), dg_kv.reshape(-1), dsink.reshape(-1),
             dg_final.reshape(-1), db_qkv.reshape(-1), db_o.reshape(-1)]
    soffs = np.concatenate([[0], np.cumsum([p.size for p in parts])])
    spad = -(-int(soffs[-1]) // 1024) * 1024
    spack = jnp.pad(jnp.concatenate(parts), (0, spad - int(soffs[-1]))).reshape(8, spad // 8)
    sall = _all_gather("gather_small_grads", spack, F32)
    ssum = _sum_devices(sall).reshape(-1)
    tot = [ssum[int(soffs[i]):int(soffs[i + 1])] for i in range(len(parts))]
    loss = tot[0][0]
    nsink = swa_sinks.shape[1]
    small_g = dict(b_ada=tot[1].reshape(b_ada.shape), g_mix=tot[2].reshape(g_mix.shape), g_mlp=tot[3].reshape(g_mlp.shape),
                   mla_g_q=tot[4].reshape(mla_g_q.shape), mla_g_kv=tot[5].reshape(mla_g_kv.shape),
                   swa_sinks=tot[6][:nsink].reshape(swa_sinks.shape), g_final=tot[7].reshape(g_final.shape),
                   swa_b_qkv=lax.dynamic_slice(tot[8], (me * nbq,), (nbq,)).reshape(swa_b_qkv.shape),
                   swa_b_o=lax.dynamic_slice(tot[9], (me * nbo,), (nbo,)).reshape(swa_b_o.shape))
    small_w = dict(b_ada=b_ada, g_mix=g_mix, g_mlp=g_mlp, mla_g_q=mla_g_q, mla_g_kv=mla_g_kv, swa_sinks=swa_sinks,
                   g_final=g_final, swa_b_qkv=swa_b_qkv, swa_b_o=swa_b_o)
    small_m = dict(b_ada=m_b_ada, g_mix=m_g_mix, g_mlp=m_g_mlp, mla_g_q=m_mla_g_q, mla_g_kv=m_mla_g_kv,
                   swa_sinks=m_swa_sinks, g_final=m_g_final, swa_b_qkv=m_swa_b_qkv, swa_b_o=m_swa_b_o)
    small_v = dict(b_ada=v_b_ada, g_mix=v_g_mix, g_mlp=v_g_mlp, mla_g_q=v_mla_g_q, mla_g_kv=v_mla_g_kv,
                   swa_sinks=v_swa_sinks, g_final=v_g_final, swa_b_qkv=v_swa_b_qkv, swa_b_o=v_swa_b_o)
    SMALL = list(small_w)
    woffs = np.concatenate([[0], np.cumsum([small_w[n].size for n in SMALL])])
    wpad = -(-int(woffs[-1]) // 1024) * 1024

    def spack_of(d):
        flat = jnp.concatenate([d[n].reshape(-1) for n in SMALL])
        return jnp.pad(flat, (0, wpad - int(woffs[-1]))).reshape(8, wpad // 8)

    sm = _adamw_small(spack_of(small_w), spack_of(small_g), spack_of(small_m), spack_of(small_v))
    small_d, small_nm, small_nv = (
        {n: a.reshape(-1)[int(woffs[i]):int(woffs[i + 1])].reshape(small_w[n].shape) for i, n in enumerate(SMALL)}
        for a in sm)

    b_off = int(soffs[1])
    dmod_all = sall.reshape(N_DEV, -1)[:, b_off:b_off + L * N_DEV * NC].reshape(N_DEV, L, N_DEV * NC)
    dmod_cols = jnp.moveaxis(lax.dynamic_slice_in_dim(dmod_all, me * NC, NC, axis=2), 0, 1)
    ada_g, ada_d, ada_nm, ada_nv = _ada_bwd_adamw(c_all.T, dmod_cols, w_ada, m_w_ada, v_w_ada)

    order = ["w_ada", "b_ada", "g_mix", "g_mlp", "mla_w_dq", "mla_g_q", "mla_w_uq", "mla_w_dkv", "mla_g_kv",
             "mla_w_ukv", "mla_w_o", "swa_w_qkv", "swa_b_qkv", "swa_sinks", "swa_w_o", "swa_b_o", "w_ff1", "w_ff2", "g_final"]

    def collect(ada, big, small):
        return [ada if n == "w_ada" else (big[n] if n in big else small[n]) for n in order]

    return (loss, dx0.reshape(x.shape), *collect(ada_g, big_g, small_g), *collect(ada_d, big_d, small_d),
            *collect(ada_nm, big_nm, small_nm), *collect(ada_nv, big_nv, small_nv))
```

```python
import jax
import jax.numpy as jnp
import numpy as np
from jax import lax
from jax.experimental import pallas as pl
from jax.experimental.pallas import tpu as pltpu

F32 = jnp.float32
BF16 = jnp.bfloat16
MESH_IDS = pl.DeviceIdType.MESH
N_DEV = 8

MLA_HEADS = 8
QK_NOPE = 128
QK_ROPE = 64
QK_DIM = QK_NOPE + QK_ROPE
V_DIM = 128
KV_LORA = 256
ROPE_THETA = 10000.0
SWA_HEADS = 16
SWA_KV_HEADS = 4
SWA_GROUP = SWA_HEADS // SWA_KV_HEADS
SWA_HEAD_DIM = 64
WINDOW = 128
EPS = 1e-6
LOG2E = 1.4426950408889634

ADAM_LR = 0.001
ADAM_B1 = 0.9
ADAM_B2 = 0.999
ADAM_EPS = 1e-08
ADAM_WD = 0.01
ADAM_STEP = 10

PACK_COLS = 1024
VMEM_LIMIT = 56 << 20
ROW_TILE = 512
ROW_TILE_IO = 1024
ROW_TILE_WIDE = 512
ROW_TILE_BWD = 512
ATT_TILE = 512
ATT_TILE_FWD = 1024
TN_TOKENS = 4096


def _dot(a, b):
    return jnp.dot(a, b, preferred_element_type=F32)


def _dot_nt(a, b):
    return lax.dot_general(a, b, (((1,), (1,)), ((), ())), preferred_element_type=F32)


def _dot_tn(a, b):
    return lax.dot_general(a, b, (((0,), (0,)), ((), ())), preferred_element_type=F32)


def _rstd(x):
    return lax.rsqrt(jnp.mean(x * x, axis=-1, keepdims=True) + EPS)


def _rms_bwd(dn, n, r):
    return r * (dn - n * jnp.mean(dn * n, axis=-1, keepdims=True))


def _modulate(x, g, sc, sh):
    r = _rstd(x)
    return ((x * r) * g) * (1.0 + sc) + sh


def _modulate_bwd(dh, x, g, sc):
    r = _rstd(x)
    n = x * r
    dsh = jnp.sum(dh, axis=0, keepdims=True)
    da = jnp.sum(dh * n, axis=0, keepdims=True)
    dx = _rms_bwd(dh * (g * (1.0 + sc)), n, r)
    return dx, dsh, da


def _acc(ref, val, i):
    @pl.when(i == 0)
    def _():
        ref[...] = val

    @pl.when(i != 0)
    def _():
        ref[...] += val


def _row_spec(shape, tm):
    nd = len(shape)
    return pl.BlockSpec(tuple(shape[:nd - 2]) + (tm, shape[-1]), lambda i: (0,) * (nd - 2) + (i, 0))


def _resident_spec(shape, single_buffer):
    nd = len(shape)
    if single_buffer:
        return pl.BlockSpec(tuple(shape), lambda i: (0,) * nd, pipeline_mode=pl.Buffered(1))
    return pl.BlockSpec(tuple(shape), lambda i: (0,) * nd)


def _rowcall(name, body, tokens, tm, row_in, full_in, row_out, acc_out=()):
    tm = min(tm, tokens)
    in_specs = [_row_spec(a.shape, tm) for a in row_in] + [_resident_spec(a.shape, True) for a in full_in]
    row_specs = [s[1] if isinstance(s, tuple) else _row_spec(s.shape, tm) for s in row_out]
    row_out = [s[0] if isinstance(s, tuple) else s for s in row_out]
    out_specs = row_specs + [_resident_spec(s.shape, False) for s in acc_out]
    return pl.pallas_call(
        body, name=name, grid=(tokens // tm,), in_specs=in_specs, out_specs=out_specs,
        out_shape=list(row_out) + list(acc_out),
        compiler_params=pltpu.CompilerParams(dimension_semantics=("arbitrary",), vmem_limit_bytes=VMEM_LIMIT),
    )(*row_in, *full_in)


def _sds(shape, dtype):
    return jax.ShapeDtypeStruct(tuple(shape), dtype)


def _mla_in_fwd(x, cos, sin, g, sc, sh, w_cat, g_q, w_uqx, g_kv, w_ukv, t):
    S, D = x.shape
    QL = g_q.shape[1]
    H = MLA_HEADS
    t = min(t, S)

    def body(x_ref, cos_ref, sin_ref, g_ref, sc_ref, sh_ref, wcat_ref, gq_ref, wuqx_ref, gkv_ref, wukv_ref,
             h_ref, cqp_ref, cq_ref, ckvp_ref, ckv_ref, q_ref, k_ref, v_ref, vt_ref):
        cs, sn = cos_ref[...], sin_ref[...]
        hb = _modulate(x_ref[...], g_ref[...], sc_ref[...], sh_ref[...]).astype(BF16)
        h_ref[...] = hb
        low = _dot(hb, wcat_ref[...])
        cqp = low[:, :QL]
        cqp_ref[...] = cqp
        cq = ((cqp * _rstd(cqp)) * gq_ref[...]).astype(BF16)
        cq_ref[...] = cq
        ckvp = low[:, QL:QL + KV_LORA]
        ckvp_ref[...] = ckvp
        ckv = ((ckvp * _rstd(ckvp)) * gkv_ref[...]).astype(BF16)
        ckv_ref[...] = ckv
        o = QL + KV_LORA
        kr = (low[:, o:o + QK_ROPE] * cs + low[:, o + QK_ROPE:o + 2 * QK_ROPE] * sn).astype(BF16)
        qx = _dot(cq, wuqx_ref[...])
        kv = _dot(ckv, wukv_ref[...])
        for hd in range(H):
            b = hd * 256
            q_ref[hd, :, 0:QK_NOPE] = qx[:, b:b + QK_NOPE].astype(BF16)
            q_ref[hd, :, QK_NOPE:QK_DIM] = (qx[:, b + 128:b + 192] * cs + qx[:, b + 192:b + 256] * sn).astype(BF16)
            k_ref[hd, :, 0:QK_NOPE] = kv[:, b:b + QK_NOPE].astype(BF16)
            k_ref[hd, :, QK_NOPE:QK_DIM] = kr
            vh = kv[:, b + 128:b + 256]
            v_ref[hd] = vh.astype(BF16)
            vt_ref[hd, 0, 0:V_DIM, :] = vh.T.astype(BF16)
            vt_ref[hd, 0, V_DIM:2 * V_DIM, :] = jnp.ones((V_DIM, x_ref.shape[0]), BF16)

    vt_spec = pl.BlockSpec((H, 1, 2 * V_DIM, t), lambda i: (0, i, 0, 0))
    return _rowcall(
        "mla_in_fwd", body, S, t, [x, cos, sin], [g, sc, sh, w_cat, g_q, w_uqx, g_kv, w_ukv],
        [_sds((S, D), BF16), _sds((S, QL), F32), _sds((S, QL), BF16), _sds((S, KV_LORA), F32), _sds((S, KV_LORA), BF16),
         _sds((H, S, QK_DIM), BF16), _sds((H, S, QK_DIM), BF16), _sds((H, S, V_DIM), BF16),
         (_sds((H, S // t, 2 * V_DIM, t), BF16), vt_spec)])


def _mla_attn_fwd(q, k, vt, t, sends):
    H, S, DQ = q.shape
    DV = V_DIM
    vb = vt.shape[-1]
    t = max(min(t, S), vb)
    nb = S // t
    scale = QK_DIM ** -0.5
    c2 = scale * LOG2E

    ns = len(sends)

    def body(q_ref, k_ref, vt_ref, *rest):
        send_refs, (o_ref, lse_ref), gath_refs = rest[:ns], rest[ns:ns + 2], rest[ns + 2:2 * ns + 2]
        m_s, acc_s, s_buf, send_sems, recv_sems, local_sems = rest[2 * ns + 2:]
        hd, qi = pl.program_id(0), pl.program_id(1)

        def gather():
            return _exchange_copies([lambda j, r=r: r for r in send_refs], gath_refs, send_sems, recv_sems, local_sems)

        @pl.when((hd == 0) & (qi == 0))
        def _():
            _start_exchange(gather())

        m_s[...] = jnp.full_like(m_s, -jnp.inf)
        acc_s[...] = jnp.zeros_like(acc_s)

        def scores(j, slot):
            rows = pl.ds(pl.multiple_of(j * t, t), t)
            s_buf[slot] = _dot_nt(k_ref[0, rows, :], q_ref[0])

        nvb = t // vb

        def update(s, j, blocks, cols):
            m_prev = m_s[:, cols]
            m_new = jnp.maximum(m_prev, jnp.max(s, axis=0, keepdims=True))
            alpha = jnp.exp2((m_prev - m_new) * c2)
            pb = jnp.exp2((s - m_new) * c2).astype(BF16)
            acc = alpha * acc_s[:, cols]
            for n, u in enumerate(blocks):
                acc = acc + _dot(vt_ref[0, j * nvb + u], pb[n * vb:(n + 1) * vb, :])
            acc_s[:, cols] = acc
            m_s[:, cols] = m_new

        def causal(s, shape):
            key = lax.broadcasted_iota(jnp.int32, shape, 0)
            qry = lax.broadcasted_iota(jnp.int32, shape, 1)
            return jnp.where(key <= qry, s, -jnp.inf)

        def absorb(j, slot, diagonal):
            if not diagonal:
                update(s_buf[slot], j, range(nvb), slice(None))
            elif nvb % 2:
                update(causal(s_buf[slot], (t, t)), j, range(nvb), slice(None))
            else:
                half = t // 2
                update(causal(s_buf[slot, :half], (half, t)), j, range(nvb // 2), slice(None))
                update(causal(s_buf[slot, half:, half:], (half, half)), j, range(nvb // 2, nvb), slice(half, t))

        def pair(i, carry):
            j = 2 * i
            scores(j + 1, 1)
            absorb(j, 0, False)
            scores(j + 2, 0)
            absorb(j + 1, 1, False)
            return carry

        scores(0, 0)
        lax.fori_loop(0, qi // 2, pair, 0)

        @pl.when(qi % 2 == 0)
        def _():
            absorb(qi, 0, True)

        @pl.when(qi % 2 == 1)
        def _():
            scores(qi, 1)
            absorb(qi - 1, 0, False)
            absorb(qi, 1, True)

        acc = acc_s[...]
        o_ref[...] = (acc[:DV] / acc[DV:]).T.astype(BF16)
        lse_ref[0, 0] = m_s[...] * scale + jnp.log(acc[DV:DV + 1])

        @pl.when((hd == H - 1) & (qi == nb - 1))
        def _():
            _finish_exchange(gather())

    outs = pl.pallas_call(
        body, name="mla_attn_fwd", grid=(H, nb),
        in_specs=[pl.BlockSpec((1, t, DQ), lambda h, i: (h, i, 0)),
                  pl.BlockSpec((1, S, DQ), lambda h, i: (h, 0, 0)),
                  pl.BlockSpec((1, S // vb, 2 * DV, vb), lambda h, i: (h, 0, 0, 0))] + [ANY_SPEC] * ns,
        out_specs=[pl.BlockSpec((t, DV), lambda h, i: (i, h)),
                   pl.BlockSpec((1, 1, 1, t), lambda h, i: (h, i, 0, 0))] + [ANY_SPEC] * ns,
        out_shape=[_sds((S, H * DV), BF16), _sds((H, nb, 1, t), F32)]
        + [_sds((N_DEV,) + a.shape, a.dtype) for a in sends],
        scratch_shapes=[pltpu.VMEM((1, t), F32), pltpu.VMEM((2 * DV, t), F32), pltpu.VMEM((2, t, t), F32)]
        + _comm_sems(ns),
        compiler_params=pltpu.CompilerParams(dimension_semantics=("arbitrary", "arbitrary"),
                                             vmem_limit_bytes=VMEM_LIMIT),
    )(q, k, vt, *sends)
    return outs[0], outs[1], outs[2:]


def _attn_out_fwd(o, x, w_o, b_o, gt, g, sc, sh):
    S, D = x.shape

    def body(o_ref, x_ref, wo_ref, bo_ref, gt_ref, g_ref, sc_ref, sh_ref, y_ref, x1_ref, h_ref):
        y = _dot(o_ref[...], wo_ref[...]) + bo_ref[...]
        y_ref[...] = y.astype(BF16)
        x1 = x_ref[...] + gt_ref[...] * y
        x1_ref[...] = x1
        h_ref[...] = _modulate(x1, g_ref[...], sc_ref[...], sh_ref[...]).astype(BF16)

    return _rowcall("attn_out_fwd", body, S, ROW_TILE_IO, [o, x], [w_o, b_o, gt, g, sc, sh],
                    [_sds((S, D), BF16), _sds((S, D), F32), _sds((S, D), BF16)])


def _mlp_fwd(h, x, w1, w2, gt):
    S, D = x.shape
    FF = w1.shape[1]

    def body(h_ref, x_ref, w1_ref, w2_ref, gt_ref, rl_ref, act_ref, y_ref, x2_ref):
        rl = jnp.maximum(_dot(h_ref[...], w1_ref[...]), 0.0)
        rl_ref[...] = rl.astype(BF16)
        act = (rl * rl).astype(BF16)
        act_ref[...] = act
        y = _dot(act, w2_ref[...])
        y_ref[...] = y.astype(BF16)
        x2_ref[...] = x_ref[...] + gt_ref[...] * y

    return _rowcall("mlp_fwd", body, S, ROW_TILE_WIDE, [h, x], [w1, w2, gt],
                    [_sds((S, FF), BF16), _sds((S, FF), BF16), _sds((S, D), BF16), _sds((S, D), F32)])


def _final_norm_loss(xv, target, g, d_model):
    r = _rstd(xv)
    n = xv * r
    err = n * g - target
    part = 0.5 * jnp.sum(jnp.mean(err * err, axis=-1, keepdims=True), axis=0, keepdims=True)
    dout = err / d_model
    return part, _rms_bwd(dout * g, n, r), jnp.sum(dout * n, axis=0, keepdims=True)


def _mlp_fwd_loss(h, x, w1, w2, gt, target, g_final):
    S, D = x.shape
    FF = w1.shape[1]

    def body(h_ref, x_ref, t_ref, w1_ref, w2_ref, gt_ref, g_ref, rl_ref, act_ref, y_ref, dx_ref, loss_ref, dg_ref):
        i = pl.program_id(0)
        rl = jnp.maximum(_dot(h_ref[...], w1_ref[...]), 0.0)
        rl_ref[...] = rl.astype(BF16)
        act = (rl * rl).astype(BF16)
        act_ref[...] = act
        y = _dot(act, w2_ref[...])
        y_ref[...] = y.astype(BF16)
        part, dx, dg = _final_norm_loss(x_ref[...] + gt_ref[...] * y, t_ref[...], g_ref[...], D)
        dx_ref[...] = dx
        _acc(loss_ref, jnp.broadcast_to(part, loss_ref.shape), i)
        _acc(dg_ref, dg, i)

    return _rowcall("mlp_fwd_loss", body, S, ROW_TILE_WIDE, [h, x, target], [w1, w2, gt, g_final],
                    [_sds((S, FF), BF16), _sds((S, FF), BF16), _sds((S, D), BF16), _sds((S, D), F32)],
                    [_sds((1, 128), F32), _sds((1, D), F32)])


def _swa_in_fwd(x, g, sc, sh, w_qkv, b_qkv):
    S, D = x.shape
    NQ = SWA_HEADS * SWA_HEAD_DIM
    NK = SWA_KV_HEADS * SWA_HEAD_DIM

    def body(x_ref, g_ref, sc_ref, sh_ref, w_ref, b_ref, h_ref, q_ref, k_ref, v_ref):
        hb = _modulate(x_ref[...], g_ref[...], sc_ref[...], sh_ref[...]).astype(BF16)
        h_ref[...] = hb
        qkv = _dot(hb, w_ref[...]) + b_ref[...]
        q_ref[...] = qkv[:, :NQ].astype(BF16)
        k_ref[...] = qkv[:, NQ:NQ + NK].astype(BF16)
        v_ref[...] = qkv[:, NQ + NK:].astype(BF16)

    return _rowcall("swa_in_fwd", body, S, ROW_TILE_IO, [x], [g, sc, sh, w_qkv, b_qkv],
                    [_sds((S, D), BF16), _sds((S, NQ), BF16), _sds((S, NK), BF16), _sds((S, NK), BF16)])


def _alibi_slope(head):
    return float(np.float32(2.0 ** (-8.0 * (head + 1) / SWA_HEADS)))


def _swa_geometry(n):
    W, G = WINDOW, SWA_GROUP
    key = lax.broadcasted_iota(jnp.int32, (2 * W, G * W), 0)
    qry = lax.broadcasted_iota(jnp.int32, (2 * W, G * W), 1) & (W - 1)
    dist = W + qry - key
    valid = (dist >= 0) & (dist < W) & ((n > 0) | (key >= W))
    return dist.astype(F32), valid


def _swa_group(kh, q_ref, sink_ref):
    W, G, Dh = WINDOW, SWA_GROUP, SWA_HEAD_DIM
    heads = [kh * G + g for g in range(G)]
    q4 = jnp.concatenate([q_ref[:, h * Dh:(h + 1) * Dh] for h in heads], axis=0)
    slopes = jnp.concatenate([jnp.full((1, W), _alibi_slope(h), F32) for h in heads], axis=1)
    sinks = jnp.concatenate([jnp.broadcast_to(sink_ref[:, h:h + 1], (1, W)) for h in heads], axis=1)
    return heads, q4, slopes, sinks


def _swa_band_specs(W, nb, cols):
    prev = pl.BlockSpec((W, cols), lambda n: (jnp.maximum(jnp.minimum(n, nb - 1) - 1, 0), 0))
    cur = pl.BlockSpec((W, cols), lambda n: (jnp.minimum(n, nb - 1), 0))
    return prev, cur


def _swa_attn_fwd(q, k, v, sinks, x, w_o, b_o, gt, g, sc, sh):
    S, NQ = q.shape
    NK = k.shape[1]
    D = x.shape[1]
    W, Dh, G = WINDOW, SWA_HEAD_DIM, SWA_GROUP
    nb = S // W

    def body(q_ref, kp_ref, kc_ref, vp_ref, vc_ref, sink_ref, x_ref, wo_ref, bo_ref, gt_ref, g_ref, sc_ref, sh_ref,
             o_ref, lse_ref, y_ref, x1_ref, h_ref):
        distf, valid = _swa_geometry(pl.program_id(0))
        kband = jnp.concatenate([kp_ref[...], kc_ref[...]], axis=0)
        vband_t = jnp.concatenate([vp_ref[...], vc_ref[...]], axis=0).astype(F32).T.astype(BF16)
        outs = []
        for kh in range(SWA_KV_HEADS):
            kb = kband[:, kh * Dh:(kh + 1) * Dh]
            vbt = vband_t[kh * Dh:(kh + 1) * Dh, :]
            heads, q4, slopes, sinks = _swa_group(kh, q_ref, sink_ref)
            s = _dot_nt(kb, q4) * (Dh ** -0.5) - slopes * distf
            s = jnp.where(valid, s, -jnp.inf)
            m = jnp.maximum(jnp.max(s, axis=0, keepdims=True), sinks)
            p = jnp.exp(s - m)
            denom = jnp.sum(p, axis=0, keepdims=True) + jnp.exp(sinks - m)
            out4 = _dot(vbt, (p * (1.0 / denom)).astype(BF16))
            lse4 = m + jnp.log(denom)
            for g, h in enumerate(heads):
                outs.append(out4[:, g * W:(g + 1) * W])
                lse_ref[h:h + 1, :] = lse4[:, g * W:(g + 1) * W]
        ob = jnp.concatenate(outs, axis=0).T.astype(BF16)
        o_ref[...] = ob
        y = _dot(ob, wo_ref[...]) + bo_ref[...]
        y_ref[...] = y.astype(BF16)
        x1 = x_ref[...] + gt_ref[...] * y
        x1_ref[...] = x1
        h_ref[...] = _modulate(x1, g_ref[...], sc_ref[...], sh_ref[...]).astype(BF16)

    kprev, kcur = _swa_band_specs(W, nb, NK)
    blk = lambda cols: pl.BlockSpec((W, cols), lambda n: (n, 0))
    row = pl.BlockSpec((1, D), lambda n: (0, 0))
    return pl.pallas_call(
        body, name="swa_attn_fwd", grid=(nb,),
        in_specs=[blk(NQ), kprev, kcur, kprev, kcur, pl.BlockSpec((1, SWA_HEADS), lambda n: (0, 0)), blk(D),
                  pl.BlockSpec(w_o.shape, lambda n: (0, 0), pipeline_mode=pl.Buffered(1)), row, row, row, row, row],
        out_specs=[blk(NQ), pl.BlockSpec((SWA_HEADS, W), lambda n: (0, n)), blk(D), blk(D), blk(D)],
        out_shape=[_sds((S, NQ), BF16), _sds((SWA_HEADS, S), F32), _sds((S, D), BF16), _sds((S, D), F32),
                   _sds((S, D), BF16)],
        compiler_params=pltpu.CompilerParams(dimension_semantics=("arbitrary",), vmem_limit_bytes=VMEM_LIMIT),
    )(q, k, k, v, v, sinks, x, w_o, b_o, gt, g, sc, sh)


def _mlp_bwd_a(dx, y, rl, gt, w2):
    S, D = dx.shape
    FF = rl.shape[1]

    def body(dx_ref, y_ref, rl_ref, gt_ref, w2_ref, dy_ref, du_ref, dgt_ref):
        i = pl.program_id(0)
        dxv = dx_ref[...]
        _acc(dgt_ref, jnp.sum(dxv * y_ref[...].astype(F32), axis=0, keepdims=True), i)
        dy = (dxv * gt_ref[...]).astype(BF16)
        dy_ref[...] = dy
        dact = _dot_nt(dy, w2_ref[...])
        du_ref[...] = (dact * (2.0 * rl_ref[...].astype(F32))).astype(BF16)

    return _rowcall("mlp_bwd_a", body, S, ROW_TILE_BWD, [dx, y, rl], [gt, w2],
                    [_sds((S, D), BF16), _sds((S, FF), BF16)], [_sds((1, D), F32)])


def _mlp_bwd_b(du, x, dx, w1, g, sc):
    S, D = x.shape

    def body(du_ref, x_ref, dx_ref, w1_ref, g_ref, sc_ref, dxo_ref, dsh_ref, da_ref):
        i = pl.program_id(0)
        dh = _dot_nt(du_ref[...], w1_ref[...])
        dxn, dsh, da = _modulate_bwd(dh, x_ref[...], g_ref[...], sc_ref[...])
        dxo_ref[...] = dx_ref[...] + dxn
        _acc(dsh_ref, dsh, i)
        _acc(da_ref, da, i)

    return _rowcall("mlp_bwd_b", body, S, ROW_TILE_BWD, [du, x, dx], [w1, g, sc],
                    [_sds((S, D), F32)], [_sds((1, D), F32), _sds((1, D), F32)])


def _attn_out_bwd(dx, y, o, gt, w_o, n_heads):
    S, D = dx.shape
    NO = o.shape[1]
    dh = NO // n_heads
    member = (jnp.arange(NO)[None, :] // dh == jnp.arange(16)[:, None]).astype(BF16)

    def body(dx_ref, y_ref, o_ref, gt_ref, wo_ref, mem_ref, dy_ref, do_ref, dl_ref, dgt_ref, dbo_ref):
        i = pl.program_id(0)
        dxv = dx_ref[...]
        _acc(dgt_ref, jnp.sum(dxv * y_ref[...].astype(F32), axis=0, keepdims=True), i)
        dy = dxv * gt_ref[...]
        _acc(dbo_ref, jnp.sum(dy, axis=0, keepdims=True), i)
        dyb = dy.astype(BF16)
        dy_ref[...] = dyb
        do = _dot_nt(dyb, wo_ref[...])
        do_ref[...] = do.astype(BF16)
        prod = do * o_ref[...].astype(F32)
        hi = prod.astype(BF16)
        lo = (prod - hi.astype(F32)).astype(BF16)
        dl_ref[...] = _dot_nt(mem_ref[...], hi) + _dot_nt(mem_ref[...], lo)

    tm = min(ROW_TILE_IO, S)
    return _rowcall("attn_out_bwd", body, S, ROW_TILE_IO, [dx, y, o], [gt, w_o, member],
                    [_sds((S, D), BF16), _sds((S, NO), BF16),
                     (_sds((16, S), F32), pl.BlockSpec((16, tm), lambda i: (0, i)))],
                    [_sds((1, D), F32), _sds((1, D), F32)])


def _mla_attn_bwd(q, k, v, do, lse, delta, t, gblks):
    H, S, DQ = q.shape
    DV = V_DIM
    t = min(t, S // 2)
    tk = 2 * t
    nq, nk = S // t, S // tk
    scale = QK_DIM ** -0.5
    c2 = scale * LOG2E

    ng = len(gblks)

    def body(q_ref, k_ref, v_ref, do_ref, lse_ref, dl_ref, *rest):
        g_refs, (dq_ref, dk_ref, dv_ref), recv_refs = rest[:ng], rest[ng:ng + 3], rest[ng + 3:2 * ng + 3]
        dk_s, dv_s, s_buf, dp_buf, send_sems, recv_sems, local_sems = rest[2 * ng + 3:]
        hd, kj = pl.program_id(0), pl.program_id(1)

        def scatter():
            return _exchange_copies([lambda j, r=r: r.at[j] for r in g_refs], recv_refs, send_sems, recv_sems,
                                    local_sems)

        @pl.when((hd == 0) & (kj == 0))
        def _():
            _start_exchange(scatter())

        @pl.when(kj == 0)
        def _():
            dq_ref[...] = jnp.zeros_like(dq_ref)

        dk_s[...] = jnp.zeros_like(dk_s)
        dv_s[...] = jnp.zeros_like(dv_s)

        def products(i, slot, keys=tk):
            rows = pl.ds(pl.multiple_of(i * t, t), t)
            s_buf[slot, :keys] = _dot_nt(k_ref[0, :keys], q_ref[0, rows, :])
            dp_buf[slot, :keys] = _dot_nt(v_ref[0, :keys], do_ref[rows, :])

        def absorb(i, slot, diagonal, keys=tk):
            rows = pl.ds(pl.multiple_of(i * t, t), t)
            qb, dob = q_ref[0, rows, :], do_ref[rows, :]
            p = jnp.exp2(s_buf[slot, :keys] * c2 - lse_ref[0, i])
            if diagonal is not None:
                key = lax.broadcasted_iota(jnp.int32, (keys, t), 0)
                qry = lax.broadcasted_iota(jnp.int32, (keys, t), 1) + diagonal * t
                p = jnp.where(key <= qry, p, 0.0)
            dv_s[:keys] += _dot(p.astype(BF16), dob)
            ds = (p * (dp_buf[slot, :keys] - dl_ref[0, i])).astype(BF16)
            dk_s[:keys] += _dot(ds, qb)
            dq_ref[0, rows, :] += _dot_tn(ds, k_ref[0, :keys])

        first = 2 * kj + 2
        n_off = nq - first

        def pair(i, carry):
            u = 2 * i
            products(first + u + 1, 1)
            absorb(first + u, 0, None)
            products(jnp.where(u + 2 < n_off, first + u + 2, 2 * kj + 1), 0)
            absorb(first + u + 1, 1, None)
            return carry

        products(jnp.where(n_off > 0, first, 2 * kj + 1), 0)
        lax.fori_loop(0, n_off // 2, pair, 0)
        products(2 * kj, 1, t)
        absorb(2 * kj + 1, 0, 1)
        absorb(2 * kj, 1, 0, t)

        dk_ref[0] = (dk_s[...] * scale).astype(BF16)
        dv_ref[0] = dv_s[...].astype(BF16)

        @pl.when((hd == H - 1) & (kj == nk - 1))
        def _():
            _finish_exchange(scatter())

    rowspec = pl.BlockSpec((1, nq, 1, t), lambda h, j: (h, 0, 0, 0))
    outs = pl.pallas_call(
        body, name="mla_attn_bwd", grid=(H, nk),
        in_specs=[pl.BlockSpec((1, S, DQ), lambda h, j: (h, 0, 0)),
                  pl.BlockSpec((1, tk, DQ), lambda h, j: (h, j, 0)),
                  pl.BlockSpec((1, tk, DV), lambda h, j: (h, j, 0)),
                  pl.BlockSpec((S, DV), lambda h, j: (0, h)), rowspec, rowspec] + [ANY_SPEC] * ng,
        out_specs=[pl.BlockSpec((1, S, DQ), lambda h, j: (h, 0, 0)),
                   pl.BlockSpec((1, tk, DQ), lambda h, j: (h, j, 0)),
                   pl.BlockSpec((1, tk, DV), lambda h, j: (h, j, 0))] + [ANY_SPEC] * ng,
        out_shape=[_sds((H, S, DQ), F32), _sds((H, S, DQ), BF16), _sds((H, S, DV), BF16)]
        + [_sds(g.shape, g.dtype) for g in gblks],
        scratch_shapes=[pltpu.VMEM((tk, DQ), F32), pltpu.VMEM((tk, DV), F32), pltpu.VMEM((2, tk, t), F32),
                        pltpu.VMEM((2, tk, t), F32)] + _comm_sems(ng),
        compiler_params=pltpu.CompilerParams(dimension_semantics=("arbitrary", "arbitrary"),
                                             vmem_limit_bytes=VMEM_LIMIT),
    )(q, k, v, do, lse, delta, *gblks)
    return outs[0], outs[1], outs[2], outs[3:]


def _swa_attn_bwd(q, k, v, lse, sinks, dx, y, o, gt, w_o):
    S, NQ = q.shape
    NK = k.shape[1]
    D = dx.shape[1]
    W, Dh, G = WINDOW, SWA_HEAD_DIM, SWA_GROUP
    nb = S // W
    member = (jnp.arange(NQ)[None, :] // Dh == jnp.arange(SWA_HEADS)[:, None]).astype(BF16)

    def body(q_ref, kp_ref, kc_ref, vp_ref, vc_ref, lse_ref, sink_ref, dx_ref, y_ref, o_ref, gt_ref, wo_ref, mem_ref,
             dq_ref, dk_ref, dv_ref, dsink_ref, dy_ref, dgt_ref, dbo_ref, dkc_s, dvc_s):
        n = pl.program_id(0)

        @pl.when(n == 0)
        def _():
            dkc_s[...] = jnp.zeros_like(dkc_s)
            dvc_s[...] = jnp.zeros_like(dvc_s)
            dsink_ref[...] = jnp.zeros_like(dsink_ref)

        @pl.when(n < nb)
        def _():
            dxv = dx_ref[...]
            _acc(dgt_ref, jnp.sum(dxv * y_ref[...].astype(F32), axis=0, keepdims=True), n)
            dy = dxv * gt_ref[...]
            _acc(dbo_ref, jnp.sum(dy, axis=0, keepdims=True), n)
            dyb = dy.astype(BF16)
            dy_ref[...] = dyb
            do = _dot_nt(dyb, wo_ref[...])
            dob = do.astype(BF16)
            prod = do * o_ref[...].astype(F32)
            hi = prod.astype(BF16)
            lo = (prod - hi.astype(F32)).astype(BF16)
            dl = _dot_nt(mem_ref[...], hi) + _dot_nt(mem_ref[...], lo)
            distf, valid = _swa_geometry(n)
            kband = jnp.concatenate([kp_ref[...], kc_ref[...]], axis=0)
            vband = jnp.concatenate([vp_ref[...], vc_ref[...]], axis=0)
            kband_t = kband.astype(F32).T.astype(BF16)
            dq_t = []
            for kh in range(SWA_KV_HEADS):
                ck = slice(kh * Dh, (kh + 1) * Dh)
                kb, vb, kbt = kband[:, ck], vband[:, ck], kband_t[ck, :]
                heads, q4, slopes, sinks = _swa_group(kh, q_ref, sink_ref)
                do4 = jnp.concatenate([dob[:, h * Dh:(h + 1) * Dh] for h in heads], axis=0)
                lse4 = jnp.concatenate([lse_ref[h:h + 1, :] for h in heads], axis=1)
                dl4 = jnp.concatenate([dl[h:h + 1, :] for h in heads], axis=1)
                s = _dot_nt(kb, q4) * (Dh ** -0.5) - slopes * distf
                p = jnp.where(valid, jnp.exp(s - lse4), 0.0)
                dvb = _dot(p.astype(BF16), do4)
                dp = _dot_nt(vb, do4)
                dsb = ((p * (dp - dl4)) * (Dh ** -0.5)).astype(BF16)
                dq4 = _dot(kbt, dsb)
                dkb = _dot(dsb, q4)
                dsk4 = jnp.exp(sinks - lse4) * dl4
                for g, h in enumerate(heads):
                    dq_t.append(dq4[:, g * W:(g + 1) * W])
                    dsink_ref[:, h:h + 1] += -jnp.sum(dsk4[:, g * W:(g + 1) * W], axis=1, keepdims=True)
                dk_ref[:, ck] = (dkc_s[:, ck] + dkb[:W]).astype(BF16)
                dv_ref[:, ck] = (dvc_s[:, ck] + dvb[:W]).astype(BF16)
                dkc_s[:, ck] = dkb[W:]
                dvc_s[:, ck] = dvb[W:]
            dq_ref[...] = jnp.concatenate(dq_t, axis=0).T.astype(BF16)

        @pl.when(n == nb)
        def _():
            dk_ref[...] = dkc_s[...].astype(BF16)
            dv_ref[...] = dvc_s[...].astype(BF16)

    kprev, kcur = _swa_band_specs(W, nb, NK)
    qspec = lambda cols: pl.BlockSpec((W, cols), lambda n: (jnp.minimum(n, nb - 1), 0))
    kvout = pl.BlockSpec((W, NK), lambda n: (jnp.maximum(n - 1, 0), 0))
    rowspec = pl.BlockSpec((SWA_HEADS, W), lambda n: (0, jnp.minimum(n, nb - 1)))
    fixed = lambda shape: pl.BlockSpec(shape, lambda n: (0, 0))
    return pl.pallas_call(
        body, name="swa_attn_bwd", grid=(nb + 1,),
        in_specs=[qspec(NQ), kprev, kcur, kprev, kcur, rowspec, pl.BlockSpec((1, SWA_HEADS), lambda n: (0, 0)),
                  qspec(D), qspec(D), qspec(NQ), fixed((1, D)), fixed(w_o.shape), fixed(member.shape)],
        out_specs=[qspec(NQ), kvout, kvout, fixed((1, 128)), qspec(D), fixed((1, D)), fixed((1, D))],
        out_shape=[_sds((S, NQ), BF16), _sds((S, NK), BF16), _sds((S, NK), BF16), _sds((1, 128), F32),
                   _sds((S, D), BF16), _sds((1, D), F32), _sds((1, D), F32)],
        scratch_shapes=[pltpu.VMEM((W, NK), F32), pltpu.VMEM((W, NK), F32)],
        compiler_params=pltpu.CompilerParams(dimension_semantics=("arbitrary",), vmem_limit_bytes=VMEM_LIMIT),
    )(q, k, k, v, v, lse, sinks, dx, y, o, gt, w_o, member)


def _swa_in_bwd(dq, dk, dv, x, dx, w_qkv, g, sc):
    S, D = x.shape
    N = w_qkv.shape[1]

    def body(dq_ref, dk_ref, dv_ref, x_ref, dx_ref, w_ref, g_ref, sc_ref, dqkv_ref, dxo_ref, db_ref, dsh_ref, da_ref):
        i = pl.program_id(0)
        dqkv = jnp.concatenate([dq_ref[...], dk_ref[...], dv_ref[...]], axis=1)
        dqkv_ref[...] = dqkv
        _acc(db_ref, jnp.sum(dqkv.astype(F32), axis=0, keepdims=True), i)
        dh = _dot_nt(dqkv, w_ref[...])
        dxn, dsh, da = _modulate_bwd(dh, x_ref[...], g_ref[...], sc_ref[...])
        dxo_ref[...] = dx_ref[...] + dxn
        _acc(dsh_ref, dsh, i)
        _acc(da_ref, da, i)

    return _rowcall("swa_in_bwd", body, S, ROW_TILE_IO, [dq, dk, dv, x, dx], [w_qkv, g, sc],
                    [_sds((S, N), BF16), _sds((S, D), F32)],
                    [_sds((1, N), F32), _sds((1, D), F32), _sds((1, D), F32)])


def _mla_in_bwd(dq, dk, dv, cos, sin, cqp, ckvp, x, dx, w_uqx, g_q, w_ukv, g_kv, w_cat, g, sc):
    S, D = x.shape
    H = MLA_HEADS
    QL = g_q.shape[1]
    NX = w_uqx.shape[1]
    NC = w_cat.shape[1]

    def body(dq_ref, dk_ref, dv_ref, cos_ref, sin_ref, cqp_ref, ckvp_ref, x_ref, dx_ref,
             wuqx_ref, gq_ref, wukv_ref, gkv_ref, wcat_ref, g_ref, sc_ref,
             dqx_ref, dkv_ref, dcat_ref, dxo_ref, dgq_ref, dgkv_ref, dsh_ref, da_ref):
        i = pl.program_id(0)
        cs, sn = cos_ref[...], sin_ref[...]
        dkr = jnp.zeros(cs.shape, F32)
        for hd in range(H):
            b = hd * 256
            dqh = dq_ref[hd] * (QK_DIM ** -0.5)
            dqx_ref[:, b:b + QK_NOPE] = dqh[:, :QK_NOPE].astype(BF16)
            dqx_ref[:, b + 128:b + 192] = (dqh[:, QK_NOPE:] * cs).astype(BF16)
            dqx_ref[:, b + 192:b + 256] = (dqh[:, QK_NOPE:] * sn).astype(BF16)
            dkh = dk_ref[hd]
            dkv_ref[:, b:b + QK_NOPE] = dkh[:, :QK_NOPE]
            dkv_ref[:, b + 128:b + 256] = dv_ref[hd]
            dkr = dkr + dkh[:, QK_NOPE:].astype(F32)
        dcq = _dot_nt(dqx_ref[...], wuqx_ref[...])
        cqp = cqp_ref[...]
        rq = _rstd(cqp)
        nq = cqp * rq
        _acc(dgq_ref, jnp.sum(dcq * nq, axis=0, keepdims=True), i)
        dcqp = _rms_bwd(dcq * gq_ref[...], nq, rq)
        dckv = _dot_nt(dkv_ref[...], wukv_ref[...])
        ckvp = ckvp_ref[...]
        rk = _rstd(ckvp)
        nk = ckvp * rk
        _acc(dgkv_ref, jnp.sum(dckv * nk, axis=0, keepdims=True), i)
        dckvp = _rms_bwd(dckv * gkv_ref[...], nk, rk)
        dcat_ref[:, :QL] = dcqp.astype(BF16)
        dcat_ref[:, QL:QL + KV_LORA] = dckvp.astype(BF16)
        o = QL + KV_LORA
        dcat_ref[:, o:o + QK_ROPE] = (dkr * cs).astype(BF16)
        dcat_ref[:, o + QK_ROPE:o + 2 * QK_ROPE] = (dkr * sn).astype(BF16)
        dh = _dot_nt(dcat_ref[...], wcat_ref[...])
        dxn, dsh, da = _modulate_bwd(dh, x_ref[...], g_ref[...], sc_ref[...])
        dxo_ref[...] = dx_ref[...] + dxn
        _acc(dsh_ref, dsh, i)
        _acc(da_ref, da, i)

    return _rowcall("mla_in_bwd", body, S, ROW_TILE, [dq, dk, dv, cos, sin, cqp, ckvp, x, dx],
                    [w_uqx, g_q, w_ukv, g_kv, w_cat, g, sc],
                    [_sds((S, NX), BF16), _sds((S, NX), BF16), _sds((S, NC), BF16), _sds((S, D), F32)],
                    [_sds((1, QL), F32), _sds((1, KV_LORA), F32), _sds((1, D), F32), _sds((1, D), F32)])


def _matmul_tn(name, a, b, out_dtype=F32, column_blocks=False):
    S, K = a.shape
    N = b.shape[1]
    tk, tn, ts = min(K, 1024), min(N, 1024), min(S, TN_TOKENS)
    if column_blocks:
        tn = N // N_DEV
    if N % tn:
        tn = 512 if N % 512 == 0 else (384 if N % 384 == 0 else 128)
    if K % tk:
        tk = 512 if K % 512 == 0 else (384 if K % 384 == 0 else 128)
    ns = S // ts

    def body(a_ref, b_ref, o_ref, *scratch):
        acc_ref = scratch[0] if scratch else o_ref
        _acc(acc_ref, _dot_tn(a_ref[...], b_ref[...]), pl.program_id(2))
        if scratch:
            @pl.when(pl.program_id(2) == ns - 1)
            def _():
                o_ref[...] = acc_ref[...].astype(out_dtype)

    if column_blocks:
        out_spec = pl.BlockSpec((None, tk, tn), lambda i, j, s: (j, i, 0))
        out_shape = _sds((N_DEV, K, tn), out_dtype)
    else:
        out_spec = pl.BlockSpec((tk, tn), lambda i, j, s: (i, j))
        out_shape = _sds((K, N), out_dtype)
    return pl.pallas_call(
        body, name=name, grid=(K // tk, N // tn, ns),
        in_specs=[pl.BlockSpec((ts, tk), lambda i, j, s: (s, i)), pl.BlockSpec((ts, tn), lambda i, j, s: (s, j))],
        out_specs=out_spec, out_shape=out_shape,
        scratch_shapes=[] if out_dtype == F32 else [pltpu.VMEM((tk, tn), F32)],
        compiler_params=pltpu.CompilerParams(dimension_semantics=("parallel", "parallel", "arbitrary"),
                                             vmem_limit_bytes=VMEM_LIMIT),
    )(a, b)


def _silu(c):
    return c * jax.nn.sigmoid(c)


def _ada_fwd(c_all, w_ada):
    L, D, NC = w_ada.shape

    def body(c_ref, w_ref, o_ref):
        cond = _silu(c_ref[...]).astype(BF16)
        o_ref[0] = _dot(cond, w_ref[0].astype(BF16))

    return pl.pallas_call(
        body, name="ada_fwd", grid=(L,),
        in_specs=[pl.BlockSpec(c_all.shape, lambda l: (0, 0)), pl.BlockSpec((1, D, NC), lambda l: (l, 0, 0))],
        out_specs=pl.BlockSpec((1, N_DEV, NC), lambda l: (l, 0, 0)),
        out_shape=_sds((L, N_DEV, NC), F32),
        compiler_params=pltpu.CompilerParams(dimension_semantics=("arbitrary",), vmem_limit_bytes=VMEM_LIMIT),
    )(c_all, w_ada)


def _adamw(w, g, m, v):
    m = ADAM_B1 * m + (1.0 - ADAM_B1) * g
    v = ADAM_B2 * v + (1.0 - ADAM_B2) * (g * g)
    m_hat = m / (1.0 - ADAM_B1 ** ADAM_STEP)
    v_hat = v / (1.0 - ADAM_B2 ** ADAM_STEP)
    delta = -ADAM_LR * (m_hat / (jnp.sqrt(v_hat) + ADAM_EPS) + ADAM_WD * w)
    return delta, m, v


def _ada_bwd_adamw(c_all_t, dmod_cols, w, m, v):
    L, D, NC = w.shape
    tr = min(D, 256)

    def body(ct_ref, dm_ref, w_ref, m_ref, v_ref, g_ref, d_ref, mo_ref, vo_ref):
        cond_t = _silu(ct_ref[...])
        dm = dm_ref[0]
        g = cond_t[:, 0:1] * dm[0:1, :]
        for b in range(1, N_DEV):
            g = g + cond_t[:, b:b + 1] * dm[b:b + 1, :]
        g_ref[0] = g
        d_ref[0], mo_ref[0], vo_ref[0] = _adamw(w_ref[0], g, m_ref[0], v_ref[0])

    wspec = pl.BlockSpec((1, tr, NC), lambda l, r: (l, r, 0))
    return pl.pallas_call(
        body, name="ada_bwd_adamw", grid=(L, D // tr),
        in_specs=[pl.BlockSpec((tr, N_DEV), lambda l, r: (r, 0)),
                  pl.BlockSpec((1, N_DEV, NC), lambda l, r: (l, 0, 0)), wspec, wspec, wspec],
        out_specs=[wspec] * 4, out_shape=[_sds(w.shape, F32)] * 4,
        compiler_params=pltpu.CompilerParams(dimension_semantics=("parallel", "parallel"), vmem_limit_bytes=VMEM_LIMIT),
    )(c_all_t, dmod_cols, w, m, v)


def _sum_devices(x):
    def body(x_ref, o_ref):
        s = x_ref[0]
        for j in range(1, N_DEV):
            s = s + x_ref[j]
        o_ref[...] = s

    return pl.pallas_call(body, name="sum_devices", out_shape=_sds(x.shape[1:], F32))(x)


def _adamw_small(w, g, m, v):
    def body(w_ref, g_ref, m_ref, v_ref, d_ref, mo_ref, vo_ref):
        d_ref[...], mo_ref[...], vo_ref[...] = _adamw(w_ref[...], g_ref[...], m_ref[...], v_ref[...])

    return pl.pallas_call(body, name="adamw_small", out_shape=[_sds(w.shape, F32)] * 3)(w, g, m, v)


def _me():
    return lax.axis_index("x") * 4 + lax.axis_index("y") * 2 + lax.axis_index("c")


def _peer(k):
    x, y, c = lax.axis_index("x"), lax.axis_index("y"), lax.axis_index("c")
    px = 1 - x if k & 4 else x
    py = 1 - y if k & 2 else y
    pc = 1 - c if k & 1 else c
    return (px, py, pc), px * 4 + py * 2 + pc


VMEM_SPEC = pl.BlockSpec(memory_space=pltpu.VMEM)
ANY_SPEC = pl.BlockSpec(memory_space=pl.ANY)
def _comm_sems(n):
    return [pltpu.SemaphoreType.DMA((n * (N_DEV - 1),)), pltpu.SemaphoreType.DMA((n * (N_DEV - 1),)),
            pltpu.SemaphoreType.DMA((n,))]


def _exchange_copies(srcs_of, dst_refs, send_sems, recv_sems, local_sems):
    me = _me()
    local, sends, recvs = [], [], []
    for a, (src_of, dst_ref) in enumerate(zip(srcs_of, dst_refs)):
        local.append(pltpu.make_async_copy(src_of(me), dst_ref.at[me], local_sems.at[a]))
        for k in range(1, N_DEV):
            dev, pj = _peer(k)
            i = a * (N_DEV - 1) + k - 1
            sems = dict(send_sem=send_sems.at[i], recv_sem=recv_sems.at[i], device_id=dev, device_id_type=MESH_IDS)
            sends.append(pltpu.make_async_remote_copy(src_ref=src_of(pj), dst_ref=dst_ref.at[me], **sems))
            recvs.append(pltpu.make_async_remote_copy(src_ref=src_of(pj), dst_ref=dst_ref.at[pj], **sems))
    return local, sends, recvs


def _start_exchange(copies):
    local, sends, _ = copies
    for cp in local + sends:
        cp.start()


def _finish_exchange(copies):
    local, sends, recvs = copies
    for cp in recvs:
        cp.wait_recv()
    for cp in sends:
        cp.wait_send()
    for cp in local:
        cp.wait()


def _sum_adamw(recv, w, m, v):
    shape = w.shape
    C = shape[-1]
    R = w.size // C
    rows = max(d for d in range(16, min(R, 512) + 1, 16) if R % d == 0 and d * C <= 256 * 1024)

    def body(r_ref, w_ref, m_ref, v_ref, go_ref, d_ref, mo_ref, vo_ref):
        g = r_ref[0].astype(F32)
        for j in range(1, N_DEV):
            g = g + r_ref[j].astype(F32)
        go_ref[...] = g
        d_ref[...], mo_ref[...], vo_ref[...] = _adamw(w_ref[...], g, m_ref[...], v_ref[...])

    spec = pl.BlockSpec((rows, C), lambda i: (i, 0))
    outs = pl.pallas_call(
        body, name="sum_adamw", grid=(R // rows,),
        in_specs=[pl.BlockSpec((N_DEV, rows, C), lambda i: (0, i, 0)), spec, spec, spec],
        out_specs=[spec] * 4, out_shape=[_sds((R, C), F32)] * 4,
        compiler_params=pltpu.CompilerParams(dimension_semantics=("parallel",), vmem_limit_bytes=VMEM_LIMIT),
    )(recv.reshape(N_DEV, R, C), w.reshape(R, C), m.reshape(R, C), v.reshape(R, C))
    return [o.reshape(shape) for o in outs]


def _all_gather(name, x, out_dtype):
    R, C = x.shape
    cast = out_dtype != x.dtype

    def body(x_ref, out_ref, buf, send_sems, recv_sems, local_sem):
        me = _me()
        if cast:
            buf[...] = x_ref[...].astype(out_dtype)
            src = buf
        else:
            src = x_ref
        local = pltpu.make_async_copy(src, out_ref.at[me], local_sem)
        local.start()
        sends = []
        for k in range(1, N_DEV):
            dev, _ = _peer(k)
            cp = pltpu.make_async_remote_copy(src_ref=src, dst_ref=out_ref.at[me], send_sem=send_sems.at[k - 1],
                                              recv_sem=recv_sems.at[k - 1], device_id=dev, device_id_type=MESH_IDS)
            cp.start()
            sends.append(cp)
        for k in range(1, N_DEV):
            dev, pj = _peer(k)
            pltpu.make_async_remote_copy(src_ref=src, dst_ref=out_ref.at[pj], send_sem=send_sems.at[k - 1],
                                         recv_sem=recv_sems.at[k - 1], device_id=dev, device_id_type=MESH_IDS).wait_recv()
        for cp in sends:
            cp.wait_send()
        local.wait()

    return pl.pallas_call(
        body, name=name, in_specs=[VMEM_SPEC], out_specs=ANY_SPEC, out_shape=_sds((N_DEV, R, C), out_dtype),
        scratch_shapes=[pltpu.VMEM((R, C) if cast else (8, 128), out_dtype),
                        pltpu.SemaphoreType.DMA((N_DEV - 1,)), pltpu.SemaphoreType.DMA((N_DEV - 1,)),
                        pltpu.SemaphoreType.DMA(())],
        compiler_params=pltpu.CompilerParams(vmem_limit_bytes=VMEM_LIMIT),
    )(x)


def _all_gather_two_level(name, x, out_dtype):
    xs, dtypes = list(x), list(out_dtype)
    n = len(xs)

    def body(*refs):
        x_refs, out_refs, bufs = refs[:n], refs[n:2 * n], refs[2 * n:3 * n]
        send_sems, recv_sems, local_sems = refs[3 * n:]
        x_, y_, c_ = lax.axis_index("x"), lax.axis_index("y"), lax.axis_index("c")
        me, sibling = (x_, y_, c_), (x_, y_, 1 - c_)
        chips = [(1 - x_, y_), (x_, 1 - y_), (1 - x_, 1 - y_)]
        first, passed, mine = [], [], []
        for a in range(n):
            bufs[a][...] = x_refs[a][...].astype(dtypes[a])

        def copy(a, k, block, to, src=None):
            slot = out_refs[a].at[4 * block[0] + 2 * block[1] + block[2]]
            i = a * (N_DEV - 1) + k
            return pltpu.make_async_remote_copy(src_ref=slot if src is None else src, dst_ref=slot,
                                                send_sem=send_sems.at[i], recv_sem=recv_sems.at[i], device_id=to,
                                                device_id_type=MESH_IDS)

        for a in range(n):
            mine.append(pltpu.make_async_copy(bufs[a], out_refs[a].at[4 * x_ + 2 * y_ + c_], local_sems.at[a]))
            first.append(copy(a, 0, me, sibling, src=bufs[a]))
            first += [copy(a, 1 + j, me, (*chip, c_), src=bufs[a]) for j, chip in enumerate(chips)]
        for cp in mine + first:
            cp.start()
        for a in range(n):
            for j, chip in enumerate(chips):
                copy(a, 1 + j, (*chip, c_), me).wait_recv()
                passed.append(copy(a, 4 + j, (*chip, c_), sibling))
                passed[-1].start()
        for a in range(n):
            copy(a, 0, sibling, me).wait_recv()
            for j, chip in enumerate(chips):
                copy(a, 4 + j, (*chip, 1 - c_), me).wait_recv()
        for cp in first + passed:
            cp.wait_send()
        for cp in mine:
            cp.wait()

    return pl.pallas_call(
        body, name=name, in_specs=[VMEM_SPEC] * n, out_specs=[ANY_SPEC] * n,
        out_shape=[_sds((N_DEV,) + a.shape, d) for a, d in zip(xs, dtypes)],
        scratch_shapes=[pltpu.VMEM(a.shape, d) for a, d in zip(xs, dtypes)] + _comm_sems(n),
        compiler_params=pltpu.CompilerParams(vmem_limit_bytes=VMEM_LIMIT),
    )(*xs)


def _all_to_all(name, x):
    _, R, C = x.shape

    def body(x_ref, out_ref, send_sems, recv_sems, local_sem):
        me = _me()
        local = pltpu.make_async_copy(x_ref.at[me], out_ref.at[me], local_sem)
        local.start()
        sends = []
        for k in range(1, N_DEV):
            dev, pj = _peer(k)
            cp = pltpu.make_async_remote_copy(src_ref=x_ref.at[pj], dst_ref=out_ref.at[me], send_sem=send_sems.at[k - 1],
                                              recv_sem=recv_sems.at[k - 1], device_id=dev, device_id_type=MESH_IDS)
            cp.start()
            sends.append(cp)
        for k in range(1, N_DEV):
            dev, pj = _peer(k)
            pltpu.make_async_remote_copy(src_ref=x_ref.at[pj], dst_ref=out_ref.at[pj], send_sem=send_sems.at[k - 1],
                                         recv_sem=recv_sems.at[k - 1], device_id=dev, device_id_type=MESH_IDS).wait_recv()
        for cp in sends:
            cp.wait_send()
        local.wait()

    return pl.pallas_call(
        body, name=name, in_specs=[VMEM_SPEC], out_specs=VMEM_SPEC, out_shape=_sds(x.shape, x.dtype),
        scratch_shapes=[pltpu.SemaphoreType.DMA((N_DEV - 1,)), pltpu.SemaphoreType.DMA((N_DEV - 1,)),
                        pltpu.SemaphoreType.DMA(())],
    )(x)


def _reduce_scatter_adamw(name, gblk, w, m, v):
    _, R, C = gblk.shape
    rows = 8
    for cand in (136, 128, 80, 64, 40, 32, 16, 8):
        if R % cand == 0:
            rows = cand
            break

    def body(g_ref, w_ref, m_ref, v_ref, go_ref, d_ref, mo_ref, vo_ref, recv, send_sems, recv_sems, local_sem):
        me = _me()
        local = pltpu.make_async_copy(g_ref.at[me], recv.at[me], local_sem)
        local.start()
        sends = []
        for k in range(1, N_DEV):
            dev, pj = _peer(k)
            cp = pltpu.make_async_remote_copy(src_ref=g_ref.at[pj], dst_ref=recv.at[me], send_sem=send_sems.at[k - 1],
                                              recv_sem=recv_sems.at[k - 1], device_id=dev, device_id_type=MESH_IDS)
            cp.start()
            sends.append(cp)
        for k in range(1, N_DEV):
            dev, pj = _peer(k)
            pltpu.make_async_remote_copy(src_ref=g_ref.at[pj], dst_ref=recv.at[pj], send_sem=send_sems.at[k - 1],
                                         recv_sem=recv_sems.at[k - 1], device_id=dev, device_id_type=MESH_IDS).wait_recv()
        local.wait()

        def chunk(i, carry):
            r = pl.ds(pl.multiple_of(i * rows, rows), rows)
            g = recv[0, r, :].astype(F32)
            for j in range(1, N_DEV):
                g = g + recv[j, r, :].astype(F32)
            go_ref[r, :] = g
            d_ref[r, :], mo_ref[r, :], vo_ref[r, :] = _adamw(w_ref[r, :], g, m_ref[r, :], v_ref[r, :])
            return carry

        lax.fori_loop(0, R // rows, chunk, 0)
        for cp in sends:
            cp.wait_send()

    return pl.pallas_call(
        body, name=name, in_specs=[ANY_SPEC, VMEM_SPEC, VMEM_SPEC, VMEM_SPEC], out_specs=[VMEM_SPEC] * 4,
        out_shape=[_sds((R, C), F32)] * 4,
        scratch_shapes=[pltpu.VMEM((N_DEV, R, C), BF16), pltpu.SemaphoreType.DMA((N_DEV - 1,)),
                        pltpu.SemaphoreType.DMA((N_DEV - 1,)), pltpu.SemaphoreType.DMA(())],
        compiler_params=pltpu.CompilerParams(vmem_limit_bytes=VMEM_LIMIT),
    )(gblk, w, m, v)


def _reduce_scatter_adamw_two_level(name, gblk, w, m, v):
    _, R, C = gblk.shape
    rows = max(d for d in range(16, min(R, 128) + 1, 16) if R % d == 0)

    def body(g_ref, w_ref, m_ref, v_ref, go_ref, d_ref, mo_ref, vo_ref, from_sib, to_chips, from_chips, own,
             send_sems, recv_sems):
        x_, y_, c_ = lax.axis_index("x"), lax.axis_index("y"), lax.axis_index("c")
        sibling = (x_, y_, 1 - c_)
        chips = [(x_, y_), (1 - x_, y_), (x_, 1 - y_), (1 - x_, 1 - y_)]

        def idx(chip, core):
            return 4 * chip[0] + 2 * chip[1] + core

        def remote(k, src, dst, to):
            return pltpu.make_async_remote_copy(src_ref=src, dst_ref=dst, send_sem=send_sems.at[k],
                                                recv_sem=recv_sems.at[k], device_id=to, device_id_type=MESH_IDS)

        step1 = [remote(q, g_ref.at[idx(chip, 1 - c_)], from_sib.at[q], sibling) for q, chip in enumerate(chips)]
        for cp in step1:
            cp.start()
        for cp in step1:
            cp.wait_recv()

        def chip_sums(i, carry):
            r = pl.ds(pl.multiple_of(i * rows, rows), rows)
            for q, chip in enumerate(chips):
                part = g_ref[idx(chip, c_), r, :].astype(F32) + from_sib[q, r, :].astype(F32)
                if q == 0:
                    own[r, :] = part
                else:
                    to_chips[q - 1, r, :] = part.astype(BF16)
            return carry

        lax.fori_loop(0, R // rows, chip_sums, 0)
        step2 = [remote(4 + j, to_chips.at[j], from_chips.at[j], (*chip, c_)) for j, chip in enumerate(chips[1:])]
        for cp in step2:
            cp.start()
        for cp in step2:
            cp.wait_recv()

        def finish(i, carry):
            r = pl.ds(pl.multiple_of(i * rows, rows), rows)
            g = own[r, :]
            for j in range(3):
                g = g + from_chips[j, r, :].astype(F32)
            go_ref[r, :] = g
            d_ref[r, :], mo_ref[r, :], vo_ref[r, :] = _adamw(w_ref[r, :], g, m_ref[r, :], v_ref[r, :])
            return carry

        lax.fori_loop(0, R // rows, finish, 0)
        for cp in step1 + step2:
            cp.wait_send()

    return pl.pallas_call(
        body, name=name, in_specs=[VMEM_SPEC] * 4, out_specs=[VMEM_SPEC] * 4, out_shape=[_sds((R, C), F32)] * 4,
        scratch_shapes=[pltpu.VMEM((4, R, C), BF16), pltpu.VMEM((3, R, C), BF16), pltpu.VMEM((3, R, C), BF16),
                        pltpu.VMEM((R, C), F32), pltpu.SemaphoreType.DMA((N_DEV - 1,)),
                        pltpu.SemaphoreType.DMA((N_DEV - 1,))],
        compiler_params=pltpu.CompilerParams(vmem_limit_bytes=VMEM_LIMIT),
    )(gblk, w, m, v)


FIRST_WEIGHTS = ["mla_w_dq", "mla_w_uq", "mla_w_dkv", "mla_w_ukv"]
LATE_WEIGHTS = ["mla_w_o", "swa_w_qkv", "swa_w_o", "w_ff1", "w_ff2"]
ROW_SHARDED = {"mla_w_dq", "mla_w_dkv", "mla_w_o", "swa_w_o", "w_ff2"}


def _unblock(name, blocks):
    sh = blocks.shape[1:]
    if name in ROW_SHARDED:
        return jnp.moveaxis(blocks, 0, 1).reshape(sh[0], N_DEV * sh[1], sh[2])
    return jnp.moveaxis(blocks, 0, 2).reshape(sh[0], sh[1], N_DEV * sh[2])


def _block(name, full):
    L, K, N = full.shape
    if name in ROW_SHARDED:
        return jnp.moveaxis(full.reshape(L, N_DEV, K // N_DEV, N), 1, 0)
    return jnp.moveaxis(full.reshape(L, K, N_DEV, N // N_DEV), 2, 0)


def _rot_cols(w):
    half = QK_ROPE // 2
    return jnp.concatenate([-w[..., half:], w[..., :half]], axis=-1)


def _unrot_cols(gw):
    half = QK_ROPE // 2
    return jnp.concatenate([gw[..., half:], -gw[..., :half]], axis=-1)


def _row(v):
    return v.reshape(1, -1)


def _mlp_block_bwd(dx, sv, w1, w2, g, sc, gt):
    dy, du, dgt = _mlp_bwd_a(dx, sv["y2"], sv["rl"], gt, w2)
    dw2 = _matmul_tn("dw_ff2", sv["act"], dy, BF16)
    dw1 = _matmul_tn("dw_ff1", sv["h2"], du, BF16, column_blocks=True)
    dxo, dsh, da = _mlp_bwd_b(du, sv["x1"], dx, w1, g, sc)
    return dxo, dw1, dw2, dsh, da, dgt


def kernel(x, c, positions, w_ada, b_ada, g_mix, g_mlp, mla_w_dq, mla_g_q, mla_w_uq, mla_w_dkv, mla_g_kv, mla_w_ukv, mla_w_o, swa_w_qkv, swa_b_qkv, swa_sinks, swa_w_o, swa_b_o, w_ff1, w_ff2, g_final, loss_target, m_w_ada, m_b_ada, m_g_mix, m_g_mlp, m_mla_w_dq, m_mla_g_q, m_mla_w_uq, m_mla_w_dkv, m_mla_g_kv, m_mla_w_ukv, m_mla_w_o, m_swa_w_qkv, m_swa_b_qkv, m_swa_sinks, m_swa_w_o, m_swa_b_o, m_w_ff1, m_w_ff2, m_g_final, v_w_ada, v_b_ada, v_g_mix, v_g_mlp, v_mla_w_dq, v_mla_g_q, v_mla_w_uq, v_mla_w_dkv, v_mla_g_kv, v_mla_w_ukv, v_mla_w_o, v_swa_w_qkv, v_swa_b_qkv, v_swa_sinks, v_swa_w_o, v_swa_b_o, v_w_ff1, v_w_ff2, v_g_final):
    S, D = x.shape[1], x.shape[2]
    me = _me()
    x0 = x[0]
    target = loss_target[0]
    big_w = dict(mla_w_dq=mla_w_dq, mla_w_uq=mla_w_uq, mla_w_dkv=mla_w_dkv, mla_w_ukv=mla_w_ukv, mla_w_o=mla_w_o,
                 swa_w_qkv=swa_w_qkv, swa_w_o=swa_w_o, w_ff1=w_ff1, w_ff2=w_ff2)
    big_m = dict(mla_w_dq=m_mla_w_dq, mla_w_uq=m_mla_w_uq, mla_w_dkv=m_mla_w_dkv, mla_w_ukv=m_mla_w_ukv,
                 mla_w_o=m_mla_w_o, swa_w_qkv=m_swa_w_qkv, swa_w_o=m_swa_w_o, w_ff1=m_w_ff1, w_ff2=m_w_ff2)
    big_v = dict(mla_w_dq=v_mla_w_dq, mla_w_uq=v_mla_w_uq, mla_w_dkv=v_mla_w_dkv, mla_w_ukv=v_mla_w_ukv,
                 mla_w_o=v_mla_w_o, swa_w_qkv=v_swa_w_qkv, swa_w_o=v_swa_w_o, w_ff1=v_w_ff1, w_ff2=v_w_ff2)
    groups = {"first": FIRST_WEIGHTS}
    wrows = {n: -(-big_w[n].size // (PACK_COLS * 16)) * 16 for n in FIRST_WEIGHTS}
    offs = {g: np.concatenate([[0], np.cumsum([wrows[n] for n in names])]).astype(int) for g, names in groups.items()}

    def as_rows(n, a, lead=()):
        flat = a.reshape(lead + (-1,))
        pad = wrows[n] * PACK_COLS - flat.shape[-1]
        if pad:
            flat = jnp.pad(flat, ((0, 0),) * len(lead) + ((0, pad),))
        return flat.reshape(lead + (wrows[n], PACK_COLS))

    def pack(g, d):
        return jnp.concatenate([as_rows(n, d[n]) for n in groups[g]], axis=0)

    def pack_blocks(g, gfull):
        return jnp.concatenate([as_rows(n, _block(n, gfull[n]).astype(BF16), (N_DEV,)) for n in groups[g]], axis=1)

    def unpack(g, packed, lead=()):
        out = {}
        for i, n in enumerate(groups[g]):
            part = packed[..., int(offs[g][i]):int(offs[g][i + 1]), :].reshape(lead + (-1,))
            out[n] = part[..., :big_w[n].size].reshape(lead + big_w[n].shape)
        return out

    nbq, nbo = swa_b_qkv.shape[1], swa_b_o.shape[1]
    cpad = -(-(D + nbq + nbo) // 1024) * 1024
    cpack = jnp.pad(jnp.concatenate([c[0], swa_b_qkv[0], swa_b_o[0]]), (0, cpad - (D + nbq + nbo))).reshape(8, cpad // 8)
    gathered, call = _all_gather_two_level("gather_weights", [pack("first", big_w), cpack], [BF16, F32])
    call = call.reshape(N_DEV, cpad)
    wfull = {n: _unblock(n, b) for n, b in unpack("first", gathered, (N_DEV,)).items()}
    w_dq, w_dkv = wfull["mla_w_dq"][0], wfull["mla_w_dkv"][0]
    w_cat = jnp.concatenate([w_dq, w_dkv, _rot_cols(w_dkv[:, KV_LORA:])], axis=1)
    QL = w_dq.shape[1]
    w_uq = wfull["mla_w_uq"][0].reshape(QL, MLA_HEADS, QK_DIM)
    w_uqx = jnp.concatenate([w_uq, _rot_cols(w_uq[..., QK_NOPE:])], axis=-1).reshape(QL, MLA_HEADS * 256)
    w_ukv = wfull["mla_w_ukv"][0]

    L = w_ada.shape[0]
    NC = w_ada.shape[2]
    c_all = call[:, :D]
    b_qkv_full = call[:, D:D + nbq].reshape(1, N_DEV * nbq)
    b_o_full = call[:, D + nbq:D + nbq + nbo].reshape(1, N_DEV * nbo)
    mod_cols = _ada_fwd(c_all, w_ada)
    mpad = -(-(L * NC) // 1024) * 1024
    mod_send = jnp.pad(jnp.moveaxis(mod_cols, 1, 0).reshape(N_DEV, L * NC), ((0, 0), (0, mpad - L * NC)))
    mod_mine = _all_to_all("exchange_mod", mod_send.reshape(N_DEV, 8, mpad // 8)).reshape(N_DEV, mpad)[:, :L * NC]
    mod = jnp.moveaxis(mod_mine.reshape(N_DEV, L, NC), 0, 1).reshape(L, N_DEV * NC) + b_ada
    mods = mod.reshape(L, 6, 1, D)

    half = QK_ROPE // 2
    inv_freq = ROPE_THETA ** (-jnp.arange(half, dtype=F32) / half)
    ang = positions[0].astype(F32)[:, None] * inv_freq
    cos = jnp.concatenate([jnp.cos(ang), jnp.cos(ang)], axis=-1)
    sin = jnp.concatenate([jnp.sin(ang), jnp.sin(ang)], axis=-1)

    T_ATT = ATT_TILE
    zero_bias = jnp.zeros((1, D), F32)

    sh1, sc1, gt1, sh2, sc2, gt2 = [mods[0, i] for i in range(6)]
    gm0, gp0 = _row(g_mix[0]), _row(g_mlp[0])
    h1, cqp, cq, ckvp, ckv, q, k, v, vt = _mla_in_fwd(x0, cos, sin, gm0, sc1, sh1, w_cat, mla_g_q, w_uqx, mla_g_kv,
                                                      w_ukv, ROW_TILE)
    o0, lse0, gathered = _mla_attn_fwd(q, k, vt, ATT_TILE_FWD, [big_w[n].astype(BF16) for n in LATE_WEIGHTS])
    wfull = {n: _unblock(n, b) for n, b in zip(LATE_WEIGHTS, gathered)}
    w_o_mla, w_qkv, w_o_swa = wfull["mla_w_o"][0], wfull["swa_w_qkv"][0], wfull["swa_w_o"][0]
    ff1, ff2 = wfull["w_ff1"], wfull["w_ff2"]
    y1, x1, h2 = _attn_out_fwd(o0, x0, w_o_mla, zero_bias, gt1, gp0, sc2, sh2)
    rl0, act0, y2, x2 = _mlp_fwd(h2, x1, ff1[0], ff2[0], gt2)
    sv0 = dict(y2=y2, rl=rl0, act=act0, h2=h2, x1=x1)

    th1, tc1, tg1, th2, tc2, tg2 = [mods[1, i] for i in range(6)]
    gm1, gp1 = _row(g_mix[1]), _row(g_mlp[1])
    h3, sq, sk, svv = _swa_in_fwd(x2, gm1, tc1, th1, w_qkv, b_qkv_full)
    o1, lse1, y3, x3, h4 = _swa_attn_fwd(sq, sk, svv, swa_sinks, x2, w_o_swa, b_o_full, tg1, gp1, tc2, th2)
    rl1, act1, y4, dx4, loss_part, dg_final = _mlp_fwd_loss(h4, x3, ff1[1], ff2[1], tg2, target, _row(g_final))
    sv1 = dict(y2=y4, rl=rl1, act=act1, h2=h4, x1=x3)

    dx3, dw1_1, dw2_1, dsh2_1, da2_1, dgt2_1 = _mlp_block_bwd(dx4, sv1, ff1[1], ff2[1], gp1, tc2, tg2)
    dsq, dsk, dsv, dsink, dy, dgt1_1, db_o = _swa_attn_bwd(sq, sk, svv, lse1, swa_sinks, dx3, y3, o1, tg1, w_o_swa)
    dw_o_swa = _matmul_tn("dw_o", o1, dy, BF16)
    dqkv, dx2, db_qkv, dsh1_1, da1_1 = _swa_in_bwd(dsq, dsk, dsv, x2, dx3, w_qkv, gm1, tc1)
    dw_qkv = _matmul_tn("dw_qkv", h3, dqkv, BF16)

    dx1, dw1_0, dw2_0, dsh2_0, da2_0, dgt2_0 = _mlp_block_bwd(dx2, sv0, ff1[0], ff2[0], gp0, sc2, gt2)
    dy, do, dl, dgt1_0, _ = _attn_out_bwd(dx1, y1, o0, gt1, w_o_mla, MLA_HEADS)
    dw_o_mla = _matmul_tn("dw_o", o0, dy, BF16)
    tb = min(T_ATT, S)
    delta = dl[:MLA_HEADS].reshape(MLA_HEADS, S // tb, 1, tb)
    glate = dict(mla_w_o=dw_o_mla[None], swa_w_qkv=dw_qkv[None], swa_w_o=dw_o_swa[None],
                 w_ff2=jnp.stack([dw2_0, dw2_1]))
    gblocks = {n: _block(n, g).astype(BF16) for n, g in glate.items()}
    gblocks["w_ff1"] = jnp.stack([dw1_0, dw1_1], axis=1)
    lse_rows = (lse0 * LOG2E).reshape(MLA_HEADS, S // tb, 1, tb)
    dq, dk, dv, recv = _mla_attn_bwd(q, k, v, do, lse_rows, delta, T_ATT, [gblocks[n] for n in LATE_WEIGHTS])
    late = {n: _sum_adamw(r, big_w[n], big_m[n], big_v[n]) for n, r in zip(LATE_WEIGHTS, recv)}
    dqx, dkv, dcat, dx0, dg_q, dg_kv, dsh1_0, da1_0 = _mla_in_bwd(
        dq, dk, dv, cos, sin, cqp, ckvp, x0, dx1, w_uqx, mla_g_q, w_ukv, mla_g_kv, w_cat, gm0, sc1)
    dw_uqx = _matmul_tn("dw_uq", cq, dqx).reshape(QL, MLA_HEADS, 256)
    dw_ukv = _matmul_tn("dw_ukv", ckv, dkv)
    dw_cat = _matmul_tn("dw_down", h1, dcat)
    dw_uq = jnp.concatenate([dw_uqx[..., :QK_NOPE], dw_uqx[..., 128:192] + _unrot_cols(dw_uqx[..., 192:256])],
                            axis=-1).reshape(QL, MLA_HEADS * QK_DIM)
    o_kr = QL + KV_LORA
    dw_dkv = jnp.concatenate([dw_cat[:, QL:o_kr],
                              dw_cat[:, o_kr:o_kr + QK_ROPE] + _unrot_cols(dw_cat[:, o_kr + QK_ROPE:])], axis=1)

    gfirst = dict(mla_w_dq=dw_cat[None, :, :QL], mla_w_uq=dw_uq[None], mla_w_dkv=dw_dkv[None], mla_w_ukv=dw_ukv[None])
    first = _reduce_scatter_adamw_two_level("grad_exchange_adamw", pack_blocks("first", gfirst), pack("first", big_w),
                                  pack("first", big_m), pack("first", big_v))
    big_g, big_d, big_nm, big_nv = ({**unpack("first", first[j]), **{n: late[n][j] for n in LATE_WEIGHTS}}
                                    for j in range(4))

    dmod = jnp.stack([
        jnp.concatenate([dsh1_0, gm0 * da1_0, dgt1_0, dsh2_0, gp0 * da2_0, dgt2_0], axis=1),
        jnp.concatenate([dsh1_1, gm1 * da1_1, dgt1_1, dsh2_1, gp1 * da2_1, dgt2_1], axis=1)]).reshape(-1)
    dg_mix = jnp.concatenate([(1.0 + sc1) * da1_0, (1.0 + tc1) * da1_1], axis=1).reshape(-1)
    dg_mlp = jnp.concatenate([(1.0 + sc2) * da2_0, (1.0 + tc2) * da2_1], axis=1).reshape(-1)
    parts = [loss_part.reshape(-1), dmod, dg_mix, dg_mlp, dg_q.reshape(-1), dg_kv.reshape(-1), dsink.reshape(-1),
             dg_final.reshape(-1), db_qkv.reshape(-1), db_o.reshape(-1)]
    soffs = np.concatenate([[0], np.cumsum([p.size for p in parts])])
    spad = -(-int(soffs[-1]) // 1024) * 1024
    spack = jnp.pad(jnp.concatenate(parts), (0, spad - int(soffs[-1]))).reshape(8, spad // 8)
    sall = _all_gather("gather_small_grads", spack, F32)
    ssum = _sum_devices(sall).reshape(-1)
    tot = [ssum[int(soffs[i]):int(soffs[i + 1])] for i in range(len(parts))]
    loss = tot[0][0]
    nsink = swa_sinks.shape[1]
    small_g = dict(b_ada=tot[1].reshape(b_ada.shape), g_mix=tot[2].reshape(g_mix.shape), g_mlp=tot[3].reshape(g_mlp.shape),
                   mla_g_q=tot[4].reshape(mla_g_q.shape), mla_g_kv=tot[5].reshape(mla_g_kv.shape),
                   swa_sinks=tot[6][:nsink].reshape(swa_sinks.shape), g_final=tot[7].reshape(g_final.shape),
                   swa_b_qkv=lax.dynamic_slice(tot[8], (me * nbq,), (nbq,)).reshape(swa_b_qkv.shape),
                   swa_b_o=lax.dynamic_slice(tot[9], (me * nbo,), (nbo,)).reshape(swa_b_o.shape))
    small_w = dict(b_ada=b_ada, g_mix=g_mix, g_mlp=g_mlp, mla_g_q=mla_g_q, mla_g_kv=mla_g_kv, swa_sinks=swa_sinks,
                   g_final=g_final, swa_b_qkv=swa_b_qkv, swa_b_o=swa_b_o)
    small_m = dict(b_ada=m_b_ada, g_mix=m_g_mix, g_mlp=m_g_mlp, mla_g_q=m_mla_g_q, mla_g_kv=m_mla_g_kv,
                   swa_sinks=m_swa_sinks, g_final=m_g_final, swa_b_qkv=m_swa_b_qkv, swa_b_o=m_swa_b_o)
    small_v = dict(b_ada=v_b_ada, g_mix=v_g_mix, g_mlp=v_g_mlp, mla_g_q=v_mla_g_q, mla_g_kv=v_mla_g_kv,
                   swa_sinks=v_swa_sinks, g_final=v_g_final, swa_b_qkv=v_swa_b_qkv, swa_b_o=v_swa_b_o)
    SMALL = list(small_w)
    woffs = np.concatenate([[0], np.cumsum([small_w[n].size for n in SMALL])])
    wpad = -(-int(woffs[-1]) // 1024) * 1024

    def spack_of(d):
        flat = jnp.concatenate([d[n].reshape(-1) for n in SMALL])
        return jnp.pad(flat, (0, wpad - int(woffs[-1]))).reshape(8, wpad // 8)

    sm = _adamw_small(spack_of(small_w), spack_of(small_g), spack_of(small_m), spack_of(small_v))
    small_d, small_nm, small_nv = (
        {n: a.reshape(-1)[int(woffs[i]):int(woffs[i + 1])].reshape(small_w[n].shape) for i, n in enumerate(SMALL)}
        for a in sm)

    b_off = int(soffs[1])
    dmod_all = sall.reshape(N_DEV, -1)[:, b_off:b_off + L * N_DEV * NC].reshape(N_DEV, L, N_DEV * NC)
    dmod_cols = jnp.moveaxis(lax.dynamic_slice_in_dim(dmod_all, me * NC, NC, axis=2), 0, 1)
    ada_g, ada_d, ada_nm, ada_nv = _ada_bwd_adamw(c_all.T, dmod_cols, w_ada, m_w_ada, v_w_ada)

    order = ["w_ada", "b_ada", "g_mix", "g_mlp", "mla_w_dq", "mla_g_q", "mla_w_uq", "mla_w_dkv", "mla_g_kv",
             "mla_w_ukv", "mla_w_o", "swa_w_qkv", "swa_b_qkv", "swa_sinks", "swa_w_o", "swa_b_o", "w_ff1", "w_ff2", "g_final"]

    def collect(ada, big, small):
        return [ada if n == "w_ada" else (big[n] if n in big else small[n]) for n in order]

    return (loss, dx0.reshape(x.shape), *collect(ada_g, big_g, small_g), *collect(ada_d, big_d, small_d),
            *collect(ada_nm, big_nm, small_nm), *collect(ada_nv, big_nv, small_nv))
```
